```python
import math
import jax, jax.numpy as jnp
from jax import lax
import numpy as np

D_MODEL = 1024
BATCH = 8
SEQ = 8192
DEPTH = 1

HEAD_DIM = 64
N_HEADS = (D_MODEL // 2) // HEAD_DIM
N_KV_HEADS = max(1, N_HEADS // 4)
GROUP = N_HEADS // N_KV_HEADS
ATTN_WIDTH = N_HEADS * HEAD_DIM
KV_WIDTH = N_KV_HEADS * HEAD_DIM
WINDOW = 128
BLOCK = 128
POOL_WINDOWS = (2, 4, 8, 16)
N_POOL_GROUPS = len(POOL_WINDOWS)
POOL_WIDTH = D_MODEL // 2
POOL_GROUP_DIM = POOL_WIDTH // N_POOL_GROUPS
D_FF = 4 * D_MODEL
IN_WIDTH = POOL_WIDTH + ATTN_WIDTH + 2 * KV_WIDTH + 2 * D_MODEL
RMS_EPS = 1e-5
NEG_INF = -1e30
ALIBI_SLOPES = np.array([2.0 ** (-8.0 * (h + 1) / N_HEADS) for h in range(N_HEADS)], dtype=np.float32)

kernel_name = "hybrid_pool_swa_gated_block"


def rms_norm(x, g):
    xf = x.astype(jnp.float32)
    y = xf * lax.rsqrt(jnp.mean(xf * xf, axis=-1, keepdims=True) + RMS_EPS)
    return (y * g.astype(jnp.float32)).astype(x.dtype)


def pool_mixer(u, w_grp, b_grp, scale):
    B, S, _ = u.shape
    ug = u.reshape(B, S, N_POOL_GROUPS, POOL_GROUP_DIM).astype(jnp.float32)
    c = jnp.pad(jnp.cumsum(ug, axis=1), ((0, 0), (1, 0), (0, 0), (0, 0)))
    t = jnp.arange(S)[:, None]
    w = jnp.asarray(POOL_WINDOWS, dtype=jnp.int32)[None, :]
    lo = jnp.maximum(t + 1 - w, 0)
    grp = jnp.arange(N_POOL_GROUPS)[None, :]
    window_sum = c[:, 1:] - c[:, lo, grp]
    count = jnp.minimum(t + 1, w).astype(jnp.float32)
    d = (window_sum / count[None, :, :, None] - ug).astype(u.dtype)
    y = jnp.einsum('bsgc,gcd->bsgd', d, w_grp) + b_grp
    return y.reshape(B, S, POOL_WIDTH) * scale


def swa_sink_attention(q, k, v, sinks):
    B, S, _ = q.shape
    nb = S // BLOCK
    qb = q.reshape(B, nb, BLOCK, N_KV_HEADS, GROUP, HEAD_DIM)
    kb = k.reshape(B, nb, BLOCK, N_KV_HEADS, HEAD_DIM)
    vb = v.reshape(B, nb, BLOCK, N_KV_HEADS, HEAD_DIM)
    pad = ((0, 0), (1, 0), (0, 0), (0, 0), (0, 0))
    kk = jnp.concatenate([jnp.pad(kb, pad)[:, :-1], kb], axis=2)
    vv = jnp.concatenate([jnp.pad(vb, pad)[:, :-1], vb], axis=2)
    s = jnp.einsum('bnqhgd,bnkhd->bnhgqk', qb, kk, preferred_element_type=jnp.float32)
    s = s * (1.0 / math.sqrt(HEAD_DIM))
    qi = jnp.arange(BLOCK)[:, None]
    kj = jnp.arange(2 * BLOCK)[None, :]
    dist = BLOCK + qi - kj
    kpos = jnp.arange(nb)[:, None] * BLOCK - BLOCK + kj
    valid = ((dist >= 0) & (dist < WINDOW))[None] & (kpos >= 0)[:, None, :]
    slopes = jnp.asarray(ALIBI_SLOPES).reshape(N_KV_HEADS, GROUP)
    s = s - slopes[None, None, :, :, None, None] * dist.astype(jnp.float32)[None, None, None, None]
    s = jnp.where(valid[None, :, None, None], s, NEG_INF)
    sink = sinks.astype(jnp.float32).reshape(N_KV_HEADS, GROUP)[None, None, :, :, None, None]
    m = jnp.maximum(jnp.max(s, axis=-1, keepdims=True), sink)
    p = jnp.exp(s - m)
    p = p / (jnp.sum(p, axis=-1, keepdims=True) + jnp.exp(sink - m))
    o = jnp.einsum('bnhgqk,bnkhd->bnqhgd', p.astype(v.dtype), vv)
    return o.reshape(B, S, ATTN_WIDTH)


def _fwd_setup_inputs(seed: int = 0) -> dict:
    key = jax.random.key(seed)
    ks = jax.random.split(key, 16)
    f32 = jnp.float32
    nrm = lambda k, shape, fan_in: jax.random.normal(k, shape, f32) * (fan_in ** -0.5)
    return {
        "x": jax.random.normal(ks[0], (BATCH, SEQ, D_MODEL), f32),
        "norm_mix": 1.0 + 0.1 * jax.random.normal(ks[1], (DEPTH, D_MODEL), f32),
        "w_in": nrm(ks[2], (DEPTH, D_MODEL, IN_WIDTH), D_MODEL),
        "pool_w": nrm(ks[3], (DEPTH, N_POOL_GROUPS, POOL_GROUP_DIM, POOL_GROUP_DIM), POOL_GROUP_DIM),
        "pool_b": 0.02 * jax.random.normal(ks[4], (DEPTH, N_POOL_GROUPS, POOL_GROUP_DIM), f32),
        "pool_scale": 1.0 + 0.1 * jax.random.normal(ks[5], (DEPTH, POOL_WIDTH), f32),
        "attn_sinks": 0.5 * jax.random.normal(ks[6], (DEPTH, N_HEADS), f32),
        "p_pool": nrm(ks[7], (DEPTH, POOL_WIDTH, D_MODEL), POOL_WIDTH),
        "p_attn": nrm(ks[8], (DEPTH, ATTN_WIDTH, D_MODEL), ATTN_WIDTH),
        "w_out": nrm(ks[9], (DEPTH, D_MODEL, D_MODEL), D_MODEL),
        "norm_mlp": 1.0 + 0.1 * jax.random.normal(ks[10], (DEPTH, D_MODEL), f32),
        "w_up": nrm(ks[11], (DEPTH, D_MODEL, D_FF), D_MODEL),
        "w_down": nrm(ks[12], (DEPTH, D_FF, D_MODEL), D_FF),
        "norm_final": 1.0 + 0.1 * jax.random.normal(ks[13], (D_MODEL,), f32),
    }


def _fwd_reference(x, norm_mix, w_in, pool_w, pool_b, pool_scale, attn_sinks, p_pool, p_attn,
              w_out, norm_mlp, w_up, w_down, norm_final):
    h = x
    c1 = POOL_WIDTH
    c2 = c1 + ATTN_WIDTH
    c3 = c2 + KV_WIDTH
    c4 = c3 + KV_WIDTH
    c5 = c4 + D_MODEL
    for l in range(DEPTH):
        u = rms_norm(h, norm_mix[l])
        z = u @ w_in[l]
        u_pool = z[..., :c1]
        q = z[..., c1:c2]
        k = z[..., c2:c3]
        v = z[..., c3:c4]
        gate_pool = jax.nn.sigmoid(z[..., c4:c5])
        gate_attn = jax.nn.sigmoid(z[..., c5:])
        y_pool = pool_mixer(u_pool, pool_w[l], pool_b[l], pool_scale[l]) @ p_pool[l]
        y_attn = swa_sink_attention(q, k, v, attn_sinks[l]) @ p_attn[l]
        mixed = gate_pool * y_pool + gate_attn * y_attn
        h = h + mixed @ w_out[l]
        u2 = rms_norm(h, norm_mlp[l])
        a = jax.nn.relu(u2 @ w_up[l])
        h = h + (a * a) @ w_down[l]
    return rms_norm(h, norm_final)


import jax as _jax
import jax.numpy as _jnp

TWIN_FORMAT = 'train_step'
FWD_PARAMS = ['x', 'norm_mix', 'w_in', 'pool_w', 'pool_b', 'pool_scale', 'attn_sinks', 'p_pool', 'p_attn', 'w_out', 'norm_mlp', 'w_up', 'w_down', 'norm_final']
TWIN_WEIGHTS = ['norm_mix', 'w_in', 'pool_w', 'pool_b', 'pool_scale', 'attn_sinks', 'p_pool', 'p_attn', 'w_out', 'norm_mlp', 'w_up', 'w_down', 'norm_final']
TWIN_DIFF_INPUT = 'x'
TWIN_INPUTS = ['x', 'norm_mix', 'w_in', 'pool_w', 'pool_b', 'pool_scale', 'attn_sinks', 'p_pool', 'p_attn', 'w_out', 'norm_mlp', 'w_up', 'w_down', 'norm_final', 'loss_target', 'm_norm_mix', 'm_w_in', 'm_pool_w', 'm_pool_b', 'm_pool_scale', 'm_attn_sinks', 'm_p_pool', 'm_p_attn', 'm_w_out', 'm_norm_mlp', 'm_w_up', 'm_w_down', 'm_norm_final', 'v_norm_mix', 'v_w_in', 'v_pool_w', 'v_pool_b', 'v_pool_scale', 'v_attn_sinks', 'v_p_pool', 'v_p_attn', 'v_w_out', 'v_norm_mlp', 'v_w_up', 'v_w_down', 'v_norm_final']
TWIN_OUTPUTS = ['loss', 'grad_x', 'grad_norm_mix', 'grad_w_in', 'grad_pool_w', 'grad_pool_b', 'grad_pool_scale', 'grad_attn_sinks', 'grad_p_pool', 'grad_p_attn', 'grad_w_out', 'grad_norm_mlp', 'grad_w_up', 'grad_w_down', 'grad_norm_final', 'delta_norm_mix', 'delta_w_in', 'delta_pool_w', 'delta_pool_b', 'delta_pool_scale', 'delta_attn_sinks', 'delta_p_pool', 'delta_p_attn', 'delta_w_out', 'delta_norm_mlp', 'delta_w_up', 'delta_w_down', 'delta_norm_final', 'new_m_norm_mix', 'new_m_w_in', 'new_m_pool_w', 'new_m_pool_b', 'new_m_pool_scale', 'new_m_attn_sinks', 'new_m_p_pool', 'new_m_p_attn', 'new_m_w_out', 'new_m_norm_mlp', 'new_m_w_up', 'new_m_w_down', 'new_m_norm_final', 'new_v_norm_mix', 'new_v_w_in', 'new_v_pool_w', 'new_v_pool_b', 'new_v_pool_scale', 'new_v_attn_sinks', 'new_v_p_pool', 'new_v_p_attn', 'new_v_w_out', 'new_v_norm_mlp', 'new_v_w_up', 'new_v_w_down', 'new_v_norm_final']
TWIN_LEAF_KINDS = {'loss': 'loss', 'grad_x': 'grad_x', 'grad_norm_mix': 'grad_w', 'grad_w_in': 'grad_w', 'grad_pool_w': 'grad_w', 'grad_pool_b': 'grad_w', 'grad_pool_scale': 'grad_w', 'grad_attn_sinks': 'grad_w', 'grad_p_pool': 'grad_w', 'grad_p_attn': 'grad_w', 'grad_w_out': 'grad_w', 'grad_norm_mlp': 'grad_w', 'grad_w_up': 'grad_w', 'grad_w_down': 'grad_w', 'grad_norm_final': 'grad_w', 'delta_norm_mix': 'delta_w', 'delta_w_in': 'delta_w', 'delta_pool_w': 'delta_w', 'delta_pool_b': 'delta_w', 'delta_pool_scale': 'delta_w', 'delta_attn_sinks': 'delta_w', 'delta_p_pool': 'delta_w', 'delta_p_attn': 'delta_w', 'delta_w_out': 'delta_w', 'delta_norm_mlp': 'delta_w', 'delta_w_up': 'delta_w', 'delta_w_down': 'delta_w', 'delta_norm_final': 'delta_w', 'new_m_norm_mix': 'new_m', 'new_m_w_in': 'new_m', 'new_m_pool_w': 'new_m', 'new_m_pool_b': 'new_m', 'new_m_pool_scale': 'new_m', 'new_m_attn_sinks': 'new_m', 'new_m_p_pool': 'new_m', 'new_m_p_attn': 'new_m', 'new_m_w_out': 'new_m', 'new_m_norm_mlp': 'new_m', 'new_m_w_up': 'new_m', 'new_m_w_down': 'new_m', 'new_m_norm_final': 'new_m', 'new_v_norm_mix': 'new_v', 'new_v_w_in': 'new_v', 'new_v_pool_w': 'new_v', 'new_v_pool_b': 'new_v', 'new_v_pool_scale': 'new_v', 'new_v_attn_sinks': 'new_v', 'new_v_p_pool': 'new_v', 'new_v_p_attn': 'new_v', 'new_v_w_out': 'new_v', 'new_v_norm_mlp': 'new_v', 'new_v_w_up': 'new_v', 'new_v_w_down': 'new_v', 'new_v_norm_final': 'new_v'}


def _forward(args):
    return _fwd_reference(*[args[k] for k in FWD_PARAMS])


def _output_shape():
    def fwd():
        inp = _fwd_setup_inputs(0)
        return _fwd_reference(*[inp[k] for k in FWD_PARAMS])
    out = _jax.eval_shape(fwd)
    return out.shape, out.dtype

N_MICROBATCH = 1
ADAM_LR = 0.001
ADAM_B1 = 0.9
ADAM_B2 = 0.999
ADAM_EPS = 1e-08
ADAM_WD = 0.01
ADAM_STEP = 10
PER_EXAMPLE_BATCH_AXIS = {'x': 0, 'loss_target': 0}
SHARED_INPUTS = []
_WEIGHT_DTYPES = {'norm_mix': _jnp.float32, 'w_in': _jnp.float32, 'pool_w': _jnp.float32, 'pool_b': _jnp.float32, 'pool_scale': _jnp.float32, 'attn_sinks': _jnp.float32, 'p_pool': _jnp.float32, 'p_attn': _jnp.float32, 'w_out': _jnp.float32, 'norm_mlp': _jnp.float32, 'w_up': _jnp.float32, 'w_down': _jnp.float32, 'norm_final': _jnp.float32}
MOMENT_SCALE = {'norm_mix': 1.663589e-01, 'w_in': 8.365399e-02, 'pool_w': 1.959076e-01, 'pool_b': 1.887829e+00, 'pool_scale': 1.972317e-01, 'attn_sinks': 4.318687e-02, 'p_pool': 1.426481e-01, 'p_attn': 4.090104e-02, 'w_out': 1.493679e-01, 'norm_mlp': 2.497266e-01, 'w_up': 1.238884e-01, 'w_down': 1.071268e+00, 'norm_final': 6.484627e+01}


def _to_microbatches(a, axis):
    t = _jnp.moveaxis(a, axis, 0)
    t = t.reshape((N_MICROBATCH, t.shape[0] // N_MICROBATCH) + t.shape[1:])
    return _jnp.moveaxis(t, 1, axis + 1)


def setup_inputs(seed: int = 0) -> dict:
    inp = _fwd_setup_inputs(seed)
    key = _jax.random.fold_in(_jax.random.key(seed), 7919)
    shape, _ = _output_shape()
    out = dict(inp)
    out["loss_target"] = _jax.random.normal(_jax.random.fold_in(key, 0), shape, _jnp.float32)
    for i, name in enumerate(TWIN_WEIGHTS):
        w = inp[name].astype(_jnp.float32)
        if MOMENT_SCALE is None:
            s = _jnp.sqrt(_jnp.mean(_jnp.square(w)) + 1e-30)
        else:
            s = MOMENT_SCALE[name]
        km, kv = _jax.random.split(_jax.random.fold_in(key, i + 1))
        out[name] = w
        out["m_" + name] = s * _jax.random.normal(km, w.shape, _jnp.float32)
        out["v_" + name] = (s * s) * _jax.random.uniform(kv, w.shape, _jnp.float32, 0.5, 1.5)
    if N_MICROBATCH > 1:
        for name, axis in PER_EXAMPLE_BATCH_AXIS.items():
            out[name] = _to_microbatches(out[name], axis)
    return {'x': out['x'], 'norm_mix': out['norm_mix'], 'w_in': out['w_in'], 'pool_w': out['pool_w'], 'pool_b': out['pool_b'], 'pool_scale': out['pool_scale'], 'attn_sinks': out['attn_sinks'], 'p_pool': out['p_pool'], 'p_attn': out['p_attn'], 'w_out': out['w_out'], 'norm_mlp': out['norm_mlp'], 'w_up': out['w_up'], 'w_down': out['w_down'], 'norm_final': out['norm_final'], 'loss_target': out['loss_target'], 'm_norm_mix': out['m_norm_mix'], 'm_w_in': out['m_w_in'], 'm_pool_w': out['m_pool_w'], 'm_pool_b': out['m_pool_b'], 'm_pool_scale': out['m_pool_scale'], 'm_attn_sinks': out['m_attn_sinks'], 'm_p_pool': out['m_p_pool'], 'm_p_attn': out['m_p_attn'], 'm_w_out': out['m_w_out'], 'm_norm_mlp': out['m_norm_mlp'], 'm_w_up': out['m_w_up'], 'm_w_down': out['m_w_down'], 'm_norm_final': out['m_norm_final'], 'v_norm_mix': out['v_norm_mix'], 'v_w_in': out['v_w_in'], 'v_pool_w': out['v_pool_w'], 'v_pool_b': out['v_pool_b'], 'v_pool_scale': out['v_pool_scale'], 'v_attn_sinks': out['v_attn_sinks'], 'v_p_pool': out['v_p_pool'], 'v_p_attn': out['v_p_attn'], 'v_w_out': out['v_w_out'], 'v_norm_mlp': out['v_norm_mlp'], 'v_w_up': out['v_w_up'], 'v_w_down': out['v_w_down'], 'v_norm_final': out['v_norm_final']}


def _loss(weights, diff, rest, loss_target):
    with _jax.named_scope("forward"):
        args = {**rest, TWIN_DIFF_INPUT: diff, **{k: w.astype(_WEIGHT_DTYPES[k]) for k, w in weights.items()}}
        y = _forward(args)
    with _jax.named_scope("loss_head"):
        err = _jnp.square(y.astype(_jnp.float32) - loss_target)
        return 0.5 * _jnp.sum(_jnp.mean(err, axis=-1)) if err.ndim else 0.5 * err


def _adamw(w, g, m, v):
    m = ADAM_B1 * m + (1.0 - ADAM_B1) * g
    v = ADAM_B2 * v + (1.0 - ADAM_B2) * _jnp.square(g)
    m_hat = m / (1.0 - ADAM_B1 ** ADAM_STEP)
    v_hat = v / (1.0 - ADAM_B2 ** ADAM_STEP)
    delta = -ADAM_LR * (m_hat / (_jnp.sqrt(v_hat) + ADAM_EPS) + ADAM_WD * w)
    return delta, m, v


def reference(x, norm_mix, w_in, pool_w, pool_b, pool_scale, attn_sinks, p_pool, p_attn, w_out, norm_mlp, w_up, w_down, norm_final, loss_target, m_norm_mix, m_w_in, m_pool_w, m_pool_b, m_pool_scale, m_attn_sinks, m_p_pool, m_p_attn, m_w_out, m_norm_mlp, m_w_up, m_w_down, m_norm_final, v_norm_mix, v_w_in, v_pool_w, v_pool_b, v_pool_scale, v_attn_sinks, v_p_pool, v_p_attn, v_w_out, v_norm_mlp, v_w_up, v_w_down, v_norm_final):
    given = dict(x=x, norm_mix=norm_mix, w_in=w_in, pool_w=pool_w, pool_b=pool_b, pool_scale=pool_scale, attn_sinks=attn_sinks, p_pool=p_pool, p_attn=p_attn, w_out=w_out, norm_mlp=norm_mlp, w_up=w_up, w_down=w_down, norm_final=norm_final, loss_target=loss_target, m_norm_mix=m_norm_mix, m_w_in=m_w_in, m_pool_w=m_pool_w, m_pool_b=m_pool_b, m_pool_scale=m_pool_scale, m_attn_sinks=m_attn_sinks, m_p_pool=m_p_pool, m_p_attn=m_p_attn, m_w_out=m_w_out, m_norm_mlp=m_norm_mlp, m_w_up=m_w_up, m_w_down=m_w_down, m_norm_final=m_norm_final, v_norm_mix=v_norm_mix, v_w_in=v_w_in, v_pool_w=v_pool_w, v_pool_b=v_pool_b, v_pool_scale=v_pool_scale, v_attn_sinks=v_attn_sinks, v_p_pool=v_p_pool, v_p_attn=v_p_attn, v_w_out=v_w_out, v_norm_mlp=v_norm_mlp, v_w_up=v_w_up, v_w_down=v_w_down, v_norm_final=v_norm_final)
    weights = {n: given[n] for n in TWIN_WEIGHTS}
    shared = {n: given[n] for n in SHARED_INPUTS}
    per_example = {n: given[n] for n in ['x']}
    grad_fn = _jax.value_and_grad(_loss, argnums=(0, 1))

    def one_microbatch(ex, loss_target):
        ex = dict(ex)
        diff = ex.pop(TWIN_DIFF_INPUT)
        return grad_fn(weights, diff, {**shared, **ex}, loss_target)

    if N_MICROBATCH == 1:
        loss, (grad_w, grad_x) = one_microbatch(per_example, given["loss_target"])
    else:
        def body(carry, xs):
            loss_sum, grad_sum = carry
            l_k, (gw_k, gx_k) = one_microbatch(xs[0], xs[1])
            with _jax.named_scope("update"):
                return (loss_sum + l_k, _jax.tree.map(_jnp.add, grad_sum, gw_k)), gx_k

        init = (_jnp.zeros((), _jnp.float32), _jax.tree.map(_jnp.zeros_like, weights))
        (loss, grad_w), grad_x = _jax.lax.scan(body, init, (per_example, given["loss_target"]))
    with _jax.named_scope("update"):
        delta_w, new_m, new_v = {}, {}, {}
        for n in TWIN_WEIGHTS:
            delta_w[n], new_m[n], new_v[n] = _adamw(weights[n], grad_w[n], given["m_" + n], given["v_" + n])
    return (loss, grad_x, *[grad_w[n] for n in TWIN_WEIGHTS], *[delta_w[n] for n in TWIN_WEIGHTS],
            *[new_m[n] for n in TWIN_WEIGHTS], *[new_v[n] for n in TWIN_WEIGHTS])
```

```python
import functools
import math

import numpy as np
import jax
import jax.numpy as jnp
from jax import lax
from jax.experimental import pallas as pl
from jax.experimental.pallas import tpu as pltpu

F32 = jnp.float32
BF16 = jnp.bfloat16

D_MODEL = 1024
POOL_WIDTH = 512
ATTN_WIDTH = 512
KV_WIDTH = 128
HEAD_DIM = 64
N_HEADS = 8
N_KV_HEADS = 2
GROUP = 4
BLOCK = 128
POOL_WINDOWS = (2, 4, 8, 16)
POOL_GROUP_DIM = 128
POOL_HALO = 16
D_FF = 4096
FF_CHUNK = 1024
IN_WIDTH = 3328
RMS_EPS = 1e-5
NEG_INF = -1e30
ATTN_SCALE = 1.0 / math.sqrt(HEAD_DIM)
N_DEV = 8

ADAM_LR = 0.001
ADAM_B1 = 0.9
ADAM_B2 = 0.999
ADAM_EPS = 1e-08
ADAM_WD = 0.01
ADAM_STEP = 10

LANES = 128
VMEM_LIMIT_BYTES = 56 * 1024 * 1024
MESH = pl.DeviceIdType.MESH


def _params(n_grid_axes=1):
    return pltpu.CompilerParams(
        dimension_semantics=("arbitrary",) * n_grid_axes, vmem_limit_bytes=VMEM_LIMIT_BYTES)


def _dot(a, b):
    return jnp.dot(a, b, preferred_element_type=F32)


def _dot_nt(a, b):
    return lax.dot_general(a, b, (((1,), (1,)), ((), ())), preferred_element_type=F32)


def _dot_tn(a, b):
    return lax.dot_general(a, b, (((0,), (0,)), ((), ())), preferred_element_type=F32)


def _rows(tm, n):
    return pl.BlockSpec((tm, n), lambda i: (i, 0))


def _whole(shape):
    zeros = (0,) * len(shape)
    return pl.BlockSpec(shape, lambda i: zeros)


def _rms_fwd(h, g):
    r = lax.rsqrt(jnp.mean(h * h, axis=-1, keepdims=True) + RMS_EPS)
    xh = h * r
    return r, xh, xh * g


def _rms_bwd(dy, xh, r, g):
    dxh = dy * g
    dh = r * (dxh - xh * jnp.mean(dxh * xh, axis=-1, keepdims=True))
    return dh, jnp.sum(dy * xh, axis=0, keepdims=True)


def _fwd_in(x, g_mix, w_in):
    s_len = x.shape[0]
    tm = min(512, s_len)

    def body(x_ref, g_ref, w_ref, u_ref, zp_ref, q_ref, kv_ref, zg_ref):
        _, _, u = _rms_fwd(x_ref[...], g_ref[...])
        u = u.astype(BF16)
        u_ref[...] = u
        zp_ref[...] = _dot(u, w_ref[:, 0:512]).astype(BF16)
        q_ref[...] = _dot(u, w_ref[:, 512:1024]).astype(BF16)
        kv_ref[...] = _dot(u, w_ref[:, 1024:1280]).astype(BF16)
        zg_ref[...] = _dot(u, w_ref[:, 1280:3328]).astype(BF16)

    return pl.pallas_call(
        body, name="fwd_in", grid=(s_len // tm,),
        in_specs=[_rows(tm, D_MODEL), _whole((1, D_MODEL)), _whole((D_MODEL, IN_WIDTH))],
        out_specs=[_rows(tm, D_MODEL), _rows(tm, 512), _rows(tm, 512), _rows(tm, 256), _rows(tm, 2048)],
        out_shape=[jax.ShapeDtypeStruct((s_len, n), BF16) for n in (D_MODEL, 512, 512, 256, 2048)],
        compiler_params=_params(),
    )(x, g_mix, w_in)


def _attn_constants(sinks):
    qi = np.arange(BLOCK)[:, None]
    kj = np.arange(2 * BLOCK)[None, :]
    dist = BLOCK + qi - kj
    valid = (dist >= 0) & (dist < BLOCK)
    slopes = np.array([2.0 ** (-8.0 * (h + 1) / N_HEADS) for h in range(N_HEADS)], dtype=np.float32)
    bias = np.where(valid[None], -slopes[:, None, None] * dist.astype(np.float32)[None], np.float32(NEG_INF))
    bias = bias.astype(np.float32).reshape(N_KV_HEADS, GROUP * BLOCK, 2 * BLOCK)
    sink_col = jnp.repeat(sinks.astype(F32).reshape(N_KV_HEADS, GROUP), BLOCK, axis=1)[..., None]
    return jnp.asarray(bias), sink_col


def _stack_heads(a, h):
    return jnp.concatenate(
        [a[:, (GROUP * h + g) * HEAD_DIM:(GROUP * h + g + 1) * HEAD_DIM] for g in range(GROUP)], axis=0)


def _unstack_heads(parts):
    cols = []
    for h in range(N_KV_HEADS):
        for g in range(GROUP):
            cols.append(parts[h][g * BLOCK:(g + 1) * BLOCK, :])
    return jnp.concatenate(cols, axis=1)


def _attn_probs(q_st, kk_h, bias, sink_col, first):
    s = _dot_nt(q_st, kk_h) * ATTN_SCALE + bias
    col = lax.broadcasted_iota(jnp.int32, s.shape, 1)
    s = jnp.where(jnp.logical_and(first, col < BLOCK), NEG_INF, s)
    m = jnp.maximum(jnp.max(s, axis=-1, keepdims=True), sink_col)
    p = jnp.exp(s - m)
    es = jnp.exp(sink_col - m)
    inv = 1.0 / (jnp.sum(p, axis=-1, keepdims=True) + es)
    return p * inv, es * inv


def _pool_d(ext, cur, g, row0):
    w = POOL_WINDOWS[g]
    acc = ext
    k = 1
    while k < w:
        acc = acc + pltpu.roll(acc, k, 0)
        k *= 2
    t = row0 + lax.broadcasted_iota(jnp.int32, cur.shape, 0)
    cnt = jnp.minimum(t + 1, w).astype(F32)
    return acc[POOL_HALO:, :] / cnt - cur


def _mixers_fwd(zp, q, kv, pool_w, pool_b, pool_scale, bias, sink_col):
    s_len = zp.shape[0]
    tq = min(512, s_len)
    nb = tq // BLOCK

    def body(zp_ref, zph_ref, q_ref, kv_ref, kvh_ref, pw_ref, pb_ref, ps_ref, bias_ref, sink_ref,
             pm_ref, o_ref, kvx_ref):
        i = pl.program_id(0)
        cur = zp_ref[...].astype(F32)
        halo = zph_ref[...].astype(F32) * (i > 0).astype(F32)
        ext = jnp.concatenate([halo, cur], axis=0)
        for g in range(4):
            sl = slice(g * POOL_GROUP_DIM, (g + 1) * POOL_GROUP_DIM)
            d = _pool_d(ext[:, sl], cur[:, sl], g, i * tq)
            y = _dot(d.astype(BF16), pw_ref[g]) + pb_ref[:, sl]
            pm_ref[:, sl] = (y * ps_ref[:, sl]).astype(BF16)
        kvx_ref[0:BLOCK, :] = kvh_ref[...]
        kvx_ref[BLOCK:, :] = kv_ref[...]

        def block(b, carry):
            r0 = pl.multiple_of(b * BLOCK, BLOCK)
            qb = q_ref[pl.ds(r0, BLOCK), :]
            kvb = kvx_ref[pl.ds(r0, 2 * BLOCK), :]
            first = jnp.logical_and(i == 0, b == 0)
            outs = []
            for h in range(N_KV_HEADS):
                kk = kvb[:, h * HEAD_DIM:(h + 1) * HEAD_DIM]
                vv = kvb[:, KV_WIDTH + h * HEAD_DIM:KV_WIDTH + (h + 1) * HEAD_DIM]
                pn, _ = _attn_probs(_stack_heads(qb, h), kk, bias_ref[h], sink_ref[h], first)
                outs.append(_dot(pn.astype(BF16), vv))
            o_ref[pl.ds(r0, BLOCK), :] = _unstack_heads(outs).astype(BF16)
            return carry

        lax.fori_loop(0, nb, block, 0)

    halo_pool = pl.BlockSpec((POOL_HALO, 512), lambda i: (jnp.maximum(i * (tq // POOL_HALO) - 1, 0), 0))
    halo_kv = pl.BlockSpec((BLOCK, 256), lambda i: (jnp.maximum(i * nb - 1, 0), 0))
    return pl.pallas_call(
        body, name="mixers_fwd", grid=(s_len // tq,),
        in_specs=[_rows(tq, 512), halo_pool, _rows(tq, 512), _rows(tq, 256), halo_kv,
                  _whole((4, 128, 128)), _whole((1, 512)), _whole((1, 512)),
                  _whole((N_KV_HEADS, GROUP * BLOCK, 2 * BLOCK)), _whole((N_KV_HEADS, GROUP * BLOCK, 1))],
        out_specs=[_rows(tq, 512), _rows(tq, 512)],
        out_shape=[jax.ShapeDtypeStruct((s_len, 512), BF16)] * 2,
        scratch_shapes=[pltpu.VMEM((tq + BLOCK, 256), BF16)],
        compiler_params=_params(),
    )(zp, zp, q, kv, kv, pool_w, pool_b, pool_scale, bias, sink_col)


def _gated_mix(pm, o, zg, pp_ref, pa_ref):
    yp = _dot(pm, pp_ref[...])
    ya = _dot(o, pa_ref[...])
    gp = jax.nn.sigmoid(zg[:, :D_MODEL].astype(F32))
    ga = jax.nn.sigmoid(zg[:, D_MODEL:].astype(F32))
    return yp, ya, gp, ga


def _mix_out(x, pm, o, zg, p_pool, p_attn, w_out):
    s_len = x.shape[0]
    tm = min(512, s_len)

    def body(x_ref, pm_ref, o_ref, zg_ref, pp_ref, pa_ref, wo_ref, h1_ref, mixed_ref):
        yp, ya, gp, ga = _gated_mix(pm_ref[...], o_ref[...], zg_ref[...], pp_ref, pa_ref)
        mixed = (gp * yp + ga * ya).astype(BF16)
        mixed_ref[...] = mixed
        h1_ref[...] = x_ref[...] + _dot(mixed, wo_ref[...])

    return pl.pallas_call(
        body, name="mix_out", grid=(s_len // tm,),
        in_specs=[_rows(tm, D_MODEL), _rows(tm, 512), _rows(tm, 512), _rows(tm, 2048),
                  _whole((512, D_MODEL)), _whole((512, D_MODEL)), _whole((D_MODEL, D_MODEL))],
        out_specs=[_rows(tm, D_MODEL), _rows(tm, D_MODEL)],
        out_shape=[jax.ShapeDtypeStruct((s_len, D_MODEL), F32), jax.ShapeDtypeStruct((s_len, D_MODEL), BF16)],
        compiler_params=_params(),
    )(x, pm, o, zg, p_pool, p_attn, w_out)


def _mlp_loss(h1, tgt, g_mlp, g_fin, w_up, w_down):
    s_len = h1.shape[0]
    tm = min(256, s_len)
    n_chunks = D_FF // FF_CHUNK

    def body(h1_ref, tgt_ref, gm_ref, gf_ref, wu_ref, wd_ref,
             dh1_ref, a_ref, dap_ref, u2_ref, dh2_ref, loss_ref, dgm_ref, dgf_ref):
        i = pl.program_id(0)

        @pl.when(i == 0)
        def _():
            loss_ref[...] = jnp.zeros_like(loss_ref)
            dgm_ref[...] = jnp.zeros_like(dgm_ref)
            dgf_ref[...] = jnp.zeros_like(dgf_ref)

        h1 = h1_ref[...]
        r2, xh2, u2 = _rms_fwd(h1, gm_ref[...])
        u2 = u2.astype(BF16)
        u2_ref[...] = u2
        acc = jnp.zeros((tm, D_MODEL), F32)
        for c in range(n_chunks):
            cs = slice(c * FF_CHUNK, (c + 1) * FF_CHUNK)
            a = jnp.maximum(_dot(u2, wu_ref[:, cs]), 0.0)
            a_ref[:, cs] = a.astype(BF16)
            acc = acc + _dot((a * a).astype(BF16), wd_ref[cs, :])
        h2 = h1 + acc
        r3, xh3, y = _rms_fwd(h2, gf_ref[...])
        diff = y - tgt_ref[...]
        loss_ref[...] += 0.5 * jnp.sum(jnp.mean(diff * diff, axis=-1, keepdims=True))
        dy = diff * (1.0 / D_MODEL)
        dh2, dgf = _rms_bwd(dy, xh3, r3, gf_ref[...])
        dgf_ref[...] += dgf
        dh2_bf = dh2.astype(BF16)
        dh2_ref[...] = dh2_bf
        du2 = jnp.zeros((tm, D_MODEL), F32)
        for c in range(n_chunks):
            cs = slice(c * FF_CHUNK, (c + 1) * FF_CHUNK)
            ds = _dot_nt(dh2_bf, wd_ref[cs, :])
            dap = (ds * (2.0 * a_ref[:, cs].astype(F32))).astype(BF16)
            dap_ref[:, cs] = dap
            du2 = du2 + _dot_nt(dap, wu_ref[:, cs])
        dh1n, dgm = _rms_bwd(du2, xh2, r2, gm_ref[...])
        dgm_ref[...] += dgm
        dh1_ref[...] = dh2 + dh1n

    single = dict(pipeline_mode=pl.Buffered(1))
    return pl.pallas_call(
        body, name="mlp_loss", grid=(s_len // tm,),
        in_specs=[_rows(tm, D_MODEL), _rows(tm, D_MODEL), _whole((1, D_MODEL)), _whole((1, D_MODEL)),
                  pl.BlockSpec((D_MODEL, D_FF), lambda i: (0, 0), **single),
                  pl.BlockSpec((D_FF, D_MODEL), lambda i: (0, 0), **single)],
        out_specs=[_rows(tm, D_MODEL), _rows(tm, D_FF), _rows(tm, D_FF), _rows(tm, D_MODEL), _rows(tm, D_MODEL),
                   _whole((8, LANES)), _whole((1, D_MODEL)), _whole((1, D_MODEL))],
        out_shape=[jax.ShapeDtypeStruct((s_len, D_MODEL), F32), jax.ShapeDtypeStruct((s_len, D_FF), BF16),
                   jax.ShapeDtypeStruct((s_len, D_FF), BF16), jax.ShapeDtypeStruct((s_len, D_MODEL), BF16),
                   jax.ShapeDtypeStruct((s_len, D_MODEL), BF16), jax.ShapeDtypeStruct((8, LANES), F32),
                   jax.ShapeDtypeStruct((1, D_MODEL), F32), jax.ShapeDtypeStruct((1, D_MODEL), F32)],
        compiler_params=_params(),
    )(h1, tgt, g_mlp, g_fin, w_up, w_down)


def _tn_matmul(a, b, square_a=False):
    s_len, ka = a.shape
    nb = b.shape[1]
    tt = min(1024, s_len)
    tk = min(1024, ka)
    tn = min(1024, nb)
    n_t = s_len // tt

    def body(a_ref, b_ref, o_ref, acc_ref):
        t = pl.program_id(2)

        @pl.when(t == 0)
        def _():
            acc_ref[...] = jnp.zeros_like(acc_ref)

        av = a_ref[...]
        if square_a:
            av = av * av
        acc_ref[...] += _dot_tn(av.astype(BF16), b_ref[...].astype(BF16))

        @pl.when(t == n_t - 1)
        def _():
            o_ref[...] = acc_ref[...].astype(o_ref.dtype)

    return pl.pallas_call(
        body, name="tn_matmul", grid=(ka // tk, nb // tn, n_t),
        in_specs=[pl.BlockSpec((tt, tk), lambda k, j, t: (t, k)), pl.BlockSpec((tt, tn), lambda k, j, t: (t, j))],
        out_specs=pl.BlockSpec((tk, tn), lambda k, j, t: (k, j)),
        out_shape=jax.ShapeDtypeStruct((ka, nb), BF16),
        scratch_shapes=[pltpu.VMEM((tk, tn), F32)],
        compiler_params=_params(3),
    )(a, b)


def _mix_bwd(dh1, pm, o, zg, p_pool, p_attn, w_out):
    s_len = dh1.shape[0]
    tm = min(512, s_len)

    def body(dh1_ref, pm_ref, o_ref, zg_ref, pp_ref, pa_ref, wo_ref, dyp_ref, dya_ref, dzg_ref, dpm_ref, do_ref):
        dm = _dot_nt(dh1_ref[...].astype(BF16), wo_ref[...])
        yp, ya, gp, ga = _gated_mix(pm_ref[...], o_ref[...], zg_ref[...], pp_ref, pa_ref)
        dyp = (dm * gp).astype(BF16)
        dya = (dm * ga).astype(BF16)
        dyp_ref[...] = dyp
        dya_ref[...] = dya
        dzg_ref[:, :D_MODEL] = (dm * yp * (gp * (1.0 - gp))).astype(BF16)
        dzg_ref[:, D_MODEL:] = (dm * ya * (ga * (1.0 - ga))).astype(BF16)
        dpm_ref[...] = _dot_nt(dyp, pp_ref[...]).astype(BF16)
        do_ref[...] = _dot_nt(dya, pa_ref[...]).astype(BF16)

    return pl.pallas_call(
        body, name="mix_bwd", grid=(s_len // tm,),
        in_specs=[_rows(tm, D_MODEL), _rows(tm, 512), _rows(tm, 512), _rows(tm, 2048),
                  _whole((512, D_MODEL)), _whole((512, D_MODEL)), _whole((D_MODEL, D_MODEL))],
        out_specs=[_rows(tm, D_MODEL), _rows(tm, D_MODEL), _rows(tm, 2048), _rows(tm, 512), _rows(tm, 512)],
        out_shape=[jax.ShapeDtypeStruct((s_len, n), BF16) for n in (D_MODEL, D_MODEL, 2048, 512, 512)],
        compiler_params=_params(),
    )(dh1, pm, o, zg, p_pool, p_attn, w_out)


def _mixers_bwd(zp, q, kv, dpm, do, pool_w, pool_b, pool_scale, bias, sink_col):
    s_len = zp.shape[0]
    tq = min(512, s_len)
    nb = tq // BLOCK
    n_steps = s_len // tq

    def body(zp_ref, zph_ref, q_ref, kv_ref, kvh_ref, dpm_ref, dpmh_ref, do_ref, pw_ref, pb_ref, ps_ref,
             bias_ref, sink_ref, dzp_ref, dq_ref, dkv_ref, dpw_ref, dpb_ref, dps_ref, dsk_ref, kvx_ref, dsk_acc):
        i = pl.program_id(0)

        @pl.when(i == 0)
        def _():
            dkv_ref[...] = jnp.zeros_like(dkv_ref)
            dpw_ref[...] = jnp.zeros_like(dpw_ref)
            dpb_ref[...] = jnp.zeros_like(dpb_ref)
            dps_ref[...] = jnp.zeros_like(dps_ref)
            dsk_acc[...] = jnp.zeros_like(dsk_acc)

        cur = zp_ref[...].astype(F32)
        halo = zph_ref[...].astype(F32) * (i > 0).astype(F32)
        ext = jnp.concatenate([halo, cur], axis=0)
        dpm_next = dpmh_ref[...].astype(F32) * (i < n_steps - 1).astype(F32)
        dpm_ext = jnp.concatenate([dpm_ref[...].astype(F32), dpm_next], axis=0)
        n_ext = tq + POOL_HALO
        for g in range(4):
            sl = slice(g * POOL_GROUP_DIM, (g + 1) * POOL_GROUP_DIM)
            w = POOL_WINDOWS[g]
            d = _pool_d(ext[:, sl], cur[:, sl], g, i * tq).astype(BF16)
            y_lin = _dot(d, pw_ref[g]) + pb_ref[:, sl]
            dps_ref[:, sl] += jnp.sum(dpm_ext[:tq, sl] * y_lin, axis=0, keepdims=True)
            dyl_ext = dpm_ext[:, sl] * ps_ref[:, sl]
            dpb_ref[:, sl] += jnp.sum(dyl_ext[:tq], axis=0, keepdims=True)
            dyl_bf = dyl_ext.astype(BF16)
            dpw_ref[g] += _dot_tn(d, dyl_bf[:tq])
            dd = _dot_nt(dyl_bf, pw_ref[g])
            t = i * tq + lax.broadcasted_iota(jnp.int32, dd.shape, 0)
            e = dd / jnp.minimum(t + 1, w).astype(F32)
            acc = e
            k = 1
            while k < w:
                acc = acc + pltpu.roll(acc, n_ext - k, 0)
                k *= 2
            dzp_ref[:, sl] = (acc[:tq] - dd[:tq]).astype(BF16)

        kvx_ref[0:BLOCK, :] = kvh_ref[...]
        kvx_ref[BLOCK:, :] = kv_ref[...]

        def block(b, carry):
            r0 = pl.multiple_of(b * BLOCK, BLOCK)
            qb = q_ref[pl.ds(r0, BLOCK), :]
            dob = do_ref[pl.ds(r0, BLOCK), :]
            kvb = kvx_ref[pl.ds(r0, 2 * BLOCK), :]
            first = jnp.logical_and(i == 0, b == 0)
            dqs, dks, dvs = [], [], []
            for h in range(N_KV_HEADS):
                kk = kvb[:, h * HEAD_DIM:(h + 1) * HEAD_DIM]
                vv = kvb[:, KV_WIDTH + h * HEAD_DIM:KV_WIDTH + (h + 1) * HEAD_DIM]
                q_st = _stack_heads(qb, h)
                do_st = _stack_heads(dob, h)
                pn, psink = _attn_probs(q_st, kk, bias_ref[h], sink_ref[h], first)
                dp = _dot_nt(do_st, vv)
                delta = jnp.sum(pn * dp, axis=-1, keepdims=True)
                dsk_acc[h] += -psink * delta
                ds = ((pn * (dp - delta)) * ATTN_SCALE).astype(BF16)
                dqs.append(_dot(ds, kk))
                dks.append(_dot_tn(ds, q_st))
                dvs.append(_dot_tn(pn.astype(BF16), do_st))
            dq_ref[pl.ds(r0, BLOCK), :] = _unstack_heads(dqs).astype(BF16)
            g0 = pl.multiple_of(i * tq + r0, BLOCK)
            dkv_ref[pl.ds(g0, 2 * BLOCK), :] += jnp.concatenate(dks + dvs, axis=1)
            return carry

        lax.fori_loop(0, nb, block, 0)

        @pl.when(i == n_steps - 1)
        def _():
            for h in range(N_KV_HEADS):
                for g in range(GROUP):
                    tot = jnp.sum(dsk_acc[h, g * BLOCK:(g + 1) * BLOCK, :], axis=0, keepdims=True)
                    dsk_ref[GROUP * h + g:GROUP * h + g + 1, :] = jnp.broadcast_to(tot, (1, LANES))

    blocks_per_tile = tq // POOL_HALO
    last_halo = s_len // POOL_HALO - 1
    halo_prev = pl.BlockSpec((POOL_HALO, 512), lambda i: (jnp.maximum(i * blocks_per_tile - 1, 0), 0))
    halo_next = pl.BlockSpec((POOL_HALO, 512), lambda i: (jnp.minimum((i + 1) * blocks_per_tile, last_halo), 0))
    halo_kv = pl.BlockSpec((BLOCK, 256), lambda i: (jnp.maximum(i * nb - 1, 0), 0))
    return pl.pallas_call(
        body, name="mixers_bwd", grid=(n_steps,),
        in_specs=[_rows(tq, 512), halo_prev, _rows(tq, 512), _rows(tq, 256), halo_kv,
                  _rows(tq, 512), halo_next, _rows(tq, 512),
                  _whole((4, 128, 128)), _whole((1, 512)), _whole((1, 512)),
                  _whole((N_KV_HEADS, GROUP * BLOCK, 2 * BLOCK)), _whole((N_KV_HEADS, GROUP * BLOCK, 1))],
        out_specs=[_rows(tq, 512), _rows(tq, 512), _whole((s_len + BLOCK, 256)),
                   _whole((4, 128, 128)), _whole((1, 512)), _whole((1, 512)), _whole((8, LANES))],
        out_shape=[jax.ShapeDtypeStruct((s_len, 512), BF16), jax.ShapeDtypeStruct((s_len, 512), BF16),
                   jax.ShapeDtypeStruct((s_len + BLOCK, 256), F32), jax.ShapeDtypeStruct((4, 128, 128), F32),
                   jax.ShapeDtypeStruct((1, 512), F32), jax.ShapeDtypeStruct((1, 512), F32),
                   jax.ShapeDtypeStruct((8, LANES), F32)],
        scratch_shapes=[pltpu.VMEM((tq + BLOCK, 256), BF16), pltpu.VMEM((N_KV_HEADS, GROUP * BLOCK, 1), F32)],
        compiler_params=_params(),
    )(zp, zp, q, kv, kv, dpm, dpm, do, pool_w, pool_b, pool_scale, bias, sink_col)


def _in_bwd(dzp, dq, dkv, dzg, w_in, x, dh1, g_mix):
    s_len = x.shape[0]
    tm = min(512, s_len)

    def body(dzp_ref, dq_ref, dkv_ref, dzg_ref, w_ref, x_ref, dh1_ref, g_ref, dx_ref, dg_ref):
        i = pl.program_id(0)

        @pl.when(i == 0)
        def _():
            dg_ref[...] = jnp.zeros_like(dg_ref)

        du = _dot_nt(dzp_ref[...], w_ref[:, 0:512])
        du = du + _dot_nt(dq_ref[...], w_ref[:, 512:1024])
        du = du + _dot_nt(dkv_ref[...], w_ref[:, 1024:1280])
        du = du + _dot_nt(dzg_ref[...], w_ref[:, 1280:3328])
        r, xh, _ = _rms_fwd(x_ref[...], g_ref[...])
        dxn, dg = _rms_bwd(du, xh, r, g_ref[...])
        dg_ref[...] += dg
        dx_ref[...] = dh1_ref[...] + dxn

    return pl.pallas_call(
        body, name="in_bwd", grid=(s_len // tm,),
        in_specs=[_rows(tm, 512), _rows(tm, 512), _rows(tm, 256), _rows(tm, 2048), _whole((D_MODEL, IN_WIDTH)),
                  _rows(tm, D_MODEL), _rows(tm, D_MODEL), _whole((1, D_MODEL))],
        out_specs=[_rows(tm, D_MODEL), _whole((1, D_MODEL))],
        out_shape=[jax.ShapeDtypeStruct((s_len, D_MODEL), F32), jax.ShapeDtypeStruct((1, D_MODEL), F32)],
        compiler_params=_params(),
    )(dzp, dq, dkv, dzg, w_in, x, dh1, g_mix)


ANY = pl.BlockSpec(memory_space=pl.ANY)


def _all_gather_weights(shards):
    n = len(shards)

    def body(*refs):
        ins, outs = refs[:n], refs[n:2 * n]
        send_sems, recv_sems, local_sems = refs[2 * n:]
        x, y, c = lax.axis_index("x"), lax.axis_index("y"), lax.axis_index("c")
        me, sibling = (x, y, c), (x, y, 1 - c)
        chips = [(1 - x, y), (x, 1 - y), (1 - x, 1 - y)]

        def slot(a, px, py, pc):
            return outs[a].at[4 * px + 2 * py + pc]

        def copy(a, k, block, to, src=None):
            return pltpu.make_async_remote_copy(
                src_ref=slot(a, *block) if src is None else src, dst_ref=slot(a, *block),
                send_sem=send_sems.at[a, k], recv_sem=recv_sems.at[a, k], device_id=to, device_id_type=MESH)

        mine = [pltpu.make_async_copy(ins[a], slot(a, *me), local_sems.at[a]) for a in range(n)]
        for cp in mine:
            cp.start()
        first = []
        for a in range(n):
            first.append(copy(a, 0, me, sibling, src=ins[a]))
            first += [copy(a, 1 + j, me, (*chip, c), src=ins[a]) for j, chip in enumerate(chips)]
        for cp in first:
            cp.start()
        passed = []
        for a in range(n):
            for j, chip in enumerate(chips):
                copy(a, 1 + j, (*chip, c), me).wait_recv()
                cp = copy(a, 4 + j, (*chip, c), sibling)
                cp.start()
                passed.append(cp)
        for a in range(n):
            copy(a, 0, sibling, me).wait_recv()
            for j, chip in enumerate(chips):
                copy(a, 4 + j, (*chip, 1 - c), me).wait_recv()
        for cp in first + passed:
            cp.wait_send()
        for cp in mine:
            cp.wait()

    return pl.pallas_call(
        body, name="all_gather_weights",
        in_specs=[ANY] * n, out_specs=[ANY] * n,
        out_shape=[jax.ShapeDtypeStruct((N_DEV,) + s.shape, s.dtype) for s in shards],
        scratch_shapes=[pltpu.SemaphoreType.DMA((n, 7)), pltpu.SemaphoreType.DMA((n, 7)), pltpu.SemaphoreType.DMA((n,))],
    )(*shards)


def _exchange_grads(sends, small):
    n = len(sends)

    def body(*refs):
        ins, small_ref = refs[:n], refs[n]
        outs, small_out = refs[n + 1:2 * n + 1], refs[2 * n + 1]
        send_sems, recv_sems, local_sems = refs[2 * n + 2:]
        x, y, c = lax.axis_index("x"), lax.axis_index("y"), lax.axis_index("c")
        me_idx = 4 * x + 2 * y + c
        local = [pltpu.make_async_copy(ins[a].at[me_idx], outs[a].at[0], local_sems.at[a]) for a in range(n)]
        local.append(pltpu.make_async_copy(small_ref, small_out.at[me_idx], local_sems.at[n]))
        for cp in local:
            cp.start()
        copies = []
        for k in range(1, N_DEV):
            kx, ky, kc = (k >> 2) & 1, (k >> 1) & 1, k & 1
            px = 1 - x if kx else x
            py = 1 - y if ky else y
            pc = 1 - c if kc else c
            p_idx = 4 * px + 2 * py + pc
            for a in range(n):
                copies.append(pltpu.make_async_remote_copy(
                    src_ref=ins[a].at[p_idx], dst_ref=outs[a].at[k], send_sem=send_sems.at[a, k - 1],
                    recv_sem=recv_sems.at[a, k - 1], device_id=(px, py, pc), device_id_type=MESH))
            copies.append(pltpu.make_async_remote_copy(
                src_ref=small_ref, dst_ref=small_out.at[me_idx], send_sem=send_sems.at[n, k - 1],
                recv_sem=recv_sems.at[n, k - 1], device_id=(px, py, pc), device_id_type=MESH))
        for cp in copies:
            cp.start()
        for cp in copies:
            cp.wait_send()
        for k in range(1, N_DEV):
            kx, ky, kc = (k >> 2) & 1, (k >> 1) & 1, k & 1
            px = 1 - x if kx else x
            py = 1 - y if ky else y
            pc = 1 - c if kc else c
            p_idx = 4 * px + 2 * py + pc
            for a in range(n):
                pltpu.make_async_remote_copy(
                    src_ref=ins[a].at[p_idx], dst_ref=outs[a].at[k], send_sem=send_sems.at[a, k - 1],
                    recv_sem=recv_sems.at[a, k - 1], device_id=(px, py, pc), device_id_type=MESH).wait_recv()
            pltpu.make_async_remote_copy(
                src_ref=small_ref, dst_ref=small_out.at[p_idx], send_sem=send_sems.at[n, k - 1],
                recv_sem=recv_sems.at[n, k - 1], device_id=(px, py, pc), device_id_type=MESH).wait_recv()
        for cp in local:
            cp.wait()

    return pl.pallas_call(
        body, name="exchange_grads",
        in_specs=[ANY] * (n + 1), out_specs=[ANY] * (n + 1),
        out_shape=[jax.ShapeDtypeStruct(s.shape, s.dtype) for s in sends]
        + [jax.ShapeDtypeStruct((N_DEV,) + small.shape, small.dtype)],
        scratch_shapes=[pltpu.SemaphoreType.DMA((n + 1, 7)), pltpu.SemaphoreType.DMA((n + 1, 7)),
                        pltpu.SemaphoreType.DMA((n + 1,))],
    )(*sends, small)


def _adamw(parts, w, m, v):
    r, c = w.shape
    tr = 256 if r % 256 == 0 else r

    def body(p_ref, w_ref, m_ref, v_ref, g_ref, d_ref, nm_ref, nv_ref):
        g = p_ref[0].astype(F32)
        for k in range(1, N_DEV):
            g = g + p_ref[k].astype(F32)
        m_new = ADAM_B1 * m_ref[...] + (1.0 - ADAM_B1) * g
        v_new = ADAM_B2 * v_ref[...] + (1.0 - ADAM_B2) * (g * g)
        m_hat = m_new / (1.0 - ADAM_B1 ** ADAM_STEP)
        v_hat = v_new / (1.0 - ADAM_B2 ** ADAM_STEP)
        g_ref[...] = g
        d_ref[...] = -ADAM_LR * (m_hat / (jnp.sqrt(v_hat) + ADAM_EPS) + ADAM_WD * w_ref[...])
        nm_ref[...] = m_new
        nv_ref[...] = v_new

    return pl.pallas_call(
        body, name="adamw", grid=(r // tr,),
        in_specs=[pl.BlockSpec((N_DEV, tr, c), lambda i: (0, i, 0))] + [_rows(tr, c)] * 3,
        out_specs=[_rows(tr, c)] * 4,
        out_shape=[jax.ShapeDtypeStruct((r, c), F32)] * 4,
        compiler_params=_params(),
    )(parts, w, m, v)


SMALL_LAYOUT = (("norm_mix", 8), ("pool_w", 512), ("pool_b", 4), ("pool_scale", 4), ("attn_sinks", 1),
                ("norm_mlp", 8), ("norm_final", 8), ("loss", 1))
SMALL_ROWS = 552


def _pack_small(vals):
    rows = []
    for name, n_rows in SMALL_LAYOUT:
        flat = vals[name].astype(F32).reshape(-1)
        flat = jnp.pad(flat, (0, n_rows * LANES - flat.shape[0]))
        rows.append(flat.reshape(n_rows, LANES))
    used = sum(n for _, n in SMALL_LAYOUT)
    rows.append(jnp.zeros((SMALL_ROWS - used, LANES), F32))
    return jnp.concatenate(rows, axis=0)


def _unpack_small(pack, shapes):
    out, r0 = {}, 0
    for name, n_rows in SMALL_LAYOUT:
        size = int(np.prod(shapes[name])) if shapes[name] else 1
        out[name] = pack[r0:r0 + n_rows].reshape(-1)[:size].reshape(shapes[name])
        r0 += n_rows
    return out


def kernel(x, norm_mix, w_in, pool_w, pool_b, pool_scale, attn_sinks, p_pool, p_attn, w_out, norm_mlp, w_up, w_down, norm_final, loss_target, m_norm_mix, m_w_in, m_pool_w, m_pool_b, m_pool_scale, m_attn_sinks, m_p_pool, m_p_attn, m_w_out, m_norm_mlp, m_w_up, m_w_down, m_norm_final, v_norm_mix, v_w_in, v_pool_w, v_pool_b, v_pool_scale, v_attn_sinks, v_p_pool, v_p_attn, v_w_out, v_norm_mlp, v_w_up, v_w_down, v_norm_final):
    xs = x[0]
    tgt = loss_target[0]
    s_len = xs.shape[0]

    shards = [w_in[0], p_pool[0], p_attn[0], w_out[0], w_up[0], w_down[0]]
    gathered = _all_gather_weights([s.astype(BF16) for s in shards])
    w_in_f = gathered[0].transpose(1, 0, 2).reshape(D_MODEL, IN_WIDTH)
    p_pool_f = gathered[1].transpose(1, 0, 2).reshape(POOL_WIDTH, D_MODEL)
    p_attn_f = gathered[2].transpose(1, 0, 2).reshape(ATTN_WIDTH, D_MODEL)
    w_out_f = gathered[3].reshape(D_MODEL, D_MODEL)
    w_up_f = gathered[4].transpose(1, 0, 2).reshape(D_MODEL, D_FF)
    w_down_f = gathered[5].reshape(D_FF, D_MODEL)

    pool_w_bf = pool_w[0].astype(BF16)
    pool_b_row = pool_b[0].reshape(1, POOL_WIDTH)
    bias, sink_col = _attn_constants(attn_sinks[0])

    u, zp, q, kv, zg = _fwd_in(xs, norm_mix, w_in_f)
    pm, o = _mixers_fwd(zp, q, kv, pool_w_bf, pool_b_row, pool_scale, bias, sink_col)
    h1, mixed = _mix_out(xs, pm, o, zg, p_pool_f, p_attn_f, w_out_f)
    dh1, a, dapre, u2, dh2, loss_part, g_norm_mlp, g_norm_final = _mlp_loss(
        h1, tgt, norm_mlp, norm_final.reshape(1, D_MODEL), w_up_f, w_down_f)
    gw_down = _tn_matmul(a, dh2, square_a=True)
    gw_up = _tn_matmul(u2, dapre)
    dyp, dya, dzg, dpm, do = _mix_bwd(dh1, pm, o, zg, p_pool_f, p_attn_f, w_out_f)
    gw_out = _tn_matmul(mixed, dh1)
    gp_pool = _tn_matmul(pm, dyp)
    gp_attn = _tn_matmul(o, dya)
    dzp, dq, dkv_acc, g_pool_w, g_pool_b, g_pool_scale, g_sinks = _mixers_bwd(
        zp, q, kv, dpm, do, pool_w_bf, pool_b_row, pool_scale, bias, sink_col)
    dkv = dkv_acc[BLOCK:].astype(BF16)
    dx, g_norm_mix = _in_bwd(dzp, dq, dkv, dzg, w_in_f, xs, dh1, norm_mix)
    gw_in = jnp.concatenate(
        [_tn_matmul(u, dzp), _tn_matmul(u, dq), _tn_matmul(u, dkv), _tn_matmul(u, dzg)], axis=1)

    sends = [
        gw_in.reshape(D_MODEL, N_DEV, IN_WIDTH // N_DEV).transpose(1, 0, 2),
        gp_pool.reshape(POOL_WIDTH, N_DEV, D_MODEL // N_DEV).transpose(1, 0, 2),
        gp_attn.reshape(ATTN_WIDTH, N_DEV, D_MODEL // N_DEV).transpose(1, 0, 2),
        gw_out.reshape(N_DEV, D_MODEL // N_DEV, D_MODEL),
        gw_up.reshape(D_MODEL, N_DEV, D_FF // N_DEV).transpose(1, 0, 2),
        gw_down.reshape(N_DEV, D_FF // N_DEV, D_MODEL),
    ]
    small_vals = dict(norm_mix=g_norm_mix, pool_w=g_pool_w, pool_b=g_pool_b, pool_scale=g_pool_scale,
                      attn_sinks=g_sinks[:, 0], norm_mlp=g_norm_mlp, norm_final=g_norm_final, loss=loss_part[0, 0])
    *recvs, small_all = _exchange_grads(sends, _pack_small(small_vals))

    big_names = ["w_in", "p_pool", "p_attn", "w_out", "w_up", "w_down"]
    big_w = dict(w_in=w_in, p_pool=p_pool, p_attn=p_attn, w_out=w_out, w_up=w_up, w_down=w_down)
    big_m = dict(w_in=m_w_in, p_pool=m_p_pool, p_attn=m_p_attn, w_out=m_w_out, w_up=m_w_up, w_down=m_w_down)
    big_v = dict(w_in=v_w_in, p_pool=v_p_pool, p_attn=v_p_attn, w_out=v_w_out, w_up=v_w_up, w_down=v_w_down)
    res = {}
    for name, parts in zip(big_names, recvs):
        outs = _adamw(parts, big_w[name][0], big_m[name][0], big_v[name][0])
        res[name] = [t[None] for t in outs]

    small_names = ["norm_mix", "pool_w", "pool_b", "pool_scale", "attn_sinks", "norm_mlp", "norm_final"]
    small_w = dict(norm_mix=norm_mix, pool_w=pool_w, pool_b=pool_b, pool_scale=pool_scale, attn_sinks=attn_sinks,
                   norm_mlp=norm_mlp, norm_final=norm_final)
    small_m = dict(norm_mix=m_norm_mix, pool_w=m_pool_w, pool_b=m_pool_b, pool_scale=m_pool_scale,
                   attn_sinks=m_attn_sinks, norm_mlp=m_norm_mlp, norm_final=m_norm_final)
    small_v = dict(norm_mix=v_norm_mix, pool_w=v_pool_w, pool_b=v_pool_b, pool_scale=v_pool_scale,
                   attn_sinks=v_attn_sinks, norm_mlp=v_norm_mlp, norm_final=v_norm_final)
    zero = jnp.zeros((), F32)
    packs = [_pack_small({**d, "loss": zero}) for d in (small_w, small_m, small_v)]
    s_outs = _adamw(small_all, *packs)
    shapes = {k: small_w[k].shape for k in small_names}
    shapes["loss"] = ()
    s_res = [_unpack_small(t, shapes) for t in s_outs]
    loss = s_res[0]["loss"]
    for name in small_names:
        res[name] = [t[name] for t in s_res]

    order = ["norm_mix", "w_in", "pool_w", "pool_b", "pool_scale", "attn_sinks", "p_pool", "p_attn", "w_out",
             "norm_mlp", "w_up", "w_down", "norm_final"]
    out = [loss, dx[None]]
    for kind in range(4):
        out += [res[name][kind] for name in order]
    return tuple(out)
```

```python
import functools
import math

import numpy as np
import jax
import jax.numpy as jnp
from jax import lax
from jax.experimental import pallas as pl
from jax.experimental.pallas import tpu as pltpu

F32 = jnp.float32
BF16 = jnp.bfloat16

D_MODEL = 1024
POOL_WIDTH = 512
ATTN_WIDTH = 512
KV_WIDTH = 128
HEAD_DIM = 64
N_HEADS = 8
N_KV_HEADS = 2
GROUP = 4
BLOCK = 128
POOL_WINDOWS = (2, 4, 8, 16)
POOL_GROUP_DIM = 128
POOL_HALO = 16
D_FF = 4096
FF_CHUNK = 1024
IN_WIDTH = 3328
RMS_EPS = 1e-5
NEG_INF = -1e30
ATTN_SCALE = 1.0 / math.sqrt(HEAD_DIM)
N_DEV = 8

ADAM_LR = 0.001
ADAM_B1 = 0.9
ADAM_B2 = 0.999
ADAM_EPS = 1e-08
ADAM_WD = 0.01
ADAM_STEP = 10

LANES = 128
VMEM_LIMIT_BYTES = 56 * 1024 * 1024
MESH = pl.DeviceIdType.MESH


def _params(n_grid_axes=1):
    return pltpu.CompilerParams(
        dimension_semantics=("arbitrary",) * n_grid_axes, vmem_limit_bytes=VMEM_LIMIT_BYTES)


def _dot(a, b):
    return jnp.dot(a, b, preferred_element_type=F32)


def _dot_nt(a, b):
    return lax.dot_general(a, b, (((1,), (1,)), ((), ())), preferred_element_type=F32)


def _dot_tn(a, b):
    return lax.dot_general(a, b, (((0,), (0,)), ((), ())), preferred_element_type=F32)


def _rows(tm, n):
    return pl.BlockSpec((tm, n), lambda i: (i, 0))


def _whole(shape):
    zeros = (0,) * len(shape)
    return pl.BlockSpec(shape, lambda i: zeros)


def _rms_fwd(h, g):
    r = lax.rsqrt(jnp.mean(h * h, axis=-1, keepdims=True) + RMS_EPS)
    xh = h * r
    return r, xh, xh * g


def _rms_bwd(dy, xh, r, g):
    dxh = dy * g
    dh = r * (dxh - xh * jnp.mean(dxh * xh, axis=-1, keepdims=True))
    return dh, jnp.sum(dy * xh, axis=0, keepdims=True)


def _fwd_in(x, g_mix, w_in):
    s_len = x.shape[0]
    tm = min(512, s_len)

    def body(x_ref, g_ref, w_ref, u_ref, zp_ref, q_ref, kv_ref, zg_ref):
        _, _, u = _rms_fwd(x_ref[...], g_ref[...])
        u = u.astype(BF16)
        u_ref[...] = u
        zp_ref[...] = _dot(u, w_ref[:, 0:512]).astype(BF16)
        q_ref[...] = _dot(u, w_ref[:, 512:1024]).astype(BF16)
        kv_ref[...] = _dot(u, w_ref[:, 1024:1280]).astype(BF16)
        zg_ref[...] = _dot(u, w_ref[:, 1280:3328]).astype(BF16)

    return pl.pallas_call(
        body, name="fwd_in", grid=(s_len // tm,),
        in_specs=[_rows(tm, D_MODEL), _whole((1, D_MODEL)), _whole((D_MODEL, IN_WIDTH))],
        out_specs=[_rows(tm, D_MODEL), _rows(tm, 512), _rows(tm, 512), _rows(tm, 256), _rows(tm, 2048)],
        out_shape=[jax.ShapeDtypeStruct((s_len, n), BF16) for n in (D_MODEL, 512, 512, 256, 2048)],
        compiler_params=_params(),
    )(x, g_mix, w_in)


def _attn_constants(sinks):
    qi = np.arange(BLOCK)[:, None]
    kj = np.arange(2 * BLOCK)[None, :]
    dist = BLOCK + qi - kj
    valid = (dist >= 0) & (dist < BLOCK)
    slopes = np.array([2.0 ** (-8.0 * (h + 1) / N_HEADS) for h in range(N_HEADS)], dtype=np.float32)
    bias = np.where(valid[None], -slopes[:, None, None] * dist.astype(np.float32)[None], np.float32(NEG_INF))
    bias = bias.astype(np.float32).reshape(N_KV_HEADS, GROUP * BLOCK, 2 * BLOCK).transpose(0, 2, 1)
    sink_row = jnp.repeat(sinks.astype(F32).reshape(N_KV_HEADS, GROUP), BLOCK, axis=1)[:, None, :]
    return jnp.asarray(np.ascontiguousarray(bias)), sink_row


def _left_half(shape):
    return lax.broadcasted_iota(jnp.int32, shape, 1) < HEAD_DIM


def _dup_halves(slab):
    swapped = pltpu.roll(slab, HEAD_DIM, 1)
    left = _left_half(slab.shape)
    return jnp.where(left, slab, swapped), jnp.where(left, swapped, slab)


def _fill_kv_slabs(kvh_ref, kv_ref, ka_ref, vd_ref):
    for rows, src in ((slice(0, BLOCK), kvh_ref), (slice(BLOCK, None), kv_ref)):
        kvf = src[...].astype(F32)
        for ref, lanes in ((ka_ref, slice(0, KV_WIDTH)), (vd_ref, slice(KV_WIDTH, 2 * KV_WIDTH))):
            d0, d1 = _dup_halves(kvf[:, lanes])
            ref[0, rows, :] = d0.astype(BF16)
            ref[1, rows, :] = d1.astype(BF16)


def _stack_pairs(a, h):
    pieces = []
    for j in range(2):
        pair = a[:, h * 256 + j * LANES:h * 256 + (j + 1) * LANES]
        left = _left_half(pair.shape)
        zero = jnp.zeros_like(pair)
        pieces += [jnp.where(left, pair, zero), jnp.where(left, zero, pair)]
    return jnp.concatenate(pieces, axis=0)


def _attn_probs_t(kk, q_st, bias_t, sink_row, first):
    s = _dot_nt(kk, q_st) * ATTN_SCALE + bias_t
    if first is not None:
        row = lax.broadcasted_iota(jnp.int32, s.shape, 0)
        s = jnp.where(jnp.logical_and(first, row < BLOCK), NEG_INF, s)
    m = jnp.maximum(jnp.max(s, axis=0, keepdims=True), sink_row)
    p = jnp.exp(s - m)
    es = jnp.exp(sink_row - m)
    inv = 1.0 / (jnp.sum(p, axis=0, keepdims=True) + es)
    return p * inv, es * inv


def _pool_d(ext, cur, g, row0):
    w = POOL_WINDOWS[g]
    acc = ext
    k = 1
    while k < w:
        acc = acc + pltpu.roll(acc, k, 0)
        k *= 2
    t = row0 + lax.broadcasted_iota(jnp.int32, cur.shape, 0)
    cnt = jnp.minimum(t + 1, w).astype(F32)
    return acc[POOL_HALO:, :] / cnt - cur


def _mixers_fwd(zp, q, kv, pool_w, pool_b, pool_scale, bias_t, sink_row):
    s_len = zp.shape[0]
    tq = min(512, s_len)
    nb = tq // BLOCK

    def body(zp_ref, zph_ref, q_ref, kv_ref, kvh_ref, pw_ref, pb_ref, ps_ref, bias_ref, sink_ref,
             pm_ref, o_ref, ka_ref, vd_ref):
        i = pl.program_id(0)
        cur = zp_ref[...].astype(F32)
        halo = zph_ref[...].astype(F32) * (i > 0).astype(F32)
        ext = jnp.concatenate([halo, cur], axis=0)
        for g in range(4):
            sl = slice(g * POOL_GROUP_DIM, (g + 1) * POOL_GROUP_DIM)
            d = _pool_d(ext[:, sl], cur[:, sl], g, i * tq)
            y = _dot(d.astype(BF16), pw_ref[g]) + pb_ref[:, sl]
            pm_ref[:, sl] = (y * ps_ref[:, sl]).astype(BF16)
        _fill_kv_slabs(kvh_ref, kv_ref, ka_ref, vd_ref)
        for b in range(nb):
            rq = slice(b * BLOCK, (b + 1) * BLOCK)
            rk = slice(b * BLOCK, (b + 2) * BLOCK)
            qb = q_ref[rq, :]
            for h in range(N_KV_HEADS):
                pn, _ = _attn_probs_t(ka_ref[h, rk, :], _stack_pairs(qb, h), bias_ref[h], sink_ref[h],
                                      (i == 0) if b == 0 else None)
                pn = pn.astype(BF16)
                vd = vd_ref[h, rk, :]
                left = _left_half(vd.shape)
                zero = jnp.zeros_like(vd)
                va, vb = jnp.where(left, vd, zero), jnp.where(left, zero, vd)
                for j in range(2):
                    o_pair = (_dot_tn(pn[:, (2 * j) * BLOCK:(2 * j + 1) * BLOCK], va)
                              + _dot_tn(pn[:, (2 * j + 1) * BLOCK:(2 * j + 2) * BLOCK], vb))
                    o_ref[rq, h * 256 + j * LANES:h * 256 + (j + 1) * LANES] = o_pair.astype(BF16)

    halo_pool = pl.BlockSpec((POOL_HALO, 512), lambda i: (jnp.maximum(i * (tq // POOL_HALO) - 1, 0), 0))
    halo_kv = pl.BlockSpec((BLOCK, 256), lambda i: (jnp.maximum(i * nb - 1, 0), 0))
    return pl.pallas_call(
        body, name="mixers_fwd", grid=(s_len // tq,),
        in_specs=[_rows(tq, 512), halo_pool, _rows(tq, 512), _rows(tq, 256), halo_kv,
                  _whole((4, 128, 128)), _whole((1, 512)), _whole((1, 512)),
                  _whole((N_KV_HEADS, 2 * BLOCK, GROUP * BLOCK)), _whole((N_KV_HEADS, 1, GROUP * BLOCK))],
        out_specs=[_rows(tq, 512), _rows(tq, 512)],
        out_shape=[jax.ShapeDtypeStruct((s_len, 512), BF16)] * 2,
        scratch_shapes=[pltpu.VMEM((N_KV_HEADS, tq + BLOCK, LANES), BF16)] * 2,
        compiler_params=_params(),
    )(zp, zp, q, kv, kv, pool_w, pool_b, pool_scale, bias_t, sink_row)


def _gated_mix(pm, o, zg, pp_ref, pa_ref):
    yp = _dot(pm, pp_ref[...])
    ya = _dot(o, pa_ref[...])
    gp = jax.nn.sigmoid(zg[:, :D_MODEL].astype(F32))
    ga = jax.nn.sigmoid(zg[:, D_MODEL:].astype(F32))
    return yp, ya, gp, ga


def _mix_out(x, pm, o, zg, p_pool, p_attn, w_out):
    s_len = x.shape[0]
    tm = min(512, s_len)

    def body(x_ref, pm_ref, o_ref, zg_ref, pp_ref, pa_ref, wo_ref, h1_ref, mixed_ref):
        yp, ya, gp, ga = _gated_mix(pm_ref[...], o_ref[...], zg_ref[...], pp_ref, pa_ref)
        mixed = (gp * yp + ga * ya).astype(BF16)
        mixed_ref[...] = mixed
        h1_ref[...] = x_ref[...] + _dot(mixed, wo_ref[...])

    return pl.pallas_call(
        body, name="mix_out", grid=(s_len // tm,),
        in_specs=[_rows(tm, D_MODEL), _rows(tm, 512), _rows(tm, 512), _rows(tm, 2048),
                  _whole((512, D_MODEL)), _whole((512, D_MODEL)), _whole((D_MODEL, D_MODEL))],
        out_specs=[_rows(tm, D_MODEL), _rows(tm, D_MODEL)],
        out_shape=[jax.ShapeDtypeStruct((s_len, D_MODEL), F32), jax.ShapeDtypeStruct((s_len, D_MODEL), BF16)],
        compiler_params=_params(),
    )(x, pm, o, zg, p_pool, p_attn, w_out)


def _mlp_loss(h1, tgt, g_mlp, g_fin, w_up, w_down):
    s_len = h1.shape[0]
    tm = min(256, s_len)
    n_chunks = D_FF // FF_CHUNK

    def body(h1_ref, tgt_ref, gm_ref, gf_ref, wu_ref, wd_ref,
             dh1_ref, a_ref, dap_ref, u2_ref, dh2_ref, loss_ref, dgm_ref, dgf_ref):
        i = pl.program_id(0)

        @pl.when(i == 0)
        def _():
            loss_ref[...] = jnp.zeros_like(loss_ref)
            dgm_ref[...] = jnp.zeros_like(dgm_ref)
            dgf_ref[...] = jnp.zeros_like(dgf_ref)

        h1 = h1_ref[...]
        r2, xh2, u2 = _rms_fwd(h1, gm_ref[...])
        u2 = u2.astype(BF16)
        u2_ref[...] = u2
        acc = jnp.zeros((tm, D_MODEL), F32)
        for c in range(n_chunks):
            cs = slice(c * FF_CHUNK, (c + 1) * FF_CHUNK)
            a = jnp.maximum(_dot(u2, wu_ref[:, cs]), 0.0)
            a_ref[:, cs] = a.astype(BF16)
            acc = acc + _dot((a * a).astype(BF16), wd_ref[cs, :])
        h2 = h1 + acc
        r3, xh3, y = _rms_fwd(h2, gf_ref[...])
        diff = y - tgt_ref[...]
        loss_ref[...] += 0.5 * jnp.sum(jnp.mean(diff * diff, axis=-1, keepdims=True))
        dy = diff * (1.0 / D_MODEL)
        dh2, dgf = _rms_bwd(dy, xh3, r3, gf_ref[...])
        dgf_ref[...] += dgf
        dh2_bf = dh2.astype(BF16)
        dh2_ref[...] = dh2_bf
        du2 = jnp.zeros((tm, D_MODEL), F32)
        for c in range(n_chunks):
            cs = slice(c * FF_CHUNK, (c + 1) * FF_CHUNK)
            ds = _dot_nt(dh2_bf, wd_ref[cs, :])
            dap = (ds * (2.0 * a_ref[:, cs].astype(F32))).astype(BF16)
            dap_ref[:, cs] = dap
            du2 = du2 + _dot_nt(dap, wu_ref[:, cs])
        dh1n, dgm = _rms_bwd(du2, xh2, r2, gm_ref[...])
        dgm_ref[...] += dgm
        dh1_ref[...] = dh2 + dh1n

    single = dict(pipeline_mode=pl.Buffered(1))
    return pl.pallas_call(
        body, name="mlp_loss", grid=(s_len // tm,),
        in_specs=[_rows(tm, D_MODEL), _rows(tm, D_MODEL), _whole((1, D_MODEL)), _whole((1, D_MODEL)),
                  pl.BlockSpec((D_MODEL, D_FF), lambda i: (0, 0), **single),
                  pl.BlockSpec((D_FF, D_MODEL), lambda i: (0, 0), **single)],
        out_specs=[_rows(tm, D_MODEL), _rows(tm, D_FF), _rows(tm, D_FF), _rows(tm, D_MODEL), _rows(tm, D_MODEL),
                   _whole((8, LANES)), _whole((1, D_MODEL)), _whole((1, D_MODEL))],
        out_shape=[jax.ShapeDtypeStruct((s_len, D_MODEL), F32), jax.ShapeDtypeStruct((s_len, D_FF), BF16),
                   jax.ShapeDtypeStruct((s_len, D_FF), BF16), jax.ShapeDtypeStruct((s_len, D_MODEL), BF16),
                   jax.ShapeDtypeStruct((s_len, D_MODEL), BF16), jax.ShapeDtypeStruct((8, LANES), F32),
                   jax.ShapeDtypeStruct((1, D_MODEL), F32), jax.ShapeDtypeStruct((1, D_MODEL), F32)],
        compiler_params=_params(),
    )(h1, tgt, g_mlp, g_fin, w_up, w_down)


def _tn_matmul(a, b, square_a=False):
    s_len, ka = a.shape
    nb = b.shape[1]
    tt = min(1024, s_len)
    tk = min(1024, ka)
    tn = min(1024, nb)
    n_t = s_len // tt

    def body(a_ref, b_ref, o_ref, acc_ref):
        t = pl.program_id(2)

        @pl.when(t == 0)
        def _():
            acc_ref[...] = jnp.zeros_like(acc_ref)

        av = a_ref[...]
        if square_a:
            av = av * av
        acc_ref[...] += _dot_tn(av.astype(BF16), b_ref[...].astype(BF16))

        @pl.when(t == n_t - 1)
        def _():
            o_ref[...] = acc_ref[...].astype(o_ref.dtype)

    return pl.pallas_call(
        body, name="tn_matmul", grid=(ka // tk, nb // tn, n_t),
        in_specs=[pl.BlockSpec((tt, tk), lambda k, j, t: (t, k)), pl.BlockSpec((tt, tn), lambda k, j, t: (t, j))],
        out_specs=pl.BlockSpec((tk, tn), lambda k, j, t: (k, j)),
        out_shape=jax.ShapeDtypeStruct((ka, nb), BF16),
        scratch_shapes=[pltpu.VMEM((tk, tn), F32)],
        compiler_params=_params(3),
    )(a, b)


def _mix_bwd(dh1, pm, o, zg, p_pool, p_attn, w_out):
    s_len = dh1.shape[0]
    tm = min(512, s_len)

    def body(dh1_ref, pm_ref, o_ref, zg_ref, pp_ref, pa_ref, wo_ref, dyp_ref, dya_ref, dzg_ref, dpm_ref, do_ref):
        dm = _dot_nt(dh1_ref[...].astype(BF16), wo_ref[...])
        yp, ya, gp, ga = _gated_mix(pm_ref[...], o_ref[...], zg_ref[...], pp_ref, pa_ref)
        dyp = (dm * gp).astype(BF16)
        dya = (dm * ga).astype(BF16)
        dyp_ref[...] = dyp
        dya_ref[...] = dya
        dzg_ref[:, :D_MODEL] = (dm * yp * (gp * (1.0 - gp))).astype(BF16)
        dzg_ref[:, D_MODEL:] = (dm * ya * (ga * (1.0 - ga))).astype(BF16)
        dpm_ref[...] = _dot_nt(dyp, pp_ref[...]).astype(BF16)
        do_ref[...] = _dot_nt(dya, pa_ref[...]).astype(BF16)

    return pl.pallas_call(
        body, name="mix_bwd", grid=(s_len // tm,),
        in_specs=[_rows(tm, D_MODEL), _rows(tm, 512), _rows(tm, 512), _rows(tm, 2048),
                  _whole((512, D_MODEL)), _whole((512, D_MODEL)), _whole((D_MODEL, D_MODEL))],
        out_specs=[_rows(tm, D_MODEL), _rows(tm, D_MODEL), _rows(tm, 2048), _rows(tm, 512), _rows(tm, 512)],
        out_shape=[jax.ShapeDtypeStruct((s_len, n), BF16) for n in (D_MODEL, D_MODEL, 2048, 512, 512)],
        compiler_params=_params(),
    )(dh1, pm, o, zg, p_pool, p_attn, w_out)


def _mixers_bwd(zp, q, kv, dpm, do, pool_w, pool_b, pool_scale, bias_t, sink_row):
    s_len = zp.shape[0]
    tq = min(512, s_len)
    nb = tq // BLOCK
    n_steps = s_len // tq

    def body(zp_ref, zph_ref, q_ref, kv_ref, kvh_ref, dpm_ref, dpmh_ref, do_ref, pw_ref, pb_ref, ps_ref,
             bias_ref, sink_ref, dzp_ref, dq_ref, dkv_ref, dpw_ref, dpb_ref, dps_ref, dsk_ref, ka_ref, vd_ref, dsk_acc):
        i = pl.program_id(0)

        @pl.when(i == 0)
        def _():
            dkv_ref[...] = jnp.zeros_like(dkv_ref)
            dpw_ref[...] = jnp.zeros_like(dpw_ref)
            dpb_ref[...] = jnp.zeros_like(dpb_ref)
            dps_ref[...] = jnp.zeros_like(dps_ref)
            dsk_acc[...] = jnp.zeros_like(dsk_acc)

        cur = zp_ref[...].astype(F32)
        halo = zph_ref[...].astype(F32) * (i > 0).astype(F32)
        ext = jnp.concatenate([halo, cur], axis=0)
        dpm_next = dpmh_ref[...].astype(F32) * (i < n_steps - 1).astype(F32)
        dpm_ext = jnp.concatenate([dpm_ref[...].astype(F32), dpm_next], axis=0)
        n_ext = tq + POOL_HALO
        for g in range(4):
            sl = slice(g * POOL_GROUP_DIM, (g + 1) * POOL_GROUP_DIM)
            w = POOL_WINDOWS[g]
            d = _pool_d(ext[:, sl], cur[:, sl], g, i * tq).astype(BF16)
            y_lin = _dot(d, pw_ref[g]) + pb_ref[:, sl]
            dps_ref[:, sl] += jnp.sum(dpm_ext[:tq, sl] * y_lin, axis=0, keepdims=True)
            dyl_ext = dpm_ext[:, sl] * ps_ref[:, sl]
            dpb_ref[:, sl] += jnp.sum(dyl_ext[:tq], axis=0, keepdims=True)
            dyl_bf = dyl_ext.astype(BF16)
            dpw_ref[g] += _dot_tn(d, dyl_bf[:tq])
            dd = _dot_nt(dyl_bf, pw_ref[g])
            t = i * tq + lax.broadcasted_iota(jnp.int32, dd.shape, 0)
            e = dd / jnp.minimum(t + 1, w).astype(F32)
            acc = e
            k = 1
            while k < w:
                acc = acc + pltpu.roll(acc, n_ext - k, 0)
                k *= 2
            dzp_ref[:, sl] = (acc[:tq] - dd[:tq]).astype(BF16)

        _fill_kv_slabs(kvh_ref, kv_ref, ka_ref, vd_ref)

        def fold(dup):
            return dup + pltpu.roll(dup, HEAD_DIM, 1)

        for b in range(nb):
            rq = slice(b * BLOCK, (b + 1) * BLOCK)
            rk = slice(b * BLOCK, (b + 2) * BLOCK)
            qb = q_ref[rq, :]
            dob = do_ref[rq, :]
            dk_dup, dv_dup = [], []
            for h in range(N_KV_HEADS):
                kk = ka_ref[h, rk, :]
                q_st = _stack_pairs(qb, h)
                do_st = _stack_pairs(dob, h)
                pn, psink = _attn_probs_t(kk, q_st, bias_ref[h], sink_ref[h], (i == 0) if b == 0 else None)
                dp = _dot_nt(vd_ref[h, rk, :], do_st)
                delta = jnp.sum(pn * dp, axis=0, keepdims=True)
                dsk_acc[h] += -psink * delta
                ds = ((pn * (dp - delta)) * ATTN_SCALE).astype(BF16)
                dq_st = _dot_tn(ds, kk)
                for j in range(2):
                    left = _left_half((BLOCK, LANES))
                    dq_pair = jnp.where(left, dq_st[(2 * j) * BLOCK:(2 * j + 1) * BLOCK],
                                        dq_st[(2 * j + 1) * BLOCK:(2 * j + 2) * BLOCK])
                    dq_ref[rq, h * 256 + j * LANES:h * 256 + (j + 1) * LANES] = dq_pair.astype(BF16)
                dk_dup.append(fold(_dot(ds, q_st)))
                dv_dup.append(fold(_dot(pn.astype(BF16), do_st)))
            left = _left_half((2 * BLOCK, LANES))
            dkv_blk = jnp.concatenate([jnp.where(left, dk_dup[0], dk_dup[1]),
                                       jnp.where(left, dv_dup[0], dv_dup[1])], axis=1)
            g0 = pl.multiple_of(i * tq + b * BLOCK, BLOCK)
            dkv_ref[pl.ds(g0, 2 * BLOCK), :] += dkv_blk

        @pl.when(i == n_steps - 1)
        def _():
            for h in range(N_KV_HEADS):
                for g in range(GROUP):
                    tot = jnp.sum(dsk_acc[h, :, g * BLOCK:(g + 1) * BLOCK], axis=1, keepdims=True)
                    dsk_ref[GROUP * h + g:GROUP * h + g + 1, :] = jnp.broadcast_to(tot, (1, LANES))

    blocks_per_tile = tq // POOL_HALO
    last_halo = s_len // POOL_HALO - 1
    halo_prev = pl.BlockSpec((POOL_HALO, 512), lambda i: (jnp.maximum(i * blocks_per_tile - 1, 0), 0))
    halo_next = pl.BlockSpec((POOL_HALO, 512), lambda i: (jnp.minimum((i + 1) * blocks_per_tile, last_halo), 0))
    halo_kv = pl.BlockSpec((BLOCK, 256), lambda i: (jnp.maximum(i * nb - 1, 0), 0))
    return pl.pallas_call(
        body, name="mixers_bwd", grid=(n_steps,),
        in_specs=[_rows(tq, 512), halo_prev, _rows(tq, 512), _rows(tq, 256), halo_kv,
                  _rows(tq, 512), halo_next, _rows(tq, 512),
                  _whole((4, 128, 128)), _whole((1, 512)), _whole((1, 512)),
                  _whole((N_KV_HEADS, 2 * BLOCK, GROUP * BLOCK)), _whole((N_KV_HEADS, 1, GROUP * BLOCK))],
        out_specs=[_rows(tq, 512), _rows(tq, 512), _whole((s_len + BLOCK, 256)),
                   _whole((4, 128, 128)), _whole((1, 512)), _whole((1, 512)), _whole((8, LANES))],
        out_shape=[jax.ShapeDtypeStruct((s_len, 512), BF16), jax.ShapeDtypeStruct((s_len, 512), BF16),
                   jax.ShapeDtypeStruct((s_len + BLOCK, 256), F32), jax.ShapeDtypeStruct((4, 128, 128), F32),
                   jax.ShapeDtypeStruct((1, 512), F32), jax.ShapeDtypeStruct((1, 512), F32),
                   jax.ShapeDtypeStruct((8, LANES), F32)],
        scratch_shapes=[pltpu.VMEM((N_KV_HEADS, tq + BLOCK, LANES), BF16)] * 2
        + [pltpu.VMEM((N_KV_HEADS, 1, GROUP * BLOCK), F32)],
        compiler_params=_params(),
    )(zp, zp, q, kv, kv, dpm, dpm, do, pool_w, pool_b, pool_scale, bias_t, sink_row)


def _in_bwd(dzp, dq, dkv, dzg, w_in, x, dh1, g_mix):
    s_len = x.shape[0]
    tm = min(512, s_len)

    def body(dzp_ref, dq_ref, dkv_ref, dzg_ref, w_ref, x_ref, dh1_ref, g_ref, dx_ref, dg_ref):
        i = pl.program_id(0)

        @pl.when(i == 0)
        def _():
            dg_ref[...] = jnp.zeros_like(dg_ref)

        du = _dot_nt(dzp_ref[...], w_ref[:, 0:512])
        du = du + _dot_nt(dq_ref[...], w_ref[:, 512:1024])
        du = du + _dot_nt(dkv_ref[...], w_ref[:, 1024:1280])
        du = du + _dot_nt(dzg_ref[...], w_ref[:, 1280:3328])
        r, xh, _ = _rms_fwd(x_ref[...], g_ref[...])
        dxn, dg = _rms_bwd(du, xh, r, g_ref[...])
        dg_ref[...] += dg
        dx_ref[...] = dh1_ref[...] + dxn

    return pl.pallas_call(
        body, name="in_bwd", grid=(s_len // tm,),
        in_specs=[_rows(tm, 512), _rows(tm, 512), _rows(tm, 256), _rows(tm, 2048), _whole((D_MODEL, IN_WIDTH)),
                  _rows(tm, D_MODEL), _rows(tm, D_MODEL), _whole((1, D_MODEL))],
        out_specs=[_rows(tm, D_MODEL), _whole((1, D_MODEL))],
        out_shape=[jax.ShapeDtypeStruct((s_len, D_MODEL), F32), jax.ShapeDtypeStruct((1, D_MODEL), F32)],
        compiler_params=_params(),
    )(dzp, dq, dkv, dzg, w_in, x, dh1, g_mix)


ANY = pl.BlockSpec(memory_space=pl.ANY)


def _all_gather_weights(shards):
    n = len(shards)

    def body(*refs):
        ins, outs = refs[:n], refs[n:2 * n]
        send_sems, recv_sems, local_sems = refs[2 * n:]
        x, y, c = lax.axis_index("x"), lax.axis_index("y"), lax.axis_index("c")
        me, sibling = (x, y, c), (x, y, 1 - c)
        chips = [(1 - x, y), (x, 1 - y), (1 - x, 1 - y)]

        def slot(a, px, py, pc):
            return outs[a].at[4 * px + 2 * py + pc]

        def copy(a, k, block, to, src=None):
            return pltpu.make_async_remote_copy(
                src_ref=slot(a, *block) if src is None else src, dst_ref=slot(a, *block),
                send_sem=send_sems.at[a, k], recv_sem=recv_sems.at[a, k], device_id=to, device_id_type=MESH)

        mine = [pltpu.make_async_copy(ins[a], slot(a, *me), local_sems.at[a]) for a in range(n)]
        for cp in mine:
            cp.start()
        first = []
        for a in range(n):
            first.append(copy(a, 0, me, sibling, src=ins[a]))
            first += [copy(a, 1 + j, me, (*chip, c), src=ins[a]) for j, chip in enumerate(chips)]
        for cp in first:
            cp.start()
        passed = []
        for a in range(n):
            for j, chip in enumerate(chips):
                copy(a, 1 + j, (*chip, c), me).wait_recv()
                cp = copy(a, 4 + j, (*chip, c), sibling)
                cp.start()
                passed.append(cp)
        for a in range(n):
            copy(a, 0, sibling, me).wait_recv()
            for j, chip in enumerate(chips):
                copy(a, 4 + j, (*chip, 1 - c), me).wait_recv()
        for cp in first + passed:
            cp.wait_send()
        for cp in mine:
            cp.wait()

    return pl.pallas_call(
        body, name="all_gather_weights",
        in_specs=[ANY] * n, out_specs=[ANY] * n,
        out_shape=[jax.ShapeDtypeStruct((N_DEV,) + s.shape, s.dtype) for s in shards],
        scratch_shapes=[pltpu.SemaphoreType.DMA((n, 7)), pltpu.SemaphoreType.DMA((n, 7)), pltpu.SemaphoreType.DMA((n,))],
    )(*shards)


def _exchange_grads(sends, small):
    n = len(sends)

    def body(*refs):
        ins, small_ref = refs[:n], refs[n]
        outs, small_out = refs[n + 1:2 * n + 1], refs[2 * n + 1]
        send_sems, recv_sems, local_sems = refs[2 * n + 2:]
        x, y, c = lax.axis_index("x"), lax.axis_index("y"), lax.axis_index("c")
        me_idx = 4 * x + 2 * y + c
        local = [pltpu.make_async_copy(ins[a].at[me_idx], outs[a].at[0], local_sems.at[a]) for a in range(n)]
        local.append(pltpu.make_async_copy(small_ref, small_out.at[me_idx], local_sems.at[n]))
        for cp in local:
            cp.start()
        copies = []
        for k in range(1, N_DEV):
            kx, ky, kc = (k >> 2) & 1, (k >> 1) & 1, k & 1
            px = 1 - x if kx else x
            py = 1 - y if ky else y
            pc = 1 - c if kc else c
            p_idx = 4 * px + 2 * py + pc
            for a in range(n):
                copies.append(pltpu.make_async_remote_copy(
                    src_ref=ins[a].at[p_idx], dst_ref=outs[a].at[k], send_sem=send_sems.at[a, k - 1],
                    recv_sem=recv_sems.at[a, k - 1], device_id=(px, py, pc), device_id_type=MESH))
            copies.append(pltpu.make_async_remote_copy(
                src_ref=small_ref, dst_ref=small_out.at[me_idx], send_sem=send_sems.at[n, k - 1],
                recv_sem=recv_sems.at[n, k - 1], device_id=(px, py, pc), device_id_type=MESH))
        for cp in copies:
            cp.start()
        for cp in copies:
            cp.wait_send()
        for k in range(1, N_DEV):
            kx, ky, kc = (k >> 2) & 1, (k >> 1) & 1, k & 1
            px = 1 - x if kx else x
            py = 1 - y if ky else y
            pc = 1 - c if kc else c
            p_idx = 4 * px + 2 * py + pc
            for a in range(n):
                pltpu.make_async_remote_copy(
                    src_ref=ins[a].at[p_idx], dst_ref=outs[a].at[k], send_sem=send_sems.at[a, k - 1],
                    recv_sem=recv_sems.at[a, k - 1], device_id=(px, py, pc), device_id_type=MESH).wait_recv()
            pltpu.make_async_remote_copy(
                src_ref=small_ref, dst_ref=small_out.at[p_idx], send_sem=send_sems.at[n, k - 1],
                recv_sem=recv_sems.at[n, k - 1], device_id=(px, py, pc), device_id_type=MESH).wait_recv()
        for cp in local:
            cp.wait()

    return pl.pallas_call(
        body, name="exchange_grads",
        in_specs=[ANY] * (n + 1), out_specs=[ANY] * (n + 1),
        out_shape=[jax.ShapeDtypeStruct(s.shape, s.dtype) for s in sends]
        + [jax.ShapeDtypeStruct((N_DEV,) + small.shape, small.dtype)],
        scratch_shapes=[pltpu.SemaphoreType.DMA((n + 1, 7)), pltpu.SemaphoreType.DMA((n + 1, 7)),
                        pltpu.SemaphoreType.DMA((n + 1,))],
    )(*sends, small)


def _adamw(parts, w, m, v):
    r, c = w.shape
    tr = 256 if r % 256 == 0 else r

    def body(p_ref, w_ref, m_ref, v_ref, g_ref, d_ref, nm_ref, nv_ref):
        g = p_ref[0].astype(F32)
        for k in range(1, N_DEV):
            g = g + p_ref[k].astype(F32)
        m_new = ADAM_B1 * m_ref[...] + (1.0 - ADAM_B1) * g
        v_new = ADAM_B2 * v_ref[...] + (1.0 - ADAM_B2) * (g * g)
        m_hat = m_new / (1.0 - ADAM_B1 ** ADAM_STEP)
        v_hat = v_new / (1.0 - ADAM_B2 ** ADAM_STEP)
        g_ref[...] = g
        d_ref[...] = -ADAM_LR * (m_hat / (jnp.sqrt(v_hat) + ADAM_EPS) + ADAM_WD * w_ref[...])
        nm_ref[...] = m_new
        nv_ref[...] = v_new

    return pl.pallas_call(
        body, name="adamw", grid=(r // tr,),
        in_specs=[pl.BlockSpec((N_DEV, tr, c), lambda i: (0, i, 0))] + [_rows(tr, c)] * 3,
        out_specs=[_rows(tr, c)] * 4,
        out_shape=[jax.ShapeDtypeStruct((r, c), F32)] * 4,
        compiler_params=_params(),
    )(parts, w, m, v)


SMALL_LAYOUT = (("norm_mix", 8), ("pool_w", 512), ("pool_b", 8), ("pool_scale", 8), ("attn_sinks", 8),
                ("norm_mlp", 8), ("norm_final", 8), ("loss", 8))
SMALL_ROWS = sum(n for _, n in SMALL_LAYOUT)


def _pack_small(vals):
    rows = []
    for name, n_rows in SMALL_LAYOUT:
        flat = vals[name].astype(F32).reshape(-1)
        flat = jnp.pad(flat, (0, n_rows * LANES - flat.shape[0]))
        rows.append(flat.reshape(n_rows, LANES))
    return jnp.concatenate(rows, axis=0)


def _unpack_small(pack, shapes):
    out, r0 = {}, 0
    for name, n_rows in SMALL_LAYOUT:
        size = int(np.prod(shapes[name])) if shapes[name] else 1
        out[name] = pack[r0:r0 + n_rows].reshape(-1)[:size].reshape(shapes[name])
        r0 += n_rows
    return out


def kernel(x, norm_mix, w_in, pool_w, pool_b, pool_scale, attn_sinks, p_pool, p_attn, w_out, norm_mlp, w_up, w_down, norm_final, loss_target, m_norm_mix, m_w_in, m_pool_w, m_pool_b, m_pool_scale, m_attn_sinks, m_p_pool, m_p_attn, m_w_out, m_norm_mlp, m_w_up, m_w_down, m_norm_final, v_norm_mix, v_w_in, v_pool_w, v_pool_b, v_pool_scale, v_attn_sinks, v_p_pool, v_p_attn, v_w_out, v_norm_mlp, v_w_up, v_w_down, v_norm_final):
    xs = x[0]
    tgt = loss_target[0]
    s_len = xs.shape[0]

    shards = [w_in[0], p_pool[0], p_attn[0], w_out[0], w_up[0], w_down[0]]
    gathered = _all_gather_weights([s.astype(BF16) for s in shards])
    w_in_f = gathered[0].transpose(1, 0, 2).reshape(D_MODEL, IN_WIDTH)
    p_pool_f = gathered[1].transpose(1, 0, 2).reshape(POOL_WIDTH, D_MODEL)
    p_attn_f = gathered[2].transpose(1, 0, 2).reshape(ATTN_WIDTH, D_MODEL)
    w_out_f = gathered[3].reshape(D_MODEL, D_MODEL)
    w_up_f = gathered[4].transpose(1, 0, 2).reshape(D_MODEL, D_FF)
    w_down_f = gathered[5].reshape(D_FF, D_MODEL)

    pool_w_bf = pool_w[0].astype(BF16)
    pool_b_row = pool_b[0].reshape(1, POOL_WIDTH)
    bias_t, sink_row = _attn_constants(attn_sinks[0])

    u, zp, q, kv, zg = _fwd_in(xs, norm_mix, w_in_f)
    pm, o = _mixers_fwd(zp, q, kv, pool_w_bf, pool_b_row, pool_scale, bias_t, sink_row)
    h1, mixed = _mix_out(xs, pm, o, zg, p_pool_f, p_attn_f, w_out_f)
    dh1, a, dapre, u2, dh2, loss_part, g_norm_mlp, g_norm_final = _mlp_loss(
        h1, tgt, norm_mlp, norm_final.reshape(1, D_MODEL), w_up_f, w_down_f)
    gw_down = _tn_matmul(a, dh2, square_a=True)
    gw_up = _tn_matmul(u2, dapre)
    dyp, dya, dzg, dpm, do = _mix_bwd(dh1, pm, o, zg, p_pool_f, p_attn_f, w_out_f)
    gw_out = _tn_matmul(mixed, dh1)
    gp_pool = _tn_matmul(pm, dyp)
    gp_attn = _tn_matmul(o, dya)
    dzp, dq, dkv_acc, g_pool_w, g_pool_b, g_pool_scale, g_sinks = _mixers_bwd(
        zp, q, kv, dpm, do, pool_w_bf, pool_b_row, pool_scale, bias_t, sink_row)
    dkv = dkv_acc[BLOCK:].astype(BF16)
    dx, g_norm_mix = _in_bwd(dzp, dq, dkv, dzg, w_in_f, xs, dh1, norm_mix)
    gw_in = jnp.concatenate(
        [_tn_matmul(u, dzp), _tn_matmul(u, dq), _tn_matmul(u, dkv), _tn_matmul(u, dzg)], axis=1)

    sends = [
        gw_in.reshape(D_MODEL, N_DEV, IN_WIDTH // N_DEV).transpose(1, 0, 2),
        gp_pool.reshape(POOL_WIDTH, N_DEV, D_MODEL // N_DEV).transpose(1, 0, 2),
        gp_attn.reshape(ATTN_WIDTH, N_DEV, D_MODEL // N_DEV).transpose(1, 0, 2),
        gw_out.reshape(N_DEV, D_MODEL // N_DEV, D_MODEL),
        gw_up.reshape(D_MODEL, N_DEV, D_FF // N_DEV).transpose(1, 0, 2),
        gw_down.reshape(N_DEV, D_FF // N_DEV, D_MODEL),
    ]
    small_vals = dict(norm_mix=g_norm_mix, pool_w=g_pool_w, pool_b=g_pool_b, pool_scale=g_pool_scale,
                      attn_sinks=g_sinks[:, 0], norm_mlp=g_norm_mlp, norm_final=g_norm_final, loss=loss_part[0, 0])
    *recvs, small_all = _exchange_grads(sends, _pack_small(small_vals))

    big_names = ["w_in", "p_pool", "p_attn", "w_out", "w_up", "w_down"]
    big_w = dict(w_in=w_in, p_pool=p_pool, p_attn=p_attn, w_out=w_out, w_up=w_up, w_down=w_down)
    big_m = dict(w_in=m_w_in, p_pool=m_p_pool, p_attn=m_p_attn, w_out=m_w_out, w_up=m_w_up, w_down=m_w_down)
    big_v = dict(w_in=v_w_in, p_pool=v_p_pool, p_attn=v_p_attn, w_out=v_w_out, w_up=v_w_up, w_down=v_w_down)
    res = {}
    for name, parts in zip(big_names, recvs):
        outs = _adamw(parts, big_w[name][0], big_m[name][0], big_v[name][0])
        res[name] = [t[None] for t in outs]

    small_names = ["norm_mix", "pool_w", "pool_b", "pool_scale", "attn_sinks", "norm_mlp", "norm_final"]
    small_w = dict(norm_mix=norm_mix, pool_w=pool_w, pool_b=pool_b, pool_scale=pool_scale, attn_sinks=attn_sinks,
                   norm_mlp=norm_mlp, norm_final=norm_final)
    small_m = dict(norm_mix=m_norm_mix, pool_w=m_pool_w, pool_b=m_pool_b, pool_scale=m_pool_scale,
                   attn_sinks=m_attn_sinks, norm_mlp=m_norm_mlp, norm_final=m_norm_final)
    small_v = dict(norm_mix=v_norm_mix, pool_w=v_pool_w, pool_b=v_pool_b, pool_scale=v_pool_scale,
                   attn_sinks=v_attn_sinks, norm_mlp=v_norm_mlp, norm_final=v_norm_final)
    zero = jnp.zeros((), F32)
    packs = [_pack_small({**d, "loss": zero}) for d in (small_w, small_m, small_v)]
    s_outs = _adamw(small_all, *packs)
    shapes = {k: small_w[k].shape for k in small_names}
    shapes["loss"] = ()
    s_res = [_unpack_small(t, shapes) for t in s_outs]
    loss = s_res[0]["loss"]
    for name in small_names:
        res[name] = [t[name] for t in s_res]

    order = ["norm_mix", "w_in", "pool_w", "pool_b", "pool_scale", "attn_sinks", "p_pool", "p_attn", "w_out",
             "norm_mlp", "w_up", "w_down", "norm_final"]
    out = [loss, dx[None]]
    for kind in range(4):
        out += [res[name][kind] for name in order]
    return tuple(out)
```

```python
import functools
import math

import numpy as np
import jax
import jax.numpy as jnp
from jax import lax
from jax.experimental import pallas as pl
from jax.experimental.pallas import tpu as pltpu

F32 = jnp.float32
BF16 = jnp.bfloat16

D_MODEL = 1024
POOL_WIDTH = 512
ATTN_WIDTH = 512
KV_WIDTH = 128
HEAD_DIM = 64
N_HEADS = 8
N_KV_HEADS = 2
GROUP = 4
BLOCK = 128
POOL_WINDOWS = (2, 4, 8, 16)
POOL_GROUP_DIM = 128
POOL_HALO = 16
D_FF = 4096
FF_CHUNK = 1024
IN_WIDTH = 3328
RMS_EPS = 1e-5
NEG_INF = -1e30
ATTN_SCALE = 1.0 / math.sqrt(HEAD_DIM)
N_DEV = 8

ADAM_LR = 0.001
ADAM_B1 = 0.9
ADAM_B2 = 0.999
ADAM_EPS = 1e-08
ADAM_WD = 0.01
ADAM_STEP = 10

LANES = 128
VMEM_LIMIT_BYTES = 56 * 1024 * 1024
MESH = pl.DeviceIdType.MESH


def _params(n_grid_axes=1):
    return pltpu.CompilerParams(
        dimension_semantics=("arbitrary",) * n_grid_axes, vmem_limit_bytes=VMEM_LIMIT_BYTES)


def _dot(a, b):
    return jnp.dot(a, b, preferred_element_type=F32)


def _dot_nt(a, b):
    return lax.dot_general(a, b, (((1,), (1,)), ((), ())), preferred_element_type=F32)


def _dot_tn(a, b):
    return lax.dot_general(a, b, (((0,), (0,)), ((), ())), preferred_element_type=F32)


def _rows(tm, n):
    return pl.BlockSpec((tm, n), lambda i: (i, 0))


def _whole(shape):
    zeros = (0,) * len(shape)
    return pl.BlockSpec(shape, lambda i: zeros)


def _rms_fwd(h, g):
    r = lax.rsqrt(jnp.mean(h * h, axis=-1, keepdims=True) + RMS_EPS)
    xh = h * r
    return r, xh, xh * g


def _rms_bwd(dy, xh, r, g):
    dxh = dy * g
    dh = r * (dxh - xh * jnp.mean(dxh * xh, axis=-1, keepdims=True))
    return dh, jnp.sum(dy * xh, axis=0, keepdims=True)


def _fwd_in(x, g_mix, w_in):
    s_len = x.shape[0]
    tm = min(512, s_len)

    def body(x_ref, g_ref, w_ref, u_ref, zp_ref, q_ref, kv_ref, zg_ref):
        _, _, u = _rms_fwd(x_ref[...], g_ref[...])
        u = u.astype(BF16)
        u_ref[...] = u
        zp_ref[...] = _dot(u, w_ref[:, 0:512]).astype(BF16)
        q_ref[...] = _dot(u, w_ref[:, 512:1024]).astype(BF16)
        kv_ref[...] = _dot(u, w_ref[:, 1024:1280]).astype(BF16)
        zg_ref[...] = _dot(u, w_ref[:, 1280:3328]).astype(BF16)

    return pl.pallas_call(
        body, name="fwd_in", grid=(s_len // tm,),
        in_specs=[_rows(tm, D_MODEL), _whole((1, D_MODEL)), _whole((D_MODEL, IN_WIDTH))],
        out_specs=[_rows(tm, D_MODEL), _rows(tm, 512), _rows(tm, 512), _rows(tm, 256), _rows(tm, 2048)],
        out_shape=[jax.ShapeDtypeStruct((s_len, n), BF16) for n in (D_MODEL, 512, 512, 256, 2048)],
        compiler_params=_params(),
    )(x, g_mix, w_in)


def _attn_constants(sinks):
    qi = np.arange(BLOCK)[:, None]
    kj = np.arange(2 * BLOCK)[None, :]
    dist = BLOCK + qi - kj
    valid = (dist >= 0) & (dist < BLOCK)
    slopes = np.array([2.0 ** (-8.0 * (h + 1) / N_HEADS) for h in range(N_HEADS)], dtype=np.float32)
    bias = np.where(valid[None], -slopes[:, None, None] * dist.astype(np.float32)[None], np.float32(NEG_INF))
    bias = bias.astype(np.float32).reshape(N_KV_HEADS, GROUP * BLOCK, 2 * BLOCK).transpose(0, 2, 1)
    sink_row = jnp.repeat(sinks.astype(F32).reshape(N_KV_HEADS, GROUP), BLOCK, axis=1)[:, None, :]
    return jnp.asarray(np.ascontiguousarray(bias)), sink_row


def _left_half(shape):
    return lax.broadcasted_iota(jnp.int32, shape, 1) < HEAD_DIM


def _dup_halves(slab):
    swapped = pltpu.roll(slab, HEAD_DIM, 1)
    left = _left_half(slab.shape)
    return jnp.where(left, slab, swapped), jnp.where(left, swapped, slab)


def _fill_kv_slabs(kvh_ref, kv_ref, ka_ref, vd_ref):
    for rows, src in ((slice(0, BLOCK), kvh_ref), (slice(BLOCK, None), kv_ref)):
        kvf = src[...].astype(F32)
        for ref, lanes in ((ka_ref, slice(0, KV_WIDTH)), (vd_ref, slice(KV_WIDTH, 2 * KV_WIDTH))):
            d0, d1 = _dup_halves(kvf[:, lanes])
            ref[0, rows, :] = d0.astype(BF16)
            ref[1, rows, :] = d1.astype(BF16)


def _stack_pairs(a, h):
    pieces = []
    for j in range(2):
        pair = a[:, h * 256 + j * LANES:h * 256 + (j + 1) * LANES]
        left = _left_half(pair.shape)
        zero = jnp.zeros_like(pair)
        pieces += [jnp.where(left, pair, zero), jnp.where(left, zero, pair)]
    return jnp.concatenate(pieces, axis=0)


def _attn_probs_t(kk, q_st, bias_t, sink_row, first):
    s = _dot_nt(kk, q_st) * ATTN_SCALE + bias_t
    if first is not None:
        row = lax.broadcasted_iota(jnp.int32, s.shape, 0)
        s = jnp.where(jnp.logical_and(first, row < BLOCK), NEG_INF, s)
    m = jnp.maximum(jnp.max(s, axis=0, keepdims=True), sink_row)
    p = jnp.exp(s - m)
    es = jnp.exp(sink_row - m)
    inv = 1.0 / (jnp.sum(p, axis=0, keepdims=True) + es)
    return p * inv, es * inv


def _pool_d(ext, cur, g, row0):
    w = POOL_WINDOWS[g]
    acc = ext
    k = 1
    while k < w:
        acc = acc + pltpu.roll(acc, k, 0)
        k *= 2
    t = row0 + lax.broadcasted_iota(jnp.int32, cur.shape, 0)
    cnt = jnp.minimum(t + 1, w).astype(F32)
    return acc[POOL_HALO:, :] / cnt - cur


def _mixers_fwd(zp, q, kv, pool_w, pool_b, pool_scale, bias_t, sink_row):
    s_len = zp.shape[0]
    tq = min(512, s_len)
    nb = tq // BLOCK

    def body(zp_ref, zph_ref, q_ref, kv_ref, kvh_ref, pw_ref, pb_ref, ps_ref, bias_ref, sink_ref,
             pm_ref, o_ref, ka_ref, vd_ref):
        i = pl.program_id(0)
        cur = zp_ref[...].astype(F32)
        halo = zph_ref[...].astype(F32) * (i > 0).astype(F32)
        ext = jnp.concatenate([halo, cur], axis=0)
        for g in range(4):
            sl = slice(g * POOL_GROUP_DIM, (g + 1) * POOL_GROUP_DIM)
            d = _pool_d(ext[:, sl], cur[:, sl], g, i * tq)
            y = _dot(d.astype(BF16), pw_ref[g]) + pb_ref[:, sl]
            pm_ref[:, sl] = (y * ps_ref[:, sl]).astype(BF16)
        _fill_kv_slabs(kvh_ref, kv_ref, ka_ref, vd_ref)
        for b in range(nb):
            rq = slice(b * BLOCK, (b + 1) * BLOCK)
            rk = slice(b * BLOCK, (b + 2) * BLOCK)
            qb = q_ref[rq, :]
            for h in range(N_KV_HEADS):
                pn, _ = _attn_probs_t(ka_ref[h, rk, :], _stack_pairs(qb, h), bias_ref[h], sink_ref[h],
                                      (i == 0) if b == 0 else None)
                pn = pn.astype(BF16)
                vd = vd_ref[h, rk, :]
                left = _left_half(vd.shape)
                zero = jnp.zeros_like(vd)
                va, vb = jnp.where(left, vd, zero), jnp.where(left, zero, vd)
                for j in range(2):
                    o_pair = (_dot_tn(pn[:, (2 * j) * BLOCK:(2 * j + 1) * BLOCK], va)
                              + _dot_tn(pn[:, (2 * j + 1) * BLOCK:(2 * j + 2) * BLOCK], vb))
                    o_ref[rq, h * 256 + j * LANES:h * 256 + (j + 1) * LANES] = o_pair.astype(BF16)

    halo_pool = pl.BlockSpec((POOL_HALO, 512), lambda i: (jnp.maximum(i * (tq // POOL_HALO) - 1, 0), 0))
    halo_kv = pl.BlockSpec((BLOCK, 256), lambda i: (jnp.maximum(i * nb - 1, 0), 0))
    return pl.pallas_call(
        body, name="mixers_fwd", grid=(s_len // tq,),
        in_specs=[_rows(tq, 512), halo_pool, _rows(tq, 512), _rows(tq, 256), halo_kv,
                  _whole((4, 128, 128)), _whole((1, 512)), _whole((1, 512)),
                  _whole((N_KV_HEADS, 2 * BLOCK, GROUP * BLOCK)), _whole((N_KV_HEADS, 1, GROUP * BLOCK))],
        out_specs=[_rows(tq, 512), _rows(tq, 512)],
        out_shape=[jax.ShapeDtypeStruct((s_len, 512), BF16)] * 2,
        scratch_shapes=[pltpu.VMEM((N_KV_HEADS, tq + BLOCK, LANES), BF16)] * 2,
        compiler_params=_params(),
    )(zp, zp, q, kv, kv, pool_w, pool_b, pool_scale, bias_t, sink_row)


def _gated_mix(pm, o, zg, pp_ref, pa_ref):
    yp = _dot(pm, pp_ref[...])
    ya = _dot(o, pa_ref[...])
    gp = jax.nn.sigmoid(zg[:, :D_MODEL].astype(F32))
    ga = jax.nn.sigmoid(zg[:, D_MODEL:].astype(F32))
    return yp, ya, gp, ga


def _mix_out(x, pm, o, zg, p_pool, p_attn, w_out):
    s_len = x.shape[0]
    tm = min(512, s_len)

    def body(x_ref, pm_ref, o_ref, zg_ref, pp_ref, pa_ref, wo_ref, h1_ref, mixed_ref):
        yp, ya, gp, ga = _gated_mix(pm_ref[...], o_ref[...], zg_ref[...], pp_ref, pa_ref)
        mixed = (gp * yp + ga * ya).astype(BF16)
        mixed_ref[...] = mixed
        h1_ref[...] = x_ref[...] + _dot(mixed, wo_ref[...])

    return pl.pallas_call(
        body, name="mix_out", grid=(s_len // tm,),
        in_specs=[_rows(tm, D_MODEL), _rows(tm, 512), _rows(tm, 512), _rows(tm, 2048),
                  _whole((512, D_MODEL)), _whole((512, D_MODEL)), _whole((D_MODEL, D_MODEL))],
        out_specs=[_rows(tm, D_MODEL), _rows(tm, D_MODEL)],
        out_shape=[jax.ShapeDtypeStruct((s_len, D_MODEL), F32), jax.ShapeDtypeStruct((s_len, D_MODEL), BF16)],
        compiler_params=_params(),
    )(x, pm, o, zg, p_pool, p_attn, w_out)


def _mlp_loss(h1, tgt, g_mlp, g_fin, w_up, w_down):
    s_len = h1.shape[0]
    tm = min(256, s_len)
    n_chunks = D_FF // FF_CHUNK

    def body(h1_ref, tgt_ref, gm_ref, gf_ref, wu_ref, wd_ref,
             dh1_ref, a_ref, dap_ref, u2_ref, dh2_ref, loss_ref, dgm_ref, dgf_ref):
        i = pl.program_id(0)

        @pl.when(i == 0)
        def _():
            loss_ref[...] = jnp.zeros_like(loss_ref)
            dgm_ref[...] = jnp.zeros_like(dgm_ref)
            dgf_ref[...] = jnp.zeros_like(dgf_ref)

        h1 = h1_ref[...]
        r2, xh2, u2 = _rms_fwd(h1, gm_ref[...])
        u2 = u2.astype(BF16)
        u2_ref[...] = u2
        acc = jnp.zeros((tm, D_MODEL), F32)
        for c in range(n_chunks):
            cs = slice(c * FF_CHUNK, (c + 1) * FF_CHUNK)
            a = jnp.maximum(_dot(u2, wu_ref[:, cs]), 0.0)
            a_ref[:, cs] = a.astype(BF16)
            acc = acc + _dot((a * a).astype(BF16), wd_ref[cs, :])
        h2 = h1 + acc
        r3, xh3, y = _rms_fwd(h2, gf_ref[...])
        diff = y - tgt_ref[...]
        loss_ref[...] += 0.5 * jnp.sum(jnp.mean(diff * diff, axis=-1, keepdims=True))
        dy = diff * (1.0 / D_MODEL)
        dh2, dgf = _rms_bwd(dy, xh3, r3, gf_ref[...])
        dgf_ref[...] += dgf
        dh2_bf = dh2.astype(BF16)
        dh2_ref[...] = dh2_bf
        du2 = jnp.zeros((tm, D_MODEL), F32)
        for c in range(n_chunks):
            cs = slice(c * FF_CHUNK, (c + 1) * FF_CHUNK)
            ds = _dot_nt(dh2_bf, wd_ref[cs, :])
            dap = (ds * (2.0 * a_ref[:, cs].astype(F32))).astype(BF16)
            dap_ref[:, cs] = dap
            du2 = du2 + _dot_nt(dap, wu_ref[:, cs])
        dh1n, dgm = _rms_bwd(du2, xh2, r2, gm_ref[...])
        dgm_ref[...] += dgm
        dh1_ref[...] = dh2 + dh1n

    single = dict(pipeline_mode=pl.Buffered(1))
    return pl.pallas_call(
        body, name="mlp_loss", grid=(s_len // tm,),
        in_specs=[_rows(tm, D_MODEL), _rows(tm, D_MODEL), _whole((1, D_MODEL)), _whole((1, D_MODEL)),
                  pl.BlockSpec((D_MODEL, D_FF), lambda i: (0, 0), **single),
                  pl.BlockSpec((D_FF, D_MODEL), lambda i: (0, 0), **single)],
        out_specs=[_rows(tm, D_MODEL), _rows(tm, D_FF), _rows(tm, D_FF), _rows(tm, D_MODEL), _rows(tm, D_MODEL),
                   _whole((8, LANES)), _whole((1, D_MODEL)), _whole((1, D_MODEL))],
        out_shape=[jax.ShapeDtypeStruct((s_len, D_MODEL), F32), jax.ShapeDtypeStruct((s_len, D_FF), BF16),
                   jax.ShapeDtypeStruct((s_len, D_FF), BF16), jax.ShapeDtypeStruct((s_len, D_MODEL), BF16),
                   jax.ShapeDtypeStruct((s_len, D_MODEL), BF16), jax.ShapeDtypeStruct((8, LANES), F32),
                   jax.ShapeDtypeStruct((1, D_MODEL), F32), jax.ShapeDtypeStruct((1, D_MODEL), F32)],
        compiler_params=_params(),
    )(h1, tgt, g_mlp, g_fin, w_up, w_down)


def _tn_matmul(a, b, square_a=False):
    s_len, ka = a.shape
    nb = b.shape[1]
    tt = min(1024, s_len)
    tk = min(1024, ka)
    tn = min(1024, nb)
    n_t = s_len // tt

    def body(a_ref, b_ref, o_ref, acc_ref):
        t = pl.program_id(2)

        @pl.when(t == 0)
        def _():
            acc_ref[...] = jnp.zeros_like(acc_ref)

        av = a_ref[...]
        if square_a:
            av = av * av
        acc_ref[...] += _dot_tn(av.astype(BF16), b_ref[...].astype(BF16))

        @pl.when(t == n_t - 1)
        def _():
            o_ref[...] = acc_ref[...].astype(o_ref.dtype)

    return pl.pallas_call(
        body, name="tn_matmul", grid=(ka // tk, nb // tn, n_t),
        in_specs=[pl.BlockSpec((tt, tk), lambda k, j, t: (t, k)), pl.BlockSpec((tt, tn), lambda k, j, t: (t, j))],
        out_specs=pl.BlockSpec((tk, tn), lambda k, j, t: (k, j)),
        out_shape=jax.ShapeDtypeStruct((ka, nb), BF16),
        scratch_shapes=[pltpu.VMEM((tk, tn), F32)],
        compiler_params=_params(3),
    )(a, b)


def _mix_bwd(dh1, pm, o, zg, p_pool, p_attn, w_out):
    s_len = dh1.shape[0]
    tm = min(512, s_len)

    def body(dh1_ref, pm_ref, o_ref, zg_ref, pp_ref, pa_ref, wo_ref, dyp_ref, dya_ref, dzg_ref, dpm_ref, do_ref):
        dm = _dot_nt(dh1_ref[...].astype(BF16), wo_ref[...])
        yp, ya, gp, ga = _gated_mix(pm_ref[...], o_ref[...], zg_ref[...], pp_ref, pa_ref)
        dyp = (dm * gp).astype(BF16)
        dya = (dm * ga).astype(BF16)
        dyp_ref[...] = dyp
        dya_ref[...] = dya
        dzg_ref[:, :D_MODEL] = (dm * yp * (gp * (1.0 - gp))).astype(BF16)
        dzg_ref[:, D_MODEL:] = (dm * ya * (ga * (1.0 - ga))).astype(BF16)
        dpm_ref[...] = _dot_nt(dyp, pp_ref[...]).astype(BF16)
        do_ref[...] = _dot_nt(dya, pa_ref[...]).astype(BF16)

    return pl.pallas_call(
        body, name="mix_bwd", grid=(s_len // tm,),
        in_specs=[_rows(tm, D_MODEL), _rows(tm, 512), _rows(tm, 512), _rows(tm, 2048),
                  _whole((512, D_MODEL)), _whole((512, D_MODEL)), _whole((D_MODEL, D_MODEL))],
        out_specs=[_rows(tm, D_MODEL), _rows(tm, D_MODEL), _rows(tm, 2048), _rows(tm, 512), _rows(tm, 512)],
        out_shape=[jax.ShapeDtypeStruct((s_len, n), BF16) for n in (D_MODEL, D_MODEL, 2048, 512, 512)],
        compiler_params=_params(),
    )(dh1, pm, o, zg, p_pool, p_attn, w_out)


def _mixers_bwd(zp, q, kv, dpm, do, pool_w, pool_b, pool_scale, bias_t, sink_row):
    s_len = zp.shape[0]
    tq = min(512, s_len)
    nb = tq // BLOCK
    n_steps = s_len // tq

    def body(zp_ref, zph_ref, q_ref, kv_ref, kvh_ref, dpm_ref, dpmh_ref, do_ref, pw_ref, pb_ref, ps_ref,
             bias_ref, sink_ref, dzp_ref, dq_ref, dkv_ref, dpw_ref, dpb_ref, dps_ref, dsk_ref, ka_ref, vd_ref, dsk_acc):
        i = pl.program_id(0)

        @pl.when(i == 0)
        def _():
            dkv_ref[...] = jnp.zeros_like(dkv_ref)
            dpw_ref[...] = jnp.zeros_like(dpw_ref)
            dpb_ref[...] = jnp.zeros_like(dpb_ref)
            dps_ref[...] = jnp.zeros_like(dps_ref)
            dsk_acc[...] = jnp.zeros_like(dsk_acc)

        cur = zp_ref[...].astype(F32)
        halo = zph_ref[...].astype(F32) * (i > 0).astype(F32)
        ext = jnp.concatenate([halo, cur], axis=0)
        dpm_next = dpmh_ref[...].astype(F32) * (i < n_steps - 1).astype(F32)
        dpm_ext = jnp.concatenate([dpm_ref[...].astype(F32), dpm_next], axis=0)
        n_ext = tq + POOL_HALO
        for g in range(4):
            sl = slice(g * POOL_GROUP_DIM, (g + 1) * POOL_GROUP_DIM)
            w = POOL_WINDOWS[g]
            d = _pool_d(ext[:, sl], cur[:, sl], g, i * tq).astype(BF16)
            y_lin = _dot(d, pw_ref[g]) + pb_ref[:, sl]
            dps_ref[:, sl] += jnp.sum(dpm_ext[:tq, sl] * y_lin, axis=0, keepdims=True)
            dyl_ext = dpm_ext[:, sl] * ps_ref[:, sl]
            dpb_ref[:, sl] += jnp.sum(dyl_ext[:tq], axis=0, keepdims=True)
            dyl_bf = dyl_ext.astype(BF16)
            dpw_ref[g] += _dot_tn(d, dyl_bf[:tq])
            dd = _dot_nt(dyl_bf, pw_ref[g])
            t = i * tq + lax.broadcasted_iota(jnp.int32, dd.shape, 0)
            e = dd / jnp.minimum(t + 1, w).astype(F32)
            acc = e
            k = 1
            while k < w:
                acc = acc + pltpu.roll(acc, n_ext - k, 0)
                k *= 2
            dzp_ref[:, sl] = (acc[:tq] - dd[:tq]).astype(BF16)

        _fill_kv_slabs(kvh_ref, kv_ref, ka_ref, vd_ref)

        def fold(dup):
            return dup + pltpu.roll(dup, HEAD_DIM, 1)

        for b in range(nb):
            rq = slice(b * BLOCK, (b + 1) * BLOCK)
            rk = slice(b * BLOCK, (b + 2) * BLOCK)
            qb = q_ref[rq, :]
            dob = do_ref[rq, :]
            dk_dup, dv_dup = [], []
            for h in range(N_KV_HEADS):
                kk = ka_ref[h, rk, :]
                q_st = _stack_pairs(qb, h)
                do_st = _stack_pairs(dob, h)
                pn, psink = _attn_probs_t(kk, q_st, bias_ref[h], sink_ref[h], (i == 0) if b == 0 else None)
                dp = _dot_nt(vd_ref[h, rk, :], do_st)
                delta = jnp.sum(pn * dp, axis=0, keepdims=True)
                dsk_acc[h] += -psink * delta
                ds = ((pn * (dp - delta)) * ATTN_SCALE).astype(BF16)
                dq_st = _dot_tn(ds, kk)
                for j in range(2):
                    left = _left_half((BLOCK, LANES))
                    dq_pair = jnp.where(left, dq_st[(2 * j) * BLOCK:(2 * j + 1) * BLOCK],
                                        dq_st[(2 * j + 1) * BLOCK:(2 * j + 2) * BLOCK])
                    dq_ref[rq, h * 256 + j * LANES:h * 256 + (j + 1) * LANES] = dq_pair.astype(BF16)
                dk_dup.append(fold(_dot(ds, q_st)))
                dv_dup.append(fold(_dot(pn.astype(BF16), do_st)))
            left = _left_half((2 * BLOCK, LANES))
            dkv_blk = jnp.concatenate([jnp.where(left, dk_dup[0], dk_dup[1]),
                                       jnp.where(left, dv_dup[0], dv_dup[1])], axis=1)
            g0 = pl.multiple_of(i * tq + b * BLOCK, BLOCK)
            dkv_ref[pl.ds(g0, 2 * BLOCK), :] += dkv_blk

        @pl.when(i == n_steps - 1)
        def _():
            for h in range(N_KV_HEADS):
                for g in range(GROUP):
                    tot = jnp.sum(dsk_acc[h, :, g * BLOCK:(g + 1) * BLOCK], axis=1, keepdims=True)
                    dsk_ref[GROUP * h + g:GROUP * h + g + 1, :] = jnp.broadcast_to(tot, (1, LANES))

    blocks_per_tile = tq // POOL_HALO
    last_halo = s_len // POOL_HALO - 1
    halo_prev = pl.BlockSpec((POOL_HALO, 512), lambda i: (jnp.maximum(i * blocks_per_tile - 1, 0), 0))
    halo_next = pl.BlockSpec((POOL_HALO, 512), lambda i: (jnp.minimum((i + 1) * blocks_per_tile, last_halo), 0))
    halo_kv = pl.BlockSpec((BLOCK, 256), lambda i: (jnp.maximum(i * nb - 1, 0), 0))
    return pl.pallas_call(
        body, name="mixers_bwd", grid=(n_steps,),
        in_specs=[_rows(tq, 512), halo_prev, _rows(tq, 512), _rows(tq, 256), halo_kv,
                  _rows(tq, 512), halo_next, _rows(tq, 512),
                  _whole((4, 128, 128)), _whole((1, 512)), _whole((1, 512)),
                  _whole((N_KV_HEADS, 2 * BLOCK, GROUP * BLOCK)), _whole((N_KV_HEADS, 1, GROUP * BLOCK))],
        out_specs=[_rows(tq, 512), _rows(tq, 512), _whole((s_len + BLOCK, 256)),
                   _whole((4, 128, 128)), _whole((1, 512)), _whole((1, 512)), _whole((8, LANES))],
        out_shape=[jax.ShapeDtypeStruct((s_len, 512), BF16), jax.ShapeDtypeStruct((s_len, 512), BF16),
                   jax.ShapeDtypeStruct((s_len + BLOCK, 256), F32), jax.ShapeDtypeStruct((4, 128, 128), F32),
                   jax.ShapeDtypeStruct((1, 512), F32), jax.ShapeDtypeStruct((1, 512), F32),
                   jax.ShapeDtypeStruct((8, LANES), F32)],
        scratch_shapes=[pltpu.VMEM((N_KV_HEADS, tq + BLOCK, LANES), BF16)] * 2
        + [pltpu.VMEM((N_KV_HEADS, 1, GROUP * BLOCK), F32)],
        compiler_params=_params(),
    )(zp, zp, q, kv, kv, dpm, dpm, do, pool_w, pool_b, pool_scale, bias_t, sink_row)


def _in_bwd(dzp, dq, dkv, dzg, w_in, x, dh1, g_mix):
    s_len = x.shape[0]
    tm = min(512, s_len)

    def body(dzp_ref, dq_ref, dkv_ref, dzg_ref, w_ref, x_ref, dh1_ref, g_ref, dx_ref, dg_ref):
        i = pl.program_id(0)

        @pl.when(i == 0)
        def _():
            dg_ref[...] = jnp.zeros_like(dg_ref)

        du = _dot_nt(dzp_ref[...], w_ref[:, 0:512])
        du = du + _dot_nt(dq_ref[...], w_ref[:, 512:1024])
        du = du + _dot_nt(dkv_ref[...], w_ref[:, 1024:1280])
        du = du + _dot_nt(dzg_ref[...], w_ref[:, 1280:3328])
        r, xh, _ = _rms_fwd(x_ref[...], g_ref[...])
        dxn, dg = _rms_bwd(du, xh, r, g_ref[...])
        dg_ref[...] += dg
        dx_ref[...] = dh1_ref[...] + dxn

    return pl.pallas_call(
        body, name="in_bwd", grid=(s_len // tm,),
        in_specs=[_rows(tm, 512), _rows(tm, 512), _rows(tm, 256), _rows(tm, 2048), _whole((D_MODEL, IN_WIDTH)),
                  _rows(tm, D_MODEL), _rows(tm, D_MODEL), _whole((1, D_MODEL))],
        out_specs=[_rows(tm, D_MODEL), _whole((1, D_MODEL))],
        out_shape=[jax.ShapeDtypeStruct((s_len, D_MODEL), F32), jax.ShapeDtypeStruct((1, D_MODEL), F32)],
        compiler_params=_params(),
    )(dzp, dq, dkv, dzg, w_in, x, dh1, g_mix)


ANY = pl.BlockSpec(memory_space=pl.ANY)


def _all_gather_weights(name, shards):
    n = len(shards)

    def body(*refs):
        ins, outs = refs[:n], refs[n:2 * n]
        send_sems, recv_sems, local_sems = refs[2 * n:]
        x, y, c = lax.axis_index("x"), lax.axis_index("y"), lax.axis_index("c")
        me, sibling = (x, y, c), (x, y, 1 - c)
        chips = [(1 - x, y), (x, 1 - y), (1 - x, 1 - y)]

        def slot(a, px, py, pc):
            return outs[a].at[4 * px + 2 * py + pc]

        def copy(a, k, block, to, src=None):
            return pltpu.make_async_remote_copy(
                src_ref=slot(a, *block) if src is None else src, dst_ref=slot(a, *block),
                send_sem=send_sems.at[a, k], recv_sem=recv_sems.at[a, k], device_id=to, device_id_type=MESH)

        mine = [pltpu.make_async_copy(ins[a], slot(a, *me), local_sems.at[a]) for a in range(n)]
        for cp in mine:
            cp.start()
        first = []
        for a in range(n):
            first.append(copy(a, 0, me, sibling, src=ins[a]))
            first += [copy(a, 1 + j, me, (*chip, c), src=ins[a]) for j, chip in enumerate(chips)]
        for cp in first:
            cp.start()
        passed = []
        for a in range(n):
            for j, chip in enumerate(chips):
                copy(a, 1 + j, (*chip, c), me).wait_recv()
                cp = copy(a, 4 + j, (*chip, c), sibling)
                cp.start()
                passed.append(cp)
        for a in range(n):
            copy(a, 0, sibling, me).wait_recv()
            for j, chip in enumerate(chips):
                copy(a, 4 + j, (*chip, 1 - c), me).wait_recv()
        for cp in first + passed:
            cp.wait_send()
        for cp in mine:
            cp.wait()

    return pl.pallas_call(
        body, name=name,
        in_specs=[ANY] * n, out_specs=[ANY] * n,
        out_shape=[jax.ShapeDtypeStruct((N_DEV,) + s.shape, s.dtype) for s in shards],
        scratch_shapes=[pltpu.SemaphoreType.DMA((n, 7)), pltpu.SemaphoreType.DMA((n, 7)), pltpu.SemaphoreType.DMA((n,))],
    )(*shards)


HBM_SPEC = pl.BlockSpec(memory_space=pltpu.HBM)
SEM_SPEC = pl.BlockSpec(memory_space=pltpu.SEMAPHORE)
DATAFLOW = pltpu.SideEffectType.DATAFLOW_SIDE_EFFECTING
N_PEERS = N_DEV - 1


def _peer_copies(srcs, lands, scatter, send_sems, recv_sems):
    x, y, c = lax.axis_index("x"), lax.axis_index("y"), lax.axis_index("c")
    me_idx = 4 * x + 2 * y + c
    copies = []
    for k in range(1, N_DEV):
        px = 1 - x if (k >> 2) & 1 else x
        py = 1 - y if (k >> 1) & 1 else y
        pc = 1 - c if k & 1 else c
        p_idx = 4 * px + 2 * py + pc
        for a in range(len(srcs)):
            src = srcs[a].at[p_idx] if scatter[a] else srcs[a]
            dst = lands[a].at[k] if scatter[a] else lands[a].at[me_idx]
            copies.append(pltpu.make_async_remote_copy(
                src_ref=src, dst_ref=dst, send_sem=send_sems.at[a * N_PEERS + k - 1],
                recv_sem=recv_sems.at[a * N_PEERS + k - 1],
                device_id=(px, py, pc), device_id_type=MESH))
    return copies


def _exchange_start(name, srcs, scatter):
    n = len(srcs)
    lands = [lax.empty(s.shape if sc else (N_DEV,) + s.shape, s.dtype) for s, sc in zip(srcs, scatter)]

    def body(*refs):
        src_refs, land_refs = refs[:n], refs[n:2 * n]
        send_sems, recv_sems = refs[2 * n], refs[2 * n + 1]
        token, local_sems = refs[4 * n + 2], refs[4 * n + 3]
        me_idx = 4 * lax.axis_index("x") + 2 * lax.axis_index("y") + lax.axis_index("c")
        for cp in _peer_copies(src_refs, land_refs, scatter, send_sems, recv_sems):
            cp.start()
        own = [pltpu.make_async_copy(src_refs[a].at[me_idx], land_refs[a].at[0], local_sems.at[a]) if scatter[a]
               else pltpu.make_async_copy(src_refs[a], land_refs[a].at[me_idx], local_sems.at[a]) for a in range(n)]
        for cp in own:
            cp.start()
        for cp in own:
            cp.wait()
        token[...] = jnp.zeros_like(token)

    hbm = lambda t: pltpu.HBM(t.shape, t.dtype)
    outs = pl.pallas_call(
        body, name=name,
        out_shape=[pltpu.SemaphoreType.DMA((n * N_PEERS,)), pltpu.SemaphoreType.DMA((n * N_PEERS,))]
        + [hbm(t) for t in srcs] + [hbm(t) for t in lands] + [jax.ShapeDtypeStruct((8, LANES), F32)],
        in_specs=[HBM_SPEC] * (2 * n),
        out_specs=[SEM_SPEC, SEM_SPEC] + [HBM_SPEC] * (2 * n) + [pl.BlockSpec(memory_space=pltpu.VMEM)],
        input_output_aliases={i: 2 + i for i in range(2 * n)},
        scratch_shapes=[pltpu.SemaphoreType.DMA((n,))],
        compiler_params=pltpu.CompilerParams(has_side_effects=DATAFLOW),
    )(*[pltpu.with_memory_space_constraint(t, pltpu.HBM) for t in list(srcs) + lands])
    return dict(n=n, scatter=scatter, send_sems=outs[0], recv_sems=outs[1], srcs=outs[2:2 + n],
                lands=outs[2 + n:2 + 2 * n], token=outs[2 + 2 * n])


def _exchange_wait(name, handle, after):
    n, scatter = handle["n"], handle["scatter"]

    def body(*refs):
        src_refs, land_refs = refs[:n], refs[n:2 * n]
        send_sems, recv_sems = refs[2 * n], refs[2 * n + 1]
        for cp in _peer_copies(src_refs, land_refs, scatter, send_sems, recv_sems):
            cp.wait_send()
            cp.wait_recv()

    both = list(handle["srcs"]) + list(handle["lands"])
    outs = pl.pallas_call(
        body, name=name,
        out_shape=[pltpu.HBM(t.shape, t.dtype) for t in both],
        in_specs=[HBM_SPEC] * (2 * n) + [SEM_SPEC, SEM_SPEC, ANY],
        out_specs=[HBM_SPEC] * (2 * n),
        input_output_aliases={i: i for i in range(2 * n)},
        compiler_params=pltpu.CompilerParams(has_side_effects=DATAFLOW),
    )(*both, handle["send_sems"], handle["recv_sems"], after)
    return outs[n:]


def _behind(token, *arrays):
    tied = lax.optimization_barrier((token,) + tuple(arrays))
    return tied[1] if len(arrays) == 1 else tied[1:]


def _adamw(parts, w, m, v):
    r, c = w.shape
    tr = 256 if r % 256 == 0 else r

    def body(p_ref, w_ref, m_ref, v_ref, g_ref, d_ref, nm_ref, nv_ref):
        g = p_ref[0].astype(F32)
        for k in range(1, N_DEV):
            g = g + p_ref[k].astype(F32)
        m_new = ADAM_B1 * m_ref[...] + (1.0 - ADAM_B1) * g
        v_new = ADAM_B2 * v_ref[...] + (1.0 - ADAM_B2) * (g * g)
        m_hat = m_new / (1.0 - ADAM_B1 ** ADAM_STEP)
        v_hat = v_new / (1.0 - ADAM_B2 ** ADAM_STEP)
        g_ref[...] = g
        d_ref[...] = -ADAM_LR * (m_hat / (jnp.sqrt(v_hat) + ADAM_EPS) + ADAM_WD * w_ref[...])
        nm_ref[...] = m_new
        nv_ref[...] = v_new

    return pl.pallas_call(
        body, name="adamw", grid=(r // tr,),
        in_specs=[pl.BlockSpec((N_DEV, tr, c), lambda i: (0, i, 0))] + [_rows(tr, c)] * 3,
        out_specs=[_rows(tr, c)] * 4,
        out_shape=[jax.ShapeDtypeStruct((r, c), F32)] * 4,
        compiler_params=_params(),
    )(parts, w, m, v)


SMALL_LAYOUT = (("norm_mix", 8), ("pool_w", 512), ("pool_b", 8), ("pool_scale", 8), ("attn_sinks", 8),
                ("norm_mlp", 8), ("norm_final", 8), ("loss", 8))
SMALL_ROWS = sum(n for _, n in SMALL_LAYOUT)


def _pack_small(vals, layout=SMALL_LAYOUT):
    rows = []
    for name, n_rows in layout:
        flat = vals[name].astype(F32).reshape(-1)
        flat = jnp.pad(flat, (0, n_rows * LANES - flat.shape[0]))
        rows.append(flat.reshape(n_rows, LANES))
    return jnp.concatenate(rows, axis=0)


def _unpack_small(pack, shapes):
    out, r0 = {}, 0
    for name, n_rows in SMALL_LAYOUT:
        size = int(np.prod(shapes[name])) if shapes[name] else 1
        out[name] = pack[r0:r0 + n_rows].reshape(-1)[:size].reshape(shapes[name])
        r0 += n_rows
    return out


def kernel(x, norm_mix, w_in, pool_w, pool_b, pool_scale, attn_sinks, p_pool, p_attn, w_out, norm_mlp, w_up, w_down, norm_final, loss_target, m_norm_mix, m_w_in, m_pool_w, m_pool_b, m_pool_scale, m_attn_sinks, m_p_pool, m_p_attn, m_w_out, m_norm_mlp, m_w_up, m_w_down, m_norm_final, v_norm_mix, v_w_in, v_pool_w, v_pool_b, v_pool_scale, v_attn_sinks, v_p_pool, v_p_attn, v_w_out, v_norm_mlp, v_w_up, v_w_down, v_norm_final):
    xs = x[0]
    tgt = loss_target[0]
    s_len = xs.shape[0]

    w_in_bf, p_pool_bf, p_attn_bf, w_out_bf, w_up_bf, w_down_bf = [
        t[0].astype(BF16) for t in (w_in, p_pool, p_attn, w_out, w_up, w_down)]
    (w_in_g,) = _all_gather_weights("all_gather_w_in", [w_in_bf])
    p_pool_bf = _behind(w_in_g, p_pool_bf)
    ag_proj = _exchange_start("ag_proj_start", [p_pool_bf, p_attn_bf, w_out_bf], (False,) * 3)
    w_up_bf = _behind(ag_proj["token"], w_up_bf)
    ag_mlp = _exchange_start("ag_mlp_start", [w_up_bf, w_down_bf], (False,) * 2)
    w_in_f = _behind(ag_mlp["token"], w_in_g).transpose(1, 0, 2).reshape(D_MODEL, IN_WIDTH)

    pool_w_bf = pool_w[0].astype(BF16)
    pool_b_row = pool_b[0].reshape(1, POOL_WIDTH)
    bias_t, sink_row = _attn_constants(attn_sinks[0])

    u, zp, q, kv, zg = _fwd_in(xs, norm_mix, w_in_f)
    pm, o = _mixers_fwd(zp, q, kv, pool_w_bf, pool_b_row, pool_scale, bias_t, sink_row)
    p_pool_g, p_attn_g, w_out_g = _exchange_wait("ag_proj_wait", ag_proj, pm)
    p_pool_f = p_pool_g.transpose(1, 0, 2).reshape(POOL_WIDTH, D_MODEL)
    p_attn_f = p_attn_g.transpose(1, 0, 2).reshape(ATTN_WIDTH, D_MODEL)
    w_out_f = w_out_g.reshape(D_MODEL, D_MODEL)
    h1, mixed = _mix_out(xs, pm, o, zg, p_pool_f, p_attn_f, w_out_f)
    w_up_g, w_down_g = _exchange_wait("ag_mlp_wait", ag_mlp, h1)
    w_up_f = w_up_g.transpose(1, 0, 2).reshape(D_MODEL, D_FF)
    w_down_f = w_down_g.reshape(D_FF, D_MODEL)
    dh1, a, dapre, u2, dh2, loss_part, g_norm_mlp, g_norm_final = _mlp_loss(
        h1, tgt, norm_mlp, norm_final.reshape(1, D_MODEL), w_up_f, w_down_f)
    gw_down = _tn_matmul(a, dh2, square_a=True)
    gw_up = _tn_matmul(u2, dapre)
    ex_mlp = _exchange_start(
        "ex_mlp_start", [gw_up.reshape(D_MODEL, N_DEV, D_FF // N_DEV).transpose(1, 0, 2),
                         gw_down.reshape(N_DEV, D_FF // N_DEV, D_MODEL)], (True, True))
    dyp, dya, dzg, dpm, do = _mix_bwd(_behind(ex_mlp["token"], dh1), pm, o, zg, p_pool_f, p_attn_f, w_out_f)
    gw_out = _tn_matmul(mixed, dh1)
    gp_pool = _tn_matmul(pm, dyp)
    gp_attn = _tn_matmul(o, dya)
    ex_proj = _exchange_start(
        "ex_proj_start", [gp_pool.reshape(POOL_WIDTH, N_DEV, D_MODEL // N_DEV).transpose(1, 0, 2),
                          gp_attn.reshape(ATTN_WIDTH, N_DEV, D_MODEL // N_DEV).transpose(1, 0, 2),
                          gw_out.reshape(N_DEV, D_MODEL // N_DEV, D_MODEL)], (True,) * 3)
    dzp, dq, dkv_acc, g_pool_w, g_pool_b, g_pool_scale, g_sinks = _mixers_bwd(
        zp, q, kv, _behind(ex_proj["token"], dpm), do, pool_w_bf, pool_b_row, pool_scale, bias_t, sink_row)
    dkv = dkv_acc[BLOCK:].astype(BF16)
    gw_in = jnp.concatenate(
        [_tn_matmul(u, dzp), _tn_matmul(u, dq), _tn_matmul(u, dkv), _tn_matmul(u, dzg)], axis=1)
    small_vals = dict(pool_w=g_pool_w, pool_b=g_pool_b, pool_scale=g_pool_scale, attn_sinks=g_sinks[:, 0],
                      norm_mlp=g_norm_mlp, norm_final=g_norm_final, loss=loss_part[0, 0])
    ex_in = _exchange_start(
        "ex_in_start", [gw_in.reshape(D_MODEL, N_DEV, IN_WIDTH // N_DEV).transpose(1, 0, 2),
                        _pack_small(small_vals, SMALL_LAYOUT[1:])], (True, False))
    dx, g_norm_mix = _in_bwd(_behind(ex_in["token"], dzp), dq, dkv, dzg, w_in_f, xs, dh1, norm_mix)
    (norm_mix_all,) = _all_gather_weights(
        "all_gather_norm_mix", [_pack_small(dict(norm_mix=g_norm_mix), SMALL_LAYOUT[:1])])
    r_up, r_down = _exchange_wait("ex_mlp_wait", ex_mlp, norm_mix_all)
    r_pool, r_attn, r_out = _exchange_wait("ex_proj_wait", ex_proj, norm_mix_all)
    r_in, small_rest = _exchange_wait("ex_in_wait", ex_in, norm_mix_all)
    recvs = [r_in, r_pool, r_attn, r_out, r_up, r_down]
    small_all = jnp.concatenate([norm_mix_all, small_rest], axis=1)

    big_names = ["w_in", "p_pool", "p_attn", "w_out", "w_up", "w_down"]
    big_w = dict(w_in=w_in, p_pool=p_pool, p_attn=p_attn, w_out=w_out, w_up=w_up, w_down=w_down)
    big_m = dict(w_in=m_w_in, p_pool=m_p_pool, p_attn=m_p_attn, w_out=m_w_out, w_up=m_w_up, w_down=m_w_down)
    big_v = dict(w_in=v_w_in, p_pool=v_p_pool, p_attn=v_p_attn, w_out=v_w_out, w_up=v_w_up, w_down=v_w_down)
    res = {}
    for name, parts in zip(big_names, recvs):
        outs = _adamw(parts, big_w[name][0], big_m[name][0], big_v[name][0])
        res[name] = [t[None] for t in outs]

    small_names = ["norm_mix", "pool_w", "pool_b", "pool_scale", "attn_sinks", "norm_mlp", "norm_final"]
    small_w = dict(norm_mix=norm_mix, pool_w=pool_w, pool_b=pool_b, pool_scale=pool_scale, attn_sinks=attn_sinks,
                   norm_mlp=norm_mlp, norm_final=norm_final)
    small_m = dict(norm_mix=m_norm_mix, pool_w=m_pool_w, pool_b=m_pool_b, pool_scale=m_pool_scale,
                   attn_sinks=m_attn_sinks, norm_mlp=m_norm_mlp, norm_final=m_norm_final)
    small_v = dict(norm_mix=v_norm_mix, pool_w=v_pool_w, pool_b=v_pool_b, pool_scale=v_pool_scale,
                   attn_sinks=v_attn_sinks, norm_mlp=v_norm_mlp, norm_final=v_norm_final)
    zero = jnp.zeros((), F32)
    packs = [_pack_small({**d, "loss": zero}) for d in (small_w, small_m, small_v)]
    s_outs = _adamw(small_all, *packs)
    shapes = {k: small_w[k].shape for k in small_names}
    shapes["loss"] = ()
    s_res = [_unpack_small(t, shapes) for t in s_outs]
    loss = s_res[0]["loss"]
    for name in small_names:
        res[name] = [t[name] for t in s_res]

    order = ["norm_mix", "w_in", "pool_w", "pool_b", "pool_scale", "attn_sinks", "p_pool", "p_attn", "w_out",
             "norm_mlp", "w_up", "w_down", "norm_final"]
    out = [loss, dx[None]]
    for kind in range(4):
        out += [res[name][kind] for name in order]
    return tuple(out)
```

```python
import functools
import math

import numpy as np
import jax
import jax.numpy as jnp
from jax import lax
from jax.experimental import pallas as pl
from jax.experimental.pallas import tpu as pltpu

F32 = jnp.float32
BF16 = jnp.bfloat16

D_MODEL = 1024
POOL_WIDTH = 512
ATTN_WIDTH = 512
KV_WIDTH = 128
HEAD_DIM = 64
N_HEADS = 8
N_KV_HEADS = 2
GROUP = 4
BLOCK = 128
POOL_WINDOWS = (2, 4, 8, 16)
POOL_GROUP_DIM = 128
POOL_HALO = 16
D_FF = 4096
FF_CHUNK = 1024
IN_WIDTH = 3328
RMS_EPS = 1e-5
NEG_INF = -1e30
ATTN_SCALE = 1.0 / math.sqrt(HEAD_DIM)
N_DEV = 8

ADAM_LR = 0.001
ADAM_B1 = 0.9
ADAM_B2 = 0.999
ADAM_EPS = 1e-08
ADAM_WD = 0.01
ADAM_STEP = 10

LANES = 128
VMEM_LIMIT_BYTES = 56 * 1024 * 1024
MESH = pl.DeviceIdType.MESH


def _params(n_grid_axes=1):
    return pltpu.CompilerParams(
        dimension_semantics=("arbitrary",) * n_grid_axes, vmem_limit_bytes=VMEM_LIMIT_BYTES)


def _dot(a, b):
    return jnp.dot(a, b, preferred_element_type=F32)


def _dot_nt(a, b):
    return lax.dot_general(a, b, (((1,), (1,)), ((), ())), preferred_element_type=F32)


def _dot_tn(a, b):
    return lax.dot_general(a, b, (((0,), (0,)), ((), ())), preferred_element_type=F32)


ANY = pl.BlockSpec(memory_space=pl.ANY)


def _rows(tm, n):
    return pl.BlockSpec((tm, n), lambda i: (i, 0))


def _whole(shape):
    zeros = (0,) * len(shape)
    return pl.BlockSpec(shape, lambda i: zeros)


def _rms_fwd(h, g):
    r = lax.rsqrt(jnp.mean(h * h, axis=-1, keepdims=True) + RMS_EPS)
    xh = h * r
    return r, xh, xh * g


def _rms_bwd(dy, xh, r, g):
    dxh = dy * g
    dh = r * (dxh - xh * jnp.mean(dxh * xh, axis=-1, keepdims=True))
    return dh, jnp.sum(dy * xh, axis=0, keepdims=True)


def _fwd_in(after, x, g_mix, w_in):
    s_len = x.shape[0]
    tm = min(512, s_len)

    def body(after_ref, x_ref, g_ref, w_ref, u_ref, zp_ref, q_ref, kv_ref, zg_ref):
        _, _, u = _rms_fwd(x_ref[...], g_ref[...])
        u = u.astype(BF16)
        u_ref[...] = u
        zp_ref[...] = _dot(u, w_ref[:, 0:512]).astype(BF16)
        q_ref[...] = _dot(u, w_ref[:, 512:1024]).astype(BF16)
        kv_ref[...] = _dot(u, w_ref[:, 1024:1280]).astype(BF16)
        zg_ref[...] = _dot(u, w_ref[:, 1280:3328]).astype(BF16)

    return pl.pallas_call(
        body, name="fwd_in", grid=(s_len // tm,),
        in_specs=[ANY, _rows(tm, D_MODEL), _whole((1, D_MODEL)), _whole((D_MODEL, IN_WIDTH))],
        out_specs=[_rows(tm, D_MODEL), _rows(tm, 512), _rows(tm, 512), _rows(tm, 256), _rows(tm, 2048)],
        out_shape=[jax.ShapeDtypeStruct((s_len, n), BF16) for n in (D_MODEL, 512, 512, 256, 2048)],
        compiler_params=_params(),
    )(after, x, g_mix, w_in)


def _attn_constants(sinks):
    qi = np.arange(BLOCK)[:, None]
    kj = np.arange(2 * BLOCK)[None, :]
    dist = BLOCK + qi - kj
    valid = (dist >= 0) & (dist < BLOCK)
    slopes = np.array([2.0 ** (-8.0 * (h + 1) / N_HEADS) for h in range(N_HEADS)], dtype=np.float32)
    bias = np.where(valid[None], -slopes[:, None, None] * dist.astype(np.float32)[None], np.float32(NEG_INF))
    bias = bias.astype(np.float32).reshape(N_KV_HEADS, GROUP * BLOCK, 2 * BLOCK).transpose(0, 2, 1)
    sink_row = jnp.repeat(sinks.astype(F32).reshape(N_KV_HEADS, GROUP), BLOCK, axis=1)[:, None, :]
    return jnp.asarray(np.ascontiguousarray(bias)), sink_row


def _left_half(shape):
    return lax.broadcasted_iota(jnp.int32, shape, 1) < HEAD_DIM


def _dup_halves(slab):
    swapped = pltpu.roll(slab, HEAD_DIM, 1)
    left = _left_half(slab.shape)
    return jnp.where(left, slab, swapped), jnp.where(left, swapped, slab)


def _fill_kv_slabs(kvh_ref, kv_ref, ka_ref, vd_ref):
    for rows, src in ((slice(0, BLOCK), kvh_ref), (slice(BLOCK, None), kv_ref)):
        kvf = src[...].astype(F32)
        for ref, lanes in ((ka_ref, slice(0, KV_WIDTH)), (vd_ref, slice(KV_WIDTH, 2 * KV_WIDTH))):
            d0, d1 = _dup_halves(kvf[:, lanes])
            ref[0, rows, :] = d0.astype(BF16)
            ref[1, rows, :] = d1.astype(BF16)


def _stack_pairs(a, h):
    pieces = []
    for j in range(2):
        pair = a[:, h * 256 + j * LANES:h * 256 + (j + 1) * LANES]
        left = _left_half(pair.shape)
        zero = jnp.zeros_like(pair)
        pieces += [jnp.where(left, pair, zero), jnp.where(left, zero, pair)]
    return jnp.concatenate(pieces, axis=0)


def _attn_probs_t(kk, q_st, bias_t, sink_row, first):
    s = _dot_nt(kk, q_st) * ATTN_SCALE + bias_t
    if first is not None:
        row = lax.broadcasted_iota(jnp.int32, s.shape, 0)
        s = jnp.where(jnp.logical_and(first, row < BLOCK), NEG_INF, s)
    m = jnp.maximum(jnp.max(s, axis=0, keepdims=True), sink_row)
    p = jnp.exp(s - m)
    es = jnp.exp(sink_row - m)
    inv = 1.0 / (jnp.sum(p, axis=0, keepdims=True) + es)
    return p * inv, es * inv


def _pool_d(ext, cur, g, row0):
    w = POOL_WINDOWS[g]
    acc = ext
    k = 1
    while k < w:
        acc = acc + pltpu.roll(acc, k, 0)
        k *= 2
    t = row0 + lax.broadcasted_iota(jnp.int32, cur.shape, 0)
    cnt = jnp.minimum(t + 1, w).astype(F32)
    return acc[POOL_HALO:, :] / cnt - cur


def _mixers_fwd(zp, q, kv, pool_w, pool_b, pool_scale, bias_t, sink_row):
    s_len = zp.shape[0]
    tq = min(512, s_len)
    nb = tq // BLOCK

    def body(zp_ref, zph_ref, q_ref, kv_ref, kvh_ref, pw_ref, pb_ref, ps_ref, bias_ref, sink_ref,
             pm_ref, o_ref, ka_ref, vd_ref):
        i = pl.program_id(0)
        cur = zp_ref[...].astype(F32)
        halo = zph_ref[...].astype(F32) * (i > 0).astype(F32)
        ext = jnp.concatenate([halo, cur], axis=0)
        for g in range(4):
            sl = slice(g * POOL_GROUP_DIM, (g + 1) * POOL_GROUP_DIM)
            d = _pool_d(ext[:, sl], cur[:, sl], g, i * tq)
            y = _dot(d.astype(BF16), pw_ref[g]) + pb_ref[:, sl]
            pm_ref[:, sl] = (y * ps_ref[:, sl]).astype(BF16)
        _fill_kv_slabs(kvh_ref, kv_ref, ka_ref, vd_ref)
        for b in range(nb):
            rq = slice(b * BLOCK, (b + 1) * BLOCK)
            rk = slice(b * BLOCK, (b + 2) * BLOCK)
            qb = q_ref[rq, :]
            for h in range(N_KV_HEADS):
                pn, _ = _attn_probs_t(ka_ref[h, rk, :], _stack_pairs(qb, h), bias_ref[h], sink_ref[h],
                                      (i == 0) if b == 0 else None)
                pn = pn.astype(BF16)
                vd = vd_ref[h, rk, :]
                left = _left_half(vd.shape)
                zero = jnp.zeros_like(vd)
                va, vb = jnp.where(left, vd, zero), jnp.where(left, zero, vd)
                for j in range(2):
                    o_pair = (_dot_tn(pn[:, (2 * j) * BLOCK:(2 * j + 1) * BLOCK], va)
                              + _dot_tn(pn[:, (2 * j + 1) * BLOCK:(2 * j + 2) * BLOCK], vb))
                    o_ref[rq, h * 256 + j * LANES:h * 256 + (j + 1) * LANES] = o_pair.astype(BF16)

    halo_pool = pl.BlockSpec((POOL_HALO, 512), lambda i: (jnp.maximum(i * (tq // POOL_HALO) - 1, 0), 0))
    halo_kv = pl.BlockSpec((BLOCK, 256), lambda i: (jnp.maximum(i * nb - 1, 0), 0))
    return pl.pallas_call(
        body, name="mixers_fwd", grid=(s_len // tq,),
        in_specs=[_rows(tq, 512), halo_pool, _rows(tq, 512), _rows(tq, 256), halo_kv,
                  _whole((4, 128, 128)), _whole((1, 512)), _whole((1, 512)),
                  _whole((N_KV_HEADS, 2 * BLOCK, GROUP * BLOCK)), _whole((N_KV_HEADS, 1, GROUP * BLOCK))],
        out_specs=[_rows(tq, 512), _rows(tq, 512)],
        out_shape=[jax.ShapeDtypeStruct((s_len, 512), BF16)] * 2,
        scratch_shapes=[pltpu.VMEM((N_KV_HEADS, tq + BLOCK, LANES), BF16)] * 2,
        compiler_params=_params(),
    )(zp, zp, q, kv, kv, pool_w, pool_b, pool_scale, bias_t, sink_row)


def _gated_mix(pm, o, zg, pp_ref, pa_ref):
    yp = _dot(pm, pp_ref[...])
    ya = _dot(o, pa_ref[...])
    gp = jax.nn.sigmoid(zg[:, :D_MODEL].astype(F32))
    ga = jax.nn.sigmoid(zg[:, D_MODEL:].astype(F32))
    return yp, ya, gp, ga


def _mix_out(x, pm, o, zg, p_pool, p_attn, w_out):
    s_len = x.shape[0]
    tm = min(512, s_len)

    def body(x_ref, pm_ref, o_ref, zg_ref, pp_ref, pa_ref, wo_ref, h1_ref, mixed_ref):
        yp, ya, gp, ga = _gated_mix(pm_ref[...], o_ref[...], zg_ref[...], pp_ref, pa_ref)
        mixed = (gp * yp + ga * ya).astype(BF16)
        mixed_ref[...] = mixed
        h1_ref[...] = x_ref[...] + _dot(mixed, wo_ref[...])

    return pl.pallas_call(
        body, name="mix_out", grid=(s_len // tm,),
        in_specs=[_rows(tm, D_MODEL), _rows(tm, 512), _rows(tm, 512), _rows(tm, 2048),
                  _whole((512, D_MODEL)), _whole((512, D_MODEL)), _whole((D_MODEL, D_MODEL))],
        out_specs=[_rows(tm, D_MODEL), _rows(tm, D_MODEL)],
        out_shape=[jax.ShapeDtypeStruct((s_len, D_MODEL), F32), jax.ShapeDtypeStruct((s_len, D_MODEL), BF16)],
        compiler_params=_params(),
    )(x, pm, o, zg, p_pool, p_attn, w_out)


def _mlp_loss(h1, tgt, g_mlp, g_fin, w_up, w_down):
    s_len = h1.shape[0]
    tm = min(256, s_len)
    n_chunks = D_FF // FF_CHUNK

    def body(h1_ref, tgt_ref, gm_ref, gf_ref, wu_ref, wd_ref,
             dh1_ref, a_ref, dap_ref, u2_ref, dh2_ref, loss_ref, dgm_ref, dgf_ref):
        i = pl.program_id(0)

        @pl.when(i == 0)
        def _():
            loss_ref[...] = jnp.zeros_like(loss_ref)
            dgm_ref[...] = jnp.zeros_like(dgm_ref)
            dgf_ref[...] = jnp.zeros_like(dgf_ref)

        h1 = h1_ref[...]
        r2, xh2, u2 = _rms_fwd(h1, gm_ref[...])
        u2 = u2.astype(BF16)
        u2_ref[...] = u2
        acc = jnp.zeros((tm, D_MODEL), F32)
        for c in range(n_chunks):
            cs = slice(c * FF_CHUNK, (c + 1) * FF_CHUNK)
            a = jnp.maximum(_dot(u2, wu_ref[:, cs]), 0.0)
            a_ref[:, cs] = a.astype(BF16)
            acc = acc + _dot((a * a).astype(BF16), wd_ref[cs, :])
        h2 = h1 + acc
        r3, xh3, y = _rms_fwd(h2, gf_ref[...])
        diff = y - tgt_ref[...]
        loss_ref[...] += 0.5 * jnp.sum(jnp.mean(diff * diff, axis=-1, keepdims=True))
        dy = diff * (1.0 / D_MODEL)
        dh2, dgf = _rms_bwd(dy, xh3, r3, gf_ref[...])
        dgf_ref[...] += dgf
        dh2_bf = dh2.astype(BF16)
        dh2_ref[...] = dh2_bf
        du2 = jnp.zeros((tm, D_MODEL), F32)
        for c in range(n_chunks):
            cs = slice(c * FF_CHUNK, (c + 1) * FF_CHUNK)
            ds = _dot_nt(dh2_bf, wd_ref[cs, :])
            dap = (ds * (2.0 * a_ref[:, cs].astype(F32))).astype(BF16)
            dap_ref[:, cs] = dap
            du2 = du2 + _dot_nt(dap, wu_ref[:, cs])
        dh1n, dgm = _rms_bwd(du2, xh2, r2, gm_ref[...])
        dgm_ref[...] += dgm
        dh1_ref[...] = dh2 + dh1n

    single = dict(pipeline_mode=pl.Buffered(1))
    return pl.pallas_call(
        body, name="mlp_loss", grid=(s_len // tm,),
        in_specs=[_rows(tm, D_MODEL), _rows(tm, D_MODEL), _whole((1, D_MODEL)), _whole((1, D_MODEL)),
                  pl.BlockSpec((D_MODEL, D_FF), lambda i: (0, 0), **single),
                  pl.BlockSpec((D_FF, D_MODEL), lambda i: (0, 0), **single)],
        out_specs=[_rows(tm, D_MODEL), _rows(tm, D_FF), _rows(tm, D_FF), _rows(tm, D_MODEL), _rows(tm, D_MODEL),
                   _whole((8, LANES)), _whole((1, D_MODEL)), _whole((1, D_MODEL))],
        out_shape=[jax.ShapeDtypeStruct((s_len, D_MODEL), F32), jax.ShapeDtypeStruct((s_len, D_FF), BF16),
                   jax.ShapeDtypeStruct((s_len, D_FF), BF16), jax.ShapeDtypeStruct((s_len, D_MODEL), BF16),
                   jax.ShapeDtypeStruct((s_len, D_MODEL), BF16), jax.ShapeDtypeStruct((8, LANES), F32),
                   jax.ShapeDtypeStruct((1, D_MODEL), F32), jax.ShapeDtypeStruct((1, D_MODEL), F32)],
        compiler_params=_params(),
    )(h1, tgt, g_mlp, g_fin, w_up, w_down)


def _tn_matmul(a, b, square_a=False):
    s_len, ka = a.shape
    nb = b.shape[1]
    tt = min(1024, s_len)
    tk = min(1024, ka)
    tn = min(1024, nb)
    n_t = s_len // tt

    def body(a_ref, b_ref, o_ref, acc_ref):
        t = pl.program_id(2)

        @pl.when(t == 0)
        def _():
            acc_ref[...] = jnp.zeros_like(acc_ref)

        av = a_ref[...]
        if square_a:
            av = av * av
        acc_ref[...] += _dot_tn(av.astype(BF16), b_ref[...].astype(BF16))

        @pl.when(t == n_t - 1)
        def _():
            o_ref[...] = acc_ref[...].astype(o_ref.dtype)

    return pl.pallas_call(
        body, name="tn_matmul", grid=(ka // tk, nb // tn, n_t),
        in_specs=[pl.BlockSpec((tt, tk), lambda k, j, t: (t, k)), pl.BlockSpec((tt, tn), lambda k, j, t: (t, j))],
        out_specs=pl.BlockSpec((tk, tn), lambda k, j, t: (k, j)),
        out_shape=jax.ShapeDtypeStruct((ka, nb), BF16),
        scratch_shapes=[pltpu.VMEM((tk, tn), F32)],
        compiler_params=_params(3),
    )(a, b)


def _mix_bwd(after, dh1, pm, o, zg, p_pool, p_attn, w_out):
    s_len = dh1.shape[0]
    tm = min(512, s_len)

    def body(after_ref, dh1_ref, pm_ref, o_ref, zg_ref, pp_ref, pa_ref, wo_ref,
             dyp_ref, dya_ref, dzg_ref, dpm_ref, do_ref):
        dm = _dot_nt(dh1_ref[...].astype(BF16), wo_ref[...])
        yp, ya, gp, ga = _gated_mix(pm_ref[...], o_ref[...], zg_ref[...], pp_ref, pa_ref)
        dyp = (dm * gp).astype(BF16)
        dya = (dm * ga).astype(BF16)
        dyp_ref[...] = dyp
        dya_ref[...] = dya
        dzg_ref[:, :D_MODEL] = (dm * yp * (gp * (1.0 - gp))).astype(BF16)
        dzg_ref[:, D_MODEL:] = (dm * ya * (ga * (1.0 - ga))).astype(BF16)
        dpm_ref[...] = _dot_nt(dyp, pp_ref[...]).astype(BF16)
        do_ref[...] = _dot_nt(dya, pa_ref[...]).astype(BF16)

    return pl.pallas_call(
        body, name="mix_bwd", grid=(s_len // tm,),
        in_specs=[ANY, _rows(tm, D_MODEL), _rows(tm, 512), _rows(tm, 512), _rows(tm, 2048),
                  _whole((512, D_MODEL)), _whole((512, D_MODEL)), _whole((D_MODEL, D_MODEL))],
        out_specs=[_rows(tm, D_MODEL), _rows(tm, D_MODEL), _rows(tm, 2048), _rows(tm, 512), _rows(tm, 512)],
        out_shape=[jax.ShapeDtypeStruct((s_len, n), BF16) for n in (D_MODEL, D_MODEL, 2048, 512, 512)],
        compiler_params=_params(),
    )(after, dh1, pm, o, zg, p_pool, p_attn, w_out)


def _mixers_bwd(after, zp, q, kv, dpm, do, pool_w, pool_b, pool_scale, bias_t, sink_row):
    s_len = zp.shape[0]
    tq = min(512, s_len)
    nb = tq // BLOCK
    n_steps = s_len // tq

    def body(after_ref, zp_ref, zph_ref, q_ref, kv_ref, kvh_ref, dpm_ref, dpmh_ref, do_ref, pw_ref, pb_ref, ps_ref,
             bias_ref, sink_ref, dzp_ref, dq_ref, dkv_ref, dpw_ref, dpb_ref, dps_ref, dsk_ref, ka_ref, vd_ref, dsk_acc):
        i = pl.program_id(0)

        @pl.when(i == 0)
        def _():
            dkv_ref[...] = jnp.zeros_like(dkv_ref)
            dpw_ref[...] = jnp.zeros_like(dpw_ref)
            dpb_ref[...] = jnp.zeros_like(dpb_ref)
            dps_ref[...] = jnp.zeros_like(dps_ref)
            dsk_acc[...] = jnp.zeros_like(dsk_acc)

        cur = zp_ref[...].astype(F32)
        halo = zph_ref[...].astype(F32) * (i > 0).astype(F32)
        ext = jnp.concatenate([halo, cur], axis=0)
        dpm_next = dpmh_ref[...].astype(F32) * (i < n_steps - 1).astype(F32)
        dpm_ext = jnp.concatenate([dpm_ref[...].astype(F32), dpm_next], axis=0)
        n_ext = tq + POOL_HALO
        for g in range(4):
            sl = slice(g * POOL_GROUP_DIM, (g + 1) * POOL_GROUP_DIM)
            w = POOL_WINDOWS[g]
            d = _pool_d(ext[:, sl], cur[:, sl], g, i * tq).astype(BF16)
            y_lin = _dot(d, pw_ref[g]) + pb_ref[:, sl]
            dps_ref[:, sl] += jnp.sum(dpm_ext[:tq, sl] * y_lin, axis=0, keepdims=True)
            dyl_ext = dpm_ext[:, sl] * ps_ref[:, sl]
            dpb_ref[:, sl] += jnp.sum(dyl_ext[:tq], axis=0, keepdims=True)
            dyl_bf = dyl_ext.astype(BF16)
            dpw_ref[g] += _dot_tn(d, dyl_bf[:tq])
            dd = _dot_nt(dyl_bf, pw_ref[g])
            t = i * tq + lax.broadcasted_iota(jnp.int32, dd.shape, 0)
            e = dd / jnp.minimum(t + 1, w).astype(F32)
            acc = e
            k = 1
            while k < w:
                acc = acc + pltpu.roll(acc, n_ext - k, 0)
                k *= 2
            dzp_ref[:, sl] = (acc[:tq] - dd[:tq]).astype(BF16)

        _fill_kv_slabs(kvh_ref, kv_ref, ka_ref, vd_ref)

        def fold(dup):
            return dup + pltpu.roll(dup, HEAD_DIM, 1)

        for b in range(nb):
            rq = slice(b * BLOCK, (b + 1) * BLOCK)
            rk = slice(b * BLOCK, (b + 2) * BLOCK)
            qb = q_ref[rq, :]
            dob = do_ref[rq, :]
            dk_dup, dv_dup = [], []
            for h in range(N_KV_HEADS):
                kk = ka_ref[h, rk, :]
                q_st = _stack_pairs(qb, h)
                do_st = _stack_pairs(dob, h)
                pn, psink = _attn_probs_t(kk, q_st, bias_ref[h], sink_ref[h], (i == 0) if b == 0 else None)
                dp = _dot_nt(vd_ref[h, rk, :], do_st)
                delta = jnp.sum(pn * dp, axis=0, keepdims=True)
                dsk_acc[h] += -psink * delta
                ds = ((pn * (dp - delta)) * ATTN_SCALE).astype(BF16)
                dq_st = _dot_tn(ds, kk)
                for j in range(2):
                    left = _left_half((BLOCK, LANES))
                    dq_pair = jnp.where(left, dq_st[(2 * j) * BLOCK:(2 * j + 1) * BLOCK],
                                        dq_st[(2 * j + 1) * BLOCK:(2 * j + 2) * BLOCK])
                    dq_ref[rq, h * 256 + j * LANES:h * 256 + (j + 1) * LANES] = dq_pair.astype(BF16)
                dk_dup.append(fold(_dot(ds, q_st)))
                dv_dup.append(fold(_dot(pn.astype(BF16), do_st)))
            left = _left_half((2 * BLOCK, LANES))
            dkv_blk = jnp.concatenate([jnp.where(left, dk_dup[0], dk_dup[1]),
                                       jnp.where(left, dv_dup[0], dv_dup[1])], axis=1)
            g0 = pl.multiple_of(i * tq + b * BLOCK, BLOCK)
            dkv_ref[pl.ds(g0, 2 * BLOCK), :] += dkv_blk

        @pl.when(i == n_steps - 1)
        def _():
            for h in range(N_KV_HEADS):
                for g in range(GROUP):
                    tot = jnp.sum(dsk_acc[h, :, g * BLOCK:(g + 1) * BLOCK], axis=1, keepdims=True)
                    dsk_ref[GROUP * h + g:GROUP * h + g + 1, :] = jnp.broadcast_to(tot, (1, LANES))

    blocks_per_tile = tq // POOL_HALO
    last_halo = s_len // POOL_HALO - 1
    halo_prev = pl.BlockSpec((POOL_HALO, 512), lambda i: (jnp.maximum(i * blocks_per_tile - 1, 0), 0))
    halo_next = pl.BlockSpec((POOL_HALO, 512), lambda i: (jnp.minimum((i + 1) * blocks_per_tile, last_halo), 0))
    halo_kv = pl.BlockSpec((BLOCK, 256), lambda i: (jnp.maximum(i * nb - 1, 0), 0))
    return pl.pallas_call(
        body, name="mixers_bwd", grid=(n_steps,),
        in_specs=[ANY, _rows(tq, 512), halo_prev, _rows(tq, 512), _rows(tq, 256), halo_kv,
                  _rows(tq, 512), halo_next, _rows(tq, 512),
                  _whole((4, 128, 128)), _whole((1, 512)), _whole((1, 512)),
                  _whole((N_KV_HEADS, 2 * BLOCK, GROUP * BLOCK)), _whole((N_KV_HEADS, 1, GROUP * BLOCK))],
        out_specs=[_rows(tq, 512), _rows(tq, 512), _whole((s_len + BLOCK, 256)),
                   _whole((4, 128, 128)), _whole((1, 512)), _whole((1, 512)), _whole((8, LANES))],
        out_shape=[jax.ShapeDtypeStruct((s_len, 512), BF16), jax.ShapeDtypeStruct((s_len, 512), BF16),
                   jax.ShapeDtypeStruct((s_len + BLOCK, 256), F32), jax.ShapeDtypeStruct((4, 128, 128), F32),
                   jax.ShapeDtypeStruct((1, 512), F32), jax.ShapeDtypeStruct((1, 512), F32),
                   jax.ShapeDtypeStruct((8, LANES), F32)],
        scratch_shapes=[pltpu.VMEM((N_KV_HEADS, tq + BLOCK, LANES), BF16)] * 2
        + [pltpu.VMEM((N_KV_HEADS, 1, GROUP * BLOCK), F32)],
        compiler_params=_params(),
    )(after, zp, zp, q, kv, kv, dpm, dpm, do, pool_w, pool_b, pool_scale, bias_t, sink_row)


def _in_bwd(after, dzp, dq, dkv, dzg, w_in, x, dh1, g_mix):
    s_len = x.shape[0]
    tm = min(512, s_len)

    def body(after_ref, dzp_ref, dq_ref, dkv_ref, dzg_ref, w_ref, x_ref, dh1_ref, g_ref, dx_ref, dg_ref):
        i = pl.program_id(0)

        @pl.when(i == 0)
        def _():
            dg_ref[...] = jnp.zeros_like(dg_ref)

        du = _dot_nt(dzp_ref[...], w_ref[:, 0:512])
        du = du + _dot_nt(dq_ref[...], w_ref[:, 512:1024])
        du = du + _dot_nt(dkv_ref[...], w_ref[:, 1024:1280])
        du = du + _dot_nt(dzg_ref[...], w_ref[:, 1280:3328])
        r, xh, _ = _rms_fwd(x_ref[...], g_ref[...])
        dxn, dg = _rms_bwd(du, xh, r, g_ref[...])
        dg_ref[...] += dg
        dx_ref[...] = dh1_ref[...] + dxn

    return pl.pallas_call(
        body, name="in_bwd", grid=(s_len // tm,),
        in_specs=[ANY, _rows(tm, 512), _rows(tm, 512), _rows(tm, 256), _rows(tm, 2048), _whole((D_MODEL, IN_WIDTH)),
                  _rows(tm, D_MODEL), _rows(tm, D_MODEL), _whole((1, D_MODEL))],
        out_specs=[_rows(tm, D_MODEL), _whole((1, D_MODEL))],
        out_shape=[jax.ShapeDtypeStruct((s_len, D_MODEL), F32), jax.ShapeDtypeStruct((1, D_MODEL), F32)],
        compiler_params=_params(),
    )(after, dzp, dq, dkv, dzg, w_in, x, dh1, g_mix)


def _all_gather_weights(name, shards):
    n = len(shards)

    def body(*refs):
        ins, outs = refs[:n], refs[n:2 * n]
        send_sems, recv_sems, local_sems = refs[2 * n:]
        x, y, c = lax.axis_index("x"), lax.axis_index("y"), lax.axis_index("c")
        me, sibling = (x, y, c), (x, y, 1 - c)
        chips = [(1 - x, y), (x, 1 - y), (1 - x, 1 - y)]

        def slot(a, px, py, pc):
            return outs[a].at[4 * px + 2 * py + pc]

        def copy(a, k, block, to, src=None):
            return pltpu.make_async_remote_copy(
                src_ref=slot(a, *block) if src is None else src, dst_ref=slot(a, *block),
                send_sem=send_sems.at[a, k], recv_sem=recv_sems.at[a, k], device_id=to, device_id_type=MESH)

        mine = [pltpu.make_async_copy(ins[a], slot(a, *me), local_sems.at[a]) for a in range(n)]
        for cp in mine:
            cp.start()
        first = []
        for a in range(n):
            first.append(copy(a, 0, me, sibling, src=ins[a]))
            first += [copy(a, 1 + j, me, (*chip, c), src=ins[a]) for j, chip in enumerate(chips)]
        for cp in first:
            cp.start()
        passed = []
        for a in range(n):
            for j, chip in enumerate(chips):
                copy(a, 1 + j, (*chip, c), me).wait_recv()
                cp = copy(a, 4 + j, (*chip, c), sibling)
                cp.start()
                passed.append(cp)
        for a in range(n):
            copy(a, 0, sibling, me).wait_recv()
            for j, chip in enumerate(chips):
                copy(a, 4 + j, (*chip, 1 - c), me).wait_recv()
        for cp in first + passed:
            cp.wait_send()
        for cp in mine:
            cp.wait()

    return pl.pallas_call(
        body, name=name,
        in_specs=[ANY] * n, out_specs=[ANY] * n,
        out_shape=[jax.ShapeDtypeStruct((N_DEV,) + s.shape, s.dtype) for s in shards],
        scratch_shapes=[pltpu.SemaphoreType.DMA((n, 7)), pltpu.SemaphoreType.DMA((n, 7)), pltpu.SemaphoreType.DMA((n,))],
    )(*shards)


HBM_SPEC = pl.BlockSpec(memory_space=pltpu.HBM)
SEM_SPEC = pl.BlockSpec(memory_space=pltpu.SEMAPHORE)
DATAFLOW = pltpu.SideEffectType.DATAFLOW_SIDE_EFFECTING
N_PEERS = N_DEV - 1


def _peer_copies(srcs, lands, scatter, send_sems, recv_sems):
    x, y, c = lax.axis_index("x"), lax.axis_index("y"), lax.axis_index("c")
    me_idx = 4 * x + 2 * y + c
    copies = []
    for k in range(1, N_DEV):
        px = 1 - x if (k >> 2) & 1 else x
        py = 1 - y if (k >> 1) & 1 else y
        pc = 1 - c if k & 1 else c
        p_idx = 4 * px + 2 * py + pc
        for a in range(len(srcs)):
            src = srcs[a].at[p_idx] if scatter[a] else srcs[a]
            dst = lands[a].at[k] if scatter[a] else lands[a].at[me_idx]
            copies.append(pltpu.make_async_remote_copy(
                src_ref=src, dst_ref=dst, send_sem=send_sems.at[a * N_PEERS + k - 1],
                recv_sem=recv_sems.at[a * N_PEERS + k - 1],
                device_id=(px, py, pc), device_id_type=MESH))
    return copies


def _exchange_start(name, srcs, scatter, after):
    n = len(srcs)
    lands = [lax.empty(s.shape if sc else (N_DEV,) + s.shape, s.dtype) for s, sc in zip(srcs, scatter)]

    def body(*refs):
        src_refs, land_refs = refs[:n], refs[n:2 * n]
        send_sems, recv_sems = refs[2 * n + 1], refs[2 * n + 2]
        token = refs[4 * n + 3]
        for cp in _peer_copies(src_refs, land_refs, scatter, send_sems, recv_sems):
            cp.start()
        token[...] = jnp.zeros_like(token)

    hbm = lambda t: pltpu.HBM(t.shape, t.dtype)
    outs = pl.pallas_call(
        body, name=name,
        out_shape=[pltpu.SemaphoreType.DMA((n * N_PEERS,)), pltpu.SemaphoreType.DMA((n * N_PEERS,))]
        + [hbm(t) for t in srcs] + [hbm(t) for t in lands] + [jax.ShapeDtypeStruct((8, LANES), F32)],
        in_specs=[HBM_SPEC] * (2 * n) + [ANY],
        out_specs=[SEM_SPEC, SEM_SPEC] + [HBM_SPEC] * (2 * n) + [pl.BlockSpec(memory_space=pltpu.VMEM)],
        input_output_aliases={i: 2 + i for i in range(2 * n)},
        compiler_params=pltpu.CompilerParams(has_side_effects=DATAFLOW),
    )(*[pltpu.with_memory_space_constraint(t, pltpu.HBM) for t in list(srcs) + lands], after)
    return dict(n=n, scatter=scatter, send_sems=outs[0], recv_sems=outs[1], srcs=outs[2:2 + n],
                lands=outs[2 + n:2 + 2 * n], token=outs[2 + 2 * n])


def _exchange_wait(name, handle, after):
    n, scatter = handle["n"], handle["scatter"]

    def body(*refs):
        src_refs, land_refs = refs[:n], refs[n:2 * n]
        send_sems, recv_sems = refs[2 * n], refs[2 * n + 1]
        local_sems = refs[4 * n + 3]
        me_idx = 4 * lax.axis_index("x") + 2 * lax.axis_index("y") + lax.axis_index("c")
        own = [pltpu.make_async_copy(src_refs[a].at[me_idx], land_refs[a].at[0], local_sems.at[a]) if scatter[a]
               else pltpu.make_async_copy(src_refs[a], land_refs[a].at[me_idx], local_sems.at[a]) for a in range(n)]
        for cp in own:
            cp.start()
        for cp in _peer_copies(src_refs, land_refs, scatter, send_sems, recv_sems):
            cp.wait_send()
            cp.wait_recv()
        for cp in own:
            cp.wait()

    both = list(handle["srcs"]) + list(handle["lands"])
    outs = pl.pallas_call(
        body, name=name,
        out_shape=[pltpu.HBM(t.shape, t.dtype) for t in both],
        in_specs=[HBM_SPEC] * (2 * n) + [SEM_SPEC, SEM_SPEC, ANY],
        out_specs=[HBM_SPEC] * (2 * n),
        input_output_aliases={i: i for i in range(2 * n)},
        scratch_shapes=[pltpu.SemaphoreType.DMA((n,))],
        compiler_params=pltpu.CompilerParams(has_side_effects=DATAFLOW),
    )(*both, handle["send_sems"], handle["recv_sems"], after)
    return outs[n:]


def _adamw(parts, w, m, v):
    r, c = w.shape
    tr = 256 if r % 256 == 0 else r

    def body(p_ref, w_ref, m_ref, v_ref, g_ref, d_ref, nm_ref, nv_ref):
        g = p_ref[0].astype(F32)
        for k in range(1, N_DEV):
            g = g + p_ref[k].astype(F32)
        m_new = ADAM_B1 * m_ref[...] + (1.0 - ADAM_B1) * g
        v_new = ADAM_B2 * v_ref[...] + (1.0 - ADAM_B2) * (g * g)
        m_hat = m_new / (1.0 - ADAM_B1 ** ADAM_STEP)
        v_hat = v_new / (1.0 - ADAM_B2 ** ADAM_STEP)
        g_ref[...] = g
        d_ref[...] = -ADAM_LR * (m_hat / (jnp.sqrt(v_hat) + ADAM_EPS) + ADAM_WD * w_ref[...])
        nm_ref[...] = m_new
        nv_ref[...] = v_new

    return pl.pallas_call(
        body, name="adamw", grid=(r // tr,),
        in_specs=[pl.BlockSpec((N_DEV, tr, c), lambda i: (0, i, 0))] + [_rows(tr, c)] * 3,
        out_specs=[_rows(tr, c)] * 4,
        out_shape=[jax.ShapeDtypeStruct((r, c), F32)] * 4,
        compiler_params=_params(),
    )(parts, w, m, v)


SMALL_LAYOUT = (("norm_mix", 8), ("pool_w", 512), ("pool_b", 8), ("pool_scale", 8), ("attn_sinks", 8),
                ("norm_mlp", 8), ("norm_final", 8), ("loss", 8))
SMALL_ROWS = sum(n for _, n in SMALL_LAYOUT)


def _pack_small(vals, layout=SMALL_LAYOUT):
    rows = []
    for name, n_rows in layout:
        flat = vals[name].astype(F32).reshape(-1)
        flat = jnp.pad(flat, (0, n_rows * LANES - flat.shape[0]))
        rows.append(flat.reshape(n_rows, LANES))
    return jnp.concatenate(rows, axis=0)


def _unpack_small(pack, shapes):
    out, r0 = {}, 0
    for name, n_rows in SMALL_LAYOUT:
        size = int(np.prod(shapes[name])) if shapes[name] else 1
        out[name] = pack[r0:r0 + n_rows].reshape(-1)[:size].reshape(shapes[name])
        r0 += n_rows
    return out


def kernel(x, norm_mix, w_in, pool_w, pool_b, pool_scale, attn_sinks, p_pool, p_attn, w_out, norm_mlp, w_up, w_down, norm_final, loss_target, m_norm_mix, m_w_in, m_pool_w, m_pool_b, m_pool_scale, m_attn_sinks, m_p_pool, m_p_attn, m_w_out, m_norm_mlp, m_w_up, m_w_down, m_norm_final, v_norm_mix, v_w_in, v_pool_w, v_pool_b, v_pool_scale, v_attn_sinks, v_p_pool, v_p_attn, v_w_out, v_norm_mlp, v_w_up, v_w_down, v_norm_final):
    xs = x[0]
    tgt = loss_target[0]
    s_len = xs.shape[0]

    w_in_bf, p_pool_bf, p_attn_bf, w_out_bf, w_up_bf, w_down_bf = [
        t[0].astype(BF16) for t in (w_in, p_pool, p_attn, w_out, w_up, w_down)]
    (w_in_g,) = _all_gather_weights("all_gather_w_in", [w_in_bf])
    ag_proj = _exchange_start("ag_proj_start", [p_pool_bf, p_attn_bf, w_out_bf], (False,) * 3, w_in_g)
    ag_mlp = _exchange_start("ag_mlp_start", [w_up_bf, w_down_bf], (False,) * 2, ag_proj["token"])
    w_in_f = w_in_g.transpose(1, 0, 2).reshape(D_MODEL, IN_WIDTH)

    pool_w_bf = pool_w[0].astype(BF16)
    pool_b_row = pool_b[0].reshape(1, POOL_WIDTH)
    bias_t, sink_row = _attn_constants(attn_sinks[0])

    u, zp, q, kv, zg = _fwd_in(ag_mlp["token"], xs, norm_mix, w_in_f)
    pm, o = _mixers_fwd(zp, q, kv, pool_w_bf, pool_b_row, pool_scale, bias_t, sink_row)
    p_pool_g, p_attn_g, w_out_g = _exchange_wait("ag_proj_wait", ag_proj, pm)
    p_pool_f = p_pool_g.transpose(1, 0, 2).reshape(POOL_WIDTH, D_MODEL)
    p_attn_f = p_attn_g.transpose(1, 0, 2).reshape(ATTN_WIDTH, D_MODEL)
    w_out_f = w_out_g.reshape(D_MODEL, D_MODEL)
    h1, mixed = _mix_out(xs, pm, o, zg, p_pool_f, p_attn_f, w_out_f)
    w_up_g, w_down_g = _exchange_wait("ag_mlp_wait", ag_mlp, h1)
    w_up_f = w_up_g.transpose(1, 0, 2).reshape(D_MODEL, D_FF)
    w_down_f = w_down_g.reshape(D_FF, D_MODEL)
    dh1, a, dapre, u2, dh2, loss_part, g_norm_mlp, g_norm_final = _mlp_loss(
        h1, tgt, norm_mlp, norm_final.reshape(1, D_MODEL), w_up_f, w_down_f)
    gw_down = _tn_matmul(a, dh2, square_a=True)
    gw_up = _tn_matmul(u2, dapre)
    ex_mlp = _exchange_start(
        "ex_mlp_start", [gw_up.reshape(D_MODEL, N_DEV, D_FF // N_DEV).transpose(1, 0, 2),
                         gw_down.reshape(N_DEV, D_FF // N_DEV, D_MODEL)], (True, True), gw_down)
    dyp, dya, dzg, dpm, do = _mix_bwd(ex_mlp["token"], dh1, pm, o, zg, p_pool_f, p_attn_f, w_out_f)
    gw_out = _tn_matmul(mixed, dh1)
    gp_pool = _tn_matmul(pm, dyp)
    gp_attn = _tn_matmul(o, dya)
    ex_proj = _exchange_start(
        "ex_proj_start", [gp_pool.reshape(POOL_WIDTH, N_DEV, D_MODEL // N_DEV).transpose(1, 0, 2),
                          gp_attn.reshape(ATTN_WIDTH, N_DEV, D_MODEL // N_DEV).transpose(1, 0, 2),
                          gw_out.reshape(N_DEV, D_MODEL // N_DEV, D_MODEL)], (True,) * 3, gw_out)
    dzp, dq, dkv_acc, g_pool_w, g_pool_b, g_pool_scale, g_sinks = _mixers_bwd(
        ex_proj["token"], zp, q, kv, dpm, do, pool_w_bf, pool_b_row, pool_scale, bias_t, sink_row)
    dkv = dkv_acc[BLOCK:].astype(BF16)
    gw_in = jnp.concatenate(
        [_tn_matmul(u, dzp), _tn_matmul(u, dq), _tn_matmul(u, dkv), _tn_matmul(u, dzg)], axis=1)
    small_vals = dict(pool_w=g_pool_w, pool_b=g_pool_b, pool_scale=g_pool_scale, attn_sinks=g_sinks[:, 0],
                      norm_mlp=g_norm_mlp, norm_final=g_norm_final, loss=loss_part[0, 0])
    ex_in = _exchange_start(
        "ex_in_start", [gw_in.reshape(D_MODEL, N_DEV, IN_WIDTH // N_DEV).transpose(1, 0, 2),
                        _pack_small(small_vals, SMALL_LAYOUT[1:])], (True, False), gw_in)
    dx, g_norm_mix = _in_bwd(ex_in["token"], dzp, dq, dkv, dzg, w_in_f, xs, dh1, norm_mix)
    (norm_mix_all,) = _all_gather_weights(
        "all_gather_norm_mix", [_pack_small(dict(norm_mix=g_norm_mix), SMALL_LAYOUT[:1])])
    r_up, r_down = _exchange_wait("ex_mlp_wait", ex_mlp, norm_mix_all)
    r_pool, r_attn, r_out = _exchange_wait("ex_proj_wait", ex_proj, norm_mix_all)
    r_in, small_rest = _exchange_wait("ex_in_wait", ex_in, norm_mix_all)
    recvs = [r_in, r_pool, r_attn, r_out, r_up, r_down]
    small_all = jnp.concatenate([norm_mix_all, small_rest], axis=1)

    big_names = ["w_in", "p_pool", "p_attn", "w_out", "w_up", "w_down"]
    big_w = dict(w_in=w_in, p_pool=p_pool, p_attn=p_attn, w_out=w_out, w_up=w_up, w_down=w_down)
    big_m = dict(w_in=m_w_in, p_pool=m_p_pool, p_attn=m_p_attn, w_out=m_w_out, w_up=m_w_up, w_down=m_w_down)
    big_v = dict(w_in=v_w_in, p_pool=v_p_pool, p_attn=v_p_attn, w_out=v_w_out, w_up=v_w_up, w_down=v_w_down)
    res = {}
    for name, parts in zip(big_names, recvs):
        outs = _adamw(parts, big_w[name][0], big_m[name][0], big_v[name][0])
        res[name] = [t[None] for t in outs]

    small_names = ["norm_mix", "pool_w", "pool_b", "pool_scale", "attn_sinks", "norm_mlp", "norm_final"]
    small_w = dict(norm_mix=norm_mix, pool_w=pool_w, pool_b=pool_b, pool_scale=pool_scale, attn_sinks=attn_sinks,
                   norm_mlp=norm_mlp, norm_final=norm_final)
    small_m = dict(norm_mix=m_norm_mix, pool_w=m_pool_w, pool_b=m_pool_b, pool_scale=m_pool_scale,
                   attn_sinks=m_attn_sinks, norm_mlp=m_norm_mlp, norm_final=m_norm_final)
    small_v = dict(norm_mix=v_norm_mix, pool_w=v_pool_w, pool_b=v_pool_b, pool_scale=v_pool_scale,
                   attn_sinks=v_attn_sinks, norm_mlp=v_norm_mlp, norm_final=v_norm_final)
    zero = jnp.zeros((), F32)
    packs = [_pack_small({**d, "loss": zero}) for d in (small_w, small_m, small_v)]
    s_outs = _adamw(small_all, *packs)
    shapes = {k: small_w[k].shape for k in small_names}
    shapes["loss"] = ()
    s_res = [_unpack_small(t, shapes) for t in s_outs]
    loss = s_res[0]["loss"]
    for name in small_names:
        res[name] = [t[name] for t in s_res]

    order = ["norm_mix", "w_in", "pool_w", "pool_b", "pool_scale", "attn_sinks", "p_pool", "p_attn", "w_out",
             "norm_mlp", "w_up", "w_down", "norm_final"]
    out = [loss, dx[None]]
    for kind in range(4):
        out += [res[name][kind] for name in order]
    return tuple(out)
```

```python
import functools
import math

import numpy as np
import jax
import jax.numpy as jnp
from jax import lax
from jax.experimental import pallas as pl
from jax.experimental.pallas import tpu as pltpu

F32 = jnp.float32
BF16 = jnp.bfloat16

D_MODEL = 1024
POOL_WIDTH = 512
ATTN_WIDTH = 512
KV_WIDTH = 128
HEAD_DIM = 64
N_HEADS = 8
N_KV_HEADS = 2
GROUP = 4
BLOCK = 128
POOL_WINDOWS = (2, 4, 8, 16)
POOL_GROUP_DIM = 128
POOL_HALO = 16
D_FF = 4096
FF_CHUNK = 1024
IN_WIDTH = 3328
RMS_EPS = 1e-5
NEG_INF = -1e30
ATTN_SCALE = 1.0 / math.sqrt(HEAD_DIM)
N_DEV = 8

ADAM_LR = 0.001
ADAM_B1 = 0.9
ADAM_B2 = 0.999
ADAM_EPS = 1e-08
ADAM_WD = 0.01
ADAM_STEP = 10

LANES = 128
VMEM_LIMIT_BYTES = 56 * 1024 * 1024
MESH = pl.DeviceIdType.MESH


def _params(n_grid_axes=1):
    return pltpu.CompilerParams(
        dimension_semantics=("arbitrary",) * n_grid_axes, vmem_limit_bytes=VMEM_LIMIT_BYTES)


def _dot(a, b):
    return jnp.dot(a, b, preferred_element_type=F32)


def _dot_nt(a, b):
    return lax.dot_general(a, b, (((1,), (1,)), ((), ())), preferred_element_type=F32)


def _dot_tn(a, b):
    return lax.dot_general(a, b, (((0,), (0,)), ((), ())), preferred_element_type=F32)


ANY = pl.BlockSpec(memory_space=pl.ANY)


def _rows(tm, n):
    return pl.BlockSpec((tm, n), lambda i: (i, 0))


def _whole(shape):
    zeros = (0,) * len(shape)
    return pl.BlockSpec(shape, lambda i: zeros)


def _rms_fwd(h, g):
    r = lax.rsqrt(jnp.mean(h * h, axis=-1, keepdims=True) + RMS_EPS)
    xh = h * r
    return r, xh, xh * g


def _rms_bwd(dy, xh, r, g):
    dxh = dy * g
    dh = r * (dxh - xh * jnp.mean(dxh * xh, axis=-1, keepdims=True))
    return dh, jnp.sum(dy * xh, axis=0, keepdims=True)


def _fwd_in(after, x, g_mix, w_in):
    s_len = x.shape[0]
    tm = min(512, s_len)

    def body(after_ref, x_ref, g_ref, w_ref, u_ref, zp_ref, q_ref, kv_ref, zg_ref):
        _, _, u = _rms_fwd(x_ref[...], g_ref[...])
        u = u.astype(BF16)
        u_ref[...] = u
        zp_ref[...] = _dot(u, w_ref[:, 0:512]).astype(BF16)
        q_ref[...] = _dot(u, w_ref[:, 512:1024]).astype(BF16)
        kv_ref[...] = _dot(u, w_ref[:, 1024:1280]).astype(BF16)
        zg_ref[...] = _dot(u, w_ref[:, 1280:3328]).astype(BF16)

    return pl.pallas_call(
        body, name="fwd_in", grid=(s_len // tm,),
        in_specs=[ANY, _rows(tm, D_MODEL), _whole((1, D_MODEL)), _whole((D_MODEL, IN_WIDTH))],
        out_specs=[_rows(tm, D_MODEL), _rows(tm, 512), _rows(tm, 512), _rows(tm, 256), _rows(tm, 2048)],
        out_shape=[jax.ShapeDtypeStruct((s_len, n), BF16) for n in (D_MODEL, 512, 512, 256, 2048)],
        compiler_params=_params(),
    )(after, x, g_mix, w_in)


def _attn_constants(sinks):
    qi = np.arange(BLOCK)[:, None]
    kj = np.arange(2 * BLOCK)[None, :]
    dist = BLOCK + qi - kj
    valid = (dist >= 0) & (dist < BLOCK)
    slopes = np.array([2.0 ** (-8.0 * (h + 1) / N_HEADS) for h in range(N_HEADS)], dtype=np.float32)
    bias = np.where(valid[None], -slopes[:, None, None] * dist.astype(np.float32)[None], np.float32(NEG_INF))
    bias = bias.astype(np.float32).reshape(N_KV_HEADS, GROUP * BLOCK, 2 * BLOCK).transpose(0, 2, 1)
    sink_row = jnp.repeat(sinks.astype(F32).reshape(N_KV_HEADS, GROUP), BLOCK, axis=1)[:, None, :]
    return jnp.asarray(np.ascontiguousarray(bias)), sink_row


def _left_half(shape):
    return lax.broadcasted_iota(jnp.int32, shape, 1) < HEAD_DIM


def _dup_halves(slab):
    swapped = pltpu.roll(slab, HEAD_DIM, 1)
    left = _left_half(slab.shape)
    return jnp.where(left, slab, swapped), jnp.where(left, swapped, slab)


def _fill_kv_slabs(kvh_ref, kv_ref, ka_ref, vd_ref):
    for rows, src in ((slice(0, BLOCK), kvh_ref), (slice(BLOCK, None), kv_ref)):
        kvf = src[...].astype(F32)
        for ref, lanes in ((ka_ref, slice(0, KV_WIDTH)), (vd_ref, slice(KV_WIDTH, 2 * KV_WIDTH))):
            d0, d1 = _dup_halves(kvf[:, lanes])
            ref[0, rows, :] = d0.astype(BF16)
            ref[1, rows, :] = d1.astype(BF16)


def _stack_pairs(a, h):
    pieces = []
    for j in range(2):
        pair = a[:, h * 256 + j * LANES:h * 256 + (j + 1) * LANES]
        left = _left_half(pair.shape)
        zero = jnp.zeros_like(pair)
        pieces += [jnp.where(left, pair, zero), jnp.where(left, zero, pair)]
    return jnp.concatenate(pieces, axis=0)


def _attn_probs_t(kk, q_st, bias_t, sink_row, first):
    s = _dot_nt(kk, q_st) * ATTN_SCALE + bias_t
    if first is not None:
        row = lax.broadcasted_iota(jnp.int32, s.shape, 0)
        s = jnp.where(jnp.logical_and(first, row < BLOCK), NEG_INF, s)
    m = jnp.maximum(jnp.max(s, axis=0, keepdims=True), sink_row)
    p = jnp.exp(s - m)
    es = jnp.exp(sink_row - m)
    inv = 1.0 / (jnp.sum(p, axis=0, keepdims=True) + es)
    return p * inv, es * inv


def _pool_d(ext, cur, g, row0):
    w = POOL_WINDOWS[g]
    acc = ext
    k = 1
    while k < w:
        acc = acc + pltpu.roll(acc, k, 0)
        k *= 2
    t = row0 + lax.broadcasted_iota(jnp.int32, cur.shape, 0)
    cnt = jnp.minimum(t + 1, w).astype(F32)
    return acc[POOL_HALO:, :] / cnt - cur


def _mixers_fwd(zp, q, kv, pool_w, pool_b, pool_scale, bias_t, sink_row):
    s_len = zp.shape[0]
    tq = min(512, s_len)
    nb = tq // BLOCK

    def body(zp_ref, zph_ref, q_ref, kv_ref, kvh_ref, pw_ref, pb_ref, ps_ref, bias_ref, sink_ref,
             pm_ref, o_ref, ka_ref, vd_ref):
        i = pl.program_id(0)
        cur = zp_ref[...].astype(F32)
        halo = zph_ref[...].astype(F32) * (i > 0).astype(F32)
        ext = jnp.concatenate([halo, cur], axis=0)
        for g in range(4):
            sl = slice(g * POOL_GROUP_DIM, (g + 1) * POOL_GROUP_DIM)
            d = _pool_d(ext[:, sl], cur[:, sl], g, i * tq)
            y = _dot(d.astype(BF16), pw_ref[g]) + pb_ref[:, sl]
            pm_ref[:, sl] = (y * ps_ref[:, sl]).astype(BF16)
        _fill_kv_slabs(kvh_ref, kv_ref, ka_ref, vd_ref)
        for b in range(nb):
            rq = slice(b * BLOCK, (b + 1) * BLOCK)
            rk = slice(b * BLOCK, (b + 2) * BLOCK)
            qb = q_ref[rq, :]
            for h in range(N_KV_HEADS):
                pn, _ = _attn_probs_t(ka_ref[h, rk, :], _stack_pairs(qb, h), bias_ref[h], sink_ref[h],
                                      (i == 0) if b == 0 else None)
                pn = pn.astype(BF16)
                vd = vd_ref[h, rk, :]
                left = _left_half(vd.shape)
                zero = jnp.zeros_like(vd)
                va, vb = jnp.where(left, vd, zero), jnp.where(left, zero, vd)
                for j in range(2):
                    o_pair = (_dot_tn(pn[:, (2 * j) * BLOCK:(2 * j + 1) * BLOCK], va)
                              + _dot_tn(pn[:, (2 * j + 1) * BLOCK:(2 * j + 2) * BLOCK], vb))
                    o_ref[rq, h * 256 + j * LANES:h * 256 + (j + 1) * LANES] = o_pair.astype(BF16)

    halo_pool = pl.BlockSpec((POOL_HALO, 512), lambda i: (jnp.maximum(i * (tq // POOL_HALO) - 1, 0), 0))
    halo_kv = pl.BlockSpec((BLOCK, 256), lambda i: (jnp.maximum(i * nb - 1, 0), 0))
    return pl.pallas_call(
        body, name="mixers_fwd", grid=(s_len // tq,),
        in_specs=[_rows(tq, 512), halo_pool, _rows(tq, 512), _rows(tq, 256), halo_kv,
                  _whole((4, 128, 128)), _whole((1, 512)), _whole((1, 512)),
                  _whole((N_KV_HEADS, 2 * BLOCK, GROUP * BLOCK)), _whole((N_KV_HEADS, 1, GROUP * BLOCK))],
        out_specs=[_rows(tq, 512), _rows(tq, 512)],
        out_shape=[jax.ShapeDtypeStruct((s_len, 512), BF16)] * 2,
        scratch_shapes=[pltpu.VMEM((N_KV_HEADS, tq + BLOCK, LANES), BF16)] * 2,
        compiler_params=_params(),
    )(zp, zp, q, kv, kv, pool_w, pool_b, pool_scale, bias_t, sink_row)


def _gated_mix(pm, o, zg, pp_ref, pa_ref):
    yp = _dot(pm, pp_ref[...])
    ya = _dot(o, pa_ref[...])
    gp = jax.nn.sigmoid(zg[:, :D_MODEL].astype(F32))
    ga = jax.nn.sigmoid(zg[:, D_MODEL:].astype(F32))
    return yp, ya, gp, ga


def _mix_out(x, pm, o, zg, p_pool, p_attn, w_out):
    s_len = x.shape[0]
    tm = min(512, s_len)

    def body(x_ref, pm_ref, o_ref, zg_ref, pp_ref, pa_ref, wo_ref, h1_ref, mixed_ref):
        yp, ya, gp, ga = _gated_mix(pm_ref[...], o_ref[...], zg_ref[...], pp_ref, pa_ref)
        mixed = (gp * yp + ga * ya).astype(BF16)
        mixed_ref[...] = mixed
        h1_ref[...] = x_ref[...] + _dot(mixed, wo_ref[...])

    return pl.pallas_call(
        body, name="mix_out", grid=(s_len // tm,),
        in_specs=[_rows(tm, D_MODEL), _rows(tm, 512), _rows(tm, 512), _rows(tm, 2048),
                  _whole((512, D_MODEL)), _whole((512, D_MODEL)), _whole((D_MODEL, D_MODEL))],
        out_specs=[_rows(tm, D_MODEL), _rows(tm, D_MODEL)],
        out_shape=[jax.ShapeDtypeStruct((s_len, D_MODEL), F32), jax.ShapeDtypeStruct((s_len, D_MODEL), BF16)],
        compiler_params=_params(),
    )(x, pm, o, zg, p_pool, p_attn, w_out)


def _mlp_loss(h1, tgt, g_mlp, g_fin, w_up, w_down):
    s_len = h1.shape[0]
    tm = min(256, s_len)
    n_chunks = D_FF // FF_CHUNK

    def body(h1_ref, tgt_ref, gm_ref, gf_ref, wu_ref, wd_ref,
             dh1_ref, a_ref, dap_ref, u2_ref, dh2_ref, loss_ref, dgm_ref, dgf_ref):
        i = pl.program_id(0)

        @pl.when(i == 0)
        def _():
            loss_ref[...] = jnp.zeros_like(loss_ref)
            dgm_ref[...] = jnp.zeros_like(dgm_ref)
            dgf_ref[...] = jnp.zeros_like(dgf_ref)

        h1 = h1_ref[...]
        r2, xh2, u2 = _rms_fwd(h1, gm_ref[...])
        u2 = u2.astype(BF16)
        u2_ref[...] = u2
        acc = jnp.zeros((tm, D_MODEL), F32)
        for c in range(n_chunks):
            cs = slice(c * FF_CHUNK, (c + 1) * FF_CHUNK)
            a = jnp.maximum(_dot(u2, wu_ref[:, cs]), 0.0)
            a_ref[:, cs] = a.astype(BF16)
            acc = acc + _dot((a * a).astype(BF16), wd_ref[cs, :])
        h2 = h1 + acc
        r3, xh3, y = _rms_fwd(h2, gf_ref[...])
        diff = y - tgt_ref[...]
        loss_ref[...] += 0.5 * jnp.sum(jnp.mean(diff * diff, axis=-1, keepdims=True))
        dy = diff * (1.0 / D_MODEL)
        dh2, dgf = _rms_bwd(dy, xh3, r3, gf_ref[...])
        dgf_ref[...] += dgf
        dh2_bf = dh2.astype(BF16)
        dh2_ref[...] = dh2_bf
        du2 = jnp.zeros((tm, D_MODEL), F32)
        for c in range(n_chunks):
            cs = slice(c * FF_CHUNK, (c + 1) * FF_CHUNK)
            ds = _dot_nt(dh2_bf, wd_ref[cs, :])
            dap = (ds * (2.0 * a_ref[:, cs].astype(F32))).astype(BF16)
            dap_ref[:, cs] = dap
            du2 = du2 + _dot_nt(dap, wu_ref[:, cs])
        dh1n, dgm = _rms_bwd(du2, xh2, r2, gm_ref[...])
        dgm_ref[...] += dgm
        dh1_ref[...] = dh2 + dh1n

    single = dict(pipeline_mode=pl.Buffered(1))
    return pl.pallas_call(
        body, name="mlp_loss", grid=(s_len // tm,),
        in_specs=[_rows(tm, D_MODEL), _rows(tm, D_MODEL), _whole((1, D_MODEL)), _whole((1, D_MODEL)),
                  pl.BlockSpec((D_MODEL, D_FF), lambda i: (0, 0), **single),
                  pl.BlockSpec((D_FF, D_MODEL), lambda i: (0, 0), **single)],
        out_specs=[_rows(tm, D_MODEL), _rows(tm, D_FF), _rows(tm, D_FF), _rows(tm, D_MODEL), _rows(tm, D_MODEL),
                   _whole((8, LANES)), _whole((1, D_MODEL)), _whole((1, D_MODEL))],
        out_shape=[jax.ShapeDtypeStruct((s_len, D_MODEL), F32), jax.ShapeDtypeStruct((s_len, D_FF), BF16),
                   jax.ShapeDtypeStruct((s_len, D_FF), BF16), jax.ShapeDtypeStruct((s_len, D_MODEL), BF16),
                   jax.ShapeDtypeStruct((s_len, D_MODEL), BF16), jax.ShapeDtypeStruct((8, LANES), F32),
                   jax.ShapeDtypeStruct((1, D_MODEL), F32), jax.ShapeDtypeStruct((1, D_MODEL), F32)],
        compiler_params=_params(),
    )(h1, tgt, g_mlp, g_fin, w_up, w_down)


def _tn_matmul(a, b, square_a=False):
    s_len, ka = a.shape
    nb = b.shape[1]
    tt = min(1024, s_len)
    tk = min(1024, ka)
    tn = min(1024, nb)
    n_t = s_len // tt

    def body(a_ref, b_ref, o_ref, acc_ref):
        t = pl.program_id(2)

        @pl.when(t == 0)
        def _():
            acc_ref[...] = jnp.zeros_like(acc_ref)

        av = a_ref[...]
        if square_a:
            av = av * av
        acc_ref[...] += _dot_tn(av.astype(BF16), b_ref[...].astype(BF16))

        @pl.when(t == n_t - 1)
        def _():
            o_ref[...] = acc_ref[...].astype(o_ref.dtype)

    return pl.pallas_call(
        body, name="tn_matmul", grid=(ka // tk, nb // tn, n_t),
        in_specs=[pl.BlockSpec((tt, tk), lambda k, j, t: (t, k)), pl.BlockSpec((tt, tn), lambda k, j, t: (t, j))],
        out_specs=pl.BlockSpec((tk, tn), lambda k, j, t: (k, j)),
        out_shape=jax.ShapeDtypeStruct((ka, nb), BF16),
        scratch_shapes=[pltpu.VMEM((tk, tn), F32)],
        compiler_params=_params(3),
    )(a, b)


def _mix_bwd(after, dh1, pm, o, zg, p_pool, p_attn, w_out):
    s_len = dh1.shape[0]
    tm = min(512, s_len)

    def body(after_ref, dh1_ref, pm_ref, o_ref, zg_ref, pp_ref, pa_ref, wo_ref,
             dyp_ref, dya_ref, dzg_ref, dpm_ref, do_ref):
        dm = _dot_nt(dh1_ref[...].astype(BF16), wo_ref[...])
        yp, ya, gp, ga = _gated_mix(pm_ref[...], o_ref[...], zg_ref[...], pp_ref, pa_ref)
        dyp = (dm * gp).astype(BF16)
        dya = (dm * ga).astype(BF16)
        dyp_ref[...] = dyp
        dya_ref[...] = dya
        dzg_ref[:, :D_MODEL] = (dm * yp * (gp * (1.0 - gp))).astype(BF16)
        dzg_ref[:, D_MODEL:] = (dm * ya * (ga * (1.0 - ga))).astype(BF16)
        dpm_ref[...] = _dot_nt(dyp, pp_ref[...]).astype(BF16)
        do_ref[...] = _dot_nt(dya, pa_ref[...]).astype(BF16)

    return pl.pallas_call(
        body, name="mix_bwd", grid=(s_len // tm,),
        in_specs=[ANY, _rows(tm, D_MODEL), _rows(tm, 512), _rows(tm, 512), _rows(tm, 2048),
                  _whole((512, D_MODEL)), _whole((512, D_MODEL)), _whole((D_MODEL, D_MODEL))],
        out_specs=[_rows(tm, D_MODEL), _rows(tm, D_MODEL), _rows(tm, 2048), _rows(tm, 512), _rows(tm, 512)],
        out_shape=[jax.ShapeDtypeStruct((s_len, n), BF16) for n in (D_MODEL, D_MODEL, 2048, 512, 512)],
        compiler_params=_params(),
    )(after, dh1, pm, o, zg, p_pool, p_attn, w_out)


def _mixers_bwd(after, zp, q, kv, dpm, do, pool_w, pool_b, pool_scale, bias_t, sink_row):
    s_len = zp.shape[0]
    tq = min(512, s_len)
    nb = tq // BLOCK
    n_steps = s_len // tq

    def body(after_ref, zp_ref, zph_ref, q_ref, kv_ref, kvh_ref, dpm_ref, dpmh_ref, do_ref, pw_ref, pb_ref, ps_ref,
             bias_ref, sink_ref, dzp_ref, dq_ref, dkv_ref, dpw_ref, dpb_ref, dps_ref, dsk_ref, ka_ref, vd_ref, dsk_acc):
        i = pl.program_id(0)

        @pl.when(i == 0)
        def _():
            dkv_ref[...] = jnp.zeros_like(dkv_ref)
            dpw_ref[...] = jnp.zeros_like(dpw_ref)
            dpb_ref[...] = jnp.zeros_like(dpb_ref)
            dps_ref[...] = jnp.zeros_like(dps_ref)
            dsk_acc[...] = jnp.zeros_like(dsk_acc)

        cur = zp_ref[...].astype(F32)
        halo = zph_ref[...].astype(F32) * (i > 0).astype(F32)
        ext = jnp.concatenate([halo, cur], axis=0)
        dpm_next = dpmh_ref[...].astype(F32) * (i < n_steps - 1).astype(F32)
        dpm_ext = jnp.concatenate([dpm_ref[...].astype(F32), dpm_next], axis=0)
        n_ext = tq + POOL_HALO
        for g in range(4):
            sl = slice(g * POOL_GROUP_DIM, (g + 1) * POOL_GROUP_DIM)
            w = POOL_WINDOWS[g]
            d = _pool_d(ext[:, sl], cur[:, sl], g, i * tq).astype(BF16)
            y_lin = _dot(d, pw_ref[g]) + pb_ref[:, sl]
            dps_ref[:, sl] += jnp.sum(dpm_ext[:tq, sl] * y_lin, axis=0, keepdims=True)
            dyl_ext = dpm_ext[:, sl] * ps_ref[:, sl]
            dpb_ref[:, sl] += jnp.sum(dyl_ext[:tq], axis=0, keepdims=True)
            dyl_bf = dyl_ext.astype(BF16)
            dpw_ref[g] += _dot_tn(d, dyl_bf[:tq])
            dd = _dot_nt(dyl_bf, pw_ref[g])
            t = i * tq + lax.broadcasted_iota(jnp.int32, dd.shape, 0)
            e = dd / jnp.minimum(t + 1, w).astype(F32)
            acc = e
            k = 1
            while k < w:
                acc = acc + pltpu.roll(acc, n_ext - k, 0)
                k *= 2
            dzp_ref[:, sl] = (acc[:tq] - dd[:tq]).astype(BF16)

        _fill_kv_slabs(kvh_ref, kv_ref, ka_ref, vd_ref)

        def fold(dup):
            return dup + pltpu.roll(dup, HEAD_DIM, 1)

        for b in range(nb):
            rq = slice(b * BLOCK, (b + 1) * BLOCK)
            rk = slice(b * BLOCK, (b + 2) * BLOCK)
            qb = q_ref[rq, :]
            dob = do_ref[rq, :]
            dk_dup, dv_dup = [], []
            for h in range(N_KV_HEADS):
                kk = ka_ref[h, rk, :]
                q_st = _stack_pairs(qb, h)
                do_st = _stack_pairs(dob, h)
                pn, psink = _attn_probs_t(kk, q_st, bias_ref[h], sink_ref[h], (i == 0) if b == 0 else None)
                dp = _dot_nt(vd_ref[h, rk, :], do_st)
                delta = jnp.sum(pn * dp, axis=0, keepdims=True)
                dsk_acc[h] += -psink * delta
                ds = ((pn * (dp - delta)) * ATTN_SCALE).astype(BF16)
                dq_st = _dot_tn(ds, kk)
                for j in range(2):
                    left = _left_half((BLOCK, LANES))
                    dq_pair = jnp.where(left, dq_st[(2 * j) * BLOCK:(2 * j + 1) * BLOCK],
                                        dq_st[(2 * j + 1) * BLOCK:(2 * j + 2) * BLOCK])
                    dq_ref[rq, h * 256 + j * LANES:h * 256 + (j + 1) * LANES] = dq_pair.astype(BF16)
                dk_dup.append(fold(_dot(ds, q_st)))
                dv_dup.append(fold(_dot(pn.astype(BF16), do_st)))
            left = _left_half((2 * BLOCK, LANES))
            dkv_blk = jnp.concatenate([jnp.where(left, dk_dup[0], dk_dup[1]),
                                       jnp.where(left, dv_dup[0], dv_dup[1])], axis=1)
            g0 = pl.multiple_of(i * tq + b * BLOCK, BLOCK)
            dkv_ref[pl.ds(g0, 2 * BLOCK), :] += dkv_blk

        @pl.when(i == n_steps - 1)
        def _():
            for h in range(N_KV_HEADS):
                for g in range(GROUP):
                    tot = jnp.sum(dsk_acc[h, :, g * BLOCK:(g + 1) * BLOCK], axis=1, keepdims=True)
                    dsk_ref[GROUP * h + g:GROUP * h + g + 1, :] = jnp.broadcast_to(tot, (1, LANES))

    blocks_per_tile = tq // POOL_HALO
    last_halo = s_len // POOL_HALO - 1
    halo_prev = pl.BlockSpec((POOL_HALO, 512), lambda i: (jnp.maximum(i * blocks_per_tile - 1, 0), 0))
    halo_next = pl.BlockSpec((POOL_HALO, 512), lambda i: (jnp.minimum((i + 1) * blocks_per_tile, last_halo), 0))
    halo_kv = pl.BlockSpec((BLOCK, 256), lambda i: (jnp.maximum(i * nb - 1, 0), 0))
    return pl.pallas_call(
        body, name="mixers_bwd", grid=(n_steps,),
        in_specs=[ANY, _rows(tq, 512), halo_prev, _rows(tq, 512), _rows(tq, 256), halo_kv,
                  _rows(tq, 512), halo_next, _rows(tq, 512),
                  _whole((4, 128, 128)), _whole((1, 512)), _whole((1, 512)),
                  _whole((N_KV_HEADS, 2 * BLOCK, GROUP * BLOCK)), _whole((N_KV_HEADS, 1, GROUP * BLOCK))],
        out_specs=[_rows(tq, 512), _rows(tq, 512), _whole((s_len + BLOCK, 256)),
                   _whole((4, 128, 128)), _whole((1, 512)), _whole((1, 512)), _whole((8, LANES))],
        out_shape=[jax.ShapeDtypeStruct((s_len, 512), BF16), jax.ShapeDtypeStruct((s_len, 512), BF16),
                   jax.ShapeDtypeStruct((s_len + BLOCK, 256), F32), jax.ShapeDtypeStruct((4, 128, 128), F32),
                   jax.ShapeDtypeStruct((1, 512), F32), jax.ShapeDtypeStruct((1, 512), F32),
                   jax.ShapeDtypeStruct((8, LANES), F32)],
        scratch_shapes=[pltpu.VMEM((N_KV_HEADS, tq + BLOCK, LANES), BF16)] * 2
        + [pltpu.VMEM((N_KV_HEADS, 1, GROUP * BLOCK), F32)],
        compiler_params=_params(),
    )(after, zp, zp, q, kv, kv, dpm, dpm, do, pool_w, pool_b, pool_scale, bias_t, sink_row)


def _in_bwd(after, dzp, dq, dkv, dzg, w_in, x, dh1, g_mix):
    s_len = x.shape[0]
    tm = min(512, s_len)

    def body(after_ref, dzp_ref, dq_ref, dkv_ref, dzg_ref, w_ref, x_ref, dh1_ref, g_ref, dx_ref, dg_ref):
        i = pl.program_id(0)

        @pl.when(i == 0)
        def _():
            dg_ref[...] = jnp.zeros_like(dg_ref)

        du = _dot_nt(dzp_ref[...], w_ref[:, 0:512])
        du = du + _dot_nt(dq_ref[...], w_ref[:, 512:1024])
        du = du + _dot_nt(dkv_ref[...], w_ref[:, 1024:1280])
        du = du + _dot_nt(dzg_ref[...], w_ref[:, 1280:3328])
        r, xh, _ = _rms_fwd(x_ref[...], g_ref[...])
        dxn, dg = _rms_bwd(du, xh, r, g_ref[...])
        dg_ref[...] += dg
        dx_ref[...] = dh1_ref[...] + dxn

    return pl.pallas_call(
        body, name="in_bwd", grid=(s_len // tm,),
        in_specs=[ANY, _rows(tm, 512), _rows(tm, 512), _rows(tm, 256), _rows(tm, 2048), _whole((D_MODEL, IN_WIDTH)),
                  _rows(tm, D_MODEL), _rows(tm, D_MODEL), _whole((1, D_MODEL))],
        out_specs=[_rows(tm, D_MODEL), _whole((1, D_MODEL))],
        out_shape=[jax.ShapeDtypeStruct((s_len, D_MODEL), F32), jax.ShapeDtypeStruct((1, D_MODEL), F32)],
        compiler_params=_params(),
    )(after, dzp, dq, dkv, dzg, w_in, x, dh1, g_mix)


def _all_gather_weights(name, shards, after=None):
    n = len(shards)
    extra = [] if after is None else [after]
    n_extra = len(extra)

    def body(*refs):
        ins, outs = refs[:n], refs[n + n_extra:2 * n + n_extra]
        send_sems, recv_sems, local_sems = refs[2 * n + n_extra:]
        x, y, c = lax.axis_index("x"), lax.axis_index("y"), lax.axis_index("c")
        me, sibling = (x, y, c), (x, y, 1 - c)
        chips = [(1 - x, y), (x, 1 - y), (1 - x, 1 - y)]

        def slot(a, px, py, pc):
            return outs[a].at[4 * px + 2 * py + pc]

        def copy(a, k, block, to, src=None):
            return pltpu.make_async_remote_copy(
                src_ref=slot(a, *block) if src is None else src, dst_ref=slot(a, *block),
                send_sem=send_sems.at[a, k], recv_sem=recv_sems.at[a, k], device_id=to, device_id_type=MESH)

        mine = [pltpu.make_async_copy(ins[a], slot(a, *me), local_sems.at[a]) for a in range(n)]
        for cp in mine:
            cp.start()
        first = []
        for a in range(n):
            first.append(copy(a, 0, me, sibling, src=ins[a]))
            first += [copy(a, 1 + j, me, (*chip, c), src=ins[a]) for j, chip in enumerate(chips)]
        for cp in first:
            cp.start()
        passed = []
        for a in range(n):
            for j, chip in enumerate(chips):
                copy(a, 1 + j, (*chip, c), me).wait_recv()
                cp = copy(a, 4 + j, (*chip, c), sibling)
                cp.start()
                passed.append(cp)
        for a in range(n):
            copy(a, 0, sibling, me).wait_recv()
            for j, chip in enumerate(chips):
                copy(a, 4 + j, (*chip, 1 - c), me).wait_recv()
        for cp in first + passed:
            cp.wait_send()
        for cp in mine:
            cp.wait()

    return pl.pallas_call(
        body, name=name,
        in_specs=[ANY] * (n + n_extra), out_specs=[ANY] * n,
        out_shape=[jax.ShapeDtypeStruct((N_DEV,) + s.shape, s.dtype) for s in shards],
        scratch_shapes=[pltpu.SemaphoreType.DMA((n, 7)), pltpu.SemaphoreType.DMA((n, 7)), pltpu.SemaphoreType.DMA((n,))],
    )(*shards, *extra)


HBM_SPEC = pl.BlockSpec(memory_space=pltpu.HBM)
SEM_SPEC = pl.BlockSpec(memory_space=pltpu.SEMAPHORE)
DATAFLOW = pltpu.SideEffectType.DATAFLOW_SIDE_EFFECTING
N_PEERS = N_DEV - 1


def _peer_copies(srcs, lands, scatter, send_sems, recv_sems):
    x, y, c = lax.axis_index("x"), lax.axis_index("y"), lax.axis_index("c")
    me_idx = 4 * x + 2 * y + c
    copies = []
    for k in range(1, N_DEV):
        px = 1 - x if (k >> 2) & 1 else x
        py = 1 - y if (k >> 1) & 1 else y
        pc = 1 - c if k & 1 else c
        p_idx = 4 * px + 2 * py + pc
        for a in range(len(srcs)):
            src = srcs[a].at[p_idx] if scatter[a] else srcs[a]
            dst = lands[a].at[k] if scatter[a] else lands[a].at[me_idx]
            copies.append(pltpu.make_async_remote_copy(
                src_ref=src, dst_ref=dst, send_sem=send_sems.at[a * N_PEERS + k - 1],
                recv_sem=recv_sems.at[a * N_PEERS + k - 1],
                device_id=(px, py, pc), device_id_type=MESH))
    return copies


def _exchange_start(name, srcs, scatter, after):
    n = len(srcs)
    lands = [lax.empty(s.shape if sc else (N_DEV,) + s.shape, s.dtype) for s, sc in zip(srcs, scatter)]

    def body(*refs):
        src_refs, land_refs = refs[:n], refs[n:2 * n]
        send_sems, recv_sems = refs[2 * n + 1], refs[2 * n + 2]
        token = refs[4 * n + 3]
        for cp in _peer_copies(src_refs, land_refs, scatter, send_sems, recv_sems):
            cp.start()
        token[...] = jnp.zeros_like(token)

    hbm = lambda t: pltpu.HBM(t.shape, t.dtype)
    outs = pl.pallas_call(
        body, name=name,
        out_shape=[pltpu.SemaphoreType.DMA((n * N_PEERS,)), pltpu.SemaphoreType.DMA((n * N_PEERS,))]
        + [hbm(t) for t in srcs] + [hbm(t) for t in lands] + [jax.ShapeDtypeStruct((8, LANES), F32)],
        in_specs=[HBM_SPEC] * (2 * n) + [ANY],
        out_specs=[SEM_SPEC, SEM_SPEC] + [HBM_SPEC] * (2 * n) + [pl.BlockSpec(memory_space=pltpu.VMEM)],
        input_output_aliases={i: 2 + i for i in range(2 * n)},
        compiler_params=pltpu.CompilerParams(has_side_effects=DATAFLOW),
    )(*[pltpu.with_memory_space_constraint(t, pltpu.HBM) for t in list(srcs) + lands], after)
    return dict(n=n, scatter=scatter, send_sems=outs[0], recv_sems=outs[1], srcs=outs[2:2 + n],
                lands=outs[2 + n:2 + 2 * n], token=outs[2 + 2 * n])


def _exchange_wait(name, handle, after):
    n, scatter = handle["n"], handle["scatter"]

    def body(*refs):
        src_refs, land_refs = refs[:n], refs[n:2 * n]
        send_sems, recv_sems = refs[2 * n], refs[2 * n + 1]
        for cp in _peer_copies(src_refs, land_refs, scatter, send_sems, recv_sems):
            cp.wait_send()
            cp.wait_recv()

    both = list(handle["srcs"]) + list(handle["lands"])
    outs = pl.pallas_call(
        body, name=name,
        out_shape=[pltpu.HBM(t.shape, t.dtype) for t in both],
        in_specs=[HBM_SPEC] * (2 * n) + [SEM_SPEC, SEM_SPEC, ANY],
        out_specs=[HBM_SPEC] * (2 * n),
        input_output_aliases={i: i for i in range(2 * n)},
        compiler_params=pltpu.CompilerParams(has_side_effects=DATAFLOW),
    )(*both, handle["send_sems"], handle["recv_sems"], after)
    me_idx = _my_index()
    lands = [land if sc else lax.dynamic_update_index_in_dim(land, src, me_idx, 0)
             for land, src, sc in zip(outs[n:], outs[:n], scatter)]
    return lands, outs[:n]


def _my_index():
    return 4 * lax.axis_index("x") + 2 * lax.axis_index("y") + lax.axis_index("c")


def _adamw(parts, w, m, v, sent=None):
    r, c = w.shape
    tr = 256 if r % 256 == 0 else r
    own = sent is not None

    def body(*refs):
        if own:
            _, p_ref, own_ref, w_ref, m_ref, v_ref, g_ref, d_ref, nm_ref, nv_ref = refs
            g = own_ref[...].astype(F32)
        else:
            p_ref, w_ref, m_ref, v_ref, g_ref, d_ref, nm_ref, nv_ref = refs
            g = p_ref[0].astype(F32)
        for k in range(1, N_DEV):
            g = g + p_ref[k].astype(F32)
        m_new = ADAM_B1 * m_ref[...] + (1.0 - ADAM_B1) * g
        v_new = ADAM_B2 * v_ref[...] + (1.0 - ADAM_B2) * (g * g)
        m_hat = m_new / (1.0 - ADAM_B1 ** ADAM_STEP)
        v_hat = v_new / (1.0 - ADAM_B2 ** ADAM_STEP)
        g_ref[...] = g
        d_ref[...] = -ADAM_LR * (m_hat / (jnp.sqrt(v_hat) + ADAM_EPS) + ADAM_WD * w_ref[...])
        nm_ref[...] = m_new
        nv_ref[...] = v_new

    out_shape = [jax.ShapeDtypeStruct((r, c), F32)] * 4
    if not own:
        return pl.pallas_call(
            body, name="adamw", grid=(r // tr,),
            in_specs=[pl.BlockSpec((N_DEV, tr, c), lambda i: (0, i, 0))] + [_rows(tr, c)] * 3,
            out_specs=[_rows(tr, c)] * 4, out_shape=out_shape, compiler_params=_params(),
        )(parts, w, m, v)
    rows = pl.BlockSpec((tr, c), lambda i, me: (i, 0))
    return pl.pallas_call(
        body, name="adamw_own", out_shape=out_shape, compiler_params=_params(),
        grid_spec=pltpu.PrefetchScalarGridSpec(
            num_scalar_prefetch=1, grid=(r // tr,),
            in_specs=[pl.BlockSpec((N_DEV, tr, c), lambda i, me: (0, i, 0)),
                      pl.BlockSpec((None, tr, c), lambda i, me: (me[0], i, 0))] + [rows] * 3,
            out_specs=[rows] * 4),
    )(_my_index().reshape(1).astype(jnp.int32), parts, sent, w, m, v)


SMALL_LAYOUT = (("norm_mix", 8), ("pool_w", 512), ("pool_b", 8), ("pool_scale", 8), ("attn_sinks", 8),
                ("norm_mlp", 8), ("norm_final", 8), ("loss", 8))
SMALL_ROWS = sum(n for _, n in SMALL_LAYOUT)


def _pack_small(vals, layout=SMALL_LAYOUT):
    rows = []
    for name, n_rows in layout:
        flat = vals[name].astype(F32).reshape(-1)
        flat = jnp.pad(flat, (0, n_rows * LANES - flat.shape[0]))
        rows.append(flat.reshape(n_rows, LANES))
    return jnp.concatenate(rows, axis=0)


def _unpack_small(pack, shapes):
    out, r0 = {}, 0
    for name, n_rows in SMALL_LAYOUT:
        size = int(np.prod(shapes[name])) if shapes[name] else 1
        out[name] = pack[r0:r0 + n_rows].reshape(-1)[:size].reshape(shapes[name])
        r0 += n_rows
    return out


def kernel(x, norm_mix, w_in, pool_w, pool_b, pool_scale, attn_sinks, p_pool, p_attn, w_out, norm_mlp, w_up, w_down, norm_final, loss_target, m_norm_mix, m_w_in, m_pool_w, m_pool_b, m_pool_scale, m_attn_sinks, m_p_pool, m_p_attn, m_w_out, m_norm_mlp, m_w_up, m_w_down, m_norm_final, v_norm_mix, v_w_in, v_pool_w, v_pool_b, v_pool_scale, v_attn_sinks, v_p_pool, v_p_attn, v_w_out, v_norm_mlp, v_w_up, v_w_down, v_norm_final):
    xs = x[0]
    tgt = loss_target[0]
    s_len = xs.shape[0]

    w_in_bf, p_pool_bf, p_attn_bf, w_out_bf, w_up_bf, w_down_bf = [
        t[0].astype(BF16) for t in (w_in, p_pool, p_attn, w_out, w_up, w_down)]
    (w_in_g,) = _all_gather_weights("all_gather_w_in", [w_in_bf])
    ag_proj = _exchange_start("ag_proj_start", [p_pool_bf, p_attn_bf, w_out_bf], (False,) * 3, w_in_g)
    ag_mlp = _exchange_start("ag_mlp_start", [w_up_bf, w_down_bf], (False,) * 2, ag_proj["token"])
    w_in_f = w_in_g.transpose(1, 0, 2).reshape(D_MODEL, IN_WIDTH)

    pool_w_bf = pool_w[0].astype(BF16)
    pool_b_row = pool_b[0].reshape(1, POOL_WIDTH)
    bias_t, sink_row = _attn_constants(attn_sinks[0])

    u, zp, q, kv, zg = _fwd_in(ag_mlp["token"], xs, norm_mix, w_in_f)
    pm, o = _mixers_fwd(zp, q, kv, pool_w_bf, pool_b_row, pool_scale, bias_t, sink_row)
    (p_pool_g, p_attn_g, w_out_g), _ = _exchange_wait("ag_proj_wait", ag_proj, pm)
    p_pool_f = p_pool_g.transpose(1, 0, 2).reshape(POOL_WIDTH, D_MODEL)
    p_attn_f = p_attn_g.transpose(1, 0, 2).reshape(ATTN_WIDTH, D_MODEL)
    w_out_f = w_out_g.reshape(D_MODEL, D_MODEL)
    h1, mixed = _mix_out(xs, pm, o, zg, p_pool_f, p_attn_f, w_out_f)
    (w_up_g, w_down_g), _ = _exchange_wait("ag_mlp_wait", ag_mlp, h1)
    w_up_f = w_up_g.transpose(1, 0, 2).reshape(D_MODEL, D_FF)
    w_down_f = w_down_g.reshape(D_FF, D_MODEL)
    dh1, a, dapre, u2, dh2, loss_part, g_norm_mlp, g_norm_final = _mlp_loss(
        h1, tgt, norm_mlp, norm_final.reshape(1, D_MODEL), w_up_f, w_down_f)
    gw_down = _tn_matmul(a, dh2, square_a=True)
    gw_up = _tn_matmul(u2, dapre)
    ex_mlp = _exchange_start(
        "ex_mlp_start", [gw_up.reshape(D_MODEL, N_DEV, D_FF // N_DEV).transpose(1, 0, 2),
                         gw_down.reshape(N_DEV, D_FF // N_DEV, D_MODEL)], (True, True), gw_down)
    dyp, dya, dzg, dpm, do = _mix_bwd(ex_mlp["token"], dh1, pm, o, zg, p_pool_f, p_attn_f, w_out_f)
    gw_out = _tn_matmul(mixed, dh1)
    gp_pool = _tn_matmul(pm, dyp)
    gp_attn = _tn_matmul(o, dya)
    ex_proj = _exchange_start(
        "ex_proj_start", [gp_pool.reshape(POOL_WIDTH, N_DEV, D_MODEL // N_DEV).transpose(1, 0, 2),
                          gp_attn.reshape(ATTN_WIDTH, N_DEV, D_MODEL // N_DEV).transpose(1, 0, 2),
                          gw_out.reshape(N_DEV, D_MODEL // N_DEV, D_MODEL)], (True,) * 3, gw_out)
    dzp, dq, dkv_acc, g_pool_w, g_pool_b, g_pool_scale, g_sinks = _mixers_bwd(
        ex_proj["token"], zp, q, kv, dpm, do, pool_w_bf, pool_b_row, pool_scale, bias_t, sink_row)
    dkv = dkv_acc[BLOCK:].astype(BF16)
    gw_in = jnp.concatenate(
        [_tn_matmul(u, dzp), _tn_matmul(u, dq), _tn_matmul(u, dkv), _tn_matmul(u, dzg)], axis=1)
    small_vals = dict(pool_w=g_pool_w, pool_b=g_pool_b, pool_scale=g_pool_scale, attn_sinks=g_sinks[:, 0],
                      norm_mlp=g_norm_mlp, norm_final=g_norm_final, loss=loss_part[0, 0])
    ex_in = _exchange_start(
        "ex_in_start", [gw_in.reshape(D_MODEL, N_DEV, IN_WIDTH // N_DEV).transpose(1, 0, 2),
                        _pack_small(small_vals, SMALL_LAYOUT[1:])], (True, False), gw_in)
    dx, g_norm_mix = _in_bwd(ex_in["token"], dzp, dq, dkv, dzg, w_in_f, xs, dh1, norm_mix)

    big_w = dict(w_in=w_in, p_pool=p_pool, p_attn=p_attn, w_out=w_out, w_up=w_up, w_down=w_down)
    big_m = dict(w_in=m_w_in, p_pool=m_p_pool, p_attn=m_p_attn, w_out=m_w_out, w_up=m_w_up, w_down=m_w_down)
    big_v = dict(w_in=v_w_in, p_pool=v_p_pool, p_attn=v_p_attn, w_out=v_w_out, w_up=v_w_up, w_down=v_w_down)
    res = {}

    def update(names, recvs, sents):
        for name, parts, sent in zip(names, recvs, sents):
            outs = _adamw(parts, big_w[name][0], big_m[name][0], big_v[name][0], sent)
            res[name] = [t[None] for t in outs]

    update(["w_up", "w_down"], *_exchange_wait("ex_mlp_wait", ex_mlp, dx))
    update(["p_pool", "p_attn", "w_out"], *_exchange_wait("ex_proj_wait", ex_proj, res["w_down"][0]))
    (norm_mix_all,) = _all_gather_weights(
        "all_gather_norm_mix", [_pack_small(dict(norm_mix=g_norm_mix), SMALL_LAYOUT[:1])], res["w_out"][0])
    (r_in, small_rest), (s_in, _) = _exchange_wait("ex_in_wait", ex_in, norm_mix_all)
    update(["w_in"], [r_in], [s_in])
    small_all = jnp.concatenate([norm_mix_all, small_rest], axis=1)

    small_names = ["norm_mix", "pool_w", "pool_b", "pool_scale", "attn_sinks", "norm_mlp", "norm_final"]
    small_w = dict(norm_mix=norm_mix, pool_w=pool_w, pool_b=pool_b, pool_scale=pool_scale, attn_sinks=attn_sinks,
                   norm_mlp=norm_mlp, norm_final=norm_final)
    small_m = dict(norm_mix=m_norm_mix, pool_w=m_pool_w, pool_b=m_pool_b, pool_scale=m_pool_scale,
                   attn_sinks=m_attn_sinks, norm_mlp=m_norm_mlp, norm_final=m_norm_final)
    small_v = dict(norm_mix=v_norm_mix, pool_w=v_pool_w, pool_b=v_pool_b, pool_scale=v_pool_scale,
                   attn_sinks=v_attn_sinks, norm_mlp=v_norm_mlp, norm_final=v_norm_final)
    zero = jnp.zeros((), F32)
    packs = [_pack_small({**d, "loss": zero}) for d in (small_w, small_m, small_v)]
    s_outs = _adamw(small_all, *packs)
    shapes = {k: small_w[k].shape for k in small_names}
    shapes["loss"] = ()
    s_res = [_unpack_small(t, shapes) for t in s_outs]
    loss = s_res[0]["loss"]
    for name in small_names:
        res[name] = [t[name] for t in s_res]

    order = ["norm_mix", "w_in", "pool_w", "pool_b", "pool_scale", "attn_sinks", "p_pool", "p_attn", "w_out",
             "norm_mlp", "w_up", "w_down", "norm_final"]
    out = [loss, dx[None]]
    for kind in range(4):
        out += [res[name][kind] for name in order]
    return tuple(out)
```

```python
import functools
import math

import numpy as np
import jax
import jax.numpy as jnp
from jax import lax
from jax.experimental import pallas as pl
from jax.experimental.pallas import tpu as pltpu

F32 = jnp.float32
BF16 = jnp.bfloat16

D_MODEL = 1024
POOL_WIDTH = 512
ATTN_WIDTH = 512
KV_WIDTH = 128
HEAD_DIM = 64
N_HEADS = 8
N_KV_HEADS = 2
GROUP = 4
BLOCK = 128
POOL_WINDOWS = (2, 4, 8, 16)
POOL_GROUP_DIM = 128
POOL_HALO = 16
D_FF = 4096
FF_CHUNK = 1024
IN_WIDTH = 3328
RMS_EPS = 1e-5
NEG_INF = -1e30
ATTN_SCALE = 1.0 / math.sqrt(HEAD_DIM)
N_DEV = 8

ADAM_LR = 0.001
ADAM_B1 = 0.9
ADAM_B2 = 0.999
ADAM_EPS = 1e-08
ADAM_WD = 0.01
ADAM_STEP = 10

LANES = 128
VMEM_LIMIT_BYTES = 56 * 1024 * 1024
MESH = pl.DeviceIdType.MESH


def _params(n_grid_axes=1):
    return pltpu.CompilerParams(
        dimension_semantics=("arbitrary",) * n_grid_axes, vmem_limit_bytes=VMEM_LIMIT_BYTES)


def _dot(a, b):
    return jnp.dot(a, b, preferred_element_type=F32)


def _dot_nt(a, b):
    return lax.dot_general(a, b, (((1,), (1,)), ((), ())), preferred_element_type=F32)


def _dot_tn(a, b):
    return lax.dot_general(a, b, (((0,), (0,)), ((), ())), preferred_element_type=F32)


ANY = pl.BlockSpec(memory_space=pl.ANY)


def _rows(tm, n):
    return pl.BlockSpec((tm, n), lambda i: (i, 0))


def _whole(shape):
    zeros = (0,) * len(shape)
    return pl.BlockSpec(shape, lambda i: zeros)


def _rms_fwd(h, g):
    r = lax.rsqrt(jnp.mean(h * h, axis=-1, keepdims=True) + RMS_EPS)
    xh = h * r
    return r, xh, xh * g


def _rms_bwd(dy, xh, r, g):
    dxh = dy * g
    dh = r * (dxh - xh * jnp.mean(dxh * xh, axis=-1, keepdims=True))
    return dh, jnp.sum(dy * xh, axis=0, keepdims=True)


def _fwd_in(after, x, g_mix, w_in):
    s_len = x.shape[0]
    tm = min(512, s_len)

    def body(after_ref, x_ref, g_ref, w_ref, u_ref, zp_ref, q_ref, kv_ref, zg_ref):
        _, _, u = _rms_fwd(x_ref[...], g_ref[...])
        u = u.astype(BF16)
        u_ref[...] = u
        zp_ref[...] = _dot(u, w_ref[:, 0:512]).astype(BF16)
        q_ref[...] = _dot(u, w_ref[:, 512:1024]).astype(BF16)
        kv_ref[...] = _dot(u, w_ref[:, 1024:1280]).astype(BF16)
        zg_ref[...] = _dot(u, w_ref[:, 1280:3328]).astype(BF16)

    return pl.pallas_call(
        body, name="fwd_in", grid=(s_len // tm,),
        in_specs=[ANY, _rows(tm, D_MODEL), _whole((1, D_MODEL)), _whole((D_MODEL, IN_WIDTH))],
        out_specs=[_rows(tm, D_MODEL), _rows(tm, 512), _rows(tm, 512), _rows(tm, 256), _rows(tm, 2048)],
        out_shape=[jax.ShapeDtypeStruct((s_len, n), BF16) for n in (D_MODEL, 512, 512, 256, 2048)],
        compiler_params=_params(),
    )(after, x, g_mix, w_in)


def _attn_constants(sinks):
    qi = np.arange(BLOCK)[:, None]
    kj = np.arange(2 * BLOCK)[None, :]
    dist = BLOCK + qi - kj
    valid = (dist >= 0) & (dist < BLOCK)
    slopes = np.array([2.0 ** (-8.0 * (h + 1) / N_HEADS) for h in range(N_HEADS)], dtype=np.float32)
    bias = np.where(valid[None], -slopes[:, None, None] * dist.astype(np.float32)[None], np.float32(NEG_INF))
    bias = bias.astype(np.float32).reshape(N_KV_HEADS, GROUP * BLOCK, 2 * BLOCK).transpose(0, 2, 1)
    sink_row = jnp.repeat(sinks.astype(F32).reshape(N_KV_HEADS, GROUP), BLOCK, axis=1)[:, None, :]
    return jnp.asarray(np.ascontiguousarray(bias)), sink_row


def _left_half(shape):
    return lax.broadcasted_iota(jnp.int32, shape, 1) < HEAD_DIM


def _dup_halves(slab):
    swapped = pltpu.roll(slab, HEAD_DIM, 1)
    left = _left_half(slab.shape)
    return jnp.where(left, slab, swapped), jnp.where(left, swapped, slab)


def _fill_kv_slabs(kvh_ref, kv_ref, ka_ref, vd_ref):
    for rows, src in ((slice(0, BLOCK), kvh_ref), (slice(BLOCK, None), kv_ref)):
        kvf = src[...].astype(F32)
        for ref, lanes in ((ka_ref, slice(0, KV_WIDTH)), (vd_ref, slice(KV_WIDTH, 2 * KV_WIDTH))):
            d0, d1 = _dup_halves(kvf[:, lanes])
            ref[0, rows, :] = d0.astype(BF16)
            ref[1, rows, :] = d1.astype(BF16)


def _stack_pairs(a, h):
    pieces = []
    for j in range(2):
        pair = a[:, h * 256 + j * LANES:h * 256 + (j + 1) * LANES]
        left = _left_half(pair.shape)
        zero = jnp.zeros_like(pair)
        pieces += [jnp.where(left, pair, zero), jnp.where(left, zero, pair)]
    return jnp.concatenate(pieces, axis=0)


def _attn_probs_t(kk, q_st, bias_t, sink_row, first):
    s = _dot_nt(kk, q_st) * ATTN_SCALE + bias_t
    if first is not None:
        row = lax.broadcasted_iota(jnp.int32, s.shape, 0)
        s = jnp.where(jnp.logical_and(first, row < BLOCK), NEG_INF, s)
    m = jnp.maximum(jnp.max(s, axis=0, keepdims=True), sink_row)
    p = jnp.exp(s - m)
    es = jnp.exp(sink_row - m)
    inv = 1.0 / (jnp.sum(p, axis=0, keepdims=True) + es)
    return p * inv, es * inv


def _pool_d(ext, cur, g, row0):
    w = POOL_WINDOWS[g]
    acc = ext
    k = 1
    while k < w:
        acc = acc + pltpu.roll(acc, k, 0)
        k *= 2
    t = row0 + lax.broadcasted_iota(jnp.int32, cur.shape, 0)
    cnt = jnp.minimum(t + 1, w).astype(F32)
    return acc[POOL_HALO:, :] / cnt - cur


def _mixers_fwd(zp, q, kv, pool_w, pool_b, pool_scale, bias_t, sink_row):
    s_len = zp.shape[0]
    tq = min(512, s_len)
    nb = tq // BLOCK

    def body(zp_ref, zph_ref, q_ref, kv_ref, kvh_ref, pw_ref, pb_ref, ps_ref, bias_ref, sink_ref,
             pm_ref, o_ref, ka_ref, vd_ref):
        i = pl.program_id(0)
        cur = zp_ref[...].astype(F32)
        halo = zph_ref[...].astype(F32) * (i > 0).astype(F32)
        ext = jnp.concatenate([halo, cur], axis=0)
        for g in range(4):
            sl = slice(g * POOL_GROUP_DIM, (g + 1) * POOL_GROUP_DIM)
            d = _pool_d(ext[:, sl], cur[:, sl], g, i * tq)
            y = _dot(d.astype(BF16), pw_ref[g]) + pb_ref[:, sl]
            pm_ref[:, sl] = (y * ps_ref[:, sl]).astype(BF16)
        _fill_kv_slabs(kvh_ref, kv_ref, ka_ref, vd_ref)
        for b in range(nb):
            rq = slice(b * BLOCK, (b + 1) * BLOCK)
            rk = slice(b * BLOCK, (b + 2) * BLOCK)
            qb = q_ref[rq, :]
            for h in range(N_KV_HEADS):
                pn, _ = _attn_probs_t(ka_ref[h, rk, :], _stack_pairs(qb, h), bias_ref[h], sink_ref[h],
                                      (i == 0) if b == 0 else None)
                pn = pn.astype(BF16)
                vd = vd_ref[h, rk, :]
                left = _left_half(vd.shape)
                zero = jnp.zeros_like(vd)
                va, vb = jnp.where(left, vd, zero), jnp.where(left, zero, vd)
                for j in range(2):
                    o_pair = (_dot_tn(pn[:, (2 * j) * BLOCK:(2 * j + 1) * BLOCK], va)
                              + _dot_tn(pn[:, (2 * j + 1) * BLOCK:(2 * j + 2) * BLOCK], vb))
                    o_ref[rq, h * 256 + j * LANES:h * 256 + (j + 1) * LANES] = o_pair.astype(BF16)

    halo_pool = pl.BlockSpec((POOL_HALO, 512), lambda i: (jnp.maximum(i * (tq // POOL_HALO) - 1, 0), 0))
    halo_kv = pl.BlockSpec((BLOCK, 256), lambda i: (jnp.maximum(i * nb - 1, 0), 0))
    return pl.pallas_call(
        body, name="mixers_fwd", grid=(s_len // tq,),
        in_specs=[_rows(tq, 512), halo_pool, _rows(tq, 512), _rows(tq, 256), halo_kv,
                  _whole((4, 128, 128)), _whole((1, 512)), _whole((1, 512)),
                  _whole((N_KV_HEADS, 2 * BLOCK, GROUP * BLOCK)), _whole((N_KV_HEADS, 1, GROUP * BLOCK))],
        out_specs=[_rows(tq, 512), _rows(tq, 512)],
        out_shape=[jax.ShapeDtypeStruct((s_len, 512), BF16)] * 2,
        scratch_shapes=[pltpu.VMEM((N_KV_HEADS, tq + BLOCK, LANES), BF16)] * 2,
        compiler_params=_params(),
    )(zp, zp, q, kv, kv, pool_w, pool_b, pool_scale, bias_t, sink_row)


def _gated_mix(pm, o, zg, pp_ref, pa_ref):
    yp = _dot(pm, pp_ref[...])
    ya = _dot(o, pa_ref[...])
    gp = jax.nn.sigmoid(zg[:, :D_MODEL].astype(F32))
    ga = jax.nn.sigmoid(zg[:, D_MODEL:].astype(F32))
    return yp, ya, gp, ga


def _mix_out(x, pm, o, zg, p_pool, p_attn, w_out):
    s_len = x.shape[0]
    tm = min(512, s_len)

    def body(x_ref, pm_ref, o_ref, zg_ref, pp_ref, pa_ref, wo_ref, h1_ref, mixed_ref):
        yp, ya, gp, ga = _gated_mix(pm_ref[...], o_ref[...], zg_ref[...], pp_ref, pa_ref)
        mixed = (gp * yp + ga * ya).astype(BF16)
        mixed_ref[...] = mixed
        h1_ref[...] = x_ref[...] + _dot(mixed, wo_ref[...])

    return pl.pallas_call(
        body, name="mix_out", grid=(s_len // tm,),
        in_specs=[_rows(tm, D_MODEL), _rows(tm, 512), _rows(tm, 512), _rows(tm, 2048),
                  _whole((512, D_MODEL)), _whole((512, D_MODEL)), _whole((D_MODEL, D_MODEL))],
        out_specs=[_rows(tm, D_MODEL), _rows(tm, D_MODEL)],
        out_shape=[jax.ShapeDtypeStruct((s_len, D_MODEL), F32), jax.ShapeDtypeStruct((s_len, D_MODEL), BF16)],
        compiler_params=_params(),
    )(x, pm, o, zg, p_pool, p_attn, w_out)


def _mlp_loss(h1, tgt, g_mlp, g_fin, w_up_blocks, w_down):
    s_len = h1.shape[0]
    tm = min(256, s_len)
    n_chunks = D_FF // FF_CHUNK
    up_block = D_FF // N_DEV
    per_chunk = FF_CHUNK // up_block

    def body(h1_ref, tgt_ref, gm_ref, gf_ref, wu_ref, wd_ref,
             dh1_ref, a_ref, dap_ref, u2_ref, dh2_ref, loss_ref, dgm_ref, dgf_ref):
        i = pl.program_id(0)

        @pl.when(i == 0)
        def _():
            loss_ref[...] = jnp.zeros_like(loss_ref)
            dgm_ref[...] = jnp.zeros_like(dgm_ref)
            dgf_ref[...] = jnp.zeros_like(dgf_ref)

        h1 = h1_ref[...]
        r2, xh2, u2 = _rms_fwd(h1, gm_ref[...])
        u2 = u2.astype(BF16)
        u2_ref[...] = u2
        acc = jnp.zeros((tm, D_MODEL), F32)
        for c in range(n_chunks):
            cs = slice(c * FF_CHUNK, (c + 1) * FF_CHUNK)
            a = jnp.concatenate([_dot(u2, wu_ref[per_chunk * c + j]) for j in range(per_chunk)], axis=1)
            a = jnp.maximum(a, 0.0)
            a_ref[:, cs] = a.astype(BF16)
            acc = acc + _dot((a * a).astype(BF16), wd_ref[cs, :])
        h2 = h1 + acc
        r3, xh3, y = _rms_fwd(h2, gf_ref[...])
        diff = y - tgt_ref[...]
        loss_ref[...] += 0.5 * jnp.sum(jnp.mean(diff * diff, axis=-1, keepdims=True))
        dy = diff * (1.0 / D_MODEL)
        dh2, dgf = _rms_bwd(dy, xh3, r3, gf_ref[...])
        dgf_ref[...] += dgf
        dh2_bf = dh2.astype(BF16)
        dh2_ref[...] = dh2_bf
        du2 = jnp.zeros((tm, D_MODEL), F32)
        for c in range(n_chunks):
            cs = slice(c * FF_CHUNK, (c + 1) * FF_CHUNK)
            ds = _dot_nt(dh2_bf, wd_ref[cs, :])
            dap = (ds * (2.0 * a_ref[:, cs].astype(F32))).astype(BF16)
            dap_ref[:, cs] = dap
            for j in range(per_chunk):
                du2 = du2 + _dot_nt(dap[:, j * up_block:(j + 1) * up_block], wu_ref[per_chunk * c + j])
        dh1n, dgm = _rms_bwd(du2, xh2, r2, gm_ref[...])
        dgm_ref[...] += dgm
        dh1_ref[...] = dh2 + dh1n

    single = dict(pipeline_mode=pl.Buffered(1))
    return pl.pallas_call(
        body, name="mlp_loss", grid=(s_len // tm,),
        in_specs=[_rows(tm, D_MODEL), _rows(tm, D_MODEL), _whole((1, D_MODEL)), _whole((1, D_MODEL)),
                  pl.BlockSpec((N_DEV, D_MODEL, up_block), lambda i: (0, 0, 0), **single),
                  pl.BlockSpec((D_FF, D_MODEL), lambda i: (0, 0), **single)],
        out_specs=[_rows(tm, D_MODEL), _rows(tm, D_FF), _rows(tm, D_FF), _rows(tm, D_MODEL), _rows(tm, D_MODEL),
                   _whole((8, LANES)), _whole((1, D_MODEL)), _whole((1, D_MODEL))],
        out_shape=[jax.ShapeDtypeStruct((s_len, D_MODEL), F32), jax.ShapeDtypeStruct((s_len, D_FF), BF16),
                   jax.ShapeDtypeStruct((s_len, D_FF), BF16), jax.ShapeDtypeStruct((s_len, D_MODEL), BF16),
                   jax.ShapeDtypeStruct((s_len, D_MODEL), BF16), jax.ShapeDtypeStruct((8, LANES), F32),
                   jax.ShapeDtypeStruct((1, D_MODEL), F32), jax.ShapeDtypeStruct((1, D_MODEL), F32)],
        compiler_params=_params(),
    )(h1, tgt, g_mlp, g_fin, w_up_blocks, w_down)


def _tn_matmul(a, b, square_a=False, col_blocks=None):
    s_len, ka = a.shape
    nb = b.shape[1]
    tt = min(1024, s_len)
    tk = min(1024, ka)
    tn = min(1024, nb)
    n_t = s_len // tt
    if col_blocks is None:
        out_spec = pl.BlockSpec((tk, tn), lambda k, j, t: (k, j))
        out_shape = jax.ShapeDtypeStruct((ka, nb), BF16)
    else:
        width = nb // col_blocks
        per_tile = tn // width
        out_spec = pl.BlockSpec((per_tile, tk, width), lambda k, j, t: (j, k, 0))
        out_shape = jax.ShapeDtypeStruct((col_blocks, ka, width), BF16)

    def body(a_ref, b_ref, o_ref, acc_ref):
        t = pl.program_id(2)

        @pl.when(t == 0)
        def _():
            acc_ref[...] = jnp.zeros_like(acc_ref)

        av = a_ref[...]
        if square_a:
            av = av * av
        acc_ref[...] += _dot_tn(av.astype(BF16), b_ref[...].astype(BF16))

        @pl.when(t == n_t - 1)
        def _():
            if col_blocks is None:
                o_ref[...] = acc_ref[...].astype(o_ref.dtype)
            else:
                for blk in range(per_tile):
                    o_ref[blk] = acc_ref[:, blk * width:(blk + 1) * width].astype(o_ref.dtype)

    return pl.pallas_call(
        body, name="tn_matmul", grid=(ka // tk, nb // tn, n_t),
        in_specs=[pl.BlockSpec((tt, tk), lambda k, j, t: (t, k)), pl.BlockSpec((tt, tn), lambda k, j, t: (t, j))],
        out_specs=out_spec, out_shape=out_shape,
        scratch_shapes=[pltpu.VMEM((tk, tn), F32)],
        compiler_params=_params(3),
    )(a, b)


def _mix_bwd(after, dh1, pm, o, zg, p_pool, p_attn, w_out):
    s_len = dh1.shape[0]
    tm = min(512, s_len)

    def body(after_ref, dh1_ref, pm_ref, o_ref, zg_ref, pp_ref, pa_ref, wo_ref,
             dyp_ref, dya_ref, dzg_ref, dpm_ref, do_ref):
        dm = _dot_nt(dh1_ref[...].astype(BF16), wo_ref[...])
        yp, ya, gp, ga = _gated_mix(pm_ref[...], o_ref[...], zg_ref[...], pp_ref, pa_ref)
        dyp = (dm * gp).astype(BF16)
        dya = (dm * ga).astype(BF16)
        dyp_ref[...] = dyp
        dya_ref[...] = dya
        dzg_ref[:, :D_MODEL] = (dm * yp * (gp * (1.0 - gp))).astype(BF16)
        dzg_ref[:, D_MODEL:] = (dm * ya * (ga * (1.0 - ga))).astype(BF16)
        dpm_ref[...] = _dot_nt(dyp, pp_ref[...]).astype(BF16)
        do_ref[...] = _dot_nt(dya, pa_ref[...]).astype(BF16)

    return pl.pallas_call(
        body, name="mix_bwd", grid=(s_len // tm,),
        in_specs=[ANY, _rows(tm, D_MODEL), _rows(tm, 512), _rows(tm, 512), _rows(tm, 2048),
                  _whole((512, D_MODEL)), _whole((512, D_MODEL)), _whole((D_MODEL, D_MODEL))],
        out_specs=[_rows(tm, D_MODEL), _rows(tm, D_MODEL), _rows(tm, 2048), _rows(tm, 512), _rows(tm, 512)],
        out_shape=[jax.ShapeDtypeStruct((s_len, n), BF16) for n in (D_MODEL, D_MODEL, 2048, 512, 512)],
        compiler_params=_params(),
    )(after, dh1, pm, o, zg, p_pool, p_attn, w_out)


def _mixers_bwd(after, zp, q, kv, dpm, do, pool_w, pool_b, pool_scale, bias_t, sink_row):
    s_len = zp.shape[0]
    tq = min(512, s_len)
    nb = tq // BLOCK
    n_steps = s_len // tq

    def body(after_ref, zp_ref, zph_ref, q_ref, kv_ref, kvh_ref, dpm_ref, dpmh_ref, do_ref, pw_ref, pb_ref, ps_ref,
             bias_ref, sink_ref, dzp_ref, dq_ref, dkv_ref, dpw_ref, dpb_ref, dps_ref, dsk_ref,
             ka_ref, vd_ref, dsk_acc, dkv_acc):
        i = pl.program_id(0)

        @pl.when(i == 0)
        def _():
            dkv_acc[...] = jnp.zeros_like(dkv_acc)
            dpw_ref[...] = jnp.zeros_like(dpw_ref)
            dpb_ref[...] = jnp.zeros_like(dpb_ref)
            dps_ref[...] = jnp.zeros_like(dps_ref)
            dsk_acc[...] = jnp.zeros_like(dsk_acc)

        cur = zp_ref[...].astype(F32)
        halo = zph_ref[...].astype(F32) * (i > 0).astype(F32)
        ext = jnp.concatenate([halo, cur], axis=0)
        dpm_next = dpmh_ref[...].astype(F32) * (i < n_steps - 1).astype(F32)
        dpm_ext = jnp.concatenate([dpm_ref[...].astype(F32), dpm_next], axis=0)
        n_ext = tq + POOL_HALO
        for g in range(4):
            sl = slice(g * POOL_GROUP_DIM, (g + 1) * POOL_GROUP_DIM)
            w = POOL_WINDOWS[g]
            d = _pool_d(ext[:, sl], cur[:, sl], g, i * tq).astype(BF16)
            y_lin = _dot(d, pw_ref[g]) + pb_ref[:, sl]
            dps_ref[:, sl] += jnp.sum(dpm_ext[:tq, sl] * y_lin, axis=0, keepdims=True)
            dyl_ext = dpm_ext[:, sl] * ps_ref[:, sl]
            dpb_ref[:, sl] += jnp.sum(dyl_ext[:tq], axis=0, keepdims=True)
            dyl_bf = dyl_ext.astype(BF16)
            dpw_ref[g] += _dot_tn(d, dyl_bf[:tq])
            dd = _dot_nt(dyl_bf, pw_ref[g])
            t = i * tq + lax.broadcasted_iota(jnp.int32, dd.shape, 0)
            e = dd / jnp.minimum(t + 1, w).astype(F32)
            acc = e
            k = 1
            while k < w:
                acc = acc + pltpu.roll(acc, n_ext - k, 0)
                k *= 2
            dzp_ref[:, sl] = (acc[:tq] - dd[:tq]).astype(BF16)

        _fill_kv_slabs(kvh_ref, kv_ref, ka_ref, vd_ref)

        def fold(dup):
            return dup + pltpu.roll(dup, HEAD_DIM, 1)

        for b in range(nb):
            rq = slice(b * BLOCK, (b + 1) * BLOCK)
            rk = slice(b * BLOCK, (b + 2) * BLOCK)
            qb = q_ref[rq, :]
            dob = do_ref[rq, :]
            dk_dup, dv_dup = [], []
            for h in range(N_KV_HEADS):
                kk = ka_ref[h, rk, :]
                q_st = _stack_pairs(qb, h)
                do_st = _stack_pairs(dob, h)
                pn, psink = _attn_probs_t(kk, q_st, bias_ref[h], sink_ref[h], (i == 0) if b == 0 else None)
                dp = _dot_nt(vd_ref[h, rk, :], do_st)
                delta = jnp.sum(pn * dp, axis=0, keepdims=True)
                dsk_acc[h] += -psink * delta
                ds = ((pn * (dp - delta)) * ATTN_SCALE).astype(BF16)
                dq_st = _dot_tn(ds, kk)
                for j in range(2):
                    left = _left_half((BLOCK, LANES))
                    dq_pair = jnp.where(left, dq_st[(2 * j) * BLOCK:(2 * j + 1) * BLOCK],
                                        dq_st[(2 * j + 1) * BLOCK:(2 * j + 2) * BLOCK])
                    dq_ref[rq, h * 256 + j * LANES:h * 256 + (j + 1) * LANES] = dq_pair.astype(BF16)
                dk_dup.append(fold(_dot(ds, q_st)))
                dv_dup.append(fold(_dot(pn.astype(BF16), do_st)))
            left = _left_half((2 * BLOCK, LANES))
            dkv_blk = jnp.concatenate([jnp.where(left, dk_dup[0], dk_dup[1]),
                                       jnp.where(left, dv_dup[0], dv_dup[1])], axis=1)
            g0 = pl.multiple_of(i * tq + b * BLOCK, BLOCK)
            dkv_acc[pl.ds(g0, 2 * BLOCK), :] += dkv_blk

        @pl.when(i == n_steps - 1)
        def _():
            dkv_ref[...] = dkv_acc[BLOCK:, :].astype(BF16)
            for h in range(N_KV_HEADS):
                for g in range(GROUP):
                    tot = jnp.sum(dsk_acc[h, :, g * BLOCK:(g + 1) * BLOCK], axis=1, keepdims=True)
                    dsk_ref[GROUP * h + g:GROUP * h + g + 1, :] = jnp.broadcast_to(tot, (1, LANES))

    blocks_per_tile = tq // POOL_HALO
    last_halo = s_len // POOL_HALO - 1
    halo_prev = pl.BlockSpec((POOL_HALO, 512), lambda i: (jnp.maximum(i * blocks_per_tile - 1, 0), 0))
    halo_next = pl.BlockSpec((POOL_HALO, 512), lambda i: (jnp.minimum((i + 1) * blocks_per_tile, last_halo), 0))
    halo_kv = pl.BlockSpec((BLOCK, 256), lambda i: (jnp.maximum(i * nb - 1, 0), 0))
    return pl.pallas_call(
        body, name="mixers_bwd", grid=(n_steps,),
        in_specs=[ANY, _rows(tq, 512), halo_prev, _rows(tq, 512), _rows(tq, 256), halo_kv,
                  _rows(tq, 512), halo_next, _rows(tq, 512),
                  _whole((4, 128, 128)), _whole((1, 512)), _whole((1, 512)),
                  _whole((N_KV_HEADS, 2 * BLOCK, GROUP * BLOCK)), _whole((N_KV_HEADS, 1, GROUP * BLOCK))],
        out_specs=[_rows(tq, 512), _rows(tq, 512), _whole((s_len, 256)),
                   _whole((4, 128, 128)), _whole((1, 512)), _whole((1, 512)), _whole((8, LANES))],
        out_shape=[jax.ShapeDtypeStruct((s_len, 512), BF16), jax.ShapeDtypeStruct((s_len, 512), BF16),
                   jax.ShapeDtypeStruct((s_len, 256), BF16), jax.ShapeDtypeStruct((4, 128, 128), F32),
                   jax.ShapeDtypeStruct((1, 512), F32), jax.ShapeDtypeStruct((1, 512), F32),
                   jax.ShapeDtypeStruct((8, LANES), F32)],
        scratch_shapes=[pltpu.VMEM((N_KV_HEADS, tq + BLOCK, LANES), BF16)] * 2
        + [pltpu.VMEM((N_KV_HEADS, 1, GROUP * BLOCK), F32), pltpu.VMEM((s_len + BLOCK, 256), F32)],
        compiler_params=_params(),
    )(after, zp, zp, q, kv, kv, dpm, dpm, do, pool_w, pool_b, pool_scale, bias_t, sink_row)


def _in_bwd(after, dzp, dq, dkv, dzg, w_in, x, dh1, g_mix):
    s_len = x.shape[0]
    tm = min(512, s_len)

    def body(after_ref, dzp_ref, dq_ref, dkv_ref, dzg_ref, w_ref, x_ref, dh1_ref, g_ref, dx_ref, dg_ref):
        i = pl.program_id(0)

        @pl.when(i == 0)
        def _():
            dg_ref[...] = jnp.zeros_like(dg_ref)

        du = _dot_nt(dzp_ref[...], w_ref[:, 0:512])
        du = du + _dot_nt(dq_ref[...], w_ref[:, 512:1024])
        du = du + _dot_nt(dkv_ref[...], w_ref[:, 1024:1280])
        du = du + _dot_nt(dzg_ref[...], w_ref[:, 1280:3328])
        r, xh, _ = _rms_fwd(x_ref[...], g_ref[...])
        dxn, dg = _rms_bwd(du, xh, r, g_ref[...])
        dg_ref[...] += dg
        dx_ref[...] = dh1_ref[...] + dxn

    return pl.pallas_call(
        body, name="in_bwd", grid=(s_len // tm,),
        in_specs=[ANY, _rows(tm, 512), _rows(tm, 512), _rows(tm, 256), _rows(tm, 2048), _whole((D_MODEL, IN_WIDTH)),
                  _rows(tm, D_MODEL), _rows(tm, D_MODEL), _whole((1, D_MODEL))],
        out_specs=[_rows(tm, D_MODEL), _whole((1, D_MODEL))],
        out_shape=[jax.ShapeDtypeStruct((s_len, D_MODEL), F32), jax.ShapeDtypeStruct((1, D_MODEL), F32)],
        compiler_params=_params(),
    )(after, dzp, dq, dkv, dzg, w_in, x, dh1, g_mix)


def _all_gather_weights(name, shards, after=None):
    n = len(shards)
    extra = [] if after is None else [after]
    n_extra = len(extra)

    def body(*refs):
        ins, outs = refs[:n], refs[n + n_extra:2 * n + n_extra]
        send_sems, recv_sems, local_sems = refs[2 * n + n_extra:]
        x, y, c = lax.axis_index("x"), lax.axis_index("y"), lax.axis_index("c")
        me, sibling = (x, y, c), (x, y, 1 - c)
        chips = [(1 - x, y), (x, 1 - y), (1 - x, 1 - y)]

        def slot(a, px, py, pc):
            return outs[a].at[4 * px + 2 * py + pc]

        def copy(a, k, block, to, src=None):
            return pltpu.make_async_remote_copy(
                src_ref=slot(a, *block) if src is None else src, dst_ref=slot(a, *block),
                send_sem=send_sems.at[a, k], recv_sem=recv_sems.at[a, k], device_id=to, device_id_type=MESH)

        mine = [pltpu.make_async_copy(ins[a], slot(a, *me), local_sems.at[a]) for a in range(n)]
        for cp in mine:
            cp.start()
        first = []
        for a in range(n):
            first.append(copy(a, 0, me, sibling, src=ins[a]))
            first += [copy(a, 1 + j, me, (*chip, c), src=ins[a]) for j, chip in enumerate(chips)]
        for cp in first:
            cp.start()
        passed = []
        for a in range(n):
            for j, chip in enumerate(chips):
                copy(a, 1 + j, (*chip, c), me).wait_recv()
                cp = copy(a, 4 + j, (*chip, c), sibling)
                cp.start()
                passed.append(cp)
        for a in range(n):
            copy(a, 0, sibling, me).wait_recv()
            for j, chip in enumerate(chips):
                copy(a, 4 + j, (*chip, 1 - c), me).wait_recv()
        for cp in first + passed:
            cp.wait_send()
        for cp in mine:
            cp.wait()

    return pl.pallas_call(
        body, name=name,
        in_specs=[ANY] * (n + n_extra), out_specs=[ANY] * n,
        out_shape=[jax.ShapeDtypeStruct((N_DEV,) + s.shape, s.dtype) for s in shards],
        scratch_shapes=[pltpu.SemaphoreType.DMA((n, 7)), pltpu.SemaphoreType.DMA((n, 7)), pltpu.SemaphoreType.DMA((n,))],
    )(*shards, *extra)


HBM_SPEC = pl.BlockSpec(memory_space=pltpu.HBM)
SEM_SPEC = pl.BlockSpec(memory_space=pltpu.SEMAPHORE)
DATAFLOW = pltpu.SideEffectType.DATAFLOW_SIDE_EFFECTING
N_PEERS = N_DEV - 1


def _peer_copies(srcs, lands, scatter, send_sems, recv_sems):
    x, y, c = lax.axis_index("x"), lax.axis_index("y"), lax.axis_index("c")
    me_idx = 4 * x + 2 * y + c
    copies = []
    for k in range(1, N_DEV):
        px = 1 - x if (k >> 2) & 1 else x
        py = 1 - y if (k >> 1) & 1 else y
        pc = 1 - c if k & 1 else c
        p_idx = 4 * px + 2 * py + pc
        for a in range(len(srcs)):
            src = srcs[a].at[p_idx] if scatter[a] else srcs[a]
            dst = lands[a].at[k] if scatter[a] else lands[a].at[me_idx]
            copies.append(pltpu.make_async_remote_copy(
                src_ref=src, dst_ref=dst, send_sem=send_sems.at[a * N_PEERS + k - 1],
                recv_sem=recv_sems.at[a * N_PEERS + k - 1],
                device_id=(px, py, pc), device_id_type=MESH))
    return copies


def _exchange_start(name, srcs, scatter, after):
    n = len(srcs)
    lands = [lax.empty(s.shape if sc else (N_DEV,) + s.shape, s.dtype) for s, sc in zip(srcs, scatter)]

    def body(*refs):
        src_refs, land_refs = refs[:n], refs[n:2 * n]
        send_sems, recv_sems = refs[2 * n + 1], refs[2 * n + 2]
        token = refs[4 * n + 3]
        for cp in _peer_copies(src_refs, land_refs, scatter, send_sems, recv_sems):
            cp.start()
        token[...] = jnp.zeros_like(token)

    hbm = lambda t: pltpu.HBM(t.shape, t.dtype)
    outs = pl.pallas_call(
        body, name=name,
        out_shape=[pltpu.SemaphoreType.DMA((n * N_PEERS,)), pltpu.SemaphoreType.DMA((n * N_PEERS,))]
        + [hbm(t) for t in srcs] + [hbm(t) for t in lands] + [jax.ShapeDtypeStruct((8, LANES), F32)],
        in_specs=[HBM_SPEC] * (2 * n) + [ANY],
        out_specs=[SEM_SPEC, SEM_SPEC] + [HBM_SPEC] * (2 * n) + [pl.BlockSpec(memory_space=pltpu.VMEM)],
        input_output_aliases={i: 2 + i for i in range(2 * n)},
        compiler_params=pltpu.CompilerParams(has_side_effects=DATAFLOW),
    )(*[pltpu.with_memory_space_constraint(t, pltpu.HBM) for t in list(srcs) + lands], after)
    return dict(n=n, scatter=scatter, send_sems=outs[0], recv_sems=outs[1], srcs=outs[2:2 + n],
                lands=outs[2 + n:2 + 2 * n], token=outs[2 + 2 * n])


def _exchange_wait(name, handle, after):
    n, scatter = handle["n"], handle["scatter"]

    def body(*refs):
        src_refs, land_refs = refs[:n], refs[n:2 * n]
        send_sems, recv_sems = refs[2 * n], refs[2 * n + 1]
        for cp in _peer_copies(src_refs, land_refs, scatter, send_sems, recv_sems):
            cp.wait_send()
            cp.wait_recv()

    both = list(handle["srcs"]) + list(handle["lands"])
    outs = pl.pallas_call(
        body, name=name,
        out_shape=[pltpu.HBM(t.shape, t.dtype) for t in both],
        in_specs=[HBM_SPEC] * (2 * n) + [SEM_SPEC, SEM_SPEC, ANY],
        out_specs=[HBM_SPEC] * (2 * n),
        input_output_aliases={i: i for i in range(2 * n)},
        compiler_params=pltpu.CompilerParams(has_side_effects=DATAFLOW),
    )(*both, handle["send_sems"], handle["recv_sems"], after)
    me_idx = _my_index()
    lands = [land if sc else lax.dynamic_update_index_in_dim(land, src, me_idx, 0)
             for land, src, sc in zip(outs[n:], outs[:n], scatter)]
    return lands, outs[:n]


def _my_index():
    return 4 * lax.axis_index("x") + 2 * lax.axis_index("y") + lax.axis_index("c")


def _adamw(parts, w, m, v, sent=None):
    r, c = w.shape
    tr = 256 if r % 256 == 0 else r
    own = sent is not None

    def body(*refs):
        if own:
            _, p_ref, own_ref, w_ref, m_ref, v_ref, g_ref, d_ref, nm_ref, nv_ref = refs
            g = own_ref[...].astype(F32)
        else:
            p_ref, w_ref, m_ref, v_ref, g_ref, d_ref, nm_ref, nv_ref = refs
            g = p_ref[0].astype(F32)
        for k in range(1, N_DEV):
            g = g + p_ref[k].astype(F32)
        m_new = ADAM_B1 * m_ref[...] + (1.0 - ADAM_B1) * g
        v_new = ADAM_B2 * v_ref[...] + (1.0 - ADAM_B2) * (g * g)
        m_hat = m_new / (1.0 - ADAM_B1 ** ADAM_STEP)
        v_hat = v_new / (1.0 - ADAM_B2 ** ADAM_STEP)
        g_ref[...] = g
        d_ref[...] = -ADAM_LR * (m_hat / (jnp.sqrt(v_hat) + ADAM_EPS) + ADAM_WD * w_ref[...])
        nm_ref[...] = m_new
        nv_ref[...] = v_new

    out_shape = [jax.ShapeDtypeStruct((r, c), F32)] * 4
    if not own:
        return pl.pallas_call(
            body, name="adamw", grid=(r // tr,),
            in_specs=[pl.BlockSpec((N_DEV, tr, c), lambda i: (0, i, 0))] + [_rows(tr, c)] * 3,
            out_specs=[_rows(tr, c)] * 4, out_shape=out_shape, compiler_params=_params(),
        )(parts, w, m, v)
    rows = pl.BlockSpec((tr, c), lambda i, me: (i, 0))
    return pl.pallas_call(
        body, name="adamw_own", out_shape=out_shape, compiler_params=_params(),
        grid_spec=pltpu.PrefetchScalarGridSpec(
            num_scalar_prefetch=1, grid=(r // tr,),
            in_specs=[pl.BlockSpec((N_DEV, tr, c), lambda i, me: (0, i, 0)),
                      pl.BlockSpec((None, tr, c), lambda i, me: (me[0], i, 0))] + [rows] * 3,
            out_specs=[rows] * 4),
    )(_my_index().reshape(1).astype(jnp.int32), parts, sent, w, m, v)


SMALL_LAYOUT = (("norm_mix", 8), ("pool_w", 512), ("pool_b", 8), ("pool_scale", 8), ("attn_sinks", 8),
                ("norm_mlp", 8), ("norm_final", 8), ("loss", 8))
SMALL_ROWS = sum(n for _, n in SMALL_LAYOUT)


def _pack_small(vals, layout=SMALL_LAYOUT):
    rows = []
    for name, n_rows in layout:
        flat = vals[name].astype(F32).reshape(-1)
        flat = jnp.pad(flat, (0, n_rows * LANES - flat.shape[0]))
        rows.append(flat.reshape(n_rows, LANES))
    return jnp.concatenate(rows, axis=0)


def _unpack_small(pack, shapes):
    out, r0 = {}, 0
    for name, n_rows in SMALL_LAYOUT:
        size = int(np.prod(shapes[name])) if shapes[name] else 1
        out[name] = pack[r0:r0 + n_rows].reshape(-1)[:size].reshape(shapes[name])
        r0 += n_rows
    return out


def kernel(x, norm_mix, w_in, pool_w, pool_b, pool_scale, attn_sinks, p_pool, p_attn, w_out, norm_mlp, w_up, w_down, norm_final, loss_target, m_norm_mix, m_w_in, m_pool_w, m_pool_b, m_pool_scale, m_attn_sinks, m_p_pool, m_p_attn, m_w_out, m_norm_mlp, m_w_up, m_w_down, m_norm_final, v_norm_mix, v_w_in, v_pool_w, v_pool_b, v_pool_scale, v_attn_sinks, v_p_pool, v_p_attn, v_w_out, v_norm_mlp, v_w_up, v_w_down, v_norm_final):
    xs = x[0]
    tgt = loss_target[0]
    s_len = xs.shape[0]

    w_in_bf, p_pool_bf, p_attn_bf, w_out_bf, w_up_bf, w_down_bf = [
        t[0].astype(BF16) for t in (w_in, p_pool, p_attn, w_out, w_up, w_down)]
    (w_in_g,) = _all_gather_weights("all_gather_w_in", [w_in_bf])
    ag_proj = _exchange_start("ag_proj_start", [p_pool_bf, p_attn_bf, w_out_bf], (False,) * 3, w_in_g)
    ag_mlp = _exchange_start("ag_mlp_start", [w_up_bf, w_down_bf], (False,) * 2, ag_proj["token"])
    w_in_f = w_in_g.transpose(1, 0, 2).reshape(D_MODEL, IN_WIDTH)

    pool_w_bf = pool_w[0].astype(BF16)
    pool_b_row = pool_b[0].reshape(1, POOL_WIDTH)
    bias_t, sink_row = _attn_constants(attn_sinks[0])

    u, zp, q, kv, zg = _fwd_in(ag_mlp["token"], xs, norm_mix, w_in_f)
    pm, o = _mixers_fwd(zp, q, kv, pool_w_bf, pool_b_row, pool_scale, bias_t, sink_row)
    (p_pool_g, p_attn_g, w_out_g), _ = _exchange_wait("ag_proj_wait", ag_proj, pm)
    p_pool_f = p_pool_g.transpose(1, 0, 2).reshape(POOL_WIDTH, D_MODEL)
    p_attn_f = p_attn_g.transpose(1, 0, 2).reshape(ATTN_WIDTH, D_MODEL)
    w_out_f = w_out_g.reshape(D_MODEL, D_MODEL)
    h1, mixed = _mix_out(xs, pm, o, zg, p_pool_f, p_attn_f, w_out_f)
    (w_up_g, w_down_g), _ = _exchange_wait("ag_mlp_wait", ag_mlp, h1)
    w_down_f = w_down_g.reshape(D_FF, D_MODEL)
    dh1, a, dapre, u2, dh2, loss_part, g_norm_mlp, g_norm_final = _mlp_loss(
        h1, tgt, norm_mlp, norm_final.reshape(1, D_MODEL), w_up_g, w_down_f)
    gw_down = _tn_matmul(a, dh2, square_a=True)
    gw_up = _tn_matmul(u2, dapre, col_blocks=N_DEV)
    ex_mlp = _exchange_start(
        "ex_mlp_start", [gw_up, gw_down.reshape(N_DEV, D_FF // N_DEV, D_MODEL)], (True, True), gw_down)
    dyp, dya, dzg, dpm, do = _mix_bwd(ex_mlp["token"], dh1, pm, o, zg, p_pool_f, p_attn_f, w_out_f)
    gw_out = _tn_matmul(mixed, dh1)
    gp_pool = _tn_matmul(pm, dyp, col_blocks=N_DEV)
    gp_attn = _tn_matmul(o, dya, col_blocks=N_DEV)
    ex_proj = _exchange_start(
        "ex_proj_start", [gp_pool, gp_attn, gw_out.reshape(N_DEV, D_MODEL // N_DEV, D_MODEL)], (True,) * 3, gw_out)
    dzp, dq, dkv, g_pool_w, g_pool_b, g_pool_scale, g_sinks = _mixers_bwd(
        ex_proj["token"], zp, q, kv, dpm, do, pool_w_bf, pool_b_row, pool_scale, bias_t, sink_row)
    gw_in = jnp.concatenate(
        [_tn_matmul(u, dzp), _tn_matmul(u, dq), _tn_matmul(u, dkv), _tn_matmul(u, dzg)], axis=1)
    small_vals = dict(pool_w=g_pool_w, pool_b=g_pool_b, pool_scale=g_pool_scale, attn_sinks=g_sinks[:, 0],
                      norm_mlp=g_norm_mlp, norm_final=g_norm_final, loss=loss_part[0, 0])
    ex_in = _exchange_start(
        "ex_in_start", [gw_in.reshape(D_MODEL, N_DEV, IN_WIDTH // N_DEV).transpose(1, 0, 2),
                        _pack_small(small_vals, SMALL_LAYOUT[1:])], (True, False), gw_in)
    dx, g_norm_mix = _in_bwd(ex_in["token"], dzp, dq, dkv, dzg, w_in_f, xs, dh1, norm_mix)

    big_w = dict(w_in=w_in, p_pool=p_pool, p_attn=p_attn, w_out=w_out, w_up=w_up, w_down=w_down)
    big_m = dict(w_in=m_w_in, p_pool=m_p_pool, p_attn=m_p_attn, w_out=m_w_out, w_up=m_w_up, w_down=m_w_down)
    big_v = dict(w_in=v_w_in, p_pool=v_p_pool, p_attn=v_p_attn, w_out=v_w_out, w_up=v_w_up, w_down=v_w_down)
    res = {}

    def update(names, recvs, sents):
        for name, parts, sent in zip(names, recvs, sents):
            outs = _adamw(parts, big_w[name][0], big_m[name][0], big_v[name][0], sent)
            res[name] = [t[None] for t in outs]

    update(["w_up", "w_down"], *_exchange_wait("ex_mlp_wait", ex_mlp, dx))
    update(["p_pool", "p_attn", "w_out"], *_exchange_wait("ex_proj_wait", ex_proj, res["w_down"][0]))
    (norm_mix_all,) = _all_gather_weights(
        "all_gather_norm_mix", [_pack_small(dict(norm_mix=g_norm_mix), SMALL_LAYOUT[:1])], res["w_out"][0])
    (r_in, small_rest), (s_in, _) = _exchange_wait("ex_in_wait", ex_in, norm_mix_all)
    update(["w_in"], [r_in], [s_in])
    small_all = jnp.concatenate([norm_mix_all, small_rest], axis=1)

    small_names = ["norm_mix", "pool_w", "pool_b", "pool_scale", "attn_sinks", "norm_mlp", "norm_final"]
    small_w = dict(norm_mix=norm_mix, pool_w=pool_w, pool_b=pool_b, pool_scale=pool_scale, attn_sinks=attn_sinks,
                   norm_mlp=norm_mlp, norm_final=norm_final)
    small_m = dict(norm_mix=m_norm_mix, pool_w=m_pool_w, pool_b=m_pool_b, pool_scale=m_pool_scale,
                   attn_sinks=m_attn_sinks, norm_mlp=m_norm_mlp, norm_final=m_norm_final)
    small_v = dict(norm_mix=v_norm_mix, pool_w=v_pool_w, pool_b=v_pool_b, pool_scale=v_pool_scale,
                   attn_sinks=v_attn_sinks, norm_mlp=v_norm_mlp, norm_final=v_norm_final)
    zero = jnp.zeros((), F32)
    packs = [_pack_small({**d, "loss": zero}) for d in (small_w, small_m, small_v)]
    s_outs = _adamw(small_all, *packs)
    shapes = {k: small_w[k].shape for k in small_names}
    shapes["loss"] = ()
    s_res = [_unpack_small(t, shapes) for t in s_outs]
    loss = s_res[0]["loss"]
    for name in small_names:
        res[name] = [t[name] for t in s_res]

    order = ["norm_mix", "w_in", "pool_w", "pool_b", "pool_scale", "attn_sinks", "p_pool", "p_attn", "w_out",
             "norm_mlp", "w_up", "w_down", "norm_final"]
    out = [loss, dx[None]]
    for kind in range(4):
        out += [res[name][kind] for name in order]
    return tuple(out)
```

```python
import functools
import math

import numpy as np
import jax
import jax.numpy as jnp
from jax import lax
from jax.experimental import pallas as pl
from jax.experimental.pallas import tpu as pltpu

F32 = jnp.float32
BF16 = jnp.bfloat16

D_MODEL = 1024
POOL_WIDTH = 512
ATTN_WIDTH = 512
KV_WIDTH = 128
HEAD_DIM = 64
N_HEADS = 8
N_KV_HEADS = 2
GROUP = 4
BLOCK = 128
POOL_WINDOWS = (2, 4, 8, 16)
POOL_GROUP_DIM = 128
POOL_HALO = 16
D_FF = 4096
FF_CHUNK = 1024
IN_WIDTH = 3328
RMS_EPS = 1e-5
NEG_INF = -1e30
ATTN_SCALE = 1.0 / math.sqrt(HEAD_DIM)
N_DEV = 8

ADAM_LR = 0.001
ADAM_B1 = 0.9
ADAM_B2 = 0.999
ADAM_EPS = 1e-08
ADAM_WD = 0.01
ADAM_STEP = 10

LANES = 128
VMEM_LIMIT_BYTES = 56 * 1024 * 1024
MESH = pl.DeviceIdType.MESH


def _params(n_grid_axes=1):
    return pltpu.CompilerParams(
        dimension_semantics=("arbitrary",) * n_grid_axes, vmem_limit_bytes=VMEM_LIMIT_BYTES)


def _dot(a, b):
    return jnp.dot(a, b, preferred_element_type=F32)


def _dot_nt(a, b):
    return lax.dot_general(a, b, (((1,), (1,)), ((), ())), preferred_element_type=F32)


def _dot_tn(a, b):
    return lax.dot_general(a, b, (((0,), (0,)), ((), ())), preferred_element_type=F32)


ANY = pl.BlockSpec(memory_space=pl.ANY)


def _rows(tm, n):
    return pl.BlockSpec((tm, n), lambda i: (i, 0))


def _whole(shape):
    zeros = (0,) * len(shape)
    return pl.BlockSpec(shape, lambda i: zeros)


def _rms_fwd(h, g):
    r = lax.rsqrt(jnp.mean(h * h, axis=-1, keepdims=True) + RMS_EPS)
    xh = h * r
    return r, xh, xh * g


def _rms_bwd(dy, xh, r, g):
    dxh = dy * g
    dh = r * (dxh - xh * jnp.mean(dxh * xh, axis=-1, keepdims=True))
    return dh, jnp.sum(dy * xh, axis=0, keepdims=True)


def _fwd_in(after, x, g_mix, w_in):
    s_len = x.shape[0]
    tm = min(512, s_len)

    def body(after_ref, x_ref, g_ref, w_ref, u_ref, zp_ref, q_ref, kv_ref, zg_ref):
        _, _, u = _rms_fwd(x_ref[...], g_ref[...])
        u = u.astype(BF16)
        u_ref[...] = u
        zp_ref[...] = _dot(u, w_ref[:, 0:512]).astype(BF16)
        q_ref[...] = _dot(u, w_ref[:, 512:1024]).astype(BF16)
        kv_ref[...] = _dot(u, w_ref[:, 1024:1280]).astype(BF16)
        zg_ref[...] = _dot(u, w_ref[:, 1280:3328]).astype(BF16)

    return pl.pallas_call(
        body, name="fwd_in", grid=(s_len // tm,),
        in_specs=[ANY, _rows(tm, D_MODEL), _whole((1, D_MODEL)), _whole((D_MODEL, IN_WIDTH))],
        out_specs=[_rows(tm, D_MODEL), _rows(tm, 512), _rows(tm, 512), _rows(tm, 256), _rows(tm, 2048)],
        out_shape=[jax.ShapeDtypeStruct((s_len, n), BF16) for n in (D_MODEL, 512, 512, 256, 2048)],
        compiler_params=_params(),
    )(after, x, g_mix, w_in)


def _attn_constants(sinks):
    qi = np.arange(BLOCK)[:, None]
    kj = np.arange(2 * BLOCK)[None, :]
    dist = BLOCK + qi - kj
    valid = (dist >= 0) & (dist < BLOCK)
    slopes = np.array([2.0 ** (-8.0 * (h + 1) / N_HEADS) for h in range(N_HEADS)], dtype=np.float32)
    bias = np.where(valid[None], -slopes[:, None, None] * dist.astype(np.float32)[None], np.float32(NEG_INF))
    bias = bias.astype(np.float32).reshape(N_KV_HEADS, GROUP * BLOCK, 2 * BLOCK).transpose(0, 2, 1)
    sink_row = jnp.repeat(sinks.astype(F32).reshape(N_KV_HEADS, GROUP), BLOCK, axis=1)[:, None, :]
    return jnp.asarray(np.ascontiguousarray(bias)), sink_row


def _left_half(shape):
    return lax.broadcasted_iota(jnp.int32, shape, 1) < HEAD_DIM


def _dup_halves(slab):
    swapped = pltpu.roll(slab, HEAD_DIM, 1)
    left = _left_half(slab.shape)
    return jnp.where(left, slab, swapped), jnp.where(left, swapped, slab)


def _fill_kv_slabs(kvh_ref, kv_ref, ka_ref, vd_ref):
    for rows, src in ((slice(0, BLOCK), kvh_ref), (slice(BLOCK, None), kv_ref)):
        kvf = src[...].astype(F32)
        for ref, lanes in ((ka_ref, slice(0, KV_WIDTH)), (vd_ref, slice(KV_WIDTH, 2 * KV_WIDTH))):
            d0, d1 = _dup_halves(kvf[:, lanes])
            ref[0, rows, :] = d0.astype(BF16)
            ref[1, rows, :] = d1.astype(BF16)


def _stack_pairs(a, h):
    pieces = []
    for j in range(2):
        pair = a[:, h * 256 + j * LANES:h * 256 + (j + 1) * LANES]
        left = _left_half(pair.shape)
        zero = jnp.zeros_like(pair)
        pieces += [jnp.where(left, pair, zero), jnp.where(left, zero, pair)]
    return jnp.concatenate(pieces, axis=0)


def _attn_probs_t(kk, q_st, bias_t, sink_row, first):
    s = _dot_nt(kk, q_st) * ATTN_SCALE + bias_t
    if first is not None:
        row = lax.broadcasted_iota(jnp.int32, s.shape, 0)
        s = jnp.where(jnp.logical_and(first, row < BLOCK), NEG_INF, s)
    m = jnp.maximum(jnp.max(s, axis=0, keepdims=True), sink_row)
    p = jnp.exp(s - m)
    es = jnp.exp(sink_row - m)
    inv = 1.0 / (jnp.sum(p, axis=0, keepdims=True) + es)
    return p * inv, es * inv


def _pool_d(ext, cur, g, row0):
    w = POOL_WINDOWS[g]
    acc = ext
    k = 1
    while k < w:
        acc = acc + pltpu.roll(acc, k, 0)
        k *= 2
    t = row0 + lax.broadcasted_iota(jnp.int32, cur.shape, 0)
    cnt = jnp.minimum(t + 1, w).astype(F32)
    return acc[POOL_HALO:, :] / cnt - cur


def _mixers_fwd(zp, q, kv, pool_w, pool_b, pool_scale, bias_t, sink_row):
    s_len = zp.shape[0]
    tq = min(512, s_len)
    nb = tq // BLOCK

    def body(zp_ref, zph_ref, q_ref, kv_ref, kvh_ref, pw_ref, pb_ref, ps_ref, bias_ref, sink_ref,
             pm_ref, o_ref, ka_ref, vd_ref):
        i = pl.program_id(0)
        cur = zp_ref[...].astype(F32)
        halo = zph_ref[...].astype(F32) * (i > 0).astype(F32)
        ext = jnp.concatenate([halo, cur], axis=0)
        for g in range(4):
            sl = slice(g * POOL_GROUP_DIM, (g + 1) * POOL_GROUP_DIM)
            d = _pool_d(ext[:, sl], cur[:, sl], g, i * tq)
            y = _dot(d.astype(BF16), pw_ref[g]) + pb_ref[:, sl]
            pm_ref[:, sl] = (y * ps_ref[:, sl]).astype(BF16)
        _fill_kv_slabs(kvh_ref, kv_ref, ka_ref, vd_ref)
        for b in range(nb):
            rq = slice(b * BLOCK, (b + 1) * BLOCK)
            rk = slice(b * BLOCK, (b + 2) * BLOCK)
            qb = q_ref[rq, :]
            for h in range(N_KV_HEADS):
                pn, _ = _attn_probs_t(ka_ref[h, rk, :], _stack_pairs(qb, h), bias_ref[h], sink_ref[h],
                                      (i == 0) if b == 0 else None)
                pn = pn.astype(BF16)
                vd = vd_ref[h, rk, :]
                left = _left_half(vd.shape)
                zero = jnp.zeros_like(vd)
                va, vb = jnp.where(left, vd, zero), jnp.where(left, zero, vd)
                for j in range(2):
                    o_pair = (_dot_tn(pn[:, (2 * j) * BLOCK:(2 * j + 1) * BLOCK], va)
                              + _dot_tn(pn[:, (2 * j + 1) * BLOCK:(2 * j + 2) * BLOCK], vb))
                    o_ref[rq, h * 256 + j * LANES:h * 256 + (j + 1) * LANES] = o_pair.astype(BF16)

    halo_pool = pl.BlockSpec((POOL_HALO, 512), lambda i: (jnp.maximum(i * (tq // POOL_HALO) - 1, 0), 0))
    halo_kv = pl.BlockSpec((BLOCK, 256), lambda i: (jnp.maximum(i * nb - 1, 0), 0))
    return pl.pallas_call(
        body, name="mixers_fwd", grid=(s_len // tq,),
        in_specs=[_rows(tq, 512), halo_pool, _rows(tq, 512), _rows(tq, 256), halo_kv,
                  _whole((4, 128, 128)), _whole((1, 512)), _whole((1, 512)),
                  _whole((N_KV_HEADS, 2 * BLOCK, GROUP * BLOCK)), _whole((N_KV_HEADS, 1, GROUP * BLOCK))],
        out_specs=[_rows(tq, 512), _rows(tq, 512)],
        out_shape=[jax.ShapeDtypeStruct((s_len, 512), BF16)] * 2,
        scratch_shapes=[pltpu.VMEM((N_KV_HEADS, tq + BLOCK, LANES), BF16)] * 2,
        compiler_params=_params(),
    )(zp, zp, q, kv, kv, pool_w, pool_b, pool_scale, bias_t, sink_row)


def _gated_mix(pm, o, zg, pp_ref, pa_ref):
    yp = _dot(pm, pp_ref[...])
    ya = _dot(o, pa_ref[...])
    gp = jax.nn.sigmoid(zg[:, :D_MODEL].astype(F32))
    ga = jax.nn.sigmoid(zg[:, D_MODEL:].astype(F32))
    return yp, ya, gp, ga


def _mix_out(x, pm, o, zg, p_pool, p_attn, w_out):
    s_len = x.shape[0]
    tm = min(512, s_len)

    def body(x_ref, pm_ref, o_ref, zg_ref, pp_ref, pa_ref, wo_ref, h1_ref, mixed_ref):
        yp, ya, gp, ga = _gated_mix(pm_ref[...], o_ref[...], zg_ref[...], pp_ref, pa_ref)
        mixed = (gp * yp + ga * ya).astype(BF16)
        mixed_ref[...] = mixed
        h1_ref[...] = x_ref[...] + _dot(mixed, wo_ref[...])

    return pl.pallas_call(
        body, name="mix_out", grid=(s_len // tm,),
        in_specs=[_rows(tm, D_MODEL), _rows(tm, 512), _rows(tm, 512), _rows(tm, 2048),
                  _whole((512, D_MODEL)), _whole((512, D_MODEL)), _whole((D_MODEL, D_MODEL))],
        out_specs=[_rows(tm, D_MODEL), _rows(tm, D_MODEL)],
        out_shape=[jax.ShapeDtypeStruct((s_len, D_MODEL), F32), jax.ShapeDtypeStruct((s_len, D_MODEL), BF16)],
        compiler_params=_params(),
    )(x, pm, o, zg, p_pool, p_attn, w_out)


def _mlp_loss(h1, tgt, g_mlp, g_fin, w_up_blocks, w_down):
    s_len = h1.shape[0]
    tm = min(256, s_len)
    n_chunks = D_FF // FF_CHUNK
    up_block = D_FF // N_DEV
    per_chunk = FF_CHUNK // up_block

    def body(h1_ref, tgt_ref, gm_ref, gf_ref, wu_ref, wd_ref,
             dh1_ref, a_ref, dap_ref, u2_ref, dh2_ref, small_ref):
        i = pl.program_id(0)

        @pl.when(i == 0)
        def _():
            small_ref[...] = jnp.zeros_like(small_ref)

        h1 = h1_ref[...]
        r2, xh2, u2 = _rms_fwd(h1, gm_ref[...])
        u2 = u2.astype(BF16)
        u2_ref[...] = u2
        acc = jnp.zeros((tm, D_MODEL), F32)
        for c in range(n_chunks):
            cs = slice(c * FF_CHUNK, (c + 1) * FF_CHUNK)
            a = jnp.concatenate([_dot(u2, wu_ref[per_chunk * c + j]) for j in range(per_chunk)], axis=1)
            a = jnp.maximum(a, 0.0)
            a_ref[:, cs] = a.astype(BF16)
            acc = acc + _dot((a * a).astype(BF16), wd_ref[cs, :])
        h2 = h1 + acc
        r3, xh3, y = _rms_fwd(h2, gf_ref[...])
        diff = y - tgt_ref[...]
        small_ref[2:3, :] += 0.5 * jnp.sum(jnp.mean(diff * diff, axis=-1, keepdims=True))
        dy = diff * (1.0 / D_MODEL)
        dh2, dgf = _rms_bwd(dy, xh3, r3, gf_ref[...])
        small_ref[1:2, :] += dgf
        dh2_bf = dh2.astype(BF16)
        dh2_ref[...] = dh2_bf
        du2 = jnp.zeros((tm, D_MODEL), F32)
        for c in range(n_chunks):
            cs = slice(c * FF_CHUNK, (c + 1) * FF_CHUNK)
            ds = _dot_nt(dh2_bf, wd_ref[cs, :])
            dap = (ds * (2.0 * a_ref[:, cs].astype(F32))).astype(BF16)
            dap_ref[:, cs] = dap
            for j in range(per_chunk):
                du2 = du2 + _dot_nt(dap[:, j * up_block:(j + 1) * up_block], wu_ref[per_chunk * c + j])
        dh1n, dgm = _rms_bwd(du2, xh2, r2, gm_ref[...])
        small_ref[0:1, :] += dgm
        dh1_ref[...] = dh2 + dh1n

    single = dict(pipeline_mode=pl.Buffered(1))
    return pl.pallas_call(
        body, name="mlp_loss", grid=(s_len // tm,),
        in_specs=[_rows(tm, D_MODEL), _rows(tm, D_MODEL), _whole((1, D_MODEL)), _whole((1, D_MODEL)),
                  pl.BlockSpec((N_DEV, D_MODEL, up_block), lambda i: (0, 0, 0), **single),
                  pl.BlockSpec((D_FF, D_MODEL), lambda i: (0, 0), **single)],
        out_specs=[_rows(tm, D_MODEL), _rows(tm, D_FF), _rows(tm, D_FF), _rows(tm, D_MODEL), _rows(tm, D_MODEL),
                   _whole((8, D_MODEL))],
        out_shape=[jax.ShapeDtypeStruct((s_len, D_MODEL), F32), jax.ShapeDtypeStruct((s_len, D_FF), BF16),
                   jax.ShapeDtypeStruct((s_len, D_FF), BF16), jax.ShapeDtypeStruct((s_len, D_MODEL), BF16),
                   jax.ShapeDtypeStruct((s_len, D_MODEL), BF16), jax.ShapeDtypeStruct((8, D_MODEL), F32)],
        compiler_params=_params(),
    )(h1, tgt, g_mlp, g_fin, w_up_blocks, w_down)


def _tn_matmul(a, b, square_a=False, col_blocks=None):
    s_len, ka = a.shape
    nb = b.shape[1]
    tt = min(2048, s_len)
    tk = min(1024, ka)
    tn = min(1024, nb)
    n_t = s_len // tt
    if col_blocks is None:
        out_spec = pl.BlockSpec((tk, tn), lambda k, j, t: (k, j))
        out_shape = jax.ShapeDtypeStruct((ka, nb), BF16)
    else:
        width = nb // col_blocks
        per_tile = tn // width
        out_spec = pl.BlockSpec((per_tile, tk, width), lambda k, j, t: (j, k, 0))
        out_shape = jax.ShapeDtypeStruct((col_blocks, ka, width), BF16)

    def body(a_ref, b_ref, o_ref, acc_ref):
        t = pl.program_id(2)

        @pl.when(t == 0)
        def _():
            acc_ref[...] = jnp.zeros_like(acc_ref)

        av = a_ref[...]
        if square_a:
            av = av * av
        acc_ref[...] += _dot_tn(av.astype(BF16), b_ref[...].astype(BF16))

        @pl.when(t == n_t - 1)
        def _():
            if col_blocks is None:
                o_ref[...] = acc_ref[...].astype(o_ref.dtype)
            else:
                for blk in range(per_tile):
                    o_ref[blk] = acc_ref[:, blk * width:(blk + 1) * width].astype(o_ref.dtype)

    return pl.pallas_call(
        body, name="tn_matmul", grid=(ka // tk, nb // tn, n_t),
        in_specs=[pl.BlockSpec((tt, tk), lambda k, j, t: (t, k)), pl.BlockSpec((tt, tn), lambda k, j, t: (t, j))],
        out_specs=out_spec, out_shape=out_shape,
        scratch_shapes=[pltpu.VMEM((tk, tn), F32)],
        compiler_params=_params(3),
    )(a, b)


def _mix_bwd(after, dh1, pm, o, zg, p_pool, p_attn, w_out):
    s_len = dh1.shape[0]
    tm = min(512, s_len)

    def body(after_ref, dh1_ref, pm_ref, o_ref, zg_ref, pp_ref, pa_ref, wo_ref,
             dyp_ref, dya_ref, dzg_ref, dpm_ref, do_ref):
        dm = _dot_nt(dh1_ref[...].astype(BF16), wo_ref[...])
        yp, ya, gp, ga = _gated_mix(pm_ref[...], o_ref[...], zg_ref[...], pp_ref, pa_ref)
        dyp = (dm * gp).astype(BF16)
        dya = (dm * ga).astype(BF16)
        dyp_ref[...] = dyp
        dya_ref[...] = dya
        dzg_ref[:, :D_MODEL] = (dm * yp * (gp * (1.0 - gp))).astype(BF16)
        dzg_ref[:, D_MODEL:] = (dm * ya * (ga * (1.0 - ga))).astype(BF16)
        dpm_ref[...] = _dot_nt(dyp, pp_ref[...]).astype(BF16)
        do_ref[...] = _dot_nt(dya, pa_ref[...]).astype(BF16)

    return pl.pallas_call(
        body, name="mix_bwd", grid=(s_len // tm,),
        in_specs=[ANY, _rows(tm, D_MODEL), _rows(tm, 512), _rows(tm, 512), _rows(tm, 2048),
                  _whole((512, D_MODEL)), _whole((512, D_MODEL)), _whole((D_MODEL, D_MODEL))],
        out_specs=[_rows(tm, D_MODEL), _rows(tm, D_MODEL), _rows(tm, 2048), _rows(tm, 512), _rows(tm, 512)],
        out_shape=[jax.ShapeDtypeStruct((s_len, n), BF16) for n in (D_MODEL, D_MODEL, 2048, 512, 512)],
        compiler_params=_params(),
    )(after, dh1, pm, o, zg, p_pool, p_attn, w_out)


MIX_POOL_B = 4 * POOL_GROUP_DIM
MIX_SINKS = MIX_POOL_B + 8
MIX_ROWS = MIX_SINKS + 8


def _mixers_bwd(after, zp, q, kv, dpm, do, pool_w, pool_b, pool_scale, bias_t, sink_row):
    s_len = zp.shape[0]
    tq = min(512, s_len)
    nb = tq // BLOCK
    n_steps = s_len // tq

    def body(after_ref, zp_ref, zph_ref, q_ref, kv_ref, kvh_ref, dpm_ref, dpmh_ref, do_ref, pw_ref, pb_ref, ps_ref,
             bias_ref, sink_ref, dzp_ref, dq_ref, dkv_ref, small_ref, dps_ref,
             ka_ref, vd_ref, dsk_acc, dkv_acc):
        i = pl.program_id(0)

        @pl.when(i == 0)
        def _():
            dkv_acc[...] = jnp.zeros_like(dkv_acc)
            small_ref[...] = jnp.zeros_like(small_ref)
            dps_ref[...] = jnp.zeros_like(dps_ref)
            dsk_acc[...] = jnp.zeros_like(dsk_acc)

        cur = zp_ref[...].astype(F32)
        halo = zph_ref[...].astype(F32) * (i > 0).astype(F32)
        ext = jnp.concatenate([halo, cur], axis=0)
        dpm_next = dpmh_ref[...].astype(F32) * (i < n_steps - 1).astype(F32)
        dpm_ext = jnp.concatenate([dpm_ref[...].astype(F32), dpm_next], axis=0)
        n_ext = tq + POOL_HALO
        for g in range(4):
            sl = slice(g * POOL_GROUP_DIM, (g + 1) * POOL_GROUP_DIM)
            w = POOL_WINDOWS[g]
            d = _pool_d(ext[:, sl], cur[:, sl], g, i * tq).astype(BF16)
            y_lin = _dot(d, pw_ref[g]) + pb_ref[:, sl]
            dps_ref[:, sl] += jnp.sum(dpm_ext[:tq, sl] * y_lin, axis=0, keepdims=True)
            dyl_ext = dpm_ext[:, sl] * ps_ref[:, sl]
            small_ref[MIX_POOL_B + g:MIX_POOL_B + g + 1, :] += jnp.sum(dyl_ext[:tq], axis=0, keepdims=True)
            dyl_bf = dyl_ext.astype(BF16)
            small_ref[g * POOL_GROUP_DIM:(g + 1) * POOL_GROUP_DIM, :] += _dot_tn(d, dyl_bf[:tq])
            dd = _dot_nt(dyl_bf, pw_ref[g])
            t = i * tq + lax.broadcasted_iota(jnp.int32, dd.shape, 0)
            e = dd / jnp.minimum(t + 1, w).astype(F32)
            acc = e
            k = 1
            while k < w:
                acc = acc + pltpu.roll(acc, n_ext - k, 0)
                k *= 2
            dzp_ref[:, sl] = (acc[:tq] - dd[:tq]).astype(BF16)

        _fill_kv_slabs(kvh_ref, kv_ref, ka_ref, vd_ref)

        def fold(dup):
            return dup + pltpu.roll(dup, HEAD_DIM, 1)

        for b in range(nb):
            rq = slice(b * BLOCK, (b + 1) * BLOCK)
            rk = slice(b * BLOCK, (b + 2) * BLOCK)
            qb = q_ref[rq, :]
            dob = do_ref[rq, :]
            dk_dup, dv_dup = [], []
            for h in range(N_KV_HEADS):
                kk = ka_ref[h, rk, :]
                q_st = _stack_pairs(qb, h)
                do_st = _stack_pairs(dob, h)
                pn, psink = _attn_probs_t(kk, q_st, bias_ref[h], sink_ref[h], (i == 0) if b == 0 else None)
                dp = _dot_nt(vd_ref[h, rk, :], do_st)
                delta = jnp.sum(pn * dp, axis=0, keepdims=True)
                dsk_acc[h] += -psink * delta
                ds = ((pn * (dp - delta)) * ATTN_SCALE).astype(BF16)
                dq_st = _dot_tn(ds, kk)
                for j in range(2):
                    left = _left_half((BLOCK, LANES))
                    dq_pair = jnp.where(left, dq_st[(2 * j) * BLOCK:(2 * j + 1) * BLOCK],
                                        dq_st[(2 * j + 1) * BLOCK:(2 * j + 2) * BLOCK])
                    dq_ref[rq, h * 256 + j * LANES:h * 256 + (j + 1) * LANES] = dq_pair.astype(BF16)
                dk_dup.append(fold(_dot(ds, q_st)))
                dv_dup.append(fold(_dot(pn.astype(BF16), do_st)))
            left = _left_half((2 * BLOCK, LANES))
            dkv_blk = jnp.concatenate([jnp.where(left, dk_dup[0], dk_dup[1]),
                                       jnp.where(left, dv_dup[0], dv_dup[1])], axis=1)
            g0 = pl.multiple_of(i * tq + b * BLOCK, BLOCK)
            dkv_acc[pl.ds(g0, 2 * BLOCK), :] += dkv_blk

        @pl.when(i == n_steps - 1)
        def _():
            dkv_ref[...] = dkv_acc[BLOCK:, :].astype(BF16)
            lane = lax.broadcasted_iota(jnp.int32, (1, LANES), 1)
            row = jnp.zeros((1, LANES), F32)
            for h in range(N_KV_HEADS):
                for g in range(GROUP):
                    tot = jnp.sum(dsk_acc[h, :, g * BLOCK:(g + 1) * BLOCK], axis=1, keepdims=True)
                    row = jnp.where(lane == GROUP * h + g, tot, row)
            small_ref[MIX_SINKS:MIX_SINKS + 1, :] = row

    blocks_per_tile = tq // POOL_HALO
    last_halo = s_len // POOL_HALO - 1
    halo_prev = pl.BlockSpec((POOL_HALO, 512), lambda i: (jnp.maximum(i * blocks_per_tile - 1, 0), 0))
    halo_next = pl.BlockSpec((POOL_HALO, 512), lambda i: (jnp.minimum((i + 1) * blocks_per_tile, last_halo), 0))
    halo_kv = pl.BlockSpec((BLOCK, 256), lambda i: (jnp.maximum(i * nb - 1, 0), 0))
    return pl.pallas_call(
        body, name="mixers_bwd", grid=(n_steps,),
        in_specs=[ANY, _rows(tq, 512), halo_prev, _rows(tq, 512), _rows(tq, 256), halo_kv,
                  _rows(tq, 512), halo_next, _rows(tq, 512),
                  _whole((4, 128, 128)), _whole((1, 512)), _whole((1, 512)),
                  _whole((N_KV_HEADS, 2 * BLOCK, GROUP * BLOCK)), _whole((N_KV_HEADS, 1, GROUP * BLOCK))],
        out_specs=[_rows(tq, 512), _rows(tq, 512), _whole((s_len, 256)),
                   _whole((MIX_ROWS, LANES)), _whole((1, 512))],
        out_shape=[jax.ShapeDtypeStruct((s_len, 512), BF16), jax.ShapeDtypeStruct((s_len, 512), BF16),
                   jax.ShapeDtypeStruct((s_len, 256), BF16), jax.ShapeDtypeStruct((MIX_ROWS, LANES), F32),
                   jax.ShapeDtypeStruct((1, 512), F32)],
        scratch_shapes=[pltpu.VMEM((N_KV_HEADS, tq + BLOCK, LANES), BF16)] * 2
        + [pltpu.VMEM((N_KV_HEADS, 1, GROUP * BLOCK), F32), pltpu.VMEM((s_len + BLOCK, 256), F32)],
        compiler_params=_params(),
    )(after, zp, zp, q, kv, kv, dpm, dpm, do, pool_w, pool_b, pool_scale, bias_t, sink_row)


def _in_bwd(after, dzp, dq, dkv, dzg, w_in, x, dh1, g_mix):
    s_len = x.shape[0]
    tm = min(512, s_len)

    def body(after_ref, dzp_ref, dq_ref, dkv_ref, dzg_ref, w_ref, x_ref, dh1_ref, g_ref, dx_ref, dg_ref):
        i = pl.program_id(0)

        @pl.when(i == 0)
        def _():
            dg_ref[...] = jnp.zeros_like(dg_ref)

        du = _dot_nt(dzp_ref[...], w_ref[:, 0:512])
        du = du + _dot_nt(dq_ref[...], w_ref[:, 512:1024])
        du = du + _dot_nt(dkv_ref[...], w_ref[:, 1024:1280])
        du = du + _dot_nt(dzg_ref[...], w_ref[:, 1280:3328])
        r, xh, _ = _rms_fwd(x_ref[...], g_ref[...])
        dxn, dg = _rms_bwd(du, xh, r, g_ref[...])
        dg_ref[...] += dg
        dx_ref[...] = dh1_ref[...] + dxn

    return pl.pallas_call(
        body, name="in_bwd", grid=(s_len // tm,),
        in_specs=[ANY, _rows(tm, 512), _rows(tm, 512), _rows(tm, 256), _rows(tm, 2048), _whole((D_MODEL, IN_WIDTH)),
                  _rows(tm, D_MODEL), _rows(tm, D_MODEL), _whole((1, D_MODEL))],
        out_specs=[_rows(tm, D_MODEL), _whole((1, D_MODEL))],
        out_shape=[jax.ShapeDtypeStruct((s_len, D_MODEL), F32), jax.ShapeDtypeStruct((1, D_MODEL), F32)],
        compiler_params=_params(),
    )(after, dzp, dq, dkv, dzg, w_in, x, dh1, g_mix)


def _all_gather_weights(name, shards, after=None):
    n = len(shards)
    extra = [] if after is None else [after]
    n_extra = len(extra)

    def body(*refs):
        ins, outs = refs[:n], refs[n + n_extra:2 * n + n_extra]
        send_sems, recv_sems, local_sems = refs[2 * n + n_extra:]
        x, y, c = lax.axis_index("x"), lax.axis_index("y"), lax.axis_index("c")
        me, sibling = (x, y, c), (x, y, 1 - c)
        chips = [(1 - x, y), (x, 1 - y), (1 - x, 1 - y)]

        def slot(a, px, py, pc):
            return outs[a].at[4 * px + 2 * py + pc]

        def copy(a, k, block, to, src=None):
            return pltpu.make_async_remote_copy(
                src_ref=slot(a, *block) if src is None else src, dst_ref=slot(a, *block),
                send_sem=send_sems.at[a, k], recv_sem=recv_sems.at[a, k], device_id=to, device_id_type=MESH)

        mine = [pltpu.make_async_copy(ins[a], slot(a, *me), local_sems.at[a]) for a in range(n)]
        for cp in mine:
            cp.start()
        first = []
        for a in range(n):
            first.append(copy(a, 0, me, sibling, src=ins[a]))
            first += [copy(a, 1 + j, me, (*chip, c), src=ins[a]) for j, chip in enumerate(chips)]
        for cp in first:
            cp.start()
        passed = []
        for a in range(n):
            for j, chip in enumerate(chips):
                copy(a, 1 + j, (*chip, c), me).wait_recv()
                cp = copy(a, 4 + j, (*chip, c), sibling)
                cp.start()
                passed.append(cp)
        for a in range(n):
            copy(a, 0, sibling, me).wait_recv()
            for j, chip in enumerate(chips):
                copy(a, 4 + j, (*chip, 1 - c), me).wait_recv()
        for cp in first + passed:
            cp.wait_send()
        for cp in mine:
            cp.wait()

    return pl.pallas_call(
        body, name=name,
        in_specs=[ANY] * (n + n_extra), out_specs=[ANY] * n,
        out_shape=[jax.ShapeDtypeStruct((N_DEV,) + s.shape, s.dtype) for s in shards],
        scratch_shapes=[pltpu.SemaphoreType.DMA((n, 7)), pltpu.SemaphoreType.DMA((n, 7)), pltpu.SemaphoreType.DMA((n,))],
    )(*shards, *extra)


HBM_SPEC = pl.BlockSpec(memory_space=pltpu.HBM)
SEM_SPEC = pl.BlockSpec(memory_space=pltpu.SEMAPHORE)
DATAFLOW = pltpu.SideEffectType.DATAFLOW_SIDE_EFFECTING
N_PEERS = N_DEV - 1


def _peer_copies(srcs, lands, scatter, send_sems, recv_sems):
    x, y, c = lax.axis_index("x"), lax.axis_index("y"), lax.axis_index("c")
    me_idx = 4 * x + 2 * y + c
    copies = []
    for k in range(1, N_DEV):
        px = 1 - x if (k >> 2) & 1 else x
        py = 1 - y if (k >> 1) & 1 else y
        pc = 1 - c if k & 1 else c
        p_idx = 4 * px + 2 * py + pc
        for a in range(len(srcs)):
            src = srcs[a].at[p_idx] if scatter[a] else srcs[a]
            dst = lands[a].at[k] if scatter[a] else lands[a].at[me_idx]
            copies.append(pltpu.make_async_remote_copy(
                src_ref=src, dst_ref=dst, send_sem=send_sems.at[a * N_PEERS + k - 1],
                recv_sem=recv_sems.at[a * N_PEERS + k - 1],
                device_id=(px, py, pc), device_id_type=MESH))
    return copies


def _exchange_start(name, srcs, scatter, after):
    n = len(srcs)
    lands = [lax.empty(s.shape if sc else (N_DEV,) + s.shape, s.dtype) for s, sc in zip(srcs, scatter)]

    def body(*refs):
        src_refs, land_refs = refs[:n], refs[n:2 * n]
        send_sems, recv_sems = refs[2 * n + 1], refs[2 * n + 2]
        token = refs[4 * n + 3]
        for cp in _peer_copies(src_refs, land_refs, scatter, send_sems, recv_sems):
            cp.start()
        token[...] = jnp.zeros_like(token)

    hbm = lambda t: pltpu.HBM(t.shape, t.dtype)
    outs = pl.pallas_call(
        body, name=name,
        out_shape=[pltpu.SemaphoreType.DMA((n * N_PEERS,)), pltpu.SemaphoreType.DMA((n * N_PEERS,))]
        + [hbm(t) for t in srcs] + [hbm(t) for t in lands] + [jax.ShapeDtypeStruct((8, LANES), F32)],
        in_specs=[HBM_SPEC] * (2 * n) + [ANY],
        out_specs=[SEM_SPEC, SEM_SPEC] + [HBM_SPEC] * (2 * n) + [pl.BlockSpec(memory_space=pltpu.VMEM)],
        input_output_aliases={i: 2 + i for i in range(2 * n)},
        compiler_params=pltpu.CompilerParams(has_side_effects=DATAFLOW),
    )(*[pltpu.with_memory_space_constraint(t, pltpu.HBM) for t in list(srcs) + lands], after)
    return dict(n=n, scatter=scatter, send_sems=outs[0], recv_sems=outs[1], srcs=outs[2:2 + n],
                lands=outs[2 + n:2 + 2 * n], token=outs[2 + 2 * n])


def _exchange_wait(name, handle, after):
    n, scatter = handle["n"], handle["scatter"]

    def body(*refs):
        src_refs, land_refs = refs[:n], refs[n:2 * n]
        send_sems, recv_sems = refs[2 * n], refs[2 * n + 1]
        for cp in _peer_copies(src_refs, land_refs, scatter, send_sems, recv_sems):
            cp.wait_send()
            cp.wait_recv()

    both = list(handle["srcs"]) + list(handle["lands"])
    outs = pl.pallas_call(
        body, name=name,
        out_shape=[pltpu.HBM(t.shape, t.dtype) for t in both],
        in_specs=[HBM_SPEC] * (2 * n) + [SEM_SPEC, SEM_SPEC, ANY],
        out_specs=[HBM_SPEC] * (2 * n),
        input_output_aliases={i: i for i in range(2 * n)},
        compiler_params=pltpu.CompilerParams(has_side_effects=DATAFLOW),
    )(*both, handle["send_sems"], handle["recv_sems"], after)
    me_idx = _my_index()
    lands = [land if sc else lax.dynamic_update_index_in_dim(land, src, me_idx, 0)
             for land, src, sc in zip(outs[n:], outs[:n], scatter)]
    return lands, outs[:n]


def _my_index():
    return 4 * lax.axis_index("x") + 2 * lax.axis_index("y") + lax.axis_index("c")


def _adamw(parts, w, m, v, sent=None):
    r, c = w.shape
    tr = 256 if r % 256 == 0 else r
    own = sent is not None

    def body(*refs):
        if own:
            _, p_ref, own_ref, w_ref, m_ref, v_ref, g_ref, d_ref, nm_ref, nv_ref = refs
            g = own_ref[...].astype(F32)
        else:
            p_ref, w_ref, m_ref, v_ref, g_ref, d_ref, nm_ref, nv_ref = refs
            g = p_ref[0].astype(F32)
        for k in range(1, N_DEV):
            g = g + p_ref[k].astype(F32)
        m_new = ADAM_B1 * m_ref[...] + (1.0 - ADAM_B1) * g
        v_new = ADAM_B2 * v_ref[...] + (1.0 - ADAM_B2) * (g * g)
        m_hat = m_new / (1.0 - ADAM_B1 ** ADAM_STEP)
        v_hat = v_new / (1.0 - ADAM_B2 ** ADAM_STEP)
        g_ref[...] = g
        d_ref[...] = -ADAM_LR * (m_hat / (jnp.sqrt(v_hat) + ADAM_EPS) + ADAM_WD * w_ref[...])
        nm_ref[...] = m_new
        nv_ref[...] = v_new

    out_shape = [jax.ShapeDtypeStruct((r, c), F32)] * 4
    if not own:
        return pl.pallas_call(
            body, name="adamw", grid=(r // tr,),
            in_specs=[pl.BlockSpec((N_DEV, tr, c), lambda i: (0, i, 0))] + [_rows(tr, c)] * 3,
            out_specs=[_rows(tr, c)] * 4, out_shape=out_shape, compiler_params=_params(),
        )(parts, w, m, v)
    rows = pl.BlockSpec((tr, c), lambda i, me: (i, 0))
    return pl.pallas_call(
        body, name="adamw_own", out_shape=out_shape, compiler_params=_params(),
        grid_spec=pltpu.PrefetchScalarGridSpec(
            num_scalar_prefetch=1, grid=(r // tr,),
            in_specs=[pl.BlockSpec((N_DEV, tr, c), lambda i, me: (0, i, 0)),
                      pl.BlockSpec((None, tr, c), lambda i, me: (me[0], i, 0))] + [rows] * 3,
            out_specs=[rows] * 4),
    )(_my_index().reshape(1).astype(jnp.int32), parts, sent, w, m, v)


def _adam_step(g, w, m, v):
    m_new = ADAM_B1 * m + (1.0 - ADAM_B1) * g
    v_new = ADAM_B2 * v + (1.0 - ADAM_B2) * (g * g)
    m_hat = m_new / (1.0 - ADAM_B1 ** ADAM_STEP)
    v_hat = v_new / (1.0 - ADAM_B2 ** ADAM_STEP)
    return -ADAM_LR * (m_hat / (jnp.sqrt(v_hat) + ADAM_EPS) + ADAM_WD * w), m_new, v_new


SMALL_NAMES = ("norm_mix", "pool_w", "pool_b", "pool_scale", "attn_sinks", "norm_mlp", "norm_final")


def _adamw_small(mlp_all, mix_all, scale_all, nmix_all, w, m, v):
    def body(mlp_ref, mix_ref, scale_ref, nmix_ref, *refs):
        ins, outs = refs[:21], refs[21:]

        def total(ref, rows, lanes=slice(None)):
            g = ref[0, rows, lanes]
            for k in range(1, N_DEV):
                g = g + ref[k, rows, lanes]
            return g

        grads = dict(
            norm_mix=total(nmix_ref, slice(0, 1)), pool_w=total(mix_ref, slice(0, MIX_POOL_B)),
            pool_b=total(mix_ref, slice(MIX_POOL_B, MIX_POOL_B + 4)), pool_scale=total(scale_ref, slice(0, 1)),
            attn_sinks=total(mix_ref, slice(MIX_SINKS, MIX_SINKS + 1)),
            norm_mlp=total(mlp_ref, slice(0, 1)), norm_final=total(mlp_ref, slice(1, 2)))
        for i, name in enumerate(SMALL_NAMES):
            g = grads[name]
            d, m_new, v_new = _adam_step(g, ins[3 * i][...], ins[3 * i + 1][...], ins[3 * i + 2][...])
            for ref, val in zip(outs[4 * i:4 * i + 4], (g, d, m_new, v_new)):
                ref[...] = val
        outs[28][...] = jnp.broadcast_to(total(mlp_ref, slice(2, 3), slice(0, LANES)), (8, LANES))

    operands, out_shape = [], []
    for name in SMALL_NAMES:
        operands += [w[name], m[name], v[name]]
        out_shape += [jax.ShapeDtypeStruct(w[name].shape, F32)] * 4
    out_shape.append(jax.ShapeDtypeStruct((8, LANES), F32))
    outs = pl.pallas_call(body, name="adamw_small", out_shape=out_shape)(
        mlp_all, mix_all, scale_all, nmix_all, *operands)
    return {name: outs[4 * i:4 * i + 4] for i, name in enumerate(SMALL_NAMES)}, outs[28]


def kernel(x, norm_mix, w_in, pool_w, pool_b, pool_scale, attn_sinks, p_pool, p_attn, w_out, norm_mlp, w_up, w_down, norm_final, loss_target, m_norm_mix, m_w_in, m_pool_w, m_pool_b, m_pool_scale, m_attn_sinks, m_p_pool, m_p_attn, m_w_out, m_norm_mlp, m_w_up, m_w_down, m_norm_final, v_norm_mix, v_w_in, v_pool_w, v_pool_b, v_pool_scale, v_attn_sinks, v_p_pool, v_p_attn, v_w_out, v_norm_mlp, v_w_up, v_w_down, v_norm_final):
    xs = x[0]
    tgt = loss_target[0]
    s_len = xs.shape[0]

    w_in_bf, p_pool_bf, p_attn_bf, w_out_bf, w_up_bf, w_down_bf = [
        t[0].astype(BF16) for t in (w_in, p_pool, p_attn, w_out, w_up, w_down)]
    (w_in_g,) = _all_gather_weights("all_gather_w_in", [w_in_bf])
    ag_proj = _exchange_start("ag_proj_start", [p_pool_bf, p_attn_bf, w_out_bf], (False,) * 3, w_in_g)
    ag_mlp = _exchange_start("ag_mlp_start", [w_up_bf, w_down_bf], (False,) * 2, ag_proj["token"])
    w_in_f = w_in_g.transpose(1, 0, 2).reshape(D_MODEL, IN_WIDTH)

    pool_w_bf = pool_w[0].astype(BF16)
    pool_b_row = pool_b[0].reshape(1, POOL_WIDTH)
    bias_t, sink_row = _attn_constants(attn_sinks[0])

    u, zp, q, kv, zg = _fwd_in(ag_mlp["token"], xs, norm_mix, w_in_f)
    pm, o = _mixers_fwd(zp, q, kv, pool_w_bf, pool_b_row, pool_scale, bias_t, sink_row)
    (p_pool_g, p_attn_g, w_out_g), _ = _exchange_wait("ag_proj_wait", ag_proj, pm)
    p_pool_f = p_pool_g.transpose(1, 0, 2).reshape(POOL_WIDTH, D_MODEL)
    p_attn_f = p_attn_g.transpose(1, 0, 2).reshape(ATTN_WIDTH, D_MODEL)
    w_out_f = w_out_g.reshape(D_MODEL, D_MODEL)
    h1, mixed = _mix_out(xs, pm, o, zg, p_pool_f, p_attn_f, w_out_f)
    (w_up_g, w_down_g), _ = _exchange_wait("ag_mlp_wait", ag_mlp, h1)
    w_down_f = w_down_g.reshape(D_FF, D_MODEL)
    dh1, a, dapre, u2, dh2, small_mlp = _mlp_loss(
        h1, tgt, norm_mlp, norm_final.reshape(1, D_MODEL), w_up_g, w_down_f)
    gw_down = _tn_matmul(a, dh2, square_a=True)
    gw_up = _tn_matmul(u2, dapre, col_blocks=N_DEV)
    ex_mlp = _exchange_start(
        "ex_mlp_start", [gw_up, gw_down.reshape(N_DEV, D_FF // N_DEV, D_MODEL)], (True, True), small_mlp)
    dyp, dya, dzg, dpm, do = _mix_bwd(ex_mlp["token"], dh1, pm, o, zg, p_pool_f, p_attn_f, w_out_f)
    gw_out = _tn_matmul(mixed, dh1)
    gp_pool = _tn_matmul(pm, dyp, col_blocks=N_DEV)
    gp_attn = _tn_matmul(o, dya, col_blocks=N_DEV)
    ex_proj = _exchange_start(
        "ex_proj_start", [gp_pool, gp_attn, gw_out.reshape(N_DEV, D_MODEL // N_DEV, D_MODEL)], (True,) * 3, small_mlp)
    dzp, dq, dkv, small_mix, g_pool_scale = _mixers_bwd(
        ex_proj["token"], zp, q, kv, dpm, do, pool_w_bf, pool_b_row, pool_scale, bias_t, sink_row)
    gw_in = jnp.concatenate(
        [_tn_matmul(u, dzp), _tn_matmul(u, dq), _tn_matmul(u, dkv), _tn_matmul(u, dzg)], axis=1)
    ex_in = _exchange_start(
        "ex_in_start", [gw_in.reshape(D_MODEL, N_DEV, IN_WIDTH // N_DEV).transpose(1, 0, 2),
                        small_mlp, small_mix, g_pool_scale], (True, False, False, False), gw_in)
    dx, g_norm_mix = _in_bwd(ex_in["token"], dzp, dq, dkv, dzg, w_in_f, xs, dh1, norm_mix)

    big_w = dict(w_in=w_in, p_pool=p_pool, p_attn=p_attn, w_out=w_out, w_up=w_up, w_down=w_down)
    big_m = dict(w_in=m_w_in, p_pool=m_p_pool, p_attn=m_p_attn, w_out=m_w_out, w_up=m_w_up, w_down=m_w_down)
    big_v = dict(w_in=v_w_in, p_pool=v_p_pool, p_attn=v_p_attn, w_out=v_w_out, w_up=v_w_up, w_down=v_w_down)
    res = {}

    def update(names, recvs, sents):
        for name, parts, sent in zip(names, recvs, sents):
            outs = _adamw(parts, big_w[name][0], big_m[name][0], big_v[name][0], sent)
            res[name] = [t[None] for t in outs]

    update(["w_up", "w_down"], *_exchange_wait("ex_mlp_wait", ex_mlp, dx))
    update(["p_pool", "p_attn", "w_out"], *_exchange_wait("ex_proj_wait", ex_proj, res["w_down"][0]))
    (norm_mix_all,) = _all_gather_weights("all_gather_norm_mix", [g_norm_mix], res["w_out"][0])
    (r_in, mlp_all, mix_all, scale_all), (s_in, _, _, _) = _exchange_wait("ex_in_wait", ex_in, norm_mix_all)
    update(["w_in"], [r_in], [s_in])

    natural = dict(norm_mix=(1, D_MODEL), pool_w=(MIX_POOL_B, LANES), pool_b=(4, LANES), pool_scale=(1, POOL_WIDTH),
                   attn_sinks=(1, LANES), norm_mlp=(1, D_MODEL), norm_final=(1, D_MODEL))

    def as_2d(t, name):
        if name == "attn_sinks":
            return jnp.pad(t, ((0, 0), (0, LANES - N_HEADS)))
        return t.reshape(natural[name])

    small_w = dict(norm_mix=norm_mix, pool_w=pool_w, pool_b=pool_b, pool_scale=pool_scale, attn_sinks=attn_sinks,
                   norm_mlp=norm_mlp, norm_final=norm_final)
    small_m = dict(norm_mix=m_norm_mix, pool_w=m_pool_w, pool_b=m_pool_b, pool_scale=m_pool_scale,
                   attn_sinks=m_attn_sinks, norm_mlp=m_norm_mlp, norm_final=m_norm_final)
    small_v = dict(norm_mix=v_norm_mix, pool_w=v_pool_w, pool_b=v_pool_b, pool_scale=v_pool_scale,
                   attn_sinks=v_attn_sinks, norm_mlp=v_norm_mlp, norm_final=v_norm_final)
    small_res, loss_all = _adamw_small(
        mlp_all, mix_all, scale_all, norm_mix_all,
        *[{k: as_2d(t, k) for k, t in d.items()} for d in (small_w, small_m, small_v)])
    loss = loss_all[0, 0]
    for name in SMALL_NAMES:
        shape = small_w[name].shape
        res[name] = [(t[:, :N_HEADS] if name == "attn_sinks" else t).reshape(shape) for t in small_res[name]]

    order = ["norm_mix", "w_in", "pool_w", "pool_b", "pool_scale", "attn_sinks", "p_pool", "p_attn", "w_out",
             "norm_mlp", "w_up", "w_down", "norm_final"]
    out = [loss, dx[None]]
    for kind in range(4):
        out += [res[name][kind] for name in order]
    return tuple(out)
```

```python
import functools
import math

import numpy as np
import jax
import jax.numpy as jnp
from jax import lax
from jax.experimental import pallas as pl
from jax.experimental.pallas import tpu as pltpu

F32 = jnp.float32
BF16 = jnp.bfloat16

D_MODEL = 1024
POOL_WIDTH = 512
ATTN_WIDTH = 512
KV_WIDTH = 128
HEAD_DIM = 64
N_HEADS = 8
N_KV_HEADS = 2
GROUP = 4
BLOCK = 128
POOL_WINDOWS = (2, 4, 8, 16)
POOL_GROUP_DIM = 128
POOL_HALO = 16
D_FF = 4096
FF_CHUNK = 1024
IN_WIDTH = 3328
RMS_EPS = 1e-5
NEG_INF = -1e30
ATTN_SCALE = 1.0 / math.sqrt(HEAD_DIM)
N_DEV = 8

ADAM_LR = 0.001
ADAM_B1 = 0.9
ADAM_B2 = 0.999
ADAM_EPS = 1e-08
ADAM_WD = 0.01
ADAM_STEP = 10

LANES = 128
VMEM_LIMIT_BYTES = 56 * 1024 * 1024
MESH = pl.DeviceIdType.MESH


def _params(n_grid_axes=1):
    return pltpu.CompilerParams(
        dimension_semantics=("arbitrary",) * n_grid_axes, vmem_limit_bytes=VMEM_LIMIT_BYTES)


def _dot(a, b):
    return jnp.dot(a, b, preferred_element_type=F32)


def _dot_nt(a, b):
    return lax.dot_general(a, b, (((1,), (1,)), ((), ())), preferred_element_type=F32)


def _dot_tn(a, b):
    return lax.dot_general(a, b, (((0,), (0,)), ((), ())), preferred_element_type=F32)


ANY = pl.BlockSpec(memory_space=pl.ANY)


def _rows(tm, n):
    return pl.BlockSpec((tm, n), lambda i: (i, 0))


def _whole(shape):
    zeros = (0,) * len(shape)
    return pl.BlockSpec(shape, lambda i: zeros)


def _rms_fwd(h, g):
    r = lax.rsqrt(jnp.mean(h * h, axis=-1, keepdims=True) + RMS_EPS)
    xh = h * r
    return r, xh, xh * g


def _rms_bwd(dy, xh, r, g):
    dxh = dy * g
    dh = r * (dxh - xh * jnp.mean(dxh * xh, axis=-1, keepdims=True))
    return dh, jnp.sum(dy * xh, axis=0, keepdims=True)


def _fwd_in(after, x, g_mix, w_in_blocks):
    s_len = x.shape[0]
    tm = min(512, s_len)
    width = IN_WIDTH // N_DEV

    def body(after_ref, x_ref, g_ref, wb_ref, u_ref, zp_ref, q_ref, kv_ref, zg_ref, w_ref):
        @pl.when(pl.program_id(0) == 0)
        def _():
            for j in range(N_DEV):
                w_ref[:, j * width:(j + 1) * width] = wb_ref[j]

        _, _, u = _rms_fwd(x_ref[...], g_ref[...])
        u = u.astype(BF16)
        u_ref[...] = u
        zp_ref[...] = _dot(u, w_ref[:, 0:512]).astype(BF16)
        q_ref[...] = _dot(u, w_ref[:, 512:1024]).astype(BF16)
        kv_ref[...] = _dot(u, w_ref[:, 1024:1280]).astype(BF16)
        zg_ref[...] = _dot(u, w_ref[:, 1280:3328]).astype(BF16)

    return pl.pallas_call(
        body, name="fwd_in", grid=(s_len // tm,),
        in_specs=[ANY, _rows(tm, D_MODEL), _whole((1, D_MODEL)),
                  pl.BlockSpec((N_DEV, D_MODEL, width), lambda i: (0, 0, 0), pipeline_mode=pl.Buffered(1))],
        out_specs=[_rows(tm, D_MODEL), _rows(tm, 512), _rows(tm, 512), _rows(tm, 256), _rows(tm, 2048),
                   _whole((D_MODEL, IN_WIDTH))],
        out_shape=[jax.ShapeDtypeStruct((s_len, n), BF16) for n in (D_MODEL, 512, 512, 256, 2048)]
        + [jax.ShapeDtypeStruct((D_MODEL, IN_WIDTH), BF16)],
        compiler_params=_params(),
    )(after, x, g_mix, w_in_blocks)


def _attn_constants(sinks):
    qi = np.arange(BLOCK)[:, None]
    kj = np.arange(2 * BLOCK)[None, :]
    dist = BLOCK + qi - kj
    valid = (dist >= 0) & (dist < BLOCK)
    slopes = np.array([2.0 ** (-8.0 * (h + 1) / N_HEADS) for h in range(N_HEADS)], dtype=np.float32)
    bias = np.where(valid[None], -slopes[:, None, None] * dist.astype(np.float32)[None], np.float32(NEG_INF))
    bias = bias.astype(np.float32).reshape(N_KV_HEADS, GROUP * BLOCK, 2 * BLOCK).transpose(0, 2, 1)
    sink_row = jnp.repeat(sinks.astype(F32).reshape(N_KV_HEADS, GROUP), BLOCK, axis=1)[:, None, :]
    return jnp.asarray(np.ascontiguousarray(bias)), sink_row


def _left_half(shape):
    return lax.broadcasted_iota(jnp.int32, shape, 1) < HEAD_DIM


def _dup_halves(slab):
    swapped = pltpu.roll(slab, HEAD_DIM, 1)
    left = _left_half(slab.shape)
    return jnp.where(left, slab, swapped), jnp.where(left, swapped, slab)


def _fill_kv_slabs(kvh_ref, kv_ref, ka_ref, vd_ref):
    for rows, src in ((slice(0, BLOCK), kvh_ref), (slice(BLOCK, None), kv_ref)):
        kvf = src[...].astype(F32)
        for ref, lanes in ((ka_ref, slice(0, KV_WIDTH)), (vd_ref, slice(KV_WIDTH, 2 * KV_WIDTH))):
            d0, d1 = _dup_halves(kvf[:, lanes])
            ref[0, rows, :] = d0.astype(BF16)
            ref[1, rows, :] = d1.astype(BF16)


def _stack_pairs(a, h):
    pieces = []
    for j in range(2):
        pair = a[:, h * 256 + j * LANES:h * 256 + (j + 1) * LANES]
        left = _left_half(pair.shape)
        zero = jnp.zeros_like(pair)
        pieces += [jnp.where(left, pair, zero), jnp.where(left, zero, pair)]
    return jnp.concatenate(pieces, axis=0)


def _attn_probs_t(kk, q_st, bias_t, sink_row, first):
    s = _dot_nt(kk, q_st) * ATTN_SCALE + bias_t
    if first is not None:
        row = lax.broadcasted_iota(jnp.int32, s.shape, 0)
        s = jnp.where(jnp.logical_and(first, row < BLOCK), NEG_INF, s)
    m = jnp.maximum(jnp.max(s, axis=0, keepdims=True), sink_row)
    p = jnp.exp(s - m)
    es = jnp.exp(sink_row - m)
    inv = 1.0 / (jnp.sum(p, axis=0, keepdims=True) + es)
    return p * inv, es * inv


def _pool_d(ext, cur, g, row0):
    w = POOL_WINDOWS[g]
    acc = ext
    k = 1
    while k < w:
        acc = acc + pltpu.roll(acc, k, 0)
        k *= 2
    t = row0 + lax.broadcasted_iota(jnp.int32, cur.shape, 0)
    cnt = jnp.minimum(t + 1, w).astype(F32)
    return acc[POOL_HALO:, :] / cnt - cur


def _mixers_fwd(zp, q, kv, pool_w, pool_b, pool_scale, bias_t, sink_row):
    s_len = zp.shape[0]
    tq = min(512, s_len)
    nb = tq // BLOCK

    def body(zp_ref, zph_ref, q_ref, kv_ref, kvh_ref, pw_ref, pb_ref, ps_ref, bias_ref, sink_ref,
             pm_ref, o_ref, ka_ref, vd_ref):
        i = pl.program_id(0)
        cur = zp_ref[...].astype(F32)
        halo = zph_ref[...].astype(F32) * (i > 0).astype(F32)
        ext = jnp.concatenate([halo, cur], axis=0)
        for g in range(4):
            sl = slice(g * POOL_GROUP_DIM, (g + 1) * POOL_GROUP_DIM)
            d = _pool_d(ext[:, sl], cur[:, sl], g, i * tq)
            y = _dot(d.astype(BF16), pw_ref[g]) + pb_ref[:, sl]
            pm_ref[:, sl] = (y * ps_ref[:, sl]).astype(BF16)
        _fill_kv_slabs(kvh_ref, kv_ref, ka_ref, vd_ref)
        for b in range(nb):
            rq = slice(b * BLOCK, (b + 1) * BLOCK)
            rk = slice(b * BLOCK, (b + 2) * BLOCK)
            qb = q_ref[rq, :]
            for h in range(N_KV_HEADS):
                pn, _ = _attn_probs_t(ka_ref[h, rk, :], _stack_pairs(qb, h), bias_ref[h], sink_ref[h],
                                      (i == 0) if b == 0 else None)
                pn = pn.astype(BF16)
                vd = vd_ref[h, rk, :]
                left = _left_half(vd.shape)
                zero = jnp.zeros_like(vd)
                va, vb = jnp.where(left, vd, zero), jnp.where(left, zero, vd)
                for j in range(2):
                    o_pair = (_dot_tn(pn[:, (2 * j) * BLOCK:(2 * j + 1) * BLOCK], va)
                              + _dot_tn(pn[:, (2 * j + 1) * BLOCK:(2 * j + 2) * BLOCK], vb))
                    o_ref[rq, h * 256 + j * LANES:h * 256 + (j + 1) * LANES] = o_pair.astype(BF16)

    halo_pool = pl.BlockSpec((POOL_HALO, 512), lambda i: (jnp.maximum(i * (tq // POOL_HALO) - 1, 0), 0))
    halo_kv = pl.BlockSpec((BLOCK, 256), lambda i: (jnp.maximum(i * nb - 1, 0), 0))
    return pl.pallas_call(
        body, name="mixers_fwd", grid=(s_len // tq,),
        in_specs=[_rows(tq, 512), halo_pool, _rows(tq, 512), _rows(tq, 256), halo_kv,
                  _whole((4, 128, 128)), _whole((1, 512)), _whole((1, 512)),
                  _whole((N_KV_HEADS, 2 * BLOCK, GROUP * BLOCK)), _whole((N_KV_HEADS, 1, GROUP * BLOCK))],
        out_specs=[_rows(tq, 512), _rows(tq, 512)],
        out_shape=[jax.ShapeDtypeStruct((s_len, 512), BF16)] * 2,
        scratch_shapes=[pltpu.VMEM((N_KV_HEADS, tq + BLOCK, LANES), BF16)] * 2,
        compiler_params=_params(),
    )(zp, zp, q, kv, kv, pool_w, pool_b, pool_scale, bias_t, sink_row)


def _gated_mix(pm, o, zg, pp_ref, pa_ref):
    yp = _dot(pm, pp_ref[...])
    ya = _dot(o, pa_ref[...])
    gp = jax.nn.sigmoid(zg[:, :D_MODEL].astype(F32))
    ga = jax.nn.sigmoid(zg[:, D_MODEL:].astype(F32))
    return yp, ya, gp, ga


def _mix_out(x, pm, o, zg, p_pool, p_attn, w_out):
    s_len = x.shape[0]
    tm = min(512, s_len)

    def body(x_ref, pm_ref, o_ref, zg_ref, pp_ref, pa_ref, wo_ref, h1_ref, mixed_ref):
        yp, ya, gp, ga = _gated_mix(pm_ref[...], o_ref[...], zg_ref[...], pp_ref, pa_ref)
        mixed = (gp * yp + ga * ya).astype(BF16)
        mixed_ref[...] = mixed
        h1_ref[...] = x_ref[...] + _dot(mixed, wo_ref[...])

    return pl.pallas_call(
        body, name="mix_out", grid=(s_len // tm,),
        in_specs=[_rows(tm, D_MODEL), _rows(tm, 512), _rows(tm, 512), _rows(tm, 2048),
                  _whole((512, D_MODEL)), _whole((512, D_MODEL)), _whole((D_MODEL, D_MODEL))],
        out_specs=[_rows(tm, D_MODEL), _rows(tm, D_MODEL)],
        out_shape=[jax.ShapeDtypeStruct((s_len, D_MODEL), F32), jax.ShapeDtypeStruct((s_len, D_MODEL), BF16)],
        compiler_params=_params(),
    )(x, pm, o, zg, p_pool, p_attn, w_out)


def _mlp_loss(h1, tgt, g_mlp, g_fin, w_up_blocks, w_down):
    s_len = h1.shape[0]
    tm = min(256, s_len)
    n_chunks = D_FF // FF_CHUNK
    up_block = D_FF // N_DEV
    per_chunk = FF_CHUNK // up_block

    def body(h1_ref, tgt_ref, gm_ref, gf_ref, wu_ref, wd_ref,
             dh1_ref, a_ref, dap_ref, u2_ref, dh2_ref, small_ref):
        i = pl.program_id(0)

        @pl.when(i == 0)
        def _():
            small_ref[...] = jnp.zeros_like(small_ref)

        h1 = h1_ref[...]
        r2, xh2, u2 = _rms_fwd(h1, gm_ref[...])
        u2 = u2.astype(BF16)
        u2_ref[...] = u2
        acc = jnp.zeros((tm, D_MODEL), F32)
        for c in range(n_chunks):
            cs = slice(c * FF_CHUNK, (c + 1) * FF_CHUNK)
            a = jnp.concatenate([_dot(u2, wu_ref[per_chunk * c + j]) for j in range(per_chunk)], axis=1)
            a = jnp.maximum(a, 0.0)
            a_ref[:, cs] = a.astype(BF16)
            acc = acc + _dot((a * a).astype(BF16), wd_ref[cs, :])
        h2 = h1 + acc
        r3, xh3, y = _rms_fwd(h2, gf_ref[...])
        diff = y - tgt_ref[...]
        small_ref[2:3, :] += 0.5 * jnp.sum(jnp.mean(diff * diff, axis=-1, keepdims=True))
        dy = diff * (1.0 / D_MODEL)
        dh2, dgf = _rms_bwd(dy, xh3, r3, gf_ref[...])
        small_ref[1:2, :] += dgf
        dh2_bf = dh2.astype(BF16)
        dh2_ref[...] = dh2_bf
        du2 = jnp.zeros((tm, D_MODEL), F32)
        for c in range(n_chunks):
            cs = slice(c * FF_CHUNK, (c + 1) * FF_CHUNK)
            ds = _dot_nt(dh2_bf, wd_ref[cs, :])
            dap = (ds * (2.0 * a_ref[:, cs].astype(F32))).astype(BF16)
            dap_ref[:, cs] = dap
            for j in range(per_chunk):
                du2 = du2 + _dot_nt(dap[:, j * up_block:(j + 1) * up_block], wu_ref[per_chunk * c + j])
        dh1n, dgm = _rms_bwd(du2, xh2, r2, gm_ref[...])
        small_ref[0:1, :] += dgm
        dh1_ref[...] = dh2 + dh1n

    single = dict(pipeline_mode=pl.Buffered(1))
    return pl.pallas_call(
        body, name="mlp_loss", grid=(s_len // tm,),
        in_specs=[_rows(tm, D_MODEL), _rows(tm, D_MODEL), _whole((1, D_MODEL)), _whole((1, D_MODEL)),
                  pl.BlockSpec((N_DEV, D_MODEL, up_block), lambda i: (0, 0, 0), **single),
                  pl.BlockSpec((D_FF, D_MODEL), lambda i: (0, 0), **single)],
        out_specs=[_rows(tm, D_MODEL), _rows(tm, D_FF), _rows(tm, D_FF), _rows(tm, D_MODEL), _rows(tm, D_MODEL),
                   _whole((8, D_MODEL))],
        out_shape=[jax.ShapeDtypeStruct((s_len, D_MODEL), F32), jax.ShapeDtypeStruct((s_len, D_FF), BF16),
                   jax.ShapeDtypeStruct((s_len, D_FF), BF16), jax.ShapeDtypeStruct((s_len, D_MODEL), BF16),
                   jax.ShapeDtypeStruct((s_len, D_MODEL), BF16), jax.ShapeDtypeStruct((8, D_MODEL), F32)],
        compiler_params=_params(),
    )(h1, tgt, g_mlp, g_fin, w_up_blocks, w_down)


def _tn_matmul(a, b, square_a=False, col_blocks=None):
    s_len, ka = a.shape
    nb = b.shape[1]
    tt = min(2048, s_len)
    tk = min(1024, ka)
    tn = min(1024, nb)
    n_t = s_len // tt
    if col_blocks is None:
        out_spec = pl.BlockSpec((tk, tn), lambda k, j, t: (k, j))
        out_shape = jax.ShapeDtypeStruct((ka, nb), BF16)
    else:
        width = nb // col_blocks
        per_tile = tn // width
        out_spec = pl.BlockSpec((per_tile, tk, width), lambda k, j, t: (j, k, 0))
        out_shape = jax.ShapeDtypeStruct((col_blocks, ka, width), BF16)

    def body(a_ref, b_ref, o_ref, acc_ref):
        t = pl.program_id(2)

        @pl.when(t == 0)
        def _():
            acc_ref[...] = jnp.zeros_like(acc_ref)

        av = a_ref[...]
        if square_a:
            av = av * av
        acc_ref[...] += _dot_tn(av.astype(BF16), b_ref[...].astype(BF16))

        @pl.when(t == n_t - 1)
        def _():
            if col_blocks is None:
                o_ref[...] = acc_ref[...].astype(o_ref.dtype)
            else:
                for blk in range(per_tile):
                    o_ref[blk] = acc_ref[:, blk * width:(blk + 1) * width].astype(o_ref.dtype)

    return pl.pallas_call(
        body, name="tn_matmul", grid=(ka // tk, nb // tn, n_t),
        in_specs=[pl.BlockSpec((tt, tk), lambda k, j, t: (t, k)), pl.BlockSpec((tt, tn), lambda k, j, t: (t, j))],
        out_specs=out_spec, out_shape=out_shape,
        scratch_shapes=[pltpu.VMEM((tk, tn), F32)],
        compiler_params=_params(3),
    )(a, b)


def _tn_w_in(u, dzp, dq, dkv, dzg):
    s_len = u.shape[0]
    tt = min(1024, s_len)
    n_t = s_len // tt
    width = IN_WIDTH // N_DEV
    pieces = ((0, 512), (512, 1024), (1024, 1280), (1280, IN_WIDTH))

    def body(u_ref, dzp_ref, dq_ref, dkv_ref, dzg_ref, o_ref, acc_ref):
        t = pl.program_id(0)

        @pl.when(t == 0)
        def _():
            acc_ref[...] = jnp.zeros_like(acc_ref)

        uv = u_ref[...]
        for (c0, c1), ref in zip(pieces, (dzp_ref, dq_ref, dkv_ref, dzg_ref)):
            acc_ref[:, c0:c1] += _dot_tn(uv, ref[...])

        @pl.when(t == n_t - 1)
        def _():
            for j in range(N_DEV):
                o_ref[j] = acc_ref[:, j * width:(j + 1) * width].astype(BF16)

    return pl.pallas_call(
        body, name="tn_w_in", grid=(n_t,),
        in_specs=[_rows(tt, D_MODEL)] + [_rows(tt, c1 - c0) for c0, c1 in pieces],
        out_specs=_whole((N_DEV, D_MODEL, width)),
        out_shape=jax.ShapeDtypeStruct((N_DEV, D_MODEL, width), BF16),
        scratch_shapes=[pltpu.VMEM((D_MODEL, IN_WIDTH), F32)],
        compiler_params=_params(),
    )(u, dzp, dq, dkv, dzg)


def _mix_bwd(after, dh1, pm, o, zg, p_pool, p_attn, w_out):
    s_len = dh1.shape[0]
    tm = min(512, s_len)

    def body(after_ref, dh1_ref, pm_ref, o_ref, zg_ref, pp_ref, pa_ref, wo_ref,
             dyp_ref, dya_ref, dzg_ref, dpm_ref, do_ref):
        dm = _dot_nt(dh1_ref[...].astype(BF16), wo_ref[...])
        yp, ya, gp, ga = _gated_mix(pm_ref[...], o_ref[...], zg_ref[...], pp_ref, pa_ref)
        dyp = (dm * gp).astype(BF16)
        dya = (dm * ga).astype(BF16)
        dyp_ref[...] = dyp
        dya_ref[...] = dya
        dzg_ref[:, :D_MODEL] = (dm * yp * (gp * (1.0 - gp))).astype(BF16)
        dzg_ref[:, D_MODEL:] = (dm * ya * (ga * (1.0 - ga))).astype(BF16)
        dpm_ref[...] = _dot_nt(dyp, pp_ref[...]).astype(BF16)
        do_ref[...] = _dot_nt(dya, pa_ref[...]).astype(BF16)

    return pl.pallas_call(
        body, name="mix_bwd", grid=(s_len // tm,),
        in_specs=[ANY, _rows(tm, D_MODEL), _rows(tm, 512), _rows(tm, 512), _rows(tm, 2048),
                  _whole((512, D_MODEL)), _whole((512, D_MODEL)), _whole((D_MODEL, D_MODEL))],
        out_specs=[_rows(tm, D_MODEL), _rows(tm, D_MODEL), _rows(tm, 2048), _rows(tm, 512), _rows(tm, 512)],
        out_shape=[jax.ShapeDtypeStruct((s_len, n), BF16) for n in (D_MODEL, D_MODEL, 2048, 512, 512)],
        compiler_params=_params(),
    )(after, dh1, pm, o, zg, p_pool, p_attn, w_out)


MIX_POOL_B = 4 * POOL_GROUP_DIM
MIX_SINKS = MIX_POOL_B + 8
MIX_ROWS = MIX_SINKS + 8


def _mixers_bwd(after, zp, q, kv, dpm, do, pool_w, pool_b, pool_scale, bias_t, sink_row):
    s_len = zp.shape[0]
    tq = min(512, s_len)
    nb = tq // BLOCK
    n_steps = s_len // tq

    def body(after_ref, zp_ref, zph_ref, q_ref, kv_ref, kvh_ref, dpm_ref, dpmh_ref, do_ref, pw_ref, pb_ref, ps_ref,
             bias_ref, sink_ref, dzp_ref, dq_ref, dkv_ref, small_ref, dps_ref,
             ka_ref, vd_ref, dsk_acc, dkv_acc):
        i = pl.program_id(0)

        @pl.when(i == 0)
        def _():
            dkv_acc[...] = jnp.zeros_like(dkv_acc)
            small_ref[...] = jnp.zeros_like(small_ref)
            dps_ref[...] = jnp.zeros_like(dps_ref)
            dsk_acc[...] = jnp.zeros_like(dsk_acc)

        cur = zp_ref[...].astype(F32)
        halo = zph_ref[...].astype(F32) * (i > 0).astype(F32)
        ext = jnp.concatenate([halo, cur], axis=0)
        dpm_next = dpmh_ref[...].astype(F32) * (i < n_steps - 1).astype(F32)
        dpm_ext = jnp.concatenate([dpm_ref[...].astype(F32), dpm_next], axis=0)
        n_ext = tq + POOL_HALO
        for g in range(4):
            sl = slice(g * POOL_GROUP_DIM, (g + 1) * POOL_GROUP_DIM)
            w = POOL_WINDOWS[g]
            d = _pool_d(ext[:, sl], cur[:, sl], g, i * tq).astype(BF16)
            y_lin = _dot(d, pw_ref[g]) + pb_ref[:, sl]
            dps_ref[:, sl] += jnp.sum(dpm_ext[:tq, sl] * y_lin, axis=0, keepdims=True)
            dyl_ext = dpm_ext[:, sl] * ps_ref[:, sl]
            small_ref[MIX_POOL_B + g:MIX_POOL_B + g + 1, :] += jnp.sum(dyl_ext[:tq], axis=0, keepdims=True)
            dyl_bf = dyl_ext.astype(BF16)
            small_ref[g * POOL_GROUP_DIM:(g + 1) * POOL_GROUP_DIM, :] += _dot_tn(d, dyl_bf[:tq])
            dd = _dot_nt(dyl_bf, pw_ref[g])
            t = i * tq + lax.broadcasted_iota(jnp.int32, dd.shape, 0)
            e = dd / jnp.minimum(t + 1, w).astype(F32)
            acc = e
            k = 1
            while k < w:
                acc = acc + pltpu.roll(acc, n_ext - k, 0)
                k *= 2
            dzp_ref[:, sl] = (acc[:tq] - dd[:tq]).astype(BF16)

        _fill_kv_slabs(kvh_ref, kv_ref, ka_ref, vd_ref)

        def fold(dup):
            return dup + pltpu.roll(dup, HEAD_DIM, 1)

        for b in range(nb):
            rq = slice(b * BLOCK, (b + 1) * BLOCK)
            rk = slice(b * BLOCK, (b + 2) * BLOCK)
            qb = q_ref[rq, :]
            dob = do_ref[rq, :]
            dk_dup, dv_dup = [], []
            for h in range(N_KV_HEADS):
                kk = ka_ref[h, rk, :]
                q_st = _stack_pairs(qb, h)
                do_st = _stack_pairs(dob, h)
                pn, psink = _attn_probs_t(kk, q_st, bias_ref[h], sink_ref[h], (i == 0) if b == 0 else None)
                dp = _dot_nt(vd_ref[h, rk, :], do_st)
                delta = jnp.sum(pn * dp, axis=0, keepdims=True)
                dsk_acc[h] += -psink * delta
                ds = ((pn * (dp - delta)) * ATTN_SCALE).astype(BF16)
                dq_st = _dot_tn(ds, kk)
                for j in range(2):
                    left = _left_half((BLOCK, LANES))
                    dq_pair = jnp.where(left, dq_st[(2 * j) * BLOCK:(2 * j + 1) * BLOCK],
                                        dq_st[(2 * j + 1) * BLOCK:(2 * j + 2) * BLOCK])
                    dq_ref[rq, h * 256 + j * LANES:h * 256 + (j + 1) * LANES] = dq_pair.astype(BF16)
                dk_dup.append(fold(_dot(ds, q_st)))
                dv_dup.append(fold(_dot(pn.astype(BF16), do_st)))
            left = _left_half((2 * BLOCK, LANES))
            dkv_blk = jnp.concatenate([jnp.where(left, dk_dup[0], dk_dup[1]),
                                       jnp.where(left, dv_dup[0], dv_dup[1])], axis=1)
            g0 = pl.multiple_of(i * tq + b * BLOCK, BLOCK)
            dkv_acc[pl.ds(g0, 2 * BLOCK), :] += dkv_blk

        @pl.when(i == n_steps - 1)
        def _():
            dkv_ref[...] = dkv_acc[BLOCK:, :].astype(BF16)
            lane = lax.broadcasted_iota(jnp.int32, (1, LANES), 1)
            row = jnp.zeros((1, LANES), F32)
            for h in range(N_KV_HEADS):
                for g in range(GROUP):
                    tot = jnp.sum(dsk_acc[h, :, g * BLOCK:(g + 1) * BLOCK], axis=1, keepdims=True)
                    row = jnp.where(lane == GROUP * h + g, tot, row)
            small_ref[MIX_SINKS:MIX_SINKS + 1, :] = row

    blocks_per_tile = tq // POOL_HALO
    last_halo = s_len // POOL_HALO - 1
    halo_prev = pl.BlockSpec((POOL_HALO, 512), lambda i: (jnp.maximum(i * blocks_per_tile - 1, 0), 0))
    halo_next = pl.BlockSpec((POOL_HALO, 512), lambda i: (jnp.minimum((i + 1) * blocks_per_tile, last_halo), 0))
    halo_kv = pl.BlockSpec((BLOCK, 256), lambda i: (jnp.maximum(i * nb - 1, 0), 0))
    return pl.pallas_call(
        body, name="mixers_bwd", grid=(n_steps,),
        in_specs=[ANY, _rows(tq, 512), halo_prev, _rows(tq, 512), _rows(tq, 256), halo_kv,
                  _rows(tq, 512), halo_next, _rows(tq, 512),
                  _whole((4, 128, 128)), _whole((1, 512)), _whole((1, 512)),
                  _whole((N_KV_HEADS, 2 * BLOCK, GROUP * BLOCK)), _whole((N_KV_HEADS, 1, GROUP * BLOCK))],
        out_specs=[_rows(tq, 512), _rows(tq, 512), _whole((s_len, 256)),
                   _whole((MIX_ROWS, LANES)), _whole((1, 512))],
        out_shape=[jax.ShapeDtypeStruct((s_len, 512), BF16), jax.ShapeDtypeStruct((s_len, 512), BF16),
                   jax.ShapeDtypeStruct((s_len, 256), BF16), jax.ShapeDtypeStruct((MIX_ROWS, LANES), F32),
                   jax.ShapeDtypeStruct((1, 512), F32)],
        scratch_shapes=[pltpu.VMEM((N_KV_HEADS, tq + BLOCK, LANES), BF16)] * 2
        + [pltpu.VMEM((N_KV_HEADS, 1, GROUP * BLOCK), F32), pltpu.VMEM((s_len + BLOCK, 256), F32)],
        compiler_params=_params(),
    )(after, zp, zp, q, kv, kv, dpm, dpm, do, pool_w, pool_b, pool_scale, bias_t, sink_row)


def _in_bwd(after, dzp, dq, dkv, dzg, w_in, x, dh1, g_mix):
    s_len = x.shape[0]
    tm = min(512, s_len)

    def body(after_ref, dzp_ref, dq_ref, dkv_ref, dzg_ref, w_ref, x_ref, dh1_ref, g_ref, dx_ref, dg_ref):
        i = pl.program_id(0)

        @pl.when(i == 0)
        def _():
            dg_ref[...] = jnp.zeros_like(dg_ref)

        du = _dot_nt(dzp_ref[...], w_ref[:, 0:512])
        du = du + _dot_nt(dq_ref[...], w_ref[:, 512:1024])
        du = du + _dot_nt(dkv_ref[...], w_ref[:, 1024:1280])
        du = du + _dot_nt(dzg_ref[...], w_ref[:, 1280:3328])
        r, xh, _ = _rms_fwd(x_ref[...], g_ref[...])
        dxn, dg = _rms_bwd(du, xh, r, g_ref[...])
        dg_ref[...] += dg
        dx_ref[...] = dh1_ref[...] + dxn

    return pl.pallas_call(
        body, name="in_bwd", grid=(s_len // tm,),
        in_specs=[ANY, _rows(tm, 512), _rows(tm, 512), _rows(tm, 256), _rows(tm, 2048), _whole((D_MODEL, IN_WIDTH)),
                  _rows(tm, D_MODEL), _rows(tm, D_MODEL), _whole((1, D_MODEL))],
        out_specs=[_rows(tm, D_MODEL), _whole((1, D_MODEL))],
        out_shape=[jax.ShapeDtypeStruct((s_len, D_MODEL), F32), jax.ShapeDtypeStruct((1, D_MODEL), F32)],
        compiler_params=_params(),
    )(after, dzp, dq, dkv, dzg, w_in, x, dh1, g_mix)


def _all_gather_weights(name, shards, after=None):
    n = len(shards)
    extra = [] if after is None else [after]
    n_extra = len(extra)

    def body(*refs):
        ins, outs = refs[:n], refs[n + n_extra:2 * n + n_extra]
        send_sems, recv_sems, local_sems = refs[2 * n + n_extra:]
        x, y, c = lax.axis_index("x"), lax.axis_index("y"), lax.axis_index("c")
        me, sibling = (x, y, c), (x, y, 1 - c)
        chips = [(1 - x, y), (x, 1 - y), (1 - x, 1 - y)]

        def slot(a, px, py, pc):
            return outs[a].at[4 * px + 2 * py + pc]

        def copy(a, k, block, to, src=None):
            return pltpu.make_async_remote_copy(
                src_ref=slot(a, *block) if src is None else src, dst_ref=slot(a, *block),
                send_sem=send_sems.at[a, k], recv_sem=recv_sems.at[a, k], device_id=to, device_id_type=MESH)

        mine = [pltpu.make_async_copy(ins[a], slot(a, *me), local_sems.at[a]) for a in range(n)]
        for cp in mine:
            cp.start()
        first = []
        for a in range(n):
            first.append(copy(a, 0, me, sibling, src=ins[a]))
            first += [copy(a, 1 + j, me, (*chip, c), src=ins[a]) for j, chip in enumerate(chips)]
        for cp in first:
            cp.start()
        passed = []
        for a in range(n):
            for j, chip in enumerate(chips):
                copy(a, 1 + j, (*chip, c), me).wait_recv()
                cp = copy(a, 4 + j, (*chip, c), sibling)
                cp.start()
                passed.append(cp)
        for a in range(n):
            copy(a, 0, sibling, me).wait_recv()
            for j, chip in enumerate(chips):
                copy(a, 4 + j, (*chip, 1 - c), me).wait_recv()
        for cp in first + passed:
            cp.wait_send()
        for cp in mine:
            cp.wait()

    return pl.pallas_call(
        body, name=name,
        in_specs=[ANY] * (n + n_extra), out_specs=[ANY] * n,
        out_shape=[jax.ShapeDtypeStruct((N_DEV,) + s.shape, s.dtype) for s in shards],
        scratch_shapes=[pltpu.SemaphoreType.DMA((n, 7)), pltpu.SemaphoreType.DMA((n, 7)), pltpu.SemaphoreType.DMA((n,))],
    )(*shards, *extra)


HBM_SPEC = pl.BlockSpec(memory_space=pltpu.HBM)
SEM_SPEC = pl.BlockSpec(memory_space=pltpu.SEMAPHORE)
DATAFLOW = pltpu.SideEffectType.DATAFLOW_SIDE_EFFECTING
N_PEERS = N_DEV - 1


def _peer_copies(srcs, lands, scatter, send_sems, recv_sems):
    x, y, c = lax.axis_index("x"), lax.axis_index("y"), lax.axis_index("c")
    me_idx = 4 * x + 2 * y + c
    copies = []
    for k in range(1, N_DEV):
        px = 1 - x if (k >> 2) & 1 else x
        py = 1 - y if (k >> 1) & 1 else y
        pc = 1 - c if k & 1 else c
        p_idx = 4 * px + 2 * py + pc
        for a in range(len(srcs)):
            src = srcs[a].at[p_idx] if scatter[a] else srcs[a]
            dst = lands[a].at[k] if scatter[a] else lands[a].at[me_idx]
            copies.append(pltpu.make_async_remote_copy(
                src_ref=src, dst_ref=dst, send_sem=send_sems.at[a * N_PEERS + k - 1],
                recv_sem=recv_sems.at[a * N_PEERS + k - 1],
                device_id=(px, py, pc), device_id_type=MESH))
    return copies


def _exchange_start(name, srcs, scatter, after):
    n = len(srcs)
    lands = [lax.empty(s.shape if sc else (N_DEV,) + s.shape, s.dtype) for s, sc in zip(srcs, scatter)]

    def body(*refs):
        src_refs, land_refs = refs[:n], refs[n:2 * n]
        send_sems, recv_sems = refs[2 * n + 1], refs[2 * n + 2]
        token = refs[4 * n + 3]
        for cp in _peer_copies(src_refs, land_refs, scatter, send_sems, recv_sems):
            cp.start()
        token[...] = jnp.zeros_like(token)

    hbm = lambda t: pltpu.HBM(t.shape, t.dtype)
    outs = pl.pallas_call(
        body, name=name,
        out_shape=[pltpu.SemaphoreType.DMA((n * N_PEERS,)), pltpu.SemaphoreType.DMA((n * N_PEERS,))]
        + [hbm(t) for t in srcs] + [hbm(t) for t in lands] + [jax.ShapeDtypeStruct((8, LANES), F32)],
        in_specs=[HBM_SPEC] * (2 * n) + [ANY],
        out_specs=[SEM_SPEC, SEM_SPEC] + [HBM_SPEC] * (2 * n) + [pl.BlockSpec(memory_space=pltpu.VMEM)],
        input_output_aliases={i: 2 + i for i in range(2 * n)},
        compiler_params=pltpu.CompilerParams(has_side_effects=DATAFLOW),
    )(*[pltpu.with_memory_space_constraint(t, pltpu.HBM) for t in list(srcs) + lands], after)
    return dict(n=n, scatter=scatter, send_sems=outs[0], recv_sems=outs[1], srcs=outs[2:2 + n],
                lands=outs[2 + n:2 + 2 * n], token=outs[2 + 2 * n])


def _exchange_wait(name, handle, after):
    n, scatter = handle["n"], handle["scatter"]

    def body(*refs):
        src_refs, land_refs = refs[:n], refs[n:2 * n]
        send_sems, recv_sems = refs[2 * n], refs[2 * n + 1]
        for cp in _peer_copies(src_refs, land_refs, scatter, send_sems, recv_sems):
            cp.wait_send()
            cp.wait_recv()

    both = list(handle["srcs"]) + list(handle["lands"])
    outs = pl.pallas_call(
        body, name=name,
        out_shape=[pltpu.HBM(t.shape, t.dtype) for t in both],
        in_specs=[HBM_SPEC] * (2 * n) + [SEM_SPEC, SEM_SPEC, ANY],
        out_specs=[HBM_SPEC] * (2 * n),
        input_output_aliases={i: i for i in range(2 * n)},
        compiler_params=pltpu.CompilerParams(has_side_effects=DATAFLOW),
    )(*both, handle["send_sems"], handle["recv_sems"], after)
    me_idx = _my_index()
    lands = [land if sc else lax.dynamic_update_index_in_dim(land, src, me_idx, 0)
             for land, src, sc in zip(outs[n:], outs[:n], scatter)]
    return lands, outs[:n]


def _my_index():
    return 4 * lax.axis_index("x") + 2 * lax.axis_index("y") + lax.axis_index("c")


def _adamw(parts, w, m, v, sent=None):
    r, c = w.shape
    tr = 256 if r % 256 == 0 else r
    own = sent is not None

    def body(*refs):
        if own:
            _, p_ref, own_ref, w_ref, m_ref, v_ref, g_ref, d_ref, nm_ref, nv_ref = refs
            g = own_ref[...].astype(F32)
        else:
            p_ref, w_ref, m_ref, v_ref, g_ref, d_ref, nm_ref, nv_ref = refs
            g = p_ref[0].astype(F32)
        for k in range(1, N_DEV):
            g = g + p_ref[k].astype(F32)
        m_new = ADAM_B1 * m_ref[...] + (1.0 - ADAM_B1) * g
        v_new = ADAM_B2 * v_ref[...] + (1.0 - ADAM_B2) * (g * g)
        m_hat = m_new / (1.0 - ADAM_B1 ** ADAM_STEP)
        v_hat = v_new / (1.0 - ADAM_B2 ** ADAM_STEP)
        g_ref[...] = g
        d_ref[...] = -ADAM_LR * (m_hat / (jnp.sqrt(v_hat) + ADAM_EPS) + ADAM_WD * w_ref[...])
        nm_ref[...] = m_new
        nv_ref[...] = v_new

    out_shape = [jax.ShapeDtypeStruct((r, c), F32)] * 4
    if not own:
        return pl.pallas_call(
            body, name="adamw", grid=(r // tr,),
            in_specs=[pl.BlockSpec((N_DEV, tr, c), lambda i: (0, i, 0))] + [_rows(tr, c)] * 3,
            out_specs=[_rows(tr, c)] * 4, out_shape=out_shape, compiler_params=_params(),
        )(parts, w, m, v)
    rows = pl.BlockSpec((tr, c), lambda i, me: (i, 0))
    return pl.pallas_call(
        body, name="adamw_own", out_shape=out_shape, compiler_params=_params(),
        grid_spec=pltpu.PrefetchScalarGridSpec(
            num_scalar_prefetch=1, grid=(r // tr,),
            in_specs=[pl.BlockSpec((N_DEV, tr, c), lambda i, me: (0, i, 0)),
                      pl.BlockSpec((None, tr, c), lambda i, me: (me[0], i, 0))] + [rows] * 3,
            out_specs=[rows] * 4),
    )(_my_index().reshape(1).astype(jnp.int32), parts, sent, w, m, v)


def _adam_step(g, w, m, v):
    m_new = ADAM_B1 * m + (1.0 - ADAM_B1) * g
    v_new = ADAM_B2 * v + (1.0 - ADAM_B2) * (g * g)
    m_hat = m_new / (1.0 - ADAM_B1 ** ADAM_STEP)
    v_hat = v_new / (1.0 - ADAM_B2 ** ADAM_STEP)
    return -ADAM_LR * (m_hat / (jnp.sqrt(v_hat) + ADAM_EPS) + ADAM_WD * w), m_new, v_new


SMALL_NAMES = ("norm_mix", "pool_w", "pool_b", "pool_scale", "attn_sinks", "norm_mlp", "norm_final")


def _adamw_small(mlp_all, mix_all, scale_all, nmix_all, w, m, v):
    def body(mlp_ref, mix_ref, scale_ref, nmix_ref, *refs):
        ins, outs = refs[:21], refs[21:]

        def total(ref, rows, lanes=slice(None)):
            g = ref[0, rows, lanes]
            for k in range(1, N_DEV):
                g = g + ref[k, rows, lanes]
            return g

        grads = dict(
            norm_mix=total(nmix_ref, slice(0, 1)), pool_w=total(mix_ref, slice(0, MIX_POOL_B)),
            pool_b=total(mix_ref, slice(MIX_POOL_B, MIX_POOL_B + 4)), pool_scale=total(scale_ref, slice(0, 1)),
            attn_sinks=total(mix_ref, slice(MIX_SINKS, MIX_SINKS + 1)),
            norm_mlp=total(mlp_ref, slice(0, 1)), norm_final=total(mlp_ref, slice(1, 2)))
        for i, name in enumerate(SMALL_NAMES):
            g = grads[name]
            d, m_new, v_new = _adam_step(g, ins[3 * i][...], ins[3 * i + 1][...], ins[3 * i + 2][...])
            for ref, val in zip(outs[4 * i:4 * i + 4], (g, d, m_new, v_new)):
                ref[...] = val
        outs[28][...] = jnp.broadcast_to(total(mlp_ref, slice(2, 3), slice(0, LANES)), (8, LANES))

    operands, out_shape = [], []
    for name in SMALL_NAMES:
        operands += [w[name], m[name], v[name]]
        out_shape += [jax.ShapeDtypeStruct(w[name].shape, F32)] * 4
    out_shape.append(jax.ShapeDtypeStruct((8, LANES), F32))
    outs = pl.pallas_call(body, name="adamw_small", out_shape=out_shape)(
        mlp_all, mix_all, scale_all, nmix_all, *operands)
    return {name: outs[4 * i:4 * i + 4] for i, name in enumerate(SMALL_NAMES)}, outs[28]


def kernel(x, norm_mix, w_in, pool_w, pool_b, pool_scale, attn_sinks, p_pool, p_attn, w_out, norm_mlp, w_up, w_down, norm_final, loss_target, m_norm_mix, m_w_in, m_pool_w, m_pool_b, m_pool_scale, m_attn_sinks, m_p_pool, m_p_attn, m_w_out, m_norm_mlp, m_w_up, m_w_down, m_norm_final, v_norm_mix, v_w_in, v_pool_w, v_pool_b, v_pool_scale, v_attn_sinks, v_p_pool, v_p_attn, v_w_out, v_norm_mlp, v_w_up, v_w_down, v_norm_final):
    xs = x[0]
    tgt = loss_target[0]
    s_len = xs.shape[0]

    w_in_bf, p_pool_bf, p_attn_bf, w_out_bf, w_up_bf, w_down_bf = [
        t[0].astype(BF16) for t in (w_in, p_pool, p_attn, w_out, w_up, w_down)]
    (w_in_g,) = _all_gather_weights("all_gather_w_in", [w_in_bf])
    ag_proj = _exchange_start("ag_proj_start", [p_pool_bf, p_attn_bf, w_out_bf], (False,) * 3, w_in_g)
    ag_mlp = _exchange_start("ag_mlp_start", [w_up_bf, w_down_bf], (False,) * 2, ag_proj["token"])

    pool_w_bf = pool_w[0].astype(BF16)
    pool_b_row = pool_b[0].reshape(1, POOL_WIDTH)
    bias_t, sink_row = _attn_constants(attn_sinks[0])

    u, zp, q, kv, zg, w_in_f = _fwd_in(ag_mlp["token"], xs, norm_mix, w_in_g)
    pm, o = _mixers_fwd(zp, q, kv, pool_w_bf, pool_b_row, pool_scale, bias_t, sink_row)
    (p_pool_g, p_attn_g, w_out_g), _ = _exchange_wait("ag_proj_wait", ag_proj, pm)
    p_pool_f = p_pool_g.transpose(1, 0, 2).reshape(POOL_WIDTH, D_MODEL)
    p_attn_f = p_attn_g.transpose(1, 0, 2).reshape(ATTN_WIDTH, D_MODEL)
    w_out_f = w_out_g.reshape(D_MODEL, D_MODEL)
    h1, mixed = _mix_out(xs, pm, o, zg, p_pool_f, p_attn_f, w_out_f)
    (w_up_g, w_down_g), _ = _exchange_wait("ag_mlp_wait", ag_mlp, h1)
    w_down_f = w_down_g.reshape(D_FF, D_MODEL)
    dh1, a, dapre, u2, dh2, small_mlp = _mlp_loss(
        h1, tgt, norm_mlp, norm_final.reshape(1, D_MODEL), w_up_g, w_down_f)
    gw_down = _tn_matmul(a, dh2, square_a=True)
    gw_up = _tn_matmul(u2, dapre, col_blocks=N_DEV)
    ex_mlp = _exchange_start(
        "ex_mlp_start", [gw_up, gw_down.reshape(N_DEV, D_FF // N_DEV, D_MODEL)], (True, True), small_mlp)
    dyp, dya, dzg, dpm, do = _mix_bwd(ex_mlp["token"], dh1, pm, o, zg, p_pool_f, p_attn_f, w_out_f)
    gw_out = _tn_matmul(mixed, dh1)
    gp_pool = _tn_matmul(pm, dyp, col_blocks=N_DEV)
    gp_attn = _tn_matmul(o, dya, col_blocks=N_DEV)
    ex_proj = _exchange_start(
        "ex_proj_start", [gp_pool, gp_attn, gw_out.reshape(N_DEV, D_MODEL // N_DEV, D_MODEL)], (True,) * 3, small_mlp)
    dzp, dq, dkv, small_mix, g_pool_scale = _mixers_bwd(
        ex_proj["token"], zp, q, kv, dpm, do, pool_w_bf, pool_b_row, pool_scale, bias_t, sink_row)
    gw_in = _tn_w_in(u, dzp, dq, dkv, dzg)
    ex_in = _exchange_start(
        "ex_in_start", [gw_in, small_mlp, small_mix, g_pool_scale], (True, False, False, False), dq)
    dx, g_norm_mix = _in_bwd(ex_in["token"], dzp, dq, dkv, dzg, w_in_f, xs, dh1, norm_mix)

    big_w = dict(w_in=w_in, p_pool=p_pool, p_attn=p_attn, w_out=w_out, w_up=w_up, w_down=w_down)
    big_m = dict(w_in=m_w_in, p_pool=m_p_pool, p_attn=m_p_attn, w_out=m_w_out, w_up=m_w_up, w_down=m_w_down)
    big_v = dict(w_in=v_w_in, p_pool=v_p_pool, p_attn=v_p_attn, w_out=v_w_out, w_up=v_w_up, w_down=v_w_down)
    res = {}

    def update(names, recvs, sents):
        for name, parts, sent in zip(names, recvs, sents):
            outs = _adamw(parts, big_w[name][0], big_m[name][0], big_v[name][0], sent)
            res[name] = [t[None] for t in outs]

    update(["w_up", "w_down"], *_exchange_wait("ex_mlp_wait", ex_mlp, dx))
    update(["p_pool", "p_attn", "w_out"], *_exchange_wait("ex_proj_wait", ex_proj, res["w_down"][0]))
    (norm_mix_all,) = _all_gather_weights("all_gather_norm_mix", [g_norm_mix], res["w_out"][0])
    (r_in, mlp_all, mix_all, scale_all), (s_in, _, _, _) = _exchange_wait("ex_in_wait", ex_in, norm_mix_all)
    update(["w_in"], [r_in], [s_in])

    natural = dict(norm_mix=(1, D_MODEL), pool_w=(MIX_POOL_B, LANES), pool_b=(4, LANES), pool_scale=(1, POOL_WIDTH),
                   attn_sinks=(1, LANES), norm_mlp=(1, D_MODEL), norm_final=(1, D_MODEL))

    def as_2d(t, name):
        if name == "attn_sinks":
            return jnp.pad(t, ((0, 0), (0, LANES - N_HEADS)))
        return t.reshape(natural[name])

    small_w = dict(norm_mix=norm_mix, pool_w=pool_w, pool_b=pool_b, pool_scale=pool_scale, attn_sinks=attn_sinks,
                   norm_mlp=norm_mlp, norm_final=norm_final)
    small_m = dict(norm_mix=m_norm_mix, pool_w=m_pool_w, pool_b=m_pool_b, pool_scale=m_pool_scale,
                   attn_sinks=m_attn_sinks, norm_mlp=m_norm_mlp, norm_final=m_norm_final)
    small_v = dict(norm_mix=v_norm_mix, pool_w=v_pool_w, pool_b=v_pool_b, pool_scale=v_pool_scale,
                   attn_sinks=v_attn_sinks, norm_mlp=v_norm_mlp, norm_final=v_norm_final)
    small_res, loss_all = _adamw_small(
        mlp_all, mix_all, scale_all, norm_mix_all,
        *[{k: as_2d(t, k) for k, t in d.items()} for d in (small_w, small_m, small_v)])
    loss = loss_all[0, 0]
    for name in SMALL_NAMES:
        shape = small_w[name].shape
        res[name] = [(t[:, :N_HEADS] if name == "attn_sinks" else t).reshape(shape) for t in small_res[name]]

    order = ["norm_mix", "w_in", "pool_w", "pool_b", "pool_scale", "attn_sinks", "p_pool", "p_attn", "w_out",
             "norm_mlp", "w_up", "w_down", "norm_final"]
    out = [loss, dx[None]]
    for kind in range(4):
        out += [res[name][kind] for name in order]
    return tuple(out)
```

```python
import functools
import math

import numpy as np
import jax
import jax.numpy as jnp
from jax import lax
from jax.experimental import pallas as pl
from jax.experimental.pallas import tpu as pltpu

F32 = jnp.float32
BF16 = jnp.bfloat16

D_MODEL = 1024
POOL_WIDTH = 512
ATTN_WIDTH = 512
KV_WIDTH = 128
HEAD_DIM = 64
N_HEADS = 8
N_KV_HEADS = 2
GROUP = 4
BLOCK = 128
POOL_WINDOWS = (2, 4, 8, 16)
POOL_GROUP_DIM = 128
POOL_HALO = 16
D_FF = 4096
FF_CHUNK = 1024
IN_WIDTH = 3328
RMS_EPS = 1e-5
NEG_INF = -1e30
ATTN_SCALE = 1.0 / math.sqrt(HEAD_DIM)
N_DEV = 8

ADAM_LR = 0.001
ADAM_B1 = 0.9
ADAM_B2 = 0.999
ADAM_EPS = 1e-08
ADAM_WD = 0.01
ADAM_STEP = 10

LANES = 128
VMEM_LIMIT_BYTES = 56 * 1024 * 1024
MESH = pl.DeviceIdType.MESH


def _params(n_grid_axes=1):
    return pltpu.CompilerParams(
        dimension_semantics=("arbitrary",) * n_grid_axes, vmem_limit_bytes=VMEM_LIMIT_BYTES)


def _dot(a, b):
    return jnp.dot(a, b, preferred_element_type=F32)


def _dot_nt(a, b):
    return lax.dot_general(a, b, (((1,), (1,)), ((), ())), preferred_element_type=F32)


def _dot_tn(a, b):
    return lax.dot_general(a, b, (((0,), (0,)), ((), ())), preferred_element_type=F32)


ANY = pl.BlockSpec(memory_space=pl.ANY)


def _rows(tm, n):
    return pl.BlockSpec((tm, n), lambda i: (i, 0))


def _whole(shape):
    zeros = (0,) * len(shape)
    return pl.BlockSpec(shape, lambda i: zeros)


def _rms_fwd(h, g):
    r = lax.rsqrt(jnp.mean(h * h, axis=-1, keepdims=True) + RMS_EPS)
    xh = h * r
    return r, xh, xh * g


def _rms_bwd(dy, xh, r, g):
    dxh = dy * g
    dh = r * (dxh - xh * jnp.mean(dxh * xh, axis=-1, keepdims=True))
    return dh, jnp.sum(dy * xh, axis=0, keepdims=True)


def _fwd_in(after, x, g_mix, w_in_blocks):
    s_len = x.shape[0]
    tm = min(512, s_len)
    width = IN_WIDTH // N_DEV

    def body(after_ref, x_ref, g_ref, wb_ref, u_ref, zp_ref, q_ref, kv_ref, zg_ref, w_ref):
        @pl.when(pl.program_id(0) == 0)
        def _():
            for j in range(N_DEV):
                w_ref[:, j * width:(j + 1) * width] = wb_ref[j]

        _, _, u = _rms_fwd(x_ref[...], g_ref[...])
        u = u.astype(BF16)
        u_ref[...] = u
        zp_ref[...] = _dot(u, w_ref[:, 0:512]).astype(BF16)
        q_ref[...] = _dot(u, w_ref[:, 512:1024]).astype(BF16)
        kv_ref[...] = _dot(u, w_ref[:, 1024:1280]).astype(BF16)
        zg_ref[...] = _dot(u, w_ref[:, 1280:3328]).astype(BF16)

    return pl.pallas_call(
        body, name="fwd_in", grid=(s_len // tm,),
        in_specs=[ANY, _rows(tm, D_MODEL), _whole((1, D_MODEL)),
                  pl.BlockSpec((N_DEV, D_MODEL, width), lambda i: (0, 0, 0), pipeline_mode=pl.Buffered(1))],
        out_specs=[_rows(tm, D_MODEL), _rows(tm, 512), _rows(tm, 512), _rows(tm, 256), _rows(tm, 2048),
                   _whole((D_MODEL, IN_WIDTH))],
        out_shape=[jax.ShapeDtypeStruct((s_len, n), BF16) for n in (D_MODEL, 512, 512, 256, 2048)]
        + [jax.ShapeDtypeStruct((D_MODEL, IN_WIDTH), BF16)],
        compiler_params=_params(),
    )(after, x, g_mix, w_in_blocks)


def _attn_constants(sinks):
    qi = np.arange(BLOCK)[:, None]
    kj = np.arange(2 * BLOCK)[None, :]
    dist = BLOCK + qi - kj
    valid = (dist >= 0) & (dist < BLOCK)
    slopes = np.array([2.0 ** (-8.0 * (h + 1) / N_HEADS) for h in range(N_HEADS)], dtype=np.float32)
    bias = np.where(valid[None], -slopes[:, None, None] * dist.astype(np.float32)[None], np.float32(NEG_INF))
    bias = bias.astype(np.float32).reshape(N_KV_HEADS, GROUP * BLOCK, 2 * BLOCK).transpose(0, 2, 1)
    sink_row = jnp.repeat(sinks.astype(F32).reshape(N_KV_HEADS, GROUP), BLOCK, axis=1)[:, None, :]
    return jnp.asarray(np.ascontiguousarray(bias)), sink_row


def _left_half(shape):
    return lax.broadcasted_iota(jnp.int32, shape, 1) < HEAD_DIM


def _dup_halves(slab):
    swapped = pltpu.roll(slab, HEAD_DIM, 1)
    left = _left_half(slab.shape)
    return jnp.where(left, slab, swapped), jnp.where(left, swapped, slab)


def _fill_kv_slabs(kvh_ref, kv_ref, ka_ref, vd_ref):
    for rows, src in ((slice(0, BLOCK), kvh_ref), (slice(BLOCK, None), kv_ref)):
        kvf = src[...].astype(F32)
        for ref, lanes in ((ka_ref, slice(0, KV_WIDTH)), (vd_ref, slice(KV_WIDTH, 2 * KV_WIDTH))):
            d0, d1 = _dup_halves(kvf[:, lanes])
            ref[0, rows, :] = d0.astype(BF16)
            ref[1, rows, :] = d1.astype(BF16)


def _stack_pairs(a, h):
    pieces = []
    for j in range(2):
        pair = a[:, h * 256 + j * LANES:h * 256 + (j + 1) * LANES]
        left = _left_half(pair.shape)
        zero = jnp.zeros_like(pair)
        pieces += [jnp.where(left, pair, zero), jnp.where(left, zero, pair)]
    return jnp.concatenate(pieces, axis=0)


def _attn_probs_t(kk, q_st, bias_t, sink_row, first):
    s = _dot_nt(kk, q_st) * ATTN_SCALE + bias_t
    if first is not None:
        row = lax.broadcasted_iota(jnp.int32, s.shape, 0)
        s = jnp.where(jnp.logical_and(first, row < BLOCK), NEG_INF, s)
    m = jnp.maximum(jnp.max(s, axis=0, keepdims=True), sink_row)
    p = jnp.exp(s - m)
    es = jnp.exp(sink_row - m)
    inv = 1.0 / (jnp.sum(p, axis=0, keepdims=True) + es)
    return p * inv, es * inv


def _pool_d(ext, cur, g, row0):
    w = POOL_WINDOWS[g]
    acc = ext
    k = 1
    while k < w:
        acc = acc + pltpu.roll(acc, k, 0)
        k *= 2
    t = row0 + lax.broadcasted_iota(jnp.int32, cur.shape, 0)
    cnt = jnp.minimum(t + 1, w).astype(F32)
    return acc[POOL_HALO:, :] / cnt - cur


def _mixers_fwd(zp, q, kv, pool_w, pool_b, pool_scale, bias_t, sink_row):
    s_len = zp.shape[0]
    tq = min(512, s_len)
    nb = tq // BLOCK

    def body(zp_ref, zph_ref, q_ref, kv_ref, kvh_ref, pw_ref, pb_ref, ps_ref, bias_ref, sink_ref,
             pm_ref, o_ref, ka_ref, vd_ref):
        i = pl.program_id(0)
        cur = zp_ref[...].astype(F32)
        halo = zph_ref[...].astype(F32) * (i > 0).astype(F32)
        ext = jnp.concatenate([halo, cur], axis=0)
        for g in range(4):
            sl = slice(g * POOL_GROUP_DIM, (g + 1) * POOL_GROUP_DIM)
            d = _pool_d(ext[:, sl], cur[:, sl], g, i * tq)
            y = _dot(d.astype(BF16), pw_ref[g]) + pb_ref[:, sl]
            pm_ref[:, sl] = (y * ps_ref[:, sl]).astype(BF16)
        _fill_kv_slabs(kvh_ref, kv_ref, ka_ref, vd_ref)
        for b in range(nb):
            rq = slice(b * BLOCK, (b + 1) * BLOCK)
            rk = slice(b * BLOCK, (b + 2) * BLOCK)
            qb = q_ref[rq, :]
            for h in range(N_KV_HEADS):
                pn, _ = _attn_probs_t(ka_ref[h, rk, :], _stack_pairs(qb, h), bias_ref[h], sink_ref[h],
                                      (i == 0) if b == 0 else None)
                pn = pn.astype(BF16)
                vd = vd_ref[h, rk, :]
                left = _left_half(vd.shape)
                zero = jnp.zeros_like(vd)
                va, vb = jnp.where(left, vd, zero), jnp.where(left, zero, vd)
                for j in range(2):
                    o_pair = (_dot_tn(pn[:, (2 * j) * BLOCK:(2 * j + 1) * BLOCK], va)
                              + _dot_tn(pn[:, (2 * j + 1) * BLOCK:(2 * j + 2) * BLOCK], vb))
                    o_ref[rq, h * 256 + j * LANES:h * 256 + (j + 1) * LANES] = o_pair.astype(BF16)

    halo_pool = pl.BlockSpec((POOL_HALO, 512), lambda i: (jnp.maximum(i * (tq // POOL_HALO) - 1, 0), 0))
    halo_kv = pl.BlockSpec((BLOCK, 256), lambda i: (jnp.maximum(i * nb - 1, 0), 0))
    return pl.pallas_call(
        body, name="mixers_fwd", grid=(s_len // tq,),
        in_specs=[_rows(tq, 512), halo_pool, _rows(tq, 512), _rows(tq, 256), halo_kv,
                  _whole((4, 128, 128)), _whole((1, 512)), _whole((1, 512)),
                  _whole((N_KV_HEADS, 2 * BLOCK, GROUP * BLOCK)), _whole((N_KV_HEADS, 1, GROUP * BLOCK))],
        out_specs=[_rows(tq, 512), _rows(tq, 512)],
        out_shape=[jax.ShapeDtypeStruct((s_len, 512), BF16)] * 2,
        scratch_shapes=[pltpu.VMEM((N_KV_HEADS, tq + BLOCK, LANES), BF16)] * 2,
        compiler_params=_params(),
    )(zp, zp, q, kv, kv, pool_w, pool_b, pool_scale, bias_t, sink_row)


def _gated_mix(pm, o, zg, pp_ref, pa_ref):
    yp = _dot(pm, pp_ref[...])
    ya = _dot(o, pa_ref[...])
    gp = jax.nn.sigmoid(zg[:, :D_MODEL].astype(F32))
    ga = jax.nn.sigmoid(zg[:, D_MODEL:].astype(F32))
    return yp, ya, gp, ga


def _core(x, pm, o, zg, tgt, g_mlp, g_fin, p_pool, p_attn, w_out, w_up_blocks, w_down):
    s_len = x.shape[0]
    tm = min(256, s_len)
    n_chunks = D_FF // FF_CHUNK
    up_block = D_FF // N_DEV
    per_chunk = FF_CHUNK // up_block

    def body(x_ref, pm_ref, o_ref, zg_ref, tgt_ref, gm_ref, gf_ref, pp_ref, pa_ref, wo_ref, wu_ref, wd_ref,
             mixed_ref, dh1_ref, a_ref, dap_ref, u2_ref, dh2_ref, small_ref,
             dyp_ref, dya_ref, dzg_ref, dpm_ref, do_ref):
        i = pl.program_id(0)

        @pl.when(i == 0)
        def _():
            small_ref[...] = jnp.zeros_like(small_ref)

        yp, ya, gp, ga = _gated_mix(pm_ref[...], o_ref[...], zg_ref[...], pp_ref, pa_ref)
        mixed = (gp * yp + ga * ya).astype(BF16)
        mixed_ref[...] = mixed
        h1 = x_ref[...] + _dot(mixed, wo_ref[...])
        r2, xh2, u2 = _rms_fwd(h1, gm_ref[...])
        u2 = u2.astype(BF16)
        u2_ref[...] = u2
        acc = jnp.zeros((tm, D_MODEL), F32)
        for c in range(n_chunks):
            cs = slice(c * FF_CHUNK, (c + 1) * FF_CHUNK)
            a = jnp.concatenate([_dot(u2, wu_ref[per_chunk * c + j]) for j in range(per_chunk)], axis=1)
            a = jnp.maximum(a, 0.0)
            a_ref[:, cs] = a.astype(BF16)
            acc = acc + _dot((a * a).astype(BF16), wd_ref[cs, :])
        h2 = h1 + acc
        r3, xh3, y = _rms_fwd(h2, gf_ref[...])
        diff = y - tgt_ref[...]
        small_ref[2:3, :] += 0.5 * jnp.sum(jnp.mean(diff * diff, axis=-1, keepdims=True))
        dy = diff * (1.0 / D_MODEL)
        dh2, dgf = _rms_bwd(dy, xh3, r3, gf_ref[...])
        small_ref[1:2, :] += dgf
        dh2_bf = dh2.astype(BF16)
        dh2_ref[...] = dh2_bf
        du2 = jnp.zeros((tm, D_MODEL), F32)
        for c in range(n_chunks):
            cs = slice(c * FF_CHUNK, (c + 1) * FF_CHUNK)
            ds = _dot_nt(dh2_bf, wd_ref[cs, :])
            dap = (ds * (2.0 * a_ref[:, cs].astype(F32))).astype(BF16)
            dap_ref[:, cs] = dap
            for j in range(per_chunk):
                du2 = du2 + _dot_nt(dap[:, j * up_block:(j + 1) * up_block], wu_ref[per_chunk * c + j])
        dh1n, dgm = _rms_bwd(du2, xh2, r2, gm_ref[...])
        small_ref[0:1, :] += dgm
        dh1 = dh2 + dh1n
        dh1_ref[...] = dh1
        dm = _dot_nt(dh1.astype(BF16), wo_ref[...])
        dyp = (dm * gp).astype(BF16)
        dya = (dm * ga).astype(BF16)
        dyp_ref[...] = dyp
        dya_ref[...] = dya
        dzg_ref[:, :D_MODEL] = (dm * yp * (gp * (1.0 - gp))).astype(BF16)
        dzg_ref[:, D_MODEL:] = (dm * ya * (ga * (1.0 - ga))).astype(BF16)
        dpm_ref[...] = _dot_nt(dyp, pp_ref[...]).astype(BF16)
        do_ref[...] = _dot_nt(dya, pa_ref[...]).astype(BF16)

    def fixed(shape):
        return pl.BlockSpec(shape, lambda i: (0,) * len(shape), pipeline_mode=pl.Buffered(1))

    widths_dtypes = ((D_MODEL, BF16), (D_MODEL, F32), (D_FF, BF16), (D_FF, BF16), (D_MODEL, BF16), (D_MODEL, BF16))
    back = ((D_MODEL, BF16), (D_MODEL, BF16), (2048, BF16), (512, BF16), (512, BF16))
    return pl.pallas_call(
        body, name="core", grid=(s_len // tm,),
        in_specs=[_rows(tm, D_MODEL), _rows(tm, 512), _rows(tm, 512), _rows(tm, 2048), _rows(tm, D_MODEL),
                  _whole((1, D_MODEL)), _whole((1, D_MODEL)),
                  fixed((512, D_MODEL)), fixed((512, D_MODEL)), fixed((D_MODEL, D_MODEL)),
                  fixed((N_DEV, D_MODEL, up_block)), fixed((D_FF, D_MODEL))],
        out_specs=[_rows(tm, n) for n, _ in widths_dtypes] + [_whole((8, D_MODEL))] + [_rows(tm, n) for n, _ in back],
        out_shape=[jax.ShapeDtypeStruct((s_len, n), d) for n, d in widths_dtypes]
        + [jax.ShapeDtypeStruct((8, D_MODEL), F32)] + [jax.ShapeDtypeStruct((s_len, n), d) for n, d in back],
        compiler_params=_params(),
    )(x, pm, o, zg, tgt, g_mlp, g_fin, p_pool, p_attn, w_out, w_up_blocks, w_down)


def _tn_matmul(a, b, square_a=False, col_blocks=None, after=None):
    s_len, ka = a.shape
    nb = b.shape[1]
    tt = min(2048, s_len)
    tk = min(1024, ka)
    tn = min(1024, nb)
    n_t = s_len // tt
    if col_blocks is None:
        out_spec = pl.BlockSpec((tk, tn), lambda k, j, t: (k, j))
        out_shape = jax.ShapeDtypeStruct((ka, nb), BF16)
    else:
        width = nb // col_blocks
        per_tile = tn // width
        out_spec = pl.BlockSpec((per_tile, tk, width), lambda k, j, t: (j, k, 0))
        out_shape = jax.ShapeDtypeStruct((col_blocks, ka, width), BF16)

    extra = [] if after is None else [after]

    def body(a_ref, b_ref, *rest):
        o_ref, acc_ref = rest[len(extra):]
        t = pl.program_id(2)

        @pl.when(t == 0)
        def _():
            acc_ref[...] = jnp.zeros_like(acc_ref)

        av = a_ref[...]
        if square_a:
            av = av * av
        acc_ref[...] += _dot_tn(av.astype(BF16), b_ref[...].astype(BF16))

        @pl.when(t == n_t - 1)
        def _():
            if col_blocks is None:
                o_ref[...] = acc_ref[...].astype(o_ref.dtype)
            else:
                for blk in range(per_tile):
                    o_ref[blk] = acc_ref[:, blk * width:(blk + 1) * width].astype(o_ref.dtype)

    return pl.pallas_call(
        body, name="tn_matmul", grid=(ka // tk, nb // tn, n_t),
        in_specs=[pl.BlockSpec((tt, tk), lambda k, j, t: (t, k)), pl.BlockSpec((tt, tn), lambda k, j, t: (t, j))]
        + [ANY] * len(extra),
        out_specs=out_spec, out_shape=out_shape,
        scratch_shapes=[pltpu.VMEM((tk, tn), F32)],
        compiler_params=_params(3),
    )(a, b, *extra)


def _tn_w_in(u, dzp, dq, dkv, dzg):
    s_len = u.shape[0]
    tt = min(1024, s_len)
    n_t = s_len // tt
    width = IN_WIDTH // N_DEV
    pieces = ((0, 512), (512, 1024), (1024, 1280), (1280, IN_WIDTH))

    def body(u_ref, dzp_ref, dq_ref, dkv_ref, dzg_ref, o_ref, acc_ref):
        t = pl.program_id(0)

        @pl.when(t == 0)
        def _():
            acc_ref[...] = jnp.zeros_like(acc_ref)

        uv = u_ref[...]
        for (c0, c1), ref in zip(pieces, (dzp_ref, dq_ref, dkv_ref, dzg_ref)):
            acc_ref[:, c0:c1] += _dot_tn(uv, ref[...])

        @pl.when(t == n_t - 1)
        def _():
            for j in range(N_DEV):
                o_ref[j] = acc_ref[:, j * width:(j + 1) * width].astype(BF16)

    return pl.pallas_call(
        body, name="tn_w_in", grid=(n_t,),
        in_specs=[_rows(tt, D_MODEL)] + [_rows(tt, c1 - c0) for c0, c1 in pieces],
        out_specs=_whole((N_DEV, D_MODEL, width)),
        out_shape=jax.ShapeDtypeStruct((N_DEV, D_MODEL, width), BF16),
        scratch_shapes=[pltpu.VMEM((D_MODEL, IN_WIDTH), F32)],
        compiler_params=_params(),
    )(u, dzp, dq, dkv, dzg)


MIX_POOL_B = 4 * POOL_GROUP_DIM
MIX_SINKS = MIX_POOL_B + 8
MIX_ROWS = MIX_SINKS + 8


def _mixers_bwd(after, zp, q, kv, dpm, do, pool_w, pool_b, pool_scale, bias_t, sink_row):
    s_len = zp.shape[0]
    tq = min(512, s_len)
    nb = tq // BLOCK
    n_steps = s_len // tq

    def body(after_ref, zp_ref, zph_ref, q_ref, kv_ref, kvh_ref, dpm_ref, dpmh_ref, do_ref, pw_ref, pb_ref, ps_ref,
             bias_ref, sink_ref, dzp_ref, dq_ref, dkv_ref, small_ref, dps_ref,
             ka_ref, vd_ref, dsk_acc, dkv_acc):
        i = pl.program_id(0)

        @pl.when(i == 0)
        def _():
            dkv_acc[...] = jnp.zeros_like(dkv_acc)
            small_ref[...] = jnp.zeros_like(small_ref)
            dps_ref[...] = jnp.zeros_like(dps_ref)
            dsk_acc[...] = jnp.zeros_like(dsk_acc)

        cur = zp_ref[...].astype(F32)
        halo = zph_ref[...].astype(F32) * (i > 0).astype(F32)
        ext = jnp.concatenate([halo, cur], axis=0)
        dpm_next = dpmh_ref[...].astype(F32) * (i < n_steps - 1).astype(F32)
        dpm_ext = jnp.concatenate([dpm_ref[...].astype(F32), dpm_next], axis=0)
        n_ext = tq + POOL_HALO
        for g in range(4):
            sl = slice(g * POOL_GROUP_DIM, (g + 1) * POOL_GROUP_DIM)
            w = POOL_WINDOWS[g]
            d = _pool_d(ext[:, sl], cur[:, sl], g, i * tq).astype(BF16)
            y_lin = _dot(d, pw_ref[g]) + pb_ref[:, sl]
            dps_ref[:, sl] += jnp.sum(dpm_ext[:tq, sl] * y_lin, axis=0, keepdims=True)
            dyl_ext = dpm_ext[:, sl] * ps_ref[:, sl]
            small_ref[MIX_POOL_B + g:MIX_POOL_B + g + 1, :] += jnp.sum(dyl_ext[:tq], axis=0, keepdims=True)
            dyl_bf = dyl_ext.astype(BF16)
            small_ref[g * POOL_GROUP_DIM:(g + 1) * POOL_GROUP_DIM, :] += _dot_tn(d, dyl_bf[:tq])
            dd = _dot_nt(dyl_bf, pw_ref[g])
            t = i * tq + lax.broadcasted_iota(jnp.int32, dd.shape, 0)
            e = dd / jnp.minimum(t + 1, w).astype(F32)
            acc = e
            k = 1
            while k < w:
                acc = acc + pltpu.roll(acc, n_ext - k, 0)
                k *= 2
            dzp_ref[:, sl] = (acc[:tq] - dd[:tq]).astype(BF16)

        _fill_kv_slabs(kvh_ref, kv_ref, ka_ref, vd_ref)

        def fold(dup):
            return dup + pltpu.roll(dup, HEAD_DIM, 1)

        for b in range(nb):
            rq = slice(b * BLOCK, (b + 1) * BLOCK)
            rk = slice(b * BLOCK, (b + 2) * BLOCK)
            qb = q_ref[rq, :]
            dob = do_ref[rq, :]
            dk_dup, dv_dup = [], []
            for h in range(N_KV_HEADS):
                kk = ka_ref[h, rk, :]
                q_st = _stack_pairs(qb, h)
                do_st = _stack_pairs(dob, h)
                pn, psink = _attn_probs_t(kk, q_st, bias_ref[h], sink_ref[h], (i == 0) if b == 0 else None)
                dp = _dot_nt(vd_ref[h, rk, :], do_st)
                delta = jnp.sum(pn * dp, axis=0, keepdims=True)
                dsk_acc[h] += -psink * delta
                ds = ((pn * (dp - delta)) * ATTN_SCALE).astype(BF16)
                dq_st = _dot_tn(ds, kk)
                for j in range(2):
                    left = _left_half((BLOCK, LANES))
                    dq_pair = jnp.where(left, dq_st[(2 * j) * BLOCK:(2 * j + 1) * BLOCK],
                                        dq_st[(2 * j + 1) * BLOCK:(2 * j + 2) * BLOCK])
                    dq_ref[rq, h * 256 + j * LANES:h * 256 + (j + 1) * LANES] = dq_pair.astype(BF16)
                dk_dup.append(fold(_dot(ds, q_st)))
                dv_dup.append(fold(_dot(pn.astype(BF16), do_st)))
            left = _left_half((2 * BLOCK, LANES))
            dkv_blk = jnp.concatenate([jnp.where(left, dk_dup[0], dk_dup[1]),
                                       jnp.where(left, dv_dup[0], dv_dup[1])], axis=1)
            g0 = pl.multiple_of(i * tq + b * BLOCK, BLOCK)
            dkv_acc[pl.ds(g0, 2 * BLOCK), :] += dkv_blk

        @pl.when(i == n_steps - 1)
        def _():
            dkv_ref[...] = dkv_acc[BLOCK:, :].astype(BF16)
            lane = lax.broadcasted_iota(jnp.int32, (1, LANES), 1)
            row = jnp.zeros((1, LANES), F32)
            for h in range(N_KV_HEADS):
                for g in range(GROUP):
                    tot = jnp.sum(dsk_acc[h, :, g * BLOCK:(g + 1) * BLOCK], axis=1, keepdims=True)
                    row = jnp.where(lane == GROUP * h + g, tot, row)
            small_ref[MIX_SINKS:MIX_SINKS + 1, :] = row

    blocks_per_tile = tq // POOL_HALO
    last_halo = s_len // POOL_HALO - 1
    halo_prev = pl.BlockSpec((POOL_HALO, 512), lambda i: (jnp.maximum(i * blocks_per_tile - 1, 0), 0))
    halo_next = pl.BlockSpec((POOL_HALO, 512), lambda i: (jnp.minimum((i + 1) * blocks_per_tile, last_halo), 0))
    halo_kv = pl.BlockSpec((BLOCK, 256), lambda i: (jnp.maximum(i * nb - 1, 0), 0))
    return pl.pallas_call(
        body, name="mixers_bwd", grid=(n_steps,),
        in_specs=[ANY, _rows(tq, 512), halo_prev, _rows(tq, 512), _rows(tq, 256), halo_kv,
                  _rows(tq, 512), halo_next, _rows(tq, 512),
                  _whole((4, 128, 128)), _whole((1, 512)), _whole((1, 512)),
                  _whole((N_KV_HEADS, 2 * BLOCK, GROUP * BLOCK)), _whole((N_KV_HEADS, 1, GROUP * BLOCK))],
        out_specs=[_rows(tq, 512), _rows(tq, 512), _whole((s_len, 256)),
                   _whole((MIX_ROWS, LANES)), _whole((1, 512))],
        out_shape=[jax.ShapeDtypeStruct((s_len, 512), BF16), jax.ShapeDtypeStruct((s_len, 512), BF16),
                   jax.ShapeDtypeStruct((s_len, 256), BF16), jax.ShapeDtypeStruct((MIX_ROWS, LANES), F32),
                   jax.ShapeDtypeStruct((1, 512), F32)],
        scratch_shapes=[pltpu.VMEM((N_KV_HEADS, tq + BLOCK, LANES), BF16)] * 2
        + [pltpu.VMEM((N_KV_HEADS, 1, GROUP * BLOCK), F32), pltpu.VMEM((s_len + BLOCK, 256), F32)],
        compiler_params=_params(),
    )(after, zp, zp, q, kv, kv, dpm, dpm, do, pool_w, pool_b, pool_scale, bias_t, sink_row)


def _in_bwd(after, dzp, dq, dkv, dzg, w_in, x, dh1, g_mix):
    s_len = x.shape[0]
    tm = min(512, s_len)

    def body(after_ref, dzp_ref, dq_ref, dkv_ref, dzg_ref, w_ref, x_ref, dh1_ref, g_ref, dx_ref, dg_ref):
        i = pl.program_id(0)

        @pl.when(i == 0)
        def _():
            dg_ref[...] = jnp.zeros_like(dg_ref)

        du = _dot_nt(dzp_ref[...], w_ref[:, 0:512])
        du = du + _dot_nt(dq_ref[...], w_ref[:, 512:1024])
        du = du + _dot_nt(dkv_ref[...], w_ref[:, 1024:1280])
        du = du + _dot_nt(dzg_ref[...], w_ref[:, 1280:3328])
        r, xh, _ = _rms_fwd(x_ref[...], g_ref[...])
        dxn, dg = _rms_bwd(du, xh, r, g_ref[...])
        dg_ref[...] += dg
        dx_ref[...] = dh1_ref[...] + dxn

    return pl.pallas_call(
        body, name="in_bwd", grid=(s_len // tm,),
        in_specs=[ANY, _rows(tm, 512), _rows(tm, 512), _rows(tm, 256), _rows(tm, 2048), _whole((D_MODEL, IN_WIDTH)),
                  _rows(tm, D_MODEL), _rows(tm, D_MODEL), _whole((1, D_MODEL))],
        out_specs=[_rows(tm, D_MODEL), _whole((1, D_MODEL))],
        out_shape=[jax.ShapeDtypeStruct((s_len, D_MODEL), F32), jax.ShapeDtypeStruct((1, D_MODEL), F32)],
        compiler_params=_params(),
    )(after, dzp, dq, dkv, dzg, w_in, x, dh1, g_mix)


def _all_gather_weights(name, shards, after=None):
    n = len(shards)
    extra = [] if after is None else [after]
    n_extra = len(extra)

    def body(*refs):
        ins, outs = refs[:n], refs[n + n_extra:2 * n + n_extra]
        send_sems, recv_sems, local_sems = refs[2 * n + n_extra:]
        x, y, c = lax.axis_index("x"), lax.axis_index("y"), lax.axis_index("c")
        me, sibling = (x, y, c), (x, y, 1 - c)
        chips = [(1 - x, y), (x, 1 - y), (1 - x, 1 - y)]

        def slot(a, px, py, pc):
            return outs[a].at[4 * px + 2 * py + pc]

        def copy(a, k, block, to, src=None):
            return pltpu.make_async_remote_copy(
                src_ref=slot(a, *block) if src is None else src, dst_ref=slot(a, *block),
                send_sem=send_sems.at[a, k], recv_sem=recv_sems.at[a, k], device_id=to, device_id_type=MESH)

        mine = [pltpu.make_async_copy(ins[a], slot(a, *me), local_sems.at[a]) for a in range(n)]
        for cp in mine:
            cp.start()
        first = []
        for a in range(n):
            first.append(copy(a, 0, me, sibling, src=ins[a]))
            first += [copy(a, 1 + j, me, (*chip, c), src=ins[a]) for j, chip in enumerate(chips)]
        for cp in first:
            cp.start()
        passed = []
        for a in range(n):
            for j, chip in enumerate(chips):
                copy(a, 1 + j, (*chip, c), me).wait_recv()
                cp = copy(a, 4 + j, (*chip, c), sibling)
                cp.start()
                passed.append(cp)
        for a in range(n):
            copy(a, 0, sibling, me).wait_recv()
            for j, chip in enumerate(chips):
                copy(a, 4 + j, (*chip, 1 - c), me).wait_recv()
        for cp in first + passed:
            cp.wait_send()
        for cp in mine:
            cp.wait()

    return pl.pallas_call(
        body, name=name,
        in_specs=[ANY] * (n + n_extra), out_specs=[ANY] * n,
        out_shape=[jax.ShapeDtypeStruct((N_DEV,) + s.shape, s.dtype) for s in shards],
        scratch_shapes=[pltpu.SemaphoreType.DMA((n, 7)), pltpu.SemaphoreType.DMA((n, 7)), pltpu.SemaphoreType.DMA((n,))],
    )(*shards, *extra)


HBM_SPEC = pl.BlockSpec(memory_space=pltpu.HBM)
SEM_SPEC = pl.BlockSpec(memory_space=pltpu.SEMAPHORE)
DATAFLOW = pltpu.SideEffectType.DATAFLOW_SIDE_EFFECTING
N_PEERS = N_DEV - 1


def _peer_copies(srcs, lands, scatter, send_sems, recv_sems):
    x, y, c = lax.axis_index("x"), lax.axis_index("y"), lax.axis_index("c")
    me_idx = 4 * x + 2 * y + c
    copies = []
    for k in range(1, N_DEV):
        px = 1 - x if (k >> 2) & 1 else x
        py = 1 - y if (k >> 1) & 1 else y
        pc = 1 - c if k & 1 else c
        p_idx = 4 * px + 2 * py + pc
        for a in range(len(srcs)):
            src = srcs[a].at[p_idx] if scatter[a] else srcs[a]
            dst = lands[a].at[k] if scatter[a] else lands[a].at[me_idx]
            copies.append(pltpu.make_async_remote_copy(
                src_ref=src, dst_ref=dst, send_sem=send_sems.at[a * N_PEERS + k - 1],
                recv_sem=recv_sems.at[a * N_PEERS + k - 1],
                device_id=(px, py, pc), device_id_type=MESH))
    return copies


def _exchange_start(name, srcs, scatter, after):
    n = len(srcs)
    lands = [lax.empty(s.shape if sc else (N_DEV,) + s.shape, s.dtype) for s, sc in zip(srcs, scatter)]

    def body(*refs):
        src_refs, land_refs = refs[:n], refs[n:2 * n]
        send_sems, recv_sems = refs[2 * n + 1], refs[2 * n + 2]
        token = refs[4 * n + 3]
        for cp in _peer_copies(src_refs, land_refs, scatter, send_sems, recv_sems):
            cp.start()
        token[...] = jnp.zeros_like(token)

    hbm = lambda t: pltpu.HBM(t.shape, t.dtype)
    outs = pl.pallas_call(
        body, name=name,
        out_shape=[pltpu.SemaphoreType.DMA((n * N_PEERS,)), pltpu.SemaphoreType.DMA((n * N_PEERS,))]
        + [hbm(t) for t in srcs] + [hbm(t) for t in lands] + [jax.ShapeDtypeStruct((8, LANES), F32)],
        in_specs=[HBM_SPEC] * (2 * n) + [ANY],
        out_specs=[SEM_SPEC, SEM_SPEC] + [HBM_SPEC] * (2 * n) + [pl.BlockSpec(memory_space=pltpu.VMEM)],
        input_output_aliases={i: 2 + i for i in range(2 * n)},
        compiler_params=pltpu.CompilerParams(has_side_effects=DATAFLOW),
    )(*[pltpu.with_memory_space_constraint(t, pltpu.HBM) for t in list(srcs) + lands], after)
    return dict(n=n, scatter=scatter, send_sems=outs[0], recv_sems=outs[1], srcs=outs[2:2 + n],
                lands=outs[2 + n:2 + 2 * n], token=outs[2 + 2 * n])


def _exchange_wait(name, handle, after):
    n, scatter = handle["n"], handle["scatter"]

    def body(*refs):
        src_refs, land_refs = refs[:n], refs[n:2 * n]
        send_sems, recv_sems = refs[2 * n], refs[2 * n + 1]
        for cp in _peer_copies(src_refs, land_refs, scatter, send_sems, recv_sems):
            cp.wait_send()
            cp.wait_recv()

    both = list(handle["srcs"]) + list(handle["lands"])
    outs = pl.pallas_call(
        body, name=name,
        out_shape=[pltpu.HBM(t.shape, t.dtype) for t in both],
        in_specs=[HBM_SPEC] * (2 * n) + [SEM_SPEC, SEM_SPEC, ANY],
        out_specs=[HBM_SPEC] * (2 * n),
        input_output_aliases={i: i for i in range(2 * n)},
        compiler_params=pltpu.CompilerParams(has_side_effects=DATAFLOW),
    )(*both, handle["send_sems"], handle["recv_sems"], after)
    me_idx = _my_index()
    lands = [land if sc else lax.dynamic_update_index_in_dim(land, src, me_idx, 0)
             for land, src, sc in zip(outs[n:], outs[:n], scatter)]
    return lands, outs[:n]


def _my_index():
    return 4 * lax.axis_index("x") + 2 * lax.axis_index("y") + lax.axis_index("c")


def _adamw(parts, w, m, v, sent=None):
    r, c = w.shape
    tr = 256 if r % 256 == 0 else r
    own = sent is not None

    def body(*refs):
        if own:
            _, p_ref, own_ref, w_ref, m_ref, v_ref, g_ref, d_ref, nm_ref, nv_ref = refs
            g = own_ref[...].astype(F32)
        else:
            p_ref, w_ref, m_ref, v_ref, g_ref, d_ref, nm_ref, nv_ref = refs
            g = p_ref[0].astype(F32)
        for k in range(1, N_DEV):
            g = g + p_ref[k].astype(F32)
        m_new = ADAM_B1 * m_ref[...] + (1.0 - ADAM_B1) * g
        v_new = ADAM_B2 * v_ref[...] + (1.0 - ADAM_B2) * (g * g)
        m_hat = m_new / (1.0 - ADAM_B1 ** ADAM_STEP)
        v_hat = v_new / (1.0 - ADAM_B2 ** ADAM_STEP)
        g_ref[...] = g
        d_ref[...] = -ADAM_LR * (m_hat / (jnp.sqrt(v_hat) + ADAM_EPS) + ADAM_WD * w_ref[...])
        nm_ref[...] = m_new
        nv_ref[...] = v_new

    out_shape = [jax.ShapeDtypeStruct((r, c), F32)] * 4
    if not own:
        return pl.pallas_call(
            body, name="adamw", grid=(r // tr,),
            in_specs=[pl.BlockSpec((N_DEV, tr, c), lambda i: (0, i, 0))] + [_rows(tr, c)] * 3,
            out_specs=[_rows(tr, c)] * 4, out_shape=out_shape, compiler_params=_params(),
        )(parts, w, m, v)
    rows = pl.BlockSpec((tr, c), lambda i, me: (i, 0))
    return pl.pallas_call(
        body, name="adamw_own", out_shape=out_shape, compiler_params=_params(),
        grid_spec=pltpu.PrefetchScalarGridSpec(
            num_scalar_prefetch=1, grid=(r // tr,),
            in_specs=[pl.BlockSpec((N_DEV, tr, c), lambda i, me: (0, i, 0)),
                      pl.BlockSpec((None, tr, c), lambda i, me: (me[0], i, 0))] + [rows] * 3,
            out_specs=[rows] * 4),
    )(_my_index().reshape(1).astype(jnp.int32), parts, sent, w, m, v)


def _adam_step(g, w, m, v):
    m_new = ADAM_B1 * m + (1.0 - ADAM_B1) * g
    v_new = ADAM_B2 * v + (1.0 - ADAM_B2) * (g * g)
    m_hat = m_new / (1.0 - ADAM_B1 ** ADAM_STEP)
    v_hat = v_new / (1.0 - ADAM_B2 ** ADAM_STEP)
    return -ADAM_LR * (m_hat / (jnp.sqrt(v_hat) + ADAM_EPS) + ADAM_WD * w), m_new, v_new


SMALL_NAMES = ("norm_mix", "pool_w", "pool_b", "pool_scale", "attn_sinks", "norm_mlp", "norm_final")


def _adamw_small(mlp_all, mix_all, scale_all, nmix_all, w, m, v):
    def body(mlp_ref, mix_ref, scale_ref, nmix_ref, *refs):
        ins, outs = refs[:21], refs[21:]

        def total(ref, rows, lanes=slice(None)):
            g = ref[0, rows, lanes]
            for k in range(1, N_DEV):
                g = g + ref[k, rows, lanes]
            return g

        grads = dict(
            norm_mix=total(nmix_ref, slice(0, 1)), pool_w=total(mix_ref, slice(0, MIX_POOL_B)),
            pool_b=total(mix_ref, slice(MIX_POOL_B, MIX_POOL_B + 4)), pool_scale=total(scale_ref, slice(0, 1)),
            attn_sinks=total(mix_ref, slice(MIX_SINKS, MIX_SINKS + 1)),
            norm_mlp=total(mlp_ref, slice(0, 1)), norm_final=total(mlp_ref, slice(1, 2)))
        for i, name in enumerate(SMALL_NAMES):
            g = grads[name]
            d, m_new, v_new = _adam_step(g, ins[3 * i][...], ins[3 * i + 1][...], ins[3 * i + 2][...])
            for ref, val in zip(outs[4 * i:4 * i + 4], (g, d, m_new, v_new)):
                ref[...] = val
        outs[28][...] = jnp.broadcast_to(total(mlp_ref, slice(2, 3), slice(0, LANES)), (8, LANES))

    operands, out_shape = [], []
    for name in SMALL_NAMES:
        operands += [w[name], m[name], v[name]]
        out_shape += [jax.ShapeDtypeStruct(w[name].shape, F32)] * 4
    out_shape.append(jax.ShapeDtypeStruct((8, LANES), F32))
    outs = pl.pallas_call(body, name="adamw_small", out_shape=out_shape)(
        mlp_all, mix_all, scale_all, nmix_all, *operands)
    return {name: outs[4 * i:4 * i + 4] for i, name in enumerate(SMALL_NAMES)}, outs[28]


def kernel(x, norm_mix, w_in, pool_w, pool_b, pool_scale, attn_sinks, p_pool, p_attn, w_out, norm_mlp, w_up, w_down, norm_final, loss_target, m_norm_mix, m_w_in, m_pool_w, m_pool_b, m_pool_scale, m_attn_sinks, m_p_pool, m_p_attn, m_w_out, m_norm_mlp, m_w_up, m_w_down, m_norm_final, v_norm_mix, v_w_in, v_pool_w, v_pool_b, v_pool_scale, v_attn_sinks, v_p_pool, v_p_attn, v_w_out, v_norm_mlp, v_w_up, v_w_down, v_norm_final):
    xs = x[0]
    tgt = loss_target[0]
    s_len = xs.shape[0]

    w_in_bf, p_pool_bf, p_attn_bf, w_out_bf, w_up_bf, w_down_bf = [
        t[0].astype(BF16) for t in (w_in, p_pool, p_attn, w_out, w_up, w_down)]
    (w_in_g,) = _all_gather_weights("all_gather_w_in", [w_in_bf])
    ag_proj = _exchange_start("ag_proj_start", [p_pool_bf, p_attn_bf, w_out_bf], (False,) * 3, w_in_g)
    ag_mlp = _exchange_start("ag_mlp_start", [w_up_bf, w_down_bf], (False,) * 2, ag_proj["token"])

    pool_w_bf = pool_w[0].astype(BF16)
    pool_b_row = pool_b[0].reshape(1, POOL_WIDTH)
    bias_t, sink_row = _attn_constants(attn_sinks[0])

    u, zp, q, kv, zg, w_in_f = _fwd_in(ag_mlp["token"], xs, norm_mix, w_in_g)
    pm, o = _mixers_fwd(zp, q, kv, pool_w_bf, pool_b_row, pool_scale, bias_t, sink_row)
    (p_pool_g, p_attn_g, w_out_g), _ = _exchange_wait("ag_proj_wait", ag_proj, pm)
    p_pool_f = p_pool_g.transpose(1, 0, 2).reshape(POOL_WIDTH, D_MODEL)
    p_attn_f = p_attn_g.transpose(1, 0, 2).reshape(ATTN_WIDTH, D_MODEL)
    w_out_f = w_out_g.reshape(D_MODEL, D_MODEL)
    (w_up_g, w_down_g), _ = _exchange_wait("ag_mlp_wait", ag_mlp, o)
    w_down_f = w_down_g.reshape(D_FF, D_MODEL)
    mixed, dh1, a, dapre, u2, dh2, small_mlp, dyp, dya, dzg, dpm, do = _core(
        xs, pm, o, zg, tgt, norm_mlp, norm_final.reshape(1, D_MODEL), p_pool_f, p_attn_f, w_out_f, w_up_g, w_down_f)
    gw_down = _tn_matmul(a, dh2, square_a=True)
    gw_up = _tn_matmul(u2, dapre, col_blocks=N_DEV)
    ex_mlp = _exchange_start(
        "ex_mlp_start", [gw_up, gw_down.reshape(N_DEV, D_FF // N_DEV, D_MODEL)], (True, True), small_mlp)
    gw_out = _tn_matmul(mixed, dh1, after=ex_mlp["token"])
    gp_pool = _tn_matmul(pm, dyp, col_blocks=N_DEV)
    gp_attn = _tn_matmul(o, dya, col_blocks=N_DEV)
    ex_proj = _exchange_start(
        "ex_proj_start", [gp_pool, gp_attn, gw_out.reshape(N_DEV, D_MODEL // N_DEV, D_MODEL)], (True,) * 3, small_mlp)
    dzp, dq, dkv, small_mix, g_pool_scale = _mixers_bwd(
        ex_proj["token"], zp, q, kv, dpm, do, pool_w_bf, pool_b_row, pool_scale, bias_t, sink_row)
    gw_in = _tn_w_in(u, dzp, dq, dkv, dzg)
    ex_in = _exchange_start(
        "ex_in_start", [gw_in, small_mlp, small_mix, g_pool_scale], (True, False, False, False), dq)
    dx, g_norm_mix = _in_bwd(ex_in["token"], dzp, dq, dkv, dzg, w_in_f, xs, dh1, norm_mix)

    big_w = dict(w_in=w_in, p_pool=p_pool, p_attn=p_attn, w_out=w_out, w_up=w_up, w_down=w_down)
    big_m = dict(w_in=m_w_in, p_pool=m_p_pool, p_attn=m_p_attn, w_out=m_w_out, w_up=m_w_up, w_down=m_w_down)
    big_v = dict(w_in=v_w_in, p_pool=v_p_pool, p_attn=v_p_attn, w_out=v_w_out, w_up=v_w_up, w_down=v_w_down)
    res = {}

    def update(names, recvs, sents):
        for name, parts, sent in zip(names, recvs, sents):
            outs = _adamw(parts, big_w[name][0], big_m[name][0], big_v[name][0], sent)
            res[name] = [t[None] for t in outs]

    update(["w_up", "w_down"], *_exchange_wait("ex_mlp_wait", ex_mlp, dx))
    update(["p_pool", "p_attn", "w_out"], *_exchange_wait("ex_proj_wait", ex_proj, res["w_down"][0]))
    (norm_mix_all,) = _all_gather_weights("all_gather_norm_mix", [g_norm_mix], res["w_out"][0])
    (r_in, mlp_all, mix_all, scale_all), (s_in, _, _, _) = _exchange_wait("ex_in_wait", ex_in, norm_mix_all)
    update(["w_in"], [r_in], [s_in])

    natural = dict(norm_mix=(1, D_MODEL), pool_w=(MIX_POOL_B, LANES), pool_b=(4, LANES), pool_scale=(1, POOL_WIDTH),
                   attn_sinks=(1, LANES), norm_mlp=(1, D_MODEL), norm_final=(1, D_MODEL))

    def as_2d(t, name):
        if name == "attn_sinks":
            return jnp.pad(t, ((0, 0), (0, LANES - N_HEADS)))
        return t.reshape(natural[name])

    small_w = dict(norm_mix=norm_mix, pool_w=pool_w, pool_b=pool_b, pool_scale=pool_scale, attn_sinks=attn_sinks,
                   norm_mlp=norm_mlp, norm_final=norm_final)
    small_m = dict(norm_mix=m_norm_mix, pool_w=m_pool_w, pool_b=m_pool_b, pool_scale=m_pool_scale,
                   attn_sinks=m_attn_sinks, norm_mlp=m_norm_mlp, norm_final=m_norm_final)
    small_v = dict(norm_mix=v_norm_mix, pool_w=v_pool_w, pool_b=v_pool_b, pool_scale=v_pool_scale,
                   attn_sinks=v_attn_sinks, norm_mlp=v_norm_mlp, norm_final=v_norm_final)
    small_res, loss_all = _adamw_small(
        mlp_all, mix_all, scale_all, norm_mix_all,
        *[{k: as_2d(t, k) for k, t in d.items()} for d in (small_w, small_m, small_v)])
    loss = loss_all[0, 0]
    for name in SMALL_NAMES:
        shape = small_w[name].shape
        res[name] = [(t[:, :N_HEADS] if name == "attn_sinks" else t).reshape(shape) for t in small_res[name]]

    order = ["norm_mix", "w_in", "pool_w", "pool_b", "pool_scale", "attn_sinks", "p_pool", "p_attn", "w_out",
             "norm_mlp", "w_up", "w_down", "norm_final"]
    out = [loss, dx[None]]
    for kind in range(4):
        out += [res[name][kind] for name in order]
    return tuple(out)
```

```python
import functools
import math

import numpy as np
import jax
import jax.numpy as jnp
from jax import lax
from jax.experimental import pallas as pl
from jax.experimental.pallas import tpu as pltpu

F32 = jnp.float32
BF16 = jnp.bfloat16

D_MODEL = 1024
POOL_WIDTH = 512
ATTN_WIDTH = 512
KV_WIDTH = 128
HEAD_DIM = 64
N_HEADS = 8
N_KV_HEADS = 2
GROUP = 4
BLOCK = 128
POOL_WINDOWS = (2, 4, 8, 16)
POOL_GROUP_DIM = 128
POOL_HALO = 16
D_FF = 4096
FF_CHUNK = 1024
IN_WIDTH = 3328
RMS_EPS = 1e-5
NEG_INF = -1e30
ATTN_SCALE = 1.0 / math.sqrt(HEAD_DIM)
N_DEV = 8

ADAM_LR = 0.001
ADAM_B1 = 0.9
ADAM_B2 = 0.999
ADAM_EPS = 1e-08
ADAM_WD = 0.01
ADAM_STEP = 10

LANES = 128
VMEM_LIMIT_BYTES = 56 * 1024 * 1024
MESH = pl.DeviceIdType.MESH


def _params(n_grid_axes=1):
    return pltpu.CompilerParams(
        dimension_semantics=("arbitrary",) * n_grid_axes, vmem_limit_bytes=VMEM_LIMIT_BYTES)


def _dot(a, b):
    return jnp.dot(a, b, preferred_element_type=F32)


def _dot_nt(a, b):
    return lax.dot_general(a, b, (((1,), (1,)), ((), ())), preferred_element_type=F32)


def _dot_tn(a, b):
    return lax.dot_general(a, b, (((0,), (0,)), ((), ())), preferred_element_type=F32)


ANY = pl.BlockSpec(memory_space=pl.ANY)


def _rows(tm, n):
    return pl.BlockSpec((tm, n), lambda i: (i, 0))


def _whole(shape):
    zeros = (0,) * len(shape)
    return pl.BlockSpec(shape, lambda i: zeros)


def _rms_fwd(h, g):
    r = lax.rsqrt(jnp.mean(h * h, axis=-1, keepdims=True) + RMS_EPS)
    xh = h * r
    return r, xh, xh * g


def _rms_bwd(dy, xh, r, g):
    dxh = dy * g
    dh = r * (dxh - xh * jnp.mean(dxh * xh, axis=-1, keepdims=True))
    return dh, jnp.sum(dy * xh, axis=0, keepdims=True)


def _fwd_in(after, x, g_mix, w_in_blocks):
    s_len = x.shape[0]
    tm = min(512, s_len)
    width = IN_WIDTH // N_DEV

    def body(after_ref, x_ref, g_ref, wb_ref, u_ref, zp_ref, q_ref, kv_ref, zg_ref, w_ref):
        @pl.when(pl.program_id(0) == 0)
        def _():
            for j in range(N_DEV):
                w_ref[:, j * width:(j + 1) * width] = wb_ref[j]

        _, _, u = _rms_fwd(x_ref[...], g_ref[...])
        u = u.astype(BF16)
        u_ref[...] = u
        zp_ref[...] = _dot(u, w_ref[:, 0:512]).astype(BF16)
        q_ref[...] = _dot(u, w_ref[:, 512:1024]).astype(BF16)
        kv_ref[...] = _dot(u, w_ref[:, 1024:1280]).astype(BF16)
        zg_ref[...] = _dot(u, w_ref[:, 1280:3328]).astype(BF16)

    return pl.pallas_call(
        body, name="fwd_in", grid=(s_len // tm,),
        in_specs=[ANY, _rows(tm, D_MODEL), _whole((1, D_MODEL)),
                  pl.BlockSpec((N_DEV, D_MODEL, width), lambda i: (0, 0, 0), pipeline_mode=pl.Buffered(1))],
        out_specs=[_rows(tm, D_MODEL), _rows(tm, 512), _rows(tm, 512), _rows(tm, 256), _rows(tm, 2048),
                   _whole((D_MODEL, IN_WIDTH))],
        out_shape=[jax.ShapeDtypeStruct((s_len, n), BF16) for n in (D_MODEL, 512, 512, 256, 2048)]
        + [jax.ShapeDtypeStruct((D_MODEL, IN_WIDTH), BF16)],
        compiler_params=_params(),
    )(after, x, g_mix, w_in_blocks)


def _attn_constants(sinks):
    qi = np.arange(BLOCK)[:, None]
    kj = np.arange(2 * BLOCK)[None, :]
    dist = BLOCK + qi - kj
    valid = (dist >= 0) & (dist < BLOCK)
    slopes = np.array([2.0 ** (-8.0 * (h + 1) / N_HEADS) for h in range(N_HEADS)], dtype=np.float32)
    bias = np.where(valid[None], -slopes[:, None, None] * dist.astype(np.float32)[None], np.float32(NEG_INF))
    bias = bias.astype(np.float32).reshape(N_KV_HEADS, GROUP * BLOCK, 2 * BLOCK).transpose(0, 2, 1)
    sink_row = jnp.repeat(sinks.astype(F32).reshape(N_KV_HEADS, GROUP), BLOCK, axis=1)[:, None, :]
    return jnp.asarray(np.ascontiguousarray(bias)), sink_row


def _left_half(shape):
    return lax.broadcasted_iota(jnp.int32, shape, 1) < HEAD_DIM


def _dup_halves(slab):
    swapped = pltpu.roll(slab, HEAD_DIM, 1)
    left = _left_half(slab.shape)
    return jnp.where(left, slab, swapped), jnp.where(left, swapped, slab)


def _fill_kv_slabs(kvh_ref, kv_ref, ka_ref, vd_ref):
    for rows, src in ((slice(0, BLOCK), kvh_ref), (slice(BLOCK, None), kv_ref)):
        kvf = src[...].astype(F32)
        for ref, lanes in ((ka_ref, slice(0, KV_WIDTH)), (vd_ref, slice(KV_WIDTH, 2 * KV_WIDTH))):
            d0, d1 = _dup_halves(kvf[:, lanes])
            ref[0, rows, :] = d0.astype(BF16)
            ref[1, rows, :] = d1.astype(BF16)


def _stack_pairs(a, h):
    pieces = []
    for j in range(2):
        pair = a[:, h * 256 + j * LANES:h * 256 + (j + 1) * LANES]
        left = _left_half(pair.shape)
        zero = jnp.zeros_like(pair)
        pieces += [jnp.where(left, pair, zero), jnp.where(left, zero, pair)]
    return jnp.concatenate(pieces, axis=0)


def _attn_probs_t(kk, q_st, bias_t, sink_row, first):
    s = _dot_nt(kk, q_st) * ATTN_SCALE + bias_t
    if first is not None:
        row = lax.broadcasted_iota(jnp.int32, s.shape, 0)
        s = jnp.where(jnp.logical_and(first, row < BLOCK), NEG_INF, s)
    m = jnp.maximum(jnp.max(s, axis=0, keepdims=True), sink_row)
    p = jnp.exp(s - m)
    es = jnp.exp(sink_row - m)
    inv = 1.0 / (jnp.sum(p, axis=0, keepdims=True) + es)
    return p * inv, es * inv


def _pool_d(ext, cur, g, row0):
    w = POOL_WINDOWS[g]
    acc = ext
    k = 1
    while k < w:
        acc = acc + pltpu.roll(acc, k, 0)
        k *= 2
    t = row0 + lax.broadcasted_iota(jnp.int32, cur.shape, 0)
    cnt = jnp.minimum(t + 1, w).astype(F32)
    return acc[POOL_HALO:, :] / cnt - cur


def _mixers_fwd(zp, q, kv, pool_w, pool_b, pool_scale, bias_t, sink_row):
    s_len = zp.shape[0]
    tq = min(512, s_len)
    nb = tq // BLOCK

    def body(zp_ref, zph_ref, q_ref, kv_ref, kvh_ref, pw_ref, pb_ref, ps_ref, bias_ref, sink_ref,
             pm_ref, o_ref, ka_ref, vd_ref):
        i = pl.program_id(0)
        cur = zp_ref[...].astype(F32)
        halo = zph_ref[...].astype(F32) * (i > 0).astype(F32)
        ext = jnp.concatenate([halo, cur], axis=0)
        for g in range(4):
            sl = slice(g * POOL_GROUP_DIM, (g + 1) * POOL_GROUP_DIM)
            d = _pool_d(ext[:, sl], cur[:, sl], g, i * tq)
            y = _dot(d.astype(BF16), pw_ref[g]) + pb_ref[:, sl]
            pm_ref[:, sl] = (y * ps_ref[:, sl]).astype(BF16)
        _fill_kv_slabs(kvh_ref, kv_ref, ka_ref, vd_ref)
        for b in range(nb):
            rq = slice(b * BLOCK, (b + 1) * BLOCK)
            rk = slice(b * BLOCK, (b + 2) * BLOCK)
            qb = q_ref[rq, :]
            for h in range(N_KV_HEADS):
                pn, _ = _attn_probs_t(ka_ref[h, rk, :], _stack_pairs(qb, h), bias_ref[h], sink_ref[h],
                                      (i == 0) if b == 0 else None)
                pn = pn.astype(BF16)
                vd = vd_ref[h, rk, :]
                left = _left_half(vd.shape)
                zero = jnp.zeros_like(vd)
                va, vb = jnp.where(left, vd, zero), jnp.where(left, zero, vd)
                for j in range(2):
                    o_pair = (_dot_tn(pn[:, (2 * j) * BLOCK:(2 * j + 1) * BLOCK], va)
                              + _dot_tn(pn[:, (2 * j + 1) * BLOCK:(2 * j + 2) * BLOCK], vb))
                    o_ref[rq, h * 256 + j * LANES:h * 256 + (j + 1) * LANES] = o_pair.astype(BF16)

    halo_pool = pl.BlockSpec((POOL_HALO, 512), lambda i: (jnp.maximum(i * (tq // POOL_HALO) - 1, 0), 0))
    halo_kv = pl.BlockSpec((BLOCK, 256), lambda i: (jnp.maximum(i * nb - 1, 0), 0))
    return pl.pallas_call(
        body, name="mixers_fwd", grid=(s_len // tq,),
        in_specs=[_rows(tq, 512), halo_pool, _rows(tq, 512), _rows(tq, 256), halo_kv,
                  _whole((4, 128, 128)), _whole((1, 512)), _whole((1, 512)),
                  _whole((N_KV_HEADS, 2 * BLOCK, GROUP * BLOCK)), _whole((N_KV_HEADS, 1, GROUP * BLOCK))],
        out_specs=[_rows(tq, 512), _rows(tq, 512)],
        out_shape=[jax.ShapeDtypeStruct((s_len, 512), BF16)] * 2,
        scratch_shapes=[pltpu.VMEM((N_KV_HEADS, tq + BLOCK, LANES), BF16)] * 2,
        compiler_params=_params(),
    )(zp, zp, q, kv, kv, pool_w, pool_b, pool_scale, bias_t, sink_row)


def _gated_mix(pm, o, zg, pp_ref, pa_ref):
    yp = _dot(pm, pp_ref[...])
    ya = _dot(o, pa_ref[...])
    gp = jax.nn.sigmoid(zg[:, :D_MODEL].astype(F32))
    ga = jax.nn.sigmoid(zg[:, D_MODEL:].astype(F32))
    return yp, ya, gp, ga


def _core(x, pm, o, zg, tgt, g_mlp, g_fin, p_pool, p_attn, w_out, w_up_blocks, w_down):
    s_len = x.shape[0]
    tm = min(256, s_len)
    n_chunks = D_FF // FF_CHUNK
    up_block = D_FF // N_DEV
    per_chunk = FF_CHUNK // up_block

    def body(x_ref, pm_ref, o_ref, zg_ref, tgt_ref, gm_ref, gf_ref, pp_ref, pa_ref, wo_ref, wu_ref, wd_ref,
             mixed_ref, dh1_ref, a_ref, dap_ref, u2_ref, dh2_ref, small_ref,
             dyp_ref, dya_ref, dzg_ref, dpm_ref, do_ref):
        i = pl.program_id(0)

        @pl.when(i == 0)
        def _():
            small_ref[...] = jnp.zeros_like(small_ref)

        yp, ya, gp, ga = _gated_mix(pm_ref[...], o_ref[...], zg_ref[...], pp_ref, pa_ref)
        mixed = (gp * yp + ga * ya).astype(BF16)
        mixed_ref[...] = mixed
        h1 = x_ref[...] + _dot(mixed, wo_ref[...])
        r2, xh2, u2 = _rms_fwd(h1, gm_ref[...])
        u2 = u2.astype(BF16)
        u2_ref[...] = u2
        acc = jnp.zeros((tm, D_MODEL), F32)
        for c in range(n_chunks):
            cs = slice(c * FF_CHUNK, (c + 1) * FF_CHUNK)
            a = jnp.concatenate([_dot(u2, wu_ref[per_chunk * c + j]) for j in range(per_chunk)], axis=1)
            a = jnp.maximum(a, 0.0)
            a_ref[:, cs] = a.astype(BF16)
            acc = acc + _dot((a * a).astype(BF16), wd_ref[cs, :])
        h2 = h1 + acc
        r3, xh3, y = _rms_fwd(h2, gf_ref[...])
        diff = y - tgt_ref[...]
        small_ref[2:3, :] += 0.5 * jnp.sum(jnp.mean(diff * diff, axis=-1, keepdims=True))
        dy = diff * (1.0 / D_MODEL)
        dh2, dgf = _rms_bwd(dy, xh3, r3, gf_ref[...])
        small_ref[1:2, :] += dgf
        dh2_bf = dh2.astype(BF16)
        dh2_ref[...] = dh2_bf
        du2 = jnp.zeros((tm, D_MODEL), F32)
        for c in range(n_chunks):
            cs = slice(c * FF_CHUNK, (c + 1) * FF_CHUNK)
            ds = _dot_nt(dh2_bf, wd_ref[cs, :])
            dap = (ds * (2.0 * a_ref[:, cs].astype(F32))).astype(BF16)
            dap_ref[:, cs] = dap
            for j in range(per_chunk):
                du2 = du2 + _dot_nt(dap[:, j * up_block:(j + 1) * up_block], wu_ref[per_chunk * c + j])
        dh1n, dgm = _rms_bwd(du2, xh2, r2, gm_ref[...])
        small_ref[0:1, :] += dgm
        dh1 = dh2 + dh1n
        dh1_ref[...] = dh1
        dm = _dot_nt(dh1.astype(BF16), wo_ref[...])
        dyp = (dm * gp).astype(BF16)
        dya = (dm * ga).astype(BF16)
        dyp_ref[...] = dyp
        dya_ref[...] = dya
        dzg_ref[:, :D_MODEL] = (dm * yp * (gp * (1.0 - gp))).astype(BF16)
        dzg_ref[:, D_MODEL:] = (dm * ya * (ga * (1.0 - ga))).astype(BF16)
        dpm_ref[...] = _dot_nt(dyp, pp_ref[...]).astype(BF16)
        do_ref[...] = _dot_nt(dya, pa_ref[...]).astype(BF16)

    def fixed(shape):
        return pl.BlockSpec(shape, lambda i: (0,) * len(shape), pipeline_mode=pl.Buffered(1))

    widths_dtypes = ((D_MODEL, BF16), (D_MODEL, F32), (D_FF, BF16), (D_FF, BF16), (D_MODEL, BF16), (D_MODEL, BF16))
    back = ((D_MODEL, BF16), (D_MODEL, BF16), (2048, BF16), (512, BF16), (512, BF16))
    return pl.pallas_call(
        body, name="core", grid=(s_len // tm,),
        in_specs=[_rows(tm, D_MODEL), _rows(tm, 512), _rows(tm, 512), _rows(tm, 2048), _rows(tm, D_MODEL),
                  _whole((1, D_MODEL)), _whole((1, D_MODEL)),
                  fixed((512, D_MODEL)), fixed((512, D_MODEL)), fixed((D_MODEL, D_MODEL)),
                  fixed((N_DEV, D_MODEL, up_block)), fixed((D_FF, D_MODEL))],
        out_specs=[_rows(tm, n) for n, _ in widths_dtypes] + [_whole((8, D_MODEL))] + [_rows(tm, n) for n, _ in back],
        out_shape=[jax.ShapeDtypeStruct((s_len, n), d) for n, d in widths_dtypes]
        + [jax.ShapeDtypeStruct((8, D_MODEL), F32)] + [jax.ShapeDtypeStruct((s_len, n), d) for n, d in back],
        compiler_params=_params(),
    )(x, pm, o, zg, tgt, g_mlp, g_fin, p_pool, p_attn, w_out, w_up_blocks, w_down)


def _tn_matmul(a, b, square_a=False, col_blocks=None, after=None):
    s_len, ka = a.shape
    nb = b.shape[1]
    tt = min(2048, s_len)
    tk = min(1024, ka)
    tn = min(1024, nb)
    n_t = s_len // tt
    if col_blocks is None:
        out_spec = pl.BlockSpec((tk, tn), lambda k, j, t: (k, j))
        out_shape = jax.ShapeDtypeStruct((ka, nb), BF16)
    else:
        width = nb // col_blocks
        per_tile = tn // width
        out_spec = pl.BlockSpec((per_tile, tk, width), lambda k, j, t: (j, k, 0))
        out_shape = jax.ShapeDtypeStruct((col_blocks, ka, width), BF16)

    extra = [] if after is None else [after]

    def body(a_ref, b_ref, *rest):
        o_ref, acc_ref = rest[len(extra):]
        t = pl.program_id(2)

        @pl.when(t == 0)
        def _():
            acc_ref[...] = jnp.zeros_like(acc_ref)

        av = a_ref[...]
        if square_a:
            av = av * av
        acc_ref[...] += _dot_tn(av.astype(BF16), b_ref[...].astype(BF16))

        @pl.when(t == n_t - 1)
        def _():
            if col_blocks is None:
                o_ref[...] = acc_ref[...].astype(o_ref.dtype)
            else:
                for blk in range(per_tile):
                    o_ref[blk] = acc_ref[:, blk * width:(blk + 1) * width].astype(o_ref.dtype)

    return pl.pallas_call(
        body, name="tn_matmul", grid=(ka // tk, nb // tn, n_t),
        in_specs=[pl.BlockSpec((tt, tk), lambda k, j, t: (t, k)), pl.BlockSpec((tt, tn), lambda k, j, t: (t, j))]
        + [ANY] * len(extra),
        out_specs=out_spec, out_shape=out_shape,
        scratch_shapes=[pltpu.VMEM((tk, tn), F32)],
        compiler_params=_params(3),
    )(a, b, *extra)


def _tn_w_in(u, dzp, dq, dkv, dzg):
    s_len = u.shape[0]
    tt = min(1024, s_len)
    n_t = s_len // tt
    width = IN_WIDTH // N_DEV
    pieces = ((0, 512), (512, 1024), (1024, 1280), (1280, IN_WIDTH))

    def body(u_ref, dzp_ref, dq_ref, dkv_ref, dzg_ref, o_ref, acc_ref):
        t = pl.program_id(0)

        @pl.when(t == 0)
        def _():
            acc_ref[...] = jnp.zeros_like(acc_ref)

        uv = u_ref[...]
        for (c0, c1), ref in zip(pieces, (dzp_ref, dq_ref, dkv_ref, dzg_ref)):
            acc_ref[:, c0:c1] += _dot_tn(uv, ref[...])

        @pl.when(t == n_t - 1)
        def _():
            for j in range(N_DEV):
                o_ref[j] = acc_ref[:, j * width:(j + 1) * width].astype(BF16)

    return pl.pallas_call(
        body, name="tn_w_in", grid=(n_t,),
        in_specs=[_rows(tt, D_MODEL)] + [_rows(tt, c1 - c0) for c0, c1 in pieces],
        out_specs=_whole((N_DEV, D_MODEL, width)),
        out_shape=jax.ShapeDtypeStruct((N_DEV, D_MODEL, width), BF16),
        scratch_shapes=[pltpu.VMEM((D_MODEL, IN_WIDTH), F32)],
        compiler_params=_params(),
    )(u, dzp, dq, dkv, dzg)


MIX_POOL_B = 4 * POOL_GROUP_DIM
MIX_SINKS = MIX_POOL_B + 8
MIX_ROWS = MIX_SINKS + 8


def _mixers_bwd(after, zp, q, kv, dpm, do, pool_w, pool_b, pool_scale, bias_t, sink_row):
    s_len = zp.shape[0]
    tq = min(512, s_len)
    nb = tq // BLOCK
    n_steps = s_len // tq

    def body(after_ref, zp_ref, zph_ref, q_ref, kv_ref, kvh_ref, dpm_ref, dpmh_ref, do_ref, pw_ref, pb_ref, ps_ref,
             bias_ref, sink_ref, dzp_ref, dq_ref, dkv_ref, small_ref, dps_ref,
             ka_ref, vd_ref, dsk_acc, dkv_acc):
        i = pl.program_id(0)

        @pl.when(i == 0)
        def _():
            dkv_acc[...] = jnp.zeros_like(dkv_acc)
            small_ref[...] = jnp.zeros_like(small_ref)
            dps_ref[...] = jnp.zeros_like(dps_ref)
            dsk_acc[...] = jnp.zeros_like(dsk_acc)

        cur = zp_ref[...].astype(F32)
        halo = zph_ref[...].astype(F32) * (i > 0).astype(F32)
        ext = jnp.concatenate([halo, cur], axis=0)
        dpm_next = dpmh_ref[...].astype(F32) * (i < n_steps - 1).astype(F32)
        dpm_ext = jnp.concatenate([dpm_ref[...].astype(F32), dpm_next], axis=0)
        n_ext = tq + POOL_HALO
        for g in range(4):
            sl = slice(g * POOL_GROUP_DIM, (g + 1) * POOL_GROUP_DIM)
            w = POOL_WINDOWS[g]
            d = _pool_d(ext[:, sl], cur[:, sl], g, i * tq).astype(BF16)
            y_lin = _dot(d, pw_ref[g]) + pb_ref[:, sl]
            dps_ref[:, sl] += jnp.sum(dpm_ext[:tq, sl] * y_lin, axis=0, keepdims=True)
            dyl_ext = dpm_ext[:, sl] * ps_ref[:, sl]
            small_ref[MIX_POOL_B + g:MIX_POOL_B + g + 1, :] += jnp.sum(dyl_ext[:tq], axis=0, keepdims=True)
            dyl_bf = dyl_ext.astype(BF16)
            small_ref[g * POOL_GROUP_DIM:(g + 1) * POOL_GROUP_DIM, :] += _dot_tn(d, dyl_bf[:tq])
            dd = _dot_nt(dyl_bf, pw_ref[g])
            t = i * tq + lax.broadcasted_iota(jnp.int32, dd.shape, 0)
            e = dd / jnp.minimum(t + 1, w).astype(F32)
            acc = e
            k = 1
            while k < w:
                acc = acc + pltpu.roll(acc, n_ext - k, 0)
                k *= 2
            dzp_ref[:, sl] = (acc[:tq] - dd[:tq]).astype(BF16)

        _fill_kv_slabs(kvh_ref, kv_ref, ka_ref, vd_ref)

        def fold(dup):
            return dup + pltpu.roll(dup, HEAD_DIM, 1)

        for b in range(nb):
            rq = slice(b * BLOCK, (b + 1) * BLOCK)
            rk = slice(b * BLOCK, (b + 2) * BLOCK)
            qb = q_ref[rq, :]
            dob = do_ref[rq, :]
            dk_dup, dv_dup = [], []
            for h in range(N_KV_HEADS):
                kk = ka_ref[h, rk, :]
                q_st = _stack_pairs(qb, h)
                do_st = _stack_pairs(dob, h)
                pn, psink = _attn_probs_t(kk, q_st, bias_ref[h], sink_ref[h], (i == 0) if b == 0 else None)
                dp = _dot_nt(vd_ref[h, rk, :], do_st)
                delta = jnp.sum(pn * dp, axis=0, keepdims=True)
                dsk_acc[h] += -psink * delta
                ds = ((pn * (dp - delta)) * ATTN_SCALE).astype(BF16)
                dq_st = _dot_tn(ds, kk)
                for j in range(2):
                    left = _left_half((BLOCK, LANES))
                    dq_pair = jnp.where(left, dq_st[(2 * j) * BLOCK:(2 * j + 1) * BLOCK],
                                        dq_st[(2 * j + 1) * BLOCK:(2 * j + 2) * BLOCK])
                    dq_ref[rq, h * 256 + j * LANES:h * 256 + (j + 1) * LANES] = dq_pair.astype(BF16)
                dk_dup.append(fold(_dot(ds, q_st)))
                dv_dup.append(fold(_dot(pn.astype(BF16), do_st)))
            left = _left_half((2 * BLOCK, LANES))
            dkv_blk = jnp.concatenate([jnp.where(left, dk_dup[0], dk_dup[1]),
                                       jnp.where(left, dv_dup[0], dv_dup[1])], axis=1)
            g0 = pl.multiple_of(i * tq + b * BLOCK, BLOCK)
            dkv_acc[pl.ds(g0, 2 * BLOCK), :] += dkv_blk

        @pl.when(i == n_steps - 1)
        def _():
            dkv_ref[...] = dkv_acc[BLOCK:, :].astype(BF16)
            lane = lax.broadcasted_iota(jnp.int32, (1, LANES), 1)
            row = jnp.zeros((1, LANES), F32)
            for h in range(N_KV_HEADS):
                for g in range(GROUP):
                    tot = jnp.sum(dsk_acc[h, :, g * BLOCK:(g + 1) * BLOCK], axis=1, keepdims=True)
                    row = jnp.where(lane == GROUP * h + g, tot, row)
            small_ref[MIX_SINKS:MIX_SINKS + 1, :] = row

    blocks_per_tile = tq // POOL_HALO
    last_halo = s_len // POOL_HALO - 1
    halo_prev = pl.BlockSpec((POOL_HALO, 512), lambda i: (jnp.maximum(i * blocks_per_tile - 1, 0), 0))
    halo_next = pl.BlockSpec((POOL_HALO, 512), lambda i: (jnp.minimum((i + 1) * blocks_per_tile, last_halo), 0))
    halo_kv = pl.BlockSpec((BLOCK, 256), lambda i: (jnp.maximum(i * nb - 1, 0), 0))
    return pl.pallas_call(
        body, name="mixers_bwd", grid=(n_steps,),
        in_specs=[ANY, _rows(tq, 512), halo_prev, _rows(tq, 512), _rows(tq, 256), halo_kv,
                  _rows(tq, 512), halo_next, _rows(tq, 512),
                  _whole((4, 128, 128)), _whole((1, 512)), _whole((1, 512)),
                  _whole((N_KV_HEADS, 2 * BLOCK, GROUP * BLOCK)), _whole((N_KV_HEADS, 1, GROUP * BLOCK))],
        out_specs=[_rows(tq, 512), _rows(tq, 512), _whole((s_len, 256)),
                   _whole((MIX_ROWS, LANES)), _whole((1, 512))],
        out_shape=[jax.ShapeDtypeStruct((s_len, 512), BF16), jax.ShapeDtypeStruct((s_len, 512), BF16),
                   jax.ShapeDtypeStruct((s_len, 256), BF16), jax.ShapeDtypeStruct((MIX_ROWS, LANES), F32),
                   jax.ShapeDtypeStruct((1, 512), F32)],
        scratch_shapes=[pltpu.VMEM((N_KV_HEADS, tq + BLOCK, LANES), BF16)] * 2
        + [pltpu.VMEM((N_KV_HEADS, 1, GROUP * BLOCK), F32), pltpu.VMEM((s_len + BLOCK, 256), F32)],
        compiler_params=_params(),
    )(after, zp, zp, q, kv, kv, dpm, dpm, do, pool_w, pool_b, pool_scale, bias_t, sink_row)


def _in_bwd(after, dzp, dq, dkv, dzg, w_in, x, dh1, g_mix):
    s_len = x.shape[0]
    tm = min(512, s_len)

    def body(after_ref, dzp_ref, dq_ref, dkv_ref, dzg_ref, w_ref, x_ref, dh1_ref, g_ref, dx_ref, dg_ref):
        i = pl.program_id(0)

        @pl.when(i == 0)
        def _():
            dg_ref[...] = jnp.zeros_like(dg_ref)

        du = _dot_nt(dzp_ref[...], w_ref[:, 0:512])
        du = du + _dot_nt(dq_ref[...], w_ref[:, 512:1024])
        du = du + _dot_nt(dkv_ref[...], w_ref[:, 1024:1280])
        du = du + _dot_nt(dzg_ref[...], w_ref[:, 1280:3328])
        r, xh, _ = _rms_fwd(x_ref[...], g_ref[...])
        dxn, dg = _rms_bwd(du, xh, r, g_ref[...])
        dg_ref[...] += dg
        dx_ref[...] = dh1_ref[...] + dxn

    return pl.pallas_call(
        body, name="in_bwd", grid=(s_len // tm,),
        in_specs=[ANY, _rows(tm, 512), _rows(tm, 512), _rows(tm, 256), _rows(tm, 2048), _whole((D_MODEL, IN_WIDTH)),
                  _rows(tm, D_MODEL), _rows(tm, D_MODEL), _whole((1, D_MODEL))],
        out_specs=[_rows(tm, D_MODEL), _whole((1, D_MODEL))],
        out_shape=[jax.ShapeDtypeStruct((s_len, D_MODEL), F32), jax.ShapeDtypeStruct((1, D_MODEL), F32)],
        compiler_params=_params(),
    )(after, dzp, dq, dkv, dzg, w_in, x, dh1, g_mix)


def _all_gather_weights(name, shards, after=None):
    n = len(shards)
    extra = [] if after is None else [after]
    n_extra = len(extra)

    def body(*refs):
        ins, outs = refs[:n], refs[n + n_extra:2 * n + n_extra]
        send_sems, recv_sems, local_sems = refs[2 * n + n_extra:]
        x, y, c = lax.axis_index("x"), lax.axis_index("y"), lax.axis_index("c")
        me, sibling = (x, y, c), (x, y, 1 - c)
        chips = [(1 - x, y), (x, 1 - y), (1 - x, 1 - y)]

        def slot(a, px, py, pc):
            return outs[a].at[4 * px + 2 * py + pc]

        def copy(a, k, block, to, src=None):
            return pltpu.make_async_remote_copy(
                src_ref=slot(a, *block) if src is None else src, dst_ref=slot(a, *block),
                send_sem=send_sems.at[a, k], recv_sem=recv_sems.at[a, k], device_id=to, device_id_type=MESH)

        mine = [pltpu.make_async_copy(ins[a], slot(a, *me), local_sems.at[a]) for a in range(n)]
        for cp in mine:
            cp.start()
        first = []
        for a in range(n):
            first.append(copy(a, 0, me, sibling, src=ins[a]))
            first += [copy(a, 1 + j, me, (*chip, c), src=ins[a]) for j, chip in enumerate(chips)]
        for cp in first:
            cp.start()
        passed = []
        for a in range(n):
            for j, chip in enumerate(chips):
                copy(a, 1 + j, (*chip, c), me).wait_recv()
                cp = copy(a, 4 + j, (*chip, c), sibling)
                cp.start()
                passed.append(cp)
        for a in range(n):
            copy(a, 0, sibling, me).wait_recv()
            for j, chip in enumerate(chips):
                copy(a, 4 + j, (*chip, 1 - c), me).wait_recv()
        for cp in first + passed:
            cp.wait_send()
        for cp in mine:
            cp.wait()

    return pl.pallas_call(
        body, name=name,
        in_specs=[ANY] * (n + n_extra), out_specs=[ANY] * n,
        out_shape=[jax.ShapeDtypeStruct((N_DEV,) + s.shape, s.dtype) for s in shards],
        scratch_shapes=[pltpu.SemaphoreType.DMA((n, 7)), pltpu.SemaphoreType.DMA((n, 7)), pltpu.SemaphoreType.DMA((n,))],
    )(*shards, *extra)


HBM_SPEC = pl.BlockSpec(memory_space=pltpu.HBM)
SEM_SPEC = pl.BlockSpec(memory_space=pltpu.SEMAPHORE)
DATAFLOW = pltpu.SideEffectType.DATAFLOW_SIDE_EFFECTING
N_PEERS = N_DEV - 1


CHIP_PEERS = (1, 2, 4, 6)
RELAYED = (2, 4, 6)


def _peer_copies(srcs, lands, scatter, send_sems, recv_sems):
    x, y, c = lax.axis_index("x"), lax.axis_index("y"), lax.axis_index("c")
    me_idx = 4 * x + 2 * y + c
    copies = []
    for k in range(1, N_DEV):
        px = 1 - x if (k >> 2) & 1 else x
        py = 1 - y if (k >> 1) & 1 else y
        pc = 1 - c if k & 1 else c
        p_idx = 4 * px + 2 * py + pc
        for a in range(len(srcs)):
            if scatter[a] == "chip" and k not in CHIP_PEERS:
                continue
            src = srcs[a].at[p_idx] if scatter[a] is True else srcs[a]
            dst = lands[a].at[k] if scatter[a] is True else lands[a].at[me_idx]
            copies.append(pltpu.make_async_remote_copy(
                src_ref=src, dst_ref=dst, send_sem=send_sems.at[a * N_PEERS + k - 1],
                recv_sem=recv_sems.at[a * N_PEERS + k - 1],
                device_id=(px, py, pc), device_id_type=MESH))
    return copies


def _exchange_start(name, srcs, scatter, after):
    n = len(srcs)
    lands = [lax.empty(s.shape if sc is True else (N_DEV,) + s.shape, s.dtype) for s, sc in zip(srcs, scatter)]

    def body(*refs):
        src_refs, land_refs = refs[:n], refs[n:2 * n]
        send_sems, recv_sems = refs[2 * n + 1], refs[2 * n + 2]
        token = refs[4 * n + 3]
        for cp in _peer_copies(src_refs, land_refs, scatter, send_sems, recv_sems):
            cp.start()
        token[...] = jnp.zeros_like(token)

    hbm = lambda t: pltpu.HBM(t.shape, t.dtype)
    outs = pl.pallas_call(
        body, name=name,
        out_shape=[pltpu.SemaphoreType.DMA((n * N_PEERS,)), pltpu.SemaphoreType.DMA((n * N_PEERS,))]
        + [hbm(t) for t in srcs] + [hbm(t) for t in lands] + [jax.ShapeDtypeStruct((8, LANES), F32)],
        in_specs=[HBM_SPEC] * (2 * n) + [ANY],
        out_specs=[SEM_SPEC, SEM_SPEC] + [HBM_SPEC] * (2 * n) + [pl.BlockSpec(memory_space=pltpu.VMEM)],
        input_output_aliases={i: 2 + i for i in range(2 * n)},
        compiler_params=pltpu.CompilerParams(has_side_effects=DATAFLOW),
    )(*[pltpu.with_memory_space_constraint(t, pltpu.HBM) for t in list(srcs) + lands], after)
    return dict(n=n, scatter=scatter, send_sems=outs[0], recv_sems=outs[1], srcs=outs[2:2 + n],
                lands=outs[2 + n:2 + 2 * n], token=outs[2 + 2 * n])


def _exchange_wait(name, handle, after):
    n, scatter = handle["n"], handle["scatter"]

    def body(*refs):
        src_refs, land_refs = refs[:n], refs[n:2 * n]
        send_sems, recv_sems = refs[2 * n], refs[2 * n + 1]
        for cp in _peer_copies(src_refs, land_refs, scatter, send_sems, recv_sems):
            cp.wait_send()
            cp.wait_recv()

    both = list(handle["srcs"]) + list(handle["lands"])
    outs = pl.pallas_call(
        body, name=name,
        out_shape=[pltpu.HBM(t.shape, t.dtype) for t in both],
        in_specs=[HBM_SPEC] * (2 * n) + [SEM_SPEC, SEM_SPEC, ANY],
        out_specs=[HBM_SPEC] * (2 * n),
        input_output_aliases={i: i for i in range(2 * n)},
        compiler_params=pltpu.CompilerParams(has_side_effects=DATAFLOW),
    )(*both, handle["send_sems"], handle["recv_sems"], after)
    me_idx = _my_index()
    lands = [land if sc is True else lax.dynamic_update_index_in_dim(land, src, me_idx, 0)
             for land, src, sc in zip(outs[n:], outs[:n], scatter)]
    return lands, outs[:n]


def _my_index():
    return 4 * lax.axis_index("x") + 2 * lax.axis_index("y") + lax.axis_index("c")


def _relay_copies(bufs, send_sems, recv_sems):
    x, y, c = lax.axis_index("x"), lax.axis_index("y"), lax.axis_index("c")
    copies = []
    for j, k in enumerate(RELAYED):
        px = 1 - x if (k >> 2) & 1 else x
        py = 1 - y if (k >> 1) & 1 else y
        slot = 4 * px + 2 * py + c
        for a, buf in enumerate(bufs):
            copies.append(pltpu.make_async_remote_copy(
                src_ref=buf.at[slot], dst_ref=buf.at[slot], send_sem=send_sems.at[a * len(RELAYED) + j],
                recv_sem=recv_sems.at[a * len(RELAYED) + j], device_id=(x, y, 1 - c), device_id_type=MESH))
    return copies


def _relay_start(name, bufs, after):
    n = len(bufs)

    def body(*refs):
        send_sems, recv_sems = refs[n + 1], refs[n + 2]
        for cp in _relay_copies(refs[:n], send_sems, recv_sems):
            cp.start()
        token = refs[2 * n + 3]
        token[...] = jnp.zeros_like(token)

    n_sems = n * len(RELAYED)
    outs = pl.pallas_call(
        body, name=name,
        out_shape=[pltpu.SemaphoreType.DMA((n_sems,)), pltpu.SemaphoreType.DMA((n_sems,))]
        + [pltpu.HBM(t.shape, t.dtype) for t in bufs] + [jax.ShapeDtypeStruct((8, LANES), F32)],
        in_specs=[HBM_SPEC] * n + [ANY],
        out_specs=[SEM_SPEC, SEM_SPEC] + [HBM_SPEC] * n + [pl.BlockSpec(memory_space=pltpu.VMEM)],
        input_output_aliases={i: 2 + i for i in range(n)},
        compiler_params=pltpu.CompilerParams(has_side_effects=DATAFLOW),
    )(*[pltpu.with_memory_space_constraint(t, pltpu.HBM) for t in bufs], after)
    return dict(n=n, send_sems=outs[0], recv_sems=outs[1], bufs=outs[2:2 + n], token=outs[2 + n])


def _relay_wait(name, handle, after):
    n = handle["n"]

    def body(*refs):
        for cp in _relay_copies(refs[:n], refs[n], refs[n + 1]):
            cp.wait_send()
            cp.wait_recv()

    return pl.pallas_call(
        body, name=name,
        out_shape=[pltpu.HBM(t.shape, t.dtype) for t in handle["bufs"]],
        in_specs=[HBM_SPEC] * n + [SEM_SPEC, SEM_SPEC, ANY],
        out_specs=[HBM_SPEC] * n,
        input_output_aliases={i: i for i in range(n)},
        compiler_params=pltpu.CompilerParams(has_side_effects=DATAFLOW),
    )(*handle["bufs"], handle["send_sems"], handle["recv_sems"], after)


def _adamw(parts, w, m, v, sent=None):
    r, c = w.shape
    tr = 256 if r % 256 == 0 else r
    own = sent is not None

    def body(*refs):
        if own:
            _, p_ref, own_ref, w_ref, m_ref, v_ref, g_ref, d_ref, nm_ref, nv_ref = refs
            g = own_ref[...].astype(F32)
        else:
            p_ref, w_ref, m_ref, v_ref, g_ref, d_ref, nm_ref, nv_ref = refs
            g = p_ref[0].astype(F32)
        for k in range(1, N_DEV):
            g = g + p_ref[k].astype(F32)
        m_new = ADAM_B1 * m_ref[...] + (1.0 - ADAM_B1) * g
        v_new = ADAM_B2 * v_ref[...] + (1.0 - ADAM_B2) * (g * g)
        m_hat = m_new / (1.0 - ADAM_B1 ** ADAM_STEP)
        v_hat = v_new / (1.0 - ADAM_B2 ** ADAM_STEP)
        g_ref[...] = g
        d_ref[...] = -ADAM_LR * (m_hat / (jnp.sqrt(v_hat) + ADAM_EPS) + ADAM_WD * w_ref[...])
        nm_ref[...] = m_new
        nv_ref[...] = v_new

    out_shape = [jax.ShapeDtypeStruct((r, c), F32)] * 4
    if not own:
        return pl.pallas_call(
            body, name="adamw", grid=(r // tr,),
            in_specs=[pl.BlockSpec((N_DEV, tr, c), lambda i: (0, i, 0))] + [_rows(tr, c)] * 3,
            out_specs=[_rows(tr, c)] * 4, out_shape=out_shape, compiler_params=_params(),
        )(parts, w, m, v)
    rows = pl.BlockSpec((tr, c), lambda i, me: (i, 0))
    return pl.pallas_call(
        body, name="adamw_own", out_shape=out_shape, compiler_params=_params(),
        grid_spec=pltpu.PrefetchScalarGridSpec(
            num_scalar_prefetch=1, grid=(r // tr,),
            in_specs=[pl.BlockSpec((N_DEV, tr, c), lambda i, me: (0, i, 0)),
                      pl.BlockSpec((None, tr, c), lambda i, me: (me[0], i, 0))] + [rows] * 3,
            out_specs=[rows] * 4),
    )(_my_index().reshape(1).astype(jnp.int32), parts, sent, w, m, v)


def _adam_step(g, w, m, v):
    m_new = ADAM_B1 * m + (1.0 - ADAM_B1) * g
    v_new = ADAM_B2 * v + (1.0 - ADAM_B2) * (g * g)
    m_hat = m_new / (1.0 - ADAM_B1 ** ADAM_STEP)
    v_hat = v_new / (1.0 - ADAM_B2 ** ADAM_STEP)
    return -ADAM_LR * (m_hat / (jnp.sqrt(v_hat) + ADAM_EPS) + ADAM_WD * w), m_new, v_new


SMALL_NAMES = ("norm_mix", "pool_w", "pool_b", "pool_scale", "attn_sinks", "norm_mlp", "norm_final")


def _adamw_small(mlp_all, mix_all, scale_all, nmix_all, w, m, v):
    def body(mlp_ref, mix_ref, scale_ref, nmix_ref, *refs):
        ins, outs = refs[:21], refs[21:]

        def total(ref, rows, lanes=slice(None)):
            g = ref[0, rows, lanes]
            for k in range(1, N_DEV):
                g = g + ref[k, rows, lanes]
            return g

        grads = dict(
            norm_mix=total(nmix_ref, slice(0, 1)), pool_w=total(mix_ref, slice(0, MIX_POOL_B)),
            pool_b=total(mix_ref, slice(MIX_POOL_B, MIX_POOL_B + 4)), pool_scale=total(scale_ref, slice(0, 1)),
            attn_sinks=total(mix_ref, slice(MIX_SINKS, MIX_SINKS + 1)),
            norm_mlp=total(mlp_ref, slice(0, 1)), norm_final=total(mlp_ref, slice(1, 2)))
        for i, name in enumerate(SMALL_NAMES):
            g = grads[name]
            d, m_new, v_new = _adam_step(g, ins[3 * i][...], ins[3 * i + 1][...], ins[3 * i + 2][...])
            for ref, val in zip(outs[4 * i:4 * i + 4], (g, d, m_new, v_new)):
                ref[...] = val
        outs[28][...] = jnp.broadcast_to(total(mlp_ref, slice(2, 3), slice(0, LANES)), (8, LANES))

    operands, out_shape = [], []
    for name in SMALL_NAMES:
        operands += [w[name], m[name], v[name]]
        out_shape += [jax.ShapeDtypeStruct(w[name].shape, F32)] * 4
    out_shape.append(jax.ShapeDtypeStruct((8, LANES), F32))
    outs = pl.pallas_call(body, name="adamw_small", out_shape=out_shape)(
        mlp_all, mix_all, scale_all, nmix_all, *operands)
    return {name: outs[4 * i:4 * i + 4] for i, name in enumerate(SMALL_NAMES)}, outs[28]


def kernel(x, norm_mix, w_in, pool_w, pool_b, pool_scale, attn_sinks, p_pool, p_attn, w_out, norm_mlp, w_up, w_down, norm_final, loss_target, m_norm_mix, m_w_in, m_pool_w, m_pool_b, m_pool_scale, m_attn_sinks, m_p_pool, m_p_attn, m_w_out, m_norm_mlp, m_w_up, m_w_down, m_norm_final, v_norm_mix, v_w_in, v_pool_w, v_pool_b, v_pool_scale, v_attn_sinks, v_p_pool, v_p_attn, v_w_out, v_norm_mlp, v_w_up, v_w_down, v_norm_final):
    xs = x[0]
    tgt = loss_target[0]
    s_len = xs.shape[0]

    w_in_bf, p_pool_bf, p_attn_bf, w_out_bf, w_up_bf, w_down_bf = [
        t[0].astype(BF16) for t in (w_in, p_pool, p_attn, w_out, w_up, w_down)]
    (w_in_g,) = _all_gather_weights("all_gather_w_in", [w_in_bf])
    ag_rest = _exchange_start(
        "ag_rest_start", [p_pool_bf, p_attn_bf, w_out_bf, w_up_bf, w_down_bf], ("chip",) * 5, w_in_g)

    pool_w_bf = pool_w[0].astype(BF16)
    pool_b_row = pool_b[0].reshape(1, POOL_WIDTH)
    bias_t, sink_row = _attn_constants(attn_sinks[0])

    u, zp, q, kv, zg, w_in_f = _fwd_in(ag_rest["token"], xs, norm_mix, w_in_g)
    pm, o = _mixers_fwd(zp, q, kv, pool_w_bf, pool_b_row, pool_scale, bias_t, sink_row)
    first_level, _ = _exchange_wait("ag_rest_wait", ag_rest, o)
    relay = _relay_start("ag_relay_start", first_level, pm)
    p_pool_g, p_attn_g, w_out_g, w_up_g, w_down_g = _relay_wait("ag_relay_wait", relay, relay["token"])
    p_pool_f = p_pool_g.transpose(1, 0, 2).reshape(POOL_WIDTH, D_MODEL)
    p_attn_f = p_attn_g.transpose(1, 0, 2).reshape(ATTN_WIDTH, D_MODEL)
    w_out_f = w_out_g.reshape(D_MODEL, D_MODEL)
    w_down_f = w_down_g.reshape(D_FF, D_MODEL)
    mixed, dh1, a, dapre, u2, dh2, small_mlp, dyp, dya, dzg, dpm, do = _core(
        xs, pm, o, zg, tgt, norm_mlp, norm_final.reshape(1, D_MODEL), p_pool_f, p_attn_f, w_out_f, w_up_g, w_down_f)
    gw_down = _tn_matmul(a, dh2, square_a=True)
    gw_up = _tn_matmul(u2, dapre, col_blocks=N_DEV)
    ex_mlp = _exchange_start(
        "ex_mlp_start", [gw_up, gw_down.reshape(N_DEV, D_FF // N_DEV, D_MODEL)], (True, True), small_mlp)
    gw_out = _tn_matmul(mixed, dh1, after=ex_mlp["token"])
    gp_pool = _tn_matmul(pm, dyp, col_blocks=N_DEV)
    gp_attn = _tn_matmul(o, dya, col_blocks=N_DEV)
    ex_proj = _exchange_start(
        "ex_proj_start", [gp_pool, gp_attn, gw_out.reshape(N_DEV, D_MODEL // N_DEV, D_MODEL)], (True,) * 3, small_mlp)
    dzp, dq, dkv, small_mix, g_pool_scale = _mixers_bwd(
        ex_proj["token"], zp, q, kv, dpm, do, pool_w_bf, pool_b_row, pool_scale, bias_t, sink_row)
    gw_in = _tn_w_in(u, dzp, dq, dkv, dzg)
    ex_in = _exchange_start(
        "ex_in_start", [gw_in, small_mlp, small_mix, g_pool_scale], (True, False, False, False), dq)
    dx, g_norm_mix = _in_bwd(ex_in["token"], dzp, dq, dkv, dzg, w_in_f, xs, dh1, norm_mix)

    big_w = dict(w_in=w_in, p_pool=p_pool, p_attn=p_attn, w_out=w_out, w_up=w_up, w_down=w_down)
    big_m = dict(w_in=m_w_in, p_pool=m_p_pool, p_attn=m_p_attn, w_out=m_w_out, w_up=m_w_up, w_down=m_w_down)
    big_v = dict(w_in=v_w_in, p_pool=v_p_pool, p_attn=v_p_attn, w_out=v_w_out, w_up=v_w_up, w_down=v_w_down)
    res = {}

    def update(names, recvs, sents):
        for name, parts, sent in zip(names, recvs, sents):
            outs = _adamw(parts, big_w[name][0], big_m[name][0], big_v[name][0], sent)
            res[name] = [t[None] for t in outs]

    update(["w_up", "w_down"], *_exchange_wait("ex_mlp_wait", ex_mlp, dx))
    update(["p_pool", "p_attn", "w_out"], *_exchange_wait("ex_proj_wait", ex_proj, res["w_down"][0]))
    (norm_mix_all,) = _all_gather_weights("all_gather_norm_mix", [g_norm_mix], res["w_out"][0])
    (r_in, mlp_all, mix_all, scale_all), (s_in, _, _, _) = _exchange_wait("ex_in_wait", ex_in, norm_mix_all)
    update(["w_in"], [r_in], [s_in])

    natural = dict(norm_mix=(1, D_MODEL), pool_w=(MIX_POOL_B, LANES), pool_b=(4, LANES), pool_scale=(1, POOL_WIDTH),
                   attn_sinks=(1, LANES), norm_mlp=(1, D_MODEL), norm_final=(1, D_MODEL))

    def as_2d(t, name):
        if name == "attn_sinks":
            return jnp.pad(t, ((0, 0), (0, LANES - N_HEADS)))
        return t.reshape(natural[name])

    small_w = dict(norm_mix=norm_mix, pool_w=pool_w, pool_b=pool_b, pool_scale=pool_scale, attn_sinks=attn_sinks,
                   norm_mlp=norm_mlp, norm_final=norm_final)
    small_m = dict(norm_mix=m_norm_mix, pool_w=m_pool_w, pool_b=m_pool_b, pool_scale=m_pool_scale,
                   attn_sinks=m_attn_sinks, norm_mlp=m_norm_mlp, norm_final=m_norm_final)
    small_v = dict(norm_mix=v_norm_mix, pool_w=v_pool_w, pool_b=v_pool_b, pool_scale=v_pool_scale,
                   attn_sinks=v_attn_sinks, norm_mlp=v_norm_mlp, norm_final=v_norm_final)
    small_res, loss_all = _adamw_small(
        mlp_all, mix_all, scale_all, norm_mix_all,
        *[{k: as_2d(t, k) for k, t in d.items()} for d in (small_w, small_m, small_v)])
    loss = loss_all[0, 0]
    for name in SMALL_NAMES:
        shape = small_w[name].shape
        res[name] = [(t[:, :N_HEADS] if name == "attn_sinks" else t).reshape(shape) for t in small_res[name]]

    order = ["norm_mix", "w_in", "pool_w", "pool_b", "pool_scale", "attn_sinks", "p_pool", "p_attn", "w_out",
             "norm_mlp", "w_up", "w_down", "norm_final"]
    out = [loss, dx[None]]
    for kind in range(4):
        out += [res[name][kind] for name in order]
    return tuple(out)
```

```python
import functools
import math

import numpy as np
import jax
import jax.numpy as jnp
from jax import lax
from jax.experimental import pallas as pl
from jax.experimental.pallas import tpu as pltpu

F32 = jnp.float32
BF16 = jnp.bfloat16

D_MODEL = 1024
POOL_WIDTH = 512
ATTN_WIDTH = 512
KV_WIDTH = 128
HEAD_DIM = 64
N_HEADS = 8
N_KV_HEADS = 2
GROUP = 4
BLOCK = 128
POOL_WINDOWS = (2, 4, 8, 16)
POOL_GROUP_DIM = 128
POOL_HALO = 16
D_FF = 4096
FF_CHUNK = 1024
IN_WIDTH = 3328
RMS_EPS = 1e-5
NEG_INF = -1e30
ATTN_SCALE = 1.0 / math.sqrt(HEAD_DIM)
N_DEV = 8

ADAM_LR = 0.001
ADAM_B1 = 0.9
ADAM_B2 = 0.999
ADAM_EPS = 1e-08
ADAM_WD = 0.01
ADAM_STEP = 10

LANES = 128
VMEM_LIMIT_BYTES = 56 * 1024 * 1024
MESH = pl.DeviceIdType.MESH


def _params(n_grid_axes=1):
    return pltpu.CompilerParams(
        dimension_semantics=("arbitrary",) * n_grid_axes, vmem_limit_bytes=VMEM_LIMIT_BYTES)


def _dot(a, b):
    return jnp.dot(a, b, preferred_element_type=F32)


def _dot_nt(a, b):
    return lax.dot_general(a, b, (((1,), (1,)), ((), ())), preferred_element_type=F32)


def _dot_tn(a, b):
    return lax.dot_general(a, b, (((0,), (0,)), ((), ())), preferred_element_type=F32)


ANY = pl.BlockSpec(memory_space=pl.ANY)


def _rows(tm, n):
    return pl.BlockSpec((tm, n), lambda i: (i, 0))


def _whole(shape):
    zeros = (0,) * len(shape)
    return pl.BlockSpec(shape, lambda i: zeros)


def _rms_fwd(h, g):
    r = lax.rsqrt(jnp.mean(h * h, axis=-1, keepdims=True) + RMS_EPS)
    xh = h * r
    return r, xh, xh * g


def _rms_bwd(dy, xh, r, g):
    dxh = dy * g
    dh = r * (dxh - xh * jnp.mean(dxh * xh, axis=-1, keepdims=True))
    return dh, jnp.sum(dy * xh, axis=0, keepdims=True)


def _fwd_in(after, x, g_mix, w_in_blocks):
    s_len = x.shape[0]
    tm = min(512, s_len)
    width = IN_WIDTH // N_DEV

    def body(after_ref, x_ref, g_ref, wb_ref, u_ref, zp_ref, q_ref, kv_ref, zg_ref, w_ref):
        @pl.when(pl.program_id(0) == 0)
        def _():
            for j in range(N_DEV):
                w_ref[:, j * width:(j + 1) * width] = wb_ref[j]

        _, _, u = _rms_fwd(x_ref[...], g_ref[...])
        u = u.astype(BF16)
        u_ref[...] = u
        zp_ref[...] = _dot(u, w_ref[:, 0:512]).astype(BF16)
        q_ref[...] = _dot(u, w_ref[:, 512:1024]).astype(BF16)
        kv_ref[...] = _dot(u, w_ref[:, 1024:1280]).astype(BF16)
        zg_ref[...] = _dot(u, w_ref[:, 1280:3328]).astype(BF16)

    return pl.pallas_call(
        body, name="fwd_in", grid=(s_len // tm,),
        in_specs=[ANY, _rows(tm, D_MODEL), _whole((1, D_MODEL)),
                  pl.BlockSpec((N_DEV, D_MODEL, width), lambda i: (0, 0, 0), pipeline_mode=pl.Buffered(1))],
        out_specs=[_rows(tm, D_MODEL), _rows(tm, 512), _rows(tm, 512), _rows(tm, 256), _rows(tm, 2048),
                   _whole((D_MODEL, IN_WIDTH))],
        out_shape=[jax.ShapeDtypeStruct((s_len, n), BF16) for n in (D_MODEL, 512, 512, 256, 2048)]
        + [jax.ShapeDtypeStruct((D_MODEL, IN_WIDTH), BF16)],
        compiler_params=_params(),
    )(after, x, g_mix, w_in_blocks)


def _attn_constants(sinks):
    r = np.arange(BLOCK)[:, None]
    qi = np.arange(BLOCK)[None, :]
    dist = np.where(r <= qi, qi - r, BLOCK + qi - r).astype(np.float32)
    slopes = np.array([2.0 ** (-8.0 * (h + 1) / N_HEADS) for h in range(N_HEADS)], dtype=np.float32)
    bias = (-slopes[:, None, None] * dist[None]).reshape(N_KV_HEADS, GROUP, BLOCK, BLOCK)
    bias = np.ascontiguousarray(bias.transpose(0, 2, 1, 3)).reshape(N_KV_HEADS, BLOCK, GROUP * BLOCK)
    sink_row = jnp.repeat(sinks.astype(F32).reshape(N_KV_HEADS, GROUP), BLOCK, axis=1)[:, None, :]
    return jnp.asarray(bias.astype(np.float32)), sink_row


def _own_block_mask():
    shape = (BLOCK, GROUP * BLOCK)
    r = lax.broadcasted_iota(jnp.int32, shape, 0)
    qi = lax.broadcasted_iota(jnp.int32, shape, 1) & (BLOCK - 1)
    return r <= qi


def _pack_keys(t, own):
    return jnp.where(own, t[BLOCK:], t[:BLOCK])


def _unpack_keys(t, own):
    zero = jnp.zeros_like(t)
    return jnp.concatenate([jnp.where(own, zero, t), jnp.where(own, t, zero)], axis=0)


def _left_half(shape):
    return lax.broadcasted_iota(jnp.int32, shape, 1) < HEAD_DIM


def _dup_halves(slab):
    swapped = pltpu.roll(slab, HEAD_DIM, 1)
    left = _left_half(slab.shape)
    return jnp.where(left, slab, swapped), jnp.where(left, swapped, slab)


def _fill_kv_slabs(kvh_ref, kv_ref, ka_ref, vd_ref):
    for rows, src in ((slice(0, BLOCK), kvh_ref), (slice(BLOCK, None), kv_ref)):
        kvf = src[...].astype(F32)
        for ref, lanes in ((ka_ref, slice(0, KV_WIDTH)), (vd_ref, slice(KV_WIDTH, 2 * KV_WIDTH))):
            d0, d1 = _dup_halves(kvf[:, lanes])
            ref[0, rows, :] = d0.astype(BF16)
            ref[1, rows, :] = d1.astype(BF16)


def _stack_pairs(a, h):
    pieces = []
    for j in range(2):
        pair = a[:, h * 256 + j * LANES:h * 256 + (j + 1) * LANES]
        left = _left_half(pair.shape)
        zero = jnp.zeros_like(pair)
        pieces += [jnp.where(left, pair, zero), jnp.where(left, zero, pair)]
    return jnp.concatenate(pieces, axis=0)


def _attn_probs(kk, q_st, bias_p, sink_row, own, first):
    s = _pack_keys(_dot_nt(kk, q_st), own) * ATTN_SCALE + bias_p
    if first is not None:
        s = jnp.where(jnp.logical_and(first, jnp.logical_not(own)), NEG_INF, s)
    m = jnp.maximum(jnp.max(s, axis=0, keepdims=True), sink_row)
    p = jnp.exp(s - m)
    es = jnp.exp(sink_row - m)
    inv = 1.0 / (jnp.sum(p, axis=0, keepdims=True) + es)
    return p * inv, es * inv


def _pool_d(ext, cur, g, row0):
    w = POOL_WINDOWS[g]
    acc = ext
    k = 1
    while k < w:
        acc = acc + pltpu.roll(acc, k, 0)
        k *= 2
    return _window_mean(acc[POOL_HALO:, :], w, row0) - cur


def _window_mean(total, w, row0):
    t = row0 + lax.broadcasted_iota(jnp.int32, (POOL_HALO, total.shape[1]), 0)
    head = total[:POOL_HALO] / jnp.minimum(t + 1, w).astype(F32)
    return jnp.concatenate([head, total[POOL_HALO:] * (1.0 / w)], axis=0)


def _mixers_fwd(zp, q, kv, pool_w, pool_b, pool_scale, bias_t, sink_row):
    s_len = zp.shape[0]
    tq = min(512, s_len)
    nb = tq // BLOCK

    def body(zp_ref, zph_ref, q_ref, kv_ref, kvh_ref, pw_ref, pb_ref, ps_ref, bias_ref, sink_ref,
             pm_ref, o_ref, ka_ref, vd_ref):
        i = pl.program_id(0)
        cur = zp_ref[...].astype(F32)
        halo = zph_ref[...].astype(F32) * (i > 0).astype(F32)
        ext = jnp.concatenate([halo, cur], axis=0)
        for g in range(4):
            sl = slice(g * POOL_GROUP_DIM, (g + 1) * POOL_GROUP_DIM)
            d = _pool_d(ext[:, sl], cur[:, sl], g, i * tq)
            y = _dot(d.astype(BF16), pw_ref[g]) + pb_ref[:, sl]
            pm_ref[:, sl] = (y * ps_ref[:, sl]).astype(BF16)
        _fill_kv_slabs(kvh_ref, kv_ref, ka_ref, vd_ref)
        own = _own_block_mask()
        for b in range(nb):
            rq = slice(b * BLOCK, (b + 1) * BLOCK)
            rk = slice(b * BLOCK, (b + 2) * BLOCK)
            qb = q_ref[rq, :]
            for h in range(N_KV_HEADS):
                pn, _ = _attn_probs(ka_ref[h, rk, :], _stack_pairs(qb, h), bias_ref[h], sink_ref[h], own,
                                    (i == 0) if b == 0 else None)
                pn = _unpack_keys(pn, own).astype(BF16)
                vd = vd_ref[h, rk, :]
                left = _left_half(vd.shape)
                zero = jnp.zeros_like(vd)
                va, vb = jnp.where(left, vd, zero), jnp.where(left, zero, vd)
                for j in range(2):
                    o_pair = (_dot_tn(pn[:, (2 * j) * BLOCK:(2 * j + 1) * BLOCK], va)
                              + _dot_tn(pn[:, (2 * j + 1) * BLOCK:(2 * j + 2) * BLOCK], vb))
                    o_ref[rq, h * 256 + j * LANES:h * 256 + (j + 1) * LANES] = o_pair.astype(BF16)

    halo_pool = pl.BlockSpec((POOL_HALO, 512), lambda i: (jnp.maximum(i * (tq // POOL_HALO) - 1, 0), 0))
    halo_kv = pl.BlockSpec((BLOCK, 256), lambda i: (jnp.maximum(i * nb - 1, 0), 0))
    return pl.pallas_call(
        body, name="mixers_fwd", grid=(s_len // tq,),
        in_specs=[_rows(tq, 512), halo_pool, _rows(tq, 512), _rows(tq, 256), halo_kv,
                  _whole((4, 128, 128)), _whole((1, 512)), _whole((1, 512)),
                  _whole((N_KV_HEADS, BLOCK, GROUP * BLOCK)), _whole((N_KV_HEADS, 1, GROUP * BLOCK))],
        out_specs=[_rows(tq, 512), _rows(tq, 512)],
        out_shape=[jax.ShapeDtypeStruct((s_len, 512), BF16)] * 2,
        scratch_shapes=[pltpu.VMEM((N_KV_HEADS, tq + BLOCK, LANES), BF16)] * 2,
        compiler_params=_params(),
    )(zp, zp, q, kv, kv, pool_w, pool_b, pool_scale, bias_t, sink_row)


def _gated_mix(pm, o, zg, pp_ref, pa_ref):
    yp = _dot(pm, pp_ref[...])
    ya = _dot(o, pa_ref[...])
    gp = jax.nn.sigmoid(zg[:, :D_MODEL].astype(F32))
    ga = jax.nn.sigmoid(zg[:, D_MODEL:].astype(F32))
    return yp, ya, gp, ga


def _core(x, pm, o, zg, tgt, g_mlp, g_fin, p_pool, p_attn, w_out, w_up_blocks, w_down):
    s_len = x.shape[0]
    tm = min(256, s_len)
    n_chunks = D_FF // FF_CHUNK
    up_block = D_FF // N_DEV
    per_chunk = FF_CHUNK // up_block

    def body(x_ref, pm_ref, o_ref, zg_ref, tgt_ref, gm_ref, gf_ref, pp_ref, pa_ref, wo_ref, wu_ref, wd_ref,
             mixed_ref, dh1_ref, a_ref, dap_ref, u2_ref, dh2_ref, small_ref,
             dyp_ref, dya_ref, dzg_ref, dpm_ref, do_ref):
        i = pl.program_id(0)

        @pl.when(i == 0)
        def _():
            small_ref[...] = jnp.zeros_like(small_ref)

        yp, ya, gp, ga = _gated_mix(pm_ref[...], o_ref[...], zg_ref[...], pp_ref, pa_ref)
        mixed = (gp * yp + ga * ya).astype(BF16)
        mixed_ref[...] = mixed
        h1 = x_ref[...] + _dot(mixed, wo_ref[...])
        r2, xh2, u2 = _rms_fwd(h1, gm_ref[...])
        u2 = u2.astype(BF16)
        u2_ref[...] = u2
        acc = jnp.zeros((tm, D_MODEL), F32)
        for c in range(n_chunks):
            cs = slice(c * FF_CHUNK, (c + 1) * FF_CHUNK)
            a = jnp.concatenate([_dot(u2, wu_ref[per_chunk * c + j]) for j in range(per_chunk)], axis=1)
            a = jnp.maximum(a, 0.0)
            a_ref[:, cs] = a.astype(BF16)
            acc = acc + _dot((a * a).astype(BF16), wd_ref[cs, :])
        h2 = h1 + acc
        r3, xh3, y = _rms_fwd(h2, gf_ref[...])
        diff = y - tgt_ref[...]
        small_ref[2:3, :] += 0.5 * jnp.sum(jnp.mean(diff * diff, axis=-1, keepdims=True))
        dy = diff * (1.0 / D_MODEL)
        dh2, dgf = _rms_bwd(dy, xh3, r3, gf_ref[...])
        small_ref[1:2, :] += dgf
        dh2_bf = dh2.astype(BF16)
        dh2_ref[...] = dh2_bf
        du2 = jnp.zeros((tm, D_MODEL), F32)
        for c in range(n_chunks):
            cs = slice(c * FF_CHUNK, (c + 1) * FF_CHUNK)
            ds = _dot_nt(dh2_bf, wd_ref[cs, :])
            dap = (ds * (2.0 * a_ref[:, cs].astype(F32))).astype(BF16)
            dap_ref[:, cs] = dap
            for j in range(per_chunk):
                du2 = du2 + _dot_nt(dap[:, j * up_block:(j + 1) * up_block], wu_ref[per_chunk * c + j])
        dh1n, dgm = _rms_bwd(du2, xh2, r2, gm_ref[...])
        small_ref[0:1, :] += dgm
        dh1 = dh2 + dh1n
        dh1_ref[...] = dh1
        dm = _dot_nt(dh1.astype(BF16), wo_ref[...])
        dyp = (dm * gp).astype(BF16)
        dya = (dm * ga).astype(BF16)
        dyp_ref[...] = dyp
        dya_ref[...] = dya
        dzg_ref[:, :D_MODEL] = (dm * yp * (gp * (1.0 - gp))).astype(BF16)
        dzg_ref[:, D_MODEL:] = (dm * ya * (ga * (1.0 - ga))).astype(BF16)
        dpm_ref[...] = _dot_nt(dyp, pp_ref[...]).astype(BF16)
        do_ref[...] = _dot_nt(dya, pa_ref[...]).astype(BF16)

    def fixed(shape):
        return pl.BlockSpec(shape, lambda i: (0,) * len(shape), pipeline_mode=pl.Buffered(1))

    widths_dtypes = ((D_MODEL, BF16), (D_MODEL, F32), (D_FF, BF16), (D_FF, BF16), (D_MODEL, BF16), (D_MODEL, BF16))
    back = ((D_MODEL, BF16), (D_MODEL, BF16), (2048, BF16), (512, BF16), (512, BF16))
    return pl.pallas_call(
        body, name="core", grid=(s_len // tm,),
        in_specs=[_rows(tm, D_MODEL), _rows(tm, 512), _rows(tm, 512), _rows(tm, 2048), _rows(tm, D_MODEL),
                  _whole((1, D_MODEL)), _whole((1, D_MODEL)),
                  fixed((512, D_MODEL)), fixed((512, D_MODEL)), fixed((D_MODEL, D_MODEL)),
                  fixed((N_DEV, D_MODEL, up_block)), fixed((D_FF, D_MODEL))],
        out_specs=[_rows(tm, n) for n, _ in widths_dtypes] + [_whole((8, D_MODEL))] + [_rows(tm, n) for n, _ in back],
        out_shape=[jax.ShapeDtypeStruct((s_len, n), d) for n, d in widths_dtypes]
        + [jax.ShapeDtypeStruct((8, D_MODEL), F32)] + [jax.ShapeDtypeStruct((s_len, n), d) for n, d in back],
        compiler_params=_params(),
    )(x, pm, o, zg, tgt, g_mlp, g_fin, p_pool, p_attn, w_out, w_up_blocks, w_down)


def _tn_matmul(a, b, square_a=False, col_blocks=None, after=None):
    s_len, ka = a.shape
    nb = b.shape[1]
    tt = min(2048, s_len)
    tk = min(1024, ka)
    tn = min(1024, nb)
    n_t = s_len // tt
    if col_blocks is None:
        out_spec = pl.BlockSpec((tk, tn), lambda k, j, t: (k, j))
        out_shape = jax.ShapeDtypeStruct((ka, nb), BF16)
    else:
        width = nb // col_blocks
        per_tile = tn // width
        out_spec = pl.BlockSpec((per_tile, tk, width), lambda k, j, t: (j, k, 0))
        out_shape = jax.ShapeDtypeStruct((col_blocks, ka, width), BF16)

    extra = [] if after is None else [after]

    def body(a_ref, b_ref, *rest):
        o_ref, acc_ref = rest[len(extra):]
        t = pl.program_id(2)

        @pl.when(t == 0)
        def _():
            acc_ref[...] = jnp.zeros_like(acc_ref)

        av = a_ref[...]
        if square_a:
            av = av * av
        acc_ref[...] += _dot_tn(av.astype(BF16), b_ref[...].astype(BF16))

        @pl.when(t == n_t - 1)
        def _():
            if col_blocks is None:
                o_ref[...] = acc_ref[...].astype(o_ref.dtype)
            else:
                for blk in range(per_tile):
                    o_ref[blk] = acc_ref[:, blk * width:(blk + 1) * width].astype(o_ref.dtype)

    return pl.pallas_call(
        body, name="tn_matmul", grid=(ka // tk, nb // tn, n_t),
        in_specs=[pl.BlockSpec((tt, tk), lambda k, j, t: (t, k)), pl.BlockSpec((tt, tn), lambda k, j, t: (t, j))]
        + [ANY] * len(extra),
        out_specs=out_spec, out_shape=out_shape,
        scratch_shapes=[pltpu.VMEM((tk, tn), F32)],
        compiler_params=_params(3),
    )(a, b, *extra)


def _tn_w_in(u, dzp, dq, dkv, dzg):
    s_len = u.shape[0]
    tt = min(1024, s_len)
    n_t = s_len // tt
    width = IN_WIDTH // N_DEV
    pieces = ((0, 512), (512, 1024), (1024, 1280), (1280, IN_WIDTH))

    def body(u_ref, dzp_ref, dq_ref, dkv_ref, dzg_ref, o_ref, acc_ref):
        t = pl.program_id(0)

        @pl.when(t == 0)
        def _():
            acc_ref[...] = jnp.zeros_like(acc_ref)

        uv = u_ref[...]
        for (c0, c1), ref in zip(pieces, (dzp_ref, dq_ref, dkv_ref, dzg_ref)):
            acc_ref[:, c0:c1] += _dot_tn(uv, ref[...])

        @pl.when(t == n_t - 1)
        def _():
            for j in range(N_DEV):
                o_ref[j] = acc_ref[:, j * width:(j + 1) * width].astype(BF16)

    return pl.pallas_call(
        body, name="tn_w_in", grid=(n_t,),
        in_specs=[_rows(tt, D_MODEL)] + [_rows(tt, c1 - c0) for c0, c1 in pieces],
        out_specs=_whole((N_DEV, D_MODEL, width)),
        out_shape=jax.ShapeDtypeStruct((N_DEV, D_MODEL, width), BF16),
        scratch_shapes=[pltpu.VMEM((D_MODEL, IN_WIDTH), F32)],
        compiler_params=_params(),
    )(u, dzp, dq, dkv, dzg)


MIX_POOL_B = 4 * POOL_GROUP_DIM
MIX_SINKS = MIX_POOL_B + 8
MIX_ROWS = MIX_SINKS + 8


def _mixers_bwd(after, zp, q, kv, dpm, do, pool_w, pool_b, pool_scale, bias_t, sink_row):
    s_len = zp.shape[0]
    tq = min(512, s_len)
    nb = tq // BLOCK
    n_steps = s_len // tq

    def body(after_ref, zp_ref, zph_ref, q_ref, kv_ref, kvh_ref, dpm_ref, dpmh_ref, do_ref, pw_ref, pb_ref, ps_ref,
             bias_ref, sink_ref, dzp_ref, dq_ref, dkv_ref, small_ref, dps_ref,
             ka_ref, vd_ref, dsk_acc, dkv_acc):
        i = pl.program_id(0)

        @pl.when(i == 0)
        def _():
            dkv_acc[...] = jnp.zeros_like(dkv_acc)
            small_ref[...] = jnp.zeros_like(small_ref)
            dps_ref[...] = jnp.zeros_like(dps_ref)
            dsk_acc[...] = jnp.zeros_like(dsk_acc)

        cur = zp_ref[...].astype(F32)
        halo = zph_ref[...].astype(F32) * (i > 0).astype(F32)
        ext = jnp.concatenate([halo, cur], axis=0)
        dpm_next = dpmh_ref[...].astype(F32) * (i < n_steps - 1).astype(F32)
        dpm_ext = jnp.concatenate([dpm_ref[...].astype(F32), dpm_next], axis=0)
        n_ext = tq + POOL_HALO
        for g in range(4):
            sl = slice(g * POOL_GROUP_DIM, (g + 1) * POOL_GROUP_DIM)
            w = POOL_WINDOWS[g]
            d = _pool_d(ext[:, sl], cur[:, sl], g, i * tq).astype(BF16)
            y_lin = _dot(d, pw_ref[g]) + pb_ref[:, sl]
            dps_ref[:, sl] += jnp.sum(dpm_ext[:tq, sl] * y_lin, axis=0, keepdims=True)
            dyl_ext = dpm_ext[:, sl] * ps_ref[:, sl]
            small_ref[MIX_POOL_B + g:MIX_POOL_B + g + 1, :] += jnp.sum(dyl_ext[:tq], axis=0, keepdims=True)
            dyl_bf = dyl_ext.astype(BF16)
            small_ref[g * POOL_GROUP_DIM:(g + 1) * POOL_GROUP_DIM, :] += _dot_tn(d, dyl_bf[:tq])
            dd = _dot_nt(dyl_bf, pw_ref[g])
            e = _window_mean(dd, w, i * tq)
            acc = e
            k = 1
            while k < w:
                acc = acc + pltpu.roll(acc, n_ext - k, 0)
                k *= 2
            dzp_ref[:, sl] = (acc[:tq] - dd[:tq]).astype(BF16)

        _fill_kv_slabs(kvh_ref, kv_ref, ka_ref, vd_ref)

        def fold(dup):
            return dup + pltpu.roll(dup, HEAD_DIM, 1)

        own = _own_block_mask()
        for b in range(nb):
            rq = slice(b * BLOCK, (b + 1) * BLOCK)
            rk = slice(b * BLOCK, (b + 2) * BLOCK)
            qb = q_ref[rq, :]
            dob = do_ref[rq, :]
            dk_dup, dv_dup = [], []
            for h in range(N_KV_HEADS):
                kk = ka_ref[h, rk, :]
                q_st = _stack_pairs(qb, h)
                do_st = _stack_pairs(dob, h)
                pn, psink = _attn_probs(kk, q_st, bias_ref[h], sink_ref[h], own, (i == 0) if b == 0 else None)
                dp = _pack_keys(_dot_nt(vd_ref[h, rk, :], do_st), own)
                delta = jnp.sum(pn * dp, axis=0, keepdims=True)
                dsk_acc[h] += -psink * delta
                ds = _unpack_keys((pn * (dp - delta)) * ATTN_SCALE, own).astype(BF16)
                pn = _unpack_keys(pn, own)
                dq_st = _dot_tn(ds, kk)
                for j in range(2):
                    left = _left_half((BLOCK, LANES))
                    dq_pair = jnp.where(left, dq_st[(2 * j) * BLOCK:(2 * j + 1) * BLOCK],
                                        dq_st[(2 * j + 1) * BLOCK:(2 * j + 2) * BLOCK])
                    dq_ref[rq, h * 256 + j * LANES:h * 256 + (j + 1) * LANES] = dq_pair.astype(BF16)
                dk_dup.append(fold(_dot(ds, q_st)))
                dv_dup.append(fold(_dot(pn.astype(BF16), do_st)))
            left = _left_half((2 * BLOCK, LANES))
            dkv_blk = jnp.concatenate([jnp.where(left, dk_dup[0], dk_dup[1]),
                                       jnp.where(left, dv_dup[0], dv_dup[1])], axis=1)
            g0 = pl.multiple_of(i * tq + b * BLOCK, BLOCK)
            dkv_acc[pl.ds(g0, 2 * BLOCK), :] += dkv_blk

        @pl.when(i == n_steps - 1)
        def _():
            dkv_ref[...] = dkv_acc[BLOCK:, :].astype(BF16)
            lane = lax.broadcasted_iota(jnp.int32, (1, LANES), 1)
            row = jnp.zeros((1, LANES), F32)
            for h in range(N_KV_HEADS):
                for g in range(GROUP):
                    tot = jnp.sum(dsk_acc[h, :, g * BLOCK:(g + 1) * BLOCK], axis=1, keepdims=True)
                    row = jnp.where(lane == GROUP * h + g, tot, row)
            small_ref[MIX_SINKS:MIX_SINKS + 1, :] = row

    blocks_per_tile = tq // POOL_HALO
    last_halo = s_len // POOL_HALO - 1
    halo_prev = pl.BlockSpec((POOL_HALO, 512), lambda i: (jnp.maximum(i * blocks_per_tile - 1, 0), 0))
    halo_next = pl.BlockSpec((POOL_HALO, 512), lambda i: (jnp.minimum((i + 1) * blocks_per_tile, last_halo), 0))
    halo_kv = pl.BlockSpec((BLOCK, 256), lambda i: (jnp.maximum(i * nb - 1, 0), 0))
    return pl.pallas_call(
        body, name="mixers_bwd", grid=(n_steps,),
        in_specs=[ANY, _rows(tq, 512), halo_prev, _rows(tq, 512), _rows(tq, 256), halo_kv,
                  _rows(tq, 512), halo_next, _rows(tq, 512),
                  _whole((4, 128, 128)), _whole((1, 512)), _whole((1, 512)),
                  _whole((N_KV_HEADS, BLOCK, GROUP * BLOCK)), _whole((N_KV_HEADS, 1, GROUP * BLOCK))],
        out_specs=[_rows(tq, 512), _rows(tq, 512), _whole((s_len, 256)),
                   _whole((MIX_ROWS, LANES)), _whole((1, 512))],
        out_shape=[jax.ShapeDtypeStruct((s_len, 512), BF16), jax.ShapeDtypeStruct((s_len, 512), BF16),
                   jax.ShapeDtypeStruct((s_len, 256), BF16), jax.ShapeDtypeStruct((MIX_ROWS, LANES), F32),
                   jax.ShapeDtypeStruct((1, 512), F32)],
        scratch_shapes=[pltpu.VMEM((N_KV_HEADS, tq + BLOCK, LANES), BF16)] * 2
        + [pltpu.VMEM((N_KV_HEADS, 1, GROUP * BLOCK), F32), pltpu.VMEM((s_len + BLOCK, 256), F32)],
        compiler_params=_params(),
    )(after, zp, zp, q, kv, kv, dpm, dpm, do, pool_w, pool_b, pool_scale, bias_t, sink_row)


def _in_bwd(after, dzp, dq, dkv, dzg, w_in, x, dh1, g_mix):
    s_len = x.shape[0]
    tm = min(512, s_len)

    def body(after_ref, dzp_ref, dq_ref, dkv_ref, dzg_ref, w_ref, x_ref, dh1_ref, g_ref, dx_ref, dg_ref):
        i = pl.program_id(0)

        @pl.when(i == 0)
        def _():
            dg_ref[...] = jnp.zeros_like(dg_ref)

        du = _dot_nt(dzp_ref[...], w_ref[:, 0:512])
        du = du + _dot_nt(dq_ref[...], w_ref[:, 512:1024])
        du = du + _dot_nt(dkv_ref[...], w_ref[:, 1024:1280])
        du = du + _dot_nt(dzg_ref[...], w_ref[:, 1280:3328])
        r, xh, _ = _rms_fwd(x_ref[...], g_ref[...])
        dxn, dg = _rms_bwd(du, xh, r, g_ref[...])
        dg_ref[...] += dg
        dx_ref[...] = dh1_ref[...] + dxn

    return pl.pallas_call(
        body, name="in_bwd", grid=(s_len // tm,),
        in_specs=[ANY, _rows(tm, 512), _rows(tm, 512), _rows(tm, 256), _rows(tm, 2048), _whole((D_MODEL, IN_WIDTH)),
                  _rows(tm, D_MODEL), _rows(tm, D_MODEL), _whole((1, D_MODEL))],
        out_specs=[_rows(tm, D_MODEL), _whole((1, D_MODEL))],
        out_shape=[jax.ShapeDtypeStruct((s_len, D_MODEL), F32), jax.ShapeDtypeStruct((1, D_MODEL), F32)],
        compiler_params=_params(),
    )(after, dzp, dq, dkv, dzg, w_in, x, dh1, g_mix)


def _all_gather_weights(name, shards, after=None):
    n = len(shards)
    extra = [] if after is None else [after]
    n_extra = len(extra)

    def body(*refs):
        ins, outs = refs[:n], refs[n + n_extra:2 * n + n_extra]
        send_sems, recv_sems, local_sems = refs[2 * n + n_extra:]
        x, y, c = lax.axis_index("x"), lax.axis_index("y"), lax.axis_index("c")
        me, sibling = (x, y, c), (x, y, 1 - c)
        chips = [(1 - x, y), (x, 1 - y), (1 - x, 1 - y)]

        def slot(a, px, py, pc):
            return outs[a].at[4 * px + 2 * py + pc]

        def copy(a, k, block, to, src=None):
            return pltpu.make_async_remote_copy(
                src_ref=slot(a, *block) if src is None else src, dst_ref=slot(a, *block),
                send_sem=send_sems.at[a, k], recv_sem=recv_sems.at[a, k], device_id=to, device_id_type=MESH)

        mine = [pltpu.make_async_copy(ins[a], slot(a, *me), local_sems.at[a]) for a in range(n)]
        for cp in mine:
            cp.start()
        first = []
        for a in range(n):
            first.append(copy(a, 0, me, sibling, src=ins[a]))
            first += [copy(a, 1 + j, me, (*chip, c), src=ins[a]) for j, chip in enumerate(chips)]
        for cp in first:
            cp.start()
        passed = []
        for a in range(n):
            for j, chip in enumerate(chips):
                copy(a, 1 + j, (*chip, c), me).wait_recv()
                cp = copy(a, 4 + j, (*chip, c), sibling)
                cp.start()
                passed.append(cp)
        for a in range(n):
            copy(a, 0, sibling, me).wait_recv()
            for j, chip in enumerate(chips):
                copy(a, 4 + j, (*chip, 1 - c), me).wait_recv()
        for cp in first + passed:
            cp.wait_send()
        for cp in mine:
            cp.wait()

    return pl.pallas_call(
        body, name=name,
        in_specs=[ANY] * (n + n_extra), out_specs=[ANY] * n,
        out_shape=[jax.ShapeDtypeStruct((N_DEV,) + s.shape, s.dtype) for s in shards],
        scratch_shapes=[pltpu.SemaphoreType.DMA((n, 7)), pltpu.SemaphoreType.DMA((n, 7)), pltpu.SemaphoreType.DMA((n,))],
    )(*shards, *extra)


HBM_SPEC = pl.BlockSpec(memory_space=pltpu.HBM)
SEM_SPEC = pl.BlockSpec(memory_space=pltpu.SEMAPHORE)
DATAFLOW = pltpu.SideEffectType.DATAFLOW_SIDE_EFFECTING
N_PEERS = N_DEV - 1


CHIP_PEERS = (1, 2, 4, 6)
RELAYED = (2, 4, 6)


def _peer_copies(srcs, lands, scatter, send_sems, recv_sems):
    x, y, c = lax.axis_index("x"), lax.axis_index("y"), lax.axis_index("c")
    me_idx = 4 * x + 2 * y + c
    copies = []
    for k in range(1, N_DEV):
        px = 1 - x if (k >> 2) & 1 else x
        py = 1 - y if (k >> 1) & 1 else y
        pc = 1 - c if k & 1 else c
        p_idx = 4 * px + 2 * py + pc
        for a in range(len(srcs)):
            if scatter[a] == "chip" and k not in CHIP_PEERS:
                continue
            src = srcs[a].at[p_idx] if scatter[a] is True else srcs[a]
            dst = lands[a].at[k] if scatter[a] is True else lands[a].at[me_idx]
            copies.append(pltpu.make_async_remote_copy(
                src_ref=src, dst_ref=dst, send_sem=send_sems.at[a * N_PEERS + k - 1],
                recv_sem=recv_sems.at[a * N_PEERS + k - 1],
                device_id=(px, py, pc), device_id_type=MESH))
    return copies


def _exchange_start(name, srcs, scatter, after):
    n = len(srcs)
    lands = [lax.empty(s.shape if sc is True else (N_DEV,) + s.shape, s.dtype) for s, sc in zip(srcs, scatter)]

    def body(*refs):
        src_refs, land_refs = refs[:n], refs[n:2 * n]
        send_sems, recv_sems = refs[2 * n + 1], refs[2 * n + 2]
        token = refs[4 * n + 3]
        for cp in _peer_copies(src_refs, land_refs, scatter, send_sems, recv_sems):
            cp.start()
        token[...] = jnp.zeros_like(token)

    hbm = lambda t: pltpu.HBM(t.shape, t.dtype)
    outs = pl.pallas_call(
        body, name=name,
        out_shape=[pltpu.SemaphoreType.DMA((n * N_PEERS,)), pltpu.SemaphoreType.DMA((n * N_PEERS,))]
        + [hbm(t) for t in srcs] + [hbm(t) for t in lands] + [jax.ShapeDtypeStruct((8, LANES), F32)],
        in_specs=[HBM_SPEC] * (2 * n) + [ANY],
        out_specs=[SEM_SPEC, SEM_SPEC] + [HBM_SPEC] * (2 * n) + [pl.BlockSpec(memory_space=pltpu.VMEM)],
        input_output_aliases={i: 2 + i for i in range(2 * n)},
        compiler_params=pltpu.CompilerParams(has_side_effects=DATAFLOW),
    )(*[pltpu.with_memory_space_constraint(t, pltpu.HBM) for t in list(srcs) + lands], after)
    return dict(n=n, scatter=scatter, send_sems=outs[0], recv_sems=outs[1], srcs=outs[2:2 + n],
                lands=outs[2 + n:2 + 2 * n], token=outs[2 + 2 * n])


def _exchange_wait(name, handle, after):
    n, scatter = handle["n"], handle["scatter"]

    def body(*refs):
        src_refs, land_refs = refs[:n], refs[n:2 * n]
        send_sems, recv_sems = refs[2 * n], refs[2 * n + 1]
        for cp in _peer_copies(src_refs, land_refs, scatter, send_sems, recv_sems):
            cp.wait_send()
            cp.wait_recv()

    both = list(handle["srcs"]) + list(handle["lands"])
    outs = pl.pallas_call(
        body, name=name,
        out_shape=[pltpu.HBM(t.shape, t.dtype) for t in both],
        in_specs=[HBM_SPEC] * (2 * n) + [SEM_SPEC, SEM_SPEC, ANY],
        out_specs=[HBM_SPEC] * (2 * n),
        input_output_aliases={i: i for i in range(2 * n)},
        compiler_params=pltpu.CompilerParams(has_side_effects=DATAFLOW),
    )(*both, handle["send_sems"], handle["recv_sems"], after)
    me_idx = _my_index()
    lands = [land if sc is True else lax.dynamic_update_index_in_dim(land, src, me_idx, 0)
             for land, src, sc in zip(outs[n:], outs[:n], scatter)]
    return lands, outs[:n]


def _my_index():
    return 4 * lax.axis_index("x") + 2 * lax.axis_index("y") + lax.axis_index("c")


def _relay_copies(bufs, send_sems, recv_sems):
    x, y, c = lax.axis_index("x"), lax.axis_index("y"), lax.axis_index("c")
    copies = []
    for j, k in enumerate(RELAYED):
        px = 1 - x if (k >> 2) & 1 else x
        py = 1 - y if (k >> 1) & 1 else y
        slot = 4 * px + 2 * py + c
        for a, buf in enumerate(bufs):
            copies.append(pltpu.make_async_remote_copy(
                src_ref=buf.at[slot], dst_ref=buf.at[slot], send_sem=send_sems.at[a * len(RELAYED) + j],
                recv_sem=recv_sems.at[a * len(RELAYED) + j], device_id=(x, y, 1 - c), device_id_type=MESH))
    return copies


def _relay_start(name, bufs, after):
    n = len(bufs)

    def body(*refs):
        send_sems, recv_sems = refs[n + 1], refs[n + 2]
        for cp in _relay_copies(refs[:n], send_sems, recv_sems):
            cp.start()
        token = refs[2 * n + 3]
        token[...] = jnp.zeros_like(token)

    n_sems = n * len(RELAYED)
    outs = pl.pallas_call(
        body, name=name,
        out_shape=[pltpu.SemaphoreType.DMA((n_sems,)), pltpu.SemaphoreType.DMA((n_sems,))]
        + [pltpu.HBM(t.shape, t.dtype) for t in bufs] + [jax.ShapeDtypeStruct((8, LANES), F32)],
        in_specs=[HBM_SPEC] * n + [ANY],
        out_specs=[SEM_SPEC, SEM_SPEC] + [HBM_SPEC] * n + [pl.BlockSpec(memory_space=pltpu.VMEM)],
        input_output_aliases={i: 2 + i for i in range(n)},
        compiler_params=pltpu.CompilerParams(has_side_effects=DATAFLOW),
    )(*[pltpu.with_memory_space_constraint(t, pltpu.HBM) for t in bufs], after)
    return dict(n=n, send_sems=outs[0], recv_sems=outs[1], bufs=outs[2:2 + n], token=outs[2 + n])


def _relay_wait(name, handle, after):
    n = handle["n"]

    def body(*refs):
        for cp in _relay_copies(refs[:n], refs[n], refs[n + 1]):
            cp.wait_send()
            cp.wait_recv()

    return pl.pallas_call(
        body, name=name,
        out_shape=[pltpu.HBM(t.shape, t.dtype) for t in handle["bufs"]],
        in_specs=[HBM_SPEC] * n + [SEM_SPEC, SEM_SPEC, ANY],
        out_specs=[HBM_SPEC] * n,
        input_output_aliases={i: i for i in range(n)},
        compiler_params=pltpu.CompilerParams(has_side_effects=DATAFLOW),
    )(*handle["bufs"], handle["send_sems"], handle["recv_sems"], after)


def _adamw(parts, w, m, v, sent=None):
    r, c = w.shape
    tr = 256 if r % 256 == 0 else r
    own = sent is not None

    def body(*refs):
        if own:
            _, p_ref, own_ref, w_ref, m_ref, v_ref, g_ref, d_ref, nm_ref, nv_ref = refs
            g = own_ref[...].astype(F32)
        else:
            p_ref, w_ref, m_ref, v_ref, g_ref, d_ref, nm_ref, nv_ref = refs
            g = p_ref[0].astype(F32)
        for k in range(1, N_DEV):
            g = g + p_ref[k].astype(F32)
        m_new = ADAM_B1 * m_ref[...] + (1.0 - ADAM_B1) * g
        v_new = ADAM_B2 * v_ref[...] + (1.0 - ADAM_B2) * (g * g)
        m_hat = m_new / (1.0 - ADAM_B1 ** ADAM_STEP)
        v_hat = v_new / (1.0 - ADAM_B2 ** ADAM_STEP)
        g_ref[...] = g
        d_ref[...] = -ADAM_LR * (m_hat / (jnp.sqrt(v_hat) + ADAM_EPS) + ADAM_WD * w_ref[...])
        nm_ref[...] = m_new
        nv_ref[...] = v_new

    out_shape = [jax.ShapeDtypeStruct((r, c), F32)] * 4
    if not own:
        return pl.pallas_call(
            body, name="adamw", grid=(r // tr,),
            in_specs=[pl.BlockSpec((N_DEV, tr, c), lambda i: (0, i, 0))] + [_rows(tr, c)] * 3,
            out_specs=[_rows(tr, c)] * 4, out_shape=out_shape, compiler_params=_params(),
        )(parts, w, m, v)
    rows = pl.BlockSpec((tr, c), lambda i, me: (i, 0))
    return pl.pallas_call(
        body, name="adamw_own", out_shape=out_shape, compiler_params=_params(),
        grid_spec=pltpu.PrefetchScalarGridSpec(
            num_scalar_prefetch=1, grid=(r // tr,),
            in_specs=[pl.BlockSpec((N_DEV, tr, c), lambda i, me: (0, i, 0)),
                      pl.BlockSpec((None, tr, c), lambda i, me: (me[0], i, 0))] + [rows] * 3,
            out_specs=[rows] * 4),
    )(_my_index().reshape(1).astype(jnp.int32), parts, sent, w, m, v)


def _adam_step(g, w, m, v):
    m_new = ADAM_B1 * m + (1.0 - ADAM_B1) * g
    v_new = ADAM_B2 * v + (1.0 - ADAM_B2) * (g * g)
    m_hat = m_new / (1.0 - ADAM_B1 ** ADAM_STEP)
    v_hat = v_new / (1.0 - ADAM_B2 ** ADAM_STEP)
    return -ADAM_LR * (m_hat / (jnp.sqrt(v_hat) + ADAM_EPS) + ADAM_WD * w), m_new, v_new


SMALL_NAMES = ("norm_mix", "pool_w", "pool_b", "pool_scale", "attn_sinks", "norm_mlp", "norm_final")


def _adamw_small(mlp_all, mix_all, scale_all, nmix_all, w, m, v):
    def body(mlp_ref, mix_ref, scale_ref, nmix_ref, *refs):
        ins, outs = refs[:21], refs[21:]

        def total(ref, rows, lanes=slice(None)):
            g = ref[0, rows, lanes]
            for k in range(1, N_DEV):
                g = g + ref[k, rows, lanes]
            return g

        grads = dict(
            norm_mix=total(nmix_ref, slice(0, 1)), pool_w=total(mix_ref, slice(0, MIX_POOL_B)),
            pool_b=total(mix_ref, slice(MIX_POOL_B, MIX_POOL_B + 4)), pool_scale=total(scale_ref, slice(0, 1)),
            attn_sinks=total(mix_ref, slice(MIX_SINKS, MIX_SINKS + 1)),
            norm_mlp=total(mlp_ref, slice(0, 1)), norm_final=total(mlp_ref, slice(1, 2)))
        for i, name in enumerate(SMALL_NAMES):
            g = grads[name]
            d, m_new, v_new = _adam_step(g, ins[3 * i][...], ins[3 * i + 1][...], ins[3 * i + 2][...])
            for ref, val in zip(outs[4 * i:4 * i + 4], (g, d, m_new, v_new)):
                ref[...] = val
        outs[28][...] = jnp.broadcast_to(total(mlp_ref, slice(2, 3), slice(0, LANES)), (8, LANES))

    operands, out_shape = [], []
    for name in SMALL_NAMES:
        operands += [w[name], m[name], v[name]]
        out_shape += [jax.ShapeDtypeStruct(w[name].shape, F32)] * 4
    out_shape.append(jax.ShapeDtypeStruct((8, LANES), F32))
    outs = pl.pallas_call(body, name="adamw_small", out_shape=out_shape)(
        mlp_all, mix_all, scale_all, nmix_all, *operands)
    return {name: outs[4 * i:4 * i + 4] for i, name in enumerate(SMALL_NAMES)}, outs[28]


def kernel(x, norm_mix, w_in, pool_w, pool_b, pool_scale, attn_sinks, p_pool, p_attn, w_out, norm_mlp, w_up, w_down, norm_final, loss_target, m_norm_mix, m_w_in, m_pool_w, m_pool_b, m_pool_scale, m_attn_sinks, m_p_pool, m_p_attn, m_w_out, m_norm_mlp, m_w_up, m_w_down, m_norm_final, v_norm_mix, v_w_in, v_pool_w, v_pool_b, v_pool_scale, v_attn_sinks, v_p_pool, v_p_attn, v_w_out, v_norm_mlp, v_w_up, v_w_down, v_norm_final):
    xs = x[0]
    tgt = loss_target[0]
    s_len = xs.shape[0]

    w_in_bf, p_pool_bf, p_attn_bf, w_out_bf, w_up_bf, w_down_bf = [
        t[0].astype(BF16) for t in (w_in, p_pool, p_attn, w_out, w_up, w_down)]
    (w_in_g,) = _all_gather_weights("all_gather_w_in", [w_in_bf])
    ag_rest = _exchange_start(
        "ag_rest_start", [p_pool_bf, p_attn_bf, w_out_bf, w_up_bf, w_down_bf], ("chip",) * 5, w_in_g)

    pool_w_bf = pool_w[0].astype(BF16)
    pool_b_row = pool_b[0].reshape(1, POOL_WIDTH)
    bias_t, sink_row = _attn_constants(attn_sinks[0])

    u, zp, q, kv, zg, w_in_f = _fwd_in(ag_rest["token"], xs, norm_mix, w_in_g)
    pm, o = _mixers_fwd(zp, q, kv, pool_w_bf, pool_b_row, pool_scale, bias_t, sink_row)
    first_level, _ = _exchange_wait("ag_rest_wait", ag_rest, o)
    relay = _relay_start("ag_relay_start", first_level, pm)
    p_pool_g, p_attn_g, w_out_g, w_up_g, w_down_g = _relay_wait("ag_relay_wait", relay, relay["token"])
    p_pool_f = p_pool_g.transpose(1, 0, 2).reshape(POOL_WIDTH, D_MODEL)
    p_attn_f = p_attn_g.transpose(1, 0, 2).reshape(ATTN_WIDTH, D_MODEL)
    w_out_f = w_out_g.reshape(D_MODEL, D_MODEL)
    w_down_f = w_down_g.reshape(D_FF, D_MODEL)
    mixed, dh1, a, dapre, u2, dh2, small_mlp, dyp, dya, dzg, dpm, do = _core(
        xs, pm, o, zg, tgt, norm_mlp, norm_final.reshape(1, D_MODEL), p_pool_f, p_attn_f, w_out_f, w_up_g, w_down_f)
    gw_down = _tn_matmul(a, dh2, square_a=True)
    gw_up = _tn_matmul(u2, dapre, col_blocks=N_DEV)
    ex_mlp = _exchange_start(
        "ex_mlp_start", [gw_up, gw_down.reshape(N_DEV, D_FF // N_DEV, D_MODEL)], (True, True), small_mlp)
    gw_out = _tn_matmul(mixed, dh1, after=ex_mlp["token"])
    gp_pool = _tn_matmul(pm, dyp, col_blocks=N_DEV)
    gp_attn = _tn_matmul(o, dya, col_blocks=N_DEV)
    ex_proj = _exchange_start(
        "ex_proj_start", [gp_pool, gp_attn, gw_out.reshape(N_DEV, D_MODEL // N_DEV, D_MODEL)], (True,) * 3, small_mlp)
    dzp, dq, dkv, small_mix, g_pool_scale = _mixers_bwd(
        ex_proj["token"], zp, q, kv, dpm, do, pool_w_bf, pool_b_row, pool_scale, bias_t, sink_row)
    gw_in = _tn_w_in(u, dzp, dq, dkv, dzg)
    ex_in = _exchange_start(
        "ex_in_start", [gw_in, small_mlp, small_mix, g_pool_scale], (True, False, False, False), dq)
    dx, g_norm_mix = _in_bwd(ex_in["token"], dzp, dq, dkv, dzg, w_in_f, xs, dh1, norm_mix)

    big_w = dict(w_in=w_in, p_pool=p_pool, p_attn=p_attn, w_out=w_out, w_up=w_up, w_down=w_down)
    big_m = dict(w_in=m_w_in, p_pool=m_p_pool, p_attn=m_p_attn, w_out=m_w_out, w_up=m_w_up, w_down=m_w_down)
    big_v = dict(w_in=v_w_in, p_pool=v_p_pool, p_attn=v_p_attn, w_out=v_w_out, w_up=v_w_up, w_down=v_w_down)
    res = {}

    def update(names, recvs, sents):
        for name, parts, sent in zip(names, recvs, sents):
            outs = _adamw(parts, big_w[name][0], big_m[name][0], big_v[name][0], sent)
            res[name] = [t[None] for t in outs]

    update(["w_up", "w_down"], *_exchange_wait("ex_mlp_wait", ex_mlp, dx))
    update(["p_pool", "p_attn", "w_out"], *_exchange_wait("ex_proj_wait", ex_proj, res["w_down"][0]))
    (norm_mix_all,) = _all_gather_weights("all_gather_norm_mix", [g_norm_mix], res["w_out"][0])
    (r_in, mlp_all, mix_all, scale_all), (s_in, _, _, _) = _exchange_wait("ex_in_wait", ex_in, norm_mix_all)
    update(["w_in"], [r_in], [s_in])

    natural = dict(norm_mix=(1, D_MODEL), pool_w=(MIX_POOL_B, LANES), pool_b=(4, LANES), pool_scale=(1, POOL_WIDTH),
                   attn_sinks=(1, LANES), norm_mlp=(1, D_MODEL), norm_final=(1, D_MODEL))

    def as_2d(t, name):
        if name == "attn_sinks":
            return jnp.pad(t, ((0, 0), (0, LANES - N_HEADS)))
        return t.reshape(natural[name])

    small_w = dict(norm_mix=norm_mix, pool_w=pool_w, pool_b=pool_b, pool_scale=pool_scale, attn_sinks=attn_sinks,
                   norm_mlp=norm_mlp, norm_final=norm_final)
    small_m = dict(norm_mix=m_norm_mix, pool_w=m_pool_w, pool_b=m_pool_b, pool_scale=m_pool_scale,
                   attn_sinks=m_attn_sinks, norm_mlp=m_norm_mlp, norm_final=m_norm_final)
    small_v = dict(norm_mix=v_norm_mix, pool_w=v_pool_w, pool_b=v_pool_b, pool_scale=v_pool_scale,
                   attn_sinks=v_attn_sinks, norm_mlp=v_norm_mlp, norm_final=v_norm_final)
    small_res, loss_all = _adamw_small(
        mlp_all, mix_all, scale_all, norm_mix_all,
        *[{k: as_2d(t, k) for k, t in d.items()} for d in (small_w, small_m, small_v)])
    loss = loss_all[0, 0]
    for name in SMALL_NAMES:
        shape = small_w[name].shape
        res[name] = [(t[:, :N_HEADS] if name == "attn_sinks" else t).reshape(shape) for t in small_res[name]]

    order = ["norm_mix", "w_in", "pool_w", "pool_b", "pool_scale", "attn_sinks", "p_pool", "p_attn", "w_out",
             "norm_mlp", "w_up", "w_down", "norm_final"]
    out = [loss, dx[None]]
    for kind in range(4):
        out += [res[name][kind] for name in order]
    return tuple(out)
```

```python
import functools
import math

import numpy as np
import jax
import jax.numpy as jnp
from jax import lax
from jax.experimental import pallas as pl
from jax.experimental.pallas import tpu as pltpu

F32 = jnp.float32
BF16 = jnp.bfloat16

D_MODEL = 1024
POOL_WIDTH = 512
ATTN_WIDTH = 512
KV_WIDTH = 128
HEAD_DIM = 64
N_HEADS = 8
N_KV_HEADS = 2
GROUP = 4
BLOCK = 128
POOL_WINDOWS = (2, 4, 8, 16)
POOL_GROUP_DIM = 128
POOL_HALO = 16
D_FF = 4096
FF_CHUNK = 1024
IN_WIDTH = 3328
RMS_EPS = 1e-5
NEG_INF = -1e30
ATTN_SCALE = 1.0 / math.sqrt(HEAD_DIM)
N_DEV = 8

ADAM_LR = 0.001
ADAM_B1 = 0.9
ADAM_B2 = 0.999
ADAM_EPS = 1e-08
ADAM_WD = 0.01
ADAM_STEP = 10

LANES = 128
VMEM_LIMIT_BYTES = 56 * 1024 * 1024
MESH = pl.DeviceIdType.MESH


def _params(n_grid_axes=1):
    return pltpu.CompilerParams(
        dimension_semantics=("arbitrary",) * n_grid_axes, vmem_limit_bytes=VMEM_LIMIT_BYTES)


def _dot(a, b):
    return jnp.dot(a, b, preferred_element_type=F32)


def _dot_nt(a, b):
    return lax.dot_general(a, b, (((1,), (1,)), ((), ())), preferred_element_type=F32)


def _dot_tn(a, b):
    return lax.dot_general(a, b, (((0,), (0,)), ((), ())), preferred_element_type=F32)


ANY = pl.BlockSpec(memory_space=pl.ANY)


def _rows(tm, n):
    return pl.BlockSpec((tm, n), lambda i: (i, 0))


def _whole(shape):
    zeros = (0,) * len(shape)
    return pl.BlockSpec(shape, lambda i: zeros)


def _rms_fwd(h, g):
    r = lax.rsqrt(jnp.mean(h * h, axis=-1, keepdims=True) + RMS_EPS)
    xh = h * r
    return r, xh, xh * g


def _rms_bwd(dy, xh, r, g):
    dxh = dy * g
    dh = r * (dxh - xh * jnp.mean(dxh * xh, axis=-1, keepdims=True))
    return dh, jnp.sum(dy * xh, axis=0, keepdims=True)


def _fwd_in(after, x, g_mix, w_in_blocks):
    s_len = x.shape[0]
    tm = min(512, s_len)
    width = IN_WIDTH // N_DEV

    def body(after_ref, x_ref, g_ref, wb_ref, u_ref, zp_ref, q_ref, kv_ref, zg_ref, w_ref):
        @pl.when(pl.program_id(0) == 0)
        def _():
            for j in range(N_DEV):
                w_ref[:, j * width:(j + 1) * width] = wb_ref[j]

        _, _, u = _rms_fwd(x_ref[...], g_ref[...])
        u = u.astype(BF16)
        u_ref[...] = u
        zp_ref[...] = _dot(u, w_ref[:, 0:512]).astype(BF16)
        q_ref[...] = _dot(u, w_ref[:, 512:1024]).astype(BF16)
        kv_ref[...] = _dot(u, w_ref[:, 1024:1280]).astype(BF16)
        zg_ref[...] = _dot(u, w_ref[:, 1280:3328]).astype(BF16)

    return pl.pallas_call(
        body, name="fwd_in", grid=(s_len // tm,),
        in_specs=[ANY, _rows(tm, D_MODEL), _whole((1, D_MODEL)),
                  pl.BlockSpec((N_DEV, D_MODEL, width), lambda i: (0, 0, 0), pipeline_mode=pl.Buffered(1))],
        out_specs=[_rows(tm, D_MODEL), _rows(tm, 512), _rows(tm, 512), _rows(tm, 256), _rows(tm, 2048),
                   _whole((D_MODEL, IN_WIDTH))],
        out_shape=[jax.ShapeDtypeStruct((s_len, n), BF16) for n in (D_MODEL, 512, 512, 256, 2048)]
        + [jax.ShapeDtypeStruct((D_MODEL, IN_WIDTH), BF16)],
        compiler_params=_params(),
    )(after, x, g_mix, w_in_blocks)


def _attn_constants(sinks):
    r = np.arange(BLOCK)[:, None]
    qi = np.arange(BLOCK)[None, :]
    dist = np.where(r <= qi, qi - r, BLOCK + qi - r).astype(np.float32)
    slopes = np.array([2.0 ** (-8.0 * (h + 1) / N_HEADS) for h in range(N_HEADS)], dtype=np.float32)
    bias = (-slopes[:, None, None] * dist[None]).reshape(N_KV_HEADS, GROUP, BLOCK, BLOCK)
    bias = np.ascontiguousarray(bias.transpose(0, 2, 1, 3)).reshape(N_KV_HEADS, BLOCK, GROUP * BLOCK)
    sink_row = jnp.repeat(sinks.astype(F32).reshape(N_KV_HEADS, GROUP), BLOCK, axis=1)[:, None, :]
    return jnp.asarray(bias.astype(np.float32)), sink_row


def _own_block_mask():
    shape = (BLOCK, GROUP * BLOCK)
    r = lax.broadcasted_iota(jnp.int32, shape, 0)
    qi = lax.broadcasted_iota(jnp.int32, shape, 1) & (BLOCK - 1)
    return r <= qi


def _pack_keys(t, own):
    return jnp.where(own, t[BLOCK:], t[:BLOCK])


def _unpack_keys(t, own):
    zero = jnp.zeros_like(t)
    return jnp.concatenate([jnp.where(own, zero, t), jnp.where(own, t, zero)], axis=0)


def _left_half(shape):
    return lax.broadcasted_iota(jnp.int32, shape, 1) < HEAD_DIM


def _dup_halves(slab):
    swapped = pltpu.roll(slab, HEAD_DIM, 1)
    left = _left_half(slab.shape)
    return jnp.where(left, slab, swapped), jnp.where(left, swapped, slab)


def _fill_kv_slabs(kvh_ref, kv_ref, ka_ref, vd_ref):
    for rows, src in ((slice(0, BLOCK), kvh_ref), (slice(BLOCK, None), kv_ref)):
        kvf = src[...].astype(F32)
        for ref, lanes in ((ka_ref, slice(0, KV_WIDTH)), (vd_ref, slice(KV_WIDTH, 2 * KV_WIDTH))):
            d0, d1 = _dup_halves(kvf[:, lanes])
            ref[0, rows, :] = d0.astype(BF16)
            ref[1, rows, :] = d1.astype(BF16)


def _stack_pairs(a, h):
    pieces = []
    for j in range(2):
        pair = a[:, h * 256 + j * LANES:h * 256 + (j + 1) * LANES]
        left = _left_half(pair.shape)
        zero = jnp.zeros_like(pair)
        pieces += [jnp.where(left, pair, zero), jnp.where(left, zero, pair)]
    return jnp.concatenate(pieces, axis=0)


def _attn_probs(kk, q_st, bias_p, sink_row, own, first):
    s = _pack_keys(_dot_nt(kk, q_st), own) * ATTN_SCALE + bias_p
    if first is not None:
        s = jnp.where(jnp.logical_and(first, jnp.logical_not(own)), NEG_INF, s)
    m = jnp.maximum(jnp.max(s, axis=0, keepdims=True), sink_row)
    p = jnp.exp(s - m)
    es = jnp.exp(sink_row - m)
    inv = 1.0 / (jnp.sum(p, axis=0, keepdims=True) + es)
    return p * inv, es * inv


def _pool_d(ext, cur, g, row0):
    w = POOL_WINDOWS[g]
    acc = ext
    k = 1
    while k < w:
        acc = acc + pltpu.roll(acc, k, 0)
        k *= 2
    return _window_mean(acc[POOL_HALO:, :], w, row0) - cur


def _window_mean(total, w, row0):
    t = row0 + lax.broadcasted_iota(jnp.int32, (POOL_HALO, total.shape[1]), 0)
    head = total[:POOL_HALO] / jnp.minimum(t + 1, w).astype(F32)
    return jnp.concatenate([head, total[POOL_HALO:] * (1.0 / w)], axis=0)


def _mixers_fwd(zp, q, kv, pool_w, pool_b, pool_scale, bias_t, sink_row):
    s_len = zp.shape[0]
    tq = min(512, s_len)
    nb = tq // BLOCK

    def body(zp_ref, zph_ref, q_ref, kv_ref, kvh_ref, pw_ref, pb_ref, ps_ref, bias_ref, sink_ref,
             pm_ref, o_ref, ka_ref, vd_ref):
        i = pl.program_id(0)
        cur = zp_ref[...].astype(F32)
        halo = zph_ref[...].astype(F32) * (i > 0).astype(F32)
        ext = jnp.concatenate([halo, cur], axis=0)
        for g in range(4):
            sl = slice(g * POOL_GROUP_DIM, (g + 1) * POOL_GROUP_DIM)
            d = _pool_d(ext[:, sl], cur[:, sl], g, i * tq)
            y = _dot(d.astype(BF16), pw_ref[g]) + pb_ref[:, sl]
            pm_ref[:, sl] = (y * ps_ref[:, sl]).astype(BF16)
        _fill_kv_slabs(kvh_ref, kv_ref, ka_ref, vd_ref)
        own = _own_block_mask()
        for b in range(nb):
            rq = slice(b * BLOCK, (b + 1) * BLOCK)
            rk = slice(b * BLOCK, (b + 2) * BLOCK)
            qb = q_ref[rq, :]
            for h in range(N_KV_HEADS):
                pn, _ = _attn_probs(ka_ref[h, rk, :], _stack_pairs(qb, h), bias_ref[h], sink_ref[h], own,
                                    (i == 0) if b == 0 else None)
                pn = _unpack_keys(pn, own).astype(BF16)
                vd = vd_ref[h, rk, :]
                left = _left_half(vd.shape)
                zero = jnp.zeros_like(vd)
                va, vb = jnp.where(left, vd, zero), jnp.where(left, zero, vd)
                for j in range(2):
                    o_pair = (_dot_tn(pn[:, (2 * j) * BLOCK:(2 * j + 1) * BLOCK], va)
                              + _dot_tn(pn[:, (2 * j + 1) * BLOCK:(2 * j + 2) * BLOCK], vb))
                    o_ref[rq, h * 256 + j * LANES:h * 256 + (j + 1) * LANES] = o_pair.astype(BF16)

    halo_pool = pl.BlockSpec((POOL_HALO, 512), lambda i: (jnp.maximum(i * (tq // POOL_HALO) - 1, 0), 0))
    halo_kv = pl.BlockSpec((BLOCK, 256), lambda i: (jnp.maximum(i * nb - 1, 0), 0))
    return pl.pallas_call(
        body, name="mixers_fwd", grid=(s_len // tq,),
        in_specs=[_rows(tq, 512), halo_pool, _rows(tq, 512), _rows(tq, 256), halo_kv,
                  _whole((4, 128, 128)), _whole((1, 512)), _whole((1, 512)),
                  _whole((N_KV_HEADS, BLOCK, GROUP * BLOCK)), _whole((N_KV_HEADS, 1, GROUP * BLOCK))],
        out_specs=[_rows(tq, 512), _rows(tq, 512)],
        out_shape=[jax.ShapeDtypeStruct((s_len, 512), BF16)] * 2,
        scratch_shapes=[pltpu.VMEM((N_KV_HEADS, tq + BLOCK, LANES), BF16)] * 2,
        compiler_params=_params(),
    )(zp, zp, q, kv, kv, pool_w, pool_b, pool_scale, bias_t, sink_row)


def _gated_mix(pm, o, zg, pp_ref, pa_ref):
    yp = _dot(pm, pp_ref[...])
    ya = _dot(o, pa_ref[...])
    gp = jax.nn.sigmoid(zg[:, :D_MODEL].astype(F32))
    ga = jax.nn.sigmoid(zg[:, D_MODEL:].astype(F32))
    return yp, ya, gp, ga


def _core(x, pm, o, zg, tgt, g_mlp, g_fin, p_pool, p_attn, w_out, w_up_blocks, w_down):
    s_len = x.shape[0]
    tm = min(256, s_len)
    n_chunks = D_FF // FF_CHUNK
    up_block = D_FF // N_DEV
    per_chunk = FF_CHUNK // up_block

    def body(x_ref, pm_ref, o_ref, zg_ref, tgt_ref, gm_ref, gf_ref, pp_ref, pa_ref, wo_ref, wu_ref, wd_ref,
             mixed_ref, dh1_ref, a_ref, dap_ref, u2_ref, dh2_ref, small_ref,
             dyp_ref, dya_ref, dzg_ref, dpm_ref, do_ref):
        i = pl.program_id(0)

        @pl.when(i == 0)
        def _():
            small_ref[...] = jnp.zeros_like(small_ref)

        yp, ya, gp, ga = _gated_mix(pm_ref[...], o_ref[...], zg_ref[...], pp_ref, pa_ref)
        mixed = (gp * yp + ga * ya).astype(BF16)
        mixed_ref[...] = mixed
        h1 = x_ref[...] + _dot(mixed, wo_ref[...])
        r2, xh2, u2 = _rms_fwd(h1, gm_ref[...])
        u2 = u2.astype(BF16)
        u2_ref[...] = u2
        acc = jnp.zeros((tm, D_MODEL), F32)
        for c in range(n_chunks):
            cs = slice(c * FF_CHUNK, (c + 1) * FF_CHUNK)
            a = jnp.concatenate([_dot(u2, wu_ref[per_chunk * c + j]) for j in range(per_chunk)], axis=1)
            a = jnp.maximum(a, 0.0)
            a_ref[:, cs] = a.astype(BF16)
            acc = acc + _dot((a * a).astype(BF16), wd_ref[cs, :])
        h2 = h1 + acc
        r3, xh3, y = _rms_fwd(h2, gf_ref[...])
        diff = y - tgt_ref[...]
        small_ref[2:3, :] += 0.5 * jnp.sum(jnp.mean(diff * diff, axis=-1, keepdims=True))
        dy = diff * (1.0 / D_MODEL)
        dh2, dgf = _rms_bwd(dy, xh3, r3, gf_ref[...])
        small_ref[1:2, :] += dgf
        dh2_bf = dh2.astype(BF16)
        dh2_ref[...] = dh2_bf
        du2 = jnp.zeros((tm, D_MODEL), F32)
        for c in range(n_chunks):
            cs = slice(c * FF_CHUNK, (c + 1) * FF_CHUNK)
            ds = _dot_nt(dh2_bf, wd_ref[cs, :])
            dap = (ds * (2.0 * a_ref[:, cs].astype(F32))).astype(BF16)
            dap_ref[:, cs] = dap
            for j in range(per_chunk):
                du2 = du2 + _dot_nt(dap[:, j * up_block:(j + 1) * up_block], wu_ref[per_chunk * c + j])
        dh1n, dgm = _rms_bwd(du2, xh2, r2, gm_ref[...])
        small_ref[0:1, :] += dgm
        dh1 = dh2 + dh1n
        dh1_ref[...] = dh1
        dm = _dot_nt(dh1.astype(BF16), wo_ref[...])
        dyp = (dm * gp).astype(BF16)
        dya = (dm * ga).astype(BF16)
        dyp_ref[...] = dyp
        dya_ref[...] = dya
        dzg_ref[:, :D_MODEL] = (dm * yp * (gp * (1.0 - gp))).astype(BF16)
        dzg_ref[:, D_MODEL:] = (dm * ya * (ga * (1.0 - ga))).astype(BF16)
        dpm_ref[...] = _dot_nt(dyp, pp_ref[...]).astype(BF16)
        do_ref[...] = _dot_nt(dya, pa_ref[...]).astype(BF16)

    def fixed(shape):
        return pl.BlockSpec(shape, lambda i: (0,) * len(shape), pipeline_mode=pl.Buffered(1))

    widths_dtypes = ((D_MODEL, BF16), (D_MODEL, F32), (D_FF, BF16), (D_FF, BF16), (D_MODEL, BF16), (D_MODEL, BF16))
    back = ((D_MODEL, BF16), (D_MODEL, BF16), (2048, BF16), (512, BF16), (512, BF16))
    return pl.pallas_call(
        body, name="core", grid=(s_len // tm,),
        in_specs=[_rows(tm, D_MODEL), _rows(tm, 512), _rows(tm, 512), _rows(tm, 2048), _rows(tm, D_MODEL),
                  _whole((1, D_MODEL)), _whole((1, D_MODEL)),
                  fixed((512, D_MODEL)), fixed((512, D_MODEL)), fixed((D_MODEL, D_MODEL)),
                  fixed((N_DEV, D_MODEL, up_block)), fixed((D_FF, D_MODEL))],
        out_specs=[_rows(tm, n) for n, _ in widths_dtypes] + [_whole((8, D_MODEL))] + [_rows(tm, n) for n, _ in back],
        out_shape=[jax.ShapeDtypeStruct((s_len, n), d) for n, d in widths_dtypes]
        + [jax.ShapeDtypeStruct((8, D_MODEL), F32)] + [jax.ShapeDtypeStruct((s_len, n), d) for n, d in back],
        compiler_params=_params(),
    )(x, pm, o, zg, tgt, g_mlp, g_fin, p_pool, p_attn, w_out, w_up_blocks, w_down)


def _tn_matmul(a, b, square_a=False, col_blocks=None, after=None):
    s_len, ka = a.shape
    nb = b.shape[1]
    tt = min(2048, s_len)
    tk = min(1024, ka)
    tn = min(1024, nb)
    n_t = s_len // tt
    if col_blocks is None:
        out_spec = pl.BlockSpec((tk, tn), lambda k, j, t: (k, j))
        out_shape = jax.ShapeDtypeStruct((ka, nb), BF16)
    else:
        width = nb // col_blocks
        per_tile = tn // width
        out_spec = pl.BlockSpec((per_tile, tk, width), lambda k, j, t: (j, k, 0))
        out_shape = jax.ShapeDtypeStruct((col_blocks, ka, width), BF16)

    extra = [] if after is None else [after]

    def body(a_ref, b_ref, *rest):
        o_ref, acc_ref = rest[len(extra):]
        t = pl.program_id(2)

        @pl.when(t == 0)
        def _():
            acc_ref[...] = jnp.zeros_like(acc_ref)

        av = a_ref[...]
        if square_a:
            av = av * av
        acc_ref[...] += _dot_tn(av.astype(BF16), b_ref[...].astype(BF16))

        @pl.when(t == n_t - 1)
        def _():
            if col_blocks is None:
                o_ref[...] = acc_ref[...].astype(o_ref.dtype)
            else:
                for blk in range(per_tile):
                    o_ref[blk] = acc_ref[:, blk * width:(blk + 1) * width].astype(o_ref.dtype)

    return pl.pallas_call(
        body, name="tn_matmul", grid=(ka // tk, nb // tn, n_t),
        in_specs=[pl.BlockSpec((tt, tk), lambda k, j, t: (t, k)), pl.BlockSpec((tt, tn), lambda k, j, t: (t, j))]
        + [ANY] * len(extra),
        out_specs=out_spec, out_shape=out_shape,
        scratch_shapes=[pltpu.VMEM((tk, tn), F32)],
        compiler_params=_params(3),
    )(a, b, *extra)


def _tn_w_in(u, dzp, dq, dkv, dzg):
    s_len = u.shape[0]
    tt = min(1024, s_len)
    n_t = s_len // tt
    width = IN_WIDTH // N_DEV
    pieces = ((0, 512), (512, 1024), (1024, 1280), (1280, IN_WIDTH))

    def body(u_ref, dzp_ref, dq_ref, dkv_ref, dzg_ref, o_ref, acc_ref):
        t = pl.program_id(0)

        @pl.when(t == 0)
        def _():
            acc_ref[...] = jnp.zeros_like(acc_ref)

        uv = u_ref[...]
        for (c0, c1), ref in zip(pieces, (dzp_ref, dq_ref, dkv_ref, dzg_ref)):
            acc_ref[:, c0:c1] += _dot_tn(uv, ref[...])

        @pl.when(t == n_t - 1)
        def _():
            for j in range(N_DEV):
                o_ref[j] = acc_ref[:, j * width:(j + 1) * width].astype(BF16)

    return pl.pallas_call(
        body, name="tn_w_in", grid=(n_t,),
        in_specs=[_rows(tt, D_MODEL)] + [_rows(tt, c1 - c0) for c0, c1 in pieces],
        out_specs=_whole((N_DEV, D_MODEL, width)),
        out_shape=jax.ShapeDtypeStruct((N_DEV, D_MODEL, width), BF16),
        scratch_shapes=[pltpu.VMEM((D_MODEL, IN_WIDTH), F32)],
        compiler_params=_params(),
    )(u, dzp, dq, dkv, dzg)


MIX_POOL_B = 4 * POOL_GROUP_DIM
MIX_SINKS = MIX_POOL_B + 8
MIX_ROWS = MIX_SINKS + 8


def _mixers_bwd(after, zp, q, kv, dpm, do, pool_w, pool_b, pool_scale, bias_t, sink_row):
    s_len = zp.shape[0]
    tq = min(512, s_len)
    nb = tq // BLOCK
    n_steps = s_len // tq

    def body(after_ref, zp_ref, zph_ref, q_ref, kv_ref, kvh_ref, dpm_ref, dpmh_ref, do_ref, pw_ref, pb_ref, ps_ref,
             bias_ref, sink_ref, dzp_ref, dq_ref, dkv_ref, small_ref, dps_ref,
             ka_ref, vd_ref, dsk_acc, dkv_acc):
        i = pl.program_id(0)

        @pl.when(i == 0)
        def _():
            dkv_acc[...] = jnp.zeros_like(dkv_acc)
            small_ref[...] = jnp.zeros_like(small_ref)
            dps_ref[...] = jnp.zeros_like(dps_ref)
            dsk_acc[...] = jnp.zeros_like(dsk_acc)

        cur = zp_ref[...].astype(F32)
        halo = zph_ref[...].astype(F32) * (i > 0).astype(F32)
        ext = jnp.concatenate([halo, cur], axis=0)
        dpm_next = dpmh_ref[...].astype(F32) * (i < n_steps - 1).astype(F32)
        dpm_ext = jnp.concatenate([dpm_ref[...].astype(F32), dpm_next], axis=0)
        n_ext = tq + POOL_HALO
        for g in range(4):
            sl = slice(g * POOL_GROUP_DIM, (g + 1) * POOL_GROUP_DIM)
            w = POOL_WINDOWS[g]
            d = _pool_d(ext[:, sl], cur[:, sl], g, i * tq).astype(BF16)
            y_lin = _dot(d, pw_ref[g]) + pb_ref[:, sl]
            dps_ref[:, sl] += jnp.sum(dpm_ext[:tq, sl] * y_lin, axis=0, keepdims=True)
            dyl_ext = dpm_ext[:, sl] * ps_ref[:, sl]
            small_ref[MIX_POOL_B + g:MIX_POOL_B + g + 1, :] += jnp.sum(dyl_ext[:tq], axis=0, keepdims=True)
            dyl_bf = dyl_ext.astype(BF16)
            small_ref[g * POOL_GROUP_DIM:(g + 1) * POOL_GROUP_DIM, :] += _dot_tn(d, dyl_bf[:tq])
            dd = _dot_nt(dyl_bf, pw_ref[g])
            e = _window_mean(dd, w, i * tq)
            acc = e
            k = 1
            while k < w:
                acc = acc + pltpu.roll(acc, n_ext - k, 0)
                k *= 2
            dzp_ref[:, sl] = (acc[:tq] - dd[:tq]).astype(BF16)

        _fill_kv_slabs(kvh_ref, kv_ref, ka_ref, vd_ref)

        def fold(dup):
            return dup + pltpu.roll(dup, HEAD_DIM, 1)

        own = _own_block_mask()
        for b in range(nb):
            rq = slice(b * BLOCK, (b + 1) * BLOCK)
            rk = slice(b * BLOCK, (b + 2) * BLOCK)
            qb = q_ref[rq, :]
            dob = do_ref[rq, :]
            dk_dup, dv_dup = [], []
            for h in range(N_KV_HEADS):
                kk = ka_ref[h, rk, :]
                q_st = _stack_pairs(qb, h)
                do_st = _stack_pairs(dob, h)
                pn, psink = _attn_probs(kk, q_st, bias_ref[h], sink_ref[h], own, (i == 0) if b == 0 else None)
                dp = _pack_keys(_dot_nt(vd_ref[h, rk, :], do_st), own)
                delta = jnp.sum(pn * dp, axis=0, keepdims=True)
                dsk_acc[h] += -psink * delta
                ds = _unpack_keys((pn * (dp - delta)) * ATTN_SCALE, own).astype(BF16)
                pn = _unpack_keys(pn, own)
                dq_st = _dot_tn(ds, kk)
                for j in range(2):
                    left = _left_half((BLOCK, LANES))
                    dq_pair = jnp.where(left, dq_st[(2 * j) * BLOCK:(2 * j + 1) * BLOCK],
                                        dq_st[(2 * j + 1) * BLOCK:(2 * j + 2) * BLOCK])
                    dq_ref[rq, h * 256 + j * LANES:h * 256 + (j + 1) * LANES] = dq_pair.astype(BF16)
                dk_dup.append(fold(_dot(ds, q_st)))
                dv_dup.append(fold(_dot(pn.astype(BF16), do_st)))
            left = _left_half((2 * BLOCK, LANES))
            dkv_blk = jnp.concatenate([jnp.where(left, dk_dup[0], dk_dup[1]),
                                       jnp.where(left, dv_dup[0], dv_dup[1])], axis=1)
            g0 = pl.multiple_of(i * tq + b * BLOCK, BLOCK)
            dkv_acc[pl.ds(g0, 2 * BLOCK), :] += dkv_blk

        @pl.when(i == n_steps - 1)
        def _():
            dkv_ref[...] = dkv_acc[BLOCK:, :].astype(BF16)
            lane = lax.broadcasted_iota(jnp.int32, (1, LANES), 1)
            row = jnp.zeros((1, LANES), F32)
            for h in range(N_KV_HEADS):
                for g in range(GROUP):
                    tot = jnp.sum(dsk_acc[h, :, g * BLOCK:(g + 1) * BLOCK], axis=1, keepdims=True)
                    row = jnp.where(lane == GROUP * h + g, tot, row)
            small_ref[MIX_SINKS:MIX_SINKS + 1, :] = row

    blocks_per_tile = tq // POOL_HALO
    last_halo = s_len // POOL_HALO - 1
    halo_prev = pl.BlockSpec((POOL_HALO, 512), lambda i: (jnp.maximum(i * blocks_per_tile - 1, 0), 0))
    halo_next = pl.BlockSpec((POOL_HALO, 512), lambda i: (jnp.minimum((i + 1) * blocks_per_tile, last_halo), 0))
    halo_kv = pl.BlockSpec((BLOCK, 256), lambda i: (jnp.maximum(i * nb - 1, 0), 0))
    return pl.pallas_call(
        body, name="mixers_bwd", grid=(n_steps,),
        in_specs=[ANY, _rows(tq, 512), halo_prev, _rows(tq, 512), _rows(tq, 256), halo_kv,
                  _rows(tq, 512), halo_next, _rows(tq, 512),
                  _whole((4, 128, 128)), _whole((1, 512)), _whole((1, 512)),
                  _whole((N_KV_HEADS, BLOCK, GROUP * BLOCK)), _whole((N_KV_HEADS, 1, GROUP * BLOCK))],
        out_specs=[_rows(tq, 512), _rows(tq, 512), _whole((s_len, 256)),
                   _whole((MIX_ROWS, LANES)), _whole((1, 512))],
        out_shape=[jax.ShapeDtypeStruct((s_len, 512), BF16), jax.ShapeDtypeStruct((s_len, 512), BF16),
                   jax.ShapeDtypeStruct((s_len, 256), BF16), jax.ShapeDtypeStruct((MIX_ROWS, LANES), F32),
                   jax.ShapeDtypeStruct((1, 512), F32)],
        scratch_shapes=[pltpu.VMEM((N_KV_HEADS, tq + BLOCK, LANES), BF16)] * 2
        + [pltpu.VMEM((N_KV_HEADS, 1, GROUP * BLOCK), F32), pltpu.VMEM((s_len + BLOCK, 256), F32)],
        compiler_params=_params(),
    )(after, zp, zp, q, kv, kv, dpm, dpm, do, pool_w, pool_b, pool_scale, bias_t, sink_row)


def _in_bwd(after, dzp, dq, dkv, dzg, w_in, x, dh1, g_mix):
    s_len = x.shape[0]
    tm = min(512, s_len)

    def body(after_ref, dzp_ref, dq_ref, dkv_ref, dzg_ref, w_ref, x_ref, dh1_ref, g_ref, dx_ref, dg_ref):
        i = pl.program_id(0)

        @pl.when(i == 0)
        def _():
            dg_ref[...] = jnp.zeros_like(dg_ref)

        du = _dot_nt(dzp_ref[...], w_ref[:, 0:512])
        du = du + _dot_nt(dq_ref[...], w_ref[:, 512:1024])
        du = du + _dot_nt(dkv_ref[...], w_ref[:, 1024:1280])
        du = du + _dot_nt(dzg_ref[...], w_ref[:, 1280:3328])
        r, xh, _ = _rms_fwd(x_ref[...], g_ref[...])
        dxn, dg = _rms_bwd(du, xh, r, g_ref[...])
        dg_ref[...] += dg
        dx_ref[...] = dh1_ref[...] + dxn

    return pl.pallas_call(
        body, name="in_bwd", grid=(s_len // tm,),
        in_specs=[ANY, _rows(tm, 512), _rows(tm, 512), _rows(tm, 256), _rows(tm, 2048), _whole((D_MODEL, IN_WIDTH)),
                  _rows(tm, D_MODEL), _rows(tm, D_MODEL), _whole((1, D_MODEL))],
        out_specs=[_rows(tm, D_MODEL), _whole((1, D_MODEL))],
        out_shape=[jax.ShapeDtypeStruct((s_len, D_MODEL), F32), jax.ShapeDtypeStruct((1, D_MODEL), F32)],
        compiler_params=_params(),
    )(after, dzp, dq, dkv, dzg, w_in, x, dh1, g_mix)


def _all_gather_weights(name, shards, after=None):
    n = len(shards)
    extra = [] if after is None else [after]
    n_extra = len(extra)

    def body(*refs):
        ins, outs = refs[:n], refs[n + n_extra:2 * n + n_extra]
        send_sems, recv_sems = refs[2 * n + n_extra:]
        x, y, c = lax.axis_index("x"), lax.axis_index("y"), lax.axis_index("c")
        me, sibling = (x, y, c), (x, y, 1 - c)
        chips = [(1 - x, y), (x, 1 - y), (1 - x, 1 - y)]

        def slot(a, px, py, pc):
            return outs[a].at[4 * px + 2 * py + pc]

        def copy(a, k, block, to, src=None):
            return pltpu.make_async_remote_copy(
                src_ref=slot(a, *block) if src is None else src, dst_ref=slot(a, *block),
                send_sem=send_sems.at[a, k], recv_sem=recv_sems.at[a, k], device_id=to, device_id_type=MESH)

        first = []
        for a in range(n):
            first.append(copy(a, 0, me, sibling, src=ins[a]))
            first += [copy(a, 1 + j, me, (*chip, c), src=ins[a]) for j, chip in enumerate(chips)]
        for cp in first:
            cp.start()
        passed = []
        for a in range(n):
            for j, chip in enumerate(chips):
                copy(a, 1 + j, (*chip, c), me).wait_recv()
                cp = copy(a, 4 + j, (*chip, c), sibling)
                cp.start()
                passed.append(cp)
        for a in range(n):
            copy(a, 0, sibling, me).wait_recv()
            for j, chip in enumerate(chips):
                copy(a, 4 + j, (*chip, 1 - c), me).wait_recv()
        for cp in first + passed:
            cp.wait_send()

    gathered = pl.pallas_call(
        body, name=name,
        in_specs=[ANY] * (n + n_extra), out_specs=[ANY] * n,
        out_shape=[jax.ShapeDtypeStruct((N_DEV,) + s.shape, s.dtype) for s in shards],
        scratch_shapes=[pltpu.SemaphoreType.DMA((n, 7)), pltpu.SemaphoreType.DMA((n, 7))],
    )(*shards, *extra)
    me_idx = _my_index()
    return [lax.dynamic_update_index_in_dim(g, s, me_idx, 0) for g, s in zip(gathered, shards)]


HBM_SPEC = pl.BlockSpec(memory_space=pltpu.HBM)
SEM_SPEC = pl.BlockSpec(memory_space=pltpu.SEMAPHORE)
DATAFLOW = pltpu.SideEffectType.DATAFLOW_SIDE_EFFECTING
N_PEERS = N_DEV - 1


CHIP_PEERS = (1, 2, 4, 6)
RELAYED = (2, 4, 6)


def _peer_copies(srcs, lands, scatter, send_sems, recv_sems):
    x, y, c = lax.axis_index("x"), lax.axis_index("y"), lax.axis_index("c")
    me_idx = 4 * x + 2 * y + c
    copies = []
    for k in range(1, N_DEV):
        px = 1 - x if (k >> 2) & 1 else x
        py = 1 - y if (k >> 1) & 1 else y
        pc = 1 - c if k & 1 else c
        p_idx = 4 * px + 2 * py + pc
        for a in range(len(srcs)):
            if scatter[a] == "chip" and k not in CHIP_PEERS:
                continue
            src = srcs[a].at[p_idx] if scatter[a] is True else srcs[a]
            dst = lands[a].at[k] if scatter[a] is True else lands[a].at[me_idx]
            copies.append(pltpu.make_async_remote_copy(
                src_ref=src, dst_ref=dst, send_sem=send_sems.at[a * N_PEERS + k - 1],
                recv_sem=recv_sems.at[a * N_PEERS + k - 1],
                device_id=(px, py, pc), device_id_type=MESH))
    return copies


def _exchange_start(name, srcs, scatter, after):
    n = len(srcs)
    lands = [lax.empty(s.shape if sc is True else (N_DEV,) + s.shape, s.dtype) for s, sc in zip(srcs, scatter)]

    def body(*refs):
        src_refs, land_refs = refs[:n], refs[n:2 * n]
        send_sems, recv_sems = refs[2 * n + 1], refs[2 * n + 2]
        token = refs[4 * n + 3]
        for cp in _peer_copies(src_refs, land_refs, scatter, send_sems, recv_sems):
            cp.start()
        token[...] = jnp.zeros_like(token)

    hbm = lambda t: pltpu.HBM(t.shape, t.dtype)
    outs = pl.pallas_call(
        body, name=name,
        out_shape=[pltpu.SemaphoreType.DMA((n * N_PEERS,)), pltpu.SemaphoreType.DMA((n * N_PEERS,))]
        + [hbm(t) for t in srcs] + [hbm(t) for t in lands] + [jax.ShapeDtypeStruct((8, LANES), F32)],
        in_specs=[HBM_SPEC] * (2 * n) + [ANY],
        out_specs=[SEM_SPEC, SEM_SPEC] + [HBM_SPEC] * (2 * n) + [pl.BlockSpec(memory_space=pltpu.VMEM)],
        input_output_aliases={i: 2 + i for i in range(2 * n)},
        compiler_params=pltpu.CompilerParams(has_side_effects=DATAFLOW),
    )(*[pltpu.with_memory_space_constraint(t, pltpu.HBM) for t in list(srcs) + lands], after)
    return dict(n=n, scatter=scatter, send_sems=outs[0], recv_sems=outs[1], srcs=outs[2:2 + n],
                lands=outs[2 + n:2 + 2 * n], token=outs[2 + 2 * n])


def _exchange_wait(name, handle, after):
    n, scatter = handle["n"], handle["scatter"]

    def body(*refs):
        src_refs, land_refs = refs[:n], refs[n:2 * n]
        send_sems, recv_sems = refs[2 * n], refs[2 * n + 1]
        for cp in _peer_copies(src_refs, land_refs, scatter, send_sems, recv_sems):
            cp.wait_send()
            cp.wait_recv()

    both = list(handle["srcs"]) + list(handle["lands"])
    outs = pl.pallas_call(
        body, name=name,
        out_shape=[pltpu.HBM(t.shape, t.dtype) for t in both],
        in_specs=[HBM_SPEC] * (2 * n) + [SEM_SPEC, SEM_SPEC, ANY],
        out_specs=[HBM_SPEC] * (2 * n),
        input_output_aliases={i: i for i in range(2 * n)},
        compiler_params=pltpu.CompilerParams(has_side_effects=DATAFLOW),
    )(*both, handle["send_sems"], handle["recv_sems"], after)
    me_idx = _my_index()
    lands = [land if sc is True else lax.dynamic_update_index_in_dim(land, src, me_idx, 0)
             for land, src, sc in zip(outs[n:], outs[:n], scatter)]
    return lands, outs[:n]


def _my_index():
    return 4 * lax.axis_index("x") + 2 * lax.axis_index("y") + lax.axis_index("c")


def _relay_copies(bufs, send_sems, recv_sems):
    x, y, c = lax.axis_index("x"), lax.axis_index("y"), lax.axis_index("c")
    copies = []
    for j, k in enumerate(RELAYED):
        px = 1 - x if (k >> 2) & 1 else x
        py = 1 - y if (k >> 1) & 1 else y
        slot = 4 * px + 2 * py + c
        for a, buf in enumerate(bufs):
            copies.append(pltpu.make_async_remote_copy(
                src_ref=buf.at[slot], dst_ref=buf.at[slot], send_sem=send_sems.at[a * len(RELAYED) + j],
                recv_sem=recv_sems.at[a * len(RELAYED) + j], device_id=(x, y, 1 - c), device_id_type=MESH))
    return copies


def _relay_start(name, bufs, after):
    n = len(bufs)

    def body(*refs):
        send_sems, recv_sems = refs[n + 1], refs[n + 2]
        for cp in _relay_copies(refs[:n], send_sems, recv_sems):
            cp.start()
        token = refs[2 * n + 3]
        token[...] = jnp.zeros_like(token)

    n_sems = n * len(RELAYED)
    outs = pl.pallas_call(
        body, name=name,
        out_shape=[pltpu.SemaphoreType.DMA((n_sems,)), pltpu.SemaphoreType.DMA((n_sems,))]
        + [pltpu.HBM(t.shape, t.dtype) for t in bufs] + [jax.ShapeDtypeStruct((8, LANES), F32)],
        in_specs=[HBM_SPEC] * n + [ANY],
        out_specs=[SEM_SPEC, SEM_SPEC] + [HBM_SPEC] * n + [pl.BlockSpec(memory_space=pltpu.VMEM)],
        input_output_aliases={i: 2 + i for i in range(n)},
        compiler_params=pltpu.CompilerParams(has_side_effects=DATAFLOW),
    )(*[pltpu.with_memory_space_constraint(t, pltpu.HBM) for t in bufs], after)
    return dict(n=n, send_sems=outs[0], recv_sems=outs[1], bufs=outs[2:2 + n], token=outs[2 + n])


def _relay_wait(name, handle, after):
    n = handle["n"]

    def body(*refs):
        for cp in _relay_copies(refs[:n], refs[n], refs[n + 1]):
            cp.wait_send()
            cp.wait_recv()

    return pl.pallas_call(
        body, name=name,
        out_shape=[pltpu.HBM(t.shape, t.dtype) for t in handle["bufs"]],
        in_specs=[HBM_SPEC] * n + [SEM_SPEC, SEM_SPEC, ANY],
        out_specs=[HBM_SPEC] * n,
        input_output_aliases={i: i for i in range(n)},
        compiler_params=pltpu.CompilerParams(has_side_effects=DATAFLOW),
    )(*handle["bufs"], handle["send_sems"], handle["recv_sems"], after)


def _adamw(parts, w, m, v, sent=None):
    r, c = w.shape
    tr = 256 if r % 256 == 0 else r
    own = sent is not None

    def body(*refs):
        if own:
            _, p_ref, own_ref, w_ref, m_ref, v_ref, g_ref, d_ref, nm_ref, nv_ref = refs
            g = own_ref[...].astype(F32)
        else:
            p_ref, w_ref, m_ref, v_ref, g_ref, d_ref, nm_ref, nv_ref = refs
            g = p_ref[0].astype(F32)
        for k in range(1, N_DEV):
            g = g + p_ref[k].astype(F32)
        m_new = ADAM_B1 * m_ref[...] + (1.0 - ADAM_B1) * g
        v_new = ADAM_B2 * v_ref[...] + (1.0 - ADAM_B2) * (g * g)
        m_hat = m_new / (1.0 - ADAM_B1 ** ADAM_STEP)
        v_hat = v_new / (1.0 - ADAM_B2 ** ADAM_STEP)
        g_ref[...] = g
        d_ref[...] = -ADAM_LR * (m_hat / (jnp.sqrt(v_hat) + ADAM_EPS) + ADAM_WD * w_ref[...])
        nm_ref[...] = m_new
        nv_ref[...] = v_new

    out_shape = [jax.ShapeDtypeStruct((r, c), F32)] * 4
    if not own:
        return pl.pallas_call(
            body, name="adamw", grid=(r // tr,),
            in_specs=[pl.BlockSpec((N_DEV, tr, c), lambda i: (0, i, 0))] + [_rows(tr, c)] * 3,
            out_specs=[_rows(tr, c)] * 4, out_shape=out_shape, compiler_params=_params(),
        )(parts, w, m, v)
    rows = pl.BlockSpec((tr, c), lambda i, me: (i, 0))
    return pl.pallas_call(
        body, name="adamw_own", out_shape=out_shape, compiler_params=_params(),
        grid_spec=pltpu.PrefetchScalarGridSpec(
            num_scalar_prefetch=1, grid=(r // tr,),
            in_specs=[pl.BlockSpec((N_DEV, tr, c), lambda i, me: (0, i, 0)),
                      pl.BlockSpec((None, tr, c), lambda i, me: (me[0], i, 0))] + [rows] * 3,
            out_specs=[rows] * 4),
    )(_my_index().reshape(1).astype(jnp.int32), parts, sent, w, m, v)


def _adam_step(g, w, m, v):
    m_new = ADAM_B1 * m + (1.0 - ADAM_B1) * g
    v_new = ADAM_B2 * v + (1.0 - ADAM_B2) * (g * g)
    m_hat = m_new / (1.0 - ADAM_B1 ** ADAM_STEP)
    v_hat = v_new / (1.0 - ADAM_B2 ** ADAM_STEP)
    return -ADAM_LR * (m_hat / (jnp.sqrt(v_hat) + ADAM_EPS) + ADAM_WD * w), m_new, v_new


SMALL_NAMES = ("norm_mix", "pool_w", "pool_b", "pool_scale", "attn_sinks", "norm_mlp", "norm_final")


def _adamw_small(mlp_all, mix_all, scale_all, nmix_all, w, m, v):
    def body(mlp_ref, mix_ref, scale_ref, nmix_ref, *refs):
        ins, outs = refs[:21], refs[21:]

        def total(ref, rows, lanes=slice(None)):
            g = ref[0, rows, lanes]
            for k in range(1, N_DEV):
                g = g + ref[k, rows, lanes]
            return g

        grads = dict(
            norm_mix=total(nmix_ref, slice(0, 1)), pool_w=total(mix_ref, slice(0, MIX_POOL_B)),
            pool_b=total(mix_ref, slice(MIX_POOL_B, MIX_POOL_B + 4)), pool_scale=total(scale_ref, slice(0, 1)),
            attn_sinks=total(mix_ref, slice(MIX_SINKS, MIX_SINKS + 1)),
            norm_mlp=total(mlp_ref, slice(0, 1)), norm_final=total(mlp_ref, slice(1, 2)))
        for i, name in enumerate(SMALL_NAMES):
            g = grads[name]
            d, m_new, v_new = _adam_step(g, ins[3 * i][...], ins[3 * i + 1][...], ins[3 * i + 2][...])
            for ref, val in zip(outs[4 * i:4 * i + 4], (g, d, m_new, v_new)):
                ref[...] = val
        outs[28][...] = jnp.broadcast_to(total(mlp_ref, slice(2, 3), slice(0, LANES)), (8, LANES))

    operands, out_shape = [], []
    for name in SMALL_NAMES:
        operands += [w[name], m[name], v[name]]
        out_shape += [jax.ShapeDtypeStruct(w[name].shape, F32)] * 4
    out_shape.append(jax.ShapeDtypeStruct((8, LANES), F32))
    outs = pl.pallas_call(body, name="adamw_small", out_shape=out_shape)(
        mlp_all, mix_all, scale_all, nmix_all, *operands)
    return {name: outs[4 * i:4 * i + 4] for i, name in enumerate(SMALL_NAMES)}, outs[28]


def kernel(x, norm_mix, w_in, pool_w, pool_b, pool_scale, attn_sinks, p_pool, p_attn, w_out, norm_mlp, w_up, w_down, norm_final, loss_target, m_norm_mix, m_w_in, m_pool_w, m_pool_b, m_pool_scale, m_attn_sinks, m_p_pool, m_p_attn, m_w_out, m_norm_mlp, m_w_up, m_w_down, m_norm_final, v_norm_mix, v_w_in, v_pool_w, v_pool_b, v_pool_scale, v_attn_sinks, v_p_pool, v_p_attn, v_w_out, v_norm_mlp, v_w_up, v_w_down, v_norm_final):
    xs = x[0]
    tgt = loss_target[0]
    s_len = xs.shape[0]

    w_in_bf, p_pool_bf, p_attn_bf, w_out_bf, w_up_bf, w_down_bf = [
        t[0].astype(BF16) for t in (w_in, p_pool, p_attn, w_out, w_up, w_down)]
    (w_in_g,) = _all_gather_weights("all_gather_w_in", [w_in_bf])
    ag_rest = _exchange_start(
        "ag_rest_start", [p_pool_bf, p_attn_bf, w_out_bf, w_up_bf, w_down_bf], ("chip",) * 5, w_in_g)

    pool_w_bf = pool_w[0].astype(BF16)
    pool_b_row = pool_b[0].reshape(1, POOL_WIDTH)
    bias_t, sink_row = _attn_constants(attn_sinks[0])

    u, zp, q, kv, zg, w_in_f = _fwd_in(ag_rest["token"], xs, norm_mix, w_in_g)
    pm, o = _mixers_fwd(zp, q, kv, pool_w_bf, pool_b_row, pool_scale, bias_t, sink_row)
    first_level, _ = _exchange_wait("ag_rest_wait", ag_rest, o)
    relay = _relay_start("ag_relay_start", first_level, pm)
    p_pool_g, p_attn_g, w_out_g, w_up_g, w_down_g = _relay_wait("ag_relay_wait", relay, relay["token"])
    p_pool_f = p_pool_g.transpose(1, 0, 2).reshape(POOL_WIDTH, D_MODEL)
    p_attn_f = p_attn_g.transpose(1, 0, 2).reshape(ATTN_WIDTH, D_MODEL)
    w_out_f = w_out_g.reshape(D_MODEL, D_MODEL)
    w_down_f = w_down_g.reshape(D_FF, D_MODEL)
    mixed, dh1, a, dapre, u2, dh2, small_mlp, dyp, dya, dzg, dpm, do = _core(
        xs, pm, o, zg, tgt, norm_mlp, norm_final.reshape(1, D_MODEL), p_pool_f, p_attn_f, w_out_f, w_up_g, w_down_f)
    gw_down = _tn_matmul(a, dh2, square_a=True)
    gw_up = _tn_matmul(u2, dapre, col_blocks=N_DEV)
    ex_mlp = _exchange_start(
        "ex_mlp_start", [gw_up, gw_down.reshape(N_DEV, D_FF // N_DEV, D_MODEL)], (True, True), small_mlp)
    gw_out = _tn_matmul(mixed, dh1, after=ex_mlp["token"])
    gp_pool = _tn_matmul(pm, dyp, col_blocks=N_DEV)
    gp_attn = _tn_matmul(o, dya, col_blocks=N_DEV)
    ex_proj = _exchange_start(
        "ex_proj_start", [gp_pool, gp_attn, gw_out.reshape(N_DEV, D_MODEL // N_DEV, D_MODEL)], (True,) * 3, small_mlp)
    dzp, dq, dkv, small_mix, g_pool_scale = _mixers_bwd(
        ex_proj["token"], zp, q, kv, dpm, do, pool_w_bf, pool_b_row, pool_scale, bias_t, sink_row)
    gw_in = _tn_w_in(u, dzp, dq, dkv, dzg)
    ex_in = _exchange_start(
        "ex_in_start", [gw_in, small_mlp, small_mix, g_pool_scale], (True, False, False, False), dq)
    dx, g_norm_mix = _in_bwd(ex_in["token"], dzp, dq, dkv, dzg, w_in_f, xs, dh1, norm_mix)

    big_w = dict(w_in=w_in, p_pool=p_pool, p_attn=p_attn, w_out=w_out, w_up=w_up, w_down=w_down)
    big_m = dict(w_in=m_w_in, p_pool=m_p_pool, p_attn=m_p_attn, w_out=m_w_out, w_up=m_w_up, w_down=m_w_down)
    big_v = dict(w_in=v_w_in, p_pool=v_p_pool, p_attn=v_p_attn, w_out=v_w_out, w_up=v_w_up, w_down=v_w_down)
    res = {}

    def update(names, recvs, sents):
        for name, parts, sent in zip(names, recvs, sents):
            outs = _adamw(parts, big_w[name][0], big_m[name][0], big_v[name][0], sent)
            res[name] = [t[None] for t in outs]

    update(["w_up", "w_down"], *_exchange_wait("ex_mlp_wait", ex_mlp, dx))
    update(["p_pool", "p_attn", "w_out"], *_exchange_wait("ex_proj_wait", ex_proj, res["w_down"][0]))
    (norm_mix_all,) = _all_gather_weights("all_gather_norm_mix", [g_norm_mix], res["w_out"][0])
    (r_in, mlp_all, mix_all, scale_all), (s_in, _, _, _) = _exchange_wait("ex_in_wait", ex_in, norm_mix_all)
    update(["w_in"], [r_in], [s_in])

    natural = dict(norm_mix=(1, D_MODEL), pool_w=(MIX_POOL_B, LANES), pool_b=(4, LANES), pool_scale=(1, POOL_WIDTH),
                   attn_sinks=(1, LANES), norm_mlp=(1, D_MODEL), norm_final=(1, D_MODEL))

    def as_2d(t, name):
        if name == "attn_sinks":
            return jnp.pad(t, ((0, 0), (0, LANES - N_HEADS)))
        return t.reshape(natural[name])

    small_w = dict(norm_mix=norm_mix, pool_w=pool_w, pool_b=pool_b, pool_scale=pool_scale, attn_sinks=attn_sinks,
                   norm_mlp=norm_mlp, norm_final=norm_final)
    small_m = dict(norm_mix=m_norm_mix, pool_w=m_pool_w, pool_b=m_pool_b, pool_scale=m_pool_scale,
                   attn_sinks=m_attn_sinks, norm_mlp=m_norm_mlp, norm_final=m_norm_final)
    small_v = dict(norm_mix=v_norm_mix, pool_w=v_pool_w, pool_b=v_pool_b, pool_scale=v_pool_scale,
                   attn_sinks=v_attn_sinks, norm_mlp=v_norm_mlp, norm_final=v_norm_final)
    small_res, loss_all = _adamw_small(
        mlp_all, mix_all, scale_all, norm_mix_all,
        *[{k: as_2d(t, k) for k, t in d.items()} for d in (small_w, small_m, small_v)])
    loss = loss_all[0, 0]
    for name in SMALL_NAMES:
        shape = small_w[name].shape
        res[name] = [(t[:, :N_HEADS] if name == "attn_sinks" else t).reshape(shape) for t in small_res[name]]

    order = ["norm_mix", "w_in", "pool_w", "pool_b", "pool_scale", "attn_sinks", "p_pool", "p_attn", "w_out",
             "norm_mlp", "w_up", "w_down", "norm_final"]
    out = [loss, dx[None]]
    for kind in range(4):
        out += [res[name][kind] for name in order]
    return tuple(out)
```

```python
import functools
import math

import numpy as np
import jax
import jax.numpy as jnp
from jax import lax
from jax.experimental import pallas as pl
from jax.experimental.pallas import tpu as pltpu

F32 = jnp.float32
BF16 = jnp.bfloat16

D_MODEL = 1024
POOL_WIDTH = 512
ATTN_WIDTH = 512
KV_WIDTH = 128
HEAD_DIM = 64
N_HEADS = 8
N_KV_HEADS = 2
GROUP = 4
BLOCK = 128
POOL_WINDOWS = (2, 4, 8, 16)
POOL_GROUP_DIM = 128
POOL_HALO = 16
D_FF = 4096
FF_CHUNK = 1024
IN_WIDTH = 3328
RMS_EPS = 1e-5
NEG_INF = -1e30
ATTN_SCALE = 1.0 / math.sqrt(HEAD_DIM)
N_DEV = 8

ADAM_LR = 0.001
ADAM_B1 = 0.9
ADAM_B2 = 0.999
ADAM_EPS = 1e-08
ADAM_WD = 0.01
ADAM_STEP = 10

LANES = 128
VMEM_LIMIT_BYTES = 56 * 1024 * 1024
MESH = pl.DeviceIdType.MESH


def _params(n_grid_axes=1):
    return pltpu.CompilerParams(
        dimension_semantics=("arbitrary",) * n_grid_axes, vmem_limit_bytes=VMEM_LIMIT_BYTES)


def _dot(a, b):
    return jnp.dot(a, b, preferred_element_type=F32)


def _dot_nt(a, b):
    return lax.dot_general(a, b, (((1,), (1,)), ((), ())), preferred_element_type=F32)


def _dot_tn(a, b):
    return lax.dot_general(a, b, (((0,), (0,)), ((), ())), preferred_element_type=F32)


ANY = pl.BlockSpec(memory_space=pl.ANY)


def _rows(tm, n):
    return pl.BlockSpec((tm, n), lambda i: (i, 0))


def _whole(shape):
    zeros = (0,) * len(shape)
    return pl.BlockSpec(shape, lambda i: zeros)


def _rms_fwd(h, g):
    r = lax.rsqrt(jnp.mean(h * h, axis=-1, keepdims=True) + RMS_EPS)
    xh = h * r
    return r, xh, xh * g


def _rms_bwd(dy, xh, r, g):
    dxh = dy * g
    dh = r * (dxh - xh * jnp.mean(dxh * xh, axis=-1, keepdims=True))
    return dh, jnp.sum(dy * xh, axis=0, keepdims=True)


def _fwd_in(after, x, g_mix, w_in_blocks):
    s_len = x.shape[0]
    tm = min(512, s_len)
    width = IN_WIDTH // N_DEV

    def body(after_ref, x_ref, g_ref, wb_ref, u_ref, zp_ref, q_ref, kv_ref, zg_ref, w_ref):
        @pl.when(pl.program_id(0) == 0)
        def _():
            for j in range(N_DEV):
                w_ref[:, j * width:(j + 1) * width] = wb_ref[j]

        _, _, u = _rms_fwd(x_ref[...], g_ref[...])
        u = u.astype(BF16)
        u_ref[...] = u
        zp_ref[...] = _dot(u, w_ref[:, 0:512]).astype(BF16)
        q_ref[...] = _dot(u, w_ref[:, 512:1024]).astype(BF16)
        kv_ref[...] = _dot(u, w_ref[:, 1024:1280]).astype(BF16)
        zg_ref[...] = _dot(u, w_ref[:, 1280:3328]).astype(BF16)

    return pl.pallas_call(
        body, name="fwd_in", grid=(s_len // tm,),
        in_specs=[ANY, _rows(tm, D_MODEL), _whole((1, D_MODEL)),
                  pl.BlockSpec((N_DEV, D_MODEL, width), lambda i: (0, 0, 0), pipeline_mode=pl.Buffered(1))],
        out_specs=[_rows(tm, D_MODEL), _rows(tm, 512), _rows(tm, 512), _rows(tm, 256), _rows(tm, 2048),
                   _whole((D_MODEL, IN_WIDTH))],
        out_shape=[jax.ShapeDtypeStruct((s_len, n), BF16) for n in (D_MODEL, 512, 512, 256, 2048)]
        + [jax.ShapeDtypeStruct((D_MODEL, IN_WIDTH), BF16)],
        compiler_params=_params(),
    )(after, x, g_mix, w_in_blocks)


def _attn_constants(sinks):
    r = np.arange(BLOCK)[:, None]
    qi = np.arange(BLOCK)[None, :]
    dist = np.where(r <= qi, qi - r, BLOCK + qi - r).astype(np.float32)
    slopes = np.array([2.0 ** (-8.0 * (h + 1) / N_HEADS) for h in range(N_HEADS)], dtype=np.float32)
    bias = (-slopes[:, None, None] * dist[None]).reshape(N_KV_HEADS, GROUP, BLOCK, BLOCK)
    bias = np.ascontiguousarray(bias.transpose(0, 2, 1, 3)).reshape(N_KV_HEADS, BLOCK, GROUP * BLOCK)
    sink_row = jnp.repeat(sinks.astype(F32).reshape(N_KV_HEADS, GROUP), BLOCK, axis=1)[:, None, :]
    return jnp.asarray(bias.astype(np.float32)), sink_row


def _own_block_mask():
    shape = (BLOCK, GROUP * BLOCK)
    r = lax.broadcasted_iota(jnp.int32, shape, 0)
    qi = lax.broadcasted_iota(jnp.int32, shape, 1) & (BLOCK - 1)
    return r <= qi


def _pack_keys(t, own):
    return jnp.where(own, t[BLOCK:], t[:BLOCK])


def _unpack_keys(t, own):
    zero = jnp.zeros_like(t)
    return jnp.concatenate([jnp.where(own, zero, t), jnp.where(own, t, zero)], axis=0)


def _left_half(shape):
    return lax.broadcasted_iota(jnp.int32, shape, 1) < HEAD_DIM


def _dup_halves(slab):
    swapped = pltpu.roll(slab, HEAD_DIM, 1)
    left = _left_half(slab.shape)
    return jnp.where(left, slab, swapped), jnp.where(left, swapped, slab)


def _fill_kv_slabs(kvh_ref, kv_ref, ka_ref, vd_ref):
    for rows, src in ((slice(0, BLOCK), kvh_ref), (slice(BLOCK, None), kv_ref)):
        kvf = src[...].astype(F32)
        for ref, lanes in ((ka_ref, slice(0, KV_WIDTH)), (vd_ref, slice(KV_WIDTH, 2 * KV_WIDTH))):
            d0, d1 = _dup_halves(kvf[:, lanes])
            ref[0, rows, :] = d0.astype(BF16)
            ref[1, rows, :] = d1.astype(BF16)


def _stack_pairs(a, h):
    pieces = []
    for j in range(2):
        pair = a[:, h * 256 + j * LANES:h * 256 + (j + 1) * LANES]
        left = _left_half(pair.shape)
        zero = jnp.zeros_like(pair)
        pieces += [jnp.where(left, pair, zero), jnp.where(left, zero, pair)]
    return jnp.concatenate(pieces, axis=0)


def _attn_probs(kk, q_st, bias_p, sink_row, own, first):
    s = _pack_keys(_dot_nt(kk, q_st), own) * ATTN_SCALE + bias_p
    if first is not None:
        s = jnp.where(jnp.logical_and(first, jnp.logical_not(own)), NEG_INF, s)
    m = jnp.maximum(jnp.max(s, axis=0, keepdims=True), sink_row)
    p = jnp.exp(s - m)
    es = jnp.exp(sink_row - m)
    inv = 1.0 / (jnp.sum(p, axis=0, keepdims=True) + es)
    return p * inv, es * inv


def _pool_d(ext, cur, g, row0):
    w = POOL_WINDOWS[g]
    acc = ext
    k = 1
    while k < w:
        acc = acc + pltpu.roll(acc, k, 0)
        k *= 2
    return _window_mean(acc[POOL_HALO:, :], w, row0) - cur


def _window_mean(total, w, row0):
    t = row0 + lax.broadcasted_iota(jnp.int32, (POOL_HALO, total.shape[1]), 0)
    head = total[:POOL_HALO] / jnp.minimum(t + 1, w).astype(F32)
    return jnp.concatenate([head, total[POOL_HALO:] * (1.0 / w)], axis=0)


def _mixers_fwd(zp, q, kv, pool_w, pool_b, pool_scale, bias_t, sink_row):
    s_len = zp.shape[0]
    tq = min(512, s_len)
    nb = tq // BLOCK

    def body(zp_ref, zph_ref, q_ref, kv_ref, kvh_ref, pw_ref, pb_ref, ps_ref, bias_ref, sink_ref,
             pm_ref, o_ref, ka_ref, vd_ref):
        i = pl.program_id(0)
        cur = zp_ref[...].astype(F32)
        halo = zph_ref[...].astype(F32) * (i > 0).astype(F32)
        ext = jnp.concatenate([halo, cur], axis=0)
        for g in range(4):
            sl = slice(g * POOL_GROUP_DIM, (g + 1) * POOL_GROUP_DIM)
            d = _pool_d(ext[:, sl], cur[:, sl], g, i * tq)
            y = _dot(d.astype(BF16), pw_ref[g]) + pb_ref[:, sl]
            pm_ref[:, sl] = (y * ps_ref[:, sl]).astype(BF16)
        _fill_kv_slabs(kvh_ref, kv_ref, ka_ref, vd_ref)
        own = _own_block_mask()
        for b in range(nb):
            rq = slice(b * BLOCK, (b + 1) * BLOCK)
            rk = slice(b * BLOCK, (b + 2) * BLOCK)
            qb = q_ref[rq, :]
            for h in range(N_KV_HEADS):
                pn, _ = _attn_probs(ka_ref[h, rk, :], _stack_pairs(qb, h), bias_ref[h], sink_ref[h], own,
                                    (i == 0) if b == 0 else None)
                pn = _unpack_keys(pn, own).astype(BF16)
                vd = vd_ref[h, rk, :]
                left = _left_half(vd.shape)
                zero = jnp.zeros_like(vd)
                va, vb = jnp.where(left, vd, zero), jnp.where(left, zero, vd)
                for j in range(2):
                    o_pair = (_dot_tn(pn[:, (2 * j) * BLOCK:(2 * j + 1) * BLOCK], va)
                              + _dot_tn(pn[:, (2 * j + 1) * BLOCK:(2 * j + 2) * BLOCK], vb))
                    o_ref[rq, h * 256 + j * LANES:h * 256 + (j + 1) * LANES] = o_pair.astype(BF16)

    halo_pool = pl.BlockSpec((POOL_HALO, 512), lambda i: (jnp.maximum(i * (tq // POOL_HALO) - 1, 0), 0))
    halo_kv = pl.BlockSpec((BLOCK, 256), lambda i: (jnp.maximum(i * nb - 1, 0), 0))
    return pl.pallas_call(
        body, name="mixers_fwd", grid=(s_len // tq,),
        in_specs=[_rows(tq, 512), halo_pool, _rows(tq, 512), _rows(tq, 256), halo_kv,
                  _whole((4, 128, 128)), _whole((1, 512)), _whole((1, 512)),
                  _whole((N_KV_HEADS, BLOCK, GROUP * BLOCK)), _whole((N_KV_HEADS, 1, GROUP * BLOCK))],
        out_specs=[_rows(tq, 512), _rows(tq, 512)],
        out_shape=[jax.ShapeDtypeStruct((s_len, 512), BF16)] * 2,
        scratch_shapes=[pltpu.VMEM((N_KV_HEADS, tq + BLOCK, LANES), BF16)] * 2,
        compiler_params=_params(),
    )(zp, zp, q, kv, kv, pool_w, pool_b, pool_scale, bias_t, sink_row)


def _gated_mix(pm, o, zg, pp_ref, pa_ref):
    yp = _dot(pm, pp_ref[...])
    ya = _dot(o, pa_ref[...])
    gp = jax.nn.sigmoid(zg[:, :D_MODEL].astype(F32))
    ga = jax.nn.sigmoid(zg[:, D_MODEL:].astype(F32))
    return yp, ya, gp, ga


def _core(x, pm, o, zg, tgt, g_mlp, g_fin, p_pool, p_attn, w_out, w_up_blocks, w_down):
    s_len = x.shape[0]
    tm = min(256, s_len)
    n_chunks = D_FF // FF_CHUNK
    up_block = D_FF // N_DEV
    per_chunk = FF_CHUNK // up_block

    def body(x_ref, pm_ref, o_ref, zg_ref, tgt_ref, gm_ref, gf_ref, pp_ref, pa_ref, wo_ref, wu_ref, wd_ref,
             mixed_ref, dh1_ref, a_ref, dap_ref, u2_ref, dh2_ref, small_ref,
             dyp_ref, dya_ref, dzg_ref, dpm_ref, do_ref):
        i = pl.program_id(0)

        @pl.when(i == 0)
        def _():
            small_ref[...] = jnp.zeros_like(small_ref)

        yp, ya, gp, ga = _gated_mix(pm_ref[...], o_ref[...], zg_ref[...], pp_ref, pa_ref)
        mixed = (gp * yp + ga * ya).astype(BF16)
        mixed_ref[...] = mixed
        h1 = x_ref[...] + _dot(mixed, wo_ref[...])
        r2, xh2, u2 = _rms_fwd(h1, gm_ref[...])
        u2 = u2.astype(BF16)
        u2_ref[...] = u2
        acc = jnp.zeros((tm, D_MODEL), F32)
        for c in range(n_chunks):
            cs = slice(c * FF_CHUNK, (c + 1) * FF_CHUNK)
            a = jnp.concatenate([_dot(u2, wu_ref[per_chunk * c + j]) for j in range(per_chunk)], axis=1)
            a = jnp.maximum(a, 0.0)
            a_ref[:, cs] = a.astype(BF16)
            acc = acc + _dot((a * a).astype(BF16), wd_ref[cs, :])
        h2 = h1 + acc
        r3, xh3, y = _rms_fwd(h2, gf_ref[...])
        diff = y - tgt_ref[...]
        small_ref[2:3, :] += 0.5 * jnp.sum(jnp.mean(diff * diff, axis=-1, keepdims=True))
        dy = diff * (1.0 / D_MODEL)
        dh2, dgf = _rms_bwd(dy, xh3, r3, gf_ref[...])
        small_ref[1:2, :] += dgf
        dh2_bf = dh2.astype(BF16)
        dh2_ref[...] = dh2_bf
        du2 = jnp.zeros((tm, D_MODEL), F32)
        for c in range(n_chunks):
            cs = slice(c * FF_CHUNK, (c + 1) * FF_CHUNK)
            ds = _dot_nt(dh2_bf, wd_ref[cs, :])
            dap = (ds * (2.0 * a_ref[:, cs].astype(F32))).astype(BF16)
            dap_ref[:, cs] = dap
            for j in range(per_chunk):
                du2 = du2 + _dot_nt(dap[:, j * up_block:(j + 1) * up_block], wu_ref[per_chunk * c + j])
        dh1n, dgm = _rms_bwd(du2, xh2, r2, gm_ref[...])
        small_ref[0:1, :] += dgm
        dh1 = dh2 + dh1n
        dh1_ref[...] = dh1
        dm = _dot_nt(dh1.astype(BF16), wo_ref[...])
        dyp = (dm * gp).astype(BF16)
        dya = (dm * ga).astype(BF16)
        dyp_ref[...] = dyp
        dya_ref[...] = dya
        dzg_ref[:, :D_MODEL] = (dm * yp * (gp * (1.0 - gp))).astype(BF16)
        dzg_ref[:, D_MODEL:] = (dm * ya * (ga * (1.0 - ga))).astype(BF16)
        dpm_ref[...] = _dot_nt(dyp, pp_ref[...]).astype(BF16)
        do_ref[...] = _dot_nt(dya, pa_ref[...]).astype(BF16)

    def fixed(shape):
        return pl.BlockSpec(shape, lambda i: (0,) * len(shape), pipeline_mode=pl.Buffered(1))

    widths_dtypes = ((D_MODEL, BF16), (D_MODEL, F32), (D_FF, BF16), (D_FF, BF16), (D_MODEL, BF16), (D_MODEL, BF16))
    back = ((D_MODEL, BF16), (D_MODEL, BF16), (2048, BF16), (512, BF16), (512, BF16))
    return pl.pallas_call(
        body, name="core", grid=(s_len // tm,),
        in_specs=[_rows(tm, D_MODEL), _rows(tm, 512), _rows(tm, 512), _rows(tm, 2048), _rows(tm, D_MODEL),
                  _whole((1, D_MODEL)), _whole((1, D_MODEL)),
                  fixed((512, D_MODEL)), fixed((512, D_MODEL)), fixed((D_MODEL, D_MODEL)),
                  fixed((N_DEV, D_MODEL, up_block)), fixed((D_FF, D_MODEL))],
        out_specs=[_rows(tm, n) for n, _ in widths_dtypes] + [_whole((8, D_MODEL))] + [_rows(tm, n) for n, _ in back],
        out_shape=[jax.ShapeDtypeStruct((s_len, n), d) for n, d in widths_dtypes]
        + [jax.ShapeDtypeStruct((8, D_MODEL), F32)] + [jax.ShapeDtypeStruct((s_len, n), d) for n, d in back],
        compiler_params=_params(),
    )(x, pm, o, zg, tgt, g_mlp, g_fin, p_pool, p_attn, w_out, w_up_blocks, w_down)


def _tn_matmul(a, b, square_a=False, col_blocks=None, after=None):
    s_len, ka = a.shape
    nb = b.shape[1]
    tt = min(4096 if a.dtype == BF16 and b.dtype == BF16 else 2048, s_len)
    tk = min(1024, ka)
    tn = min(1024, nb)
    n_t = s_len // tt
    if col_blocks is None:
        out_spec = pl.BlockSpec((tk, tn), lambda k, j, t: (k, j))
        out_shape = jax.ShapeDtypeStruct((ka, nb), BF16)
    else:
        width = nb // col_blocks
        per_tile = tn // width
        out_spec = pl.BlockSpec((per_tile, tk, width), lambda k, j, t: (j, k, 0))
        out_shape = jax.ShapeDtypeStruct((col_blocks, ka, width), BF16)

    extra = [] if after is None else [after]

    def body(a_ref, b_ref, *rest):
        o_ref, acc_ref = rest[len(extra):]
        t = pl.program_id(2)

        @pl.when(t == 0)
        def _():
            acc_ref[...] = jnp.zeros_like(acc_ref)

        av = a_ref[...]
        if square_a:
            av = av * av
        acc_ref[...] += _dot_tn(av.astype(BF16), b_ref[...].astype(BF16))

        @pl.when(t == n_t - 1)
        def _():
            if col_blocks is None:
                o_ref[...] = acc_ref[...].astype(o_ref.dtype)
            else:
                for blk in range(per_tile):
                    o_ref[blk] = acc_ref[:, blk * width:(blk + 1) * width].astype(o_ref.dtype)

    return pl.pallas_call(
        body, name="tn_matmul", grid=(ka // tk, nb // tn, n_t),
        in_specs=[pl.BlockSpec((tt, tk), lambda k, j, t: (t, k)), pl.BlockSpec((tt, tn), lambda k, j, t: (t, j))]
        + [ANY] * len(extra),
        out_specs=out_spec, out_shape=out_shape,
        scratch_shapes=[pltpu.VMEM((tk, tn), F32)],
        compiler_params=_params(3),
    )(a, b, *extra)


def _tn_w_in(u, dzp, dq, dkv, dzg):
    s_len = u.shape[0]
    tt = min(1024, s_len)
    n_t = s_len // tt
    width = IN_WIDTH // N_DEV
    pieces = ((0, 512), (512, 1024), (1024, 1280), (1280, IN_WIDTH))

    def body(u_ref, dzp_ref, dq_ref, dkv_ref, dzg_ref, o_ref, acc_ref):
        t = pl.program_id(0)

        @pl.when(t == 0)
        def _():
            acc_ref[...] = jnp.zeros_like(acc_ref)

        uv = u_ref[...]
        for (c0, c1), ref in zip(pieces, (dzp_ref, dq_ref, dkv_ref, dzg_ref)):
            acc_ref[:, c0:c1] += _dot_tn(uv, ref[...])

        @pl.when(t == n_t - 1)
        def _():
            for j in range(N_DEV):
                o_ref[j] = acc_ref[:, j * width:(j + 1) * width].astype(BF16)

    return pl.pallas_call(
        body, name="tn_w_in", grid=(n_t,),
        in_specs=[_rows(tt, D_MODEL)] + [_rows(tt, c1 - c0) for c0, c1 in pieces],
        out_specs=_whole((N_DEV, D_MODEL, width)),
        out_shape=jax.ShapeDtypeStruct((N_DEV, D_MODEL, width), BF16),
        scratch_shapes=[pltpu.VMEM((D_MODEL, IN_WIDTH), F32)],
        compiler_params=_params(),
    )(u, dzp, dq, dkv, dzg)


MIX_POOL_B = 4 * POOL_GROUP_DIM
MIX_SINKS = MIX_POOL_B + 8
MIX_ROWS = MIX_SINKS + 8


def _mixers_bwd(after, zp, q, kv, dpm, do, pool_w, pool_b, pool_scale, bias_t, sink_row):
    s_len = zp.shape[0]
    tq = min(512, s_len)
    nb = tq // BLOCK
    n_steps = s_len // tq

    def body(after_ref, zp_ref, zph_ref, q_ref, kv_ref, kvh_ref, dpm_ref, dpmh_ref, do_ref, pw_ref, pb_ref, ps_ref,
             bias_ref, sink_ref, dzp_ref, dq_ref, dkv_ref, small_ref, dps_ref,
             ka_ref, vd_ref, dsk_acc, dkv_acc):
        i = pl.program_id(0)

        @pl.when(i == 0)
        def _():
            dkv_acc[...] = jnp.zeros_like(dkv_acc)
            small_ref[...] = jnp.zeros_like(small_ref)
            dps_ref[...] = jnp.zeros_like(dps_ref)
            dsk_acc[...] = jnp.zeros_like(dsk_acc)

        cur = zp_ref[...].astype(F32)
        halo = zph_ref[...].astype(F32) * (i > 0).astype(F32)
        ext = jnp.concatenate([halo, cur], axis=0)
        dpm_next = dpmh_ref[...].astype(F32) * (i < n_steps - 1).astype(F32)
        dpm_ext = jnp.concatenate([dpm_ref[...].astype(F32), dpm_next], axis=0)
        n_ext = tq + POOL_HALO
        for g in range(4):
            sl = slice(g * POOL_GROUP_DIM, (g + 1) * POOL_GROUP_DIM)
            w = POOL_WINDOWS[g]
            d = _pool_d(ext[:, sl], cur[:, sl], g, i * tq).astype(BF16)
            y_lin = _dot(d, pw_ref[g]) + pb_ref[:, sl]
            dps_ref[:, sl] += jnp.sum(dpm_ext[:tq, sl] * y_lin, axis=0, keepdims=True)
            dyl_ext = dpm_ext[:, sl] * ps_ref[:, sl]
            small_ref[MIX_POOL_B + g:MIX_POOL_B + g + 1, :] += jnp.sum(dyl_ext[:tq], axis=0, keepdims=True)
            dyl_bf = dyl_ext.astype(BF16)
            small_ref[g * POOL_GROUP_DIM:(g + 1) * POOL_GROUP_DIM, :] += _dot_tn(d, dyl_bf[:tq])
            dd = _dot_nt(dyl_bf, pw_ref[g])
            e = _window_mean(dd, w, i * tq)
            acc = e
            k = 1
            while k < w:
                acc = acc + pltpu.roll(acc, n_ext - k, 0)
                k *= 2
            dzp_ref[:, sl] = (acc[:tq] - dd[:tq]).astype(BF16)

        _fill_kv_slabs(kvh_ref, kv_ref, ka_ref, vd_ref)

        def fold(dup):
            return dup + pltpu.roll(dup, HEAD_DIM, 1)

        own = _own_block_mask()
        for b in range(nb):
            rq = slice(b * BLOCK, (b + 1) * BLOCK)
            rk = slice(b * BLOCK, (b + 2) * BLOCK)
            qb = q_ref[rq, :]
            dob = do_ref[rq, :]
            dk_dup, dv_dup = [], []
            for h in range(N_KV_HEADS):
                kk = ka_ref[h, rk, :]
                q_st = _stack_pairs(qb, h)
                do_st = _stack_pairs(dob, h)
                pn, psink = _attn_probs(kk, q_st, bias_ref[h], sink_ref[h], own, (i == 0) if b == 0 else None)
                dp = _pack_keys(_dot_nt(vd_ref[h, rk, :], do_st), own)
                delta = jnp.sum(pn * dp, axis=0, keepdims=True)
                dsk_acc[h] += -psink * delta
                ds = _unpack_keys((pn * (dp - delta)) * ATTN_SCALE, own).astype(BF16)
                pn = _unpack_keys(pn, own)
                dq_st = _dot_tn(ds, kk)
                for j in range(2):
                    left = _left_half((BLOCK, LANES))
                    dq_pair = jnp.where(left, dq_st[(2 * j) * BLOCK:(2 * j + 1) * BLOCK],
                                        dq_st[(2 * j + 1) * BLOCK:(2 * j + 2) * BLOCK])
                    dq_ref[rq, h * 256 + j * LANES:h * 256 + (j + 1) * LANES] = dq_pair.astype(BF16)
                dk_dup.append(fold(_dot(ds, q_st)))
                dv_dup.append(fold(_dot(pn.astype(BF16), do_st)))
            left = _left_half((2 * BLOCK, LANES))
            dkv_blk = jnp.concatenate([jnp.where(left, dk_dup[0], dk_dup[1]),
                                       jnp.where(left, dv_dup[0], dv_dup[1])], axis=1)
            g0 = pl.multiple_of(i * tq + b * BLOCK, BLOCK)
            dkv_acc[pl.ds(g0, 2 * BLOCK), :] += dkv_blk

        @pl.when(i == n_steps - 1)
        def _():
            dkv_ref[...] = dkv_acc[BLOCK:, :].astype(BF16)
            lane = lax.broadcasted_iota(jnp.int32, (1, LANES), 1)
            row = jnp.zeros((1, LANES), F32)
            for h in range(N_KV_HEADS):
                for g in range(GROUP):
                    tot = jnp.sum(dsk_acc[h, :, g * BLOCK:(g + 1) * BLOCK], axis=1, keepdims=True)
                    row = jnp.where(lane == GROUP * h + g, tot, row)
            small_ref[MIX_SINKS:MIX_SINKS + 1, :] = row

    blocks_per_tile = tq // POOL_HALO
    last_halo = s_len // POOL_HALO - 1
    halo_prev = pl.BlockSpec((POOL_HALO, 512), lambda i: (jnp.maximum(i * blocks_per_tile - 1, 0), 0))
    halo_next = pl.BlockSpec((POOL_HALO, 512), lambda i: (jnp.minimum((i + 1) * blocks_per_tile, last_halo), 0))
    halo_kv = pl.BlockSpec((BLOCK, 256), lambda i: (jnp.maximum(i * nb - 1, 0), 0))
    return pl.pallas_call(
        body, name="mixers_bwd", grid=(n_steps,),
        in_specs=[ANY, _rows(tq, 512), halo_prev, _rows(tq, 512), _rows(tq, 256), halo_kv,
                  _rows(tq, 512), halo_next, _rows(tq, 512),
                  _whole((4, 128, 128)), _whole((1, 512)), _whole((1, 512)),
                  _whole((N_KV_HEADS, BLOCK, GROUP * BLOCK)), _whole((N_KV_HEADS, 1, GROUP * BLOCK))],
        out_specs=[_rows(tq, 512), _rows(tq, 512), _whole((s_len, 256)),
                   _whole((MIX_ROWS, LANES)), _whole((1, 512))],
        out_shape=[jax.ShapeDtypeStruct((s_len, 512), BF16), jax.ShapeDtypeStruct((s_len, 512), BF16),
                   jax.ShapeDtypeStruct((s_len, 256), BF16), jax.ShapeDtypeStruct((MIX_ROWS, LANES), F32),
                   jax.ShapeDtypeStruct((1, 512), F32)],
        scratch_shapes=[pltpu.VMEM((N_KV_HEADS, tq + BLOCK, LANES), BF16)] * 2
        + [pltpu.VMEM((N_KV_HEADS, 1, GROUP * BLOCK), F32), pltpu.VMEM((s_len + BLOCK, 256), F32)],
        compiler_params=_params(),
    )(after, zp, zp, q, kv, kv, dpm, dpm, do, pool_w, pool_b, pool_scale, bias_t, sink_row)


def _in_bwd(after, dzp, dq, dkv, dzg, w_in, x, dh1, g_mix):
    s_len = x.shape[0]
    tm = min(1024, s_len)

    def body(after_ref, dzp_ref, dq_ref, dkv_ref, dzg_ref, w_ref, x_ref, dh1_ref, g_ref, dx_ref, dg_ref):
        i = pl.program_id(0)

        @pl.when(i == 0)
        def _():
            dg_ref[...] = jnp.zeros_like(dg_ref)

        du = _dot_nt(dzp_ref[...], w_ref[:, 0:512])
        du = du + _dot_nt(dq_ref[...], w_ref[:, 512:1024])
        du = du + _dot_nt(dkv_ref[...], w_ref[:, 1024:1280])
        du = du + _dot_nt(dzg_ref[...], w_ref[:, 1280:3328])
        r, xh, _ = _rms_fwd(x_ref[...], g_ref[...])
        dxn, dg = _rms_bwd(du, xh, r, g_ref[...])
        dg_ref[...] += dg
        dx_ref[...] = dh1_ref[...] + dxn

    return pl.pallas_call(
        body, name="in_bwd", grid=(s_len // tm,),
        in_specs=[ANY, _rows(tm, 512), _rows(tm, 512), _rows(tm, 256), _rows(tm, 2048), _whole((D_MODEL, IN_WIDTH)),
                  _rows(tm, D_MODEL), _rows(tm, D_MODEL), _whole((1, D_MODEL))],
        out_specs=[_rows(tm, D_MODEL), _whole((1, D_MODEL))],
        out_shape=[jax.ShapeDtypeStruct((s_len, D_MODEL), F32), jax.ShapeDtypeStruct((1, D_MODEL), F32)],
        compiler_params=_params(),
    )(after, dzp, dq, dkv, dzg, w_in, x, dh1, g_mix)


def _all_gather_weights(name, shards, after=None):
    n = len(shards)
    extra = [] if after is None else [after]
    n_extra = len(extra)

    def body(*refs):
        ins, outs = refs[:n], refs[n + n_extra:2 * n + n_extra]
        send_sems, recv_sems, local_sems = refs[2 * n + n_extra:]
        x, y, c = lax.axis_index("x"), lax.axis_index("y"), lax.axis_index("c")
        me, sibling = (x, y, c), (x, y, 1 - c)
        chips = [(1 - x, y), (x, 1 - y), (1 - x, 1 - y)]

        def slot(a, px, py, pc):
            return outs[a].at[4 * px + 2 * py + pc]

        def copy(a, k, block, to, src=None):
            return pltpu.make_async_remote_copy(
                src_ref=slot(a, *block) if src is None else src, dst_ref=slot(a, *block),
                send_sem=send_sems.at[a, k], recv_sem=recv_sems.at[a, k], device_id=to, device_id_type=MESH)

        mine = [pltpu.make_async_copy(ins[a], slot(a, *me), local_sems.at[a]) for a in range(n)]
        for cp in mine:
            cp.start()
        first = []
        for a in range(n):
            first.append(copy(a, 0, me, sibling, src=ins[a]))
            first += [copy(a, 1 + j, me, (*chip, c), src=ins[a]) for j, chip in enumerate(chips)]
        for cp in first:
            cp.start()
        passed = []
        for a in range(n):
            for j, chip in enumerate(chips):
                copy(a, 1 + j, (*chip, c), me).wait_recv()
                cp = copy(a, 4 + j, (*chip, c), sibling)
                cp.start()
                passed.append(cp)
        for a in range(n):
            copy(a, 0, sibling, me).wait_recv()
            for j, chip in enumerate(chips):
                copy(a, 4 + j, (*chip, 1 - c), me).wait_recv()
        for cp in first + passed:
            cp.wait_send()
        for cp in mine:
            cp.wait()

    return pl.pallas_call(
        body, name=name,
        in_specs=[ANY] * (n + n_extra), out_specs=[ANY] * n,
        out_shape=[jax.ShapeDtypeStruct((N_DEV,) + s.shape, s.dtype) for s in shards],
        scratch_shapes=[pltpu.SemaphoreType.DMA((n, 7)), pltpu.SemaphoreType.DMA((n, 7)), pltpu.SemaphoreType.DMA((n,))],
    )(*shards, *extra)


HBM_SPEC = pl.BlockSpec(memory_space=pltpu.HBM)
SEM_SPEC = pl.BlockSpec(memory_space=pltpu.SEMAPHORE)
DATAFLOW = pltpu.SideEffectType.DATAFLOW_SIDE_EFFECTING
N_PEERS = N_DEV - 1


CHIP_PEERS = (1, 2, 4, 6)
RELAYED = (2, 4, 6)


def _peer_copies(srcs, lands, scatter, send_sems, recv_sems):
    x, y, c = lax.axis_index("x"), lax.axis_index("y"), lax.axis_index("c")
    me_idx = 4 * x + 2 * y + c
    copies = []
    for k in range(1, N_DEV):
        px = 1 - x if (k >> 2) & 1 else x
        py = 1 - y if (k >> 1) & 1 else y
        pc = 1 - c if k & 1 else c
        p_idx = 4 * px + 2 * py + pc
        for a in range(len(srcs)):
            if scatter[a] == "chip" and k not in CHIP_PEERS:
                continue
            src = srcs[a].at[p_idx] if scatter[a] is True else srcs[a]
            dst = lands[a].at[k] if scatter[a] is True else lands[a].at[me_idx]
            copies.append(pltpu.make_async_remote_copy(
                src_ref=src, dst_ref=dst, send_sem=send_sems.at[a * N_PEERS + k - 1],
                recv_sem=recv_sems.at[a * N_PEERS + k - 1],
                device_id=(px, py, pc), device_id_type=MESH))
    return copies


def _exchange_start(name, srcs, scatter, after):
    n = len(srcs)
    lands = [lax.empty(s.shape if sc is True else (N_DEV,) + s.shape, s.dtype) for s, sc in zip(srcs, scatter)]

    def body(*refs):
        src_refs, land_refs = refs[:n], refs[n:2 * n]
        send_sems, recv_sems = refs[2 * n + 1], refs[2 * n + 2]
        token = refs[4 * n + 3]
        for cp in _peer_copies(src_refs, land_refs, scatter, send_sems, recv_sems):
            cp.start()
        token[...] = jnp.zeros_like(token)

    hbm = lambda t: pltpu.HBM(t.shape, t.dtype)
    outs = pl.pallas_call(
        body, name=name,
        out_shape=[pltpu.SemaphoreType.DMA((n * N_PEERS,)), pltpu.SemaphoreType.DMA((n * N_PEERS,))]
        + [hbm(t) for t in srcs] + [hbm(t) for t in lands] + [jax.ShapeDtypeStruct((8, LANES), F32)],
        in_specs=[HBM_SPEC] * (2 * n) + [ANY],
        out_specs=[SEM_SPEC, SEM_SPEC] + [HBM_SPEC] * (2 * n) + [pl.BlockSpec(memory_space=pltpu.VMEM)],
        input_output_aliases={i: 2 + i for i in range(2 * n)},
        compiler_params=pltpu.CompilerParams(has_side_effects=DATAFLOW),
    )(*[pltpu.with_memory_space_constraint(t, pltpu.HBM) for t in list(srcs) + lands], after)
    return dict(n=n, scatter=scatter, send_sems=outs[0], recv_sems=outs[1], srcs=outs[2:2 + n],
                lands=outs[2 + n:2 + 2 * n], token=outs[2 + 2 * n])


def _exchange_wait(name, handle, after):
    n, scatter = handle["n"], handle["scatter"]

    def body(*refs):
        src_refs, land_refs = refs[:n], refs[n:2 * n]
        send_sems, recv_sems = refs[2 * n], refs[2 * n + 1]
        for cp in _peer_copies(src_refs, land_refs, scatter, send_sems, recv_sems):
            cp.wait_send()
            cp.wait_recv()

    both = list(handle["srcs"]) + list(handle["lands"])
    outs = pl.pallas_call(
        body, name=name,
        out_shape=[pltpu.HBM(t.shape, t.dtype) for t in both],
        in_specs=[HBM_SPEC] * (2 * n) + [SEM_SPEC, SEM_SPEC, ANY],
        out_specs=[HBM_SPEC] * (2 * n),
        input_output_aliases={i: i for i in range(2 * n)},
        compiler_params=pltpu.CompilerParams(has_side_effects=DATAFLOW),
    )(*both, handle["send_sems"], handle["recv_sems"], after)
    me_idx = _my_index()
    lands = [land if sc is True else lax.dynamic_update_index_in_dim(land, src, me_idx, 0)
             for land, src, sc in zip(outs[n:], outs[:n], scatter)]
    return lands, outs[:n]


def _my_index():
    return 4 * lax.axis_index("x") + 2 * lax.axis_index("y") + lax.axis_index("c")


def _relay_copies(bufs, send_sems, recv_sems):
    x, y, c = lax.axis_index("x"), lax.axis_index("y"), lax.axis_index("c")
    copies = []
    for j, k in enumerate(RELAYED):
        px = 1 - x if (k >> 2) & 1 else x
        py = 1 - y if (k >> 1) & 1 else y
        slot = 4 * px + 2 * py + c
        for a, buf in enumerate(bufs):
            copies.append(pltpu.make_async_remote_copy(
                src_ref=buf.at[slot], dst_ref=buf.at[slot], send_sem=send_sems.at[a * len(RELAYED) + j],
                recv_sem=recv_sems.at[a * len(RELAYED) + j], device_id=(x, y, 1 - c), device_id_type=MESH))
    return copies


def _relay_start(name, bufs, after):
    n = len(bufs)

    def body(*refs):
        send_sems, recv_sems = refs[n + 1], refs[n + 2]
        for cp in _relay_copies(refs[:n], send_sems, recv_sems):
            cp.start()
        token = refs[2 * n + 3]
        token[...] = jnp.zeros_like(token)

    n_sems = n * len(RELAYED)
    outs = pl.pallas_call(
        body, name=name,
        out_shape=[pltpu.SemaphoreType.DMA((n_sems,)), pltpu.SemaphoreType.DMA((n_sems,))]
        + [pltpu.HBM(t.shape, t.dtype) for t in bufs] + [jax.ShapeDtypeStruct((8, LANES), F32)],
        in_specs=[HBM_SPEC] * n + [ANY],
        out_specs=[SEM_SPEC, SEM_SPEC] + [HBM_SPEC] * n + [pl.BlockSpec(memory_space=pltpu.VMEM)],
        input_output_aliases={i: 2 + i for i in range(n)},
        compiler_params=pltpu.CompilerParams(has_side_effects=DATAFLOW),
    )(*[pltpu.with_memory_space_constraint(t, pltpu.HBM) for t in bufs], after)
    return dict(n=n, send_sems=outs[0], recv_sems=outs[1], bufs=outs[2:2 + n], token=outs[2 + n])


def _relay_wait(name, handle, after):
    n = handle["n"]

    def body(*refs):
        for cp in _relay_copies(refs[:n], refs[n], refs[n + 1]):
            cp.wait_send()
            cp.wait_recv()

    return pl.pallas_call(
        body, name=name,
        out_shape=[pltpu.HBM(t.shape, t.dtype) for t in handle["bufs"]],
        in_specs=[HBM_SPEC] * n + [SEM_SPEC, SEM_SPEC, ANY],
        out_specs=[HBM_SPEC] * n,
        input_output_aliases={i: i for i in range(n)},
        compiler_params=pltpu.CompilerParams(has_side_effects=DATAFLOW),
    )(*handle["bufs"], handle["send_sems"], handle["recv_sems"], after)


def _adamw(parts, w, m, v, sent=None):
    r, c = w.shape
    tr = 512 if r % 512 == 0 else r
    own = sent is not None

    def body(*refs):
        if own:
            _, p_ref, own_ref, w_ref, m_ref, v_ref, g_ref, d_ref, nm_ref, nv_ref = refs
            g = own_ref[...].astype(F32)
        else:
            p_ref, w_ref, m_ref, v_ref, g_ref, d_ref, nm_ref, nv_ref = refs
            g = p_ref[0].astype(F32)
        for k in range(1, N_DEV):
            g = g + p_ref[k].astype(F32)
        m_new = ADAM_B1 * m_ref[...] + (1.0 - ADAM_B1) * g
        v_new = ADAM_B2 * v_ref[...] + (1.0 - ADAM_B2) * (g * g)
        m_hat = m_new / (1.0 - ADAM_B1 ** ADAM_STEP)
        v_hat = v_new / (1.0 - ADAM_B2 ** ADAM_STEP)
        g_ref[...] = g
        d_ref[...] = -ADAM_LR * (m_hat / (jnp.sqrt(v_hat) + ADAM_EPS) + ADAM_WD * w_ref[...])
        nm_ref[...] = m_new
        nv_ref[...] = v_new

    out_shape = [jax.ShapeDtypeStruct((r, c), F32)] * 4
    if not own:
        return pl.pallas_call(
            body, name="adamw", grid=(r // tr,),
            in_specs=[pl.BlockSpec((N_DEV, tr, c), lambda i: (0, i, 0))] + [_rows(tr, c)] * 3,
            out_specs=[_rows(tr, c)] * 4, out_shape=out_shape, compiler_params=_params(),
        )(parts, w, m, v)
    rows = pl.BlockSpec((tr, c), lambda i, me: (i, 0))
    return pl.pallas_call(
        body, name="adamw_own", out_shape=out_shape, compiler_params=_params(),
        grid_spec=pltpu.PrefetchScalarGridSpec(
            num_scalar_prefetch=1, grid=(r // tr,),
            in_specs=[pl.BlockSpec((N_DEV, tr, c), lambda i, me: (0, i, 0)),
                      pl.BlockSpec((None, tr, c), lambda i, me: (me[0], i, 0))] + [rows] * 3,
            out_specs=[rows] * 4),
    )(_my_index().reshape(1).astype(jnp.int32), parts, sent, w, m, v)


def _adam_step(g, w, m, v):
    m_new = ADAM_B1 * m + (1.0 - ADAM_B1) * g
    v_new = ADAM_B2 * v + (1.0 - ADAM_B2) * (g * g)
    m_hat = m_new / (1.0 - ADAM_B1 ** ADAM_STEP)
    v_hat = v_new / (1.0 - ADAM_B2 ** ADAM_STEP)
    return -ADAM_LR * (m_hat / (jnp.sqrt(v_hat) + ADAM_EPS) + ADAM_WD * w), m_new, v_new


SMALL_NAMES = ("norm_mix", "pool_w", "pool_b", "pool_scale", "attn_sinks", "norm_mlp", "norm_final")


def _adamw_small(mlp_all, mix_all, scale_all, nmix_all, w, m, v):
    def body(mlp_ref, mix_ref, scale_ref, nmix_ref, *refs):
        ins, outs = refs[:21], refs[21:]

        def total(ref, rows, lanes=slice(None)):
            g = ref[0, rows, lanes]
            for k in range(1, N_DEV):
                g = g + ref[k, rows, lanes]
            return g

        grads = dict(
            norm_mix=total(nmix_ref, slice(0, 1)), pool_w=total(mix_ref, slice(0, MIX_POOL_B)),
            pool_b=total(mix_ref, slice(MIX_POOL_B, MIX_POOL_B + 4)), pool_scale=total(scale_ref, slice(0, 1)),
            attn_sinks=total(mix_ref, slice(MIX_SINKS, MIX_SINKS + 1)),
            norm_mlp=total(mlp_ref, slice(0, 1)), norm_final=total(mlp_ref, slice(1, 2)))
        for i, name in enumerate(SMALL_NAMES):
            g = grads[name]
            d, m_new, v_new = _adam_step(g, ins[3 * i][...], ins[3 * i + 1][...], ins[3 * i + 2][...])
            for ref, val in zip(outs[4 * i:4 * i + 4], (g, d, m_new, v_new)):
                ref[...] = val
        outs[28][...] = jnp.broadcast_to(total(mlp_ref, slice(2, 3), slice(0, LANES)), (8, LANES))

    operands, out_shape = [], []
    for name in SMALL_NAMES:
        operands += [w[name], m[name], v[name]]
        out_shape += [jax.ShapeDtypeStruct(w[name].shape, F32)] * 4
    out_shape.append(jax.ShapeDtypeStruct((8, LANES), F32))
    outs = pl.pallas_call(body, name="adamw_small", out_shape=out_shape)(
        mlp_all, mix_all, scale_all, nmix_all, *operands)
    return {name: outs[4 * i:4 * i + 4] for i, name in enumerate(SMALL_NAMES)}, outs[28]


def kernel(x, norm_mix, w_in, pool_w, pool_b, pool_scale, attn_sinks, p_pool, p_attn, w_out, norm_mlp, w_up, w_down, norm_final, loss_target, m_norm_mix, m_w_in, m_pool_w, m_pool_b, m_pool_scale, m_attn_sinks, m_p_pool, m_p_attn, m_w_out, m_norm_mlp, m_w_up, m_w_down, m_norm_final, v_norm_mix, v_w_in, v_pool_w, v_pool_b, v_pool_scale, v_attn_sinks, v_p_pool, v_p_attn, v_w_out, v_norm_mlp, v_w_up, v_w_down, v_norm_final):
    xs = x[0]
    tgt = loss_target[0]
    s_len = xs.shape[0]

    w_in_bf, p_pool_bf, p_attn_bf, w_out_bf, w_up_bf, w_down_bf = [
        t[0].astype(BF16) for t in (w_in, p_pool, p_attn, w_out, w_up, w_down)]
    (w_in_g,) = _all_gather_weights("all_gather_w_in", [w_in_bf])
    ag_rest = _exchange_start(
        "ag_rest_start", [p_pool_bf, p_attn_bf, w_out_bf, w_up_bf, w_down_bf], ("chip",) * 5, w_in_g)

    pool_w_bf = pool_w[0].astype(BF16)
    pool_b_row = pool_b[0].reshape(1, POOL_WIDTH)
    bias_t, sink_row = _attn_constants(attn_sinks[0])

    u, zp, q, kv, zg, w_in_f = _fwd_in(ag_rest["token"], xs, norm_mix, w_in_g)
    pm, o = _mixers_fwd(zp, q, kv, pool_w_bf, pool_b_row, pool_scale, bias_t, sink_row)
    first_level, _ = _exchange_wait("ag_rest_wait", ag_rest, o)
    relay = _relay_start("ag_relay_start", first_level, pm)
    p_pool_g, p_attn_g, w_out_g, w_up_g, w_down_g = _relay_wait("ag_relay_wait", relay, relay["token"])
    p_pool_f = p_pool_g.transpose(1, 0, 2).reshape(POOL_WIDTH, D_MODEL)
    p_attn_f = p_attn_g.transpose(1, 0, 2).reshape(ATTN_WIDTH, D_MODEL)
    w_out_f = w_out_g.reshape(D_MODEL, D_MODEL)
    w_down_f = w_down_g.reshape(D_FF, D_MODEL)
    mixed, dh1, a, dapre, u2, dh2, small_mlp, dyp, dya, dzg, dpm, do = _core(
        xs, pm, o, zg, tgt, norm_mlp, norm_final.reshape(1, D_MODEL), p_pool_f, p_attn_f, w_out_f, w_up_g, w_down_f)
    gw_down = _tn_matmul(a, dh2, square_a=True)
    gw_up = _tn_matmul(u2, dapre, col_blocks=N_DEV)
    ex_mlp = _exchange_start(
        "ex_mlp_start", [gw_up, gw_down.reshape(N_DEV, D_FF // N_DEV, D_MODEL)], (True, True), small_mlp)
    gw_out = _tn_matmul(mixed, dh1, after=ex_mlp["token"])
    gp_pool = _tn_matmul(pm, dyp, col_blocks=N_DEV)
    gp_attn = _tn_matmul(o, dya, col_blocks=N_DEV)
    ex_proj = _exchange_start(
        "ex_proj_start", [gp_pool, gp_attn, gw_out.reshape(N_DEV, D_MODEL // N_DEV, D_MODEL)], (True,) * 3, small_mlp)
    dzp, dq, dkv, small_mix, g_pool_scale = _mixers_bwd(
        ex_proj["token"], zp, q, kv, dpm, do, pool_w_bf, pool_b_row, pool_scale, bias_t, sink_row)
    gw_in = _tn_w_in(u, dzp, dq, dkv, dzg)
    ex_in = _exchange_start(
        "ex_in_start", [gw_in, small_mlp, small_mix, g_pool_scale], (True, False, False, False), dq)
    dx, g_norm_mix = _in_bwd(ex_in["token"], dzp, dq, dkv, dzg, w_in_f, xs, dh1, norm_mix)

    big_w = dict(w_in=w_in, p_pool=p_pool, p_attn=p_attn, w_out=w_out, w_up=w_up, w_down=w_down)
    big_m = dict(w_in=m_w_in, p_pool=m_p_pool, p_attn=m_p_attn, w_out=m_w_out, w_up=m_w_up, w_down=m_w_down)
    big_v = dict(w_in=v_w_in, p_pool=v_p_pool, p_attn=v_p_attn, w_out=v_w_out, w_up=v_w_up, w_down=v_w_down)
    res = {}

    def update(names, recvs, sents):
        for name, parts, sent in zip(names, recvs, sents):
            outs = _adamw(parts, big_w[name][0], big_m[name][0], big_v[name][0], sent)
            res[name] = [t[None] for t in outs]

    update(["w_up", "w_down"], *_exchange_wait("ex_mlp_wait", ex_mlp, dx))
    update(["p_pool", "p_attn", "w_out"], *_exchange_wait("ex_proj_wait", ex_proj, res["w_down"][0]))
    (norm_mix_all,) = _all_gather_weights("all_gather_norm_mix", [g_norm_mix], res["w_out"][0])
    (r_in, mlp_all, mix_all, scale_all), (s_in, _, _, _) = _exchange_wait("ex_in_wait", ex_in, norm_mix_all)
    update(["w_in"], [r_in], [s_in])

    natural = dict(norm_mix=(1, D_MODEL), pool_w=(MIX_POOL_B, LANES), pool_b=(4, LANES), pool_scale=(1, POOL_WIDTH),
                   attn_sinks=(1, LANES), norm_mlp=(1, D_MODEL), norm_final=(1, D_MODEL))

    def as_2d(t, name):
        if name == "attn_sinks":
            return jnp.pad(t, ((0, 0), (0, LANES - N_HEADS)))
        return t.reshape(natural[name])

    small_w = dict(norm_mix=norm_mix, pool_w=pool_w, pool_b=pool_b, pool_scale=pool_scale, attn_sinks=attn_sinks,
                   norm_mlp=norm_mlp, norm_final=norm_final)
    small_m = dict(norm_mix=m_norm_mix, pool_w=m_pool_w, pool_b=m_pool_b, pool_scale=m_pool_scale,
                   attn_sinks=m_attn_sinks, norm_mlp=m_norm_mlp, norm_final=m_norm_final)
    small_v = dict(norm_mix=v_norm_mix, pool_w=v_pool_w, pool_b=v_pool_b, pool_scale=v_pool_scale,
                   attn_sinks=v_attn_sinks, norm_mlp=v_norm_mlp, norm_final=v_norm_final)
    small_res, loss_all = _adamw_small(
        mlp_all, mix_all, scale_all, norm_mix_all,
        *[{k: as_2d(t, k) for k, t in d.items()} for d in (small_w, small_m, small_v)])
    loss = loss_all[0, 0]
    for name in SMALL_NAMES:
        shape = small_w[name].shape
        res[name] = [(t[:, :N_HEADS] if name == "attn_sinks" else t).reshape(shape) for t in small_res[name]]

    order = ["norm_mix", "w_in", "pool_w", "pool_b", "pool_scale", "attn_sinks", "p_pool", "p_attn", "w_out",
             "norm_mlp", "w_up", "w_down", "norm_final"]
    out = [loss, dx[None]]
    for kind in range(4):
        out += [res[name][kind] for name in order]
    return tuple(out)
```

```python
import functools
import math

import numpy as np
import jax
import jax.numpy as jnp
from jax import lax
from jax.experimental import pallas as pl
from jax.experimental.pallas import tpu as pltpu

F32 = jnp.float32
BF16 = jnp.bfloat16

D_MODEL = 1024
POOL_WIDTH = 512
ATTN_WIDTH = 512
KV_WIDTH = 128
HEAD_DIM = 64
N_HEADS = 8
N_KV_HEADS = 2
GROUP = 4
BLOCK = 128
POOL_WINDOWS = (2, 4, 8, 16)
POOL_GROUP_DIM = 128
POOL_HALO = 16
D_FF = 4096
FF_CHUNK = 1024
IN_WIDTH = 3328
RMS_EPS = 1e-5
NEG_INF = -1e30
ATTN_SCALE = 1.0 / math.sqrt(HEAD_DIM)
N_DEV = 8

ADAM_LR = 0.001
ADAM_B1 = 0.9
ADAM_B2 = 0.999
ADAM_EPS = 1e-08
ADAM_WD = 0.01
ADAM_STEP = 10

LANES = 128
VMEM_LIMIT_BYTES = 56 * 1024 * 1024
MESH = pl.DeviceIdType.MESH


def _params(n_grid_axes=1):
    return pltpu.CompilerParams(
        dimension_semantics=("arbitrary",) * n_grid_axes, vmem_limit_bytes=VMEM_LIMIT_BYTES)


def _dot(a, b):
    return jnp.dot(a, b, preferred_element_type=F32)


def _dot_nt(a, b):
    return lax.dot_general(a, b, (((1,), (1,)), ((), ())), preferred_element_type=F32)


def _dot_tn(a, b):
    return lax.dot_general(a, b, (((0,), (0,)), ((), ())), preferred_element_type=F32)


ANY = pl.BlockSpec(memory_space=pl.ANY)


def _rows(tm, n):
    return pl.BlockSpec((tm, n), lambda i: (i, 0))


def _whole(shape):
    zeros = (0,) * len(shape)
    return pl.BlockSpec(shape, lambda i: zeros)


def _rms_fwd(h, g):
    r = lax.rsqrt(jnp.mean(h * h, axis=-1, keepdims=True) + RMS_EPS)
    xh = h * r
    return r, xh, xh * g


def _rms_bwd(dy, xh, r, g):
    dxh = dy * g
    dh = r * (dxh - xh * jnp.mean(dxh * xh, axis=-1, keepdims=True))
    return dh, jnp.sum(dy * xh, axis=0, keepdims=True)


def _fwd_in(after, x, g_mix, w_in_blocks):
    s_len = x.shape[0]
    tm = min(512, s_len)
    width = IN_WIDTH // N_DEV

    def body(after_ref, x_ref, g_ref, wb_ref, u_ref, zp_ref, q_ref, kv_ref, zg_ref, w_ref):
        @pl.when(pl.program_id(0) == 0)
        def _():
            for j in range(N_DEV):
                w_ref[:, j * width:(j + 1) * width] = wb_ref[j]

        _, _, u = _rms_fwd(x_ref[...], g_ref[...])
        u = u.astype(BF16)
        u_ref[...] = u
        zp_ref[...] = _dot(u, w_ref[:, 0:512]).astype(BF16)
        q_ref[...] = _dot(u, w_ref[:, 512:1024]).astype(BF16)
        kv_ref[...] = _dot(u, w_ref[:, 1024:1280]).astype(BF16)
        zg_ref[...] = _dot(u, w_ref[:, 1280:3328]).astype(BF16)

    return pl.pallas_call(
        body, name="fwd_in", grid=(s_len // tm,),
        in_specs=[ANY, _rows(tm, D_MODEL), _whole((1, D_MODEL)),
                  pl.BlockSpec((N_DEV, D_MODEL, width), lambda i: (0, 0, 0), pipeline_mode=pl.Buffered(1))],
        out_specs=[_rows(tm, D_MODEL), _rows(tm, 512), _rows(tm, 512), _rows(tm, 256), _rows(tm, 2048),
                   _whole((D_MODEL, IN_WIDTH))],
        out_shape=[jax.ShapeDtypeStruct((s_len, n), BF16) for n in (D_MODEL, 512, 512, 256, 2048)]
        + [jax.ShapeDtypeStruct((D_MODEL, IN_WIDTH), BF16)],
        compiler_params=_params(),
    )(after, x, g_mix, w_in_blocks)


def _attn_constants(sinks):
    r = np.arange(BLOCK)[:, None]
    qi = np.arange(BLOCK)[None, :]
    dist = np.where(r <= qi, qi - r, BLOCK + qi - r).astype(np.float32)
    slopes = np.array([2.0 ** (-8.0 * (h + 1) / N_HEADS) for h in range(N_HEADS)], dtype=np.float32)
    bias = (-slopes[:, None, None] * dist[None]).reshape(N_KV_HEADS, GROUP, BLOCK, BLOCK)
    bias = np.ascontiguousarray(bias.transpose(0, 2, 1, 3)).reshape(N_KV_HEADS, BLOCK, GROUP * BLOCK)
    sink_row = jnp.repeat(sinks.astype(F32).reshape(N_KV_HEADS, GROUP), BLOCK, axis=1)[:, None, :]
    return jnp.asarray(bias.astype(np.float32)), sink_row


def _own_block_mask():
    shape = (BLOCK, GROUP * BLOCK)
    r = lax.broadcasted_iota(jnp.int32, shape, 0)
    qi = lax.broadcasted_iota(jnp.int32, shape, 1) & (BLOCK - 1)
    return r <= qi


def _pack_keys(t, own):
    return jnp.where(own, t[BLOCK:], t[:BLOCK])


def _unpack_keys(t, own):
    zero = jnp.zeros_like(t)
    return jnp.concatenate([jnp.where(own, zero, t), jnp.where(own, t, zero)], axis=0)


def _left_half(shape):
    return lax.broadcasted_iota(jnp.int32, shape, 1) < HEAD_DIM


def _dup_halves(slab):
    swapped = pltpu.roll(slab, HEAD_DIM, 1)
    left = _left_half(slab.shape)
    return jnp.where(left, slab, swapped), jnp.where(left, swapped, slab)


def _fill_kv_slabs(kvh_ref, kv_ref, ka_ref, vd_ref):
    for rows, src in ((slice(0, BLOCK), kvh_ref), (slice(BLOCK, None), kv_ref)):
        kvf = src[...].astype(F32)
        for ref, lanes in ((ka_ref, slice(0, KV_WIDTH)), (vd_ref, slice(KV_WIDTH, 2 * KV_WIDTH))):
            d0, d1 = _dup_halves(kvf[:, lanes])
            ref[0, rows, :] = d0.astype(BF16)
            ref[1, rows, :] = d1.astype(BF16)


def _stack_pairs(a, h):
    pieces = []
    for j in range(2):
        pair = a[:, h * 256 + j * LANES:h * 256 + (j + 1) * LANES]
        left = _left_half(pair.shape)
        zero = jnp.zeros_like(pair)
        pieces += [jnp.where(left, pair, zero), jnp.where(left, zero, pair)]
    return jnp.concatenate(pieces, axis=0)


def _attn_probs(kk, q_st, bias_p, sink_row, own, first):
    s = _pack_keys(_dot_nt(kk, q_st), own) * ATTN_SCALE + bias_p
    if first is not None:
        s = jnp.where(jnp.logical_and(first, jnp.logical_not(own)), NEG_INF, s)
    m = jnp.maximum(jnp.max(s, axis=0, keepdims=True), sink_row)
    p = jnp.exp(s - m)
    es = jnp.exp(sink_row - m)
    inv = 1.0 / (jnp.sum(p, axis=0, keepdims=True) + es)
    return p * inv, es * inv


def _pool_d(ext, cur, g, row0):
    w = POOL_WINDOWS[g]
    acc = ext
    k = 1
    while k < w:
        acc = acc + pltpu.roll(acc, k, 0)
        k *= 2
    return _window_mean(acc[POOL_HALO:, :], w, row0) - cur


def _window_mean(total, w, row0):
    t = row0 + lax.broadcasted_iota(jnp.int32, (POOL_HALO, total.shape[1]), 0)
    head = total[:POOL_HALO] / jnp.minimum(t + 1, w).astype(F32)
    return jnp.concatenate([head, total[POOL_HALO:] * (1.0 / w)], axis=0)


def _mixers_fwd(zp, q, kv, pool_w, pool_b, pool_scale, bias_t, sink_row):
    s_len = zp.shape[0]
    tq = min(512, s_len)
    nb = tq // BLOCK

    def body(zp_ref, zph_ref, q_ref, kv_ref, kvh_ref, pw_ref, pb_ref, ps_ref, bias_ref, sink_ref,
             pm_ref, o_ref, ka_ref, vd_ref):
        i = pl.program_id(0)
        cur = zp_ref[...].astype(F32)
        halo = zph_ref[...].astype(F32) * (i > 0).astype(F32)
        ext = jnp.concatenate([halo, cur], axis=0)
        for g in range(4):
            sl = slice(g * POOL_GROUP_DIM, (g + 1) * POOL_GROUP_DIM)
            d = _pool_d(ext[:, sl], cur[:, sl], g, i * tq)
            y = _dot(d.astype(BF16), pw_ref[g]) + pb_ref[:, sl]
            pm_ref[:, sl] = (y * ps_ref[:, sl]).astype(BF16)
        _fill_kv_slabs(kvh_ref, kv_ref, ka_ref, vd_ref)
        own = _own_block_mask()
        for b in range(nb):
            rq = slice(b * BLOCK, (b + 1) * BLOCK)
            rk = slice(b * BLOCK, (b + 2) * BLOCK)
            qb = q_ref[rq, :]
            for h in range(N_KV_HEADS):
                pn, _ = _attn_probs(ka_ref[h, rk, :], _stack_pairs(qb, h), bias_ref[h], sink_ref[h], own,
                                    (i == 0) if b == 0 else None)
                pn = _unpack_keys(pn, own).astype(BF16)
                vd = vd_ref[h, rk, :]
                left = _left_half(vd.shape)
                zero = jnp.zeros_like(vd)
                va, vb = jnp.where(left, vd, zero), jnp.where(left, zero, vd)
                for j in range(2):
                    o_pair = (_dot_tn(pn[:, (2 * j) * BLOCK:(2 * j + 1) * BLOCK], va)
                              + _dot_tn(pn[:, (2 * j + 1) * BLOCK:(2 * j + 2) * BLOCK], vb))
                    o_ref[rq, h * 256 + j * LANES:h * 256 + (j + 1) * LANES] = o_pair.astype(BF16)

    halo_pool = pl.BlockSpec((POOL_HALO, 512), lambda i: (jnp.maximum(i * (tq // POOL_HALO) - 1, 0), 0))
    halo_kv = pl.BlockSpec((BLOCK, 256), lambda i: (jnp.maximum(i * nb - 1, 0), 0))
    return pl.pallas_call(
        body, name="mixers_fwd", grid=(s_len // tq,),
        in_specs=[_rows(tq, 512), halo_pool, _rows(tq, 512), _rows(tq, 256), halo_kv,
                  _whole((4, 128, 128)), _whole((1, 512)), _whole((1, 512)),
                  _whole((N_KV_HEADS, BLOCK, GROUP * BLOCK)), _whole((N_KV_HEADS, 1, GROUP * BLOCK))],
        out_specs=[_rows(tq, 512), _rows(tq, 512)],
        out_shape=[jax.ShapeDtypeStruct((s_len, 512), BF16)] * 2,
        scratch_shapes=[pltpu.VMEM((N_KV_HEADS, tq + BLOCK, LANES), BF16)] * 2,
        compiler_params=_params(),
    )(zp, zp, q, kv, kv, pool_w, pool_b, pool_scale, bias_t, sink_row)


def _gated_mix(pm, o, zg, pp_ref, pa_ref):
    yp = _dot(pm, pp_ref[...])
    ya = _dot(o, pa_ref[...])
    gp = jax.nn.sigmoid(zg[:, :D_MODEL].astype(F32))
    ga = jax.nn.sigmoid(zg[:, D_MODEL:].astype(F32))
    return yp, ya, gp, ga


def _core(x, pm, o, zg, tgt, g_mlp, g_fin, p_pool, p_attn, w_out, w_up_blocks, w_down):
    s_len = x.shape[0]
    tm = min(256, s_len)
    n_chunks = D_FF // FF_CHUNK
    up_block = D_FF // N_DEV
    per_chunk = FF_CHUNK // up_block

    def body(x_ref, pm_ref, o_ref, zg_ref, tgt_ref, gm_ref, gf_ref, pp_ref, pa_ref, wo_ref, wu_ref, wd_ref,
             mixed_ref, dh1_ref, a_ref, dap_ref, u2_ref, dh2_ref, small_ref,
             dyp_ref, dya_ref, dzg_ref, dpm_ref, do_ref):
        i = pl.program_id(0)

        @pl.when(i == 0)
        def _():
            small_ref[...] = jnp.zeros_like(small_ref)

        yp, ya, gp, ga = _gated_mix(pm_ref[...], o_ref[...], zg_ref[...], pp_ref, pa_ref)
        mixed = (gp * yp + ga * ya).astype(BF16)
        mixed_ref[...] = mixed
        h1 = x_ref[...] + _dot(mixed, wo_ref[...])
        r2, xh2, u2 = _rms_fwd(h1, gm_ref[...])
        u2 = u2.astype(BF16)
        u2_ref[...] = u2
        acc = jnp.zeros((tm, D_MODEL), F32)
        for c in range(n_chunks):
            cs = slice(c * FF_CHUNK, (c + 1) * FF_CHUNK)
            a = jnp.concatenate([_dot(u2, wu_ref[per_chunk * c + j]) for j in range(per_chunk)], axis=1)
            a = jnp.maximum(a, 0.0)
            a_ref[:, cs] = a.astype(BF16)
            acc = acc + _dot((a * a).astype(BF16), wd_ref[cs, :])
        h2 = h1 + acc
        r3, xh3, y = _rms_fwd(h2, gf_ref[...])
        diff = y - tgt_ref[...]
        small_ref[2:3, :] += 0.5 * jnp.sum(jnp.mean(diff * diff, axis=-1, keepdims=True))
        dy = diff * (1.0 / D_MODEL)
        dh2, dgf = _rms_bwd(dy, xh3, r3, gf_ref[...])
        small_ref[1:2, :] += dgf
        dh2_bf = dh2.astype(BF16)
        dh2_ref[...] = dh2_bf
        du2 = jnp.zeros((tm, D_MODEL), F32)
        for c in range(n_chunks):
            cs = slice(c * FF_CHUNK, (c + 1) * FF_CHUNK)
            ds = _dot_nt(dh2_bf, wd_ref[cs, :])
            dap = (ds * (2.0 * a_ref[:, cs].astype(F32))).astype(BF16)
            dap_ref[:, cs] = dap
            for j in range(per_chunk):
                du2 = du2 + _dot_nt(dap[:, j * up_block:(j + 1) * up_block], wu_ref[per_chunk * c + j])
        dh1n, dgm = _rms_bwd(du2, xh2, r2, gm_ref[...])
        small_ref[0:1, :] += dgm
        dh1 = dh2 + dh1n
        dh1_ref[...] = dh1
        dm = _dot_nt(dh1.astype(BF16), wo_ref[...])
        dyp = (dm * gp).astype(BF16)
        dya = (dm * ga).astype(BF16)
        dyp_ref[...] = dyp
        dya_ref[...] = dya
        dzg_ref[:, :D_MODEL] = (dm * yp * (gp * (1.0 - gp))).astype(BF16)
        dzg_ref[:, D_MODEL:] = (dm * ya * (ga * (1.0 - ga))).astype(BF16)
        dpm_ref[...] = _dot_nt(dyp, pp_ref[...]).astype(BF16)
        do_ref[...] = _dot_nt(dya, pa_ref[...]).astype(BF16)

    def fixed(shape):
        return pl.BlockSpec(shape, lambda i: (0,) * len(shape), pipeline_mode=pl.Buffered(1))

    widths_dtypes = ((D_MODEL, BF16), (D_MODEL, F32), (D_FF, BF16), (D_FF, BF16), (D_MODEL, BF16), (D_MODEL, BF16))
    back = ((D_MODEL, BF16), (D_MODEL, BF16), (2048, BF16), (512, BF16), (512, BF16))
    return pl.pallas_call(
        body, name="core", grid=(s_len // tm,),
        in_specs=[_rows(tm, D_MODEL), _rows(tm, 512), _rows(tm, 512), _rows(tm, 2048), _rows(tm, D_MODEL),
                  _whole((1, D_MODEL)), _whole((1, D_MODEL)),
                  fixed((512, D_MODEL)), fixed((512, D_MODEL)), fixed((D_MODEL, D_MODEL)),
                  fixed((N_DEV, D_MODEL, up_block)), fixed((D_FF, D_MODEL))],
        out_specs=[_rows(tm, n) for n, _ in widths_dtypes] + [_whole((8, D_MODEL))] + [_rows(tm, n) for n, _ in back],
        out_shape=[jax.ShapeDtypeStruct((s_len, n), d) for n, d in widths_dtypes]
        + [jax.ShapeDtypeStruct((8, D_MODEL), F32)] + [jax.ShapeDtypeStruct((s_len, n), d) for n, d in back],
        compiler_params=_params(),
    )(x, pm, o, zg, tgt, g_mlp, g_fin, p_pool, p_attn, w_out, w_up_blocks, w_down)


def _tn_matmul(a, b, square_a=False, col_blocks=None, after=None):
    s_len, ka = a.shape
    nb = b.shape[1]
    tt = min(2048, s_len)
    tk = min(2048, ka)
    tn = min(2048 if tk <= 1024 else 1024, nb)
    n_t = s_len // tt
    if col_blocks is None:
        out_spec = pl.BlockSpec((tk, tn), lambda k, j, t: (k, j))
        out_shape = jax.ShapeDtypeStruct((ka, nb), BF16)
    else:
        width = nb // col_blocks
        per_tile = tn // width
        out_spec = pl.BlockSpec((per_tile, tk, width), lambda k, j, t: (j, k, 0))
        out_shape = jax.ShapeDtypeStruct((col_blocks, ka, width), BF16)

    extra = [] if after is None else [after]

    def body(a_ref, b_ref, *rest):
        o_ref, acc_ref = rest[len(extra):]
        t = pl.program_id(2)

        @pl.when(t == 0)
        def _():
            acc_ref[...] = jnp.zeros_like(acc_ref)

        av = a_ref[...]
        if square_a:
            av = av * av
        acc_ref[...] += _dot_tn(av.astype(BF16), b_ref[...].astype(BF16))

        @pl.when(t == n_t - 1)
        def _():
            if col_blocks is None:
                o_ref[...] = acc_ref[...].astype(o_ref.dtype)
            else:
                for blk in range(per_tile):
                    o_ref[blk] = acc_ref[:, blk * width:(blk + 1) * width].astype(o_ref.dtype)

    return pl.pallas_call(
        body, name="tn_matmul", grid=(ka // tk, nb // tn, n_t),
        in_specs=[pl.BlockSpec((tt, tk), lambda k, j, t: (t, k)), pl.BlockSpec((tt, tn), lambda k, j, t: (t, j))]
        + [ANY] * len(extra),
        out_specs=out_spec, out_shape=out_shape,
        scratch_shapes=[pltpu.VMEM((tk, tn), F32)],
        compiler_params=_params(3),
    )(a, b, *extra)


def _tn_w_in(u, dzp, dq, dkv, dzg):
    s_len = u.shape[0]
    tt = min(1024, s_len)
    n_t = s_len // tt
    width = IN_WIDTH // N_DEV
    pieces = ((0, 512), (512, 1024), (1024, 1280), (1280, IN_WIDTH))

    def body(u_ref, dzp_ref, dq_ref, dkv_ref, dzg_ref, o_ref, acc_ref):
        t = pl.program_id(0)

        @pl.when(t == 0)
        def _():
            acc_ref[...] = jnp.zeros_like(acc_ref)

        uv = u_ref[...]
        for (c0, c1), ref in zip(pieces, (dzp_ref, dq_ref, dkv_ref, dzg_ref)):
            acc_ref[:, c0:c1] += _dot_tn(uv, ref[...])

        @pl.when(t == n_t - 1)
        def _():
            for j in range(N_DEV):
                o_ref[j] = acc_ref[:, j * width:(j + 1) * width].astype(BF16)

    return pl.pallas_call(
        body, name="tn_w_in", grid=(n_t,),
        in_specs=[_rows(tt, D_MODEL)] + [_rows(tt, c1 - c0) for c0, c1 in pieces],
        out_specs=_whole((N_DEV, D_MODEL, width)),
        out_shape=jax.ShapeDtypeStruct((N_DEV, D_MODEL, width), BF16),
        scratch_shapes=[pltpu.VMEM((D_MODEL, IN_WIDTH), F32)],
        compiler_params=_params(),
    )(u, dzp, dq, dkv, dzg)


MIX_POOL_B = 4 * POOL_GROUP_DIM
MIX_SINKS = MIX_POOL_B + 8
MIX_ROWS = MIX_SINKS + 8


def _mixers_bwd(after, zp, q, kv, dpm, do, pool_w, pool_b, pool_scale, bias_t, sink_row):
    s_len = zp.shape[0]
    tq = min(512, s_len)
    nb = tq // BLOCK
    n_steps = s_len // tq

    def body(after_ref, zp_ref, zph_ref, q_ref, kv_ref, kvh_ref, dpm_ref, dpmh_ref, do_ref, pw_ref, pb_ref, ps_ref,
             bias_ref, sink_ref, dzp_ref, dq_ref, dkv_ref, small_ref, dps_ref,
             ka_ref, vd_ref, dsk_acc, dkv_acc):
        i = pl.program_id(0)

        @pl.when(i == 0)
        def _():
            dkv_acc[...] = jnp.zeros_like(dkv_acc)
            small_ref[...] = jnp.zeros_like(small_ref)
            dps_ref[...] = jnp.zeros_like(dps_ref)
            dsk_acc[...] = jnp.zeros_like(dsk_acc)

        cur = zp_ref[...].astype(F32)
        halo = zph_ref[...].astype(F32) * (i > 0).astype(F32)
        ext = jnp.concatenate([halo, cur], axis=0)
        dpm_next = dpmh_ref[...].astype(F32) * (i < n_steps - 1).astype(F32)
        dpm_ext = jnp.concatenate([dpm_ref[...].astype(F32), dpm_next], axis=0)
        n_ext = tq + POOL_HALO
        for g in range(4):
            sl = slice(g * POOL_GROUP_DIM, (g + 1) * POOL_GROUP_DIM)
            w = POOL_WINDOWS[g]
            d = _pool_d(ext[:, sl], cur[:, sl], g, i * tq).astype(BF16)
            y_lin = _dot(d, pw_ref[g]) + pb_ref[:, sl]
            dps_ref[:, sl] += jnp.sum(dpm_ext[:tq, sl] * y_lin, axis=0, keepdims=True)
            dyl_ext = dpm_ext[:, sl] * ps_ref[:, sl]
            small_ref[MIX_POOL_B + g:MIX_POOL_B + g + 1, :] += jnp.sum(dyl_ext[:tq], axis=0, keepdims=True)
            dyl_bf = dyl_ext.astype(BF16)
            small_ref[g * POOL_GROUP_DIM:(g + 1) * POOL_GROUP_DIM, :] += _dot_tn(d, dyl_bf[:tq])
            dd = _dot_nt(dyl_bf, pw_ref[g])
            e = _window_mean(dd, w, i * tq)
            acc = e
            k = 1
            while k < w:
                acc = acc + pltpu.roll(acc, n_ext - k, 0)
                k *= 2
            dzp_ref[:, sl] = (acc[:tq] - dd[:tq]).astype(BF16)

        _fill_kv_slabs(kvh_ref, kv_ref, ka_ref, vd_ref)

        def fold(dup):
            return dup + pltpu.roll(dup, HEAD_DIM, 1)

        own = _own_block_mask()
        for b in range(nb):
            rq = slice(b * BLOCK, (b + 1) * BLOCK)
            rk = slice(b * BLOCK, (b + 2) * BLOCK)
            qb = q_ref[rq, :]
            dob = do_ref[rq, :]
            dk_dup, dv_dup = [], []
            for h in range(N_KV_HEADS):
                kk = ka_ref[h, rk, :]
                q_st = _stack_pairs(qb, h)
                do_st = _stack_pairs(dob, h)
                pn, psink = _attn_probs(kk, q_st, bias_ref[h], sink_ref[h], own, (i == 0) if b == 0 else None)
                dp = _pack_keys(_dot_nt(vd_ref[h, rk, :], do_st), own)
                delta = jnp.sum(pn * dp, axis=0, keepdims=True)
                dsk_acc[h] += -psink * delta
                ds = _unpack_keys((pn * (dp - delta)) * ATTN_SCALE, own).astype(BF16)
                pn = _unpack_keys(pn, own)
                dq_st = _dot_tn(ds, kk)
                for j in range(2):
                    left = _left_half((BLOCK, LANES))
                    dq_pair = jnp.where(left, dq_st[(2 * j) * BLOCK:(2 * j + 1) * BLOCK],
                                        dq_st[(2 * j + 1) * BLOCK:(2 * j + 2) * BLOCK])
                    dq_ref[rq, h * 256 + j * LANES:h * 256 + (j + 1) * LANES] = dq_pair.astype(BF16)
                dk_dup.append(fold(_dot(ds, q_st)))
                dv_dup.append(fold(_dot(pn.astype(BF16), do_st)))
            left = _left_half((2 * BLOCK, LANES))
            dkv_blk = jnp.concatenate([jnp.where(left, dk_dup[0], dk_dup[1]),
                                       jnp.where(left, dv_dup[0], dv_dup[1])], axis=1)
            g0 = pl.multiple_of(i * tq + b * BLOCK, BLOCK)
            dkv_acc[pl.ds(g0, 2 * BLOCK), :] += dkv_blk

        @pl.when(i == n_steps - 1)
        def _():
            dkv_ref[...] = dkv_acc[BLOCK:, :].astype(BF16)
            lane = lax.broadcasted_iota(jnp.int32, (1, LANES), 1)
            row = jnp.zeros((1, LANES), F32)
            for h in range(N_KV_HEADS):
                for g in range(GROUP):
                    tot = jnp.sum(dsk_acc[h, :, g * BLOCK:(g + 1) * BLOCK], axis=1, keepdims=True)
                    row = jnp.where(lane == GROUP * h + g, tot, row)
            small_ref[MIX_SINKS:MIX_SINKS + 1, :] = row

    blocks_per_tile = tq // POOL_HALO
    last_halo = s_len // POOL_HALO - 1
    halo_prev = pl.BlockSpec((POOL_HALO, 512), lambda i: (jnp.maximum(i * blocks_per_tile - 1, 0), 0))
    halo_next = pl.BlockSpec((POOL_HALO, 512), lambda i: (jnp.minimum((i + 1) * blocks_per_tile, last_halo), 0))
    halo_kv = pl.BlockSpec((BLOCK, 256), lambda i: (jnp.maximum(i * nb - 1, 0), 0))
    return pl.pallas_call(
        body, name="mixers_bwd", grid=(n_steps,),
        in_specs=[ANY, _rows(tq, 512), halo_prev, _rows(tq, 512), _rows(tq, 256), halo_kv,
                  _rows(tq, 512), halo_next, _rows(tq, 512),
                  _whole((4, 128, 128)), _whole((1, 512)), _whole((1, 512)),
                  _whole((N_KV_HEADS, BLOCK, GROUP * BLOCK)), _whole((N_KV_HEADS, 1, GROUP * BLOCK))],
        out_specs=[_rows(tq, 512), _rows(tq, 512), _whole((s_len, 256)),
                   _whole((MIX_ROWS, LANES)), _whole((1, 512))],
        out_shape=[jax.ShapeDtypeStruct((s_len, 512), BF16), jax.ShapeDtypeStruct((s_len, 512), BF16),
                   jax.ShapeDtypeStruct((s_len, 256), BF16), jax.ShapeDtypeStruct((MIX_ROWS, LANES), F32),
                   jax.ShapeDtypeStruct((1, 512), F32)],
        scratch_shapes=[pltpu.VMEM((N_KV_HEADS, tq + BLOCK, LANES), BF16)] * 2
        + [pltpu.VMEM((N_KV_HEADS, 1, GROUP * BLOCK), F32), pltpu.VMEM((s_len + BLOCK, 256), F32)],
        compiler_params=_params(),
    )(after, zp, zp, q, kv, kv, dpm, dpm, do, pool_w, pool_b, pool_scale, bias_t, sink_row)


def _in_bwd(after, dzp, dq, dkv, dzg, w_in, x, dh1, g_mix):
    s_len = x.shape[0]
    tm = min(512, s_len)

    def body(after_ref, dzp_ref, dq_ref, dkv_ref, dzg_ref, w_ref, x_ref, dh1_ref, g_ref, dx_ref, dg_ref):
        i = pl.program_id(0)

        @pl.when(i == 0)
        def _():
            dg_ref[...] = jnp.zeros_like(dg_ref)

        du = _dot_nt(dzp_ref[...], w_ref[:, 0:512])
        du = du + _dot_nt(dq_ref[...], w_ref[:, 512:1024])
        du = du + _dot_nt(dkv_ref[...], w_ref[:, 1024:1280])
        du = du + _dot_nt(dzg_ref[...], w_ref[:, 1280:3328])
        r, xh, _ = _rms_fwd(x_ref[...], g_ref[...])
        dxn, dg = _rms_bwd(du, xh, r, g_ref[...])
        dg_ref[...] += dg
        dx_ref[...] = dh1_ref[...] + dxn

    return pl.pallas_call(
        body, name="in_bwd", grid=(s_len // tm,),
        in_specs=[ANY, _rows(tm, 512), _rows(tm, 512), _rows(tm, 256), _rows(tm, 2048), _whole((D_MODEL, IN_WIDTH)),
                  _rows(tm, D_MODEL), _rows(tm, D_MODEL), _whole((1, D_MODEL))],
        out_specs=[_rows(tm, D_MODEL), _whole((1, D_MODEL))],
        out_shape=[jax.ShapeDtypeStruct((s_len, D_MODEL), F32), jax.ShapeDtypeStruct((1, D_MODEL), F32)],
        compiler_params=_params(),
    )(after, dzp, dq, dkv, dzg, w_in, x, dh1, g_mix)


def _all_gather_weights(name, shards, after=None):
    n = len(shards)
    extra = [] if after is None else [after]
    n_extra = len(extra)

    def body(*refs):
        ins, outs = refs[:n], refs[n + n_extra:2 * n + n_extra]
        send_sems, recv_sems, local_sems = refs[2 * n + n_extra:]
        x, y, c = lax.axis_index("x"), lax.axis_index("y"), lax.axis_index("c")
        me, sibling = (x, y, c), (x, y, 1 - c)
        chips = [(1 - x, y), (x, 1 - y), (1 - x, 1 - y)]

        def slot(a, px, py, pc):
            return outs[a].at[4 * px + 2 * py + pc]

        def copy(a, k, block, to, src=None):
            return pltpu.make_async_remote_copy(
                src_ref=slot(a, *block) if src is None else src, dst_ref=slot(a, *block),
                send_sem=send_sems.at[a, k], recv_sem=recv_sems.at[a, k], device_id=to, device_id_type=MESH)

        mine = [pltpu.make_async_copy(ins[a], slot(a, *me), local_sems.at[a]) for a in range(n)]
        for cp in mine:
            cp.start()
        first = []
        for a in range(n):
            first.append(copy(a, 0, me, sibling, src=ins[a]))
            first += [copy(a, 1 + j, me, (*chip, c), src=ins[a]) for j, chip in enumerate(chips)]
        for cp in first:
            cp.start()
        passed = []
        for a in range(n):
            for j, chip in enumerate(chips):
                copy(a, 1 + j, (*chip, c), me).wait_recv()
                cp = copy(a, 4 + j, (*chip, c), sibling)
                cp.start()
                passed.append(cp)
        for a in range(n):
            copy(a, 0, sibling, me).wait_recv()
            for j, chip in enumerate(chips):
                copy(a, 4 + j, (*chip, 1 - c), me).wait_recv()
        for cp in first + passed:
            cp.wait_send()
        for cp in mine:
            cp.wait()

    return pl.pallas_call(
        body, name=name,
        in_specs=[ANY] * (n + n_extra), out_specs=[ANY] * n,
        out_shape=[jax.ShapeDtypeStruct((N_DEV,) + s.shape, s.dtype) for s in shards],
        scratch_shapes=[pltpu.SemaphoreType.DMA((n, 7)), pltpu.SemaphoreType.DMA((n, 7)), pltpu.SemaphoreType.DMA((n,))],
    )(*shards, *extra)


HBM_SPEC = pl.BlockSpec(memory_space=pltpu.HBM)
SEM_SPEC = pl.BlockSpec(memory_space=pltpu.SEMAPHORE)
DATAFLOW = pltpu.SideEffectType.DATAFLOW_SIDE_EFFECTING
N_PEERS = N_DEV - 1


CHIP_PEERS = (1, 2, 4, 6)
RELAYED = (2, 4, 6)


def _peer_copies(srcs, lands, scatter, send_sems, recv_sems):
    x, y, c = lax.axis_index("x"), lax.axis_index("y"), lax.axis_index("c")
    me_idx = 4 * x + 2 * y + c
    copies = []
    for k in range(1, N_DEV):
        px = 1 - x if (k >> 2) & 1 else x
        py = 1 - y if (k >> 1) & 1 else y
        pc = 1 - c if k & 1 else c
        p_idx = 4 * px + 2 * py + pc
        for a in range(len(srcs)):
            if scatter[a] == "chip" and k not in CHIP_PEERS:
                continue
            src = srcs[a].at[p_idx] if scatter[a] is True else srcs[a]
            dst = lands[a].at[k] if scatter[a] is True else lands[a].at[me_idx]
            copies.append(pltpu.make_async_remote_copy(
                src_ref=src, dst_ref=dst, send_sem=send_sems.at[a * N_PEERS + k - 1],
                recv_sem=recv_sems.at[a * N_PEERS + k - 1],
                device_id=(px, py, pc), device_id_type=MESH))
    return copies


def _exchange_start(name, srcs, scatter, after):
    n = len(srcs)
    lands = [lax.empty(s.shape if sc is True else (N_DEV,) + s.shape, s.dtype) for s, sc in zip(srcs, scatter)]

    def body(*refs):
        src_refs, land_refs = refs[:n], refs[n:2 * n]
        send_sems, recv_sems = refs[2 * n + 1], refs[2 * n + 2]
        token = refs[4 * n + 3]
        for cp in _peer_copies(src_refs, land_refs, scatter, send_sems, recv_sems):
            cp.start()
        token[...] = jnp.zeros_like(token)

    hbm = lambda t: pltpu.HBM(t.shape, t.dtype)
    outs = pl.pallas_call(
        body, name=name,
        out_shape=[pltpu.SemaphoreType.DMA((n * N_PEERS,)), pltpu.SemaphoreType.DMA((n * N_PEERS,))]
        + [hbm(t) for t in srcs] + [hbm(t) for t in lands] + [jax.ShapeDtypeStruct((8, LANES), F32)],
        in_specs=[HBM_SPEC] * (2 * n) + [ANY],
        out_specs=[SEM_SPEC, SEM_SPEC] + [HBM_SPEC] * (2 * n) + [pl.BlockSpec(memory_space=pltpu.VMEM)],
        input_output_aliases={i: 2 + i for i in range(2 * n)},
        compiler_params=pltpu.CompilerParams(has_side_effects=DATAFLOW),
    )(*[pltpu.with_memory_space_constraint(t, pltpu.HBM) for t in list(srcs) + lands], after)
    return dict(n=n, scatter=scatter, send_sems=outs[0], recv_sems=outs[1], srcs=outs[2:2 + n],
                lands=outs[2 + n:2 + 2 * n], token=outs[2 + 2 * n])


def _exchange_wait(name, handle, after):
    n, scatter = handle["n"], handle["scatter"]

    def body(*refs):
        src_refs, land_refs = refs[:n], refs[n:2 * n]
        send_sems, recv_sems = refs[2 * n], refs[2 * n + 1]
        for cp in _peer_copies(src_refs, land_refs, scatter, send_sems, recv_sems):
            cp.wait_send()
            cp.wait_recv()

    both = list(handle["srcs"]) + list(handle["lands"])
    outs = pl.pallas_call(
        body, name=name,
        out_shape=[pltpu.HBM(t.shape, t.dtype) for t in both],
        in_specs=[HBM_SPEC] * (2 * n) + [SEM_SPEC, SEM_SPEC, ANY],
        out_specs=[HBM_SPEC] * (2 * n),
        input_output_aliases={i: i for i in range(2 * n)},
        compiler_params=pltpu.CompilerParams(has_side_effects=DATAFLOW),
    )(*both, handle["send_sems"], handle["recv_sems"], after)
    me_idx = _my_index()
    lands = [land if sc is True else lax.dynamic_update_index_in_dim(land, src, me_idx, 0)
             for land, src, sc in zip(outs[n:], outs[:n], scatter)]
    return lands, outs[:n]


def _my_index():
    return 4 * lax.axis_index("x") + 2 * lax.axis_index("y") + lax.axis_index("c")


def _relay_copies(bufs, send_sems, recv_sems):
    x, y, c = lax.axis_index("x"), lax.axis_index("y"), lax.axis_index("c")
    copies = []
    for j, k in enumerate(RELAYED):
        px = 1 - x if (k >> 2) & 1 else x
        py = 1 - y if (k >> 1) & 1 else y
        slot = 4 * px + 2 * py + c
        for a, buf in enumerate(bufs):
            copies.append(pltpu.make_async_remote_copy(
                src_ref=buf.at[slot], dst_ref=buf.at[slot], send_sem=send_sems.at[a * len(RELAYED) + j],
                recv_sem=recv_sems.at[a * len(RELAYED) + j], device_id=(x, y, 1 - c), device_id_type=MESH))
    return copies


def _relay_start(name, bufs, after):
    n = len(bufs)

    def body(*refs):
        send_sems, recv_sems = refs[n + 1], refs[n + 2]
        for cp in _relay_copies(refs[:n], send_sems, recv_sems):
            cp.start()
        token = refs[2 * n + 3]
        token[...] = jnp.zeros_like(token)

    n_sems = n * len(RELAYED)
    outs = pl.pallas_call(
        body, name=name,
        out_shape=[pltpu.SemaphoreType.DMA((n_sems,)), pltpu.SemaphoreType.DMA((n_sems,))]
        + [pltpu.HBM(t.shape, t.dtype) for t in bufs] + [jax.ShapeDtypeStruct((8, LANES), F32)],
        in_specs=[HBM_SPEC] * n + [ANY],
        out_specs=[SEM_SPEC, SEM_SPEC] + [HBM_SPEC] * n + [pl.BlockSpec(memory_space=pltpu.VMEM)],
        input_output_aliases={i: 2 + i for i in range(n)},
        compiler_params=pltpu.CompilerParams(has_side_effects=DATAFLOW),
    )(*[pltpu.with_memory_space_constraint(t, pltpu.HBM) for t in bufs], after)
    return dict(n=n, send_sems=outs[0], recv_sems=outs[1], bufs=outs[2:2 + n], token=outs[2 + n])


def _relay_wait(name, handle, after):
    n = handle["n"]

    def body(*refs):
        for cp in _relay_copies(refs[:n], refs[n], refs[n + 1]):
            cp.wait_send()
            cp.wait_recv()

    return pl.pallas_call(
        body, name=name,
        out_shape=[pltpu.HBM(t.shape, t.dtype) for t in handle["bufs"]],
        in_specs=[HBM_SPEC] * n + [SEM_SPEC, SEM_SPEC, ANY],
        out_specs=[HBM_SPEC] * n,
        input_output_aliases={i: i for i in range(n)},
        compiler_params=pltpu.CompilerParams(has_side_effects=DATAFLOW),
    )(*handle["bufs"], handle["send_sems"], handle["recv_sems"], after)


def _adamw(parts, w, m, v, sent=None):
    r, c = w.shape
    tr = 256 if r % 256 == 0 else r
    own = sent is not None

    def body(*refs):
        if own:
            _, p_ref, own_ref, w_ref, m_ref, v_ref, g_ref, d_ref, nm_ref, nv_ref = refs
            g = own_ref[...].astype(F32)
        else:
            p_ref, w_ref, m_ref, v_ref, g_ref, d_ref, nm_ref, nv_ref = refs
            g = p_ref[0].astype(F32)
        for k in range(1, N_DEV):
            g = g + p_ref[k].astype(F32)
        m_new = ADAM_B1 * m_ref[...] + (1.0 - ADAM_B1) * g
        v_new = ADAM_B2 * v_ref[...] + (1.0 - ADAM_B2) * (g * g)
        m_hat = m_new / (1.0 - ADAM_B1 ** ADAM_STEP)
        v_hat = v_new / (1.0 - ADAM_B2 ** ADAM_STEP)
        g_ref[...] = g
        d_ref[...] = -ADAM_LR * (m_hat / (jnp.sqrt(v_hat) + ADAM_EPS) + ADAM_WD * w_ref[...])
        nm_ref[...] = m_new
        nv_ref[...] = v_new

    out_shape = [jax.ShapeDtypeStruct((r, c), F32)] * 4
    if not own:
        return pl.pallas_call(
            body, name="adamw", grid=(r // tr,),
            in_specs=[pl.BlockSpec((N_DEV, tr, c), lambda i: (0, i, 0))] + [_rows(tr, c)] * 3,
            out_specs=[_rows(tr, c)] * 4, out_shape=out_shape, compiler_params=_params(),
        )(parts, w, m, v)
    rows = pl.BlockSpec((tr, c), lambda i, me: (i, 0))
    return pl.pallas_call(
        body, name="adamw_own", out_shape=out_shape, compiler_params=_params(),
        grid_spec=pltpu.PrefetchScalarGridSpec(
            num_scalar_prefetch=1, grid=(r // tr,),
            in_specs=[pl.BlockSpec((N_DEV, tr, c), lambda i, me: (0, i, 0)),
                      pl.BlockSpec((None, tr, c), lambda i, me: (me[0], i, 0))] + [rows] * 3,
            out_specs=[rows] * 4),
    )(_my_index().reshape(1).astype(jnp.int32), parts, sent, w, m, v)


def _adam_step(g, w, m, v):
    m_new = ADAM_B1 * m + (1.0 - ADAM_B1) * g
    v_new = ADAM_B2 * v + (1.0 - ADAM_B2) * (g * g)
    m_hat = m_new / (1.0 - ADAM_B1 ** ADAM_STEP)
    v_hat = v_new / (1.0 - ADAM_B2 ** ADAM_STEP)
    return -ADAM_LR * (m_hat / (jnp.sqrt(v_hat) + ADAM_EPS) + ADAM_WD * w), m_new, v_new


SMALL_NAMES = ("norm_mix", "pool_w", "pool_b", "pool_scale", "attn_sinks", "norm_mlp", "norm_final")


def _adamw_small(mlp_all, mix_all, scale_all, nmix_all, w, m, v):
    def body(mlp_ref, mix_ref, scale_ref, nmix_ref, *refs):
        ins, outs = refs[:21], refs[21:]

        def total(ref, rows, lanes=slice(None)):
            g = ref[0, rows, lanes]
            for k in range(1, N_DEV):
                g = g + ref[k, rows, lanes]
            return g

        grads = dict(
            norm_mix=total(nmix_ref, slice(0, 1)), pool_w=total(mix_ref, slice(0, MIX_POOL_B)),
            pool_b=total(mix_ref, slice(MIX_POOL_B, MIX_POOL_B + 4)), pool_scale=total(scale_ref, slice(0, 1)),
            attn_sinks=total(mix_ref, slice(MIX_SINKS, MIX_SINKS + 1)),
            norm_mlp=total(mlp_ref, slice(0, 1)), norm_final=total(mlp_ref, slice(1, 2)))
        for i, name in enumerate(SMALL_NAMES):
            g = grads[name]
            d, m_new, v_new = _adam_step(g, ins[3 * i][...], ins[3 * i + 1][...], ins[3 * i + 2][...])
            for ref, val in zip(outs[4 * i:4 * i + 4], (g, d, m_new, v_new)):
                ref[...] = val
        outs[28][...] = jnp.broadcast_to(total(mlp_ref, slice(2, 3), slice(0, LANES)), (8, LANES))

    operands, out_shape = [], []
    for name in SMALL_NAMES:
        operands += [w[name], m[name], v[name]]
        out_shape += [jax.ShapeDtypeStruct(w[name].shape, F32)] * 4
    out_shape.append(jax.ShapeDtypeStruct((8, LANES), F32))
    outs = pl.pallas_call(body, name="adamw_small", out_shape=out_shape)(
        mlp_all, mix_all, scale_all, nmix_all, *operands)
    return {name: outs[4 * i:4 * i + 4] for i, name in enumerate(SMALL_NAMES)}, outs[28]


def kernel(x, norm_mix, w_in, pool_w, pool_b, pool_scale, attn_sinks, p_pool, p_attn, w_out, norm_mlp, w_up, w_down, norm_final, loss_target, m_norm_mix, m_w_in, m_pool_w, m_pool_b, m_pool_scale, m_attn_sinks, m_p_pool, m_p_attn, m_w_out, m_norm_mlp, m_w_up, m_w_down, m_norm_final, v_norm_mix, v_w_in, v_pool_w, v_pool_b, v_pool_scale, v_attn_sinks, v_p_pool, v_p_attn, v_w_out, v_norm_mlp, v_w_up, v_w_down, v_norm_final):
    xs = x[0]
    tgt = loss_target[0]
    s_len = xs.shape[0]

    w_in_bf, p_pool_bf, p_attn_bf, w_out_bf, w_up_bf, w_down_bf = [
        t[0].astype(BF16) for t in (w_in, p_pool, p_attn, w_out, w_up, w_down)]
    (w_in_g,) = _all_gather_weights("all_gather_w_in", [w_in_bf])
    ag_rest = _exchange_start(
        "ag_rest_start", [p_pool_bf, p_attn_bf, w_out_bf, w_up_bf, w_down_bf], ("chip",) * 5, w_in_g)

    pool_w_bf = pool_w[0].astype(BF16)
    pool_b_row = pool_b[0].reshape(1, POOL_WIDTH)
    bias_t, sink_row = _attn_constants(attn_sinks[0])

    u, zp, q, kv, zg, w_in_f = _fwd_in(ag_rest["token"], xs, norm_mix, w_in_g)
    pm, o = _mixers_fwd(zp, q, kv, pool_w_bf, pool_b_row, pool_scale, bias_t, sink_row)
    first_level, _ = _exchange_wait("ag_rest_wait", ag_rest, o)
    relay = _relay_start("ag_relay_start", first_level, pm)
    p_pool_g, p_attn_g, w_out_g, w_up_g, w_down_g = _relay_wait("ag_relay_wait", relay, relay["token"])
    p_pool_f = p_pool_g.transpose(1, 0, 2).reshape(POOL_WIDTH, D_MODEL)
    p_attn_f = p_attn_g.transpose(1, 0, 2).reshape(ATTN_WIDTH, D_MODEL)
    w_out_f = w_out_g.reshape(D_MODEL, D_MODEL)
    w_down_f = w_down_g.reshape(D_FF, D_MODEL)
    mixed, dh1, a, dapre, u2, dh2, small_mlp, dyp, dya, dzg, dpm, do = _core(
        xs, pm, o, zg, tgt, norm_mlp, norm_final.reshape(1, D_MODEL), p_pool_f, p_attn_f, w_out_f, w_up_g, w_down_f)
    gw_down = _tn_matmul(a, dh2, square_a=True)
    gw_up = _tn_matmul(u2, dapre, col_blocks=N_DEV)
    ex_mlp = _exchange_start(
        "ex_mlp_start", [gw_up, gw_down.reshape(N_DEV, D_FF // N_DEV, D_MODEL)], (True, True), small_mlp)
    gw_out = _tn_matmul(mixed, dh1, after=ex_mlp["token"])
    gp_pool = _tn_matmul(pm, dyp, col_blocks=N_DEV)
    gp_attn = _tn_matmul(o, dya, col_blocks=N_DEV)
    ex_proj = _exchange_start(
        "ex_proj_start", [gp_pool, gp_attn, gw_out.reshape(N_DEV, D_MODEL // N_DEV, D_MODEL)], (True,) * 3, small_mlp)
    dzp, dq, dkv, small_mix, g_pool_scale = _mixers_bwd(
        ex_proj["token"], zp, q, kv, dpm, do, pool_w_bf, pool_b_row, pool_scale, bias_t, sink_row)
    gw_in = _tn_w_in(u, dzp, dq, dkv, dzg)
    ex_in = _exchange_start(
        "ex_in_start", [gw_in, small_mlp, small_mix, g_pool_scale], (True, False, False, False), dq)
    dx, g_norm_mix = _in_bwd(ex_in["token"], dzp, dq, dkv, dzg, w_in_f, xs, dh1, norm_mix)

    big_w = dict(w_in=w_in, p_pool=p_pool, p_attn=p_attn, w_out=w_out, w_up=w_up, w_down=w_down)
    big_m = dict(w_in=m_w_in, p_pool=m_p_pool, p_attn=m_p_attn, w_out=m_w_out, w_up=m_w_up, w_down=m_w_down)
    big_v = dict(w_in=v_w_in, p_pool=v_p_pool, p_attn=v_p_attn, w_out=v_w_out, w_up=v_w_up, w_down=v_w_down)
    res = {}

    def update(names, recvs, sents):
        for name, parts, sent in zip(names, recvs, sents):
            outs = _adamw(parts, big_w[name][0], big_m[name][0], big_v[name][0], sent)
            res[name] = [t[None] for t in outs]

    update(["w_up", "w_down"], *_exchange_wait("ex_mlp_wait", ex_mlp, dx))
    update(["p_pool", "p_attn", "w_out"], *_exchange_wait("ex_proj_wait", ex_proj, res["w_down"][0]))
    (norm_mix_all,) = _all_gather_weights("all_gather_norm_mix", [g_norm_mix], res["w_out"][0])
    (r_in, mlp_all, mix_all, scale_all), (s_in, _, _, _) = _exchange_wait("ex_in_wait", ex_in, norm_mix_all)
    update(["w_in"], [r_in], [s_in])

    natural = dict(norm_mix=(1, D_MODEL), pool_w=(MIX_POOL_B, LANES), pool_b=(4, LANES), pool_scale=(1, POOL_WIDTH),
                   attn_sinks=(1, LANES), norm_mlp=(1, D_MODEL), norm_final=(1, D_MODEL))

    def as_2d(t, name):
        if name == "attn_sinks":
            return jnp.pad(t, ((0, 0), (0, LANES - N_HEADS)))
        return t.reshape(natural[name])

    small_w = dict(norm_mix=norm_mix, pool_w=pool_w, pool_b=pool_b, pool_scale=pool_scale, attn_sinks=attn_sinks,
                   norm_mlp=norm_mlp, norm_final=norm_final)
    small_m = dict(norm_mix=m_norm_mix, pool_w=m_pool_w, pool_b=m_pool_b, pool_scale=m_pool_scale,
                   attn_sinks=m_attn_sinks, norm_mlp=m_norm_mlp, norm_final=m_norm_final)
    small_v = dict(norm_mix=v_norm_mix, pool_w=v_pool_w, pool_b=v_pool_b, pool_scale=v_pool_scale,
                   attn_sinks=v_attn_sinks, norm_mlp=v_norm_mlp, norm_final=v_norm_final)
    small_res, loss_all = _adamw_small(
        mlp_all, mix_all, scale_all, norm_mix_all,
        *[{k: as_2d(t, k) for k, t in d.items()} for d in (small_w, small_m, small_v)])
    loss = loss_all[0, 0]
    for name in SMALL_NAMES:
        shape = small_w[name].shape
        res[name] = [(t[:, :N_HEADS] if name == "attn_sinks" else t).reshape(shape) for t in small_res[name]]

    order = ["norm_mix", "w_in", "pool_w", "pool_b", "pool_scale", "attn_sinks", "p_pool", "p_attn", "w_out",
             "norm_mlp", "w_up", "w_down", "norm_final"]
    out = [loss, dx[None]]
    for kind in range(4):
        out += [res[name][kind] for name in order]
    return tuple(out)
```

```python
import functools
import math

import numpy as np
import jax
import jax.numpy as jnp
from jax import lax
from jax.experimental import pallas as pl
from jax.experimental.pallas import tpu as pltpu

F32 = jnp.float32
BF16 = jnp.bfloat16

D_MODEL = 1024
POOL_WIDTH = 512
ATTN_WIDTH = 512
KV_WIDTH = 128
HEAD_DIM = 64
N_HEADS = 8
N_KV_HEADS = 2
GROUP = 4
BLOCK = 128
POOL_WINDOWS = (2, 4, 8, 16)
POOL_GROUP_DIM = 128
POOL_HALO = 16
D_FF = 4096
FF_CHUNK = 1024
IN_WIDTH = 3328
RMS_EPS = 1e-5
NEG_INF = -1e30
ATTN_SCALE = 1.0 / math.sqrt(HEAD_DIM)
N_DEV = 8

ADAM_LR = 0.001
ADAM_B1 = 0.9
ADAM_B2 = 0.999
ADAM_EPS = 1e-08
ADAM_WD = 0.01
ADAM_STEP = 10

LANES = 128
VMEM_LIMIT_BYTES = 56 * 1024 * 1024
MESH = pl.DeviceIdType.MESH


def _params(n_grid_axes=1):
    return pltpu.CompilerParams(
        dimension_semantics=("arbitrary",) * n_grid_axes, vmem_limit_bytes=VMEM_LIMIT_BYTES)


def _dot(a, b):
    return jnp.dot(a, b, preferred_element_type=F32)


def _dot_nt(a, b):
    return lax.dot_general(a, b, (((1,), (1,)), ((), ())), preferred_element_type=F32)


def _dot_tn(a, b):
    return lax.dot_general(a, b, (((0,), (0,)), ((), ())), preferred_element_type=F32)


ANY = pl.BlockSpec(memory_space=pl.ANY)


def _rows(tm, n):
    return pl.BlockSpec((tm, n), lambda i: (i, 0))


def _whole(shape):
    zeros = (0,) * len(shape)
    return pl.BlockSpec(shape, lambda i: zeros)


def _rms_fwd(h, g):
    r = lax.rsqrt(jnp.mean(h * h, axis=-1, keepdims=True) + RMS_EPS)
    xh = h * r
    return r, xh, xh * g


def _rms_bwd(dy, xh, r, g):
    dxh = dy * g
    dh = r * (dxh - xh * jnp.mean(dxh * xh, axis=-1, keepdims=True))
    return dh, jnp.sum(dy * xh, axis=0, keepdims=True)


def _fwd_in(after, x, g_mix, w_in_t):
    s_len = x.shape[0]
    tm = min(512, s_len)

    def body(after_ref, x_ref, g_ref, w_ref, u_ref, zp_ref, q_ref, kv_ref, zg_ref):
        _, _, u = _rms_fwd(x_ref[...], g_ref[...])
        u = u.astype(BF16)
        u_ref[...] = u
        zp_ref[...] = _dot_nt(u, w_ref[0:512, :]).astype(BF16)
        q_ref[...] = _dot_nt(u, w_ref[512:1024, :]).astype(BF16)
        kv_ref[...] = _dot_nt(u, w_ref[1024:1280, :]).astype(BF16)
        zg_ref[...] = _dot_nt(u, w_ref[1280:3328, :]).astype(BF16)

    return pl.pallas_call(
        body, name="fwd_in", grid=(s_len // tm,),
        in_specs=[ANY, _rows(tm, D_MODEL), _whole((1, D_MODEL)),
                  pl.BlockSpec((IN_WIDTH, D_MODEL), lambda i: (0, 0), pipeline_mode=pl.Buffered(1))],
        out_specs=[_rows(tm, D_MODEL), _rows(tm, 512), _rows(tm, 512), _rows(tm, 256), _rows(tm, 2048)],
        out_shape=[jax.ShapeDtypeStruct((s_len, n), BF16) for n in (D_MODEL, 512, 512, 256, 2048)],
        compiler_params=_params(),
    )(after, x, g_mix, w_in_t)


def _attn_constants(sinks):
    r = np.arange(BLOCK)[:, None]
    qi = np.arange(BLOCK)[None, :]
    dist = np.where(r <= qi, qi - r, BLOCK + qi - r).astype(np.float32)
    slopes = np.array([2.0 ** (-8.0 * (h + 1) / N_HEADS) for h in range(N_HEADS)], dtype=np.float32)
    bias = (-slopes[:, None, None] * dist[None]).reshape(N_KV_HEADS, GROUP, BLOCK, BLOCK)
    bias = np.ascontiguousarray(bias.transpose(0, 2, 1, 3)).reshape(N_KV_HEADS, BLOCK, GROUP * BLOCK)
    sink_row = jnp.repeat(sinks.astype(F32).reshape(N_KV_HEADS, GROUP), BLOCK, axis=1)[:, None, :]
    return jnp.asarray(bias.astype(np.float32)), sink_row


def _own_block_mask():
    shape = (BLOCK, GROUP * BLOCK)
    r = lax.broadcasted_iota(jnp.int32, shape, 0)
    qi = lax.broadcasted_iota(jnp.int32, shape, 1) & (BLOCK - 1)
    return r <= qi


def _pack_keys(t, own):
    return jnp.where(own, t[BLOCK:], t[:BLOCK])


def _unpack_keys(t, own):
    zero = jnp.zeros_like(t)
    return jnp.concatenate([jnp.where(own, zero, t), jnp.where(own, t, zero)], axis=0)


def _left_half(shape):
    return lax.broadcasted_iota(jnp.int32, shape, 1) < HEAD_DIM


def _dup_halves(slab):
    swapped = pltpu.roll(slab, HEAD_DIM, 1)
    left = _left_half(slab.shape)
    return jnp.where(left, slab, swapped), jnp.where(left, swapped, slab)


def _fill_kv_slabs(kvh_ref, kv_ref, ka_ref, vd_ref):
    for rows, src in ((slice(0, BLOCK), kvh_ref), (slice(BLOCK, None), kv_ref)):
        kvf = src[...].astype(F32)
        for ref, lanes in ((ka_ref, slice(0, KV_WIDTH)), (vd_ref, slice(KV_WIDTH, 2 * KV_WIDTH))):
            d0, d1 = _dup_halves(kvf[:, lanes])
            ref[0, rows, :] = d0.astype(BF16)
            ref[1, rows, :] = d1.astype(BF16)


def _stack_pairs(a, h):
    pieces = []
    for j in range(2):
        pair = a[:, h * 256 + j * LANES:h * 256 + (j + 1) * LANES]
        left = _left_half(pair.shape)
        zero = jnp.zeros_like(pair)
        pieces += [jnp.where(left, pair, zero), jnp.where(left, zero, pair)]
    return jnp.concatenate(pieces, axis=0)


def _attn_probs(kk, q_st, bias_p, sink_row, own, first):
    s = _pack_keys(_dot_nt(kk, q_st), own) * ATTN_SCALE + bias_p
    if first is not None:
        s = jnp.where(jnp.logical_and(first, jnp.logical_not(own)), NEG_INF, s)
    m = jnp.maximum(jnp.max(s, axis=0, keepdims=True), sink_row)
    p = jnp.exp(s - m)
    es = jnp.exp(sink_row - m)
    inv = 1.0 / (jnp.sum(p, axis=0, keepdims=True) + es)
    return p * inv, es * inv


def _pool_d(ext, cur, g, row0):
    w = POOL_WINDOWS[g]
    acc = ext
    k = 1
    while k < w:
        acc = acc + pltpu.roll(acc, k, 0)
        k *= 2
    return _window_mean(acc[POOL_HALO:, :], w, row0) - cur


def _window_mean(total, w, row0):
    t = row0 + lax.broadcasted_iota(jnp.int32, (POOL_HALO, total.shape[1]), 0)
    head = total[:POOL_HALO] / jnp.minimum(t + 1, w).astype(F32)
    return jnp.concatenate([head, total[POOL_HALO:] * (1.0 / w)], axis=0)


def _mixers_fwd(zp, q, kv, pool_w, pool_b, pool_scale, bias_t, sink_row):
    s_len = zp.shape[0]
    tq = min(512, s_len)
    nb = tq // BLOCK

    def body(zp_ref, zph_ref, q_ref, kv_ref, kvh_ref, pw_ref, pb_ref, ps_ref, bias_ref, sink_ref,
             pm_ref, o_ref, ka_ref, vd_ref):
        i = pl.program_id(0)
        cur = zp_ref[...].astype(F32)
        halo = zph_ref[...].astype(F32) * (i > 0).astype(F32)
        ext = jnp.concatenate([halo, cur], axis=0)
        for g in range(4):
            sl = slice(g * POOL_GROUP_DIM, (g + 1) * POOL_GROUP_DIM)
            d = _pool_d(ext[:, sl], cur[:, sl], g, i * tq)
            y = _dot(d.astype(BF16), pw_ref[g]) + pb_ref[:, sl]
            pm_ref[:, sl] = (y * ps_ref[:, sl]).astype(BF16)
        _fill_kv_slabs(kvh_ref, kv_ref, ka_ref, vd_ref)
        own = _own_block_mask()
        for b in range(nb):
            rq = slice(b * BLOCK, (b + 1) * BLOCK)
            rk = slice(b * BLOCK, (b + 2) * BLOCK)
            qb = q_ref[rq, :]
            for h in range(N_KV_HEADS):
                pn, _ = _attn_probs(ka_ref[h, rk, :], _stack_pairs(qb, h), bias_ref[h], sink_ref[h], own,
                                    (i == 0) if b == 0 else None)
                pn = _unpack_keys(pn, own).astype(BF16)
                vd = vd_ref[h, rk, :]
                left = _left_half(vd.shape)
                zero = jnp.zeros_like(vd)
                va, vb = jnp.where(left, vd, zero), jnp.where(left, zero, vd)
                for j in range(2):
                    o_pair = (_dot_tn(pn[:, (2 * j) * BLOCK:(2 * j + 1) * BLOCK], va)
                              + _dot_tn(pn[:, (2 * j + 1) * BLOCK:(2 * j + 2) * BLOCK], vb))
                    o_ref[rq, h * 256 + j * LANES:h * 256 + (j + 1) * LANES] = o_pair.astype(BF16)

    halo_pool = pl.BlockSpec((POOL_HALO, 512), lambda i: (jnp.maximum(i * (tq // POOL_HALO) - 1, 0), 0))
    halo_kv = pl.BlockSpec((BLOCK, 256), lambda i: (jnp.maximum(i * nb - 1, 0), 0))
    return pl.pallas_call(
        body, name="mixers_fwd", grid=(s_len // tq,),
        in_specs=[_rows(tq, 512), halo_pool, _rows(tq, 512), _rows(tq, 256), halo_kv,
                  _whole((4, 128, 128)), _whole((1, 512)), _whole((1, 512)),
                  _whole((N_KV_HEADS, BLOCK, GROUP * BLOCK)), _whole((N_KV_HEADS, 1, GROUP * BLOCK))],
        out_specs=[_rows(tq, 512), _rows(tq, 512)],
        out_shape=[jax.ShapeDtypeStruct((s_len, 512), BF16)] * 2,
        scratch_shapes=[pltpu.VMEM((N_KV_HEADS, tq + BLOCK, LANES), BF16)] * 2,
        compiler_params=_params(),
    )(zp, zp, q, kv, kv, pool_w, pool_b, pool_scale, bias_t, sink_row)


def _gated_mix(pm, o, zg, pp_ref, pa_ref):
    yp = _dot(pm, pp_ref[...])
    ya = _dot(o, pa_ref[...])
    gp = jax.nn.sigmoid(zg[:, :D_MODEL].astype(F32))
    ga = jax.nn.sigmoid(zg[:, D_MODEL:].astype(F32))
    return yp, ya, gp, ga


def _core(x, pm, o, zg, tgt, g_mlp, g_fin, p_pool, p_attn, w_out, w_up_blocks, w_down):
    s_len = x.shape[0]
    tm = min(256, s_len)
    n_chunks = D_FF // FF_CHUNK
    up_block = D_FF // N_DEV
    per_chunk = FF_CHUNK // up_block

    def body(x_ref, pm_ref, o_ref, zg_ref, tgt_ref, gm_ref, gf_ref, pp_ref, pa_ref, wo_ref, wu_ref, wd_ref,
             mixed_ref, dh1_ref, a_ref, dap_ref, u2_ref, dh2_ref, small_ref,
             dyp_ref, dya_ref, dzg_ref, dpm_ref, do_ref):
        i = pl.program_id(0)

        @pl.when(i == 0)
        def _():
            small_ref[...] = jnp.zeros_like(small_ref)

        yp, ya, gp, ga = _gated_mix(pm_ref[...], o_ref[...], zg_ref[...], pp_ref, pa_ref)
        mixed = (gp * yp + ga * ya).astype(BF16)
        mixed_ref[...] = mixed
        h1 = x_ref[...] + _dot(mixed, wo_ref[...])
        r2, xh2, u2 = _rms_fwd(h1, gm_ref[...])
        u2 = u2.astype(BF16)
        u2_ref[...] = u2
        acc = jnp.zeros((tm, D_MODEL), F32)
        for c in range(n_chunks):
            cs = slice(c * FF_CHUNK, (c + 1) * FF_CHUNK)
            a = jnp.concatenate([_dot(u2, wu_ref[per_chunk * c + j]) for j in range(per_chunk)], axis=1)
            a = jnp.maximum(a, 0.0)
            a_ref[:, cs] = a.astype(BF16)
            acc = acc + _dot((a * a).astype(BF16), wd_ref[cs, :])
        h2 = h1 + acc
        r3, xh3, y = _rms_fwd(h2, gf_ref[...])
        diff = y - tgt_ref[...]
        small_ref[2:3, :] += 0.5 * jnp.sum(jnp.mean(diff * diff, axis=-1, keepdims=True))
        dy = diff * (1.0 / D_MODEL)
        dh2, dgf = _rms_bwd(dy, xh3, r3, gf_ref[...])
        small_ref[1:2, :] += dgf
        dh2_bf = dh2.astype(BF16)
        dh2_ref[...] = dh2_bf
        du2 = jnp.zeros((tm, D_MODEL), F32)
        for c in range(n_chunks):
            cs = slice(c * FF_CHUNK, (c + 1) * FF_CHUNK)
            ds = _dot_nt(dh2_bf, wd_ref[cs, :])
            dap = (ds * (2.0 * a_ref[:, cs].astype(F32))).astype(BF16)
            dap_ref[:, cs] = dap
            for j in range(per_chunk):
                du2 = du2 + _dot_nt(dap[:, j * up_block:(j + 1) * up_block], wu_ref[per_chunk * c + j])
        dh1n, dgm = _rms_bwd(du2, xh2, r2, gm_ref[...])
        small_ref[0:1, :] += dgm
        dh1 = dh2 + dh1n
        dh1_ref[...] = dh1
        dm = _dot_nt(dh1.astype(BF16), wo_ref[...])
        dyp = (dm * gp).astype(BF16)
        dya = (dm * ga).astype(BF16)
        dyp_ref[...] = dyp
        dya_ref[...] = dya
        dzg_ref[:, :D_MODEL] = (dm * yp * (gp * (1.0 - gp))).astype(BF16)
        dzg_ref[:, D_MODEL:] = (dm * ya * (ga * (1.0 - ga))).astype(BF16)
        dpm_ref[...] = _dot_nt(dyp, pp_ref[...]).astype(BF16)
        do_ref[...] = _dot_nt(dya, pa_ref[...]).astype(BF16)

    def fixed(shape):
        return pl.BlockSpec(shape, lambda i: (0,) * len(shape), pipeline_mode=pl.Buffered(1))

    widths_dtypes = ((D_MODEL, BF16), (D_MODEL, F32), (D_FF, BF16), (D_FF, BF16), (D_MODEL, BF16), (D_MODEL, BF16))
    back = ((D_MODEL, BF16), (D_MODEL, BF16), (2048, BF16), (512, BF16), (512, BF16))
    return pl.pallas_call(
        body, name="core", grid=(s_len // tm,),
        in_specs=[_rows(tm, D_MODEL), _rows(tm, 512), _rows(tm, 512), _rows(tm, 2048), _rows(tm, D_MODEL),
                  _whole((1, D_MODEL)), _whole((1, D_MODEL)),
                  fixed((512, D_MODEL)), fixed((512, D_MODEL)), fixed((D_MODEL, D_MODEL)),
                  fixed((N_DEV, D_MODEL, up_block)), fixed((D_FF, D_MODEL))],
        out_specs=[_rows(tm, n) for n, _ in widths_dtypes] + [_whole((8, D_MODEL))] + [_rows(tm, n) for n, _ in back],
        out_shape=[jax.ShapeDtypeStruct((s_len, n), d) for n, d in widths_dtypes]
        + [jax.ShapeDtypeStruct((8, D_MODEL), F32)] + [jax.ShapeDtypeStruct((s_len, n), d) for n, d in back],
        compiler_params=_params(),
    )(x, pm, o, zg, tgt, g_mlp, g_fin, p_pool, p_attn, w_out, w_up_blocks, w_down)


def _tn_matmul(a, b, square_a=False, col_blocks=None, after=None):
    s_len, ka = a.shape
    nb = b.shape[1]
    tt = min(2048, s_len)
    tk = min(1024, ka)
    tn = min(1024, nb)
    n_t = s_len // tt
    if col_blocks is None:
        out_spec = pl.BlockSpec((tk, tn), lambda k, j, t: (k, j))
        out_shape = jax.ShapeDtypeStruct((ka, nb), BF16)
    else:
        width = nb // col_blocks
        per_tile = tn // width
        out_spec = pl.BlockSpec((per_tile, tk, width), lambda k, j, t: (j, k, 0))
        out_shape = jax.ShapeDtypeStruct((col_blocks, ka, width), BF16)

    extra = [] if after is None else [after]

    def body(a_ref, b_ref, *rest):
        o_ref, acc_ref = rest[len(extra):]
        t = pl.program_id(2)

        @pl.when(t == 0)
        def _():
            acc_ref[...] = jnp.zeros_like(acc_ref)

        av = a_ref[...]
        if square_a:
            av = av * av
        acc_ref[...] += _dot_tn(av.astype(BF16), b_ref[...].astype(BF16))

        @pl.when(t == n_t - 1)
        def _():
            if col_blocks is None:
                o_ref[...] = acc_ref[...].astype(o_ref.dtype)
            else:
                for blk in range(per_tile):
                    o_ref[blk] = acc_ref[:, blk * width:(blk + 1) * width].astype(o_ref.dtype)

    return pl.pallas_call(
        body, name="tn_matmul", grid=(ka // tk, nb // tn, n_t),
        in_specs=[pl.BlockSpec((tt, tk), lambda k, j, t: (t, k)), pl.BlockSpec((tt, tn), lambda k, j, t: (t, j))]
        + [ANY] * len(extra),
        out_specs=out_spec, out_shape=out_shape,
        scratch_shapes=[pltpu.VMEM((tk, tn), F32)],
        compiler_params=_params(3),
    )(a, b, *extra)


def _tn_w_in(u, dzp, dq, dkv, dzg):
    s_len = u.shape[0]
    tt = min(1024, s_len)
    n_t = s_len // tt
    width = IN_WIDTH // N_DEV
    pieces = ((0, 512), (512, 1024), (1024, 1280), (1280, IN_WIDTH))

    def body(u_ref, dzp_ref, dq_ref, dkv_ref, dzg_ref, o_ref, acc_ref):
        t = pl.program_id(0)

        @pl.when(t == 0)
        def _():
            acc_ref[...] = jnp.zeros_like(acc_ref)

        uv = u_ref[...]
        for (c0, c1), ref in zip(pieces, (dzp_ref, dq_ref, dkv_ref, dzg_ref)):
            acc_ref[c0:c1, :] += _dot_tn(ref[...], uv)

        @pl.when(t == n_t - 1)
        def _():
            for j in range(N_DEV):
                o_ref[j] = acc_ref[j * width:(j + 1) * width, :].astype(BF16)

    return pl.pallas_call(
        body, name="tn_w_in", grid=(n_t,),
        in_specs=[_rows(tt, D_MODEL)] + [_rows(tt, c1 - c0) for c0, c1 in pieces],
        out_specs=_whole((N_DEV, width, D_MODEL)),
        out_shape=jax.ShapeDtypeStruct((N_DEV, width, D_MODEL), BF16),
        scratch_shapes=[pltpu.VMEM((IN_WIDTH, D_MODEL), F32)],
        compiler_params=_params(),
    )(u, dzp, dq, dkv, dzg)


MIX_POOL_B = 4 * POOL_GROUP_DIM
MIX_SINKS = MIX_POOL_B + 8
MIX_ROWS = MIX_SINKS + 8


def _mixers_bwd(after, zp, q, kv, dpm, do, pool_w, pool_b, pool_scale, bias_t, sink_row):
    s_len = zp.shape[0]
    tq = min(512, s_len)
    nb = tq // BLOCK
    n_steps = s_len // tq

    def body(after_ref, zp_ref, zph_ref, q_ref, kv_ref, kvh_ref, dpm_ref, dpmh_ref, do_ref, pw_ref, pb_ref, ps_ref,
             bias_ref, sink_ref, dzp_ref, dq_ref, dkv_ref, small_ref, dps_ref,
             ka_ref, vd_ref, dsk_acc, dkv_acc):
        i = pl.program_id(0)

        @pl.when(i == 0)
        def _():
            dkv_acc[...] = jnp.zeros_like(dkv_acc)
            small_ref[...] = jnp.zeros_like(small_ref)
            dps_ref[...] = jnp.zeros_like(dps_ref)
            dsk_acc[...] = jnp.zeros_like(dsk_acc)

        cur = zp_ref[...].astype(F32)
        halo = zph_ref[...].astype(F32) * (i > 0).astype(F32)
        ext = jnp.concatenate([halo, cur], axis=0)
        dpm_next = dpmh_ref[...].astype(F32) * (i < n_steps - 1).astype(F32)
        dpm_ext = jnp.concatenate([dpm_ref[...].astype(F32), dpm_next], axis=0)
        n_ext = tq + POOL_HALO
        for g in range(4):
            sl = slice(g * POOL_GROUP_DIM, (g + 1) * POOL_GROUP_DIM)
            w = POOL_WINDOWS[g]
            d = _pool_d(ext[:, sl], cur[:, sl], g, i * tq).astype(BF16)
            y_lin = _dot(d, pw_ref[g]) + pb_ref[:, sl]
            dps_ref[:, sl] += jnp.sum(dpm_ext[:tq, sl] * y_lin, axis=0, keepdims=True)
            dyl_ext = dpm_ext[:, sl] * ps_ref[:, sl]
            small_ref[MIX_POOL_B + g:MIX_POOL_B + g + 1, :] += jnp.sum(dyl_ext[:tq], axis=0, keepdims=True)
            dyl_bf = dyl_ext.astype(BF16)
            small_ref[g * POOL_GROUP_DIM:(g + 1) * POOL_GROUP_DIM, :] += _dot_tn(d, dyl_bf[:tq])
            dd = _dot_nt(dyl_bf, pw_ref[g])
            e = _window_mean(dd, w, i * tq)
            acc = e
            k = 1
            while k < w:
                acc = acc + pltpu.roll(acc, n_ext - k, 0)
                k *= 2
            dzp_ref[:, sl] = (acc[:tq] - dd[:tq]).astype(BF16)

        _fill_kv_slabs(kvh_ref, kv_ref, ka_ref, vd_ref)

        def fold(dup):
            return dup + pltpu.roll(dup, HEAD_DIM, 1)

        own = _own_block_mask()
        for b in range(nb):
            rq = slice(b * BLOCK, (b + 1) * BLOCK)
            rk = slice(b * BLOCK, (b + 2) * BLOCK)
            qb = q_ref[rq, :]
            dob = do_ref[rq, :]
            dk_dup, dv_dup = [], []
            for h in range(N_KV_HEADS):
                kk = ka_ref[h, rk, :]
                q_st = _stack_pairs(qb, h)
                do_st = _stack_pairs(dob, h)
                pn, psink = _attn_probs(kk, q_st, bias_ref[h], sink_ref[h], own, (i == 0) if b == 0 else None)
                dp = _pack_keys(_dot_nt(vd_ref[h, rk, :], do_st), own)
                delta = jnp.sum(pn * dp, axis=0, keepdims=True)
                dsk_acc[h] += -psink * delta
                ds = _unpack_keys((pn * (dp - delta)) * ATTN_SCALE, own).astype(BF16)
                pn = _unpack_keys(pn, own)
                dq_st = _dot_tn(ds, kk)
                for j in range(2):
                    left = _left_half((BLOCK, LANES))
                    dq_pair = jnp.where(left, dq_st[(2 * j) * BLOCK:(2 * j + 1) * BLOCK],
                                        dq_st[(2 * j + 1) * BLOCK:(2 * j + 2) * BLOCK])
                    dq_ref[rq, h * 256 + j * LANES:h * 256 + (j + 1) * LANES] = dq_pair.astype(BF16)
                dk_dup.append(fold(_dot(ds, q_st)))
                dv_dup.append(fold(_dot(pn.astype(BF16), do_st)))
            left = _left_half((2 * BLOCK, LANES))
            dkv_blk = jnp.concatenate([jnp.where(left, dk_dup[0], dk_dup[1]),
                                       jnp.where(left, dv_dup[0], dv_dup[1])], axis=1)
            g0 = pl.multiple_of(i * tq + b * BLOCK, BLOCK)
            dkv_acc[pl.ds(g0, 2 * BLOCK), :] += dkv_blk

        @pl.when(i == n_steps - 1)
        def _():
            dkv_ref[...] = dkv_acc[BLOCK:, :].astype(BF16)
            lane = lax.broadcasted_iota(jnp.int32, (1, LANES), 1)
            row = jnp.zeros((1, LANES), F32)
            for h in range(N_KV_HEADS):
                for g in range(GROUP):
                    tot = jnp.sum(dsk_acc[h, :, g * BLOCK:(g + 1) * BLOCK], axis=1, keepdims=True)
                    row = jnp.where(lane == GROUP * h + g, tot, row)
            small_ref[MIX_SINKS:MIX_SINKS + 1, :] = row

    blocks_per_tile = tq // POOL_HALO
    last_halo = s_len // POOL_HALO - 1
    halo_prev = pl.BlockSpec((POOL_HALO, 512), lambda i: (jnp.maximum(i * blocks_per_tile - 1, 0), 0))
    halo_next = pl.BlockSpec((POOL_HALO, 512), lambda i: (jnp.minimum((i + 1) * blocks_per_tile, last_halo), 0))
    halo_kv = pl.BlockSpec((BLOCK, 256), lambda i: (jnp.maximum(i * nb - 1, 0), 0))
    return pl.pallas_call(
        body, name="mixers_bwd", grid=(n_steps,),
        in_specs=[ANY, _rows(tq, 512), halo_prev, _rows(tq, 512), _rows(tq, 256), halo_kv,
                  _rows(tq, 512), halo_next, _rows(tq, 512),
                  _whole((4, 128, 128)), _whole((1, 512)), _whole((1, 512)),
                  _whole((N_KV_HEADS, BLOCK, GROUP * BLOCK)), _whole((N_KV_HEADS, 1, GROUP * BLOCK))],
        out_specs=[_rows(tq, 512), _rows(tq, 512), _whole((s_len, 256)),
                   _whole((MIX_ROWS, LANES)), _whole((1, 512))],
        out_shape=[jax.ShapeDtypeStruct((s_len, 512), BF16), jax.ShapeDtypeStruct((s_len, 512), BF16),
                   jax.ShapeDtypeStruct((s_len, 256), BF16), jax.ShapeDtypeStruct((MIX_ROWS, LANES), F32),
                   jax.ShapeDtypeStruct((1, 512), F32)],
        scratch_shapes=[pltpu.VMEM((N_KV_HEADS, tq + BLOCK, LANES), BF16)] * 2
        + [pltpu.VMEM((N_KV_HEADS, 1, GROUP * BLOCK), F32), pltpu.VMEM((s_len + BLOCK, 256), F32)],
        compiler_params=_params(),
    )(after, zp, zp, q, kv, kv, dpm, dpm, do, pool_w, pool_b, pool_scale, bias_t, sink_row)


def _in_bwd(after, dzp, dq, dkv, dzg, w_in_t, x, dh1, g_mix):
    s_len = x.shape[0]
    tm = min(512, s_len)

    def body(after_ref, dzp_ref, dq_ref, dkv_ref, dzg_ref, w_ref, x_ref, dh1_ref, g_ref, dx_ref, dg_ref):
        i = pl.program_id(0)

        @pl.when(i == 0)
        def _():
            dg_ref[...] = jnp.zeros_like(dg_ref)

        du = _dot(dzp_ref[...], w_ref[0:512, :])
        du = du + _dot(dq_ref[...], w_ref[512:1024, :])
        du = du + _dot(dkv_ref[...], w_ref[1024:1280, :])
        du = du + _dot(dzg_ref[...], w_ref[1280:3328, :])
        r, xh, _ = _rms_fwd(x_ref[...], g_ref[...])
        dxn, dg = _rms_bwd(du, xh, r, g_ref[...])
        dg_ref[...] += dg
        dx_ref[...] = dh1_ref[...] + dxn

    return pl.pallas_call(
        body, name="in_bwd", grid=(s_len // tm,),
        in_specs=[ANY, _rows(tm, 512), _rows(tm, 512), _rows(tm, 256), _rows(tm, 2048), _whole((IN_WIDTH, D_MODEL)),
                  _rows(tm, D_MODEL), _rows(tm, D_MODEL), _whole((1, D_MODEL))],
        out_specs=[_rows(tm, D_MODEL), _whole((1, D_MODEL))],
        out_shape=[jax.ShapeDtypeStruct((s_len, D_MODEL), F32), jax.ShapeDtypeStruct((1, D_MODEL), F32)],
        compiler_params=_params(),
    )(after, dzp, dq, dkv, dzg, w_in_t, x, dh1, g_mix)


def _all_gather_weights(name, shards, after=None):
    n = len(shards)
    extra = [] if after is None else [after]
    n_extra = len(extra)

    def body(*refs):
        ins, outs = refs[:n], refs[n + n_extra:2 * n + n_extra]
        send_sems, recv_sems, local_sems = refs[2 * n + n_extra:]
        x, y, c = lax.axis_index("x"), lax.axis_index("y"), lax.axis_index("c")
        me, sibling = (x, y, c), (x, y, 1 - c)
        chips = [(1 - x, y), (x, 1 - y), (1 - x, 1 - y)]

        def slot(a, px, py, pc):
            return outs[a].at[4 * px + 2 * py + pc]

        def copy(a, k, block, to, src=None):
            return pltpu.make_async_remote_copy(
                src_ref=slot(a, *block) if src is None else src, dst_ref=slot(a, *block),
                send_sem=send_sems.at[a, k], recv_sem=recv_sems.at[a, k], device_id=to, device_id_type=MESH)

        mine = [pltpu.make_async_copy(ins[a], slot(a, *me), local_sems.at[a]) for a in range(n)]
        for cp in mine:
            cp.start()
        first = []
        for a in range(n):
            first.append(copy(a, 0, me, sibling, src=ins[a]))
            first += [copy(a, 1 + j, me, (*chip, c), src=ins[a]) for j, chip in enumerate(chips)]
        for cp in first:
            cp.start()
        passed = []
        for a in range(n):
            for j, chip in enumerate(chips):
                copy(a, 1 + j, (*chip, c), me).wait_recv()
                cp = copy(a, 4 + j, (*chip, c), sibling)
                cp.start()
                passed.append(cp)
        for a in range(n):
            copy(a, 0, sibling, me).wait_recv()
            for j, chip in enumerate(chips):
                copy(a, 4 + j, (*chip, 1 - c), me).wait_recv()
        for cp in first + passed:
            cp.wait_send()
        for cp in mine:
            cp.wait()

    return pl.pallas_call(
        body, name=name,
        in_specs=[ANY] * (n + n_extra), out_specs=[ANY] * n,
        out_shape=[jax.ShapeDtypeStruct((N_DEV,) + s.shape, s.dtype) for s in shards],
        scratch_shapes=[pltpu.SemaphoreType.DMA((n, 7)), pltpu.SemaphoreType.DMA((n, 7)), pltpu.SemaphoreType.DMA((n,))],
    )(*shards, *extra)


HBM_SPEC = pl.BlockSpec(memory_space=pltpu.HBM)
SEM_SPEC = pl.BlockSpec(memory_space=pltpu.SEMAPHORE)
DATAFLOW = pltpu.SideEffectType.DATAFLOW_SIDE_EFFECTING
N_PEERS = N_DEV - 1


CHIP_PEERS = (1, 2, 4, 6)
RELAYED = (2, 4, 6)


def _peer_copies(srcs, lands, scatter, send_sems, recv_sems):
    x, y, c = lax.axis_index("x"), lax.axis_index("y"), lax.axis_index("c")
    me_idx = 4 * x + 2 * y + c
    copies = []
    for k in range(1, N_DEV):
        px = 1 - x if (k >> 2) & 1 else x
        py = 1 - y if (k >> 1) & 1 else y
        pc = 1 - c if k & 1 else c
        p_idx = 4 * px + 2 * py + pc
        for a in range(len(srcs)):
            if scatter[a] == "chip" and k not in CHIP_PEERS:
                continue
            src = srcs[a].at[p_idx] if scatter[a] is True else srcs[a]
            dst = lands[a].at[k] if scatter[a] is True else lands[a].at[me_idx]
            copies.append(pltpu.make_async_remote_copy(
                src_ref=src, dst_ref=dst, send_sem=send_sems.at[a * N_PEERS + k - 1],
                recv_sem=recv_sems.at[a * N_PEERS + k - 1],
                device_id=(px, py, pc), device_id_type=MESH))
    return copies


def _exchange_start(name, srcs, scatter, after):
    n = len(srcs)
    lands = [lax.empty(s.shape if sc is True else (N_DEV,) + s.shape, s.dtype) for s, sc in zip(srcs, scatter)]

    def body(*refs):
        src_refs, land_refs = refs[:n], refs[n:2 * n]
        send_sems, recv_sems = refs[2 * n + 1], refs[2 * n + 2]
        token = refs[4 * n + 3]
        for cp in _peer_copies(src_refs, land_refs, scatter, send_sems, recv_sems):
            cp.start()
        token[...] = jnp.zeros_like(token)

    hbm = lambda t: pltpu.HBM(t.shape, t.dtype)
    outs = pl.pallas_call(
        body, name=name,
        out_shape=[pltpu.SemaphoreType.DMA((n * N_PEERS,)), pltpu.SemaphoreType.DMA((n * N_PEERS,))]
        + [hbm(t) for t in srcs] + [hbm(t) for t in lands] + [jax.ShapeDtypeStruct((8, LANES), F32)],
        in_specs=[HBM_SPEC] * (2 * n) + [ANY],
        out_specs=[SEM_SPEC, SEM_SPEC] + [HBM_SPEC] * (2 * n) + [pl.BlockSpec(memory_space=pltpu.VMEM)],
        input_output_aliases={i: 2 + i for i in range(2 * n)},
        compiler_params=pltpu.CompilerParams(has_side_effects=DATAFLOW),
    )(*[pltpu.with_memory_space_constraint(t, pltpu.HBM) for t in list(srcs) + lands], after)
    return dict(n=n, scatter=scatter, send_sems=outs[0], recv_sems=outs[1], srcs=outs[2:2 + n],
                lands=outs[2 + n:2 + 2 * n], token=outs[2 + 2 * n])


def _exchange_wait(name, handle, after):
    n, scatter = handle["n"], handle["scatter"]

    def body(*refs):
        src_refs, land_refs = refs[:n], refs[n:2 * n]
        send_sems, recv_sems = refs[2 * n], refs[2 * n + 1]
        for cp in _peer_copies(src_refs, land_refs, scatter, send_sems, recv_sems):
            cp.wait_send()
            cp.wait_recv()

    both = list(handle["srcs"]) + list(handle["lands"])
    outs = pl.pallas_call(
        body, name=name,
        out_shape=[pltpu.HBM(t.shape, t.dtype) for t in both],
        in_specs=[HBM_SPEC] * (2 * n) + [SEM_SPEC, SEM_SPEC, ANY],
        out_specs=[HBM_SPEC] * (2 * n),
        input_output_aliases={i: i for i in range(2 * n)},
        compiler_params=pltpu.CompilerParams(has_side_effects=DATAFLOW),
    )(*both, handle["send_sems"], handle["recv_sems"], after)
    me_idx = _my_index()
    lands = [land if sc is True else lax.dynamic_update_index_in_dim(land, src, me_idx, 0)
             for land, src, sc in zip(outs[n:], outs[:n], scatter)]
    return lands, outs[:n]


def _my_index():
    return 4 * lax.axis_index("x") + 2 * lax.axis_index("y") + lax.axis_index("c")


def _relay_copies(bufs, send_sems, recv_sems):
    x, y, c = lax.axis_index("x"), lax.axis_index("y"), lax.axis_index("c")
    copies = []
    for j, k in enumerate(RELAYED):
        px = 1 - x if (k >> 2) & 1 else x
        py = 1 - y if (k >> 1) & 1 else y
        slot = 4 * px + 2 * py + c
        for a, buf in enumerate(bufs):
            copies.append(pltpu.make_async_remote_copy(
                src_ref=buf.at[slot], dst_ref=buf.at[slot], send_sem=send_sems.at[a * len(RELAYED) + j],
                recv_sem=recv_sems.at[a * len(RELAYED) + j], device_id=(x, y, 1 - c), device_id_type=MESH))
    return copies


def _relay_start(name, bufs, after):
    n = len(bufs)

    def body(*refs):
        send_sems, recv_sems = refs[n + 1], refs[n + 2]
        for cp in _relay_copies(refs[:n], send_sems, recv_sems):
            cp.start()
        token = refs[2 * n + 3]
        token[...] = jnp.zeros_like(token)

    n_sems = n * len(RELAYED)
    outs = pl.pallas_call(
        body, name=name,
        out_shape=[pltpu.SemaphoreType.DMA((n_sems,)), pltpu.SemaphoreType.DMA((n_sems,))]
        + [pltpu.HBM(t.shape, t.dtype) for t in bufs] + [jax.ShapeDtypeStruct((8, LANES), F32)],
        in_specs=[HBM_SPEC] * n + [ANY],
        out_specs=[SEM_SPEC, SEM_SPEC] + [HBM_SPEC] * n + [pl.BlockSpec(memory_space=pltpu.VMEM)],
        input_output_aliases={i: 2 + i for i in range(n)},
        compiler_params=pltpu.CompilerParams(has_side_effects=DATAFLOW),
    )(*[pltpu.with_memory_space_constraint(t, pltpu.HBM) for t in bufs], after)
    return dict(n=n, send_sems=outs[0], recv_sems=outs[1], bufs=outs[2:2 + n], token=outs[2 + n])


def _relay_wait(name, handle, after):
    n = handle["n"]

    def body(*refs):
        for cp in _relay_copies(refs[:n], refs[n], refs[n + 1]):
            cp.wait_send()
            cp.wait_recv()

    return pl.pallas_call(
        body, name=name,
        out_shape=[pltpu.HBM(t.shape, t.dtype) for t in handle["bufs"]],
        in_specs=[HBM_SPEC] * n + [SEM_SPEC, SEM_SPEC, ANY],
        out_specs=[HBM_SPEC] * n,
        input_output_aliases={i: i for i in range(n)},
        compiler_params=pltpu.CompilerParams(has_side_effects=DATAFLOW),
    )(*handle["bufs"], handle["send_sems"], handle["recv_sems"], after)


def _adamw(parts, w, m, v, sent=None):
    r, c = w.shape
    tr = 256 if r % 256 == 0 else r
    own = sent is not None

    def body(*refs):
        if own:
            _, p_ref, own_ref, w_ref, m_ref, v_ref, g_ref, d_ref, nm_ref, nv_ref = refs
            g = own_ref[...].astype(F32)
        else:
            p_ref, w_ref, m_ref, v_ref, g_ref, d_ref, nm_ref, nv_ref = refs
            g = p_ref[0].astype(F32)
        for k in range(1, N_DEV):
            g = g + p_ref[k].astype(F32)
        m_new = ADAM_B1 * m_ref[...] + (1.0 - ADAM_B1) * g
        v_new = ADAM_B2 * v_ref[...] + (1.0 - ADAM_B2) * (g * g)
        m_hat = m_new / (1.0 - ADAM_B1 ** ADAM_STEP)
        v_hat = v_new / (1.0 - ADAM_B2 ** ADAM_STEP)
        g_ref[...] = g
        d_ref[...] = -ADAM_LR * (m_hat / (jnp.sqrt(v_hat) + ADAM_EPS) + ADAM_WD * w_ref[...])
        nm_ref[...] = m_new
        nv_ref[...] = v_new

    out_shape = [jax.ShapeDtypeStruct((r, c), F32)] * 4
    if not own:
        return pl.pallas_call(
            body, name="adamw", grid=(r // tr,),
            in_specs=[pl.BlockSpec((N_DEV, tr, c), lambda i: (0, i, 0))] + [_rows(tr, c)] * 3,
            out_specs=[_rows(tr, c)] * 4, out_shape=out_shape, compiler_params=_params(),
        )(parts, w, m, v)
    rows = pl.BlockSpec((tr, c), lambda i, me: (i, 0))
    return pl.pallas_call(
        body, name="adamw_own", out_shape=out_shape, compiler_params=_params(),
        grid_spec=pltpu.PrefetchScalarGridSpec(
            num_scalar_prefetch=1, grid=(r // tr,),
            in_specs=[pl.BlockSpec((N_DEV, tr, c), lambda i, me: (0, i, 0)),
                      pl.BlockSpec((None, tr, c), lambda i, me: (me[0], i, 0))] + [rows] * 3,
            out_specs=[rows] * 4),
    )(_my_index().reshape(1).astype(jnp.int32), parts, sent, w, m, v)


def _adam_step(g, w, m, v):
    m_new = ADAM_B1 * m + (1.0 - ADAM_B1) * g
    v_new = ADAM_B2 * v + (1.0 - ADAM_B2) * (g * g)
    m_hat = m_new / (1.0 - ADAM_B1 ** ADAM_STEP)
    v_hat = v_new / (1.0 - ADAM_B2 ** ADAM_STEP)
    return -ADAM_LR * (m_hat / (jnp.sqrt(v_hat) + ADAM_EPS) + ADAM_WD * w), m_new, v_new


SMALL_NAMES = ("norm_mix", "pool_w", "pool_b", "pool_scale", "attn_sinks", "norm_mlp", "norm_final")


def _adamw_small(mlp_all, mix_all, scale_all, nmix_all, w, m, v):
    def body(mlp_ref, mix_ref, scale_ref, nmix_ref, *refs):
        ins, outs = refs[:21], refs[21:]

        def total(ref, rows, lanes=slice(None)):
            g = ref[0, rows, lanes]
            for k in range(1, N_DEV):
                g = g + ref[k, rows, lanes]
            return g

        grads = dict(
            norm_mix=total(nmix_ref, slice(0, 1)), pool_w=total(mix_ref, slice(0, MIX_POOL_B)),
            pool_b=total(mix_ref, slice(MIX_POOL_B, MIX_POOL_B + 4)), pool_scale=total(scale_ref, slice(0, 1)),
            attn_sinks=total(mix_ref, slice(MIX_SINKS, MIX_SINKS + 1)),
            norm_mlp=total(mlp_ref, slice(0, 1)), norm_final=total(mlp_ref, slice(1, 2)))
        for i, name in enumerate(SMALL_NAMES):
            g = grads[name]
            d, m_new, v_new = _adam_step(g, ins[3 * i][...], ins[3 * i + 1][...], ins[3 * i + 2][...])
            for ref, val in zip(outs[4 * i:4 * i + 4], (g, d, m_new, v_new)):
                ref[...] = val
        outs[28][...] = jnp.broadcast_to(total(mlp_ref, slice(2, 3), slice(0, LANES)), (8, LANES))

    operands, out_shape = [], []
    for name in SMALL_NAMES:
        operands += [w[name], m[name], v[name]]
        out_shape += [jax.ShapeDtypeStruct(w[name].shape, F32)] * 4
    out_shape.append(jax.ShapeDtypeStruct((8, LANES), F32))
    outs = pl.pallas_call(body, name="adamw_small", out_shape=out_shape)(
        mlp_all, mix_all, scale_all, nmix_all, *operands)
    return {name: outs[4 * i:4 * i + 4] for i, name in enumerate(SMALL_NAMES)}, outs[28]


def kernel(x, norm_mix, w_in, pool_w, pool_b, pool_scale, attn_sinks, p_pool, p_attn, w_out, norm_mlp, w_up, w_down, norm_final, loss_target, m_norm_mix, m_w_in, m_pool_w, m_pool_b, m_pool_scale, m_attn_sinks, m_p_pool, m_p_attn, m_w_out, m_norm_mlp, m_w_up, m_w_down, m_norm_final, v_norm_mix, v_w_in, v_pool_w, v_pool_b, v_pool_scale, v_attn_sinks, v_p_pool, v_p_attn, v_w_out, v_norm_mlp, v_w_up, v_w_down, v_norm_final):
    xs = x[0]
    tgt = loss_target[0]
    s_len = xs.shape[0]

    p_pool_bf, p_attn_bf, w_out_bf, w_up_bf, w_down_bf = [
        t[0].astype(BF16) for t in (p_pool, p_attn, w_out, w_up, w_down)]
    w_in_bf = w_in[0].T.astype(BF16)
    (w_in_g,) = _all_gather_weights("all_gather_w_in", [w_in_bf])
    ag_rest = _exchange_start(
        "ag_rest_start", [p_pool_bf, p_attn_bf, w_out_bf, w_up_bf, w_down_bf], ("chip",) * 5, w_in_g)

    pool_w_bf = pool_w[0].astype(BF16)
    pool_b_row = pool_b[0].reshape(1, POOL_WIDTH)
    bias_t, sink_row = _attn_constants(attn_sinks[0])

    w_in_t = w_in_g.reshape(IN_WIDTH, D_MODEL)
    u, zp, q, kv, zg = _fwd_in(ag_rest["token"], xs, norm_mix, w_in_t)
    pm, o = _mixers_fwd(zp, q, kv, pool_w_bf, pool_b_row, pool_scale, bias_t, sink_row)
    first_level, _ = _exchange_wait("ag_rest_wait", ag_rest, o)
    relay = _relay_start("ag_relay_start", first_level, pm)
    p_pool_g, p_attn_g, w_out_g, w_up_g, w_down_g = _relay_wait("ag_relay_wait", relay, relay["token"])
    p_pool_f = p_pool_g.transpose(1, 0, 2).reshape(POOL_WIDTH, D_MODEL)
    p_attn_f = p_attn_g.transpose(1, 0, 2).reshape(ATTN_WIDTH, D_MODEL)
    w_out_f = w_out_g.reshape(D_MODEL, D_MODEL)
    w_down_f = w_down_g.reshape(D_FF, D_MODEL)
    mixed, dh1, a, dapre, u2, dh2, small_mlp, dyp, dya, dzg, dpm, do = _core(
        xs, pm, o, zg, tgt, norm_mlp, norm_final.reshape(1, D_MODEL), p_pool_f, p_attn_f, w_out_f, w_up_g, w_down_f)
    gw_down = _tn_matmul(a, dh2, square_a=True)
    gw_up = _tn_matmul(u2, dapre, col_blocks=N_DEV)
    ex_mlp = _exchange_start(
        "ex_mlp_start", [gw_up, gw_down.reshape(N_DEV, D_FF // N_DEV, D_MODEL)], (True, True), small_mlp)
    gw_out = _tn_matmul(mixed, dh1, after=ex_mlp["token"])
    gp_pool = _tn_matmul(pm, dyp, col_blocks=N_DEV)
    gp_attn = _tn_matmul(o, dya, col_blocks=N_DEV)
    ex_proj = _exchange_start(
        "ex_proj_start", [gp_pool, gp_attn, gw_out.reshape(N_DEV, D_MODEL // N_DEV, D_MODEL)], (True,) * 3, small_mlp)
    dzp, dq, dkv, small_mix, g_pool_scale = _mixers_bwd(
        ex_proj["token"], zp, q, kv, dpm, do, pool_w_bf, pool_b_row, pool_scale, bias_t, sink_row)
    gw_in = _tn_w_in(u, dzp, dq, dkv, dzg)
    ex_in = _exchange_start(
        "ex_in_start", [gw_in, small_mlp, small_mix, g_pool_scale], (True, False, False, False), dq)
    dx, g_norm_mix = _in_bwd(ex_in["token"], dzp, dq, dkv, dzg, w_in_t, xs, dh1, norm_mix)

    big_w = dict(w_in=w_in, p_pool=p_pool, p_attn=p_attn, w_out=w_out, w_up=w_up, w_down=w_down)
    big_m = dict(w_in=m_w_in, p_pool=m_p_pool, p_attn=m_p_attn, w_out=m_w_out, w_up=m_w_up, w_down=m_w_down)
    big_v = dict(w_in=v_w_in, p_pool=v_p_pool, p_attn=v_p_attn, w_out=v_w_out, w_up=v_w_up, w_down=v_w_down)
    res = {}

    def update(names, recvs, sents):
        for name, parts, sent in zip(names, recvs, sents):
            flip = (lambda t: t.T) if name == "w_in" else (lambda t: t)
            outs = _adamw(parts, flip(big_w[name][0]), flip(big_m[name][0]), flip(big_v[name][0]), sent)
            res[name] = [flip(t)[None] for t in outs]

    update(["w_up", "w_down"], *_exchange_wait("ex_mlp_wait", ex_mlp, dx))
    update(["p_pool", "p_attn", "w_out"], *_exchange_wait("ex_proj_wait", ex_proj, res["w_down"][0]))
    (norm_mix_all,) = _all_gather_weights("all_gather_norm_mix", [g_norm_mix], res["w_out"][0])
    (r_in, mlp_all, mix_all, scale_all), (s_in, _, _, _) = _exchange_wait("ex_in_wait", ex_in, norm_mix_all)
    update(["w_in"], [r_in], [s_in])

    natural = dict(norm_mix=(1, D_MODEL), pool_w=(MIX_POOL_B, LANES), pool_b=(4, LANES), pool_scale=(1, POOL_WIDTH),
                   attn_sinks=(1, LANES), norm_mlp=(1, D_MODEL), norm_final=(1, D_MODEL))

    def as_2d(t, name):
        if name == "attn_sinks":
            return jnp.pad(t, ((0, 0), (0, LANES - N_HEADS)))
        return t.reshape(natural[name])

    small_w = dict(norm_mix=norm_mix, pool_w=pool_w, pool_b=pool_b, pool_scale=pool_scale, attn_sinks=attn_sinks,
                   norm_mlp=norm_mlp, norm_final=norm_final)
    small_m = dict(norm_mix=m_norm_mix, pool_w=m_pool_w, pool_b=m_pool_b, pool_scale=m_pool_scale,
                   attn_sinks=m_attn_sinks, norm_mlp=m_norm_mlp, norm_final=m_norm_final)
    small_v = dict(norm_mix=v_norm_mix, pool_w=v_pool_w, pool_b=v_pool_b, pool_scale=v_pool_scale,
                   attn_sinks=v_attn_sinks, norm_mlp=v_norm_mlp, norm_final=v_norm_final)
    small_res, loss_all = _adamw_small(
        mlp_all, mix_all, scale_all, norm_mix_all,
        *[{k: as_2d(t, k) for k, t in d.items()} for d in (small_w, small_m, small_v)])
    loss = loss_all[0, 0]
    for name in SMALL_NAMES:
        shape = small_w[name].shape
        res[name] = [(t[:, :N_HEADS] if name == "attn_sinks" else t).reshape(shape) for t in small_res[name]]

    order = ["norm_mix", "w_in", "pool_w", "pool_b", "pool_scale", "attn_sinks", "p_pool", "p_attn", "w_out",
             "norm_mlp", "w_up", "w_down", "norm_final"]
    out = [loss, dx[None]]
    for kind in range(4):
        out += [res[name][kind] for name in order]
    return tuple(out)
```

```python
import functools
import math

import numpy as np
import jax
import jax.numpy as jnp
from jax import lax
from jax.experimental import pallas as pl
from jax.experimental.pallas import tpu as pltpu

F32 = jnp.float32
BF16 = jnp.bfloat16

D_MODEL = 1024
POOL_WIDTH = 512
ATTN_WIDTH = 512
KV_WIDTH = 128
HEAD_DIM = 64
N_HEADS = 8
N_KV_HEADS = 2
GROUP = 4
BLOCK = 128
POOL_WINDOWS = (2, 4, 8, 16)
POOL_GROUP_DIM = 128
POOL_HALO = 16
D_FF = 4096
FF_CHUNK = 1024
IN_WIDTH = 3328
RMS_EPS = 1e-5
NEG_INF = -1e30
ATTN_SCALE = 1.0 / math.sqrt(HEAD_DIM)
N_DEV = 8

ADAM_LR = 0.001
ADAM_B1 = 0.9
ADAM_B2 = 0.999
ADAM_EPS = 1e-08
ADAM_WD = 0.01
ADAM_STEP = 10

LANES = 128
VMEM_LIMIT_BYTES = 56 * 1024 * 1024
MESH = pl.DeviceIdType.MESH


def _params(n_grid_axes=1):
    return pltpu.CompilerParams(
        dimension_semantics=("arbitrary",) * n_grid_axes, vmem_limit_bytes=VMEM_LIMIT_BYTES)


def _dot(a, b):
    return jnp.dot(a, b, preferred_element_type=F32)


def _dot_nt(a, b):
    return lax.dot_general(a, b, (((1,), (1,)), ((), ())), preferred_element_type=F32)


def _dot_tn(a, b):
    return lax.dot_general(a, b, (((0,), (0,)), ((), ())), preferred_element_type=F32)


ANY = pl.BlockSpec(memory_space=pl.ANY)


def _rows(tm, n):
    return pl.BlockSpec((tm, n), lambda i: (i, 0))


def _whole(shape):
    zeros = (0,) * len(shape)
    return pl.BlockSpec(shape, lambda i: zeros)


def _rms_fwd(h, g):
    r = lax.rsqrt(jnp.mean(h * h, axis=-1, keepdims=True) + RMS_EPS)
    xh = h * r
    return r, xh, xh * g


def _rms_bwd(dy, xh, r, g):
    dxh = dy * g
    dh = r * (dxh - xh * jnp.mean(dxh * xh, axis=-1, keepdims=True))
    return dh, jnp.sum(dy * xh, axis=0, keepdims=True)


def _fwd_in(after, x, g_mix, w_in_t):
    s_len = x.shape[0]
    tm = min(512, s_len)

    def body(after_ref, x_ref, g_ref, w_ref, u_ref, zp_ref, q_ref, kv_ref, zg_ref):
        _, _, u = _rms_fwd(x_ref[...], g_ref[...])
        u = u.astype(BF16)
        u_ref[...] = u
        zp_ref[...] = _dot_nt(u, w_ref[0:512, :]).astype(BF16)
        q_ref[...] = _dot_nt(u, w_ref[512:1024, :]).astype(BF16)
        kv_ref[...] = _dot_nt(u, w_ref[1024:1280, :]).astype(BF16)
        zg_ref[...] = _dot_nt(u, w_ref[1280:3328, :]).astype(BF16)

    return pl.pallas_call(
        body, name="fwd_in", grid=(s_len // tm,),
        in_specs=[ANY, _rows(tm, D_MODEL), _whole((1, D_MODEL)),
                  pl.BlockSpec((IN_WIDTH, D_MODEL), lambda i: (0, 0), pipeline_mode=pl.Buffered(1))],
        out_specs=[_rows(tm, D_MODEL), _rows(tm, 512), _rows(tm, 512), _rows(tm, 256), _rows(tm, 2048)],
        out_shape=[jax.ShapeDtypeStruct((s_len, n), BF16) for n in (D_MODEL, 512, 512, 256, 2048)],
        compiler_params=_params(),
    )(after, x, g_mix, w_in_t)


def _attn_constants(sinks):
    r = np.arange(BLOCK)[:, None]
    qi = np.arange(BLOCK)[None, :]
    dist = np.where(r <= qi, qi - r, BLOCK + qi - r).astype(np.float32)
    slopes = np.array([2.0 ** (-8.0 * (h + 1) / N_HEADS) for h in range(N_HEADS)], dtype=np.float32)
    bias = (-slopes[:, None, None] * dist[None]).reshape(N_KV_HEADS, GROUP, BLOCK, BLOCK)
    bias = np.ascontiguousarray(bias.transpose(0, 2, 1, 3)).reshape(N_KV_HEADS, BLOCK, GROUP * BLOCK)
    sink_row = jnp.repeat(sinks.astype(F32).reshape(N_KV_HEADS, GROUP), BLOCK, axis=1)[:, None, :]
    return jnp.asarray(bias.astype(np.float32)), sink_row


def _own_block_mask():
    shape = (BLOCK, GROUP * BLOCK)
    r = lax.broadcasted_iota(jnp.int32, shape, 0)
    qi = lax.broadcasted_iota(jnp.int32, shape, 1) & (BLOCK - 1)
    return r <= qi


def _pack_keys(t, own):
    return jnp.where(own, t[BLOCK:], t[:BLOCK])


def _unpack_keys(t, own):
    zero = jnp.zeros_like(t)
    return jnp.concatenate([jnp.where(own, zero, t), jnp.where(own, t, zero)], axis=0)


def _left_half(shape):
    return lax.broadcasted_iota(jnp.int32, shape, 1) < HEAD_DIM


def _dup_halves(slab):
    swapped = pltpu.roll(slab, HEAD_DIM, 1)
    left = _left_half(slab.shape)
    return jnp.where(left, slab, swapped), jnp.where(left, swapped, slab)


def _fill_kv_slabs(kvh_ref, kv_ref, ka_ref, vd_ref):
    for rows, src in ((slice(0, BLOCK), kvh_ref), (slice(BLOCK, None), kv_ref)):
        kvf = src[...].astype(F32)
        for ref, lanes in ((ka_ref, slice(0, KV_WIDTH)), (vd_ref, slice(KV_WIDTH, 2 * KV_WIDTH))):
            d0, d1 = _dup_halves(kvf[:, lanes])
            ref[0, rows, :] = d0.astype(BF16)
            ref[1, rows, :] = d1.astype(BF16)


def _stack_pairs(a, h):
    pieces = []
    for j in range(2):
        pair = a[:, h * 256 + j * LANES:h * 256 + (j + 1) * LANES]
        left = _left_half(pair.shape)
        zero = jnp.zeros_like(pair)
        pieces += [jnp.where(left, pair, zero), jnp.where(left, zero, pair)]
    return jnp.concatenate(pieces, axis=0)


def _attn_probs(kk, q_st, bias_p, sink_row, own, first):
    s = _pack_keys(_dot_nt(kk, q_st), own) * ATTN_SCALE + bias_p
    if first is not None:
        s = jnp.where(jnp.logical_and(first, jnp.logical_not(own)), NEG_INF, s)
    m = jnp.maximum(jnp.max(s, axis=0, keepdims=True), sink_row)
    p = jnp.exp(s - m)
    es = jnp.exp(sink_row - m)
    inv = 1.0 / (jnp.sum(p, axis=0, keepdims=True) + es)
    return p * inv, es * inv


def _pool_d(ext, cur, g, row0):
    w = POOL_WINDOWS[g]
    acc = ext
    k = 1
    while k < w:
        acc = acc + pltpu.roll(acc, k, 0)
        k *= 2
    return _window_mean(acc[POOL_HALO:, :], w, row0) - cur


def _window_mean(total, w, row0):
    t = row0 + lax.broadcasted_iota(jnp.int32, (POOL_HALO, total.shape[1]), 0)
    head = total[:POOL_HALO] / jnp.minimum(t + 1, w).astype(F32)
    return jnp.concatenate([head, total[POOL_HALO:] * (1.0 / w)], axis=0)


def _mixers_fwd(zp, q, kv, pool_w, pool_b, pool_scale, bias_t, sink_row):
    s_len = zp.shape[0]
    tq = min(512, s_len)
    nb = tq // BLOCK

    def body(zp_ref, zph_ref, q_ref, kv_ref, kvh_ref, pw_ref, pb_ref, ps_ref, bias_ref, sink_ref,
             pm_ref, o_ref, ka_ref, vd_ref):
        i = pl.program_id(0)
        cur = zp_ref[...].astype(F32)
        halo = zph_ref[...].astype(F32) * (i > 0).astype(F32)
        ext = jnp.concatenate([halo, cur], axis=0)
        for g in range(4):
            sl = slice(g * POOL_GROUP_DIM, (g + 1) * POOL_GROUP_DIM)
            d = _pool_d(ext[:, sl], cur[:, sl], g, i * tq)
            y = _dot(d.astype(BF16), pw_ref[g]) + pb_ref[:, sl]
            pm_ref[:, sl] = (y * ps_ref[:, sl]).astype(BF16)
        _fill_kv_slabs(kvh_ref, kv_ref, ka_ref, vd_ref)
        own = _own_block_mask()
        for b in range(nb):
            rq = slice(b * BLOCK, (b + 1) * BLOCK)
            rk = slice(b * BLOCK, (b + 2) * BLOCK)
            qb = q_ref[rq, :]
            for h in range(N_KV_HEADS):
                pn, _ = _attn_probs(ka_ref[h, rk, :], _stack_pairs(qb, h), bias_ref[h], sink_ref[h], own,
                                    (i == 0) if b == 0 else None)
                pn = _unpack_keys(pn, own).astype(BF16)
                vd = vd_ref[h, rk, :]
                left = _left_half(vd.shape)
                zero = jnp.zeros_like(vd)
                va, vb = jnp.where(left, vd, zero), jnp.where(left, zero, vd)
                for j in range(2):
                    o_pair = (_dot_tn(pn[:, (2 * j) * BLOCK:(2 * j + 1) * BLOCK], va)
                              + _dot_tn(pn[:, (2 * j + 1) * BLOCK:(2 * j + 2) * BLOCK], vb))
                    o_ref[rq, h * 256 + j * LANES:h * 256 + (j + 1) * LANES] = o_pair.astype(BF16)

    halo_pool = pl.BlockSpec((POOL_HALO, 512), lambda i: (jnp.maximum(i * (tq // POOL_HALO) - 1, 0), 0))
    halo_kv = pl.BlockSpec((BLOCK, 256), lambda i: (jnp.maximum(i * nb - 1, 0), 0))
    return pl.pallas_call(
        body, name="mixers_fwd", grid=(s_len // tq,),
        in_specs=[_rows(tq, 512), halo_pool, _rows(tq, 512), _rows(tq, 256), halo_kv,
                  _whole((4, 128, 128)), _whole((1, 512)), _whole((1, 512)),
                  _whole((N_KV_HEADS, BLOCK, GROUP * BLOCK)), _whole((N_KV_HEADS, 1, GROUP * BLOCK))],
        out_specs=[_rows(tq, 512), _rows(tq, 512)],
        out_shape=[jax.ShapeDtypeStruct((s_len, 512), BF16)] * 2,
        scratch_shapes=[pltpu.VMEM((N_KV_HEADS, tq + BLOCK, LANES), BF16)] * 2,
        compiler_params=_params(),
    )(zp, zp, q, kv, kv, pool_w, pool_b, pool_scale, bias_t, sink_row)


def _gated_mix(pm, o, zg, pp_ref, pa_ref):
    yp = _dot(pm, pp_ref[...])
    ya = _dot(o, pa_ref[...])
    gp = jax.nn.sigmoid(zg[:, :D_MODEL].astype(F32))
    ga = jax.nn.sigmoid(zg[:, D_MODEL:].astype(F32))
    return yp, ya, gp, ga


def _core(x, pm, o, zg, tgt, g_mlp, g_fin, p_pool, p_attn, w_out, w_up_blocks, w_down):
    s_len = x.shape[0]
    tm = min(256, s_len)
    n_chunks = D_FF // FF_CHUNK
    up_block = D_FF // N_DEV
    per_chunk = FF_CHUNK // up_block

    def body(x_ref, pm_ref, o_ref, zg_ref, tgt_ref, gm_ref, gf_ref, pp_ref, pa_ref, wo_ref, wu_ref, wd_ref,
             mixed_ref, dh1_ref, a_ref, dap_ref, u2_ref, dh2_ref, small_ref,
             dyp_ref, dya_ref, dzg_ref, dpm_ref, do_ref):
        i = pl.program_id(0)

        @pl.when(i == 0)
        def _():
            small_ref[...] = jnp.zeros_like(small_ref)

        yp, ya, gp, ga = _gated_mix(pm_ref[...], o_ref[...], zg_ref[...], pp_ref, pa_ref)
        mixed = (gp * yp + ga * ya).astype(BF16)
        mixed_ref[...] = mixed
        h1 = x_ref[...] + _dot(mixed, wo_ref[...])
        r2, xh2, u2 = _rms_fwd(h1, gm_ref[...])
        u2 = u2.astype(BF16)
        u2_ref[...] = u2
        acc = jnp.zeros((tm, D_MODEL), F32)
        for c in range(n_chunks):
            cs = slice(c * FF_CHUNK, (c + 1) * FF_CHUNK)
            a = jnp.concatenate([_dot(u2, wu_ref[per_chunk * c + j]) for j in range(per_chunk)], axis=1)
            a = jnp.maximum(a, 0.0)
            a_ref[:, cs] = a.astype(BF16)
            acc = acc + _dot((a * a).astype(BF16), wd_ref[cs, :])
        h2 = h1 + acc
        r3, xh3, y = _rms_fwd(h2, gf_ref[...])
        diff = y - tgt_ref[...]
        small_ref[2:3, :] += 0.5 * jnp.sum(jnp.mean(diff * diff, axis=-1, keepdims=True))
        dy = diff * (1.0 / D_MODEL)
        dh2, dgf = _rms_bwd(dy, xh3, r3, gf_ref[...])
        small_ref[1:2, :] += dgf
        dh2_bf = dh2.astype(BF16)
        dh2_ref[...] = dh2_bf
        du2 = jnp.zeros((tm, D_MODEL), F32)
        for c in range(n_chunks):
            cs = slice(c * FF_CHUNK, (c + 1) * FF_CHUNK)
            ds = _dot_nt(dh2_bf, wd_ref[cs, :])
            dap = (ds * (2.0 * a_ref[:, cs].astype(F32))).astype(BF16)
            dap_ref[:, cs] = dap
            for j in range(per_chunk):
                du2 = du2 + _dot_nt(dap[:, j * up_block:(j + 1) * up_block], wu_ref[per_chunk * c + j])
        dh1n, dgm = _rms_bwd(du2, xh2, r2, gm_ref[...])
        small_ref[0:1, :] += dgm
        dh1 = dh2 + dh1n
        dh1_ref[...] = dh1
        dm = _dot_nt(dh1.astype(BF16), wo_ref[...])
        dyp = (dm * gp).astype(BF16)
        dya = (dm * ga).astype(BF16)
        dyp_ref[...] = dyp
        dya_ref[...] = dya
        dzg_ref[:, :D_MODEL] = (dm * yp * (gp * (1.0 - gp))).astype(BF16)
        dzg_ref[:, D_MODEL:] = (dm * ya * (ga * (1.0 - ga))).astype(BF16)
        dpm_ref[...] = _dot_nt(dyp, pp_ref[...]).astype(BF16)
        do_ref[...] = _dot_nt(dya, pa_ref[...]).astype(BF16)

    def fixed(shape):
        return pl.BlockSpec(shape, lambda i: (0,) * len(shape), pipeline_mode=pl.Buffered(1))

    widths_dtypes = ((D_MODEL, BF16), (D_MODEL, F32), (D_FF, BF16), (D_FF, BF16), (D_MODEL, BF16), (D_MODEL, BF16))
    back = ((D_MODEL, BF16), (D_MODEL, BF16), (2048, BF16), (512, BF16), (512, BF16))
    return pl.pallas_call(
        body, name="core", grid=(s_len // tm,),
        in_specs=[_rows(tm, D_MODEL), _rows(tm, 512), _rows(tm, 512), _rows(tm, 2048), _rows(tm, D_MODEL),
                  _whole((1, D_MODEL)), _whole((1, D_MODEL)),
                  fixed((512, D_MODEL)), fixed((512, D_MODEL)), fixed((D_MODEL, D_MODEL)),
                  fixed((N_DEV, D_MODEL, up_block)), fixed((D_FF, D_MODEL))],
        out_specs=[_rows(tm, n) for n, _ in widths_dtypes] + [_whole((8, D_MODEL))] + [_rows(tm, n) for n, _ in back],
        out_shape=[jax.ShapeDtypeStruct((s_len, n), d) for n, d in widths_dtypes]
        + [jax.ShapeDtypeStruct((8, D_MODEL), F32)] + [jax.ShapeDtypeStruct((s_len, n), d) for n, d in back],
        compiler_params=_params(),
    )(x, pm, o, zg, tgt, g_mlp, g_fin, p_pool, p_attn, w_out, w_up_blocks, w_down)


def _tn_matmul(a, b, square_a=False, col_blocks=None, after=None):
    s_len, ka = a.shape
    nb = b.shape[1]
    tt = min(2048, s_len)
    tk = min(1024, ka)
    tn = min(1024, nb)
    n_t = s_len // tt
    if col_blocks is None:
        out_spec = pl.BlockSpec((tk, tn), lambda k, j, t: (k, j))
        out_shape = jax.ShapeDtypeStruct((ka, nb), BF16)
    else:
        width = nb // col_blocks
        per_tile = tn // width
        out_spec = pl.BlockSpec((per_tile, tk, width), lambda k, j, t: (j, k, 0))
        out_shape = jax.ShapeDtypeStruct((col_blocks, ka, width), BF16)

    extra = [] if after is None else [after]

    def body(a_ref, b_ref, *rest):
        o_ref, acc_ref = rest[len(extra):]
        t = pl.program_id(2)

        @pl.when(t == 0)
        def _():
            acc_ref[...] = jnp.zeros_like(acc_ref)

        av = a_ref[...]
        if square_a:
            av = av * av
        acc_ref[...] += _dot_tn(av.astype(BF16), b_ref[...].astype(BF16))

        @pl.when(t == n_t - 1)
        def _():
            if col_blocks is None:
                o_ref[...] = acc_ref[...].astype(o_ref.dtype)
            else:
                for blk in range(per_tile):
                    o_ref[blk] = acc_ref[:, blk * width:(blk + 1) * width].astype(o_ref.dtype)

    return pl.pallas_call(
        body, name="tn_matmul", grid=(ka // tk, nb // tn, n_t),
        in_specs=[pl.BlockSpec((tt, tk), lambda k, j, t: (t, k)), pl.BlockSpec((tt, tn), lambda k, j, t: (t, j))]
        + [ANY] * len(extra),
        out_specs=out_spec, out_shape=out_shape,
        scratch_shapes=[pltpu.VMEM((tk, tn), F32)],
        compiler_params=_params(3),
    )(a, b, *extra)


def _tn_w_in(u, dzp, dq, dkv, dzg):
    s_len = u.shape[0]
    tt = min(1024, s_len)
    n_t = s_len // tt
    width = IN_WIDTH // N_DEV
    pieces = ((0, 512), (512, 1024), (1024, 1280), (1280, IN_WIDTH))

    def body(u_ref, dzp_ref, dq_ref, dkv_ref, dzg_ref, o_ref, acc_ref):
        t = pl.program_id(0)

        @pl.when(t == 0)
        def _():
            acc_ref[...] = jnp.zeros_like(acc_ref)

        uv = u_ref[...]
        for (c0, c1), ref in zip(pieces, (dzp_ref, dq_ref, dkv_ref, dzg_ref)):
            acc_ref[c0:c1, :] += _dot_tn(ref[...], uv)

        @pl.when(t == n_t - 1)
        def _():
            for j in range(N_DEV):
                o_ref[j] = acc_ref[j * width:(j + 1) * width, :].astype(BF16)

    return pl.pallas_call(
        body, name="tn_w_in", grid=(n_t,),
        in_specs=[_rows(tt, D_MODEL)] + [_rows(tt, c1 - c0) for c0, c1 in pieces],
        out_specs=_whole((N_DEV, width, D_MODEL)),
        out_shape=jax.ShapeDtypeStruct((N_DEV, width, D_MODEL), BF16),
        scratch_shapes=[pltpu.VMEM((IN_WIDTH, D_MODEL), F32)],
        compiler_params=_params(),
    )(u, dzp, dq, dkv, dzg)


MIX_POOL_B = 4 * POOL_GROUP_DIM
MIX_SINKS = MIX_POOL_B + 8
MIX_ROWS = MIX_SINKS + 8


def _mixers_bwd(after, zp, q, kv, dpm, do, pool_w, pool_b, pool_scale, bias_t, sink_row):
    s_len = zp.shape[0]
    tq = min(512, s_len)
    nb = tq // BLOCK
    n_steps = s_len // tq

    def body(after_ref, zp_ref, zph_ref, q_ref, kv_ref, kvh_ref, dpm_ref, dpmh_ref, do_ref, pw_ref, pb_ref, ps_ref,
             bias_ref, sink_ref, dzp_ref, dq_ref, dkv_ref, small_ref, dps_ref,
             ka_ref, vd_ref, dsk_acc, dkv_acc):
        i = pl.program_id(0)

        @pl.when(i == 0)
        def _():
            dkv_acc[...] = jnp.zeros_like(dkv_acc)
            small_ref[...] = jnp.zeros_like(small_ref)
            dps_ref[...] = jnp.zeros_like(dps_ref)
            dsk_acc[...] = jnp.zeros_like(dsk_acc)

        cur = zp_ref[...].astype(F32)
        halo = zph_ref[...].astype(F32) * (i > 0).astype(F32)
        ext = jnp.concatenate([halo, cur], axis=0)
        dpm_next = dpmh_ref[...].astype(F32) * (i < n_steps - 1).astype(F32)
        dpm_ext = jnp.concatenate([dpm_ref[...].astype(F32), dpm_next], axis=0)
        n_ext = tq + POOL_HALO
        for g in range(4):
            sl = slice(g * POOL_GROUP_DIM, (g + 1) * POOL_GROUP_DIM)
            w = POOL_WINDOWS[g]
            d = _pool_d(ext[:, sl], cur[:, sl], g, i * tq).astype(BF16)
            y_lin = _dot(d, pw_ref[g]) + pb_ref[:, sl]
            dps_ref[:, sl] += jnp.sum(dpm_ext[:tq, sl] * y_lin, axis=0, keepdims=True)
            dyl_ext = dpm_ext[:, sl] * ps_ref[:, sl]
            small_ref[MIX_POOL_B + g:MIX_POOL_B + g + 1, :] += jnp.sum(dyl_ext[:tq], axis=0, keepdims=True)
            dyl_bf = dyl_ext.astype(BF16)
            small_ref[g * POOL_GROUP_DIM:(g + 1) * POOL_GROUP_DIM, :] += _dot_tn(d, dyl_bf[:tq])
            dd = _dot_nt(dyl_bf, pw_ref[g])
            e = _window_mean(dd, w, i * tq)
            acc = e
            k = 1
            while k < w:
                acc = acc + pltpu.roll(acc, n_ext - k, 0)
                k *= 2
            dzp_ref[:, sl] = (acc[:tq] - dd[:tq]).astype(BF16)

        _fill_kv_slabs(kvh_ref, kv_ref, ka_ref, vd_ref)

        def fold(dup):
            return dup + pltpu.roll(dup, HEAD_DIM, 1)

        own = _own_block_mask()
        for b in range(nb):
            rq = slice(b * BLOCK, (b + 1) * BLOCK)
            rk = slice(b * BLOCK, (b + 2) * BLOCK)
            qb = q_ref[rq, :]
            dob = do_ref[rq, :]
            dk_dup, dv_dup = [], []
            for h in range(N_KV_HEADS):
                kk = ka_ref[h, rk, :]
                q_st = _stack_pairs(qb, h)
                do_st = _stack_pairs(dob, h)
                pn, psink = _attn_probs(kk, q_st, bias_ref[h], sink_ref[h], own, (i == 0) if b == 0 else None)
                dp = _pack_keys(_dot_nt(vd_ref[h, rk, :], do_st), own)
                delta = jnp.sum(pn * dp, axis=0, keepdims=True)
                dsk_acc[h] += -psink * delta
                ds = _unpack_keys((pn * (dp - delta)) * ATTN_SCALE, own).astype(BF16)
                pn = _unpack_keys(pn, own)
                dq_st = _dot_tn(ds, kk)
                for j in range(2):
                    left = _left_half((BLOCK, LANES))
                    dq_pair = jnp.where(left, dq_st[(2 * j) * BLOCK:(2 * j + 1) * BLOCK],
                                        dq_st[(2 * j + 1) * BLOCK:(2 * j + 2) * BLOCK])
                    dq_ref[rq, h * 256 + j * LANES:h * 256 + (j + 1) * LANES] = dq_pair.astype(BF16)
                dk_dup.append(fold(_dot(ds, q_st)))
                dv_dup.append(fold(_dot(pn.astype(BF16), do_st)))
            left = _left_half((2 * BLOCK, LANES))
            dkv_blk = jnp.concatenate([jnp.where(left, dk_dup[0], dk_dup[1]),
                                       jnp.where(left, dv_dup[0], dv_dup[1])], axis=1)
            g0 = pl.multiple_of(i * tq + b * BLOCK, BLOCK)
            dkv_acc[pl.ds(g0, 2 * BLOCK), :] += dkv_blk

        @pl.when(i == n_steps - 1)
        def _():
            dkv_ref[...] = dkv_acc[BLOCK:, :].astype(BF16)
            lane = lax.broadcasted_iota(jnp.int32, (1, LANES), 1)
            row = jnp.zeros((1, LANES), F32)
            for h in range(N_KV_HEADS):
                for g in range(GROUP):
                    tot = jnp.sum(dsk_acc[h, :, g * BLOCK:(g + 1) * BLOCK], axis=1, keepdims=True)
                    row = jnp.where(lane == GROUP * h + g, tot, row)
            small_ref[MIX_SINKS:MIX_SINKS + 1, :] = row

    blocks_per_tile = tq // POOL_HALO
    last_halo = s_len // POOL_HALO - 1
    halo_prev = pl.BlockSpec((POOL_HALO, 512), lambda i: (jnp.maximum(i * blocks_per_tile - 1, 0), 0))
    halo_next = pl.BlockSpec((POOL_HALO, 512), lambda i: (jnp.minimum((i + 1) * blocks_per_tile, last_halo), 0))
    halo_kv = pl.BlockSpec((BLOCK, 256), lambda i: (jnp.maximum(i * nb - 1, 0), 0))
    return pl.pallas_call(
        body, name="mixers_bwd", grid=(n_steps,),
        in_specs=[ANY, _rows(tq, 512), halo_prev, _rows(tq, 512), _rows(tq, 256), halo_kv,
                  _rows(tq, 512), halo_next, _rows(tq, 512),
                  _whole((4, 128, 128)), _whole((1, 512)), _whole((1, 512)),
                  _whole((N_KV_HEADS, BLOCK, GROUP * BLOCK)), _whole((N_KV_HEADS, 1, GROUP * BLOCK))],
        out_specs=[_rows(tq, 512), _rows(tq, 512), _whole((s_len, 256)),
                   _whole((MIX_ROWS, LANES)), _whole((1, 512))],
        out_shape=[jax.ShapeDtypeStruct((s_len, 512), BF16), jax.ShapeDtypeStruct((s_len, 512), BF16),
                   jax.ShapeDtypeStruct((s_len, 256), BF16), jax.ShapeDtypeStruct((MIX_ROWS, LANES), F32),
                   jax.ShapeDtypeStruct((1, 512), F32)],
        scratch_shapes=[pltpu.VMEM((N_KV_HEADS, tq + BLOCK, LANES), BF16)] * 2
        + [pltpu.VMEM((N_KV_HEADS, 1, GROUP * BLOCK), F32), pltpu.VMEM((s_len + BLOCK, 256), F32)],
        compiler_params=_params(),
    )(after, zp, zp, q, kv, kv, dpm, dpm, do, pool_w, pool_b, pool_scale, bias_t, sink_row)


def _in_bwd(after, dzp, dq, dkv, dzg, w_in_t, x, dh1, g_mix):
    s_len = x.shape[0]
    tm = min(512, s_len)

    def body(after_ref, dzp_ref, dq_ref, dkv_ref, dzg_ref, w_ref, x_ref, dh1_ref, g_ref, dx_ref, dg_ref):
        i = pl.program_id(0)

        @pl.when(i == 0)
        def _():
            dg_ref[...] = jnp.zeros_like(dg_ref)

        du = _dot(dzp_ref[...], w_ref[0:512, :])
        du = du + _dot(dq_ref[...], w_ref[512:1024, :])
        du = du + _dot(dkv_ref[...], w_ref[1024:1280, :])
        du = du + _dot(dzg_ref[...], w_ref[1280:3328, :])
        r, xh, _ = _rms_fwd(x_ref[...], g_ref[...])
        dxn, dg = _rms_bwd(du, xh, r, g_ref[...])
        dg_ref[...] += dg
        dx_ref[...] = dh1_ref[...] + dxn

    return pl.pallas_call(
        body, name="in_bwd", grid=(s_len // tm,),
        in_specs=[ANY, _rows(tm, 512), _rows(tm, 512), _rows(tm, 256), _rows(tm, 2048), _whole((IN_WIDTH, D_MODEL)),
                  _rows(tm, D_MODEL), _rows(tm, D_MODEL), _whole((1, D_MODEL))],
        out_specs=[_rows(tm, D_MODEL), _whole((1, D_MODEL))],
        out_shape=[jax.ShapeDtypeStruct((s_len, D_MODEL), F32), jax.ShapeDtypeStruct((1, D_MODEL), F32)],
        compiler_params=_params(),
    )(after, dzp, dq, dkv, dzg, w_in_t, x, dh1, g_mix)


def _all_gather_weights(name, shards, after=None):
    n = len(shards)
    extra = [] if after is None else [after]
    n_extra = len(extra)

    def body(*refs):
        ins, outs = refs[:n], refs[n + n_extra:2 * n + n_extra]
        send_sems, recv_sems, local_sems = refs[2 * n + n_extra:]
        x, y, c = lax.axis_index("x"), lax.axis_index("y"), lax.axis_index("c")
        me, sibling = (x, y, c), (x, y, 1 - c)
        chips = [(1 - x, y), (x, 1 - y), (1 - x, 1 - y)]

        def slot(a, px, py, pc):
            return outs[a].at[4 * px + 2 * py + pc]

        def copy(a, k, block, to, src=None):
            return pltpu.make_async_remote_copy(
                src_ref=slot(a, *block) if src is None else src, dst_ref=slot(a, *block),
                send_sem=send_sems.at[a, k], recv_sem=recv_sems.at[a, k], device_id=to, device_id_type=MESH)

        mine = [pltpu.make_async_copy(ins[a], slot(a, *me), local_sems.at[a]) for a in range(n)]
        for cp in mine:
            cp.start()
        first = []
        for a in range(n):
            first.append(copy(a, 0, me, sibling, src=ins[a]))
            first += [copy(a, 1 + j, me, (*chip, c), src=ins[a]) for j, chip in enumerate(chips)]
        for cp in first:
            cp.start()
        passed = []
        for a in range(n):
            for j, chip in enumerate(chips):
                copy(a, 1 + j, (*chip, c), me).wait_recv()
                cp = copy(a, 4 + j, (*chip, c), sibling)
                cp.start()
                passed.append(cp)
        for a in range(n):
            copy(a, 0, sibling, me).wait_recv()
            for j, chip in enumerate(chips):
                copy(a, 4 + j, (*chip, 1 - c), me).wait_recv()
        for cp in first + passed:
            cp.wait_send()
        for cp in mine:
            cp.wait()

    return pl.pallas_call(
        body, name=name,
        in_specs=[ANY] * (n + n_extra), out_specs=[ANY] * n,
        out_shape=[jax.ShapeDtypeStruct((N_DEV,) + s.shape, s.dtype) for s in shards],
        scratch_shapes=[pltpu.SemaphoreType.DMA((n, 7)), pltpu.SemaphoreType.DMA((n, 7)), pltpu.SemaphoreType.DMA((n,))],
    )(*shards, *extra)


HBM_SPEC = pl.BlockSpec(memory_space=pltpu.HBM)
SEM_SPEC = pl.BlockSpec(memory_space=pltpu.SEMAPHORE)
DATAFLOW = pltpu.SideEffectType.DATAFLOW_SIDE_EFFECTING
N_PEERS = N_DEV - 1


CHIP_PEERS = (1, 2, 4, 6)
RELAYED = (2, 4, 6)


def _peer_copies(srcs, lands, scatter, send_sems, recv_sems):
    x, y, c = lax.axis_index("x"), lax.axis_index("y"), lax.axis_index("c")
    me_idx = 4 * x + 2 * y + c
    copies = []
    for k in range(1, N_DEV):
        px = 1 - x if (k >> 2) & 1 else x
        py = 1 - y if (k >> 1) & 1 else y
        pc = 1 - c if k & 1 else c
        p_idx = 4 * px + 2 * py + pc
        for a in range(len(srcs)):
            if scatter[a] == "chip" and k not in CHIP_PEERS:
                continue
            src = srcs[a].at[p_idx] if scatter[a] is True else srcs[a]
            dst = lands[a].at[k] if scatter[a] is True else lands[a].at[me_idx]
            copies.append(pltpu.make_async_remote_copy(
                src_ref=src, dst_ref=dst, send_sem=send_sems.at[a * N_PEERS + k - 1],
                recv_sem=recv_sems.at[a * N_PEERS + k - 1],
                device_id=(px, py, pc), device_id_type=MESH))
    return copies


def _exchange_start(name, srcs, scatter, after):
    n = len(srcs)
    lands = [lax.empty(s.shape if sc is True else (N_DEV,) + s.shape, s.dtype) for s, sc in zip(srcs, scatter)]

    def body(*refs):
        src_refs, land_refs = refs[:n], refs[n:2 * n]
        send_sems, recv_sems = refs[2 * n + 1], refs[2 * n + 2]
        token = refs[4 * n + 3]
        for cp in _peer_copies(src_refs, land_refs, scatter, send_sems, recv_sems):
            cp.start()
        token[...] = jnp.zeros_like(token)

    hbm = lambda t: pltpu.HBM(t.shape, t.dtype)
    outs = pl.pallas_call(
        body, name=name,
        out_shape=[pltpu.SemaphoreType.DMA((n * N_PEERS,)), pltpu.SemaphoreType.DMA((n * N_PEERS,))]
        + [hbm(t) for t in srcs] + [hbm(t) for t in lands] + [jax.ShapeDtypeStruct((8, LANES), F32)],
        in_specs=[HBM_SPEC] * (2 * n) + [ANY],
        out_specs=[SEM_SPEC, SEM_SPEC] + [HBM_SPEC] * (2 * n) + [pl.BlockSpec(memory_space=pltpu.VMEM)],
        input_output_aliases={i: 2 + i for i in range(2 * n)},
        compiler_params=pltpu.CompilerParams(has_side_effects=DATAFLOW),
    )(*[pltpu.with_memory_space_constraint(t, pltpu.HBM) for t in list(srcs) + lands], after)
    return dict(n=n, scatter=scatter, send_sems=outs[0], recv_sems=outs[1], srcs=outs[2:2 + n],
                lands=outs[2 + n:2 + 2 * n], token=outs[2 + 2 * n])


def _exchange_wait(name, handle, after):
    n, scatter = handle["n"], handle["scatter"]

    def body(*refs):
        src_refs, land_refs = refs[:n], refs[n:2 * n]
        send_sems, recv_sems = refs[2 * n], refs[2 * n + 1]
        for cp in _peer_copies(src_refs, land_refs, scatter, send_sems, recv_sems):
            cp.wait_send()
            cp.wait_recv()

    both = list(handle["srcs"]) + list(handle["lands"])
    outs = pl.pallas_call(
        body, name=name,
        out_shape=[pltpu.HBM(t.shape, t.dtype) for t in both],
        in_specs=[HBM_SPEC] * (2 * n) + [SEM_SPEC, SEM_SPEC, ANY],
        out_specs=[HBM_SPEC] * (2 * n),
        input_output_aliases={i: i for i in range(2 * n)},
        compiler_params=pltpu.CompilerParams(has_side_effects=DATAFLOW),
    )(*both, handle["send_sems"], handle["recv_sems"], after)
    me_idx = _my_index()
    lands = [land if sc is True else lax.dynamic_update_index_in_dim(land, src, me_idx, 0)
             for land, src, sc in zip(outs[n:], outs[:n], scatter)]
    return lands, outs[:n]


def _my_index():
    return 4 * lax.axis_index("x") + 2 * lax.axis_index("y") + lax.axis_index("c")


def _relay_copies(bufs, send_sems, recv_sems):
    x, y, c = lax.axis_index("x"), lax.axis_index("y"), lax.axis_index("c")
    copies = []
    for j, k in enumerate(RELAYED):
        px = 1 - x if (k >> 2) & 1 else x
        py = 1 - y if (k >> 1) & 1 else y
        slot = 4 * px + 2 * py + c
        for a, buf in enumerate(bufs):
            copies.append(pltpu.make_async_remote_copy(
                src_ref=buf.at[slot], dst_ref=buf.at[slot], send_sem=send_sems.at[a * len(RELAYED) + j],
                recv_sem=recv_sems.at[a * len(RELAYED) + j], device_id=(x, y, 1 - c), device_id_type=MESH))
    return copies


def _relay_start(name, bufs, after):
    n = len(bufs)

    def body(*refs):
        send_sems, recv_sems = refs[n + 1], refs[n + 2]
        for cp in _relay_copies(refs[:n], send_sems, recv_sems):
            cp.start()
        token = refs[2 * n + 3]
        token[...] = jnp.zeros_like(token)

    n_sems = n * len(RELAYED)
    outs = pl.pallas_call(
        body, name=name,
        out_shape=[pltpu.SemaphoreType.DMA((n_sems,)), pltpu.SemaphoreType.DMA((n_sems,))]
        + [pltpu.HBM(t.shape, t.dtype) for t in bufs] + [jax.ShapeDtypeStruct((8, LANES), F32)],
        in_specs=[HBM_SPEC] * n + [ANY],
        out_specs=[SEM_SPEC, SEM_SPEC] + [HBM_SPEC] * n + [pl.BlockSpec(memory_space=pltpu.VMEM)],
        input_output_aliases={i: 2 + i for i in range(n)},
        compiler_params=pltpu.CompilerParams(has_side_effects=DATAFLOW),
    )(*[pltpu.with_memory_space_constraint(t, pltpu.HBM) for t in bufs], after)
    return dict(n=n, send_sems=outs[0], recv_sems=outs[1], bufs=outs[2:2 + n], token=outs[2 + n])


def _relay_wait(name, handle, after):
    n = handle["n"]

    def body(*refs):
        for cp in _relay_copies(refs[:n], refs[n], refs[n + 1]):
            cp.wait_send()
            cp.wait_recv()

    return pl.pallas_call(
        body, name=name,
        out_shape=[pltpu.HBM(t.shape, t.dtype) for t in handle["bufs"]],
        in_specs=[HBM_SPEC] * n + [SEM_SPEC, SEM_SPEC, ANY],
        out_specs=[HBM_SPEC] * n,
        input_output_aliases={i: i for i in range(n)},
        compiler_params=pltpu.CompilerParams(has_side_effects=DATAFLOW),
    )(*handle["bufs"], handle["send_sems"], handle["recv_sems"], after)


def _adamw(parts, w, m, v, sent=None):
    r, c = w.shape
    tr = 256 if r % 256 == 0 else r
    own = sent is not None

    def body(*refs):
        if own:
            _, p_ref, own_ref, w_ref, m_ref, v_ref, g_ref, d_ref, nm_ref, nv_ref = refs
            g = own_ref[...].astype(F32)
        else:
            p_ref, w_ref, m_ref, v_ref, g_ref, d_ref, nm_ref, nv_ref = refs
            g = p_ref[0].astype(F32)
        for k in range(1, N_DEV):
            g = g + p_ref[k].astype(F32)
        m_new = ADAM_B1 * m_ref[...] + (1.0 - ADAM_B1) * g
        v_new = ADAM_B2 * v_ref[...] + (1.0 - ADAM_B2) * (g * g)
        m_hat = m_new / (1.0 - ADAM_B1 ** ADAM_STEP)
        v_hat = v_new / (1.0 - ADAM_B2 ** ADAM_STEP)
        g_ref[...] = g
        d_ref[...] = -ADAM_LR * (m_hat / (jnp.sqrt(v_hat) + ADAM_EPS) + ADAM_WD * w_ref[...])
        nm_ref[...] = m_new
        nv_ref[...] = v_new

    out_shape = [jax.ShapeDtypeStruct((r, c), F32)] * 4
    if not own:
        return pl.pallas_call(
            body, name="adamw", grid=(r // tr,),
            in_specs=[pl.BlockSpec((N_DEV, tr, c), lambda i: (0, i, 0))] + [_rows(tr, c)] * 3,
            out_specs=[_rows(tr, c)] * 4, out_shape=out_shape, compiler_params=_params(),
        )(parts, w, m, v)
    rows = pl.BlockSpec((tr, c), lambda i, me: (i, 0))
    return pl.pallas_call(
        body, name="adamw_own", out_shape=out_shape, compiler_params=_params(),
        grid_spec=pltpu.PrefetchScalarGridSpec(
            num_scalar_prefetch=1, grid=(r // tr,),
            in_specs=[pl.BlockSpec((N_DEV, tr, c), lambda i, me: (0, i, 0)),
                      pl.BlockSpec((None, tr, c), lambda i, me: (me[0], i, 0))] + [rows] * 3,
            out_specs=[rows] * 4),
    )(_my_index().reshape(1).astype(jnp.int32), parts, sent, w, m, v)


def _adam_step(g, w, m, v):
    m_new = ADAM_B1 * m + (1.0 - ADAM_B1) * g
    v_new = ADAM_B2 * v + (1.0 - ADAM_B2) * (g * g)
    m_hat = m_new / (1.0 - ADAM_B1 ** ADAM_STEP)
    v_hat = v_new / (1.0 - ADAM_B2 ** ADAM_STEP)
    return -ADAM_LR * (m_hat / (jnp.sqrt(v_hat) + ADAM_EPS) + ADAM_WD * w), m_new, v_new


SMALL_NAMES = ("norm_mix", "pool_w", "pool_b", "pool_scale", "attn_sinks", "norm_mlp", "norm_final")


def _adamw_small(mlp_all, mix_all, scale_all, nmix_all, w, m, v):
    def body(mlp_ref, mix_ref, scale_ref, nmix_ref, *refs):
        ins, outs = refs[:21], refs[21:]

        def total(ref, rows, lanes=slice(None)):
            g = ref[0, rows, lanes]
            for k in range(1, N_DEV):
                g = g + ref[k, rows, lanes]
            return g

        grads = dict(
            norm_mix=total(nmix_ref, slice(0, 1)), pool_w=total(mix_ref, slice(0, MIX_POOL_B)),
            pool_b=total(mix_ref, slice(MIX_POOL_B, MIX_POOL_B + 4)), pool_scale=total(scale_ref, slice(0, 1)),
            attn_sinks=total(mix_ref, slice(MIX_SINKS, MIX_SINKS + 1)),
            norm_mlp=total(mlp_ref, slice(0, 1)), norm_final=total(mlp_ref, slice(1, 2)))
        for i, name in enumerate(SMALL_NAMES):
            g = grads[name]
            d, m_new, v_new = _adam_step(g, ins[3 * i][...], ins[3 * i + 1][...], ins[3 * i + 2][...])
            for ref, val in zip(outs[4 * i:4 * i + 4], (g, d, m_new, v_new)):
                ref[...] = val
        outs[28][...] = jnp.broadcast_to(total(mlp_ref, slice(2, 3), slice(0, LANES)), (8, LANES))

    operands, out_shape = [], []
    for name in SMALL_NAMES:
        operands += [w[name], m[name], v[name]]
        out_shape += [jax.ShapeDtypeStruct(w[name].shape, F32)] * 4
    out_shape.append(jax.ShapeDtypeStruct((8, LANES), F32))
    outs = pl.pallas_call(body, name="adamw_small", out_shape=out_shape)(
        mlp_all, mix_all, scale_all, nmix_all, *operands)
    return {name: outs[4 * i:4 * i + 4] for i, name in enumerate(SMALL_NAMES)}, outs[28]


def kernel(x, norm_mix, w_in, pool_w, pool_b, pool_scale, attn_sinks, p_pool, p_attn, w_out, norm_mlp, w_up, w_down, norm_final, loss_target, m_norm_mix, m_w_in, m_pool_w, m_pool_b, m_pool_scale, m_attn_sinks, m_p_pool, m_p_attn, m_w_out, m_norm_mlp, m_w_up, m_w_down, m_norm_final, v_norm_mix, v_w_in, v_pool_w, v_pool_b, v_pool_scale, v_attn_sinks, v_p_pool, v_p_attn, v_w_out, v_norm_mlp, v_w_up, v_w_down, v_norm_final):
    xs = x[0]
    tgt = loss_target[0]
    s_len = xs.shape[0]

    p_pool_bf, p_attn_bf, w_out_bf, w_up_bf, w_down_bf = [
        t[0].astype(BF16) for t in (p_pool, p_attn, w_out, w_up, w_down)]
    w_in_bf = w_in[0].T.astype(BF16)
    (w_in_g,) = _all_gather_weights("all_gather_w_in", [w_in_bf])
    ag_rest = _exchange_start(
        "ag_rest_start", [p_pool_bf, p_attn_bf, w_out_bf, w_up_bf, w_down_bf], ("chip",) * 5, w_in_g)

    pool_w_bf = pool_w[0].astype(BF16)
    pool_b_row = pool_b[0].reshape(1, POOL_WIDTH)
    bias_t, sink_row = _attn_constants(attn_sinks[0])

    w_in_t = w_in_g.reshape(IN_WIDTH, D_MODEL)
    u, zp, q, kv, zg = _fwd_in(ag_rest["token"], xs, norm_mix, w_in_t)
    pm, o = _mixers_fwd(zp, q, kv, pool_w_bf, pool_b_row, pool_scale, bias_t, sink_row)
    first_level, _ = _exchange_wait("ag_rest_wait", ag_rest, o)
    relay = _relay_start("ag_relay_start", first_level, pm)
    p_pool_g, p_attn_g, w_out_g, w_up_g, w_down_g = _relay_wait("ag_relay_wait", relay, relay["token"])
    p_pool_f = p_pool_g.transpose(1, 0, 2).reshape(POOL_WIDTH, D_MODEL)
    p_attn_f = p_attn_g.transpose(1, 0, 2).reshape(ATTN_WIDTH, D_MODEL)
    w_out_f = w_out_g.reshape(D_MODEL, D_MODEL)
    w_down_f = w_down_g.reshape(D_FF, D_MODEL)
    mixed, dh1, a, dapre, u2, dh2, small_mlp, dyp, dya, dzg, dpm, do = _core(
        xs, pm, o, zg, tgt, norm_mlp, norm_final.reshape(1, D_MODEL), p_pool_f, p_attn_f, w_out_f, w_up_g, w_down_f)
    gw_down = _tn_matmul(a, dh2, square_a=True)
    gw_up = _tn_matmul(u2, dapre, col_blocks=N_DEV)
    ex_mlp = _exchange_start(
        "ex_mlp_start", [gw_up, gw_down.reshape(N_DEV, D_FF // N_DEV, D_MODEL)], (True, True), small_mlp)
    dzp, dq, dkv, small_mix, g_pool_scale = _mixers_bwd(
        ex_mlp["token"], zp, q, kv, dpm, do, pool_w_bf, pool_b_row, pool_scale, bias_t, sink_row)
    gw_in = _tn_w_in(u, dzp, dq, dkv, dzg)
    ex_in = _exchange_start(
        "ex_in_start", [gw_in, small_mlp, small_mix, g_pool_scale], (True, False, False, False), dq)
    gw_out = _tn_matmul(mixed, dh1, after=ex_in["token"])
    gp_pool = _tn_matmul(pm, dyp, col_blocks=N_DEV, after=ex_in["token"])
    gp_attn = _tn_matmul(o, dya, col_blocks=N_DEV, after=ex_in["token"])
    ex_proj = _exchange_start(
        "ex_proj_start", [gp_pool, gp_attn, gw_out.reshape(N_DEV, D_MODEL // N_DEV, D_MODEL)], (True,) * 3,
        ex_in["token"])
    dx, g_norm_mix = _in_bwd(ex_proj["token"], dzp, dq, dkv, dzg, w_in_t, xs, dh1, norm_mix)

    big_w = dict(w_in=w_in, p_pool=p_pool, p_attn=p_attn, w_out=w_out, w_up=w_up, w_down=w_down)
    big_m = dict(w_in=m_w_in, p_pool=m_p_pool, p_attn=m_p_attn, w_out=m_w_out, w_up=m_w_up, w_down=m_w_down)
    big_v = dict(w_in=v_w_in, p_pool=v_p_pool, p_attn=v_p_attn, w_out=v_w_out, w_up=v_w_up, w_down=v_w_down)
    res = {}

    def update(names, recvs, sents):
        for name, parts, sent in zip(names, recvs, sents):
            flip = (lambda t: t.T) if name == "w_in" else (lambda t: t)
            outs = _adamw(parts, flip(big_w[name][0]), flip(big_m[name][0]), flip(big_v[name][0]), sent)
            res[name] = [flip(t)[None] for t in outs]

    update(["w_up", "w_down"], *_exchange_wait("ex_mlp_wait", ex_mlp, dx))
    (norm_mix_all,) = _all_gather_weights("all_gather_norm_mix", [g_norm_mix], res["w_down"][0])
    (r_in, mlp_all, mix_all, scale_all), (s_in, _, _, _) = _exchange_wait("ex_in_wait", ex_in, norm_mix_all)
    update(["w_in"], [r_in], [s_in])

    natural = dict(norm_mix=(1, D_MODEL), pool_w=(MIX_POOL_B, LANES), pool_b=(4, LANES), pool_scale=(1, POOL_WIDTH),
                   attn_sinks=(1, LANES), norm_mlp=(1, D_MODEL), norm_final=(1, D_MODEL))

    def as_2d(t, name):
        if name == "attn_sinks":
            return jnp.pad(t, ((0, 0), (0, LANES - N_HEADS)))
        return t.reshape(natural[name])

    small_w = dict(norm_mix=norm_mix, pool_w=pool_w, pool_b=pool_b, pool_scale=pool_scale, attn_sinks=attn_sinks,
                   norm_mlp=norm_mlp, norm_final=norm_final)
    small_m = dict(norm_mix=m_norm_mix, pool_w=m_pool_w, pool_b=m_pool_b, pool_scale=m_pool_scale,
                   attn_sinks=m_attn_sinks, norm_mlp=m_norm_mlp, norm_final=m_norm_final)
    small_v = dict(norm_mix=v_norm_mix, pool_w=v_pool_w, pool_b=v_pool_b, pool_scale=v_pool_scale,
                   attn_sinks=v_attn_sinks, norm_mlp=v_norm_mlp, norm_final=v_norm_final)
    small_res, loss_all = _adamw_small(
        mlp_all, mix_all, scale_all, norm_mix_all,
        *[{k: as_2d(t, k) for k, t in d.items()} for d in (small_w, small_m, small_v)])
    loss = loss_all[0, 0]
    for name in SMALL_NAMES:
        shape = small_w[name].shape
        res[name] = [(t[:, :N_HEADS] if name == "attn_sinks" else t).reshape(shape) for t in small_res[name]]
    update(["p_pool", "p_attn", "w_out"], *_exchange_wait("ex_proj_wait", ex_proj, loss_all))

    order = ["norm_mix", "w_in", "pool_w", "pool_b", "pool_scale", "attn_sinks", "p_pool", "p_attn", "w_out",
             "norm_mlp", "w_up", "w_down", "norm_final"]
    out = [loss, dx[None]]
    for kind in range(4):
        out += [res[name][kind] for name in order]
    return tuple(out)
```

```python
import functools
import math

import numpy as np
import jax
import jax.numpy as jnp
from jax import lax
from jax.experimental import pallas as pl
from jax.experimental.pallas import tpu as pltpu

F32 = jnp.float32
BF16 = jnp.bfloat16

D_MODEL = 1024
POOL_WIDTH = 512
ATTN_WIDTH = 512
KV_WIDTH = 128
HEAD_DIM = 64
N_HEADS = 8
N_KV_HEADS = 2
GROUP = 4
BLOCK = 128
POOL_WINDOWS = (2, 4, 8, 16)
POOL_GROUP_DIM = 128
POOL_HALO = 16
D_FF = 4096
FF_CHUNK = 1024
IN_WIDTH = 3328
RMS_EPS = 1e-5
NEG_INF = -1e30
ATTN_SCALE = 1.0 / math.sqrt(HEAD_DIM)
N_DEV = 8

ADAM_LR = 0.001
ADAM_B1 = 0.9
ADAM_B2 = 0.999
ADAM_EPS = 1e-08
ADAM_WD = 0.01
ADAM_STEP = 10

LANES = 128
VMEM_LIMIT_BYTES = 56 * 1024 * 1024
MESH = pl.DeviceIdType.MESH


def _params(n_grid_axes=1):
    return pltpu.CompilerParams(
        dimension_semantics=("arbitrary",) * n_grid_axes, vmem_limit_bytes=VMEM_LIMIT_BYTES)


def _dot(a, b):
    return jnp.dot(a, b, preferred_element_type=F32)


def _dot_nt(a, b):
    return lax.dot_general(a, b, (((1,), (1,)), ((), ())), preferred_element_type=F32)


def _dot_tn(a, b):
    return lax.dot_general(a, b, (((0,), (0,)), ((), ())), preferred_element_type=F32)


ANY = pl.BlockSpec(memory_space=pl.ANY)


def _rows(tm, n):
    return pl.BlockSpec((tm, n), lambda i: (i, 0))


def _whole(shape):
    zeros = (0,) * len(shape)
    return pl.BlockSpec(shape, lambda i: zeros)


def _rms_fwd(h, g):
    r = lax.rsqrt(jnp.mean(h * h, axis=-1, keepdims=True) + RMS_EPS)
    xh = h * r
    return r, xh, xh * g


def _rms_bwd(dy, xh, r, g):
    dxh = dy * g
    dh = r * (dxh - xh * jnp.mean(dxh * xh, axis=-1, keepdims=True))
    return dh, jnp.sum(dy * xh, axis=0, keepdims=True)


def _fwd_in(after, x, g_mix, w_in_t):
    s_len = x.shape[0]
    tm = min(512, s_len)

    def body(after_ref, x_ref, g_ref, w_ref, u_ref, zp_ref, q_ref, kv_ref, zg_ref):
        _, _, u = _rms_fwd(x_ref[...], g_ref[...])
        u = u.astype(BF16)
        u_ref[...] = u
        zp_ref[...] = _dot_nt(u, w_ref[0:512, :]).astype(BF16)
        q_ref[...] = _dot_nt(u, w_ref[512:1024, :]).astype(BF16)
        kv_ref[...] = _dot_nt(u, w_ref[1024:1280, :]).astype(BF16)
        zg_ref[...] = _dot_nt(u, w_ref[1280:3328, :]).astype(BF16)

    return pl.pallas_call(
        body, name="fwd_in", grid=(s_len // tm,),
        in_specs=[ANY, _rows(tm, D_MODEL), _whole((1, D_MODEL)),
                  pl.BlockSpec((IN_WIDTH, D_MODEL), lambda i: (0, 0), pipeline_mode=pl.Buffered(1))],
        out_specs=[_rows(tm, D_MODEL), _rows(tm, 512), _rows(tm, 512), _rows(tm, 256), _rows(tm, 2048)],
        out_shape=[jax.ShapeDtypeStruct((s_len, n), BF16) for n in (D_MODEL, 512, 512, 256, 2048)],
        compiler_params=_params(),
    )(after, x, g_mix, w_in_t)


def _attn_constants(sinks):
    r = np.arange(BLOCK)[:, None]
    qi = np.arange(BLOCK)[None, :]
    dist = np.where(r <= qi, qi - r, BLOCK + qi - r).astype(np.float32)
    slopes = np.array([2.0 ** (-8.0 * (h + 1) / N_HEADS) for h in range(N_HEADS)], dtype=np.float32)
    bias = (-slopes[:, None, None] * dist[None]).reshape(N_KV_HEADS, GROUP, BLOCK, BLOCK)
    bias = np.ascontiguousarray(bias.transpose(0, 2, 1, 3)).reshape(N_KV_HEADS, BLOCK, GROUP * BLOCK)
    sink_row = jnp.repeat(sinks.astype(F32).reshape(N_KV_HEADS, GROUP), BLOCK, axis=1)[:, None, :]
    return jnp.asarray(bias.astype(np.float32)), sink_row


def _own_block_mask():
    shape = (BLOCK, GROUP * BLOCK)
    r = lax.broadcasted_iota(jnp.int32, shape, 0)
    qi = lax.broadcasted_iota(jnp.int32, shape, 1) & (BLOCK - 1)
    return r <= qi


def _pack_keys(t, own):
    return jnp.where(own, t[BLOCK:], t[:BLOCK])


def _unpack_keys(t, own):
    zero = jnp.zeros_like(t)
    return jnp.concatenate([jnp.where(own, zero, t), jnp.where(own, t, zero)], axis=0)


def _left_half(shape):
    return lax.broadcasted_iota(jnp.int32, shape, 1) < HEAD_DIM


def _dup_halves(slab):
    swapped = pltpu.roll(slab, HEAD_DIM, 1)
    left = _left_half(slab.shape)
    return jnp.where(left, slab, swapped), jnp.where(left, swapped, slab)


def _fill_kv_slabs(kvh_ref, kv_ref, ka_ref, vd_ref):
    for rows, src in ((slice(0, BLOCK), kvh_ref), (slice(BLOCK, None), kv_ref)):
        kvf = src[...].astype(F32)
        for ref, lanes in ((ka_ref, slice(0, KV_WIDTH)), (vd_ref, slice(KV_WIDTH, 2 * KV_WIDTH))):
            d0, d1 = _dup_halves(kvf[:, lanes])
            ref[0, rows, :] = d0.astype(BF16)
            ref[1, rows, :] = d1.astype(BF16)


def _stack_pairs(a, h):
    pieces = []
    for j in range(2):
        pair = a[:, h * 256 + j * LANES:h * 256 + (j + 1) * LANES]
        left = _left_half(pair.shape)
        zero = jnp.zeros_like(pair)
        pieces += [jnp.where(left, pair, zero), jnp.where(left, zero, pair)]
    return jnp.concatenate(pieces, axis=0)


def _attn_probs(kk, q_st, bias_p, sink_row, own, first):
    s = _pack_keys(_dot_nt(kk, q_st), own) * ATTN_SCALE + bias_p
    if first is not None:
        s = jnp.where(jnp.logical_and(first, jnp.logical_not(own)), NEG_INF, s)
    m = jnp.maximum(jnp.max(s, axis=0, keepdims=True), sink_row)
    p = jnp.exp(s - m)
    es = jnp.exp(sink_row - m)
    inv = 1.0 / (jnp.sum(p, axis=0, keepdims=True) + es)
    return p * inv, es * inv


def _pool_d(ext, cur, g, row0):
    w = POOL_WINDOWS[g]
    acc = ext
    k = 1
    while k < w:
        acc = acc + pltpu.roll(acc, k, 0)
        k *= 2
    return _window_mean(acc[POOL_HALO:, :], w, row0) - cur


def _window_mean(total, w, row0):
    t = row0 + lax.broadcasted_iota(jnp.int32, (POOL_HALO, total.shape[1]), 0)
    head = total[:POOL_HALO] / jnp.minimum(t + 1, w).astype(F32)
    return jnp.concatenate([head, total[POOL_HALO:] * (1.0 / w)], axis=0)


def _mixers_fwd(zp, q, kv, pool_w, pool_b, pool_scale, bias_t, sink_row):
    s_len = zp.shape[0]
    tq = min(512, s_len)
    nb = tq // BLOCK

    def body(zp_ref, zph_ref, q_ref, kv_ref, kvh_ref, pw_ref, pb_ref, ps_ref, bias_ref, sink_ref,
             pm_ref, o_ref, ka_ref, vd_ref):
        i = pl.program_id(0)
        cur = zp_ref[...].astype(F32)
        halo = zph_ref[...].astype(F32) * (i > 0).astype(F32)
        ext = jnp.concatenate([halo, cur], axis=0)
        for g in range(4):
            sl = slice(g * POOL_GROUP_DIM, (g + 1) * POOL_GROUP_DIM)
            d = _pool_d(ext[:, sl], cur[:, sl], g, i * tq)
            y = _dot(d.astype(BF16), pw_ref[g]) + pb_ref[:, sl]
            pm_ref[:, sl] = (y * ps_ref[:, sl]).astype(BF16)
        _fill_kv_slabs(kvh_ref, kv_ref, ka_ref, vd_ref)
        own = _own_block_mask()
        for b in range(nb):
            rq = slice(b * BLOCK, (b + 1) * BLOCK)
            rk = slice(b * BLOCK, (b + 2) * BLOCK)
            qb = q_ref[rq, :]
            for h in range(N_KV_HEADS):
                pn, _ = _attn_probs(ka_ref[h, rk, :], _stack_pairs(qb, h), bias_ref[h], sink_ref[h], own,
                                    (i == 0) if b == 0 else None)
                pn = _unpack_keys(pn, own).astype(BF16)
                vd = vd_ref[h, rk, :]
                left = _left_half(vd.shape)
                zero = jnp.zeros_like(vd)
                va, vb = jnp.where(left, vd, zero), jnp.where(left, zero, vd)
                for j in range(2):
                    o_pair = (_dot_tn(pn[:, (2 * j) * BLOCK:(2 * j + 1) * BLOCK], va)
                              + _dot_tn(pn[:, (2 * j + 1) * BLOCK:(2 * j + 2) * BLOCK], vb))
                    o_ref[rq, h * 256 + j * LANES:h * 256 + (j + 1) * LANES] = o_pair.astype(BF16)

    halo_pool = pl.BlockSpec((POOL_HALO, 512), lambda i: (jnp.maximum(i * (tq // POOL_HALO) - 1, 0), 0))
    halo_kv = pl.BlockSpec((BLOCK, 256), lambda i: (jnp.maximum(i * nb - 1, 0), 0))
    return pl.pallas_call(
        body, name="mixers_fwd", grid=(s_len // tq,),
        in_specs=[_rows(tq, 512), halo_pool, _rows(tq, 512), _rows(tq, 256), halo_kv,
                  _whole((4, 128, 128)), _whole((1, 512)), _whole((1, 512)),
                  _whole((N_KV_HEADS, BLOCK, GROUP * BLOCK)), _whole((N_KV_HEADS, 1, GROUP * BLOCK))],
        out_specs=[_rows(tq, 512), _rows(tq, 512)],
        out_shape=[jax.ShapeDtypeStruct((s_len, 512), BF16)] * 2,
        scratch_shapes=[pltpu.VMEM((N_KV_HEADS, tq + BLOCK, LANES), BF16)] * 2,
        compiler_params=_params(),
    )(zp, zp, q, kv, kv, pool_w, pool_b, pool_scale, bias_t, sink_row)


def _gated_mix(pm, o, zg, pp_ref, pa_ref):
    yp = _dot(pm, pp_ref[...])
    ya = _dot(o, pa_ref[...])
    gp = jax.nn.sigmoid(zg[:, :D_MODEL].astype(F32))
    ga = jax.nn.sigmoid(zg[:, D_MODEL:].astype(F32))
    return yp, ya, gp, ga


def _core(x, pm, o, zg, tgt, g_mlp, g_fin, p_pool, p_attn, w_out, w_up_blocks, w_down):
    s_len = x.shape[0]
    tm = min(256, s_len)
    n_chunks = D_FF // FF_CHUNK
    up_block = D_FF // N_DEV
    per_chunk = FF_CHUNK // up_block

    def body(x_ref, pm_ref, o_ref, zg_ref, tgt_ref, gm_ref, gf_ref, pp_ref, pa_ref, wo_ref, wu_ref, wd_ref,
             mixed_ref, dh1_ref, a_ref, dap_ref, u2_ref, dh2_ref, small_ref,
             dyp_ref, dya_ref, dzg_ref, dpm_ref, do_ref, dh1b_ref):
        i = pl.program_id(0)

        @pl.when(i == 0)
        def _():
            small_ref[...] = jnp.zeros_like(small_ref)

        yp, ya, gp, ga = _gated_mix(pm_ref[...], o_ref[...], zg_ref[...], pp_ref, pa_ref)
        mixed = (gp * yp + ga * ya).astype(BF16)
        mixed_ref[...] = mixed
        h1 = x_ref[...] + _dot(mixed, wo_ref[...])
        r2, xh2, u2 = _rms_fwd(h1, gm_ref[...])
        u2 = u2.astype(BF16)
        u2_ref[...] = u2
        acc = jnp.zeros((tm, D_MODEL), F32)
        for c in range(n_chunks):
            cs = slice(c * FF_CHUNK, (c + 1) * FF_CHUNK)
            a = jnp.concatenate([_dot(u2, wu_ref[per_chunk * c + j]) for j in range(per_chunk)], axis=1)
            a = jnp.maximum(a, 0.0)
            a_ref[:, cs] = a.astype(BF16)
            acc = acc + _dot((a * a).astype(BF16), wd_ref[cs, :])
        h2 = h1 + acc
        r3, xh3, y = _rms_fwd(h2, gf_ref[...])
        diff = y - tgt_ref[...]
        small_ref[2:3, :] += 0.5 * jnp.sum(jnp.mean(diff * diff, axis=-1, keepdims=True))
        dy = diff * (1.0 / D_MODEL)
        dh2, dgf = _rms_bwd(dy, xh3, r3, gf_ref[...])
        small_ref[1:2, :] += dgf
        dh2_bf = dh2.astype(BF16)
        dh2_ref[...] = dh2_bf
        du2 = jnp.zeros((tm, D_MODEL), F32)
        for c in range(n_chunks):
            cs = slice(c * FF_CHUNK, (c + 1) * FF_CHUNK)
            ds = _dot_nt(dh2_bf, wd_ref[cs, :])
            dap = (ds * (2.0 * a_ref[:, cs].astype(F32))).astype(BF16)
            dap_ref[:, cs] = dap
            for j in range(per_chunk):
                du2 = du2 + _dot_nt(dap[:, j * up_block:(j + 1) * up_block], wu_ref[per_chunk * c + j])
        dh1n, dgm = _rms_bwd(du2, xh2, r2, gm_ref[...])
        small_ref[0:1, :] += dgm
        dh1 = dh2 + dh1n
        dh1_ref[...] = dh1
        dh1_bf = dh1.astype(BF16)
        dh1b_ref[...] = dh1_bf
        dm = _dot_nt(dh1_bf, wo_ref[...])
        dyp = (dm * gp).astype(BF16)
        dya = (dm * ga).astype(BF16)
        dyp_ref[...] = dyp
        dya_ref[...] = dya
        dzg_ref[:, :D_MODEL] = (dm * yp * (gp * (1.0 - gp))).astype(BF16)
        dzg_ref[:, D_MODEL:] = (dm * ya * (ga * (1.0 - ga))).astype(BF16)
        dpm_ref[...] = _dot_nt(dyp, pp_ref[...]).astype(BF16)
        do_ref[...] = _dot_nt(dya, pa_ref[...]).astype(BF16)

    def fixed(shape):
        return pl.BlockSpec(shape, lambda i: (0,) * len(shape), pipeline_mode=pl.Buffered(1))

    widths_dtypes = ((D_MODEL, BF16), (D_MODEL, F32), (D_FF, BF16), (D_FF, BF16), (D_MODEL, BF16), (D_MODEL, BF16))
    back = ((D_MODEL, BF16), (D_MODEL, BF16), (2048, BF16), (512, BF16), (512, BF16), (D_MODEL, BF16))
    return pl.pallas_call(
        body, name="core", grid=(s_len // tm,),
        in_specs=[_rows(tm, D_MODEL), _rows(tm, 512), _rows(tm, 512), _rows(tm, 2048), _rows(tm, D_MODEL),
                  _whole((1, D_MODEL)), _whole((1, D_MODEL)),
                  fixed((512, D_MODEL)), fixed((512, D_MODEL)), fixed((D_MODEL, D_MODEL)),
                  fixed((N_DEV, D_MODEL, up_block)), fixed((D_FF, D_MODEL))],
        out_specs=[_rows(tm, n) for n, _ in widths_dtypes] + [_whole((8, D_MODEL))] + [_rows(tm, n) for n, _ in back],
        out_shape=[jax.ShapeDtypeStruct((s_len, n), d) for n, d in widths_dtypes]
        + [jax.ShapeDtypeStruct((8, D_MODEL), F32)] + [jax.ShapeDtypeStruct((s_len, n), d) for n, d in back],
        compiler_params=_params(),
    )(x, pm, o, zg, tgt, g_mlp, g_fin, p_pool, p_attn, w_out, w_up_blocks, w_down)


def _tn_matmul(a, b, square_a=False, col_blocks=None, after=None):
    s_len, ka = a.shape
    nb = b.shape[1]
    tt = min(2048, s_len)
    tk = min(1024, ka)
    tn = min(1024, nb)
    n_t = s_len // tt
    if col_blocks is None:
        out_spec = pl.BlockSpec((tk, tn), lambda k, j, t: (k, j))
        out_shape = jax.ShapeDtypeStruct((ka, nb), BF16)
    else:
        width = nb // col_blocks
        per_tile = tn // width
        out_spec = pl.BlockSpec((per_tile, tk, width), lambda k, j, t: (j, k, 0))
        out_shape = jax.ShapeDtypeStruct((col_blocks, ka, width), BF16)

    extra = [] if after is None else [after]

    def body(a_ref, b_ref, *rest):
        o_ref, acc_ref = rest[len(extra):]
        t = pl.program_id(2)

        @pl.when(t == 0)
        def _():
            acc_ref[...] = jnp.zeros_like(acc_ref)

        av = a_ref[...]
        if square_a:
            av = av * av
        acc_ref[...] += _dot_tn(av.astype(BF16), b_ref[...].astype(BF16))

        @pl.when(t == n_t - 1)
        def _():
            if col_blocks is None:
                o_ref[...] = acc_ref[...].astype(o_ref.dtype)
            else:
                for blk in range(per_tile):
                    o_ref[blk] = acc_ref[:, blk * width:(blk + 1) * width].astype(o_ref.dtype)

    return pl.pallas_call(
        body, name="tn_matmul", grid=(ka // tk, nb // tn, n_t),
        in_specs=[pl.BlockSpec((tt, tk), lambda k, j, t: (t, k)), pl.BlockSpec((tt, tn), lambda k, j, t: (t, j))]
        + [ANY] * len(extra),
        out_specs=out_spec, out_shape=out_shape,
        scratch_shapes=[pltpu.VMEM((tk, tn), F32)],
        compiler_params=_params(3),
    )(a, b, *extra)


def _tn_w_in(u, dzp, dq, dkv, dzg):
    s_len = u.shape[0]
    tt = min(1024, s_len)
    n_t = s_len // tt
    width = IN_WIDTH // N_DEV
    pieces = ((0, 512), (512, 1024), (1024, 1280), (1280, IN_WIDTH))

    def body(u_ref, dzp_ref, dq_ref, dkv_ref, dzg_ref, o_ref, acc_ref):
        t = pl.program_id(0)

        @pl.when(t == 0)
        def _():
            acc_ref[...] = jnp.zeros_like(acc_ref)

        uv = u_ref[...]
        for (c0, c1), ref in zip(pieces, (dzp_ref, dq_ref, dkv_ref, dzg_ref)):
            acc_ref[c0:c1, :] += _dot_tn(ref[...], uv)

        @pl.when(t == n_t - 1)
        def _():
            for j in range(N_DEV):
                o_ref[j] = acc_ref[j * width:(j + 1) * width, :].astype(BF16)

    return pl.pallas_call(
        body, name="tn_w_in", grid=(n_t,),
        in_specs=[_rows(tt, D_MODEL)] + [_rows(tt, c1 - c0) for c0, c1 in pieces],
        out_specs=_whole((N_DEV, width, D_MODEL)),
        out_shape=jax.ShapeDtypeStruct((N_DEV, width, D_MODEL), BF16),
        scratch_shapes=[pltpu.VMEM((IN_WIDTH, D_MODEL), F32)],
        compiler_params=_params(),
    )(u, dzp, dq, dkv, dzg)


MIX_POOL_B = 4 * POOL_GROUP_DIM
MIX_SINKS = MIX_POOL_B + 8
MIX_ROWS = MIX_SINKS + 8


def _mixers_bwd(after, zp, q, kv, dpm, do, pool_w, pool_b, pool_scale, bias_t, sink_row):
    s_len = zp.shape[0]
    tq = min(512, s_len)
    nb = tq // BLOCK
    n_steps = s_len // tq

    def body(after_ref, zp_ref, zph_ref, q_ref, kv_ref, kvh_ref, dpm_ref, dpmh_ref, do_ref, pw_ref, pb_ref, ps_ref,
             bias_ref, sink_ref, dzp_ref, dq_ref, dkv_ref, small_ref, dps_ref,
             ka_ref, vd_ref, dsk_acc, dkv_acc):
        i = pl.program_id(0)

        @pl.when(i == 0)
        def _():
            dkv_acc[...] = jnp.zeros_like(dkv_acc)
            small_ref[...] = jnp.zeros_like(small_ref)
            dps_ref[...] = jnp.zeros_like(dps_ref)
            dsk_acc[...] = jnp.zeros_like(dsk_acc)

        cur = zp_ref[...].astype(F32)
        halo = zph_ref[...].astype(F32) * (i > 0).astype(F32)
        ext = jnp.concatenate([halo, cur], axis=0)
        dpm_next = dpmh_ref[...].astype(F32) * (i < n_steps - 1).astype(F32)
        dpm_ext = jnp.concatenate([dpm_ref[...].astype(F32), dpm_next], axis=0)
        n_ext = tq + POOL_HALO
        for g in range(4):
            sl = slice(g * POOL_GROUP_DIM, (g + 1) * POOL_GROUP_DIM)
            w = POOL_WINDOWS[g]
            d = _pool_d(ext[:, sl], cur[:, sl], g, i * tq).astype(BF16)
            y_lin = _dot(d, pw_ref[g]) + pb_ref[:, sl]
            dps_ref[:, sl] += jnp.sum(dpm_ext[:tq, sl] * y_lin, axis=0, keepdims=True)
            dyl_ext = dpm_ext[:, sl] * ps_ref[:, sl]
            small_ref[MIX_POOL_B + g:MIX_POOL_B + g + 1, :] += jnp.sum(dyl_ext[:tq], axis=0, keepdims=True)
            dyl_bf = dyl_ext.astype(BF16)
            small_ref[g * POOL_GROUP_DIM:(g + 1) * POOL_GROUP_DIM, :] += _dot_tn(d, dyl_bf[:tq])
            dd = _dot_nt(dyl_bf, pw_ref[g])
            e = _window_mean(dd, w, i * tq)
            acc = e
            k = 1
            while k < w:
                acc = acc + pltpu.roll(acc, n_ext - k, 0)
                k *= 2
            dzp_ref[:, sl] = (acc[:tq] - dd[:tq]).astype(BF16)

        _fill_kv_slabs(kvh_ref, kv_ref, ka_ref, vd_ref)

        def fold(dup):
            return dup + pltpu.roll(dup, HEAD_DIM, 1)

        own = _own_block_mask()
        for b in range(nb):
            rq = slice(b * BLOCK, (b + 1) * BLOCK)
            rk = slice(b * BLOCK, (b + 2) * BLOCK)
            qb = q_ref[rq, :]
            dob = do_ref[rq, :]
            dk_dup, dv_dup = [], []
            for h in range(N_KV_HEADS):
                kk = ka_ref[h, rk, :]
                q_st = _stack_pairs(qb, h)
                do_st = _stack_pairs(dob, h)
                pn, psink = _attn_probs(kk, q_st, bias_ref[h], sink_ref[h], own, (i == 0) if b == 0 else None)
                dp = _pack_keys(_dot_nt(vd_ref[h, rk, :], do_st), own)
                delta = jnp.sum(pn * dp, axis=0, keepdims=True)
                dsk_acc[h] += -psink * delta
                ds = _unpack_keys((pn * (dp - delta)) * ATTN_SCALE, own).astype(BF16)
                pn = _unpack_keys(pn, own)
                dq_st = _dot_tn(ds, kk)
                for j in range(2):
                    left = _left_half((BLOCK, LANES))
                    dq_pair = jnp.where(left, dq_st[(2 * j) * BLOCK:(2 * j + 1) * BLOCK],
                                        dq_st[(2 * j + 1) * BLOCK:(2 * j + 2) * BLOCK])
                    dq_ref[rq, h * 256 + j * LANES:h * 256 + (j + 1) * LANES] = dq_pair.astype(BF16)
                dk_dup.append(fold(_dot(ds, q_st)))
                dv_dup.append(fold(_dot(pn.astype(BF16), do_st)))
            left = _left_half((2 * BLOCK, LANES))
            dkv_blk = jnp.concatenate([jnp.where(left, dk_dup[0], dk_dup[1]),
                                       jnp.where(left, dv_dup[0], dv_dup[1])], axis=1)
            g0 = pl.multiple_of(i * tq + b * BLOCK, BLOCK)
            dkv_acc[pl.ds(g0, 2 * BLOCK), :] += dkv_blk

        @pl.when(i == n_steps - 1)
        def _():
            dkv_ref[...] = dkv_acc[BLOCK:, :].astype(BF16)
            lane = lax.broadcasted_iota(jnp.int32, (1, LANES), 1)
            row = jnp.zeros((1, LANES), F32)
            for h in range(N_KV_HEADS):
                for g in range(GROUP):
                    tot = jnp.sum(dsk_acc[h, :, g * BLOCK:(g + 1) * BLOCK], axis=1, keepdims=True)
                    row = jnp.where(lane == GROUP * h + g, tot, row)
            small_ref[MIX_SINKS:MIX_SINKS + 1, :] = row

    blocks_per_tile = tq // POOL_HALO
    last_halo = s_len // POOL_HALO - 1
    halo_prev = pl.BlockSpec((POOL_HALO, 512), lambda i: (jnp.maximum(i * blocks_per_tile - 1, 0), 0))
    halo_next = pl.BlockSpec((POOL_HALO, 512), lambda i: (jnp.minimum((i + 1) * blocks_per_tile, last_halo), 0))
    halo_kv = pl.BlockSpec((BLOCK, 256), lambda i: (jnp.maximum(i * nb - 1, 0), 0))
    return pl.pallas_call(
        body, name="mixers_bwd", grid=(n_steps,),
        in_specs=[ANY, _rows(tq, 512), halo_prev, _rows(tq, 512), _rows(tq, 256), halo_kv,
                  _rows(tq, 512), halo_next, _rows(tq, 512),
                  _whole((4, 128, 128)), _whole((1, 512)), _whole((1, 512)),
                  _whole((N_KV_HEADS, BLOCK, GROUP * BLOCK)), _whole((N_KV_HEADS, 1, GROUP * BLOCK))],
        out_specs=[_rows(tq, 512), _rows(tq, 512), _whole((s_len, 256)),
                   _whole((MIX_ROWS, LANES)), _whole((1, 512))],
        out_shape=[jax.ShapeDtypeStruct((s_len, 512), BF16), jax.ShapeDtypeStruct((s_len, 512), BF16),
                   jax.ShapeDtypeStruct((s_len, 256), BF16), jax.ShapeDtypeStruct((MIX_ROWS, LANES), F32),
                   jax.ShapeDtypeStruct((1, 512), F32)],
        scratch_shapes=[pltpu.VMEM((N_KV_HEADS, tq + BLOCK, LANES), BF16)] * 2
        + [pltpu.VMEM((N_KV_HEADS, 1, GROUP * BLOCK), F32), pltpu.VMEM((s_len + BLOCK, 256), F32)],
        compiler_params=_params(),
    )(after, zp, zp, q, kv, kv, dpm, dpm, do, pool_w, pool_b, pool_scale, bias_t, sink_row)


def _in_bwd(after, dzp, dq, dkv, dzg, w_in_t, x, dh1, g_mix):
    s_len = x.shape[0]
    tm = min(512, s_len)

    def body(after_ref, dzp_ref, dq_ref, dkv_ref, dzg_ref, w_ref, x_ref, dh1_ref, g_ref, dx_ref, dg_ref):
        i = pl.program_id(0)

        @pl.when(i == 0)
        def _():
            dg_ref[...] = jnp.zeros_like(dg_ref)

        du = _dot(dzp_ref[...], w_ref[0:512, :])
        du = du + _dot(dq_ref[...], w_ref[512:1024, :])
        du = du + _dot(dkv_ref[...], w_ref[1024:1280, :])
        du = du + _dot(dzg_ref[...], w_ref[1280:3328, :])
        r, xh, _ = _rms_fwd(x_ref[...], g_ref[...])
        dxn, dg = _rms_bwd(du, xh, r, g_ref[...])
        dg_ref[...] += dg
        dx_ref[...] = dh1_ref[...] + dxn

    return pl.pallas_call(
        body, name="in_bwd", grid=(s_len // tm,),
        in_specs=[ANY, _rows(tm, 512), _rows(tm, 512), _rows(tm, 256), _rows(tm, 2048), _whole((IN_WIDTH, D_MODEL)),
                  _rows(tm, D_MODEL), _rows(tm, D_MODEL), _whole((1, D_MODEL))],
        out_specs=[_rows(tm, D_MODEL), _whole((1, D_MODEL))],
        out_shape=[jax.ShapeDtypeStruct((s_len, D_MODEL), F32), jax.ShapeDtypeStruct((1, D_MODEL), F32)],
        compiler_params=_params(),
    )(after, dzp, dq, dkv, dzg, w_in_t, x, dh1, g_mix)


def _all_gather_weights(name, shards, after=None):
    n = len(shards)
    extra = [] if after is None else [after]
    n_extra = len(extra)

    def body(*refs):
        ins, outs = refs[:n], refs[n + n_extra:2 * n + n_extra]
        send_sems, recv_sems, local_sems = refs[2 * n + n_extra:]
        x, y, c = lax.axis_index("x"), lax.axis_index("y"), lax.axis_index("c")
        me, sibling = (x, y, c), (x, y, 1 - c)
        chips = [(1 - x, y), (x, 1 - y), (1 - x, 1 - y)]

        def slot(a, px, py, pc):
            return outs[a].at[4 * px + 2 * py + pc]

        def copy(a, k, block, to, src=None):
            return pltpu.make_async_remote_copy(
                src_ref=slot(a, *block) if src is None else src, dst_ref=slot(a, *block),
                send_sem=send_sems.at[a, k], recv_sem=recv_sems.at[a, k], device_id=to, device_id_type=MESH)

        mine = [pltpu.make_async_copy(ins[a], slot(a, *me), local_sems.at[a]) for a in range(n)]
        for cp in mine:
            cp.start()
        first = []
        for a in range(n):
            first.append(copy(a, 0, me, sibling, src=ins[a]))
            first += [copy(a, 1 + j, me, (*chip, c), src=ins[a]) for j, chip in enumerate(chips)]
        for cp in first:
            cp.start()
        passed = []
        for a in range(n):
            for j, chip in enumerate(chips):
                copy(a, 1 + j, (*chip, c), me).wait_recv()
                cp = copy(a, 4 + j, (*chip, c), sibling)
                cp.start()
                passed.append(cp)
        for a in range(n):
            copy(a, 0, sibling, me).wait_recv()
            for j, chip in enumerate(chips):
                copy(a, 4 + j, (*chip, 1 - c), me).wait_recv()
        for cp in first + passed:
            cp.wait_send()
        for cp in mine:
            cp.wait()

    return pl.pallas_call(
        body, name=name,
        in_specs=[ANY] * (n + n_extra), out_specs=[ANY] * n,
        out_shape=[jax.ShapeDtypeStruct((N_DEV,) + s.shape, s.dtype) for s in shards],
        scratch_shapes=[pltpu.SemaphoreType.DMA((n, 7)), pltpu.SemaphoreType.DMA((n, 7)), pltpu.SemaphoreType.DMA((n,))],
    )(*shards, *extra)


HBM_SPEC = pl.BlockSpec(memory_space=pltpu.HBM)
SEM_SPEC = pl.BlockSpec(memory_space=pltpu.SEMAPHORE)
DATAFLOW = pltpu.SideEffectType.DATAFLOW_SIDE_EFFECTING
N_PEERS = N_DEV - 1


CHIP_PEERS = (1, 2, 4, 6)
RELAYED = (2, 4, 6)


def _peer_copies(srcs, lands, scatter, send_sems, recv_sems):
    x, y, c = lax.axis_index("x"), lax.axis_index("y"), lax.axis_index("c")
    me_idx = 4 * x + 2 * y + c
    copies = []
    for k in range(1, N_DEV):
        px = 1 - x if (k >> 2) & 1 else x
        py = 1 - y if (k >> 1) & 1 else y
        pc = 1 - c if k & 1 else c
        p_idx = 4 * px + 2 * py + pc
        for a in range(len(srcs)):
            if scatter[a] == "chip" and k not in CHIP_PEERS:
                continue
            src = srcs[a].at[p_idx] if scatter[a] is True else srcs[a]
            dst = lands[a].at[k] if scatter[a] is True else lands[a].at[me_idx]
            copies.append(pltpu.make_async_remote_copy(
                src_ref=src, dst_ref=dst, send_sem=send_sems.at[a * N_PEERS + k - 1],
                recv_sem=recv_sems.at[a * N_PEERS + k - 1],
                device_id=(px, py, pc), device_id_type=MESH))
    return copies


def _exchange_start(name, srcs, scatter, after):
    n = len(srcs)
    lands = [lax.empty(s.shape if sc is True else (N_DEV,) + s.shape, s.dtype) for s, sc in zip(srcs, scatter)]

    def body(*refs):
        src_refs, land_refs = refs[:n], refs[n:2 * n]
        send_sems, recv_sems = refs[2 * n + 1], refs[2 * n + 2]
        token = refs[4 * n + 3]
        for cp in _peer_copies(src_refs, land_refs, scatter, send_sems, recv_sems):
            cp.start()
        token[...] = jnp.zeros_like(token)

    hbm = lambda t: pltpu.HBM(t.shape, t.dtype)
    outs = pl.pallas_call(
        body, name=name,
        out_shape=[pltpu.SemaphoreType.DMA((n * N_PEERS,)), pltpu.SemaphoreType.DMA((n * N_PEERS,))]
        + [hbm(t) for t in srcs] + [hbm(t) for t in lands] + [jax.ShapeDtypeStruct((8, LANES), F32)],
        in_specs=[HBM_SPEC] * (2 * n) + [ANY],
        out_specs=[SEM_SPEC, SEM_SPEC] + [HBM_SPEC] * (2 * n) + [pl.BlockSpec(memory_space=pltpu.VMEM)],
        input_output_aliases={i: 2 + i for i in range(2 * n)},
        compiler_params=pltpu.CompilerParams(has_side_effects=DATAFLOW),
    )(*[pltpu.with_memory_space_constraint(t, pltpu.HBM) for t in list(srcs) + lands], after)
    return dict(n=n, scatter=scatter, send_sems=outs[0], recv_sems=outs[1], srcs=outs[2:2 + n],
                lands=outs[2 + n:2 + 2 * n], token=outs[2 + 2 * n])


def _exchange_wait(name, handle, after):
    n, scatter = handle["n"], handle["scatter"]

    def body(*refs):
        src_refs, land_refs = refs[:n], refs[n:2 * n]
        send_sems, recv_sems = refs[2 * n], refs[2 * n + 1]
        for cp in _peer_copies(src_refs, land_refs, scatter, send_sems, recv_sems):
            cp.wait_send()
            cp.wait_recv()

    both = list(handle["srcs"]) + list(handle["lands"])
    outs = pl.pallas_call(
        body, name=name,
        out_shape=[pltpu.HBM(t.shape, t.dtype) for t in both],
        in_specs=[HBM_SPEC] * (2 * n) + [SEM_SPEC, SEM_SPEC, ANY],
        out_specs=[HBM_SPEC] * (2 * n),
        input_output_aliases={i: i for i in range(2 * n)},
        compiler_params=pltpu.CompilerParams(has_side_effects=DATAFLOW),
    )(*both, handle["send_sems"], handle["recv_sems"], after)
    me_idx = _my_index()
    lands = [land if sc is True else lax.dynamic_update_index_in_dim(land, src, me_idx, 0)
             for land, src, sc in zip(outs[n:], outs[:n], scatter)]
    return lands, outs[:n]


def _my_index():
    return 4 * lax.axis_index("x") + 2 * lax.axis_index("y") + lax.axis_index("c")


def _relay_copies(bufs, send_sems, recv_sems):
    x, y, c = lax.axis_index("x"), lax.axis_index("y"), lax.axis_index("c")
    copies = []
    for j, k in enumerate(RELAYED):
        px = 1 - x if (k >> 2) & 1 else x
        py = 1 - y if (k >> 1) & 1 else y
        slot = 4 * px + 2 * py + c
        for a, buf in enumerate(bufs):
            copies.append(pltpu.make_async_remote_copy(
                src_ref=buf.at[slot], dst_ref=buf.at[slot], send_sem=send_sems.at[a * len(RELAYED) + j],
                recv_sem=recv_sems.at[a * len(RELAYED) + j], device_id=(x, y, 1 - c), device_id_type=MESH))
    return copies


def _relay_start(name, bufs, after):
    n = len(bufs)

    def body(*refs):
        send_sems, recv_sems = refs[n + 1], refs[n + 2]
        for cp in _relay_copies(refs[:n], send_sems, recv_sems):
            cp.start()
        token = refs[2 * n + 3]
        token[...] = jnp.zeros_like(token)

    n_sems = n * len(RELAYED)
    outs = pl.pallas_call(
        body, name=name,
        out_shape=[pltpu.SemaphoreType.DMA((n_sems,)), pltpu.SemaphoreType.DMA((n_sems,))]
        + [pltpu.HBM(t.shape, t.dtype) for t in bufs] + [jax.ShapeDtypeStruct((8, LANES), F32)],
        in_specs=[HBM_SPEC] * n + [ANY],
        out_specs=[SEM_SPEC, SEM_SPEC] + [HBM_SPEC] * n + [pl.BlockSpec(memory_space=pltpu.VMEM)],
        input_output_aliases={i: 2 + i for i in range(n)},
        compiler_params=pltpu.CompilerParams(has_side_effects=DATAFLOW),
    )(*[pltpu.with_memory_space_constraint(t, pltpu.HBM) for t in bufs], after)
    return dict(n=n, send_sems=outs[0], recv_sems=outs[1], bufs=outs[2:2 + n], token=outs[2 + n])


def _relay_wait(name, handle, after):
    n = handle["n"]

    def body(*refs):
        for cp in _relay_copies(refs[:n], refs[n], refs[n + 1]):
            cp.wait_send()
            cp.wait_recv()

    return pl.pallas_call(
        body, name=name,
        out_shape=[pltpu.HBM(t.shape, t.dtype) for t in handle["bufs"]],
        in_specs=[HBM_SPEC] * n + [SEM_SPEC, SEM_SPEC, ANY],
        out_specs=[HBM_SPEC] * n,
        input_output_aliases={i: i for i in range(n)},
        compiler_params=pltpu.CompilerParams(has_side_effects=DATAFLOW),
    )(*handle["bufs"], handle["send_sems"], handle["recv_sems"], after)


def _adamw(parts, w, m, v, sent=None):
    r, c = w.shape
    tr = 256 if r % 256 == 0 else r
    own = sent is not None

    def body(*refs):
        if own:
            _, p_ref, own_ref, w_ref, m_ref, v_ref, g_ref, d_ref, nm_ref, nv_ref = refs
            g = own_ref[...].astype(F32)
        else:
            p_ref, w_ref, m_ref, v_ref, g_ref, d_ref, nm_ref, nv_ref = refs
            g = p_ref[0].astype(F32)
        for k in range(1, N_DEV):
            g = g + p_ref[k].astype(F32)
        m_new = ADAM_B1 * m_ref[...] + (1.0 - ADAM_B1) * g
        v_new = ADAM_B2 * v_ref[...] + (1.0 - ADAM_B2) * (g * g)
        m_hat = m_new / (1.0 - ADAM_B1 ** ADAM_STEP)
        v_hat = v_new / (1.0 - ADAM_B2 ** ADAM_STEP)
        g_ref[...] = g
        d_ref[...] = -ADAM_LR * (m_hat / (jnp.sqrt(v_hat) + ADAM_EPS) + ADAM_WD * w_ref[...])
        nm_ref[...] = m_new
        nv_ref[...] = v_new

    out_shape = [jax.ShapeDtypeStruct((r, c), F32)] * 4
    if not own:
        return pl.pallas_call(
            body, name="adamw", grid=(r // tr,),
            in_specs=[pl.BlockSpec((N_DEV, tr, c), lambda i: (0, i, 0))] + [_rows(tr, c)] * 3,
            out_specs=[_rows(tr, c)] * 4, out_shape=out_shape, compiler_params=_params(),
        )(parts, w, m, v)
    rows = pl.BlockSpec((tr, c), lambda i, me: (i, 0))
    return pl.pallas_call(
        body, name="adamw_own", out_shape=out_shape, compiler_params=_params(),
        grid_spec=pltpu.PrefetchScalarGridSpec(
            num_scalar_prefetch=1, grid=(r // tr,),
            in_specs=[pl.BlockSpec((N_DEV, tr, c), lambda i, me: (0, i, 0)),
                      pl.BlockSpec((None, tr, c), lambda i, me: (me[0], i, 0))] + [rows] * 3,
            out_specs=[rows] * 4),
    )(_my_index().reshape(1).astype(jnp.int32), parts, sent, w, m, v)


def _adam_step(g, w, m, v):
    m_new = ADAM_B1 * m + (1.0 - ADAM_B1) * g
    v_new = ADAM_B2 * v + (1.0 - ADAM_B2) * (g * g)
    m_hat = m_new / (1.0 - ADAM_B1 ** ADAM_STEP)
    v_hat = v_new / (1.0 - ADAM_B2 ** ADAM_STEP)
    return -ADAM_LR * (m_hat / (jnp.sqrt(v_hat) + ADAM_EPS) + ADAM_WD * w), m_new, v_new


SMALL_NAMES = ("norm_mix", "pool_w", "pool_b", "pool_scale", "attn_sinks", "norm_mlp", "norm_final")


def _adamw_small(mlp_all, mix_all, scale_all, nmix_all, w, m, v):
    def body(mlp_ref, mix_ref, scale_ref, nmix_ref, *refs):
        ins, outs = refs[:21], refs[21:]

        def total(ref, rows, lanes=slice(None)):
            g = ref[0, rows, lanes]
            for k in range(1, N_DEV):
                g = g + ref[k, rows, lanes]
            return g

        grads = dict(
            norm_mix=total(nmix_ref, slice(0, 1)), pool_w=total(mix_ref, slice(0, MIX_POOL_B)),
            pool_b=total(mix_ref, slice(MIX_POOL_B, MIX_POOL_B + 4)), pool_scale=total(scale_ref, slice(0, 1)),
            attn_sinks=total(mix_ref, slice(MIX_SINKS, MIX_SINKS + 1)),
            norm_mlp=total(mlp_ref, slice(0, 1)), norm_final=total(mlp_ref, slice(1, 2)))
        for i, name in enumerate(SMALL_NAMES):
            g = grads[name]
            d, m_new, v_new = _adam_step(g, ins[3 * i][...], ins[3 * i + 1][...], ins[3 * i + 2][...])
            for ref, val in zip(outs[4 * i:4 * i + 4], (g, d, m_new, v_new)):
                ref[...] = val
        outs[28][...] = jnp.broadcast_to(total(mlp_ref, slice(2, 3), slice(0, LANES)), (8, LANES))

    operands, out_shape = [], []
    for name in SMALL_NAMES:
        operands += [w[name], m[name], v[name]]
        out_shape += [jax.ShapeDtypeStruct(w[name].shape, F32)] * 4
    out_shape.append(jax.ShapeDtypeStruct((8, LANES), F32))
    outs = pl.pallas_call(body, name="adamw_small", out_shape=out_shape)(
        mlp_all, mix_all, scale_all, nmix_all, *operands)
    return {name: outs[4 * i:4 * i + 4] for i, name in enumerate(SMALL_NAMES)}, outs[28]


def kernel(x, norm_mix, w_in, pool_w, pool_b, pool_scale, attn_sinks, p_pool, p_attn, w_out, norm_mlp, w_up, w_down, norm_final, loss_target, m_norm_mix, m_w_in, m_pool_w, m_pool_b, m_pool_scale, m_attn_sinks, m_p_pool, m_p_attn, m_w_out, m_norm_mlp, m_w_up, m_w_down, m_norm_final, v_norm_mix, v_w_in, v_pool_w, v_pool_b, v_pool_scale, v_attn_sinks, v_p_pool, v_p_attn, v_w_out, v_norm_mlp, v_w_up, v_w_down, v_norm_final):
    xs = x[0]
    tgt = loss_target[0]
    s_len = xs.shape[0]

    p_pool_bf, p_attn_bf, w_out_bf, w_up_bf, w_down_bf = [
        t[0].astype(BF16) for t in (p_pool, p_attn, w_out, w_up, w_down)]
    w_in_bf = w_in[0].T.astype(BF16)
    (w_in_g,) = _all_gather_weights("all_gather_w_in", [w_in_bf])
    ag_rest = _exchange_start(
        "ag_rest_start", [p_pool_bf, p_attn_bf, w_out_bf, w_up_bf, w_down_bf], ("chip",) * 5, w_in_g)

    pool_w_bf = pool_w[0].astype(BF16)
    pool_b_row = pool_b[0].reshape(1, POOL_WIDTH)
    bias_t, sink_row = _attn_constants(attn_sinks[0])

    w_in_t = w_in_g.reshape(IN_WIDTH, D_MODEL)
    u, zp, q, kv, zg = _fwd_in(ag_rest["token"], xs, norm_mix, w_in_t)
    pm, o = _mixers_fwd(zp, q, kv, pool_w_bf, pool_b_row, pool_scale, bias_t, sink_row)
    first_level, _ = _exchange_wait("ag_rest_wait", ag_rest, o)
    relay = _relay_start("ag_relay_start", first_level, pm)
    p_pool_g, p_attn_g, w_out_g, w_up_g, w_down_g = _relay_wait("ag_relay_wait", relay, relay["token"])
    p_pool_f = p_pool_g.transpose(1, 0, 2).reshape(POOL_WIDTH, D_MODEL)
    p_attn_f = p_attn_g.transpose(1, 0, 2).reshape(ATTN_WIDTH, D_MODEL)
    w_out_f = w_out_g.reshape(D_MODEL, D_MODEL)
    w_down_f = w_down_g.reshape(D_FF, D_MODEL)
    mixed, dh1, a, dapre, u2, dh2, small_mlp, dyp, dya, dzg, dpm, do, dh1_bf = _core(
        xs, pm, o, zg, tgt, norm_mlp, norm_final.reshape(1, D_MODEL), p_pool_f, p_attn_f, w_out_f, w_up_g, w_down_f)
    gw_down = _tn_matmul(a, dh2, square_a=True)
    gw_up = _tn_matmul(u2, dapre, col_blocks=N_DEV)
    ex_mlp = _exchange_start(
        "ex_mlp_start", [gw_up, gw_down.reshape(N_DEV, D_FF // N_DEV, D_MODEL)], (True, True), small_mlp)
    dzp, dq, dkv, small_mix, g_pool_scale = _mixers_bwd(
        ex_mlp["token"], zp, q, kv, dpm, do, pool_w_bf, pool_b_row, pool_scale, bias_t, sink_row)
    gw_in = _tn_w_in(u, dzp, dq, dkv, dzg)
    ex_in = _exchange_start(
        "ex_in_start", [gw_in, small_mlp, small_mix, g_pool_scale], (True, False, False, False), dq)
    gw_out = _tn_matmul(mixed, dh1_bf, after=ex_in["token"])
    gp_pool = _tn_matmul(pm, dyp, col_blocks=N_DEV, after=ex_in["token"])
    gp_attn = _tn_matmul(o, dya, col_blocks=N_DEV, after=ex_in["token"])
    ex_proj = _exchange_start(
        "ex_proj_start", [gp_pool, gp_attn, gw_out.reshape(N_DEV, D_MODEL // N_DEV, D_MODEL)], (True,) * 3,
        ex_in["token"])
    dx, g_norm_mix = _in_bwd(ex_proj["token"], dzp, dq, dkv, dzg, w_in_t, xs, dh1, norm_mix)

    big_w = dict(w_in=w_in, p_pool=p_pool, p_attn=p_attn, w_out=w_out, w_up=w_up, w_down=w_down)
    big_m = dict(w_in=m_w_in, p_pool=m_p_pool, p_attn=m_p_attn, w_out=m_w_out, w_up=m_w_up, w_down=m_w_down)
    big_v = dict(w_in=v_w_in, p_pool=v_p_pool, p_attn=v_p_attn, w_out=v_w_out, w_up=v_w_up, w_down=v_w_down)
    res = {}

    def update(names, recvs, sents):
        for name, parts, sent in zip(names, recvs, sents):
            flip = (lambda t: t.T) if name == "w_in" else (lambda t: t)
            outs = _adamw(parts, flip(big_w[name][0]), flip(big_m[name][0]), flip(big_v[name][0]), sent)
            res[name] = [flip(t)[None] for t in outs]

    update(["w_up", "w_down"], *_exchange_wait("ex_mlp_wait", ex_mlp, dx))
    (norm_mix_all,) = _all_gather_weights("all_gather_norm_mix", [g_norm_mix], res["w_down"][0])
    (r_in, mlp_all, mix_all, scale_all), (s_in, _, _, _) = _exchange_wait("ex_in_wait", ex_in, norm_mix_all)
    update(["w_in"], [r_in], [s_in])

    natural = dict(norm_mix=(1, D_MODEL), pool_w=(MIX_POOL_B, LANES), pool_b=(4, LANES), pool_scale=(1, POOL_WIDTH),
                   attn_sinks=(1, LANES), norm_mlp=(1, D_MODEL), norm_final=(1, D_MODEL))

    def as_2d(t, name):
        if name == "attn_sinks":
            return jnp.pad(t, ((0, 0), (0, LANES - N_HEADS)))
        return t.reshape(natural[name])

    small_w = dict(norm_mix=norm_mix, pool_w=pool_w, pool_b=pool_b, pool_scale=pool_scale, attn_sinks=attn_sinks,
                   norm_mlp=norm_mlp, norm_final=norm_final)
    small_m = dict(norm_mix=m_norm_mix, pool_w=m_pool_w, pool_b=m_pool_b, pool_scale=m_pool_scale,
                   attn_sinks=m_attn_sinks, norm_mlp=m_norm_mlp, norm_final=m_norm_final)
    small_v = dict(norm_mix=v_norm_mix, pool_w=v_pool_w, pool_b=v_pool_b, pool_scale=v_pool_scale,
                   attn_sinks=v_attn_sinks, norm_mlp=v_norm_mlp, norm_final=v_norm_final)
    small_res, loss_all = _adamw_small(
        mlp_all, mix_all, scale_all, norm_mix_all,
        *[{k: as_2d(t, k) for k, t in d.items()} for d in (small_w, small_m, small_v)])
    loss = loss_all[0, 0]
    for name in SMALL_NAMES:
        shape = small_w[name].shape
        res[name] = [(t[:, :N_HEADS] if name == "attn_sinks" else t).reshape(shape) for t in small_res[name]]
    update(["p_pool", "p_attn", "w_out"], *_exchange_wait("ex_proj_wait", ex_proj, loss_all))

    order = ["norm_mix", "w_in", "pool_w", "pool_b", "pool_scale", "attn_sinks", "p_pool", "p_attn", "w_out",
             "norm_mlp", "w_up", "w_down", "norm_final"]
    out = [loss, dx[None]]
    for kind in range(4):
        out += [res[name][kind] for name in order]
    return tuple(out)
```

```python
import functools
import math

import numpy as np
import jax
import jax.numpy as jnp
from jax import lax
from jax.experimental import pallas as pl
from jax.experimental.pallas import tpu as pltpu

F32 = jnp.float32
BF16 = jnp.bfloat16

D_MODEL = 1024
POOL_WIDTH = 512
ATTN_WIDTH = 512
KV_WIDTH = 128
HEAD_DIM = 64
N_HEADS = 8
N_KV_HEADS = 2
GROUP = 4
BLOCK = 128
POOL_WINDOWS = (2, 4, 8, 16)
POOL_GROUP_DIM = 128
POOL_HALO = 16
D_FF = 4096
FF_CHUNK = 1024
IN_WIDTH = 3328
RMS_EPS = 1e-5
NEG_INF = -1e30
ATTN_SCALE = 1.0 / math.sqrt(HEAD_DIM)
N_DEV = 8

ADAM_LR = 0.001
ADAM_B1 = 0.9
ADAM_B2 = 0.999
ADAM_EPS = 1e-08
ADAM_WD = 0.01
ADAM_STEP = 10

LANES = 128
VMEM_LIMIT_BYTES = 56 * 1024 * 1024
MESH = pl.DeviceIdType.MESH


def _params(n_grid_axes=1):
    return pltpu.CompilerParams(
        dimension_semantics=("arbitrary",) * n_grid_axes, vmem_limit_bytes=VMEM_LIMIT_BYTES)


def _dot(a, b):
    return jnp.dot(a, b, preferred_element_type=F32)


def _dot_nt(a, b):
    return lax.dot_general(a, b, (((1,), (1,)), ((), ())), preferred_element_type=F32)


def _dot_tn(a, b):
    return lax.dot_general(a, b, (((0,), (0,)), ((), ())), preferred_element_type=F32)


ANY = pl.BlockSpec(memory_space=pl.ANY)


def _rows(tm, n):
    return pl.BlockSpec((tm, n), lambda i: (i, 0))


def _whole(shape):
    zeros = (0,) * len(shape)
    return pl.BlockSpec(shape, lambda i: zeros)


def _rms_fwd(h, g):
    r = lax.rsqrt(jnp.mean(h * h, axis=-1, keepdims=True) + RMS_EPS)
    xh = h * r
    return r, xh, xh * g


def _rms_bwd(dy, xh, r, g):
    dxh = dy * g
    dh = r * (dxh - xh * jnp.mean(dxh * xh, axis=-1, keepdims=True))
    return dh, jnp.sum(dy * xh, axis=0, keepdims=True)


def _fwd_in(after, x, g_mix, w_in_t):
    s_len = x.shape[0]
    tm = min(512, s_len)

    def body(after_ref, x_ref, g_ref, w_ref, u_ref, zp_ref, q_ref, kv_ref, zg_ref):
        _, _, u = _rms_fwd(x_ref[...], g_ref[...])
        u = u.astype(BF16)
        u_ref[...] = u
        zp_ref[...] = _dot_nt(u, w_ref[0:512, :]).astype(BF16)
        q_ref[...] = _dot_nt(u, w_ref[512:1024, :]).astype(BF16)
        kv_ref[...] = _dot_nt(u, w_ref[1024:1280, :]).astype(BF16)
        zg_ref[...] = _dot_nt(u, w_ref[1280:3328, :]).astype(BF16)

    return pl.pallas_call(
        body, name="fwd_in", grid=(s_len // tm,),
        in_specs=[ANY, _rows(tm, D_MODEL), _whole((1, D_MODEL)),
                  pl.BlockSpec((IN_WIDTH, D_MODEL), lambda i: (0, 0), pipeline_mode=pl.Buffered(1))],
        out_specs=[_rows(tm, D_MODEL), _rows(tm, 512), _rows(tm, 512), _rows(tm, 256), _rows(tm, 2048)],
        out_shape=[jax.ShapeDtypeStruct((s_len, n), BF16) for n in (D_MODEL, 512, 512, 256, 2048)],
        compiler_params=_params(),
    )(after, x, g_mix, w_in_t)


def _attn_constants(sinks):
    r = np.arange(BLOCK)[:, None]
    qi = np.arange(BLOCK)[None, :]
    dist = np.where(r <= qi, qi - r, BLOCK + qi - r).astype(np.float32)
    slopes = np.array([2.0 ** (-8.0 * (h + 1) / N_HEADS) for h in range(N_HEADS)], dtype=np.float32)
    bias = (-slopes[:, None, None] * dist[None]).reshape(N_KV_HEADS, GROUP, BLOCK, BLOCK)
    bias = np.ascontiguousarray(bias.transpose(0, 2, 1, 3)).reshape(N_KV_HEADS, BLOCK, GROUP * BLOCK)
    sink_row = jnp.repeat(sinks.astype(F32).reshape(N_KV_HEADS, GROUP), BLOCK, axis=1)[:, None, :]
    return jnp.asarray(bias.astype(np.float32)), sink_row


def _own_block_mask():
    shape = (BLOCK, GROUP * BLOCK)
    r = lax.broadcasted_iota(jnp.int32, shape, 0)
    qi = lax.broadcasted_iota(jnp.int32, shape, 1) & (BLOCK - 1)
    return r <= qi


def _pack_keys(t, own):
    return jnp.where(own, t[BLOCK:], t[:BLOCK])


def _unpack_keys(t, own):
    zero = jnp.zeros_like(t)
    return jnp.concatenate([jnp.where(own, zero, t), jnp.where(own, t, zero)], axis=0)


def _left_half(shape):
    return lax.broadcasted_iota(jnp.int32, shape, 1) < HEAD_DIM


def _dup_halves(slab):
    swapped = pltpu.roll(slab, HEAD_DIM, 1)
    left = _left_half(slab.shape)
    return jnp.where(left, slab, swapped), jnp.where(left, swapped, slab)


def _fill_kv_slabs(kvh_ref, kv_ref, ka_ref, vd_ref):
    for rows, src in ((slice(0, BLOCK), kvh_ref), (slice(BLOCK, None), kv_ref)):
        kvf = src[...].astype(F32)
        for ref, lanes in ((ka_ref, slice(0, KV_WIDTH)), (vd_ref, slice(KV_WIDTH, 2 * KV_WIDTH))):
            d0, d1 = _dup_halves(kvf[:, lanes])
            ref[0, rows, :] = d0.astype(BF16)
            ref[1, rows, :] = d1.astype(BF16)


def _stack_pairs(a, h):
    pieces = []
    for j in range(2):
        pair = a[:, h * 256 + j * LANES:h * 256 + (j + 1) * LANES]
        left = _left_half(pair.shape)
        zero = jnp.zeros_like(pair)
        pieces += [jnp.where(left, pair, zero), jnp.where(left, zero, pair)]
    return jnp.concatenate(pieces, axis=0)


def _attn_probs(kk, q_st, bias_p, sink_row, own, first):
    s = _pack_keys(_dot_nt(kk, q_st), own) * ATTN_SCALE + bias_p
    if first is not None:
        s = jnp.where(jnp.logical_and(first, jnp.logical_not(own)), NEG_INF, s)
    m = jnp.maximum(jnp.max(s, axis=0, keepdims=True), sink_row)
    p = jnp.exp(s - m)
    es = jnp.exp(sink_row - m)
    inv = 1.0 / (jnp.sum(p, axis=0, keepdims=True) + es)
    return p * inv, es * inv


def _pool_d(ext, cur, g, row0):
    w = POOL_WINDOWS[g]
    acc = ext
    k = 1
    while k < w:
        acc = acc + pltpu.roll(acc, k, 0)
        k *= 2
    return _window_mean(acc[POOL_HALO:, :], w, row0) - cur


def _window_mean(total, w, row0):
    t = row0 + lax.broadcasted_iota(jnp.int32, (POOL_HALO, total.shape[1]), 0)
    head = total[:POOL_HALO] / jnp.minimum(t + 1, w).astype(F32)
    return jnp.concatenate([head, total[POOL_HALO:] * (1.0 / w)], axis=0)


def _mixers_fwd(zp, q, kv, pool_w, pool_b, pool_scale, bias_t, sink_row, first_tile=0, n_tiles=None, earlier=None):
    s_len = zp.shape[0]
    tq = min(512, s_len)
    nb = tq // BLOCK
    n_tiles = s_len // tq - first_tile if n_tiles is None else n_tiles
    extra = [] if earlier is None else list(earlier)

    def body(zp_ref, zph_ref, q_ref, kv_ref, kvh_ref, pw_ref, pb_ref, ps_ref, bias_ref, sink_ref, *rest):
        pm_ref, o_ref, ka_ref, vd_ref = rest[len(extra):]
        i = pl.program_id(0) + first_tile
        cur = zp_ref[...].astype(F32)
        halo = zph_ref[...].astype(F32) * (i > 0).astype(F32)
        ext = jnp.concatenate([halo, cur], axis=0)
        for g in range(4):
            sl = slice(g * POOL_GROUP_DIM, (g + 1) * POOL_GROUP_DIM)
            d = _pool_d(ext[:, sl], cur[:, sl], g, i * tq)
            y = _dot(d.astype(BF16), pw_ref[g]) + pb_ref[:, sl]
            pm_ref[:, sl] = (y * ps_ref[:, sl]).astype(BF16)
        _fill_kv_slabs(kvh_ref, kv_ref, ka_ref, vd_ref)
        own = _own_block_mask()
        for b in range(nb):
            rq = slice(b * BLOCK, (b + 1) * BLOCK)
            rk = slice(b * BLOCK, (b + 2) * BLOCK)
            qb = q_ref[rq, :]
            for h in range(N_KV_HEADS):
                pn, _ = _attn_probs(ka_ref[h, rk, :], _stack_pairs(qb, h), bias_ref[h], sink_ref[h], own,
                                    (i == 0) if b == 0 else None)
                pn = _unpack_keys(pn, own).astype(BF16)
                vd = vd_ref[h, rk, :]
                left = _left_half(vd.shape)
                zero = jnp.zeros_like(vd)
                va, vb = jnp.where(left, vd, zero), jnp.where(left, zero, vd)
                for j in range(2):
                    o_pair = (_dot_tn(pn[:, (2 * j) * BLOCK:(2 * j + 1) * BLOCK], va)
                              + _dot_tn(pn[:, (2 * j + 1) * BLOCK:(2 * j + 2) * BLOCK], vb))
                    o_ref[rq, h * 256 + j * LANES:h * 256 + (j + 1) * LANES] = o_pair.astype(BF16)

    def tile(n):
        return pl.BlockSpec((tq, n), lambda i: (i + first_tile, 0))

    halo_pool = pl.BlockSpec(
        (POOL_HALO, 512), lambda i: (jnp.maximum((i + first_tile) * (tq // POOL_HALO) - 1, 0), 0))
    halo_kv = pl.BlockSpec((BLOCK, 256), lambda i: (jnp.maximum((i + first_tile) * nb - 1, 0), 0))
    return pl.pallas_call(
        body, name="mixers_fwd", grid=(n_tiles,),
        in_specs=[tile(512), halo_pool, tile(512), tile(256), halo_kv,
                  _whole((4, 128, 128)), _whole((1, 512)), _whole((1, 512)),
                  _whole((N_KV_HEADS, BLOCK, GROUP * BLOCK)), _whole((N_KV_HEADS, 1, GROUP * BLOCK))]
        + [ANY] * len(extra),
        out_specs=[tile(512), tile(512)],
        out_shape=[jax.ShapeDtypeStruct((s_len, 512), BF16)] * 2,
        input_output_aliases={11: 0, 12: 1} if extra else {},
        scratch_shapes=[pltpu.VMEM((N_KV_HEADS, tq + BLOCK, LANES), BF16)] * 2,
        compiler_params=_params(),
    )(zp, zp, q, kv, kv, pool_w, pool_b, pool_scale, bias_t, sink_row, *extra)


def _gated_mix(pm, o, zg, pp_ref, pa_ref):
    yp = _dot(pm, pp_ref[...])
    ya = _dot(o, pa_ref[...])
    gp = jax.nn.sigmoid(zg[:, :D_MODEL].astype(F32))
    ga = jax.nn.sigmoid(zg[:, D_MODEL:].astype(F32))
    return yp, ya, gp, ga


def _core(x, pm, o, zg, tgt, g_mlp, g_fin, p_pool, p_attn, w_out, w_up_blocks, w_down):
    s_len = x.shape[0]
    tm = min(256, s_len)
    n_chunks = D_FF // FF_CHUNK
    up_block = D_FF // N_DEV
    per_chunk = FF_CHUNK // up_block

    def body(x_ref, pm_ref, o_ref, zg_ref, tgt_ref, gm_ref, gf_ref, pp_ref, pa_ref, wo_ref, wu_ref, wd_ref,
             mixed_ref, dh1_ref, a_ref, dap_ref, u2_ref, dh2_ref, small_ref,
             dyp_ref, dya_ref, dzg_ref, dpm_ref, do_ref, dh1b_ref):
        i = pl.program_id(0)

        @pl.when(i == 0)
        def _():
            small_ref[...] = jnp.zeros_like(small_ref)

        yp, ya, gp, ga = _gated_mix(pm_ref[...], o_ref[...], zg_ref[...], pp_ref, pa_ref)
        mixed = (gp * yp + ga * ya).astype(BF16)
        mixed_ref[...] = mixed
        h1 = x_ref[...] + _dot(mixed, wo_ref[...])
        r2, xh2, u2 = _rms_fwd(h1, gm_ref[...])
        u2 = u2.astype(BF16)
        u2_ref[...] = u2
        acc = jnp.zeros((tm, D_MODEL), F32)
        for c in range(n_chunks):
            cs = slice(c * FF_CHUNK, (c + 1) * FF_CHUNK)
            a = jnp.concatenate([_dot(u2, wu_ref[per_chunk * c + j]) for j in range(per_chunk)], axis=1)
            a = jnp.maximum(a, 0.0)
            a_ref[:, cs] = a.astype(BF16)
            acc = acc + _dot((a * a).astype(BF16), wd_ref[cs, :])
        h2 = h1 + acc
        r3, xh3, y = _rms_fwd(h2, gf_ref[...])
        diff = y - tgt_ref[...]
        small_ref[2:3, :] += 0.5 * jnp.sum(jnp.mean(diff * diff, axis=-1, keepdims=True))
        dy = diff * (1.0 / D_MODEL)
        dh2, dgf = _rms_bwd(dy, xh3, r3, gf_ref[...])
        small_ref[1:2, :] += dgf
        dh2_bf = dh2.astype(BF16)
        dh2_ref[...] = dh2_bf
        du2 = jnp.zeros((tm, D_MODEL), F32)
        for c in range(n_chunks):
            cs = slice(c * FF_CHUNK, (c + 1) * FF_CHUNK)
            ds = _dot_nt(dh2_bf, wd_ref[cs, :])
            dap = (ds * (2.0 * a_ref[:, cs].astype(F32))).astype(BF16)
            dap_ref[:, cs] = dap
            for j in range(per_chunk):
                du2 = du2 + _dot_nt(dap[:, j * up_block:(j + 1) * up_block], wu_ref[per_chunk * c + j])
        dh1n, dgm = _rms_bwd(du2, xh2, r2, gm_ref[...])
        small_ref[0:1, :] += dgm
        dh1 = dh2 + dh1n
        dh1_ref[...] = dh1
        dh1_bf = dh1.astype(BF16)
        dh1b_ref[...] = dh1_bf
        dm = _dot_nt(dh1_bf, wo_ref[...])
        dyp = (dm * gp).astype(BF16)
        dya = (dm * ga).astype(BF16)
        dyp_ref[...] = dyp
        dya_ref[...] = dya
        dzg_ref[:, :D_MODEL] = (dm * yp * (gp * (1.0 - gp))).astype(BF16)
        dzg_ref[:, D_MODEL:] = (dm * ya * (ga * (1.0 - ga))).astype(BF16)
        dpm_ref[...] = _dot_nt(dyp, pp_ref[...]).astype(BF16)
        do_ref[...] = _dot_nt(dya, pa_ref[...]).astype(BF16)

    def fixed(shape):
        return pl.BlockSpec(shape, lambda i: (0,) * len(shape), pipeline_mode=pl.Buffered(1))

    widths_dtypes = ((D_MODEL, BF16), (D_MODEL, F32), (D_FF, BF16), (D_FF, BF16), (D_MODEL, BF16), (D_MODEL, BF16))
    back = ((D_MODEL, BF16), (D_MODEL, BF16), (2048, BF16), (512, BF16), (512, BF16), (D_MODEL, BF16))
    return pl.pallas_call(
        body, name="core", grid=(s_len // tm,),
        in_specs=[_rows(tm, D_MODEL), _rows(tm, 512), _rows(tm, 512), _rows(tm, 2048), _rows(tm, D_MODEL),
                  _whole((1, D_MODEL)), _whole((1, D_MODEL)),
                  fixed((512, D_MODEL)), fixed((512, D_MODEL)), fixed((D_MODEL, D_MODEL)),
                  fixed((N_DEV, D_MODEL, up_block)), fixed((D_FF, D_MODEL))],
        out_specs=[_rows(tm, n) for n, _ in widths_dtypes] + [_whole((8, D_MODEL))] + [_rows(tm, n) for n, _ in back],
        out_shape=[jax.ShapeDtypeStruct((s_len, n), d) for n, d in widths_dtypes]
        + [jax.ShapeDtypeStruct((8, D_MODEL), F32)] + [jax.ShapeDtypeStruct((s_len, n), d) for n, d in back],
        compiler_params=_params(),
    )(x, pm, o, zg, tgt, g_mlp, g_fin, p_pool, p_attn, w_out, w_up_blocks, w_down)


def _tn_matmul(a, b, square_a=False, col_blocks=None, after=None):
    s_len, ka = a.shape
    nb = b.shape[1]
    tt = min(2048, s_len)
    tk = min(1024, ka)
    tn = min(1024, nb)
    n_t = s_len // tt
    if col_blocks is None:
        out_spec = pl.BlockSpec((tk, tn), lambda k, j, t: (k, j))
        out_shape = jax.ShapeDtypeStruct((ka, nb), BF16)
    else:
        width = nb // col_blocks
        per_tile = tn // width
        out_spec = pl.BlockSpec((per_tile, tk, width), lambda k, j, t: (j, k, 0))
        out_shape = jax.ShapeDtypeStruct((col_blocks, ka, width), BF16)

    extra = [] if after is None else [after]

    def body(a_ref, b_ref, *rest):
        o_ref, acc_ref = rest[len(extra):]
        t = pl.program_id(2)

        @pl.when(t == 0)
        def _():
            acc_ref[...] = jnp.zeros_like(acc_ref)

        av = a_ref[...]
        if square_a:
            av = av * av
        acc_ref[...] += _dot_tn(av.astype(BF16), b_ref[...].astype(BF16))

        @pl.when(t == n_t - 1)
        def _():
            if col_blocks is None:
                o_ref[...] = acc_ref[...].astype(o_ref.dtype)
            else:
                for blk in range(per_tile):
                    o_ref[blk] = acc_ref[:, blk * width:(blk + 1) * width].astype(o_ref.dtype)

    return pl.pallas_call(
        body, name="tn_matmul", grid=(ka // tk, nb // tn, n_t),
        in_specs=[pl.BlockSpec((tt, tk), lambda k, j, t: (t, k)), pl.BlockSpec((tt, tn), lambda k, j, t: (t, j))]
        + [ANY] * len(extra),
        out_specs=out_spec, out_shape=out_shape,
        scratch_shapes=[pltpu.VMEM((tk, tn), F32)],
        compiler_params=_params(3),
    )(a, b, *extra)


def _tn_w_in(u, dzp, dq, dkv, dzg):
    s_len = u.shape[0]
    tt = min(1024, s_len)
    n_t = s_len // tt
    width = IN_WIDTH // N_DEV
    pieces = ((0, 512), (512, 1024), (1024, 1280), (1280, IN_WIDTH))

    def body(u_ref, dzp_ref, dq_ref, dkv_ref, dzg_ref, o_ref, acc_ref):
        t = pl.program_id(0)

        @pl.when(t == 0)
        def _():
            acc_ref[...] = jnp.zeros_like(acc_ref)

        uv = u_ref[...]
        for (c0, c1), ref in zip(pieces, (dzp_ref, dq_ref, dkv_ref, dzg_ref)):
            acc_ref[c0:c1, :] += _dot_tn(ref[...], uv)

        @pl.when(t == n_t - 1)
        def _():
            for j in range(N_DEV):
                o_ref[j] = acc_ref[j * width:(j + 1) * width, :].astype(BF16)

    return pl.pallas_call(
        body, name="tn_w_in", grid=(n_t,),
        in_specs=[_rows(tt, D_MODEL)] + [_rows(tt, c1 - c0) for c0, c1 in pieces],
        out_specs=_whole((N_DEV, width, D_MODEL)),
        out_shape=jax.ShapeDtypeStruct((N_DEV, width, D_MODEL), BF16),
        scratch_shapes=[pltpu.VMEM((IN_WIDTH, D_MODEL), F32)],
        compiler_params=_params(),
    )(u, dzp, dq, dkv, dzg)


MIX_POOL_B = 4 * POOL_GROUP_DIM
MIX_SINKS = MIX_POOL_B + 8
MIX_ROWS = MIX_SINKS + 8


def _mixers_bwd(after, zp, q, kv, dpm, do, pool_w, pool_b, pool_scale, bias_t, sink_row):
    s_len = zp.shape[0]
    tq = min(512, s_len)
    nb = tq // BLOCK
    n_steps = s_len // tq

    def body(after_ref, zp_ref, zph_ref, q_ref, kv_ref, kvh_ref, dpm_ref, dpmh_ref, do_ref, pw_ref, pb_ref, ps_ref,
             bias_ref, sink_ref, dzp_ref, dq_ref, dkv_ref, small_ref, dps_ref,
             ka_ref, vd_ref, dsk_acc, dkv_acc):
        i = pl.program_id(0)

        @pl.when(i == 0)
        def _():
            dkv_acc[...] = jnp.zeros_like(dkv_acc)
            small_ref[...] = jnp.zeros_like(small_ref)
            dps_ref[...] = jnp.zeros_like(dps_ref)
            dsk_acc[...] = jnp.zeros_like(dsk_acc)

        cur = zp_ref[...].astype(F32)
        halo = zph_ref[...].astype(F32) * (i > 0).astype(F32)
        ext = jnp.concatenate([halo, cur], axis=0)
        dpm_next = dpmh_ref[...].astype(F32) * (i < n_steps - 1).astype(F32)
        dpm_ext = jnp.concatenate([dpm_ref[...].astype(F32), dpm_next], axis=0)
        n_ext = tq + POOL_HALO
        for g in range(4):
            sl = slice(g * POOL_GROUP_DIM, (g + 1) * POOL_GROUP_DIM)
            w = POOL_WINDOWS[g]
            d = _pool_d(ext[:, sl], cur[:, sl], g, i * tq).astype(BF16)
            y_lin = _dot(d, pw_ref[g]) + pb_ref[:, sl]
            dps_ref[:, sl] += jnp.sum(dpm_ext[:tq, sl] * y_lin, axis=0, keepdims=True)
            dyl_ext = dpm_ext[:, sl] * ps_ref[:, sl]
            small_ref[MIX_POOL_B + g:MIX_POOL_B + g + 1, :] += jnp.sum(dyl_ext[:tq], axis=0, keepdims=True)
            dyl_bf = dyl_ext.astype(BF16)
            small_ref[g * POOL_GROUP_DIM:(g + 1) * POOL_GROUP_DIM, :] += _dot_tn(d, dyl_bf[:tq])
            dd = _dot_nt(dyl_bf, pw_ref[g])
            e = _window_mean(dd, w, i * tq)
            acc = e
            k = 1
            while k < w:
                acc = acc + pltpu.roll(acc, n_ext - k, 0)
                k *= 2
            dzp_ref[:, sl] = (acc[:tq] - dd[:tq]).astype(BF16)

        _fill_kv_slabs(kvh_ref, kv_ref, ka_ref, vd_ref)

        def fold(dup):
            return dup + pltpu.roll(dup, HEAD_DIM, 1)

        own = _own_block_mask()
        for b in range(nb):
            rq = slice(b * BLOCK, (b + 1) * BLOCK)
            rk = slice(b * BLOCK, (b + 2) * BLOCK)
            qb = q_ref[rq, :]
            dob = do_ref[rq, :]
            dk_dup, dv_dup = [], []
            for h in range(N_KV_HEADS):
                kk = ka_ref[h, rk, :]
                q_st = _stack_pairs(qb, h)
                do_st = _stack_pairs(dob, h)
                pn, psink = _attn_probs(kk, q_st, bias_ref[h], sink_ref[h], own, (i == 0) if b == 0 else None)
                dp = _pack_keys(_dot_nt(vd_ref[h, rk, :], do_st), own)
                delta = jnp.sum(pn * dp, axis=0, keepdims=True)
                dsk_acc[h] += -psink * delta
                ds = _unpack_keys((pn * (dp - delta)) * ATTN_SCALE, own).astype(BF16)
                pn = _unpack_keys(pn, own)
                dq_st = _dot_tn(ds, kk)
                for j in range(2):
                    left = _left_half((BLOCK, LANES))
                    dq_pair = jnp.where(left, dq_st[(2 * j) * BLOCK:(2 * j + 1) * BLOCK],
                                        dq_st[(2 * j + 1) * BLOCK:(2 * j + 2) * BLOCK])
                    dq_ref[rq, h * 256 + j * LANES:h * 256 + (j + 1) * LANES] = dq_pair.astype(BF16)
                dk_dup.append(fold(_dot(ds, q_st)))
                dv_dup.append(fold(_dot(pn.astype(BF16), do_st)))
            left = _left_half((2 * BLOCK, LANES))
            dkv_blk = jnp.concatenate([jnp.where(left, dk_dup[0], dk_dup[1]),
                                       jnp.where(left, dv_dup[0], dv_dup[1])], axis=1)
            g0 = pl.multiple_of(i * tq + b * BLOCK, BLOCK)
            dkv_acc[pl.ds(g0, 2 * BLOCK), :] += dkv_blk

        @pl.when(i == n_steps - 1)
        def _():
            dkv_ref[...] = dkv_acc[BLOCK:, :].astype(BF16)
            lane = lax.broadcasted_iota(jnp.int32, (1, LANES), 1)
            row = jnp.zeros((1, LANES), F32)
            for h in range(N_KV_HEADS):
                for g in range(GROUP):
                    tot = jnp.sum(dsk_acc[h, :, g * BLOCK:(g + 1) * BLOCK], axis=1, keepdims=True)
                    row = jnp.where(lane == GROUP * h + g, tot, row)
            small_ref[MIX_SINKS:MIX_SINKS + 1, :] = row

    blocks_per_tile = tq // POOL_HALO
    last_halo = s_len // POOL_HALO - 1
    halo_prev = pl.BlockSpec((POOL_HALO, 512), lambda i: (jnp.maximum(i * blocks_per_tile - 1, 0), 0))
    halo_next = pl.BlockSpec((POOL_HALO, 512), lambda i: (jnp.minimum((i + 1) * blocks_per_tile, last_halo), 0))
    halo_kv = pl.BlockSpec((BLOCK, 256), lambda i: (jnp.maximum(i * nb - 1, 0), 0))
    return pl.pallas_call(
        body, name="mixers_bwd", grid=(n_steps,),
        in_specs=[ANY, _rows(tq, 512), halo_prev, _rows(tq, 512), _rows(tq, 256), halo_kv,
                  _rows(tq, 512), halo_next, _rows(tq, 512),
                  _whole((4, 128, 128)), _whole((1, 512)), _whole((1, 512)),
                  _whole((N_KV_HEADS, BLOCK, GROUP * BLOCK)), _whole((N_KV_HEADS, 1, GROUP * BLOCK))],
        out_specs=[_rows(tq, 512), _rows(tq, 512), _whole((s_len, 256)),
                   _whole((MIX_ROWS, LANES)), _whole((1, 512))],
        out_shape=[jax.ShapeDtypeStruct((s_len, 512), BF16), jax.ShapeDtypeStruct((s_len, 512), BF16),
                   jax.ShapeDtypeStruct((s_len, 256), BF16), jax.ShapeDtypeStruct((MIX_ROWS, LANES), F32),
                   jax.ShapeDtypeStruct((1, 512), F32)],
        scratch_shapes=[pltpu.VMEM((N_KV_HEADS, tq + BLOCK, LANES), BF16)] * 2
        + [pltpu.VMEM((N_KV_HEADS, 1, GROUP * BLOCK), F32), pltpu.VMEM((s_len + BLOCK, 256), F32)],
        compiler_params=_params(),
    )(after, zp, zp, q, kv, kv, dpm, dpm, do, pool_w, pool_b, pool_scale, bias_t, sink_row)


def _in_bwd(after, dzp, dq, dkv, dzg, w_in_t, x, dh1, g_mix):
    s_len = x.shape[0]
    tm = min(512, s_len)

    def body(after_ref, dzp_ref, dq_ref, dkv_ref, dzg_ref, w_ref, x_ref, dh1_ref, g_ref, dx_ref, dg_ref):
        i = pl.program_id(0)

        @pl.when(i == 0)
        def _():
            dg_ref[...] = jnp.zeros_like(dg_ref)

        du = _dot(dzp_ref[...], w_ref[0:512, :])
        du = du + _dot(dq_ref[...], w_ref[512:1024, :])
        du = du + _dot(dkv_ref[...], w_ref[1024:1280, :])
        du = du + _dot(dzg_ref[...], w_ref[1280:3328, :])
        r, xh, _ = _rms_fwd(x_ref[...], g_ref[...])
        dxn, dg = _rms_bwd(du, xh, r, g_ref[...])
        dg_ref[...] += dg
        dx_ref[...] = dh1_ref[...] + dxn

    return pl.pallas_call(
        body, name="in_bwd", grid=(s_len // tm,),
        in_specs=[ANY, _rows(tm, 512), _rows(tm, 512), _rows(tm, 256), _rows(tm, 2048), _whole((IN_WIDTH, D_MODEL)),
                  _rows(tm, D_MODEL), _rows(tm, D_MODEL), _whole((1, D_MODEL))],
        out_specs=[_rows(tm, D_MODEL), _whole((1, D_MODEL))],
        out_shape=[jax.ShapeDtypeStruct((s_len, D_MODEL), F32), jax.ShapeDtypeStruct((1, D_MODEL), F32)],
        compiler_params=_params(),
    )(after, dzp, dq, dkv, dzg, w_in_t, x, dh1, g_mix)


def _all_gather_weights(name, shards, after=None):
    n = len(shards)
    extra = [] if after is None else [after]
    n_extra = len(extra)

    def body(*refs):
        ins, outs = refs[:n], refs[n + n_extra:2 * n + n_extra]
        send_sems, recv_sems, local_sems = refs[2 * n + n_extra:]
        x, y, c = lax.axis_index("x"), lax.axis_index("y"), lax.axis_index("c")
        me, sibling = (x, y, c), (x, y, 1 - c)
        chips = [(1 - x, y), (x, 1 - y), (1 - x, 1 - y)]

        def slot(a, px, py, pc):
            return outs[a].at[4 * px + 2 * py + pc]

        def copy(a, k, block, to, src=None):
            return pltpu.make_async_remote_copy(
                src_ref=slot(a, *block) if src is None else src, dst_ref=slot(a, *block),
                send_sem=send_sems.at[a, k], recv_sem=recv_sems.at[a, k], device_id=to, device_id_type=MESH)

        mine = [pltpu.make_async_copy(ins[a], slot(a, *me), local_sems.at[a]) for a in range(n)]
        for cp in mine:
            cp.start()
        first = []
        for a in range(n):
            first.append(copy(a, 0, me, sibling, src=ins[a]))
            first += [copy(a, 1 + j, me, (*chip, c), src=ins[a]) for j, chip in enumerate(chips)]
        for cp in first:
            cp.start()
        passed = []
        for a in range(n):
            for j, chip in enumerate(chips):
                copy(a, 1 + j, (*chip, c), me).wait_recv()
                cp = copy(a, 4 + j, (*chip, c), sibling)
                cp.start()
                passed.append(cp)
        for a in range(n):
            copy(a, 0, sibling, me).wait_recv()
            for j, chip in enumerate(chips):
                copy(a, 4 + j, (*chip, 1 - c), me).wait_recv()
        for cp in first + passed:
            cp.wait_send()
        for cp in mine:
            cp.wait()

    return pl.pallas_call(
        body, name=name,
        in_specs=[ANY] * (n + n_extra), out_specs=[ANY] * n,
        out_shape=[jax.ShapeDtypeStruct((N_DEV,) + s.shape, s.dtype) for s in shards],
        scratch_shapes=[pltpu.SemaphoreType.DMA((n, 7)), pltpu.SemaphoreType.DMA((n, 7)), pltpu.SemaphoreType.DMA((n,))],
    )(*shards, *extra)


HBM_SPEC = pl.BlockSpec(memory_space=pltpu.HBM)
SEM_SPEC = pl.BlockSpec(memory_space=pltpu.SEMAPHORE)
DATAFLOW = pltpu.SideEffectType.DATAFLOW_SIDE_EFFECTING
N_PEERS = N_DEV - 1


CHIP_PEERS = (1, 2, 4, 6)
RELAYED = (2, 4, 6)


def _peer_copies(srcs, lands, scatter, send_sems, recv_sems):
    x, y, c = lax.axis_index("x"), lax.axis_index("y"), lax.axis_index("c")
    me_idx = 4 * x + 2 * y + c
    copies = []
    for k in range(1, N_DEV):
        px = 1 - x if (k >> 2) & 1 else x
        py = 1 - y if (k >> 1) & 1 else y
        pc = 1 - c if k & 1 else c
        p_idx = 4 * px + 2 * py + pc
        for a in range(len(srcs)):
            if scatter[a] == "chip" and k not in CHIP_PEERS:
                continue
            src = srcs[a].at[p_idx] if scatter[a] is True else srcs[a]
            dst = lands[a].at[k] if scatter[a] is True else lands[a].at[me_idx]
            copies.append(pltpu.make_async_remote_copy(
                src_ref=src, dst_ref=dst, send_sem=send_sems.at[a * N_PEERS + k - 1],
                recv_sem=recv_sems.at[a * N_PEERS + k - 1],
                device_id=(px, py, pc), device_id_type=MESH))
    return copies


def _exchange_start(name, srcs, scatter, after):
    n = len(srcs)
    lands = [lax.empty(s.shape if sc is True else (N_DEV,) + s.shape, s.dtype) for s, sc in zip(srcs, scatter)]

    def body(*refs):
        src_refs, land_refs = refs[:n], refs[n:2 * n]
        send_sems, recv_sems = refs[2 * n + 1], refs[2 * n + 2]
        token = refs[4 * n + 3]
        for cp in _peer_copies(src_refs, land_refs, scatter, send_sems, recv_sems):
            cp.start()
        token[...] = jnp.zeros_like(token)

    hbm = lambda t: pltpu.HBM(t.shape, t.dtype)
    outs = pl.pallas_call(
        body, name=name,
        out_shape=[pltpu.SemaphoreType.DMA((n * N_PEERS,)), pltpu.SemaphoreType.DMA((n * N_PEERS,))]
        + [hbm(t) for t in srcs] + [hbm(t) for t in lands] + [jax.ShapeDtypeStruct((8, LANES), F32)],
        in_specs=[HBM_SPEC] * (2 * n) + [ANY],
        out_specs=[SEM_SPEC, SEM_SPEC] + [HBM_SPEC] * (2 * n) + [pl.BlockSpec(memory_space=pltpu.VMEM)],
        input_output_aliases={i: 2 + i for i in range(2 * n)},
        compiler_params=pltpu.CompilerParams(has_side_effects=DATAFLOW),
    )(*[pltpu.with_memory_space_constraint(t, pltpu.HBM) for t in list(srcs) + lands], after)
    return dict(n=n, scatter=scatter, send_sems=outs[0], recv_sems=outs[1], srcs=outs[2:2 + n],
                lands=outs[2 + n:2 + 2 * n], token=outs[2 + 2 * n])


def _exchange_wait(name, handle, after):
    n, scatter = handle["n"], handle["scatter"]

    def body(*refs):
        src_refs, land_refs = refs[:n], refs[n:2 * n]
        send_sems, recv_sems = refs[2 * n], refs[2 * n + 1]
        for cp in _peer_copies(src_refs, land_refs, scatter, send_sems, recv_sems):
            cp.wait_send()
            cp.wait_recv()

    both = list(handle["srcs"]) + list(handle["lands"])
    outs = pl.pallas_call(
        body, name=name,
        out_shape=[pltpu.HBM(t.shape, t.dtype) for t in both],
        in_specs=[HBM_SPEC] * (2 * n) + [SEM_SPEC, SEM_SPEC, ANY],
        out_specs=[HBM_SPEC] * (2 * n),
        input_output_aliases={i: i for i in range(2 * n)},
        compiler_params=pltpu.CompilerParams(has_side_effects=DATAFLOW),
    )(*both, handle["send_sems"], handle["recv_sems"], after)
    me_idx = _my_index()
    lands = [land if sc is True else lax.dynamic_update_index_in_dim(land, src, me_idx, 0)
             for land, src, sc in zip(outs[n:], outs[:n], scatter)]
    return lands, outs[:n]


def _my_index():
    return 4 * lax.axis_index("x") + 2 * lax.axis_index("y") + lax.axis_index("c")


def _relay_copies(bufs, send_sems, recv_sems):
    x, y, c = lax.axis_index("x"), lax.axis_index("y"), lax.axis_index("c")
    copies = []
    for j, k in enumerate(RELAYED):
        px = 1 - x if (k >> 2) & 1 else x
        py = 1 - y if (k >> 1) & 1 else y
        slot = 4 * px + 2 * py + c
        for a, buf in enumerate(bufs):
            copies.append(pltpu.make_async_remote_copy(
                src_ref=buf.at[slot], dst_ref=buf.at[slot], send_sem=send_sems.at[a * len(RELAYED) + j],
                recv_sem=recv_sems.at[a * len(RELAYED) + j], device_id=(x, y, 1 - c), device_id_type=MESH))
    return copies


def _relay_start(name, bufs, after):
    n = len(bufs)

    def body(*refs):
        send_sems, recv_sems = refs[n + 1], refs[n + 2]
        for cp in _relay_copies(refs[:n], send_sems, recv_sems):
            cp.start()
        token = refs[2 * n + 3]
        token[...] = jnp.zeros_like(token)

    n_sems = n * len(RELAYED)
    outs = pl.pallas_call(
        body, name=name,
        out_shape=[pltpu.SemaphoreType.DMA((n_sems,)), pltpu.SemaphoreType.DMA((n_sems,))]
        + [pltpu.HBM(t.shape, t.dtype) for t in bufs] + [jax.ShapeDtypeStruct((8, LANES), F32)],
        in_specs=[HBM_SPEC] * n + [ANY],
        out_specs=[SEM_SPEC, SEM_SPEC] + [HBM_SPEC] * n + [pl.BlockSpec(memory_space=pltpu.VMEM)],
        input_output_aliases={i: 2 + i for i in range(n)},
        compiler_params=pltpu.CompilerParams(has_side_effects=DATAFLOW),
    )(*[pltpu.with_memory_space_constraint(t, pltpu.HBM) for t in bufs], after)
    return dict(n=n, send_sems=outs[0], recv_sems=outs[1], bufs=outs[2:2 + n], token=outs[2 + n])


def _relay_wait(name, handle, after):
    n = handle["n"]

    def body(*refs):
        for cp in _relay_copies(refs[:n], refs[n], refs[n + 1]):
            cp.wait_send()
            cp.wait_recv()

    return pl.pallas_call(
        body, name=name,
        out_shape=[pltpu.HBM(t.shape, t.dtype) for t in handle["bufs"]],
        in_specs=[HBM_SPEC] * n + [SEM_SPEC, SEM_SPEC, ANY],
        out_specs=[HBM_SPEC] * n,
        input_output_aliases={i: i for i in range(n)},
        compiler_params=pltpu.CompilerParams(has_side_effects=DATAFLOW),
    )(*handle["bufs"], handle["send_sems"], handle["recv_sems"], after)


def _adamw(parts, w, m, v, sent=None):
    r, c = w.shape
    tr = 256 if r % 256 == 0 else r
    own = sent is not None

    def body(*refs):
        if own:
            _, p_ref, own_ref, w_ref, m_ref, v_ref, g_ref, d_ref, nm_ref, nv_ref = refs
            g = own_ref[...].astype(F32)
        else:
            p_ref, w_ref, m_ref, v_ref, g_ref, d_ref, nm_ref, nv_ref = refs
            g = p_ref[0].astype(F32)
        for k in range(1, N_DEV):
            g = g + p_ref[k].astype(F32)
        m_new = ADAM_B1 * m_ref[...] + (1.0 - ADAM_B1) * g
        v_new = ADAM_B2 * v_ref[...] + (1.0 - ADAM_B2) * (g * g)
        m_hat = m_new / (1.0 - ADAM_B1 ** ADAM_STEP)
        v_hat = v_new / (1.0 - ADAM_B2 ** ADAM_STEP)
        g_ref[...] = g
        d_ref[...] = -ADAM_LR * (m_hat / (jnp.sqrt(v_hat) + ADAM_EPS) + ADAM_WD * w_ref[...])
        nm_ref[...] = m_new
        nv_ref[...] = v_new

    out_shape = [jax.ShapeDtypeStruct((r, c), F32)] * 4
    if not own:
        return pl.pallas_call(
            body, name="adamw", grid=(r // tr,),
            in_specs=[pl.BlockSpec((N_DEV, tr, c), lambda i: (0, i, 0))] + [_rows(tr, c)] * 3,
            out_specs=[_rows(tr, c)] * 4, out_shape=out_shape, compiler_params=_params(),
        )(parts, w, m, v)
    rows = pl.BlockSpec((tr, c), lambda i, me: (i, 0))
    return pl.pallas_call(
        body, name="adamw_own", out_shape=out_shape, compiler_params=_params(),
        grid_spec=pltpu.PrefetchScalarGridSpec(
            num_scalar_prefetch=1, grid=(r // tr,),
            in_specs=[pl.BlockSpec((N_DEV, tr, c), lambda i, me: (0, i, 0)),
                      pl.BlockSpec((None, tr, c), lambda i, me: (me[0], i, 0))] + [rows] * 3,
            out_specs=[rows] * 4),
    )(_my_index().reshape(1).astype(jnp.int32), parts, sent, w, m, v)


def _adam_step(g, w, m, v):
    m_new = ADAM_B1 * m + (1.0 - ADAM_B1) * g
    v_new = ADAM_B2 * v + (1.0 - ADAM_B2) * (g * g)
    m_hat = m_new / (1.0 - ADAM_B1 ** ADAM_STEP)
    v_hat = v_new / (1.0 - ADAM_B2 ** ADAM_STEP)
    return -ADAM_LR * (m_hat / (jnp.sqrt(v_hat) + ADAM_EPS) + ADAM_WD * w), m_new, v_new


SMALL_NAMES = ("norm_mix", "pool_w", "pool_b", "pool_scale", "attn_sinks", "norm_mlp", "norm_final")


def _adamw_small(mlp_all, mix_all, scale_all, nmix_all, w, m, v):
    def body(mlp_ref, mix_ref, scale_ref, nmix_ref, *refs):
        ins, outs = refs[:21], refs[21:]

        def total(ref, rows, lanes=slice(None)):
            g = ref[0, rows, lanes]
            for k in range(1, N_DEV):
                g = g + ref[k, rows, lanes]
            return g

        grads = dict(
            norm_mix=total(nmix_ref, slice(0, 1)), pool_w=total(mix_ref, slice(0, MIX_POOL_B)),
            pool_b=total(mix_ref, slice(MIX_POOL_B, MIX_POOL_B + 4)), pool_scale=total(scale_ref, slice(0, 1)),
            attn_sinks=total(mix_ref, slice(MIX_SINKS, MIX_SINKS + 1)),
            norm_mlp=total(mlp_ref, slice(0, 1)), norm_final=total(mlp_ref, slice(1, 2)))
        for i, name in enumerate(SMALL_NAMES):
            g = grads[name]
            d, m_new, v_new = _adam_step(g, ins[3 * i][...], ins[3 * i + 1][...], ins[3 * i + 2][...])
            for ref, val in zip(outs[4 * i:4 * i + 4], (g, d, m_new, v_new)):
                ref[...] = val
        outs[28][...] = jnp.broadcast_to(total(mlp_ref, slice(2, 3), slice(0, LANES)), (8, LANES))

    operands, out_shape = [], []
    for name in SMALL_NAMES:
        operands += [w[name], m[name], v[name]]
        out_shape += [jax.ShapeDtypeStruct(w[name].shape, F32)] * 4
    out_shape.append(jax.ShapeDtypeStruct((8, LANES), F32))
    outs = pl.pallas_call(body, name="adamw_small", out_shape=out_shape)(
        mlp_all, mix_all, scale_all, nmix_all, *operands)
    return {name: outs[4 * i:4 * i + 4] for i, name in enumerate(SMALL_NAMES)}, outs[28]


def kernel(x, norm_mix, w_in, pool_w, pool_b, pool_scale, attn_sinks, p_pool, p_attn, w_out, norm_mlp, w_up, w_down, norm_final, loss_target, m_norm_mix, m_w_in, m_pool_w, m_pool_b, m_pool_scale, m_attn_sinks, m_p_pool, m_p_attn, m_w_out, m_norm_mlp, m_w_up, m_w_down, m_norm_final, v_norm_mix, v_w_in, v_pool_w, v_pool_b, v_pool_scale, v_attn_sinks, v_p_pool, v_p_attn, v_w_out, v_norm_mlp, v_w_up, v_w_down, v_norm_final):
    xs = x[0]
    tgt = loss_target[0]
    s_len = xs.shape[0]

    p_pool_bf, p_attn_bf, w_out_bf, w_up_bf, w_down_bf = [
        t[0].astype(BF16) for t in (p_pool, p_attn, w_out, w_up, w_down)]
    w_in_bf = w_in[0].T.astype(BF16)
    (w_in_g,) = _all_gather_weights("all_gather_w_in", [w_in_bf])
    ag_rest = _exchange_start(
        "ag_rest_start", [p_pool_bf, p_attn_bf, w_out_bf, w_up_bf, w_down_bf], ("chip",) * 5, w_in_g)

    pool_w_bf = pool_w[0].astype(BF16)
    pool_b_row = pool_b[0].reshape(1, POOL_WIDTH)
    bias_t, sink_row = _attn_constants(attn_sinks[0])

    w_in_t = w_in_g.reshape(IN_WIDTH, D_MODEL)
    u, zp, q, kv, zg = _fwd_in(ag_rest["token"], xs, norm_mix, w_in_t)
    half = (s_len // min(512, s_len)) // 2
    mixer_args = (zp, q, kv, pool_w_bf, pool_b_row, pool_scale, bias_t, sink_row)
    pm, o = _mixers_fwd(*mixer_args, n_tiles=half) if half else (None, None)
    first_level, _ = _exchange_wait("ag_rest_wait", ag_rest, zg if o is None else o)
    relay = _relay_start("ag_relay_start", first_level, zg)
    pm, o = _mixers_fwd(*mixer_args, first_tile=half, earlier=None if o is None else (relay["token"], pm, o))
    p_pool_g, p_attn_g, w_out_g, w_up_g, w_down_g = _relay_wait("ag_relay_wait", relay, o)
    p_pool_f = p_pool_g.transpose(1, 0, 2).reshape(POOL_WIDTH, D_MODEL)
    p_attn_f = p_attn_g.transpose(1, 0, 2).reshape(ATTN_WIDTH, D_MODEL)
    w_out_f = w_out_g.reshape(D_MODEL, D_MODEL)
    w_down_f = w_down_g.reshape(D_FF, D_MODEL)
    mixed, dh1, a, dapre, u2, dh2, small_mlp, dyp, dya, dzg, dpm, do, dh1_bf = _core(
        xs, pm, o, zg, tgt, norm_mlp, norm_final.reshape(1, D_MODEL), p_pool_f, p_attn_f, w_out_f, w_up_g, w_down_f)
    gw_down = _tn_matmul(a, dh2, square_a=True)
    gw_up = _tn_matmul(u2, dapre, col_blocks=N_DEV)
    ex_mlp = _exchange_start(
        "ex_mlp_start", [gw_up, gw_down.reshape(N_DEV, D_FF // N_DEV, D_MODEL)], (True, True), small_mlp)
    dzp, dq, dkv, small_mix, g_pool_scale = _mixers_bwd(
        ex_mlp["token"], zp, q, kv, dpm, do, pool_w_bf, pool_b_row, pool_scale, bias_t, sink_row)
    gw_in = _tn_w_in(u, dzp, dq, dkv, dzg)
    ex_in = _exchange_start(
        "ex_in_start", [gw_in, small_mlp, small_mix, g_pool_scale], (True, False, False, False), dq)
    gw_out = _tn_matmul(mixed, dh1_bf, after=ex_in["token"])
    gp_pool = _tn_matmul(pm, dyp, col_blocks=N_DEV, after=ex_in["token"])
    gp_attn = _tn_matmul(o, dya, col_blocks=N_DEV, after=ex_in["token"])
    ex_proj = _exchange_start(
        "ex_proj_start", [gp_pool, gp_attn, gw_out.reshape(N_DEV, D_MODEL // N_DEV, D_MODEL)], (True,) * 3,
        ex_in["token"])
    dx, g_norm_mix = _in_bwd(ex_proj["token"], dzp, dq, dkv, dzg, w_in_t, xs, dh1, norm_mix)

    big_w = dict(w_in=w_in, p_pool=p_pool, p_attn=p_attn, w_out=w_out, w_up=w_up, w_down=w_down)
    big_m = dict(w_in=m_w_in, p_pool=m_p_pool, p_attn=m_p_attn, w_out=m_w_out, w_up=m_w_up, w_down=m_w_down)
    big_v = dict(w_in=v_w_in, p_pool=v_p_pool, p_attn=v_p_attn, w_out=v_w_out, w_up=v_w_up, w_down=v_w_down)
    res = {}

    def update(names, recvs, sents):
        for name, parts, sent in zip(names, recvs, sents):
            flip = (lambda t: t.T) if name == "w_in" else (lambda t: t)
            outs = _adamw(parts, flip(big_w[name][0]), flip(big_m[name][0]), flip(big_v[name][0]), sent)
            res[name] = [flip(t)[None] for t in outs]

    update(["w_up", "w_down"], *_exchange_wait("ex_mlp_wait", ex_mlp, dx))
    (norm_mix_all,) = _all_gather_weights("all_gather_norm_mix", [g_norm_mix], res["w_down"][0])
    (r_in, mlp_all, mix_all, scale_all), (s_in, _, _, _) = _exchange_wait("ex_in_wait", ex_in, norm_mix_all)
    update(["w_in"], [r_in], [s_in])

    natural = dict(norm_mix=(1, D_MODEL), pool_w=(MIX_POOL_B, LANES), pool_b=(4, LANES), pool_scale=(1, POOL_WIDTH),
                   attn_sinks=(1, LANES), norm_mlp=(1, D_MODEL), norm_final=(1, D_MODEL))

    def as_2d(t, name):
        if name == "attn_sinks":
            return jnp.pad(t, ((0, 0), (0, LANES - N_HEADS)))
        return t.reshape(natural[name])

    small_w = dict(norm_mix=norm_mix, pool_w=pool_w, pool_b=pool_b, pool_scale=pool_scale, attn_sinks=attn_sinks,
                   norm_mlp=norm_mlp, norm_final=norm_final)
    small_m = dict(norm_mix=m_norm_mix, pool_w=m_pool_w, pool_b=m_pool_b, pool_scale=m_pool_scale,
                   attn_sinks=m_attn_sinks, norm_mlp=m_norm_mlp, norm_final=m_norm_final)
    small_v = dict(norm_mix=v_norm_mix, pool_w=v_pool_w, pool_b=v_pool_b, pool_scale=v_pool_scale,
                   attn_sinks=v_attn_sinks, norm_mlp=v_norm_mlp, norm_final=v_norm_final)
    small_res, loss_all = _adamw_small(
        mlp_all, mix_all, scale_all, norm_mix_all,
        *[{k: as_2d(t, k) for k, t in d.items()} for d in (small_w, small_m, small_v)])
    loss = loss_all[0, 0]
    for name in SMALL_NAMES:
        shape = small_w[name].shape
        res[name] = [(t[:, :N_HEADS] if name == "attn_sinks" else t).reshape(shape) for t in small_res[name]]
    update(["p_pool", "p_attn", "w_out"], *_exchange_wait("ex_proj_wait", ex_proj, loss_all))

    order = ["norm_mix", "w_in", "pool_w", "pool_b", "pool_scale", "attn_sinks", "p_pool", "p_attn", "w_out",
             "norm_mlp", "w_up", "w_down", "norm_final"]
    out = [loss, dx[None]]
    for kind in range(4):
        out += [res[name][kind] for name in order]
    return tuple(out)
```

```python
import functools
import math

import numpy as np
import jax
import jax.numpy as jnp
from jax import lax
from jax.experimental import pallas as pl
from jax.experimental.pallas import tpu as pltpu

F32 = jnp.float32
BF16 = jnp.bfloat16

D_MODEL = 1024
POOL_WIDTH = 512
ATTN_WIDTH = 512
KV_WIDTH = 128
HEAD_DIM = 64
N_HEADS = 8
N_KV_HEADS = 2
GROUP = 4
BLOCK = 128
POOL_WINDOWS = (2, 4, 8, 16)
POOL_GROUP_DIM = 128
POOL_HALO = 16
D_FF = 4096
FF_CHUNK = 1024
IN_WIDTH = 3328
RMS_EPS = 1e-5
NEG_INF = -1e30
ATTN_SCALE = 1.0 / math.sqrt(HEAD_DIM)
N_DEV = 8

ADAM_LR = 0.001
ADAM_B1 = 0.9
ADAM_B2 = 0.999
ADAM_EPS = 1e-08
ADAM_WD = 0.01
ADAM_STEP = 10

LANES = 128
VMEM_LIMIT_BYTES = 56 * 1024 * 1024
MESH = pl.DeviceIdType.MESH


def _params(n_grid_axes=1):
    return pltpu.CompilerParams(
        dimension_semantics=("arbitrary",) * n_grid_axes, vmem_limit_bytes=VMEM_LIMIT_BYTES)


def _dot(a, b):
    return jnp.dot(a, b, preferred_element_type=F32)


def _dot_nt(a, b):
    return lax.dot_general(a, b, (((1,), (1,)), ((), ())), preferred_element_type=F32)


def _dot_tn(a, b):
    return lax.dot_general(a, b, (((0,), (0,)), ((), ())), preferred_element_type=F32)


ANY = pl.BlockSpec(memory_space=pl.ANY)


def _rows(tm, n):
    return pl.BlockSpec((tm, n), lambda i: (i, 0))


def _whole(shape):
    zeros = (0,) * len(shape)
    return pl.BlockSpec(shape, lambda i: zeros)


def _rms_fwd(h, g):
    r = lax.rsqrt(jnp.mean(h * h, axis=-1, keepdims=True) + RMS_EPS)
    xh = h * r
    return r, xh, xh * g


def _rms_bwd(dy, xh, r, g):
    dxh = dy * g
    dh = r * (dxh - xh * jnp.mean(dxh * xh, axis=-1, keepdims=True))
    return dh, jnp.sum(dy * xh, axis=0, keepdims=True)


def _fwd_in(after, x, g_mix, w_in_t):
    s_len = x.shape[0]
    tm = min(512, s_len)

    def body(after_ref, x_ref, g_ref, w_ref, u_ref, zp_ref, q_ref, kv_ref, zg_ref):
        _, _, u = _rms_fwd(x_ref[...], g_ref[...])
        u = u.astype(BF16)
        u_ref[...] = u
        zp_ref[...] = _dot_nt(u, w_ref[0:512, :]).astype(BF16)
        q_ref[...] = _dot_nt(u, w_ref[512:1024, :]).astype(BF16)
        kv_ref[...] = _dot_nt(u, w_ref[1024:1280, :]).astype(BF16)
        zg_ref[...] = _dot_nt(u, w_ref[1280:3328, :]).astype(BF16)

    return pl.pallas_call(
        body, name="fwd_in", grid=(s_len // tm,),
        in_specs=[ANY, _rows(tm, D_MODEL), _whole((1, D_MODEL)),
                  pl.BlockSpec((IN_WIDTH, D_MODEL), lambda i: (0, 0), pipeline_mode=pl.Buffered(1))],
        out_specs=[_rows(tm, D_MODEL), _rows(tm, 512), _rows(tm, 512), _rows(tm, 256), _rows(tm, 2048)],
        out_shape=[jax.ShapeDtypeStruct((s_len, n), BF16) for n in (D_MODEL, 512, 512, 256, 2048)],
        compiler_params=_params(),
    )(after, x, g_mix, w_in_t)


def _attn_constants(sinks):
    r = np.arange(BLOCK)[:, None]
    qi = np.arange(BLOCK)[None, :]
    dist = np.where(r <= qi, qi - r, BLOCK + qi - r).astype(np.float32)
    slopes = np.array([2.0 ** (-8.0 * (h + 1) / N_HEADS) for h in range(N_HEADS)], dtype=np.float32)
    bias = (-slopes[:, None, None] * dist[None]).reshape(N_KV_HEADS, GROUP, BLOCK, BLOCK)
    bias = np.ascontiguousarray(bias.transpose(0, 2, 1, 3)).reshape(N_KV_HEADS, BLOCK, GROUP * BLOCK)
    sink_row = jnp.repeat(sinks.astype(F32).reshape(N_KV_HEADS, GROUP), BLOCK, axis=1)[:, None, :]
    return jnp.asarray(bias.astype(np.float32)), sink_row


def _own_block_mask():
    shape = (BLOCK, GROUP * BLOCK)
    r = lax.broadcasted_iota(jnp.int32, shape, 0)
    qi = lax.broadcasted_iota(jnp.int32, shape, 1) & (BLOCK - 1)
    return r <= qi


def _pack_keys(t, own):
    return jnp.where(own, t[BLOCK:], t[:BLOCK])


def _unpack_keys(t, own):
    zero = jnp.zeros_like(t)
    return jnp.concatenate([jnp.where(own, zero, t), jnp.where(own, t, zero)], axis=0)


def _left_half(shape):
    return lax.broadcasted_iota(jnp.int32, shape, 1) < HEAD_DIM


def _dup_halves(slab):
    swapped = pltpu.roll(slab, HEAD_DIM, 1)
    left = _left_half(slab.shape)
    return jnp.where(left, slab, swapped), jnp.where(left, swapped, slab)


def _fill_kv_slabs(kvh_ref, kv_ref, ka_ref, vd_ref):
    for rows, src in ((slice(0, BLOCK), kvh_ref), (slice(BLOCK, None), kv_ref)):
        kvf = src[...].astype(F32)
        for ref, lanes in ((ka_ref, slice(0, KV_WIDTH)), (vd_ref, slice(KV_WIDTH, 2 * KV_WIDTH))):
            d0, d1 = _dup_halves(kvf[:, lanes])
            ref[0, rows, :] = d0.astype(BF16)
            ref[1, rows, :] = d1.astype(BF16)


def _stack_pairs(a, h):
    pieces = []
    for j in range(2):
        pair = a[:, h * 256 + j * LANES:h * 256 + (j + 1) * LANES]
        left = _left_half(pair.shape)
        zero = jnp.zeros_like(pair)
        pieces += [jnp.where(left, pair, zero), jnp.where(left, zero, pair)]
    return jnp.concatenate(pieces, axis=0)


def _attn_probs(kk, q_st, bias_p, sink_row, own, first):
    s = _pack_keys(_dot_nt(kk, q_st), own) * ATTN_SCALE + bias_p
    if first is not None:
        s = jnp.where(jnp.logical_and(first, jnp.logical_not(own)), NEG_INF, s)
    m = jnp.maximum(jnp.max(s, axis=0, keepdims=True), sink_row)
    p = jnp.exp(s - m)
    es = jnp.exp(sink_row - m)
    inv = 1.0 / (jnp.sum(p, axis=0, keepdims=True) + es)
    return p * inv, es * inv


def _pool_d(ext, cur, g, row0):
    w = POOL_WINDOWS[g]
    acc = ext
    k = 1
    while k < w:
        acc = acc + pltpu.roll(acc, k, 0)
        k *= 2
    return _window_mean(acc[POOL_HALO:, :], w, row0) - cur


def _window_mean(total, w, row0):
    t = row0 + lax.broadcasted_iota(jnp.int32, (POOL_HALO, total.shape[1]), 0)
    head = total[:POOL_HALO] / jnp.minimum(t + 1, w).astype(F32)
    return jnp.concatenate([head, total[POOL_HALO:] * (1.0 / w)], axis=0)


def _mixers_fwd(zp, q, kv, pool_w, pool_b, pool_scale, bias_t, sink_row, first_tile=0, n_tiles=None, earlier=None):
    s_len = zp.shape[0]
    tq = min(512, s_len)
    nb = tq // BLOCK
    n_tiles = s_len // tq - first_tile if n_tiles is None else n_tiles
    extra = [] if earlier is None else list(earlier)

    def body(zp_ref, zph_ref, q_ref, kv_ref, kvh_ref, pw_ref, pb_ref, ps_ref, bias_ref, sink_ref, *rest):
        pm_ref, o_ref, ka_ref, vd_ref = rest[len(extra):]
        i = pl.program_id(0) + first_tile
        cur = zp_ref[...].astype(F32)
        halo = zph_ref[...].astype(F32) * (i > 0).astype(F32)
        ext = jnp.concatenate([halo, cur], axis=0)
        for g in range(4):
            sl = slice(g * POOL_GROUP_DIM, (g + 1) * POOL_GROUP_DIM)
            d = _pool_d(ext[:, sl], cur[:, sl], g, i * tq)
            y = _dot(d.astype(BF16), pw_ref[g]) + pb_ref[:, sl]
            pm_ref[:, sl] = (y * ps_ref[:, sl]).astype(BF16)
        _fill_kv_slabs(kvh_ref, kv_ref, ka_ref, vd_ref)
        own = _own_block_mask()
        for b in range(nb):
            rq = slice(b * BLOCK, (b + 1) * BLOCK)
            rk = slice(b * BLOCK, (b + 2) * BLOCK)
            qb = q_ref[rq, :]
            for h in range(N_KV_HEADS):
                pn, _ = _attn_probs(ka_ref[h, rk, :], _stack_pairs(qb, h), bias_ref[h], sink_ref[h], own,
                                    (i == 0) if b == 0 else None)
                pn = _unpack_keys(pn, own).astype(BF16)
                vd = vd_ref[h, rk, :]
                left = _left_half(vd.shape)
                zero = jnp.zeros_like(vd)
                va, vb = jnp.where(left, vd, zero), jnp.where(left, zero, vd)
                for j in range(2):
                    o_pair = (_dot_tn(pn[:, (2 * j) * BLOCK:(2 * j + 1) * BLOCK], va)
                              + _dot_tn(pn[:, (2 * j + 1) * BLOCK:(2 * j + 2) * BLOCK], vb))
                    o_ref[rq, h * 256 + j * LANES:h * 256 + (j + 1) * LANES] = o_pair.astype(BF16)

    def tile(n):
        return pl.BlockSpec((tq, n), lambda i: (i + first_tile, 0))

    halo_pool = pl.BlockSpec(
        (POOL_HALO, 512), lambda i: (jnp.maximum((i + first_tile) * (tq // POOL_HALO) - 1, 0), 0))
    halo_kv = pl.BlockSpec((BLOCK, 256), lambda i: (jnp.maximum((i + first_tile) * nb - 1, 0), 0))
    return pl.pallas_call(
        body, name="mixers_fwd", grid=(n_tiles,),
        in_specs=[tile(512), halo_pool, tile(512), tile(256), halo_kv,
                  _whole((4, 128, 128)), _whole((1, 512)), _whole((1, 512)),
                  _whole((N_KV_HEADS, BLOCK, GROUP * BLOCK)), _whole((N_KV_HEADS, 1, GROUP * BLOCK))]
        + [ANY] * len(extra),
        out_specs=[tile(512), tile(512)],
        out_shape=[jax.ShapeDtypeStruct((s_len, 512), BF16)] * 2,
        input_output_aliases={11: 0, 12: 1} if extra else {},
        scratch_shapes=[pltpu.VMEM((N_KV_HEADS, tq + BLOCK, LANES), BF16)] * 2,
        compiler_params=_params(),
    )(zp, zp, q, kv, kv, pool_w, pool_b, pool_scale, bias_t, sink_row, *extra)


def _gated_mix(pm, o, zg, pp_ref, pa_ref):
    yp = _dot(pm, pp_ref[...])
    ya = _dot(o, pa_ref[...])
    gp = jax.nn.sigmoid(zg[:, :D_MODEL].astype(F32))
    ga = jax.nn.sigmoid(zg[:, D_MODEL:].astype(F32))
    return yp, ya, gp, ga


def _core(x, pm, o, zg, tgt, g_mlp, g_fin, p_pool, p_attn, w_out, w_up_blocks, w_down):
    s_len = x.shape[0]
    tm = min(256, s_len)
    n_chunks = D_FF // FF_CHUNK
    up_block = D_FF // N_DEV
    per_chunk = FF_CHUNK // up_block

    def body(x_ref, pm_ref, o_ref, zg_ref, tgt_ref, gm_ref, gf_ref, pp_ref, pa_ref, wo_ref, wu_ref, wd_ref,
             mixed_ref, dh1_ref, a_ref, dap_ref, u2_ref, dh2_ref, small_ref,
             dyp_ref, dya_ref, dzg_ref, dpm_ref, do_ref, dh1b_ref):
        i = pl.program_id(0)

        @pl.when(i == 0)
        def _():
            small_ref[...] = jnp.zeros_like(small_ref)

        yp, ya, gp, ga = _gated_mix(pm_ref[...], o_ref[...], zg_ref[...], pp_ref, pa_ref)
        mixed = (gp * yp + ga * ya).astype(BF16)
        mixed_ref[...] = mixed
        h1 = x_ref[...] + _dot(mixed, wo_ref[...])
        r2, xh2, u2 = _rms_fwd(h1, gm_ref[...])
        u2 = u2.astype(BF16)
        u2_ref[...] = u2
        acc = jnp.zeros((tm, D_MODEL), F32)
        for c in range(n_chunks):
            cs = slice(c * FF_CHUNK, (c + 1) * FF_CHUNK)
            a = jnp.concatenate([_dot(u2, wu_ref[per_chunk * c + j]) for j in range(per_chunk)], axis=1)
            a = jnp.maximum(a, 0.0)
            a_ref[:, cs] = a.astype(BF16)
            acc = acc + _dot((a * a).astype(BF16), wd_ref[cs, :])
        h2 = h1 + acc
        r3, xh3, y = _rms_fwd(h2, gf_ref[...])
        diff = y - tgt_ref[...]
        small_ref[2:3, :] += 0.5 * jnp.sum(jnp.mean(diff * diff, axis=-1, keepdims=True))
        dy = diff * (1.0 / D_MODEL)
        dh2, dgf = _rms_bwd(dy, xh3, r3, gf_ref[...])
        small_ref[1:2, :] += dgf
        dh2_bf = dh2.astype(BF16)
        dh2_ref[...] = dh2_bf
        du2 = jnp.zeros((tm, D_MODEL), F32)
        for c in range(n_chunks):
            cs = slice(c * FF_CHUNK, (c + 1) * FF_CHUNK)
            ds = _dot_nt(dh2_bf, wd_ref[cs, :])
            dap = (ds * (2.0 * a_ref[:, cs].astype(F32))).astype(BF16)
            dap_ref[:, cs] = dap
            for j in range(per_chunk):
                du2 = du2 + _dot_nt(dap[:, j * up_block:(j + 1) * up_block], wu_ref[per_chunk * c + j])
        dh1n, dgm = _rms_bwd(du2, xh2, r2, gm_ref[...])
        small_ref[0:1, :] += dgm
        dh1 = dh2 + dh1n
        dh1_ref[...] = dh1
        dh1_bf = dh1.astype(BF16)
        dh1b_ref[...] = dh1_bf
        dm = _dot_nt(dh1_bf, wo_ref[...])
        dyp = (dm * gp).astype(BF16)
        dya = (dm * ga).astype(BF16)
        dyp_ref[...] = dyp
        dya_ref[...] = dya
        dzg_ref[:, :D_MODEL] = (dm * yp * (gp * (1.0 - gp))).astype(BF16)
        dzg_ref[:, D_MODEL:] = (dm * ya * (ga * (1.0 - ga))).astype(BF16)
        dpm_ref[...] = _dot_nt(dyp, pp_ref[...]).astype(BF16)
        do_ref[...] = _dot_nt(dya, pa_ref[...]).astype(BF16)

    def fixed(shape):
        return pl.BlockSpec(shape, lambda i: (0,) * len(shape), pipeline_mode=pl.Buffered(1))

    widths_dtypes = ((D_MODEL, BF16), (D_MODEL, F32), (D_FF, BF16), (D_FF, BF16), (D_MODEL, BF16), (D_MODEL, BF16))
    back = ((D_MODEL, BF16), (D_MODEL, BF16), (2048, BF16), (512, BF16), (512, BF16), (D_MODEL, BF16))
    return pl.pallas_call(
        body, name="core", grid=(s_len // tm,),
        in_specs=[_rows(tm, D_MODEL), _rows(tm, 512), _rows(tm, 512), _rows(tm, 2048), _rows(tm, D_MODEL),
                  _whole((1, D_MODEL)), _whole((1, D_MODEL)),
                  fixed((512, D_MODEL)), fixed((512, D_MODEL)), fixed((D_MODEL, D_MODEL)),
                  fixed((N_DEV, D_MODEL, up_block)), fixed((D_FF, D_MODEL))],
        out_specs=[_rows(tm, n) for n, _ in widths_dtypes] + [_whole((8, D_MODEL))] + [_rows(tm, n) for n, _ in back],
        out_shape=[jax.ShapeDtypeStruct((s_len, n), d) for n, d in widths_dtypes]
        + [jax.ShapeDtypeStruct((8, D_MODEL), F32)] + [jax.ShapeDtypeStruct((s_len, n), d) for n, d in back],
        compiler_params=_params(),
    )(x, pm, o, zg, tgt, g_mlp, g_fin, p_pool, p_attn, w_out, w_up_blocks, w_down)


def _tn_matmul(a, b, square_a=False, col_blocks=None, after=None):
    s_len, ka = a.shape
    nb = b.shape[1]
    tt = min(2048, s_len)
    tk = min(1024, ka)
    tn = min(1024, nb)
    n_t = s_len // tt
    if col_blocks is None:
        out_spec = pl.BlockSpec((tk, tn), lambda k, j, t: (k, j))
        out_shape = jax.ShapeDtypeStruct((ka, nb), BF16)
    else:
        width = nb // col_blocks
        per_tile = tn // width
        out_spec = pl.BlockSpec((per_tile, tk, width), lambda k, j, t: (j, k, 0))
        out_shape = jax.ShapeDtypeStruct((col_blocks, ka, width), BF16)

    extra = [] if after is None else [after]

    def body(a_ref, b_ref, *rest):
        o_ref, acc_ref = rest[len(extra):]
        t = pl.program_id(2)

        @pl.when(t == 0)
        def _():
            acc_ref[...] = jnp.zeros_like(acc_ref)

        av = a_ref[...]
        if square_a:
            av = av * av
        acc_ref[...] += _dot_tn(av.astype(BF16), b_ref[...].astype(BF16))

        @pl.when(t == n_t - 1)
        def _():
            if col_blocks is None:
                o_ref[...] = acc_ref[...].astype(o_ref.dtype)
            else:
                for blk in range(per_tile):
                    o_ref[blk] = acc_ref[:, blk * width:(blk + 1) * width].astype(o_ref.dtype)

    return pl.pallas_call(
        body, name="tn_matmul", grid=(ka // tk, nb // tn, n_t),
        in_specs=[pl.BlockSpec((tt, tk), lambda k, j, t: (t, k)), pl.BlockSpec((tt, tn), lambda k, j, t: (t, j))]
        + [ANY] * len(extra),
        out_specs=out_spec, out_shape=out_shape,
        scratch_shapes=[pltpu.VMEM((tk, tn), F32)],
        compiler_params=_params(3),
    )(a, b, *extra)


def _tn_w_in(u, dzp, dq, dkv, dzg):
    s_len = u.shape[0]
    tt = min(1024, s_len)
    n_t = s_len // tt
    width = IN_WIDTH // N_DEV
    pieces = ((0, 512), (512, 1024), (1024, 1280), (1280, IN_WIDTH))

    def body(u_ref, dzp_ref, dq_ref, dkv_ref, dzg_ref, o_ref, acc_ref):
        t = pl.program_id(0)

        @pl.when(t == 0)
        def _():
            acc_ref[...] = jnp.zeros_like(acc_ref)

        uv = u_ref[...]
        for (c0, c1), ref in zip(pieces, (dzp_ref, dq_ref, dkv_ref, dzg_ref)):
            acc_ref[c0:c1, :] += _dot_tn(ref[...], uv)

        @pl.when(t == n_t - 1)
        def _():
            for j in range(N_DEV):
                o_ref[j] = acc_ref[j * width:(j + 1) * width, :].astype(BF16)

    return pl.pallas_call(
        body, name="tn_w_in", grid=(n_t,),
        in_specs=[_rows(tt, D_MODEL)] + [_rows(tt, c1 - c0) for c0, c1 in pieces],
        out_specs=_whole((N_DEV, width, D_MODEL)),
        out_shape=jax.ShapeDtypeStruct((N_DEV, width, D_MODEL), BF16),
        scratch_shapes=[pltpu.VMEM((IN_WIDTH, D_MODEL), F32)],
        compiler_params=_params(),
    )(u, dzp, dq, dkv, dzg)


MIX_POOL_B = 4 * POOL_GROUP_DIM
MIX_SINKS = MIX_POOL_B + 8
MIX_ROWS = MIX_SINKS + 8


def _mixers_bwd(after, zp, q, kv, dpm, do, pool_w, pool_b, pool_scale, bias_t, sink_row):
    s_len = zp.shape[0]
    tq = min(512, s_len)
    nb = tq // BLOCK
    n_steps = s_len // tq

    def body(after_ref, zp_ref, zph_ref, q_ref, kv_ref, kvh_ref, dpm_ref, dpmh_ref, do_ref, pw_ref, pb_ref, ps_ref,
             bias_ref, sink_ref, dzp_ref, dq_ref, dkv_ref, small_ref, dps_ref,
             ka_ref, vd_ref, dsk_acc, dkv_acc):
        i = pl.program_id(0)

        @pl.when(i == 0)
        def _():
            dkv_acc[...] = jnp.zeros_like(dkv_acc)
            small_ref[...] = jnp.zeros_like(small_ref)
            dps_ref[...] = jnp.zeros_like(dps_ref)
            dsk_acc[...] = jnp.zeros_like(dsk_acc)

        cur = zp_ref[...].astype(F32)
        halo = zph_ref[...].astype(F32) * (i > 0).astype(F32)
        ext = jnp.concatenate([halo, cur], axis=0)
        dpm_next = dpmh_ref[...].astype(F32) * (i < n_steps - 1).astype(F32)
        dpm_ext = jnp.concatenate([dpm_ref[...].astype(F32), dpm_next], axis=0)
        n_ext = tq + POOL_HALO
        for g in range(4):
            sl = slice(g * POOL_GROUP_DIM, (g + 1) * POOL_GROUP_DIM)
            w = POOL_WINDOWS[g]
            d = _pool_d(ext[:, sl], cur[:, sl], g, i * tq).astype(BF16)
            y_lin = _dot(d, pw_ref[g]) + pb_ref[:, sl]
            dps_ref[:, sl] += jnp.sum(dpm_ext[:tq, sl] * y_lin, axis=0, keepdims=True)
            dyl_ext = dpm_ext[:, sl] * ps_ref[:, sl]
            small_ref[MIX_POOL_B + g:MIX_POOL_B + g + 1, :] += jnp.sum(dyl_ext[:tq], axis=0, keepdims=True)
            dyl_bf = dyl_ext.astype(BF16)
            small_ref[g * POOL_GROUP_DIM:(g + 1) * POOL_GROUP_DIM, :] += _dot_tn(d, dyl_bf[:tq])
            dd = _dot_nt(dyl_bf, pw_ref[g])
            e = _window_mean(dd, w, i * tq)
            acc = e
            k = 1
            while k < w:
                acc = acc + pltpu.roll(acc, n_ext - k, 0)
                k *= 2
            dzp_ref[:, sl] = (acc[:tq] - dd[:tq]).astype(BF16)

        _fill_kv_slabs(kvh_ref, kv_ref, ka_ref, vd_ref)

        def fold(dup):
            return dup + pltpu.roll(dup, HEAD_DIM, 1)

        own = _own_block_mask()
        for b in range(nb):
            rq = slice(b * BLOCK, (b + 1) * BLOCK)
            rk = slice(b * BLOCK, (b + 2) * BLOCK)
            qb = q_ref[rq, :]
            dob = do_ref[rq, :]
            dk_dup, dv_dup = [], []
            for h in range(N_KV_HEADS):
                kk = ka_ref[h, rk, :]
                q_st = _stack_pairs(qb, h)
                do_st = _stack_pairs(dob, h)
                pn, psink = _attn_probs(kk, q_st, bias_ref[h], sink_ref[h], own, (i == 0) if b == 0 else None)
                dp = _pack_keys(_dot_nt(vd_ref[h, rk, :], do_st), own)
                delta = jnp.sum(pn * dp, axis=0, keepdims=True)
                dsk_acc[h] += -psink * delta
                ds = _unpack_keys((pn * (dp - delta)) * ATTN_SCALE, own).astype(BF16)
                pn = _unpack_keys(pn, own)
                dq_st = _dot_tn(ds, kk)
                for j in range(2):
                    left = _left_half((BLOCK, LANES))
                    dq_pair = jnp.where(left, dq_st[(2 * j) * BLOCK:(2 * j + 1) * BLOCK],
                                        dq_st[(2 * j + 1) * BLOCK:(2 * j + 2) * BLOCK])
                    dq_ref[rq, h * 256 + j * LANES:h * 256 + (j + 1) * LANES] = dq_pair.astype(BF16)
                dk_dup.append(fold(_dot(ds, q_st)))
                dv_dup.append(fold(_dot(pn.astype(BF16), do_st)))
            left = _left_half((2 * BLOCK, LANES))
            dkv_blk = jnp.concatenate([jnp.where(left, dk_dup[0], dk_dup[1]),
                                       jnp.where(left, dv_dup[0], dv_dup[1])], axis=1)
            g0 = pl.multiple_of(i * tq + b * BLOCK, BLOCK)
            dkv_acc[pl.ds(g0, 2 * BLOCK), :] += dkv_blk

        @pl.when(i == n_steps - 1)
        def _():
            dkv_ref[...] = dkv_acc[BLOCK:, :].astype(BF16)
            lane = lax.broadcasted_iota(jnp.int32, (1, LANES), 1)
            row = jnp.zeros((1, LANES), F32)
            for h in range(N_KV_HEADS):
                for g in range(GROUP):
                    tot = jnp.sum(dsk_acc[h, :, g * BLOCK:(g + 1) * BLOCK], axis=1, keepdims=True)
                    row = jnp.where(lane == GROUP * h + g, tot, row)
            small_ref[MIX_SINKS:MIX_SINKS + 1, :] = row

    blocks_per_tile = tq // POOL_HALO
    last_halo = s_len // POOL_HALO - 1
    halo_prev = pl.BlockSpec((POOL_HALO, 512), lambda i: (jnp.maximum(i * blocks_per_tile - 1, 0), 0))
    halo_next = pl.BlockSpec((POOL_HALO, 512), lambda i: (jnp.minimum((i + 1) * blocks_per_tile, last_halo), 0))
    halo_kv = pl.BlockSpec((BLOCK, 256), lambda i: (jnp.maximum(i * nb - 1, 0), 0))
    return pl.pallas_call(
        body, name="mixers_bwd", grid=(n_steps,),
        in_specs=[ANY, _rows(tq, 512), halo_prev, _rows(tq, 512), _rows(tq, 256), halo_kv,
                  _rows(tq, 512), halo_next, _rows(tq, 512),
                  _whole((4, 128, 128)), _whole((1, 512)), _whole((1, 512)),
                  _whole((N_KV_HEADS, BLOCK, GROUP * BLOCK)), _whole((N_KV_HEADS, 1, GROUP * BLOCK))],
        out_specs=[_rows(tq, 512), _rows(tq, 512), _whole((s_len, 256)),
                   _whole((MIX_ROWS, LANES)), _whole((1, 512))],
        out_shape=[jax.ShapeDtypeStruct((s_len, 512), BF16), jax.ShapeDtypeStruct((s_len, 512), BF16),
                   jax.ShapeDtypeStruct((s_len, 256), BF16), jax.ShapeDtypeStruct((MIX_ROWS, LANES), F32),
                   jax.ShapeDtypeStruct((1, 512), F32)],
        scratch_shapes=[pltpu.VMEM((N_KV_HEADS, tq + BLOCK, LANES), BF16)] * 2
        + [pltpu.VMEM((N_KV_HEADS, 1, GROUP * BLOCK), F32), pltpu.VMEM((s_len + BLOCK, 256), F32)],
        compiler_params=_params(),
    )(after, zp, zp, q, kv, kv, dpm, dpm, do, pool_w, pool_b, pool_scale, bias_t, sink_row)


def _in_bwd(after, dzp, dq, dkv, dzg, w_in_t, x, dh1, g_mix):
    s_len = x.shape[0]
    tm = min(512, s_len)

    def body(after_ref, dzp_ref, dq_ref, dkv_ref, dzg_ref, w_ref, x_ref, dh1_ref, g_ref, dx_ref, dg_ref):
        i = pl.program_id(0)

        @pl.when(i == 0)
        def _():
            dg_ref[...] = jnp.zeros_like(dg_ref)

        du = _dot(dzp_ref[...], w_ref[0:512, :])
        du = du + _dot(dq_ref[...], w_ref[512:1024, :])
        du = du + _dot(dkv_ref[...], w_ref[1024:1280, :])
        du = du + _dot(dzg_ref[...], w_ref[1280:3328, :])
        r, xh, _ = _rms_fwd(x_ref[...], g_ref[...])
        dxn, dg = _rms_bwd(du, xh, r, g_ref[...])
        dg_ref[...] += dg
        dx_ref[...] = dh1_ref[...] + dxn

    return pl.pallas_call(
        body, name="in_bwd", grid=(s_len // tm,),
        in_specs=[ANY, _rows(tm, 512), _rows(tm, 512), _rows(tm, 256), _rows(tm, 2048), _whole((IN_WIDTH, D_MODEL)),
                  _rows(tm, D_MODEL), _rows(tm, D_MODEL), _whole((1, D_MODEL))],
        out_specs=[_rows(tm, D_MODEL), _whole((1, D_MODEL))],
        out_shape=[jax.ShapeDtypeStruct((s_len, D_MODEL), F32), jax.ShapeDtypeStruct((1, D_MODEL), F32)],
        compiler_params=_params(),
    )(after, dzp, dq, dkv, dzg, w_in_t, x, dh1, g_mix)


def _all_gather_weights(name, shards, after=None):
    n = len(shards)
    extra = [] if after is None else [after]
    n_extra = len(extra)

    def body(*refs):
        ins, outs = refs[:n], refs[n + n_extra:2 * n + n_extra]
        send_sems, recv_sems, local_sems = refs[2 * n + n_extra:]
        x, y, c = lax.axis_index("x"), lax.axis_index("y"), lax.axis_index("c")
        me, sibling = (x, y, c), (x, y, 1 - c)
        chips = [(1 - x, y), (x, 1 - y), (1 - x, 1 - y)]

        def slot(a, px, py, pc):
            return outs[a].at[4 * px + 2 * py + pc]

        def copy(a, k, block, to, src=None):
            return pltpu.make_async_remote_copy(
                src_ref=slot(a, *block) if src is None else src, dst_ref=slot(a, *block),
                send_sem=send_sems.at[a, k], recv_sem=recv_sems.at[a, k], device_id=to, device_id_type=MESH)

        mine = [pltpu.make_async_copy(ins[a], slot(a, *me), local_sems.at[a]) for a in range(n)]
        for cp in mine:
            cp.start()
        first = []
        for a in range(n):
            first.append(copy(a, 0, me, sibling, src=ins[a]))
            first += [copy(a, 1 + j, me, (*chip, c), src=ins[a]) for j, chip in enumerate(chips)]
        for cp in first:
            cp.start()
        passed = []
        for a in range(n):
            for j, chip in enumerate(chips):
                copy(a, 1 + j, (*chip, c), me).wait_recv()
                cp = copy(a, 4 + j, (*chip, c), sibling)
                cp.start()
                passed.append(cp)
        for a in range(n):
            copy(a, 0, sibling, me).wait_recv()
            for j, chip in enumerate(chips):
                copy(a, 4 + j, (*chip, 1 - c), me).wait_recv()
        for cp in first + passed:
            cp.wait_send()
        for cp in mine:
            cp.wait()

    return pl.pallas_call(
        body, name=name,
        in_specs=[ANY] * (n + n_extra), out_specs=[ANY] * n,
        out_shape=[jax.ShapeDtypeStruct((N_DEV,) + s.shape, s.dtype) for s in shards],
        scratch_shapes=[pltpu.SemaphoreType.DMA((n, 7)), pltpu.SemaphoreType.DMA((n, 7)), pltpu.SemaphoreType.DMA((n,))],
    )(*shards, *extra)


HBM_SPEC = pl.BlockSpec(memory_space=pltpu.HBM)
SEM_SPEC = pl.BlockSpec(memory_space=pltpu.SEMAPHORE)
DATAFLOW = pltpu.SideEffectType.DATAFLOW_SIDE_EFFECTING
N_PEERS = N_DEV - 1


CHIP_PEERS = (1, 2, 4, 6)
RELAYED = (2, 4, 6)


def _peer_copies(srcs, lands, scatter, send_sems, recv_sems):
    x, y, c = lax.axis_index("x"), lax.axis_index("y"), lax.axis_index("c")
    me_idx = 4 * x + 2 * y + c
    copies = []
    for k in range(1, N_DEV):
        px = 1 - x if (k >> 2) & 1 else x
        py = 1 - y if (k >> 1) & 1 else y
        pc = 1 - c if k & 1 else c
        p_idx = 4 * px + 2 * py + pc
        for a in range(len(srcs)):
            if scatter[a] == "chip" and k not in CHIP_PEERS:
                continue
            src = srcs[a].at[p_idx] if scatter[a] is True else srcs[a]
            dst = lands[a].at[k] if scatter[a] is True else lands[a].at[me_idx]
            copies.append(pltpu.make_async_remote_copy(
                src_ref=src, dst_ref=dst, send_sem=send_sems.at[a * N_PEERS + k - 1],
                recv_sem=recv_sems.at[a * N_PEERS + k - 1],
                device_id=(px, py, pc), device_id_type=MESH))
    return copies


def _exchange_start(name, srcs, scatter, after):
    n = len(srcs)
    lands = [lax.empty(s.shape if sc is True else (N_DEV,) + s.shape, s.dtype) for s, sc in zip(srcs, scatter)]

    def body(*refs):
        src_refs, land_refs = refs[:n], refs[n:2 * n]
        send_sems, recv_sems = refs[2 * n + 1], refs[2 * n + 2]
        token = refs[4 * n + 3]
        for cp in _peer_copies(src_refs, land_refs, scatter, send_sems, recv_sems):
            cp.start()
        token[...] = jnp.zeros_like(token)

    hbm = lambda t: pltpu.HBM(t.shape, t.dtype)
    outs = pl.pallas_call(
        body, name=name,
        out_shape=[pltpu.SemaphoreType.DMA((n * N_PEERS,)), pltpu.SemaphoreType.DMA((n * N_PEERS,))]
        + [hbm(t) for t in srcs] + [hbm(t) for t in lands] + [jax.ShapeDtypeStruct((8, LANES), F32)],
        in_specs=[HBM_SPEC] * (2 * n) + [ANY],
        out_specs=[SEM_SPEC, SEM_SPEC] + [HBM_SPEC] * (2 * n) + [pl.BlockSpec(memory_space=pltpu.VMEM)],
        input_output_aliases={i: 2 + i for i in range(2 * n)},
        compiler_params=pltpu.CompilerParams(has_side_effects=DATAFLOW),
    )(*[pltpu.with_memory_space_constraint(t, pltpu.HBM) for t in list(srcs) + lands], after)
    return dict(n=n, scatter=scatter, send_sems=outs[0], recv_sems=outs[1], srcs=outs[2:2 + n],
                lands=outs[2 + n:2 + 2 * n], token=outs[2 + 2 * n])


def _exchange_wait(name, handle, after):
    n, scatter = handle["n"], handle["scatter"]

    def body(*refs):
        src_refs, land_refs = refs[:n], refs[n:2 * n]
        send_sems, recv_sems = refs[2 * n], refs[2 * n + 1]
        for cp in _peer_copies(src_refs, land_refs, scatter, send_sems, recv_sems):
            cp.wait_send()
            cp.wait_recv()

    both = list(handle["srcs"]) + list(handle["lands"])
    outs = pl.pallas_call(
        body, name=name,
        out_shape=[pltpu.HBM(t.shape, t.dtype) for t in both],
        in_specs=[HBM_SPEC] * (2 * n) + [SEM_SPEC, SEM_SPEC, ANY],
        out_specs=[HBM_SPEC] * (2 * n),
        input_output_aliases={i: i for i in range(2 * n)},
        compiler_params=pltpu.CompilerParams(has_side_effects=DATAFLOW),
    )(*both, handle["send_sems"], handle["recv_sems"], after)
    me_idx = _my_index()
    lands = [land if sc is True else lax.dynamic_update_index_in_dim(land, src, me_idx, 0)
             for land, src, sc in zip(outs[n:], outs[:n], scatter)]
    return lands, outs[:n]


def _my_index():
    return 4 * lax.axis_index("x") + 2 * lax.axis_index("y") + lax.axis_index("c")


def _relay_copies(bufs, send_sems, recv_sems):
    x, y, c = lax.axis_index("x"), lax.axis_index("y"), lax.axis_index("c")
    copies = []
    for j, k in enumerate(RELAYED):
        px = 1 - x if (k >> 2) & 1 else x
        py = 1 - y if (k >> 1) & 1 else y
        slot = 4 * px + 2 * py + c
        for a, buf in enumerate(bufs):
            copies.append(pltpu.make_async_remote_copy(
                src_ref=buf.at[slot], dst_ref=buf.at[slot], send_sem=send_sems.at[a * len(RELAYED) + j],
                recv_sem=recv_sems.at[a * len(RELAYED) + j], device_id=(x, y, 1 - c), device_id_type=MESH))
    return copies


def _relay_start(name, bufs, after):
    n = len(bufs)

    def body(*refs):
        send_sems, recv_sems = refs[n + 1], refs[n + 2]
        for cp in _relay_copies(refs[:n], send_sems, recv_sems):
            cp.start()
        token = refs[2 * n + 3]
        token[...] = jnp.zeros_like(token)

    n_sems = n * len(RELAYED)
    outs = pl.pallas_call(
        body, name=name,
        out_shape=[pltpu.SemaphoreType.DMA((n_sems,)), pltpu.SemaphoreType.DMA((n_sems,))]
        + [pltpu.HBM(t.shape, t.dtype) for t in bufs] + [jax.ShapeDtypeStruct((8, LANES), F32)],
        in_specs=[HBM_SPEC] * n + [ANY],
        out_specs=[SEM_SPEC, SEM_SPEC] + [HBM_SPEC] * n + [pl.BlockSpec(memory_space=pltpu.VMEM)],
        input_output_aliases={i: 2 + i for i in range(n)},
        compiler_params=pltpu.CompilerParams(has_side_effects=DATAFLOW),
    )(*[pltpu.with_memory_space_constraint(t, pltpu.HBM) for t in bufs], after)
    return dict(n=n, send_sems=outs[0], recv_sems=outs[1], bufs=outs[2:2 + n], token=outs[2 + n])


def _relay_wait(name, handle, after):
    n = handle["n"]

    def body(*refs):
        for cp in _relay_copies(refs[:n], refs[n], refs[n + 1]):
            cp.wait_send()
            cp.wait_recv()

    return pl.pallas_call(
        body, name=name,
        out_shape=[pltpu.HBM(t.shape, t.dtype) for t in handle["bufs"]],
        in_specs=[HBM_SPEC] * n + [SEM_SPEC, SEM_SPEC, ANY],
        out_specs=[HBM_SPEC] * n,
        input_output_aliases={i: i for i in range(n)},
        compiler_params=pltpu.CompilerParams(has_side_effects=DATAFLOW),
    )(*handle["bufs"], handle["send_sems"], handle["recv_sems"], after)


def _adamw(parts, w, m, v, sent=None):
    r, c = w.shape
    tr = 256 if r % 256 == 0 else r
    own = sent is not None

    def body(*refs):
        if own:
            _, p_ref, own_ref, w_ref, m_ref, v_ref, g_ref, d_ref, nm_ref, nv_ref = refs
            g = own_ref[...].astype(F32)
        else:
            p_ref, w_ref, m_ref, v_ref, g_ref, d_ref, nm_ref, nv_ref = refs
            g = p_ref[0].astype(F32)
        for k in range(1, N_DEV):
            g = g + p_ref[k].astype(F32)
        m_new = ADAM_B1 * m_ref[...] + (1.0 - ADAM_B1) * g
        v_new = ADAM_B2 * v_ref[...] + (1.0 - ADAM_B2) * (g * g)
        m_hat = m_new / (1.0 - ADAM_B1 ** ADAM_STEP)
        v_hat = v_new / (1.0 - ADAM_B2 ** ADAM_STEP)
        g_ref[...] = g
        d_ref[...] = -ADAM_LR * (m_hat / (jnp.sqrt(v_hat) + ADAM_EPS) + ADAM_WD * w_ref[...])
        nm_ref[...] = m_new
        nv_ref[...] = v_new

    out_shape = [jax.ShapeDtypeStruct((r, c), F32)] * 4
    if not own:
        return pl.pallas_call(
            body, name="adamw", grid=(r // tr,),
            in_specs=[pl.BlockSpec((N_DEV, tr, c), lambda i: (0, i, 0))] + [_rows(tr, c)] * 3,
            out_specs=[_rows(tr, c)] * 4, out_shape=out_shape, compiler_params=_params(),
        )(parts, w, m, v)
    rows = pl.BlockSpec((tr, c), lambda i, me: (i, 0))
    return pl.pallas_call(
        body, name="adamw_own", out_shape=out_shape, compiler_params=_params(),
        grid_spec=pltpu.PrefetchScalarGridSpec(
            num_scalar_prefetch=1, grid=(r // tr,),
            in_specs=[pl.BlockSpec((N_DEV, tr, c), lambda i, me: (0, i, 0)),
                      pl.BlockSpec((None, tr, c), lambda i, me: (me[0], i, 0))] + [rows] * 3,
            out_specs=[rows] * 4),
    )(_my_index().reshape(1).astype(jnp.int32), parts, sent, w, m, v)


def _adam_step(g, w, m, v):
    m_new = ADAM_B1 * m + (1.0 - ADAM_B1) * g
    v_new = ADAM_B2 * v + (1.0 - ADAM_B2) * (g * g)
    m_hat = m_new / (1.0 - ADAM_B1 ** ADAM_STEP)
    v_hat = v_new / (1.0 - ADAM_B2 ** ADAM_STEP)
    return -ADAM_LR * (m_hat / (jnp.sqrt(v_hat) + ADAM_EPS) + ADAM_WD * w), m_new, v_new


SMALL_NAMES = ("norm_mix", "pool_w", "pool_b", "pool_scale", "attn_sinks", "norm_mlp", "norm_final")


def _adamw_small(mlp_all, mix_all, scale_all, nmix_all, w, m, v):
    def body(mlp_ref, mix_ref, scale_ref, nmix_ref, *refs):
        ins, outs = refs[:21], refs[21:]

        def total(ref, rows, lanes=slice(None)):
            g = ref[0, rows, lanes]
            for k in range(1, N_DEV):
                g = g + ref[k, rows, lanes]
            return g

        grads = dict(
            norm_mix=total(nmix_ref, slice(0, 1)), pool_w=total(mix_ref, slice(0, MIX_POOL_B)),
            pool_b=total(mix_ref, slice(MIX_POOL_B, MIX_POOL_B + 4)), pool_scale=total(scale_ref, slice(0, 1)),
            attn_sinks=total(mix_ref, slice(MIX_SINKS, MIX_SINKS + 1)),
            norm_mlp=total(mlp_ref, slice(0, 1)), norm_final=total(mlp_ref, slice(1, 2)))
        for i, name in enumerate(SMALL_NAMES):
            g = grads[name]
            d, m_new, v_new = _adam_step(g, ins[3 * i][...], ins[3 * i + 1][...], ins[3 * i + 2][...])
            for ref, val in zip(outs[4 * i:4 * i + 4], (g, d, m_new, v_new)):
                ref[...] = val
        outs[28][...] = jnp.broadcast_to(total(mlp_ref, slice(2, 3), slice(0, LANES)), (8, LANES))

    operands, out_shape = [], []
    for name in SMALL_NAMES:
        operands += [w[name], m[name], v[name]]
        out_shape += [jax.ShapeDtypeStruct(w[name].shape, F32)] * 4
    out_shape.append(jax.ShapeDtypeStruct((8, LANES), F32))
    outs = pl.pallas_call(body, name="adamw_small", out_shape=out_shape)(
        mlp_all, mix_all, scale_all, nmix_all, *operands)
    return {name: outs[4 * i:4 * i + 4] for i, name in enumerate(SMALL_NAMES)}, outs[28]


def kernel(x, norm_mix, w_in, pool_w, pool_b, pool_scale, attn_sinks, p_pool, p_attn, w_out, norm_mlp, w_up, w_down, norm_final, loss_target, m_norm_mix, m_w_in, m_pool_w, m_pool_b, m_pool_scale, m_attn_sinks, m_p_pool, m_p_attn, m_w_out, m_norm_mlp, m_w_up, m_w_down, m_norm_final, v_norm_mix, v_w_in, v_pool_w, v_pool_b, v_pool_scale, v_attn_sinks, v_p_pool, v_p_attn, v_w_out, v_norm_mlp, v_w_up, v_w_down, v_norm_final):
    xs = x[0]
    tgt = loss_target[0]
    s_len = xs.shape[0]

    p_pool_bf, p_attn_bf, w_out_bf, w_up_bf, w_down_bf = [
        t[0].astype(BF16) for t in (p_pool, p_attn, w_out, w_up, w_down)]
    w_in_bf = w_in[0].T.astype(BF16)
    (w_in_g,) = _all_gather_weights("all_gather_w_in", [w_in_bf])
    ag_rest = _exchange_start(
        "ag_rest_start", [p_pool_bf, p_attn_bf, w_out_bf, w_up_bf, w_down_bf], ("chip",) * 5, w_in_g)

    pool_w_bf = pool_w[0].astype(BF16)
    pool_b_row = pool_b[0].reshape(1, POOL_WIDTH)
    bias_t, sink_row = _attn_constants(attn_sinks[0])

    w_in_t = w_in_g.reshape(IN_WIDTH, D_MODEL)
    u, zp, q, kv, zg = _fwd_in(ag_rest["token"], xs, norm_mix, w_in_t)
    half = (s_len // min(512, s_len)) // 2
    mixer_args = (zp, q, kv, pool_w_bf, pool_b_row, pool_scale, bias_t, sink_row)
    pm, o = _mixers_fwd(*mixer_args, n_tiles=half) if half else (None, None)
    first_level, _ = _exchange_wait("ag_rest_wait", ag_rest, zg if o is None else o)
    relay = _relay_start("ag_relay_start", first_level, zg)
    pm, o = _mixers_fwd(*mixer_args, first_tile=half, earlier=None if o is None else (relay["token"], pm, o))
    p_pool_g, p_attn_g, w_out_g, w_up_g, w_down_g = _relay_wait("ag_relay_wait", relay, o)
    p_pool_f = p_pool_g.transpose(1, 0, 2).reshape(POOL_WIDTH, D_MODEL)
    p_attn_f = p_attn_g.transpose(1, 0, 2).reshape(ATTN_WIDTH, D_MODEL)
    w_out_f = w_out_g.reshape(D_MODEL, D_MODEL)
    w_down_f = w_down_g.reshape(D_FF, D_MODEL)
    mixed, dh1, a, dapre, u2, dh2, small_mlp, dyp, dya, dzg, dpm, do, dh1_bf = _core(
        xs, pm, o, zg, tgt, norm_mlp, norm_final.reshape(1, D_MODEL), p_pool_f, p_attn_f, w_out_f, w_up_g, w_down_f)
    gw_down = _tn_matmul(a, dh2, square_a=True)
    gw_up = _tn_matmul(u2, dapre, col_blocks=N_DEV)
    ex_mlp = _exchange_start(
        "ex_mlp_start", [gw_up, gw_down.reshape(N_DEV, D_FF // N_DEV, D_MODEL)], (True, True), small_mlp)
    dzp, dq, dkv, small_mix, g_pool_scale = _mixers_bwd(
        ex_mlp["token"], zp, q, kv, dpm, do, pool_w_bf, pool_b_row, pool_scale, bias_t, sink_row)
    gw_in = _tn_w_in(u, dzp, dq, dkv, dzg)
    ex_in = _exchange_start(
        "ex_in_start", [gw_in, small_mlp, small_mix, g_pool_scale], (True, False, False, False), dq)
    gw_out = _tn_matmul(mixed, dh1_bf, after=ex_in["token"])
    gp_pool = _tn_matmul(pm, dyp, col_blocks=N_DEV, after=ex_in["token"])
    gp_attn = _tn_matmul(o, dya, col_blocks=N_DEV, after=ex_in["token"])
    ex_proj = _exchange_start(
        "ex_proj_start", [gp_pool, gp_attn, gw_out.reshape(N_DEV, D_MODEL // N_DEV, D_MODEL)], (True,) * 3,
        ex_in["token"])
    dx, g_norm_mix = _in_bwd(ex_proj["token"], dzp, dq, dkv, dzg, w_in_t, xs, dh1, norm_mix)

    big_w = dict(w_in=w_in, p_pool=p_pool, p_attn=p_attn, w_out=w_out, w_up=w_up, w_down=w_down)
    big_m = dict(w_in=m_w_in, p_pool=m_p_pool, p_attn=m_p_attn, w_out=m_w_out, w_up=m_w_up, w_down=m_w_down)
    big_v = dict(w_in=v_w_in, p_pool=v_p_pool, p_attn=v_p_attn, w_out=v_w_out, w_up=v_w_up, w_down=v_w_down)
    res = {}

    def update(names, recvs, sents):
        for name, parts, sent in zip(names, recvs, sents):
            flip = (lambda t: t.T) if name == "w_in" else (lambda t: t)
            outs = _adamw(parts, flip(big_w[name][0]), flip(big_m[name][0]), flip(big_v[name][0]), sent)
            res[name] = [flip(t)[None] for t in outs]

    ag_norm = _exchange_start("ag_norm_mix_start", [g_norm_mix], (False,), dx)
    update(["w_up", "w_down"], *_exchange_wait("ex_mlp_wait", ex_mlp, ag_norm["token"]))
    (r_in, mlp_all, mix_all, scale_all), (s_in, _, _, _) = _exchange_wait("ex_in_wait", ex_in, res["w_down"][0])
    update(["w_in"], [r_in], [s_in])
    (norm_mix_all,), _ = _exchange_wait("ag_norm_mix_wait", ag_norm, res["w_in"][0])

    natural = dict(norm_mix=(1, D_MODEL), pool_w=(MIX_POOL_B, LANES), pool_b=(4, LANES), pool_scale=(1, POOL_WIDTH),
                   attn_sinks=(1, LANES), norm_mlp=(1, D_MODEL), norm_final=(1, D_MODEL))

    def as_2d(t, name):
        if name == "attn_sinks":
            return jnp.pad(t, ((0, 0), (0, LANES - N_HEADS)))
        return t.reshape(natural[name])

    small_w = dict(norm_mix=norm_mix, pool_w=pool_w, pool_b=pool_b, pool_scale=pool_scale, attn_sinks=attn_sinks,
                   norm_mlp=norm_mlp, norm_final=norm_final)
    small_m = dict(norm_mix=m_norm_mix, pool_w=m_pool_w, pool_b=m_pool_b, pool_scale=m_pool_scale,
                   attn_sinks=m_attn_sinks, norm_mlp=m_norm_mlp, norm_final=m_norm_final)
    small_v = dict(norm_mix=v_norm_mix, pool_w=v_pool_w, pool_b=v_pool_b, pool_scale=v_pool_scale,
                   attn_sinks=v_attn_sinks, norm_mlp=v_norm_mlp, norm_final=v_norm_final)
    small_res, loss_all = _adamw_small(
        mlp_all, mix_all, scale_all, norm_mix_all,
        *[{k: as_2d(t, k) for k, t in d.items()} for d in (small_w, small_m, small_v)])
    loss = loss_all[0, 0]
    for name in SMALL_NAMES:
        shape = small_w[name].shape
        res[name] = [(t[:, :N_HEADS] if name == "attn_sinks" else t).reshape(shape) for t in small_res[name]]
    update(["p_pool", "p_attn", "w_out"], *_exchange_wait("ex_proj_wait", ex_proj, loss_all))

    order = ["norm_mix", "w_in", "pool_w", "pool_b", "pool_scale", "attn_sinks", "p_pool", "p_attn", "w_out",
             "norm_mlp", "w_up", "w_down", "norm_final"]
    out = [loss, dx[None]]
    for kind in range(4):
        out += [res[name][kind] for name in order]
    return tuple(out)
```

```python
import functools
import math

import numpy as np
import jax
import jax.numpy as jnp
from jax import lax
from jax.experimental import pallas as pl
from jax.experimental.pallas import tpu as pltpu

F32 = jnp.float32
BF16 = jnp.bfloat16

D_MODEL = 1024
POOL_WIDTH = 512
ATTN_WIDTH = 512
KV_WIDTH = 128
HEAD_DIM = 64
N_HEADS = 8
N_KV_HEADS = 2
GROUP = 4
BLOCK = 128
POOL_WINDOWS = (2, 4, 8, 16)
POOL_GROUP_DIM = 128
POOL_HALO = 16
D_FF = 4096
FF_CHUNK = 1024
IN_WIDTH = 3328
RMS_EPS = 1e-5
NEG_INF = -1e30
ATTN_SCALE = 1.0 / math.sqrt(HEAD_DIM)
N_DEV = 8

ADAM_LR = 0.001
ADAM_B1 = 0.9
ADAM_B2 = 0.999
ADAM_EPS = 1e-08
ADAM_WD = 0.01
ADAM_STEP = 10

LANES = 128
VMEM_LIMIT_BYTES = 56 * 1024 * 1024
MESH = pl.DeviceIdType.MESH


def _params(n_grid_axes=1):
    return pltpu.CompilerParams(
        dimension_semantics=("arbitrary",) * n_grid_axes, vmem_limit_bytes=VMEM_LIMIT_BYTES)


def _dot(a, b):
    return jnp.dot(a, b, preferred_element_type=F32)


def _dot_nt(a, b):
    return lax.dot_general(a, b, (((1,), (1,)), ((), ())), preferred_element_type=F32)


def _dot_tn(a, b):
    return lax.dot_general(a, b, (((0,), (0,)), ((), ())), preferred_element_type=F32)


ANY = pl.BlockSpec(memory_space=pl.ANY)


def _rows(tm, n):
    return pl.BlockSpec((tm, n), lambda i: (i, 0))


def _whole(shape):
    zeros = (0,) * len(shape)
    return pl.BlockSpec(shape, lambda i: zeros)


def _rms_fwd(h, g):
    r = lax.rsqrt(jnp.mean(h * h, axis=-1, keepdims=True) + RMS_EPS)
    xh = h * r
    return r, xh, xh * g


def _rms_bwd(dy, xh, r, g):
    dxh = dy * g
    dh = r * (dxh - xh * jnp.mean(dxh * xh, axis=-1, keepdims=True))
    return dh, jnp.sum(dy * xh, axis=0, keepdims=True)


def _fwd_in(after, x, g_mix, w_in_t):
    s_len = x.shape[0]
    tm = min(512, s_len)

    def body(after_ref, x_ref, g_ref, w_ref, u_ref, zp_ref, q_ref, kv_ref, zg_ref):
        _, _, u = _rms_fwd(x_ref[...], g_ref[...])
        u = u.astype(BF16)
        u_ref[...] = u
        zp_ref[...] = _dot_nt(u, w_ref[0:512, :]).astype(BF16)
        q_ref[...] = _dot_nt(u, w_ref[512:1024, :]).astype(BF16)
        kv_ref[...] = _dot_nt(u, w_ref[1024:1280, :]).astype(BF16)
        zg_ref[...] = _dot_nt(u, w_ref[1280:3328, :]).astype(BF16)

    return pl.pallas_call(
        body, name="fwd_in", grid=(s_len // tm,),
        in_specs=[ANY, _rows(tm, D_MODEL), _whole((1, D_MODEL)),
                  pl.BlockSpec((IN_WIDTH, D_MODEL), lambda i: (0, 0), pipeline_mode=pl.Buffered(1))],
        out_specs=[_rows(tm, D_MODEL), _rows(tm, 512), _rows(tm, 512), _rows(tm, 256), _rows(tm, 2048)],
        out_shape=[jax.ShapeDtypeStruct((s_len, n), BF16) for n in (D_MODEL, 512, 512, 256, 2048)],
        compiler_params=_params(),
    )(after, x, g_mix, w_in_t)


def _attn_constants(sinks):
    r = np.arange(BLOCK)[:, None]
    qi = np.arange(BLOCK)[None, :]
    dist = np.where(r <= qi, qi - r, BLOCK + qi - r).astype(np.float32)
    slopes = np.array([2.0 ** (-8.0 * (h + 1) / N_HEADS) for h in range(N_HEADS)], dtype=np.float32)
    bias = (-slopes[:, None, None] * dist[None]).reshape(N_KV_HEADS, GROUP, BLOCK, BLOCK)
    bias = np.ascontiguousarray(bias.transpose(0, 2, 1, 3)).reshape(N_KV_HEADS, BLOCK, GROUP * BLOCK)
    sink_row = jnp.repeat(sinks.astype(F32).reshape(N_KV_HEADS, GROUP), BLOCK, axis=1)[:, None, :]
    return jnp.asarray(bias.astype(np.float32)), sink_row


def _own_block_mask():
    shape = (BLOCK, GROUP * BLOCK)
    r = lax.broadcasted_iota(jnp.int32, shape, 0)
    qi = lax.broadcasted_iota(jnp.int32, shape, 1) & (BLOCK - 1)
    return r <= qi


def _pack_keys(t, own):
    return jnp.where(own, t[BLOCK:], t[:BLOCK])


def _unpack_keys(t, own):
    zero = jnp.zeros_like(t)
    return jnp.concatenate([jnp.where(own, zero, t), jnp.where(own, t, zero)], axis=0)


def _left_half(shape):
    return lax.broadcasted_iota(jnp.int32, shape, 1) < HEAD_DIM


def _dup_halves(slab):
    swapped = pltpu.roll(slab, HEAD_DIM, 1)
    left = _left_half(slab.shape)
    return jnp.where(left, slab, swapped), jnp.where(left, swapped, slab)


def _fill_kv_slabs(kvh_ref, kv_ref, ka_ref, vd_ref):
    for rows, src in ((slice(0, BLOCK), kvh_ref), (slice(BLOCK, None), kv_ref)):
        kvf = src[...].astype(F32)
        for ref, lanes in ((ka_ref, slice(0, KV_WIDTH)), (vd_ref, slice(KV_WIDTH, 2 * KV_WIDTH))):
            d0, d1 = _dup_halves(kvf[:, lanes])
            ref[0, rows, :] = d0.astype(BF16)
            ref[1, rows, :] = d1.astype(BF16)


def _stack_pairs(a, h):
    pieces = []
    for j in range(2):
        pair = a[:, h * 256 + j * LANES:h * 256 + (j + 1) * LANES]
        left = _left_half(pair.shape)
        zero = jnp.zeros_like(pair)
        pieces += [jnp.where(left, pair, zero), jnp.where(left, zero, pair)]
    return jnp.concatenate(pieces, axis=0)


def _attn_probs(kk, q_st, bias_p, sink_row, own, first):
    s = _pack_keys(_dot_nt(kk, q_st), own) * ATTN_SCALE + bias_p
    if first is not None:
        s = jnp.where(jnp.logical_and(first, jnp.logical_not(own)), NEG_INF, s)
    m = jnp.maximum(jnp.max(s, axis=0, keepdims=True), sink_row)
    p = jnp.exp(s - m)
    es = jnp.exp(sink_row - m)
    inv = 1.0 / (jnp.sum(p, axis=0, keepdims=True) + es)
    return p * inv, es * inv


def _pool_d(ext, cur, g, row0):
    w = POOL_WINDOWS[g]
    acc = ext
    k = 1
    while k < w:
        acc = acc + pltpu.roll(acc, k, 0)
        k *= 2
    return _window_mean(acc[POOL_HALO:, :], w, row0) - cur


def _window_mean(total, w, row0):
    t = row0 + lax.broadcasted_iota(jnp.int32, (POOL_HALO, total.shape[1]), 0)
    head = total[:POOL_HALO] / jnp.minimum(t + 1, w).astype(F32)
    return jnp.concatenate([head, total[POOL_HALO:] * (1.0 / w)], axis=0)


def _mixers_fwd(zp, q, kv, pool_w, pool_b, pool_scale, bias_t, sink_row, first_tile=0, n_tiles=None, earlier=None):
    s_len = zp.shape[0]
    tq = min(512, s_len)
    nb = tq // BLOCK
    n_tiles = s_len // tq - first_tile if n_tiles is None else n_tiles
    extra = [] if earlier is None else list(earlier)

    def body(zp_ref, zph_ref, q_ref, kv_ref, kvh_ref, pw_ref, pb_ref, ps_ref, bias_ref, sink_ref, *rest):
        pm_ref, o_ref, ka_ref, vd_ref = rest[len(extra):]
        i = pl.program_id(0) + first_tile
        cur = zp_ref[...].astype(F32)
        halo = zph_ref[...].astype(F32) * (i > 0).astype(F32)
        ext = jnp.concatenate([halo, cur], axis=0)
        for g in range(4):
            sl = slice(g * POOL_GROUP_DIM, (g + 1) * POOL_GROUP_DIM)
            d = _pool_d(ext[:, sl], cur[:, sl], g, i * tq)
            y = _dot(d.astype(BF16), pw_ref[g]) + pb_ref[:, sl]
            pm_ref[:, sl] = (y * ps_ref[:, sl]).astype(BF16)
        _fill_kv_slabs(kvh_ref, kv_ref, ka_ref, vd_ref)
        own = _own_block_mask()
        for b in range(nb):
            rq = slice(b * BLOCK, (b + 1) * BLOCK)
            rk = slice(b * BLOCK, (b + 2) * BLOCK)
            qb = q_ref[rq, :]
            for h in range(N_KV_HEADS):
                pn, _ = _attn_probs(ka_ref[h, rk, :], _stack_pairs(qb, h), bias_ref[h], sink_ref[h], own,
                                    (i == 0) if b == 0 else None)
                pn = _unpack_keys(pn, own).astype(BF16)
                vd = vd_ref[h, rk, :]
                left = _left_half(vd.shape)
                zero = jnp.zeros_like(vd)
                va, vb = jnp.where(left, vd, zero), jnp.where(left, zero, vd)
                for j in range(2):
                    o_pair = (_dot_tn(pn[:, (2 * j) * BLOCK:(2 * j + 1) * BLOCK], va)
                              + _dot_tn(pn[:, (2 * j + 1) * BLOCK:(2 * j + 2) * BLOCK], vb))
                    o_ref[rq, h * 256 + j * LANES:h * 256 + (j + 1) * LANES] = o_pair.astype(BF16)

    def tile(n):
        return pl.BlockSpec((tq, n), lambda i: (i + first_tile, 0))

    halo_pool = pl.BlockSpec(
        (POOL_HALO, 512), lambda i: (jnp.maximum((i + first_tile) * (tq // POOL_HALO) - 1, 0), 0))
    halo_kv = pl.BlockSpec((BLOCK, 256), lambda i: (jnp.maximum((i + first_tile) * nb - 1, 0), 0))
    return pl.pallas_call(
        body, name="mixers_fwd", grid=(n_tiles,),
        in_specs=[tile(512), halo_pool, tile(512), tile(256), halo_kv,
                  _whole((4, 128, 128)), _whole((1, 512)), _whole((1, 512)),
                  _whole((N_KV_HEADS, BLOCK, GROUP * BLOCK)), _whole((N_KV_HEADS, 1, GROUP * BLOCK))]
        + [ANY] * len(extra),
        out_specs=[tile(512), tile(512)],
        out_shape=[jax.ShapeDtypeStruct((s_len, 512), BF16)] * 2,
        input_output_aliases={11: 0, 12: 1} if extra else {},
        scratch_shapes=[pltpu.VMEM((N_KV_HEADS, tq + BLOCK, LANES), BF16)] * 2,
        compiler_params=_params(),
    )(zp, zp, q, kv, kv, pool_w, pool_b, pool_scale, bias_t, sink_row, *extra)


def _gated_mix(pm, o, zg, pp_ref, pa_ref):
    yp = _dot(pm, pp_ref[...])
    ya = _dot(o, pa_ref[...])
    gp = jax.nn.sigmoid(zg[:, :D_MODEL].astype(F32))
    ga = jax.nn.sigmoid(zg[:, D_MODEL:].astype(F32))
    return yp, ya, gp, ga


def _core(x, pm, o, zg, tgt, g_mlp, g_fin, p_pool, p_attn, w_out, w_up_blocks, w_down):
    s_len = x.shape[0]
    tm = min(256, s_len)
    n_chunks = D_FF // FF_CHUNK
    up_block = D_FF // N_DEV
    per_chunk = FF_CHUNK // up_block

    def body(x_ref, pm_ref, o_ref, zg_ref, tgt_ref, gm_ref, gf_ref, pp_ref, pa_ref, wo_ref, wu_ref, wd_ref,
             mixed_ref, dh1_ref, a_ref, dap_ref, u2_ref, dh2_ref, small_ref,
             dyp_ref, dya_ref, dzg_ref, dpm_ref, do_ref, dh1b_ref):
        i = pl.program_id(0)

        @pl.when(i == 0)
        def _():
            small_ref[...] = jnp.zeros_like(small_ref)

        yp, ya, gp, ga = _gated_mix(pm_ref[...], o_ref[...], zg_ref[...], pp_ref, pa_ref)
        mixed = (gp * yp + ga * ya).astype(BF16)
        mixed_ref[...] = mixed
        h1 = x_ref[...] + _dot(mixed, wo_ref[...])
        r2, xh2, u2 = _rms_fwd(h1, gm_ref[...])
        u2 = u2.astype(BF16)
        u2_ref[...] = u2
        acc = jnp.zeros((tm, D_MODEL), F32)
        for c in range(n_chunks):
            cs = slice(c * FF_CHUNK, (c + 1) * FF_CHUNK)
            a = jnp.concatenate([_dot(u2, wu_ref[per_chunk * c + j]) for j in range(per_chunk)], axis=1)
            a = jnp.maximum(a, 0.0)
            a_ref[:, cs] = a.astype(BF16)
            acc = acc + _dot((a * a).astype(BF16), wd_ref[cs, :])
        h2 = h1 + acc
        r3, xh3, y = _rms_fwd(h2, gf_ref[...])
        diff = y - tgt_ref[...]
        small_ref[2:3, :] += 0.5 * jnp.sum(jnp.mean(diff * diff, axis=-1, keepdims=True))
        dy = diff * (1.0 / D_MODEL)
        dh2, dgf = _rms_bwd(dy, xh3, r3, gf_ref[...])
        small_ref[1:2, :] += dgf
        dh2_bf = dh2.astype(BF16)
        dh2_ref[...] = dh2_bf
        du2 = jnp.zeros((tm, D_MODEL), F32)
        for c in range(n_chunks):
            cs = slice(c * FF_CHUNK, (c + 1) * FF_CHUNK)
            ds = _dot_nt(dh2_bf, wd_ref[cs, :])
            dap = (ds * (2.0 * a_ref[:, cs].astype(F32))).astype(BF16)
            dap_ref[:, cs] = dap
            for j in range(per_chunk):
                du2 = du2 + _dot_nt(dap[:, j * up_block:(j + 1) * up_block], wu_ref[per_chunk * c + j])
        dh1n, dgm = _rms_bwd(du2, xh2, r2, gm_ref[...])
        small_ref[0:1, :] += dgm
        dh1 = dh2 + dh1n
        dh1_ref[...] = dh1
        dh1_bf = dh1.astype(BF16)
        dh1b_ref[...] = dh1_bf
        dm = _dot_nt(dh1_bf, wo_ref[...])
        dyp = (dm * gp).astype(BF16)
        dya = (dm * ga).astype(BF16)
        dyp_ref[...] = dyp
        dya_ref[...] = dya
        dzg_ref[:, :D_MODEL] = (dm * yp * (gp * (1.0 - gp))).astype(BF16)
        dzg_ref[:, D_MODEL:] = (dm * ya * (ga * (1.0 - ga))).astype(BF16)
        dpm_ref[...] = _dot_nt(dyp, pp_ref[...]).astype(BF16)
        do_ref[...] = _dot_nt(dya, pa_ref[...]).astype(BF16)

    def fixed(shape):
        return pl.BlockSpec(shape, lambda i: (0,) * len(shape), pipeline_mode=pl.Buffered(1))

    widths_dtypes = ((D_MODEL, BF16), (D_MODEL, F32), (D_FF, BF16), (D_FF, BF16), (D_MODEL, BF16), (D_MODEL, BF16))
    back = ((D_MODEL, BF16), (D_MODEL, BF16), (2048, BF16), (512, BF16), (512, BF16), (D_MODEL, BF16))
    return pl.pallas_call(
        body, name="core", grid=(s_len // tm,),
        in_specs=[_rows(tm, D_MODEL), _rows(tm, 512), _rows(tm, 512), _rows(tm, 2048), _rows(tm, D_MODEL),
                  _whole((1, D_MODEL)), _whole((1, D_MODEL)),
                  fixed((512, D_MODEL)), fixed((512, D_MODEL)), fixed((D_MODEL, D_MODEL)),
                  fixed((N_DEV, D_MODEL, up_block)), fixed((D_FF, D_MODEL))],
        out_specs=[_rows(tm, n) for n, _ in widths_dtypes] + [_whole((8, D_MODEL))] + [_rows(tm, n) for n, _ in back],
        out_shape=[jax.ShapeDtypeStruct((s_len, n), d) for n, d in widths_dtypes]
        + [jax.ShapeDtypeStruct((8, D_MODEL), F32)] + [jax.ShapeDtypeStruct((s_len, n), d) for n, d in back],
        compiler_params=_params(),
    )(x, pm, o, zg, tgt, g_mlp, g_fin, p_pool, p_attn, w_out, w_up_blocks, w_down)


def _tn_matmul(a, b, square_a=False, col_blocks=None, after=None):
    s_len, ka = a.shape
    nb = b.shape[1]
    tt = min(2048, s_len)
    tk = min(1024, ka)
    tn = min(1024, nb)
    n_t = s_len // tt
    if col_blocks is None:
        out_spec = pl.BlockSpec((tk, tn), lambda k, j, t: (k, j))
        out_shape = jax.ShapeDtypeStruct((ka, nb), BF16)
    else:
        width = nb // col_blocks
        per_tile = tn // width
        out_spec = pl.BlockSpec((per_tile, tk, width), lambda k, j, t: (j, k, 0))
        out_shape = jax.ShapeDtypeStruct((col_blocks, ka, width), BF16)

    extra = [] if after is None else [after]

    def body(a_ref, b_ref, *rest):
        o_ref, acc_ref = rest[len(extra):]
        t = pl.program_id(2)

        @pl.when(t == 0)
        def _():
            acc_ref[...] = jnp.zeros_like(acc_ref)

        av = a_ref[...]
        if square_a:
            av = av * av
        acc_ref[...] += _dot_tn(av.astype(BF16), b_ref[...].astype(BF16))

        @pl.when(t == n_t - 1)
        def _():
            if col_blocks is None:
                o_ref[...] = acc_ref[...].astype(o_ref.dtype)
            else:
                for blk in range(per_tile):
                    o_ref[blk] = acc_ref[:, blk * width:(blk + 1) * width].astype(o_ref.dtype)

    return pl.pallas_call(
        body, name="tn_matmul", grid=(ka // tk, nb // tn, n_t),
        in_specs=[pl.BlockSpec((tt, tk), lambda k, j, t: (t, k)), pl.BlockSpec((tt, tn), lambda k, j, t: (t, j))]
        + [ANY] * len(extra),
        out_specs=out_spec, out_shape=out_shape,
        scratch_shapes=[pltpu.VMEM((tk, tn), F32)],
        compiler_params=_params(3),
    )(a, b, *extra)


def _tn_w_in(u, dzp, dq, dkv, dzg):
    s_len = u.shape[0]
    tt = min(1024, s_len)
    n_t = s_len // tt
    width = IN_WIDTH // N_DEV
    pieces = ((0, 512), (512, 1024), (1024, 1280), (1280, IN_WIDTH))

    def body(u_ref, dzp_ref, dq_ref, dkv_ref, dzg_ref, o_ref, acc_ref):
        t = pl.program_id(0)

        @pl.when(t == 0)
        def _():
            acc_ref[...] = jnp.zeros_like(acc_ref)

        uv = u_ref[...]
        for (c0, c1), ref in zip(pieces, (dzp_ref, dq_ref, dkv_ref, dzg_ref)):
            acc_ref[c0:c1, :] += _dot_tn(ref[...], uv)

        @pl.when(t == n_t - 1)
        def _():
            for j in range(N_DEV):
                o_ref[j] = acc_ref[j * width:(j + 1) * width, :].astype(BF16)

    return pl.pallas_call(
        body, name="tn_w_in", grid=(n_t,),
        in_specs=[_rows(tt, D_MODEL)] + [_rows(tt, c1 - c0) for c0, c1 in pieces],
        out_specs=_whole((N_DEV, width, D_MODEL)),
        out_shape=jax.ShapeDtypeStruct((N_DEV, width, D_MODEL), BF16),
        scratch_shapes=[pltpu.VMEM((IN_WIDTH, D_MODEL), F32)],
        compiler_params=_params(),
    )(u, dzp, dq, dkv, dzg)


MIX_POOL_B = 4 * POOL_GROUP_DIM
MIX_SINKS = MIX_POOL_B + 8
MIX_ROWS = MIX_SINKS + 8


def _mixers_bwd(after, zp, q, kv, dpm, do, pool_w, pool_b, pool_scale, bias_t, sink_row):
    s_len = zp.shape[0]
    tq = min(512, s_len)
    nb = tq // BLOCK
    n_steps = s_len // tq

    def body(after_ref, zp_ref, zph_ref, q_ref, kv_ref, kvh_ref, dpm_ref, dpmh_ref, do_ref, pw_ref, pb_ref, ps_ref,
             bias_ref, sink_ref, dzp_ref, dq_ref, dkv_ref, small_ref, dps_ref,
             ka_ref, vd_ref, dsk_acc, dkv_acc):
        i = pl.program_id(0)

        @pl.when(i == 0)
        def _():
            dkv_acc[...] = jnp.zeros_like(dkv_acc)
            small_ref[...] = jnp.zeros_like(small_ref)
            dps_ref[...] = jnp.zeros_like(dps_ref)
            dsk_acc[...] = jnp.zeros_like(dsk_acc)

        cur = zp_ref[...].astype(F32)
        halo = zph_ref[...].astype(F32) * (i > 0).astype(F32)
        ext = jnp.concatenate([halo, cur], axis=0)
        dpm_next = dpmh_ref[...].astype(F32) * (i < n_steps - 1).astype(F32)
        dpm_ext = jnp.concatenate([dpm_ref[...].astype(F32), dpm_next], axis=0)
        n_ext = tq + POOL_HALO
        for g in range(4):
            sl = slice(g * POOL_GROUP_DIM, (g + 1) * POOL_GROUP_DIM)
            w = POOL_WINDOWS[g]
            d = _pool_d(ext[:, sl], cur[:, sl], g, i * tq).astype(BF16)
            y_lin = _dot(d, pw_ref[g]) + pb_ref[:, sl]
            dps_ref[:, sl] += jnp.sum(dpm_ext[:tq, sl] * y_lin, axis=0, keepdims=True)
            dyl_ext = dpm_ext[:, sl] * ps_ref[:, sl]
            small_ref[MIX_POOL_B + g:MIX_POOL_B + g + 1, :] += jnp.sum(dyl_ext[:tq], axis=0, keepdims=True)
            dyl_bf = dyl_ext.astype(BF16)
            small_ref[g * POOL_GROUP_DIM:(g + 1) * POOL_GROUP_DIM, :] += _dot_tn(d, dyl_bf[:tq])
            dd = _dot_nt(dyl_bf, pw_ref[g])
            e = _window_mean(dd, w, i * tq)
            acc = e
            k = 1
            while k < w:
                acc = acc + pltpu.roll(acc, n_ext - k, 0)
                k *= 2
            dzp_ref[:, sl] = (acc[:tq] - dd[:tq]).astype(BF16)

        _fill_kv_slabs(kvh_ref, kv_ref, ka_ref, vd_ref)

        def fold(dup):
            return dup + pltpu.roll(dup, HEAD_DIM, 1)

        own = _own_block_mask()
        for b in range(nb):
            rq = slice(b * BLOCK, (b + 1) * BLOCK)
            rk = slice(b * BLOCK, (b + 2) * BLOCK)
            qb = q_ref[rq, :]
            dob = do_ref[rq, :]
            dk_dup, dv_dup = [], []
            for h in range(N_KV_HEADS):
                kk = ka_ref[h, rk, :]
                q_st = _stack_pairs(qb, h)
                do_st = _stack_pairs(dob, h)
                pn, psink = _attn_probs(kk, q_st, bias_ref[h], sink_ref[h], own, (i == 0) if b == 0 else None)
                dp = _pack_keys(_dot_nt(vd_ref[h, rk, :], do_st), own)
                delta = jnp.sum(pn * dp, axis=0, keepdims=True)
                dsk_acc[h] += -psink * delta
                ds = _unpack_keys((pn * (dp - delta)) * ATTN_SCALE, own).astype(BF16)
                pn = _unpack_keys(pn, own)
                dq_st = _dot_tn(ds, kk)
                for j in range(2):
                    left = _left_half((BLOCK, LANES))
                    dq_pair = jnp.where(left, dq_st[(2 * j) * BLOCK:(2 * j + 1) * BLOCK],
                                        dq_st[(2 * j + 1) * BLOCK:(2 * j + 2) * BLOCK])
                    dq_ref[rq, h * 256 + j * LANES:h * 256 + (j + 1) * LANES] = dq_pair.astype(BF16)
                dk_dup.append(fold(_dot(ds, q_st)))
                dv_dup.append(fold(_dot(pn.astype(BF16), do_st)))
            left = _left_half((2 * BLOCK, LANES))
            dkv_blk = jnp.concatenate([jnp.where(left, dk_dup[0], dk_dup[1]),
                                       jnp.where(left, dv_dup[0], dv_dup[1])], axis=1)
            g0 = pl.multiple_of(i * tq + b * BLOCK, BLOCK)
            dkv_acc[pl.ds(g0, 2 * BLOCK), :] += dkv_blk

        @pl.when(i == n_steps - 1)
        def _():
            dkv_ref[...] = dkv_acc[BLOCK:, :].astype(BF16)
            lane = lax.broadcasted_iota(jnp.int32, (1, LANES), 1)
            row = jnp.zeros((1, LANES), F32)
            for h in range(N_KV_HEADS):
                for g in range(GROUP):
                    tot = jnp.sum(dsk_acc[h, :, g * BLOCK:(g + 1) * BLOCK], axis=1, keepdims=True)
                    row = jnp.where(lane == GROUP * h + g, tot, row)
            small_ref[MIX_SINKS:MIX_SINKS + 1, :] = row

    blocks_per_tile = tq // POOL_HALO
    last_halo = s_len // POOL_HALO - 1
    halo_prev = pl.BlockSpec((POOL_HALO, 512), lambda i: (jnp.maximum(i * blocks_per_tile - 1, 0), 0))
    halo_next = pl.BlockSpec((POOL_HALO, 512), lambda i: (jnp.minimum((i + 1) * blocks_per_tile, last_halo), 0))
    halo_kv = pl.BlockSpec((BLOCK, 256), lambda i: (jnp.maximum(i * nb - 1, 0), 0))
    return pl.pallas_call(
        body, name="mixers_bwd", grid=(n_steps,),
        in_specs=[ANY, _rows(tq, 512), halo_prev, _rows(tq, 512), _rows(tq, 256), halo_kv,
                  _rows(tq, 512), halo_next, _rows(tq, 512),
                  _whole((4, 128, 128)), _whole((1, 512)), _whole((1, 512)),
                  _whole((N_KV_HEADS, BLOCK, GROUP * BLOCK)), _whole((N_KV_HEADS, 1, GROUP * BLOCK))],
        out_specs=[_rows(tq, 512), _rows(tq, 512), _whole((s_len, 256)),
                   _whole((MIX_ROWS, LANES)), _whole((1, 512))],
        out_shape=[jax.ShapeDtypeStruct((s_len, 512), BF16), jax.ShapeDtypeStruct((s_len, 512), BF16),
                   jax.ShapeDtypeStruct((s_len, 256), BF16), jax.ShapeDtypeStruct((MIX_ROWS, LANES), F32),
                   jax.ShapeDtypeStruct((1, 512), F32)],
        scratch_shapes=[pltpu.VMEM((N_KV_HEADS, tq + BLOCK, LANES), BF16)] * 2
        + [pltpu.VMEM((N_KV_HEADS, 1, GROUP * BLOCK), F32), pltpu.VMEM((s_len + BLOCK, 256), F32)],
        compiler_params=_params(),
    )(after, zp, zp, q, kv, kv, dpm, dpm, do, pool_w, pool_b, pool_scale, bias_t, sink_row)


def _in_bwd(after, dzp, dq, dkv, dzg, w_in_t, x, dh1, g_mix):
    s_len = x.shape[0]
    tm = min(512, s_len)

    def body(after_ref, dzp_ref, dq_ref, dkv_ref, dzg_ref, w_ref, x_ref, dh1_ref, g_ref, dx_ref, dg_ref):
        i = pl.program_id(0)

        @pl.when(i == 0)
        def _():
            dg_ref[...] = jnp.zeros_like(dg_ref)

        du = _dot(dzp_ref[...], w_ref[0:512, :])
        du = du + _dot(dq_ref[...], w_ref[512:1024, :])
        du = du + _dot(dkv_ref[...], w_ref[1024:1280, :])
        du = du + _dot(dzg_ref[...], w_ref[1280:3328, :])
        r, xh, _ = _rms_fwd(x_ref[...], g_ref[...])
        dxn, dg = _rms_bwd(du, xh, r, g_ref[...])
        dg_ref[...] += dg
        dx_ref[...] = dh1_ref[...] + dxn

    return pl.pallas_call(
        body, name="in_bwd", grid=(s_len // tm,),
        in_specs=[ANY, _rows(tm, 512), _rows(tm, 512), _rows(tm, 256), _rows(tm, 2048), _whole((IN_WIDTH, D_MODEL)),
                  _rows(tm, D_MODEL), _rows(tm, D_MODEL), _whole((1, D_MODEL))],
        out_specs=[_rows(tm, D_MODEL), _whole((1, D_MODEL))],
        out_shape=[jax.ShapeDtypeStruct((s_len, D_MODEL), F32), jax.ShapeDtypeStruct((1, D_MODEL), F32)],
        compiler_params=_params(),
    )(after, dzp, dq, dkv, dzg, w_in_t, x, dh1, g_mix)


def _all_gather_weights(name, shards, after=None):
    n = len(shards)
    extra = [] if after is None else [after]
    n_extra = len(extra)

    def body(*refs):
        ins, outs = refs[:n], refs[n + n_extra:2 * n + n_extra]
        send_sems, recv_sems, local_sems = refs[2 * n + n_extra:]
        x, y, c = lax.axis_index("x"), lax.axis_index("y"), lax.axis_index("c")
        me, sibling = (x, y, c), (x, y, 1 - c)
        chips = [(1 - x, y), (x, 1 - y), (1 - x, 1 - y)]

        def slot(a, px, py, pc):
            return outs[a].at[4 * px + 2 * py + pc]

        def copy(a, k, block, to, src=None):
            return pltpu.make_async_remote_copy(
                src_ref=slot(a, *block) if src is None else src, dst_ref=slot(a, *block),
                send_sem=send_sems.at[a, k], recv_sem=recv_sems.at[a, k], device_id=to, device_id_type=MESH)

        mine = [pltpu.make_async_copy(ins[a], slot(a, *me), local_sems.at[a]) for a in range(n)]
        for cp in mine:
            cp.start()
        first = []
        for a in range(n):
            first.append(copy(a, 0, me, sibling, src=ins[a]))
            first += [copy(a, 1 + j, me, (*chip, c), src=ins[a]) for j, chip in enumerate(chips)]
        for cp in first:
            cp.start()
        passed = []
        for a in range(n):
            for j, chip in enumerate(chips):
                copy(a, 1 + j, (*chip, c), me).wait_recv()
                cp = copy(a, 4 + j, (*chip, c), sibling)
                cp.start()
                passed.append(cp)
        for a in range(n):
            copy(a, 0, sibling, me).wait_recv()
            for j, chip in enumerate(chips):
                copy(a, 4 + j, (*chip, 1 - c), me).wait_recv()
        for cp in first + passed:
            cp.wait_send()
        for cp in mine:
            cp.wait()

    return pl.pallas_call(
        body, name=name,
        in_specs=[ANY] * (n + n_extra), out_specs=[ANY] * n,
        out_shape=[jax.ShapeDtypeStruct((N_DEV,) + s.shape, s.dtype) for s in shards],
        scratch_shapes=[pltpu.SemaphoreType.DMA((n, 7)), pltpu.SemaphoreType.DMA((n, 7)), pltpu.SemaphoreType.DMA((n,))],
    )(*shards, *extra)


HBM_SPEC = pl.BlockSpec(memory_space=pltpu.HBM)
SEM_SPEC = pl.BlockSpec(memory_space=pltpu.SEMAPHORE)
DATAFLOW = pltpu.SideEffectType.DATAFLOW_SIDE_EFFECTING
N_PEERS = N_DEV - 1


CHIP_PEERS = (1, 2, 4, 6)
RELAYED = (2, 4, 6)


def _peer_copies(srcs, lands, scatter, send_sems, recv_sems):
    x, y, c = lax.axis_index("x"), lax.axis_index("y"), lax.axis_index("c")
    me_idx = 4 * x + 2 * y + c
    copies = []
    for k in range(1, N_DEV):
        px = 1 - x if (k >> 2) & 1 else x
        py = 1 - y if (k >> 1) & 1 else y
        pc = 1 - c if k & 1 else c
        p_idx = 4 * px + 2 * py + pc
        for a in range(len(srcs)):
            if scatter[a] == "chip" and k not in CHIP_PEERS:
                continue
            src = srcs[a].at[p_idx] if scatter[a] is True else srcs[a]
            dst = lands[a].at[k] if scatter[a] is True else lands[a].at[me_idx]
            copies.append(pltpu.make_async_remote_copy(
                src_ref=src, dst_ref=dst, send_sem=send_sems.at[a * N_PEERS + k - 1],
                recv_sem=recv_sems.at[a * N_PEERS + k - 1],
                device_id=(px, py, pc), device_id_type=MESH))
    return copies


def _exchange_start(name, srcs, scatter, after):
    n = len(srcs)
    lands = [lax.empty(s.shape if sc is True else (N_DEV,) + s.shape, s.dtype) for s, sc in zip(srcs, scatter)]

    def body(*refs):
        src_refs, land_refs = refs[:n], refs[n:2 * n]
        send_sems, recv_sems = refs[2 * n + 1], refs[2 * n + 2]
        token = refs[4 * n + 3]
        for cp in _peer_copies(src_refs, land_refs, scatter, send_sems, recv_sems):
            cp.start()
        token[...] = jnp.zeros_like(token)

    hbm = lambda t: pltpu.HBM(t.shape, t.dtype)
    outs = pl.pallas_call(
        body, name=name,
        out_shape=[pltpu.SemaphoreType.DMA((n * N_PEERS,)), pltpu.SemaphoreType.DMA((n * N_PEERS,))]
        + [hbm(t) for t in srcs] + [hbm(t) for t in lands] + [jax.ShapeDtypeStruct((8, LANES), F32)],
        in_specs=[HBM_SPEC] * (2 * n) + [ANY],
        out_specs=[SEM_SPEC, SEM_SPEC] + [HBM_SPEC] * (2 * n) + [pl.BlockSpec(memory_space=pltpu.VMEM)],
        input_output_aliases={i: 2 + i for i in range(2 * n)},
        compiler_params=pltpu.CompilerParams(has_side_effects=DATAFLOW),
    )(*[pltpu.with_memory_space_constraint(t, pltpu.HBM) for t in list(srcs) + lands], after)
    return dict(n=n, scatter=scatter, send_sems=outs[0], recv_sems=outs[1], srcs=outs[2:2 + n],
                lands=outs[2 + n:2 + 2 * n], token=outs[2 + 2 * n])


def _exchange_wait(name, handle, after):
    n, scatter = handle["n"], handle["scatter"]

    def body(*refs):
        src_refs, land_refs = refs[:n], refs[n:2 * n]
        send_sems, recv_sems = refs[2 * n], refs[2 * n + 1]
        for cp in _peer_copies(src_refs, land_refs, scatter, send_sems, recv_sems):
            cp.wait_send()
            cp.wait_recv()

    both = list(handle["srcs"]) + list(handle["lands"])
    outs = pl.pallas_call(
        body, name=name,
        out_shape=[pltpu.HBM(t.shape, t.dtype) for t in both],
        in_specs=[HBM_SPEC] * (2 * n) + [SEM_SPEC, SEM_SPEC, ANY],
        out_specs=[HBM_SPEC] * (2 * n),
        input_output_aliases={i: i for i in range(2 * n)},
        compiler_params=pltpu.CompilerParams(has_side_effects=DATAFLOW),
    )(*both, handle["send_sems"], handle["recv_sems"], after)
    me_idx = _my_index()
    lands = [lax.dynamic_update_index_in_dim(land, src, me_idx, 0) if sc is False else land
             for land, src, sc in zip(outs[n:], outs[:n], scatter)]
    return lands, outs[:n]


def _my_index():
    return 4 * lax.axis_index("x") + 2 * lax.axis_index("y") + lax.axis_index("c")


N_RELAYED = len(RELAYED) + 1


def _relay_copies(bufs, send_sems, recv_sems):
    x, y, c = lax.axis_index("x"), lax.axis_index("y"), lax.axis_index("c")
    slots = []
    for k in RELAYED:
        px = 1 - x if (k >> 2) & 1 else x
        py = 1 - y if (k >> 1) & 1 else y
        slots.append(4 * px + 2 * py + c)
    slots.append(4 * x + 2 * y + (1 - c))
    copies = []
    for j, slot in enumerate(slots):
        for a, buf in enumerate(bufs):
            copies.append(pltpu.make_async_remote_copy(
                src_ref=buf.at[slot], dst_ref=buf.at[slot], send_sem=send_sems.at[a * N_RELAYED + j],
                recv_sem=recv_sems.at[a * N_RELAYED + j], device_id=(x, y, 1 - c), device_id_type=MESH))
    return copies


def _relay_start(name, bufs, after):
    n = len(bufs)

    def body(*refs):
        send_sems, recv_sems = refs[n + 1], refs[n + 2]
        for cp in _relay_copies(refs[:n], send_sems, recv_sems):
            cp.start()
        token = refs[2 * n + 3]
        token[...] = jnp.zeros_like(token)

    n_sems = n * N_RELAYED
    outs = pl.pallas_call(
        body, name=name,
        out_shape=[pltpu.SemaphoreType.DMA((n_sems,)), pltpu.SemaphoreType.DMA((n_sems,))]
        + [pltpu.HBM(t.shape, t.dtype) for t in bufs] + [jax.ShapeDtypeStruct((8, LANES), F32)],
        in_specs=[HBM_SPEC] * n + [ANY],
        out_specs=[SEM_SPEC, SEM_SPEC] + [HBM_SPEC] * n + [pl.BlockSpec(memory_space=pltpu.VMEM)],
        input_output_aliases={i: 2 + i for i in range(n)},
        compiler_params=pltpu.CompilerParams(has_side_effects=DATAFLOW),
    )(*[pltpu.with_memory_space_constraint(t, pltpu.HBM) for t in bufs], after)
    return dict(n=n, send_sems=outs[0], recv_sems=outs[1], bufs=outs[2:2 + n], token=outs[2 + n])


def _relay_wait(name, handle, after):
    n = handle["n"]

    def body(*refs):
        for cp in _relay_copies(refs[:n], refs[n], refs[n + 1]):
            cp.wait_send()
            cp.wait_recv()

    return pl.pallas_call(
        body, name=name,
        out_shape=[pltpu.HBM(t.shape, t.dtype) for t in handle["bufs"]],
        in_specs=[HBM_SPEC] * n + [SEM_SPEC, SEM_SPEC, ANY],
        out_specs=[HBM_SPEC] * n,
        input_output_aliases={i: i for i in range(n)},
        compiler_params=pltpu.CompilerParams(has_side_effects=DATAFLOW),
    )(*handle["bufs"], handle["send_sems"], handle["recv_sems"], after)


def _adamw(parts, w, m, v, sent=None):
    r, c = w.shape
    tr = 256 if r % 256 == 0 else r
    own = sent is not None

    def body(*refs):
        if own:
            _, p_ref, own_ref, w_ref, m_ref, v_ref, g_ref, d_ref, nm_ref, nv_ref = refs
            g = own_ref[...].astype(F32)
        else:
            p_ref, w_ref, m_ref, v_ref, g_ref, d_ref, nm_ref, nv_ref = refs
            g = p_ref[0].astype(F32)
        for k in range(1, N_DEV):
            g = g + p_ref[k].astype(F32)
        m_new = ADAM_B1 * m_ref[...] + (1.0 - ADAM_B1) * g
        v_new = ADAM_B2 * v_ref[...] + (1.0 - ADAM_B2) * (g * g)
        m_hat = m_new / (1.0 - ADAM_B1 ** ADAM_STEP)
        v_hat = v_new / (1.0 - ADAM_B2 ** ADAM_STEP)
        g_ref[...] = g
        d_ref[...] = -ADAM_LR * (m_hat / (jnp.sqrt(v_hat) + ADAM_EPS) + ADAM_WD * w_ref[...])
        nm_ref[...] = m_new
        nv_ref[...] = v_new

    out_shape = [jax.ShapeDtypeStruct((r, c), F32)] * 4
    if not own:
        return pl.pallas_call(
            body, name="adamw", grid=(r // tr,),
            in_specs=[pl.BlockSpec((N_DEV, tr, c), lambda i: (0, i, 0))] + [_rows(tr, c)] * 3,
            out_specs=[_rows(tr, c)] * 4, out_shape=out_shape, compiler_params=_params(),
        )(parts, w, m, v)
    rows = pl.BlockSpec((tr, c), lambda i, me: (i, 0))
    return pl.pallas_call(
        body, name="adamw_own", out_shape=out_shape, compiler_params=_params(),
        grid_spec=pltpu.PrefetchScalarGridSpec(
            num_scalar_prefetch=1, grid=(r // tr,),
            in_specs=[pl.BlockSpec((N_DEV, tr, c), lambda i, me: (0, i, 0)),
                      pl.BlockSpec((None, tr, c), lambda i, me: (me[0], i, 0))] + [rows] * 3,
            out_specs=[rows] * 4),
    )(_my_index().reshape(1).astype(jnp.int32), parts, sent, w, m, v)


def _adam_step(g, w, m, v):
    m_new = ADAM_B1 * m + (1.0 - ADAM_B1) * g
    v_new = ADAM_B2 * v + (1.0 - ADAM_B2) * (g * g)
    m_hat = m_new / (1.0 - ADAM_B1 ** ADAM_STEP)
    v_hat = v_new / (1.0 - ADAM_B2 ** ADAM_STEP)
    return -ADAM_LR * (m_hat / (jnp.sqrt(v_hat) + ADAM_EPS) + ADAM_WD * w), m_new, v_new


SMALL_NAMES = ("norm_mix", "pool_w", "pool_b", "pool_scale", "attn_sinks", "norm_mlp", "norm_final")


def _adamw_small(mlp_all, mix_all, scale_all, nmix_all, w, m, v):
    def body(mlp_ref, mix_ref, scale_ref, nmix_ref, *refs):
        ins, outs = refs[:21], refs[21:]

        def total(ref, rows, lanes=slice(None)):
            g = ref[0, rows, lanes]
            for k in range(1, N_DEV):
                g = g + ref[k, rows, lanes]
            return g

        grads = dict(
            norm_mix=total(nmix_ref, slice(0, 1)), pool_w=total(mix_ref, slice(0, MIX_POOL_B)),
            pool_b=total(mix_ref, slice(MIX_POOL_B, MIX_POOL_B + 4)), pool_scale=total(scale_ref, slice(0, 1)),
            attn_sinks=total(mix_ref, slice(MIX_SINKS, MIX_SINKS + 1)),
            norm_mlp=total(mlp_ref, slice(0, 1)), norm_final=total(mlp_ref, slice(1, 2)))
        for i, name in enumerate(SMALL_NAMES):
            g = grads[name]
            d, m_new, v_new = _adam_step(g, ins[3 * i][...], ins[3 * i + 1][...], ins[3 * i + 2][...])
            for ref, val in zip(outs[4 * i:4 * i + 4], (g, d, m_new, v_new)):
                ref[...] = val
        outs[28][...] = jnp.broadcast_to(total(mlp_ref, slice(2, 3), slice(0, LANES)), (8, LANES))

    operands, out_shape = [], []
    for name in SMALL_NAMES:
        operands += [w[name], m[name], v[name]]
        out_shape += [jax.ShapeDtypeStruct(w[name].shape, F32)] * 4
    out_shape.append(jax.ShapeDtypeStruct((8, LANES), F32))
    outs = pl.pallas_call(body, name="adamw_small", out_shape=out_shape)(
        mlp_all, mix_all, scale_all, nmix_all, *operands)
    return {name: outs[4 * i:4 * i + 4] for i, name in enumerate(SMALL_NAMES)}, outs[28]


def kernel(x, norm_mix, w_in, pool_w, pool_b, pool_scale, attn_sinks, p_pool, p_attn, w_out, norm_mlp, w_up, w_down, norm_final, loss_target, m_norm_mix, m_w_in, m_pool_w, m_pool_b, m_pool_scale, m_attn_sinks, m_p_pool, m_p_attn, m_w_out, m_norm_mlp, m_w_up, m_w_down, m_norm_final, v_norm_mix, v_w_in, v_pool_w, v_pool_b, v_pool_scale, v_attn_sinks, v_p_pool, v_p_attn, v_w_out, v_norm_mlp, v_w_up, v_w_down, v_norm_final):
    xs = x[0]
    tgt = loss_target[0]
    s_len = xs.shape[0]

    p_pool_bf, p_attn_bf, w_out_bf, w_up_bf, w_down_bf = [
        t[0].astype(BF16) for t in (p_pool, p_attn, w_out, w_up, w_down)]
    w_in_bf = w_in[0].T.astype(BF16)
    (w_in_g,) = _all_gather_weights("all_gather_w_in", [w_in_bf])
    ag_rest = _exchange_start(
        "ag_rest_start", [p_pool_bf, p_attn_bf, w_out_bf, w_up_bf, w_down_bf], ("chip",) * 5, w_in_g)

    pool_w_bf = pool_w[0].astype(BF16)
    pool_b_row = pool_b[0].reshape(1, POOL_WIDTH)
    bias_t, sink_row = _attn_constants(attn_sinks[0])

    w_in_t = w_in_g.reshape(IN_WIDTH, D_MODEL)
    u, zp, q, kv, zg = _fwd_in(ag_rest["token"], xs, norm_mix, w_in_t)
    half = (s_len // min(512, s_len)) // 2
    mixer_args = (zp, q, kv, pool_w_bf, pool_b_row, pool_scale, bias_t, sink_row)
    pm, o = _mixers_fwd(*mixer_args, n_tiles=half) if half else (None, None)
    first_level, _ = _exchange_wait("ag_rest_wait", ag_rest, zg if o is None else o)
    relay = _relay_start("ag_relay_start", first_level, zg)
    pm, o = _mixers_fwd(*mixer_args, first_tile=half, earlier=None if o is None else (relay["token"], pm, o))
    p_pool_g, p_attn_g, w_out_g, w_up_g, w_down_g = _relay_wait("ag_relay_wait", relay, o)
    p_pool_f = p_pool_g.transpose(1, 0, 2).reshape(POOL_WIDTH, D_MODEL)
    p_attn_f = p_attn_g.transpose(1, 0, 2).reshape(ATTN_WIDTH, D_MODEL)
    w_out_f = w_out_g.reshape(D_MODEL, D_MODEL)
    w_down_f = w_down_g.reshape(D_FF, D_MODEL)
    mixed, dh1, a, dapre, u2, dh2, small_mlp, dyp, dya, dzg, dpm, do, dh1_bf = _core(
        xs, pm, o, zg, tgt, norm_mlp, norm_final.reshape(1, D_MODEL), p_pool_f, p_attn_f, w_out_f, w_up_g, w_down_f)
    gw_down = _tn_matmul(a, dh2, square_a=True)
    gw_up = _tn_matmul(u2, dapre, col_blocks=N_DEV)
    ex_mlp = _exchange_start(
        "ex_mlp_start", [gw_up, gw_down.reshape(N_DEV, D_FF // N_DEV, D_MODEL)], (True, True), small_mlp)
    dzp, dq, dkv, small_mix, g_pool_scale = _mixers_bwd(
        ex_mlp["token"], zp, q, kv, dpm, do, pool_w_bf, pool_b_row, pool_scale, bias_t, sink_row)
    gw_in = _tn_w_in(u, dzp, dq, dkv, dzg)
    ex_in = _exchange_start(
        "ex_in_start", [gw_in, small_mlp, small_mix, g_pool_scale], (True, False, False, False), dq)
    gw_out = _tn_matmul(mixed, dh1_bf, after=ex_in["token"])
    gp_pool = _tn_matmul(pm, dyp, col_blocks=N_DEV, after=ex_in["token"])
    gp_attn = _tn_matmul(o, dya, col_blocks=N_DEV, after=ex_in["token"])
    ex_proj = _exchange_start(
        "ex_proj_start", [gp_pool, gp_attn, gw_out.reshape(N_DEV, D_MODEL // N_DEV, D_MODEL)], (True,) * 3,
        ex_in["token"])
    dx, g_norm_mix = _in_bwd(ex_proj["token"], dzp, dq, dkv, dzg, w_in_t, xs, dh1, norm_mix)

    big_w = dict(w_in=w_in, p_pool=p_pool, p_attn=p_attn, w_out=w_out, w_up=w_up, w_down=w_down)
    big_m = dict(w_in=m_w_in, p_pool=m_p_pool, p_attn=m_p_attn, w_out=m_w_out, w_up=m_w_up, w_down=m_w_down)
    big_v = dict(w_in=v_w_in, p_pool=v_p_pool, p_attn=v_p_attn, w_out=v_w_out, w_up=v_w_up, w_down=v_w_down)
    res = {}

    def update(names, recvs, sents):
        for name, parts, sent in zip(names, recvs, sents):
            flip = (lambda t: t.T) if name == "w_in" else (lambda t: t)
            outs = _adamw(parts, flip(big_w[name][0]), flip(big_m[name][0]), flip(big_v[name][0]), sent)
            res[name] = [flip(t)[None] for t in outs]

    update(["w_up", "w_down"], *_exchange_wait("ex_mlp_wait", ex_mlp, dx))
    (norm_mix_all,) = _all_gather_weights("all_gather_norm_mix", [g_norm_mix], res["w_down"][0])
    (r_in, mlp_all, mix_all, scale_all), (s_in, _, _, _) = _exchange_wait("ex_in_wait", ex_in, norm_mix_all)
    update(["w_in"], [r_in], [s_in])

    natural = dict(norm_mix=(1, D_MODEL), pool_w=(MIX_POOL_B, LANES), pool_b=(4, LANES), pool_scale=(1, POOL_WIDTH),
                   attn_sinks=(1, LANES), norm_mlp=(1, D_MODEL), norm_final=(1, D_MODEL))

    def as_2d(t, name):
        if name == "attn_sinks":
            return jnp.pad(t, ((0, 0), (0, LANES - N_HEADS)))
        return t.reshape(natural[name])

    small_w = dict(norm_mix=norm_mix, pool_w=pool_w, pool_b=pool_b, pool_scale=pool_scale, attn_sinks=attn_sinks,
                   norm_mlp=norm_mlp, norm_final=norm_final)
    small_m = dict(norm_mix=m_norm_mix, pool_w=m_pool_w, pool_b=m_pool_b, pool_scale=m_pool_scale,
                   attn_sinks=m_attn_sinks, norm_mlp=m_norm_mlp, norm_final=m_norm_final)
    small_v = dict(norm_mix=v_norm_mix, pool_w=v_pool_w, pool_b=v_pool_b, pool_scale=v_pool_scale,
                   attn_sinks=v_attn_sinks, norm_mlp=v_norm_mlp, norm_final=v_norm_final)
    small_res, loss_all = _adamw_small(
        mlp_all, mix_all, scale_all, norm_mix_all,
        *[{k: as_2d(t, k) for k, t in d.items()} for d in (small_w, small_m, small_v)])
    loss = loss_all[0, 0]
    for name in SMALL_NAMES:
        shape = small_w[name].shape
        res[name] = [(t[:, :N_HEADS] if name == "attn_sinks" else t).reshape(shape) for t in small_res[name]]
    update(["p_pool", "p_attn", "w_out"], *_exchange_wait("ex_proj_wait", ex_proj, loss_all))

    order = ["norm_mix", "w_in", "pool_w", "pool_b", "pool_scale", "attn_sinks", "p_pool", "p_attn", "w_out",
             "norm_mlp", "w_up", "w_down", "norm_final"]
    out = [loss, dx[None]]
    for kind in range(4):
        out += [res[name][kind] for name in order]
    return tuple(out)
```

```python
import functools
import math

import numpy as np
import jax
import jax.numpy as jnp
from jax import lax
from jax.experimental import pallas as pl
from jax.experimental.pallas import tpu as pltpu

F32 = jnp.float32
BF16 = jnp.bfloat16

D_MODEL = 1024
POOL_WIDTH = 512
ATTN_WIDTH = 512
KV_WIDTH = 128
HEAD_DIM = 64
N_HEADS = 8
N_KV_HEADS = 2
GROUP = 4
BLOCK = 128
POOL_WINDOWS = (2, 4, 8, 16)
POOL_GROUP_DIM = 128
POOL_HALO = 16
D_FF = 4096
FF_CHUNK = 1024
IN_WIDTH = 3328
RMS_EPS = 1e-5
NEG_INF = -1e30
ATTN_SCALE = 1.0 / math.sqrt(HEAD_DIM)
N_DEV = 8

ADAM_LR = 0.001
ADAM_B1 = 0.9
ADAM_B2 = 0.999
ADAM_EPS = 1e-08
ADAM_WD = 0.01
ADAM_STEP = 10

LANES = 128
VMEM_LIMIT_BYTES = 56 * 1024 * 1024
MESH = pl.DeviceIdType.MESH


def _params(n_grid_axes=1):
    return pltpu.CompilerParams(
        dimension_semantics=("arbitrary",) * n_grid_axes, vmem_limit_bytes=VMEM_LIMIT_BYTES)


def _dot(a, b):
    return jnp.dot(a, b, preferred_element_type=F32)


def _dot_nt(a, b):
    return lax.dot_general(a, b, (((1,), (1,)), ((), ())), preferred_element_type=F32)


def _dot_tn(a, b):
    return lax.dot_general(a, b, (((0,), (0,)), ((), ())), preferred_element_type=F32)


ANY = pl.BlockSpec(memory_space=pl.ANY)


def _rows(tm, n):
    return pl.BlockSpec((tm, n), lambda i: (i, 0))


def _whole(shape):
    zeros = (0,) * len(shape)
    return pl.BlockSpec(shape, lambda i: zeros)


def _rms_fwd(h, g):
    r = lax.rsqrt(jnp.mean(h * h, axis=-1, keepdims=True) + RMS_EPS)
    xh = h * r
    return r, xh, xh * g


def _rms_bwd(dy, xh, r, g):
    dxh = dy * g
    dh = r * (dxh - xh * jnp.mean(dxh * xh, axis=-1, keepdims=True))
    return dh, jnp.sum(dy * xh, axis=0, keepdims=True)


def _fwd_in(after, x, g_mix, w_in_t):
    s_len = x.shape[0]
    tm = min(512, s_len)

    def body(after_ref, x_ref, g_ref, w_ref, u_ref, zp_ref, q_ref, kv_ref, zg_ref):
        _, _, u = _rms_fwd(x_ref[...], g_ref[...])
        u = u.astype(BF16)
        u_ref[...] = u
        zp_ref[...] = _dot_nt(u, w_ref[0:512, :]).astype(BF16)
        q_ref[...] = _dot_nt(u, w_ref[512:1024, :]).astype(BF16)
        kv_ref[...] = _dot_nt(u, w_ref[1024:1280, :]).astype(BF16)
        zg_ref[...] = _dot_nt(u, w_ref[1280:3328, :]).astype(BF16)

    return pl.pallas_call(
        body, name="fwd_in", grid=(s_len // tm,),
        in_specs=[ANY, _rows(tm, D_MODEL), _whole((1, D_MODEL)),
                  pl.BlockSpec((IN_WIDTH, D_MODEL), lambda i: (0, 0), pipeline_mode=pl.Buffered(1))],
        out_specs=[_rows(tm, D_MODEL), _rows(tm, 512), _rows(tm, 512), _rows(tm, 256), _rows(tm, 2048)],
        out_shape=[jax.ShapeDtypeStruct((s_len, n), BF16) for n in (D_MODEL, 512, 512, 256, 2048)],
        compiler_params=_params(),
    )(after, x, g_mix, w_in_t)


def _attn_constants(sinks):
    r = np.arange(BLOCK)[:, None]
    qi = np.arange(BLOCK)[None, :]
    dist = np.where(r <= qi, qi - r, BLOCK + qi - r).astype(np.float32)
    slopes = np.array([2.0 ** (-8.0 * (h + 1) / N_HEADS) for h in range(N_HEADS)], dtype=np.float32)
    bias = (-slopes[:, None, None] * dist[None]).reshape(N_KV_HEADS, GROUP, BLOCK, BLOCK)
    bias = np.ascontiguousarray(bias.transpose(0, 2, 1, 3)).reshape(N_KV_HEADS, BLOCK, GROUP * BLOCK)
    sink_row = jnp.repeat(sinks.astype(F32).reshape(N_KV_HEADS, GROUP), BLOCK, axis=1)[:, None, :]
    return jnp.asarray(bias.astype(np.float32)), sink_row


def _own_block_mask():
    shape = (BLOCK, GROUP * BLOCK)
    r = lax.broadcasted_iota(jnp.int32, shape, 0)
    qi = lax.broadcasted_iota(jnp.int32, shape, 1) & (BLOCK - 1)
    return r <= qi


def _pack_keys(t, own):
    return jnp.where(own, t[BLOCK:], t[:BLOCK])


def _unpack_keys(t, own):
    zero = jnp.zeros_like(t)
    return jnp.concatenate([jnp.where(own, zero, t), jnp.where(own, t, zero)], axis=0)


def _left_half(shape):
    return lax.broadcasted_iota(jnp.int32, shape, 1) < HEAD_DIM


def _dup_halves(slab):
    swapped = pltpu.roll(slab, HEAD_DIM, 1)
    left = _left_half(slab.shape)
    return jnp.where(left, slab, swapped), jnp.where(left, swapped, slab)


def _fill_kv_slabs(kvh_ref, kv_ref, ka_ref, vd_ref):
    for rows, src in ((slice(0, BLOCK), kvh_ref), (slice(BLOCK, None), kv_ref)):
        kvf = src[...].astype(F32)
        for ref, lanes in ((ka_ref, slice(0, KV_WIDTH)), (vd_ref, slice(KV_WIDTH, 2 * KV_WIDTH))):
            d0, d1 = _dup_halves(kvf[:, lanes])
            ref[0, rows, :] = d0.astype(BF16)
            ref[1, rows, :] = d1.astype(BF16)


def _stack_pairs(a, h):
    pieces = []
    for j in range(2):
        pair = a[:, h * 256 + j * LANES:h * 256 + (j + 1) * LANES]
        left = _left_half(pair.shape)
        zero = jnp.zeros_like(pair)
        pieces += [jnp.where(left, pair, zero), jnp.where(left, zero, pair)]
    return jnp.concatenate(pieces, axis=0)


def _attn_probs(kk, q_st, bias_p, sink_row, own, first):
    s = _pack_keys(_dot_nt(kk, q_st), own) * ATTN_SCALE + bias_p
    if first is not None:
        s = jnp.where(jnp.logical_and(first, jnp.logical_not(own)), NEG_INF, s)
    m = jnp.maximum(jnp.max(s, axis=0, keepdims=True), sink_row)
    p = jnp.exp(s - m)
    es = jnp.exp(sink_row - m)
    inv = 1.0 / (jnp.sum(p, axis=0, keepdims=True) + es)
    return p * inv, es * inv


def _pool_d(ext, cur, g, row0):
    w = POOL_WINDOWS[g]
    acc = ext
    k = 1
    while k < w:
        acc = acc + pltpu.roll(acc, k, 0)
        k *= 2
    return _window_mean(acc[POOL_HALO:, :], w, row0) - cur


def _window_mean(total, w, row0):
    t = row0 + lax.broadcasted_iota(jnp.int32, (POOL_HALO, total.shape[1]), 0)
    head = total[:POOL_HALO] / jnp.minimum(t + 1, w).astype(F32)
    return jnp.concatenate([head, total[POOL_HALO:] * (1.0 / w)], axis=0)


def _mixers_fwd(zp, q, kv, pool_w, pool_b, pool_scale, bias_t, sink_row, first_tile=0, n_tiles=None, earlier=None):
    s_len = zp.shape[0]
    tq = min(512, s_len)
    nb = tq // BLOCK
    n_tiles = s_len // tq - first_tile if n_tiles is None else n_tiles
    extra = [] if earlier is None else list(earlier)

    def body(zp_ref, zph_ref, q_ref, kv_ref, kvh_ref, pw_ref, pb_ref, ps_ref, bias_ref, sink_ref, *rest):
        pm_ref, o_ref, ka_ref, vd_ref = rest[len(extra):]
        i = pl.program_id(0) + first_tile
        cur = zp_ref[...].astype(F32)
        halo = zph_ref[...].astype(F32) * (i > 0).astype(F32)
        ext = jnp.concatenate([halo, cur], axis=0)
        for g in range(4):
            sl = slice(g * POOL_GROUP_DIM, (g + 1) * POOL_GROUP_DIM)
            d = _pool_d(ext[:, sl], cur[:, sl], g, i * tq)
            y = _dot(d.astype(BF16), pw_ref[g]) + pb_ref[:, sl]
            pm_ref[:, sl] = (y * ps_ref[:, sl]).astype(BF16)
        _fill_kv_slabs(kvh_ref, kv_ref, ka_ref, vd_ref)
        own = _own_block_mask()
        for b in range(nb):
            rq = slice(b * BLOCK, (b + 1) * BLOCK)
            rk = slice(b * BLOCK, (b + 2) * BLOCK)
            qb = q_ref[rq, :]
            for h in range(N_KV_HEADS):
                pn, _ = _attn_probs(ka_ref[h, rk, :], _stack_pairs(qb, h), bias_ref[h], sink_ref[h], own,
                                    (i == 0) if b == 0 else None)
                pn = _unpack_keys(pn, own).astype(BF16)
                vd = vd_ref[h, rk, :]
                left = _left_half(vd.shape)
                zero = jnp.zeros_like(vd)
                va, vb = jnp.where(left, vd, zero), jnp.where(left, zero, vd)
                for j in range(2):
                    o_pair = (_dot_tn(pn[:, (2 * j) * BLOCK:(2 * j + 1) * BLOCK], va)
                              + _dot_tn(pn[:, (2 * j + 1) * BLOCK:(2 * j + 2) * BLOCK], vb))
                    o_ref[rq, h * 256 + j * LANES:h * 256 + (j + 1) * LANES] = o_pair.astype(BF16)

    def tile(n):
        return pl.BlockSpec((tq, n), lambda i: (i + first_tile, 0))

    halo_pool = pl.BlockSpec(
        (POOL_HALO, 512), lambda i: (jnp.maximum((i + first_tile) * (tq // POOL_HALO) - 1, 0), 0))
    halo_kv = pl.BlockSpec((BLOCK, 256), lambda i: (jnp.maximum((i + first_tile) * nb - 1, 0), 0))
    return pl.pallas_call(
        body, name="mixers_fwd", grid=(n_tiles,),
        in_specs=[tile(512), halo_pool, tile(512), tile(256), halo_kv,
                  _whole((4, 128, 128)), _whole((1, 512)), _whole((1, 512)),
                  _whole((N_KV_HEADS, BLOCK, GROUP * BLOCK)), _whole((N_KV_HEADS, 1, GROUP * BLOCK))]
        + [ANY] * len(extra),
        out_specs=[tile(512), tile(512)],
        out_shape=[jax.ShapeDtypeStruct((s_len, 512), BF16)] * 2,
        input_output_aliases={11: 0, 12: 1} if extra else {},
        scratch_shapes=[pltpu.VMEM((N_KV_HEADS, tq + BLOCK, LANES), BF16)] * 2,
        compiler_params=_params(),
    )(zp, zp, q, kv, kv, pool_w, pool_b, pool_scale, bias_t, sink_row, *extra)


def _gated_mix(pm, o, zg, pp_ref, pa_ref):
    yp = _dot(pm, pp_ref[...])
    ya = _dot(o, pa_ref[...])
    gp = jax.nn.sigmoid(zg[:, :D_MODEL].astype(F32))
    ga = jax.nn.sigmoid(zg[:, D_MODEL:].astype(F32))
    return yp, ya, gp, ga


def _core(x, pm, o, zg, tgt, g_mlp, g_fin, p_pool, p_attn, w_out, w_up_blocks, w_down):
    s_len = x.shape[0]
    tm = min(256, s_len)
    n_chunks = D_FF // FF_CHUNK
    up_block = D_FF // N_DEV
    per_chunk = FF_CHUNK // up_block

    def body(x_ref, pm_ref, o_ref, zg_ref, tgt_ref, gm_ref, gf_ref, pp_ref, pa_ref, wo_ref, wu_ref, wd_ref,
             mixed_ref, dh1_ref, a_ref, dap_ref, u2_ref, dh2_ref, small_ref,
             dyp_ref, dya_ref, dzg_ref, dpm_ref, do_ref, dh1b_ref):
        i = pl.program_id(0)

        @pl.when(i == 0)
        def _():
            small_ref[...] = jnp.zeros_like(small_ref)

        yp, ya, gp, ga = _gated_mix(pm_ref[...], o_ref[...], zg_ref[...], pp_ref, pa_ref)
        mixed = (gp * yp + ga * ya).astype(BF16)
        mixed_ref[...] = mixed
        h1 = x_ref[...] + _dot(mixed, wo_ref[...])
        r2, xh2, u2 = _rms_fwd(h1, gm_ref[...])
        u2 = u2.astype(BF16)
        u2_ref[...] = u2
        acc = jnp.zeros((tm, D_MODEL), F32)
        for c in range(n_chunks):
            cs = slice(c * FF_CHUNK, (c + 1) * FF_CHUNK)
            a = jnp.concatenate([_dot(u2, wu_ref[per_chunk * c + j]) for j in range(per_chunk)], axis=1)
            a = jnp.maximum(a, 0.0)
            a_ref[:, cs] = a.astype(BF16)
            acc = acc + _dot((a * a).astype(BF16), wd_ref[cs, :])
        h2 = h1 + acc
        r3, xh3, y = _rms_fwd(h2, gf_ref[...])
        diff = y - tgt_ref[...]
        small_ref[2:3, :] += 0.5 * jnp.sum(jnp.mean(diff * diff, axis=-1, keepdims=True))
        dy = diff * (1.0 / D_MODEL)
        dh2, dgf = _rms_bwd(dy, xh3, r3, gf_ref[...])
        small_ref[1:2, :] += dgf
        dh2_bf = dh2.astype(BF16)
        dh2_ref[...] = dh2_bf
        du2 = jnp.zeros((tm, D_MODEL), F32)
        for c in range(n_chunks):
            cs = slice(c * FF_CHUNK, (c + 1) * FF_CHUNK)
            ds = _dot_nt(dh2_bf, wd_ref[cs, :])
            dap = (ds * (2.0 * a_ref[:, cs].astype(F32))).astype(BF16)
            dap_ref[:, cs] = dap
            for j in range(per_chunk):
                du2 = du2 + _dot_nt(dap[:, j * up_block:(j + 1) * up_block], wu_ref[per_chunk * c + j])
        dh1n, dgm = _rms_bwd(du2, xh2, r2, gm_ref[...])
        small_ref[0:1, :] += dgm
        dh1 = dh2 + dh1n
        dh1_ref[...] = dh1
        dh1_bf = dh1.astype(BF16)
        dh1b_ref[...] = dh1_bf
        dm = _dot_nt(dh1_bf, wo_ref[...])
        dyp = (dm * gp).astype(BF16)
        dya = (dm * ga).astype(BF16)
        dyp_ref[...] = dyp
        dya_ref[...] = dya
        dzg_ref[:, :D_MODEL] = (dm * yp * (gp * (1.0 - gp))).astype(BF16)
        dzg_ref[:, D_MODEL:] = (dm * ya * (ga * (1.0 - ga))).astype(BF16)
        dpm_ref[...] = _dot_nt(dyp, pp_ref[...]).astype(BF16)
        do_ref[...] = _dot_nt(dya, pa_ref[...]).astype(BF16)

    def fixed(shape):
        return pl.BlockSpec(shape, lambda i: (0,) * len(shape), pipeline_mode=pl.Buffered(1))

    widths_dtypes = ((D_MODEL, BF16), (D_MODEL, F32), (D_FF, BF16), (D_FF, BF16), (D_MODEL, BF16), (D_MODEL, BF16))
    back = ((D_MODEL, BF16), (D_MODEL, BF16), (2048, BF16), (512, BF16), (512, BF16), (D_MODEL, BF16))
    return pl.pallas_call(
        body, name="core", grid=(s_len // tm,),
        in_specs=[_rows(tm, D_MODEL), _rows(tm, 512), _rows(tm, 512), _rows(tm, 2048), _rows(tm, D_MODEL),
                  _whole((1, D_MODEL)), _whole((1, D_MODEL)),
                  fixed((512, D_MODEL)), fixed((512, D_MODEL)), fixed((D_MODEL, D_MODEL)),
                  fixed((N_DEV, D_MODEL, up_block)), fixed((D_FF, D_MODEL))],
        out_specs=[_rows(tm, n) for n, _ in widths_dtypes] + [_whole((8, D_MODEL))] + [_rows(tm, n) for n, _ in back],
        out_shape=[jax.ShapeDtypeStruct((s_len, n), d) for n, d in widths_dtypes]
        + [jax.ShapeDtypeStruct((8, D_MODEL), F32)] + [jax.ShapeDtypeStruct((s_len, n), d) for n, d in back],
        compiler_params=_params(),
    )(x, pm, o, zg, tgt, g_mlp, g_fin, p_pool, p_attn, w_out, w_up_blocks, w_down)


def _tn_matmul(a, b, square_a=False, col_blocks=None, after=None):
    s_len, ka = a.shape
    nb = b.shape[1]
    tt = min(2048, s_len)
    tk = min(1024, ka)
    tn = min(1024, nb)
    n_t = s_len // tt
    if col_blocks is None:
        out_spec = pl.BlockSpec((tk, tn), lambda k, j, t: (k, j))
        out_shape = jax.ShapeDtypeStruct((ka, nb), BF16)
    else:
        width = nb // col_blocks
        per_tile = tn // width
        out_spec = pl.BlockSpec((per_tile, tk, width), lambda k, j, t: (j, k, 0))
        out_shape = jax.ShapeDtypeStruct((col_blocks, ka, width), BF16)

    extra = [] if after is None else [after]

    def body(a_ref, b_ref, *rest):
        o_ref, acc_ref = rest[len(extra):]
        t = pl.program_id(2)

        @pl.when(t == 0)
        def _():
            acc_ref[...] = jnp.zeros_like(acc_ref)

        av = a_ref[...]
        if square_a:
            av = av * av
        acc_ref[...] += _dot_tn(av.astype(BF16), b_ref[...].astype(BF16))

        @pl.when(t == n_t - 1)
        def _():
            if col_blocks is None:
                o_ref[...] = acc_ref[...].astype(o_ref.dtype)
            else:
                for blk in range(per_tile):
                    o_ref[blk] = acc_ref[:, blk * width:(blk + 1) * width].astype(o_ref.dtype)

    return pl.pallas_call(
        body, name="tn_matmul", grid=(ka // tk, nb // tn, n_t),
        in_specs=[pl.BlockSpec((tt, tk), lambda k, j, t: (t, k)), pl.BlockSpec((tt, tn), lambda k, j, t: (t, j))]
        + [ANY] * len(extra),
        out_specs=out_spec, out_shape=out_shape,
        scratch_shapes=[pltpu.VMEM((tk, tn), F32)],
        compiler_params=_params(3),
    )(a, b, *extra)


def _tn_w_in(u, dzp, dq, dkv, dzg):
    s_len = u.shape[0]
    tt = min(1024, s_len)
    n_t = s_len // tt
    width = IN_WIDTH // N_DEV
    pieces = ((0, 512), (512, 1024), (1024, 1280), (1280, IN_WIDTH))

    def body(u_ref, dzp_ref, dq_ref, dkv_ref, dzg_ref, o_ref, acc_ref):
        t = pl.program_id(0)

        @pl.when(t == 0)
        def _():
            acc_ref[...] = jnp.zeros_like(acc_ref)

        uv = u_ref[...]
        for (c0, c1), ref in zip(pieces, (dzp_ref, dq_ref, dkv_ref, dzg_ref)):
            acc_ref[c0:c1, :] += _dot_tn(ref[...], uv)

        @pl.when(t == n_t - 1)
        def _():
            for j in range(N_DEV):
                o_ref[j] = acc_ref[j * width:(j + 1) * width, :].astype(BF16)

    return pl.pallas_call(
        body, name="tn_w_in", grid=(n_t,),
        in_specs=[_rows(tt, D_MODEL)] + [_rows(tt, c1 - c0) for c0, c1 in pieces],
        out_specs=_whole((N_DEV, width, D_MODEL)),
        out_shape=jax.ShapeDtypeStruct((N_DEV, width, D_MODEL), BF16),
        scratch_shapes=[pltpu.VMEM((IN_WIDTH, D_MODEL), F32)],
        compiler_params=_params(),
    )(u, dzp, dq, dkv, dzg)


MIX_POOL_B = 4 * POOL_GROUP_DIM
MIX_SINKS = MIX_POOL_B + 8
MIX_ROWS = MIX_SINKS + 8


def _mixers_bwd(after, zp, q, kv, dpm, do, pool_w, pool_b, pool_scale, bias_t, sink_row):
    s_len = zp.shape[0]
    tq = min(512, s_len)
    nb = tq // BLOCK
    n_steps = s_len // tq

    def body(after_ref, zp_ref, zph_ref, q_ref, kv_ref, kvh_ref, dpm_ref, dpmh_ref, do_ref, pw_ref, pb_ref, ps_ref,
             bias_ref, sink_ref, dzp_ref, dq_ref, dkv_ref, small_ref, dps_ref,
             ka_ref, vd_ref, dsk_acc, dkv_acc):
        i = pl.program_id(0)

        @pl.when(i == 0)
        def _():
            dkv_acc[...] = jnp.zeros_like(dkv_acc)
            small_ref[...] = jnp.zeros_like(small_ref)
            dps_ref[...] = jnp.zeros_like(dps_ref)
            dsk_acc[...] = jnp.zeros_like(dsk_acc)

        cur = zp_ref[...].astype(F32)
        halo = zph_ref[...].astype(F32) * (i > 0).astype(F32)
        ext = jnp.concatenate([halo, cur], axis=0)
        dpm_next = dpmh_ref[...].astype(F32) * (i < n_steps - 1).astype(F32)
        dpm_ext = jnp.concatenate([dpm_ref[...].astype(F32), dpm_next], axis=0)
        n_ext = tq + POOL_HALO
        for g in range(4):
            sl = slice(g * POOL_GROUP_DIM, (g + 1) * POOL_GROUP_DIM)
            w = POOL_WINDOWS[g]
            d = _pool_d(ext[:, sl], cur[:, sl], g, i * tq).astype(BF16)
            y_lin = _dot(d, pw_ref[g]) + pb_ref[:, sl]
            dps_ref[:, sl] += jnp.sum(dpm_ext[:tq, sl] * y_lin, axis=0, keepdims=True)
            dyl_ext = dpm_ext[:, sl] * ps_ref[:, sl]
            small_ref[MIX_POOL_B + g:MIX_POOL_B + g + 1, :] += jnp.sum(dyl_ext[:tq], axis=0, keepdims=True)
            dyl_bf = dyl_ext.astype(BF16)
            small_ref[g * POOL_GROUP_DIM:(g + 1) * POOL_GROUP_DIM, :] += _dot_tn(d, dyl_bf[:tq])
            dd = _dot_nt(dyl_bf, pw_ref[g])
            e = _window_mean(dd, w, i * tq)
            acc = e
            k = 1
            while k < w:
                acc = acc + pltpu.roll(acc, n_ext - k, 0)
                k *= 2
            dzp_ref[:, sl] = (acc[:tq] - dd[:tq]).astype(BF16)

        _fill_kv_slabs(kvh_ref, kv_ref, ka_ref, vd_ref)

        def fold(dup):
            return dup + pltpu.roll(dup, HEAD_DIM, 1)

        own = _own_block_mask()
        for b in range(nb):
            rq = slice(b * BLOCK, (b + 1) * BLOCK)
            rk = slice(b * BLOCK, (b + 2) * BLOCK)
            qb = q_ref[rq, :]
            dob = do_ref[rq, :]
            dk_dup, dv_dup = [], []
            for h in range(N_KV_HEADS):
                kk = ka_ref[h, rk, :]
                q_st = _stack_pairs(qb, h)
                do_st = _stack_pairs(dob, h)
                pn, psink = _attn_probs(kk, q_st, bias_ref[h], sink_ref[h], own, (i == 0) if b == 0 else None)
                dp = _pack_keys(_dot_nt(vd_ref[h, rk, :], do_st), own)
                delta = jnp.sum(pn * dp, axis=0, keepdims=True)
                dsk_acc[h] += -psink * delta
                ds = _unpack_keys((pn * (dp - delta)) * ATTN_SCALE, own).astype(BF16)
                pn = _unpack_keys(pn, own)
                dq_st = _dot_tn(ds, kk)
                for j in range(2):
                    left = _left_half((BLOCK, LANES))
                    dq_pair = jnp.where(left, dq_st[(2 * j) * BLOCK:(2 * j + 1) * BLOCK],
                                        dq_st[(2 * j + 1) * BLOCK:(2 * j + 2) * BLOCK])
                    dq_ref[rq, h * 256 + j * LANES:h * 256 + (j + 1) * LANES] = dq_pair.astype(BF16)
                dk_dup.append(fold(_dot(ds, q_st)))
                dv_dup.append(fold(_dot(pn.astype(BF16), do_st)))
            left = _left_half((2 * BLOCK, LANES))
            dkv_blk = jnp.concatenate([jnp.where(left, dk_dup[0], dk_dup[1]),
                                       jnp.where(left, dv_dup[0], dv_dup[1])], axis=1)
            g0 = pl.multiple_of(i * tq + b * BLOCK, BLOCK)
            dkv_acc[pl.ds(g0, 2 * BLOCK), :] += dkv_blk

        @pl.when(i == n_steps - 1)
        def _():
            dkv_ref[...] = dkv_acc[BLOCK:, :].astype(BF16)
            lane = lax.broadcasted_iota(jnp.int32, (1, LANES), 1)
            row = jnp.zeros((1, LANES), F32)
            for h in range(N_KV_HEADS):
                for g in range(GROUP):
                    tot = jnp.sum(dsk_acc[h, :, g * BLOCK:(g + 1) * BLOCK], axis=1, keepdims=True)
                    row = jnp.where(lane == GROUP * h + g, tot, row)
            small_ref[MIX_SINKS:MIX_SINKS + 1, :] = row

    blocks_per_tile = tq // POOL_HALO
    last_halo = s_len // POOL_HALO - 1
    halo_prev = pl.BlockSpec((POOL_HALO, 512), lambda i: (jnp.maximum(i * blocks_per_tile - 1, 0), 0))
    halo_next = pl.BlockSpec((POOL_HALO, 512), lambda i: (jnp.minimum((i + 1) * blocks_per_tile, last_halo), 0))
    halo_kv = pl.BlockSpec((BLOCK, 256), lambda i: (jnp.maximum(i * nb - 1, 0), 0))
    return pl.pallas_call(
        body, name="mixers_bwd", grid=(n_steps,),
        in_specs=[ANY, _rows(tq, 512), halo_prev, _rows(tq, 512), _rows(tq, 256), halo_kv,
                  _rows(tq, 512), halo_next, _rows(tq, 512),
                  _whole((4, 128, 128)), _whole((1, 512)), _whole((1, 512)),
                  _whole((N_KV_HEADS, BLOCK, GROUP * BLOCK)), _whole((N_KV_HEADS, 1, GROUP * BLOCK))],
        out_specs=[_rows(tq, 512), _rows(tq, 512), _whole((s_len, 256)),
                   _whole((MIX_ROWS, LANES)), _whole((1, 512))],
        out_shape=[jax.ShapeDtypeStruct((s_len, 512), BF16), jax.ShapeDtypeStruct((s_len, 512), BF16),
                   jax.ShapeDtypeStruct((s_len, 256), BF16), jax.ShapeDtypeStruct((MIX_ROWS, LANES), F32),
                   jax.ShapeDtypeStruct((1, 512), F32)],
        scratch_shapes=[pltpu.VMEM((N_KV_HEADS, tq + BLOCK, LANES), BF16)] * 2
        + [pltpu.VMEM((N_KV_HEADS, 1, GROUP * BLOCK), F32), pltpu.VMEM((s_len + BLOCK, 256), F32)],
        compiler_params=_params(),
    )(after, zp, zp, q, kv, kv, dpm, dpm, do, pool_w, pool_b, pool_scale, bias_t, sink_row)


def _in_bwd(after, dzp, dq, dkv, dzg, w_in_t, x, dh1, g_mix):
    s_len = x.shape[0]
    tm = min(512, s_len)

    def body(after_ref, dzp_ref, dq_ref, dkv_ref, dzg_ref, w_ref, x_ref, dh1_ref, g_ref, dx_ref, dg_ref):
        i = pl.program_id(0)

        @pl.when(i == 0)
        def _():
            dg_ref[...] = jnp.zeros_like(dg_ref)

        du = _dot(dzp_ref[...], w_ref[0:512, :])
        du = du + _dot(dq_ref[...], w_ref[512:1024, :])
        du = du + _dot(dkv_ref[...], w_ref[1024:1280, :])
        du = du + _dot(dzg_ref[...], w_ref[1280:3328, :])
        r, xh, _ = _rms_fwd(x_ref[...], g_ref[...])
        dxn, dg = _rms_bwd(du, xh, r, g_ref[...])
        dg_ref[...] += dg
        dx_ref[...] = dh1_ref[...] + dxn

    return pl.pallas_call(
        body, name="in_bwd", grid=(s_len // tm,),
        in_specs=[ANY, _rows(tm, 512), _rows(tm, 512), _rows(tm, 256), _rows(tm, 2048), _whole((IN_WIDTH, D_MODEL)),
                  _rows(tm, D_MODEL), _rows(tm, D_MODEL), _whole((1, D_MODEL))],
        out_specs=[_rows(tm, D_MODEL), _whole((1, D_MODEL))],
        out_shape=[jax.ShapeDtypeStruct((s_len, D_MODEL), F32), jax.ShapeDtypeStruct((1, D_MODEL), F32)],
        compiler_params=_params(),
    )(after, dzp, dq, dkv, dzg, w_in_t, x, dh1, g_mix)


def _all_gather_weights(name, shards, after=None):
    n = len(shards)
    extra = [] if after is None else [after]
    n_extra = len(extra)

    def body(*refs):
        ins, outs = refs[:n], refs[n + n_extra:2 * n + n_extra]
        send_sems, recv_sems, local_sems = refs[2 * n + n_extra:]
        x, y, c = lax.axis_index("x"), lax.axis_index("y"), lax.axis_index("c")
        me, sibling = (x, y, c), (x, y, 1 - c)
        chips = [(1 - x, y), (x, 1 - y), (1 - x, 1 - y)]

        def slot(a, px, py, pc):
            return outs[a].at[4 * px + 2 * py + pc]

        def copy(a, k, block, to, src=None):
            return pltpu.make_async_remote_copy(
                src_ref=slot(a, *block) if src is None else src, dst_ref=slot(a, *block),
                send_sem=send_sems.at[a, k], recv_sem=recv_sems.at[a, k], device_id=to, device_id_type=MESH)

        mine = [pltpu.make_async_copy(ins[a], slot(a, *me), local_sems.at[a]) for a in range(n)]
        for cp in mine:
            cp.start()
        first = []
        for a in range(n):
            first.append(copy(a, 0, me, sibling, src=ins[a]))
            first += [copy(a, 1 + j, me, (*chip, c), src=ins[a]) for j, chip in enumerate(chips)]
        for cp in first:
            cp.start()
        passed = []
        for a in range(n):
            for j, chip in enumerate(chips):
                copy(a, 1 + j, (*chip, c), me).wait_recv()
                cp = copy(a, 4 + j, (*chip, c), sibling)
                cp.start()
                passed.append(cp)
        for a in range(n):
            copy(a, 0, sibling, me).wait_recv()
            for j, chip in enumerate(chips):
                copy(a, 4 + j, (*chip, 1 - c), me).wait_recv()
        for cp in first + passed:
            cp.wait_send()
        for cp in mine:
            cp.wait()

    return pl.pallas_call(
        body, name=name,
        in_specs=[ANY] * (n + n_extra), out_specs=[ANY] * n,
        out_shape=[jax.ShapeDtypeStruct((N_DEV,) + s.shape, s.dtype) for s in shards],
        scratch_shapes=[pltpu.SemaphoreType.DMA((n, 7)), pltpu.SemaphoreType.DMA((n, 7)), pltpu.SemaphoreType.DMA((n,))],
    )(*shards, *extra)


HBM_SPEC = pl.BlockSpec(memory_space=pltpu.HBM)
SEM_SPEC = pl.BlockSpec(memory_space=pltpu.SEMAPHORE)
DATAFLOW = pltpu.SideEffectType.DATAFLOW_SIDE_EFFECTING
N_PEERS = N_DEV - 1


CHIP_PEERS = (1, 2, 4, 6)
RELAYED = (2, 4, 6)


def _peer_copies(srcs, lands, scatter, send_sems, recv_sems):
    x, y, c = lax.axis_index("x"), lax.axis_index("y"), lax.axis_index("c")
    me_idx = 4 * x + 2 * y + c
    copies = []
    for k in range(1, N_DEV):
        px = 1 - x if (k >> 2) & 1 else x
        py = 1 - y if (k >> 1) & 1 else y
        pc = 1 - c if k & 1 else c
        p_idx = 4 * px + 2 * py + pc
        for a in range(len(srcs)):
            if scatter[a] == "chip" and k not in CHIP_PEERS:
                continue
            src = srcs[a].at[p_idx] if scatter[a] is True else srcs[a]
            dst = lands[a].at[k] if scatter[a] is True else lands[a].at[me_idx]
            copies.append(pltpu.make_async_remote_copy(
                src_ref=src, dst_ref=dst, send_sem=send_sems.at[a * N_PEERS + k - 1],
                recv_sem=recv_sems.at[a * N_PEERS + k - 1],
                device_id=(px, py, pc), device_id_type=MESH))
    return copies


def _exchange_start(name, srcs, scatter, after):
    n = len(srcs)
    lands = [lax.empty(s.shape if sc is True else (N_DEV,) + s.shape, s.dtype) for s, sc in zip(srcs, scatter)]

    def body(*refs):
        src_refs, land_refs = refs[:n], refs[n:2 * n]
        send_sems, recv_sems = refs[2 * n + 1], refs[2 * n + 2]
        token = refs[4 * n + 3]
        for cp in _peer_copies(src_refs, land_refs, scatter, send_sems, recv_sems):
            cp.start()
        token[...] = jnp.zeros_like(token)

    hbm = lambda t: pltpu.HBM(t.shape, t.dtype)
    outs = pl.pallas_call(
        body, name=name,
        out_shape=[pltpu.SemaphoreType.DMA((n * N_PEERS,)), pltpu.SemaphoreType.DMA((n * N_PEERS,))]
        + [hbm(t) for t in srcs] + [hbm(t) for t in lands] + [jax.ShapeDtypeStruct((8, LANES), F32)],
        in_specs=[HBM_SPEC] * (2 * n) + [ANY],
        out_specs=[SEM_SPEC, SEM_SPEC] + [HBM_SPEC] * (2 * n) + [pl.BlockSpec(memory_space=pltpu.VMEM)],
        input_output_aliases={i: 2 + i for i in range(2 * n)},
        compiler_params=pltpu.CompilerParams(has_side_effects=DATAFLOW),
    )(*[pltpu.with_memory_space_constraint(t, pltpu.HBM) for t in list(srcs) + lands], after)
    return dict(n=n, scatter=scatter, send_sems=outs[0], recv_sems=outs[1], srcs=outs[2:2 + n],
                lands=outs[2 + n:2 + 2 * n], token=outs[2 + 2 * n])


def _exchange_wait(name, handle, after):
    n, scatter = handle["n"], handle["scatter"]

    def body(*refs):
        src_refs, land_refs = refs[:n], refs[n:2 * n]
        send_sems, recv_sems = refs[2 * n], refs[2 * n + 1]
        for cp in _peer_copies(src_refs, land_refs, scatter, send_sems, recv_sems):
            cp.wait_send()
            cp.wait_recv()

    both = list(handle["srcs"]) + list(handle["lands"])
    outs = pl.pallas_call(
        body, name=name,
        out_shape=[pltpu.HBM(t.shape, t.dtype) for t in both],
        in_specs=[HBM_SPEC] * (2 * n) + [SEM_SPEC, SEM_SPEC, ANY],
        out_specs=[HBM_SPEC] * (2 * n),
        input_output_aliases={i: i for i in range(2 * n)},
        compiler_params=pltpu.CompilerParams(has_side_effects=DATAFLOW),
    )(*both, handle["send_sems"], handle["recv_sems"], after)
    me_idx = _my_index()
    lands = [lax.dynamic_update_index_in_dim(land, src, me_idx, 0) if sc is False else land
             for land, src, sc in zip(outs[n:], outs[:n], scatter)]
    return lands, outs[:n]


def _my_index():
    return 4 * lax.axis_index("x") + 2 * lax.axis_index("y") + lax.axis_index("c")


N_RELAYED = len(RELAYED) + 1


def _relay_copies(bufs, send_sems, recv_sems):
    x, y, c = lax.axis_index("x"), lax.axis_index("y"), lax.axis_index("c")
    slots = []
    for k in RELAYED:
        px = 1 - x if (k >> 2) & 1 else x
        py = 1 - y if (k >> 1) & 1 else y
        slots.append(4 * px + 2 * py + c)
    slots.append(4 * x + 2 * y + (1 - c))
    copies = []
    for j, slot in enumerate(slots):
        for a, buf in enumerate(bufs):
            copies.append(pltpu.make_async_remote_copy(
                src_ref=buf.at[slot], dst_ref=buf.at[slot], send_sem=send_sems.at[a * N_RELAYED + j],
                recv_sem=recv_sems.at[a * N_RELAYED + j], device_id=(x, y, 1 - c), device_id_type=MESH))
    return copies


def _relay_start(name, bufs, after):
    n = len(bufs)

    def body(*refs):
        send_sems, recv_sems = refs[n + 1], refs[n + 2]
        for cp in _relay_copies(refs[:n], send_sems, recv_sems):
            cp.start()
        token = refs[2 * n + 3]
        token[...] = jnp.zeros_like(token)

    n_sems = n * N_RELAYED
    outs = pl.pallas_call(
        body, name=name,
        out_shape=[pltpu.SemaphoreType.DMA((n_sems,)), pltpu.SemaphoreType.DMA((n_sems,))]
        + [pltpu.HBM(t.shape, t.dtype) for t in bufs] + [jax.ShapeDtypeStruct((8, LANES), F32)],
        in_specs=[HBM_SPEC] * n + [ANY],
        out_specs=[SEM_SPEC, SEM_SPEC] + [HBM_SPEC] * n + [pl.BlockSpec(memory_space=pltpu.VMEM)],
        input_output_aliases={i: 2 + i for i in range(n)},
        compiler_params=pltpu.CompilerParams(has_side_effects=DATAFLOW),
    )(*[pltpu.with_memory_space_constraint(t, pltpu.HBM) for t in bufs], after)
    return dict(n=n, send_sems=outs[0], recv_sems=outs[1], bufs=outs[2:2 + n], token=outs[2 + n])


def _relay_wait(name, handle, after):
    n = handle["n"]

    def body(*refs):
        for cp in _relay_copies(refs[:n], refs[n], refs[n + 1]):
            cp.wait_send()
            cp.wait_recv()

    return pl.pallas_call(
        body, name=name,
        out_shape=[pltpu.HBM(t.shape, t.dtype) for t in handle["bufs"]],
        in_specs=[HBM_SPEC] * n + [SEM_SPEC, SEM_SPEC, ANY],
        out_specs=[HBM_SPEC] * n,
        input_output_aliases={i: i for i in range(n)},
        compiler_params=pltpu.CompilerParams(has_side_effects=DATAFLOW),
    )(*handle["bufs"], handle["send_sems"], handle["recv_sems"], after)


def _adamw(parts, w, m, v, sent=None):
    r, c = w.shape
    tr = 256 if r % 256 == 0 else r
    own = sent is not None

    def body(*refs):
        if own:
            _, p_ref, own_ref, w_ref, m_ref, v_ref, g_ref, d_ref, nm_ref, nv_ref = refs
            g = own_ref[...].astype(F32)
        else:
            p_ref, w_ref, m_ref, v_ref, g_ref, d_ref, nm_ref, nv_ref = refs
            g = p_ref[0].astype(F32)
        for k in range(1, N_DEV):
            g = g + p_ref[k].astype(F32)
        m_new = ADAM_B1 * m_ref[...] + (1.0 - ADAM_B1) * g
        v_new = ADAM_B2 * v_ref[...] + (1.0 - ADAM_B2) * (g * g)
        m_hat = m_new / (1.0 - ADAM_B1 ** ADAM_STEP)
        v_hat = v_new / (1.0 - ADAM_B2 ** ADAM_STEP)
        g_ref[...] = g
        d_ref[...] = -ADAM_LR * (m_hat / (jnp.sqrt(v_hat) + ADAM_EPS) + ADAM_WD * w_ref[...])
        nm_ref[...] = m_new
        nv_ref[...] = v_new

    out_shape = [jax.ShapeDtypeStruct((r, c), F32)] * 4
    if not own:
        return pl.pallas_call(
            body, name="adamw", grid=(r // tr,),
            in_specs=[pl.BlockSpec((N_DEV, tr, c), lambda i: (0, i, 0))] + [_rows(tr, c)] * 3,
            out_specs=[_rows(tr, c)] * 4, out_shape=out_shape, compiler_params=_params(),
        )(parts, w, m, v)
    rows = pl.BlockSpec((tr, c), lambda i, me: (i, 0))
    return pl.pallas_call(
        body, name="adamw_own", out_shape=out_shape, compiler_params=_params(),
        grid_spec=pltpu.PrefetchScalarGridSpec(
            num_scalar_prefetch=1, grid=(r // tr,),
            in_specs=[pl.BlockSpec((N_DEV, tr, c), lambda i, me: (0, i, 0)),
                      pl.BlockSpec((None, tr, c), lambda i, me: (me[0], i, 0))] + [rows] * 3,
            out_specs=[rows] * 4),
    )(_my_index().reshape(1).astype(jnp.int32), parts, sent, w, m, v)


def _adam_step(g, w, m, v):
    m_new = ADAM_B1 * m + (1.0 - ADAM_B1) * g
    v_new = ADAM_B2 * v + (1.0 - ADAM_B2) * (g * g)
    m_hat = m_new / (1.0 - ADAM_B1 ** ADAM_STEP)
    v_hat = v_new / (1.0 - ADAM_B2 ** ADAM_STEP)
    return -ADAM_LR * (m_hat / (jnp.sqrt(v_hat) + ADAM_EPS) + ADAM_WD * w), m_new, v_new


SMALL_NAMES = ("norm_mix", "pool_w", "pool_b", "pool_scale", "attn_sinks", "norm_mlp", "norm_final")


def _adamw_small(mlp_all, mix_all, scale_all, nmix_all, w, m, v):
    def body(mlp_ref, mix_ref, scale_ref, nmix_ref, *refs):
        ins, outs = refs[:21], refs[21:]

        def total(ref, rows, lanes=slice(None)):
            g = ref[0, rows, lanes]
            for k in range(1, N_DEV):
                g = g + ref[k, rows, lanes]
            return g

        grads = dict(
            norm_mix=total(nmix_ref, slice(0, 1)), pool_w=total(mix_ref, slice(0, MIX_POOL_B)),
            pool_b=total(mix_ref, slice(MIX_POOL_B, MIX_POOL_B + 4)), pool_scale=total(scale_ref, slice(0, 1)),
            attn_sinks=total(mix_ref, slice(MIX_SINKS, MIX_SINKS + 1)),
            norm_mlp=total(mlp_ref, slice(0, 1)), norm_final=total(mlp_ref, slice(1, 2)))
        for i, name in enumerate(SMALL_NAMES):
            g = grads[name]
            d, m_new, v_new = _adam_step(g, ins[3 * i][...], ins[3 * i + 1][...], ins[3 * i + 2][...])
            for ref, val in zip(outs[4 * i:4 * i + 4], (g, d, m_new, v_new)):
                ref[...] = val
        outs[28][...] = jnp.broadcast_to(total(mlp_ref, slice(2, 3), slice(0, LANES)), (8, LANES))

    operands, out_shape = [], []
    for name in SMALL_NAMES:
        operands += [w[name], m[name], v[name]]
        out_shape += [jax.ShapeDtypeStruct(w[name].shape, F32)] * 4
    out_shape.append(jax.ShapeDtypeStruct((8, LANES), F32))
    outs = pl.pallas_call(body, name="adamw_small", out_shape=out_shape)(
        mlp_all, mix_all, scale_all, nmix_all, *operands)
    return {name: outs[4 * i:4 * i + 4] for i, name in enumerate(SMALL_NAMES)}, outs[28]


def kernel(x, norm_mix, w_in, pool_w, pool_b, pool_scale, attn_sinks, p_pool, p_attn, w_out, norm_mlp, w_up, w_down, norm_final, loss_target, m_norm_mix, m_w_in, m_pool_w, m_pool_b, m_pool_scale, m_attn_sinks, m_p_pool, m_p_attn, m_w_out, m_norm_mlp, m_w_up, m_w_down, m_norm_final, v_norm_mix, v_w_in, v_pool_w, v_pool_b, v_pool_scale, v_attn_sinks, v_p_pool, v_p_attn, v_w_out, v_norm_mlp, v_w_up, v_w_down, v_norm_final):
    xs = x[0]
    tgt = loss_target[0]
    s_len = xs.shape[0]

    p_pool_bf, p_attn_bf, w_out_bf, w_up_bf, w_down_bf = [
        t[0].astype(BF16) for t in (p_pool, p_attn, w_out, w_up, w_down)]
    w_in_bf = w_in[0].T.astype(BF16)
    (w_in_g,) = _all_gather_weights("all_gather_w_in", [w_in_bf])
    ag_rest = _exchange_start(
        "ag_rest_start", [p_pool_bf, p_attn_bf, w_out_bf, w_up_bf, w_down_bf], ("chip",) * 5, w_in_g)

    pool_w_bf = pool_w[0].astype(BF16)
    pool_b_row = pool_b[0].reshape(1, POOL_WIDTH)
    bias_t, sink_row = _attn_constants(attn_sinks[0])

    w_in_t = w_in_g.reshape(IN_WIDTH, D_MODEL)
    u, zp, q, kv, zg = _fwd_in(ag_rest["token"], xs, norm_mix, w_in_t)
    n_mixer_tiles = s_len // min(512, s_len)
    half = min(n_mixer_tiles - 1, n_mixer_tiles // 2 + 1)
    mixer_args = (zp, q, kv, pool_w_bf, pool_b_row, pool_scale, bias_t, sink_row)
    pm, o = _mixers_fwd(*mixer_args, n_tiles=half) if half else (None, None)
    first_level, _ = _exchange_wait("ag_rest_wait", ag_rest, zg if o is None else o)
    relay = _relay_start("ag_relay_start", first_level, zg)
    pm, o = _mixers_fwd(*mixer_args, first_tile=half, earlier=None if o is None else (relay["token"], pm, o))
    p_pool_g, p_attn_g, w_out_g, w_up_g, w_down_g = _relay_wait("ag_relay_wait", relay, o)
    p_pool_f = p_pool_g.transpose(1, 0, 2).reshape(POOL_WIDTH, D_MODEL)
    p_attn_f = p_attn_g.transpose(1, 0, 2).reshape(ATTN_WIDTH, D_MODEL)
    w_out_f = w_out_g.reshape(D_MODEL, D_MODEL)
    w_down_f = w_down_g.reshape(D_FF, D_MODEL)
    mixed, dh1, a, dapre, u2, dh2, small_mlp, dyp, dya, dzg, dpm, do, dh1_bf = _core(
        xs, pm, o, zg, tgt, norm_mlp, norm_final.reshape(1, D_MODEL), p_pool_f, p_attn_f, w_out_f, w_up_g, w_down_f)
    gw_down = _tn_matmul(a, dh2, square_a=True)
    gw_up = _tn_matmul(u2, dapre, col_blocks=N_DEV)
    ex_mlp = _exchange_start(
        "ex_mlp_start", [gw_up, gw_down.reshape(N_DEV, D_FF // N_DEV, D_MODEL)], (True, True), small_mlp)
    dzp, dq, dkv, small_mix, g_pool_scale = _mixers_bwd(
        ex_mlp["token"], zp, q, kv, dpm, do, pool_w_bf, pool_b_row, pool_scale, bias_t, sink_row)
    gw_in = _tn_w_in(u, dzp, dq, dkv, dzg)
    ex_in = _exchange_start(
        "ex_in_start", [gw_in, small_mlp, small_mix, g_pool_scale], (True, False, False, False), dq)
    gw_out = _tn_matmul(mixed, dh1_bf, after=ex_in["token"])
    gp_pool = _tn_matmul(pm, dyp, col_blocks=N_DEV, after=ex_in["token"])
    gp_attn = _tn_matmul(o, dya, col_blocks=N_DEV, after=ex_in["token"])
    ex_proj = _exchange_start(
        "ex_proj_start", [gp_pool, gp_attn, gw_out.reshape(N_DEV, D_MODEL // N_DEV, D_MODEL)], (True,) * 3,
        ex_in["token"])
    dx, g_norm_mix = _in_bwd(ex_proj["token"], dzp, dq, dkv, dzg, w_in_t, xs, dh1, norm_mix)

    big_w = dict(w_in=w_in, p_pool=p_pool, p_attn=p_attn, w_out=w_out, w_up=w_up, w_down=w_down)
    big_m = dict(w_in=m_w_in, p_pool=m_p_pool, p_attn=m_p_attn, w_out=m_w_out, w_up=m_w_up, w_down=m_w_down)
    big_v = dict(w_in=v_w_in, p_pool=v_p_pool, p_attn=v_p_attn, w_out=v_w_out, w_up=v_w_up, w_down=v_w_down)
    res = {}

    def update(names, recvs, sents):
        for name, parts, sent in zip(names, recvs, sents):
            flip = (lambda t: t.T) if name == "w_in" else (lambda t: t)
            outs = _adamw(parts, flip(big_w[name][0]), flip(big_m[name][0]), flip(big_v[name][0]), sent)
            res[name] = [flip(t)[None] for t in outs]

    update(["w_up", "w_down"], *_exchange_wait("ex_mlp_wait", ex_mlp, dx))
    (norm_mix_all,) = _all_gather_weights("all_gather_norm_mix", [g_norm_mix], res["w_down"][0])
    (r_in, mlp_all, mix_all, scale_all), (s_in, _, _, _) = _exchange_wait("ex_in_wait", ex_in, norm_mix_all)
    update(["w_in"], [r_in], [s_in])

    natural = dict(norm_mix=(1, D_MODEL), pool_w=(MIX_POOL_B, LANES), pool_b=(4, LANES), pool_scale=(1, POOL_WIDTH),
                   attn_sinks=(1, LANES), norm_mlp=(1, D_MODEL), norm_final=(1, D_MODEL))

    def as_2d(t, name):
        if name == "attn_sinks":
            return jnp.pad(t, ((0, 0), (0, LANES - N_HEADS)))
        return t.reshape(natural[name])

    small_w = dict(norm_mix=norm_mix, pool_w=pool_w, pool_b=pool_b, pool_scale=pool_scale, attn_sinks=attn_sinks,
                   norm_mlp=norm_mlp, norm_final=norm_final)
    small_m = dict(norm_mix=m_norm_mix, pool_w=m_pool_w, pool_b=m_pool_b, pool_scale=m_pool_scale,
                   attn_sinks=m_attn_sinks, norm_mlp=m_norm_mlp, norm_final=m_norm_final)
    small_v = dict(norm_mix=v_norm_mix, pool_w=v_pool_w, pool_b=v_pool_b, pool_scale=v_pool_scale,
                   attn_sinks=v_attn_sinks, norm_mlp=v_norm_mlp, norm_final=v_norm_final)
    small_res, loss_all = _adamw_small(
        mlp_all, mix_all, scale_all, norm_mix_all,
        *[{k: as_2d(t, k) for k, t in d.items()} for d in (small_w, small_m, small_v)])
    loss = loss_all[0, 0]
    for name in SMALL_NAMES:
        shape = small_w[name].shape
        res[name] = [(t[:, :N_HEADS] if name == "attn_sinks" else t).reshape(shape) for t in small_res[name]]
    update(["p_pool", "p_attn", "w_out"], *_exchange_wait("ex_proj_wait", ex_proj, loss_all))

    order = ["norm_mix", "w_in", "pool_w", "pool_b", "pool_scale", "attn_sinks", "p_pool", "p_attn", "w_out",
             "norm_mlp", "w_up", "w_down", "norm_final"]
    out = [loss, dx[None]]
    for kind in range(4):
        out += [res[name][kind] for name in order]
    return tuple(out)
```

```python
import functools
import math

import numpy as np
import jax
import jax.numpy as jnp
from jax import lax
from jax.experimental import pallas as pl
from jax.experimental.pallas import tpu as pltpu

F32 = jnp.float32
BF16 = jnp.bfloat16

D_MODEL = 1024
POOL_WIDTH = 512
ATTN_WIDTH = 512
KV_WIDTH = 128
HEAD_DIM = 64
N_HEADS = 8
N_KV_HEADS = 2
GROUP = 4
BLOCK = 128
POOL_WINDOWS = (2, 4, 8, 16)
POOL_GROUP_DIM = 128
POOL_HALO = 16
D_FF = 4096
FF_CHUNK = 1024
IN_WIDTH = 3328
RMS_EPS = 1e-5
NEG_INF = -1e30
ATTN_SCALE = 1.0 / math.sqrt(HEAD_DIM)
N_DEV = 8

ADAM_LR = 0.001
ADAM_B1 = 0.9
ADAM_B2 = 0.999
ADAM_EPS = 1e-08
ADAM_WD = 0.01
ADAM_STEP = 10

LANES = 128
VMEM_LIMIT_BYTES = 56 * 1024 * 1024
MESH = pl.DeviceIdType.MESH


def _params(n_grid_axes=1):
    return pltpu.CompilerParams(
        dimension_semantics=("arbitrary",) * n_grid_axes, vmem_limit_bytes=VMEM_LIMIT_BYTES)


def _dot(a, b):
    return jnp.dot(a, b, preferred_element_type=F32)


def _dot_nt(a, b):
    return lax.dot_general(a, b, (((1,), (1,)), ((), ())), preferred_element_type=F32)


def _dot_tn(a, b):
    return lax.dot_general(a, b, (((0,), (0,)), ((), ())), preferred_element_type=F32)


ANY = pl.BlockSpec(memory_space=pl.ANY)


def _rows(tm, n):
    return pl.BlockSpec((tm, n), lambda i: (i, 0))


def _whole(shape):
    zeros = (0,) * len(shape)
    return pl.BlockSpec(shape, lambda i: zeros)


def _rms_fwd(h, g):
    r = lax.rsqrt(jnp.mean(h * h, axis=-1, keepdims=True) + RMS_EPS)
    xh = h * r
    return r, xh, xh * g


def _rms_bwd(dy, xh, r, g):
    dxh = dy * g
    dh = r * (dxh - xh * jnp.mean(dxh * xh, axis=-1, keepdims=True))
    return dh, jnp.sum(dy * xh, axis=0, keepdims=True)


def _fwd_in(after, x, g_mix, w_in_t):
    s_len = x.shape[0]
    tm = min(512, s_len)

    def body(after_ref, x_ref, g_ref, w_ref, u_ref, zp_ref, q_ref, kv_ref, zg_ref):
        _, _, u = _rms_fwd(x_ref[...], g_ref[...])
        u = u.astype(BF16)
        u_ref[...] = u
        zp_ref[...] = _dot_nt(u, w_ref[0:512, :]).astype(BF16)
        q_ref[...] = _dot_nt(u, w_ref[512:1024, :]).astype(BF16)
        kv_ref[...] = _dot_nt(u, w_ref[1024:1280, :]).astype(BF16)
        zg_ref[...] = _dot_nt(u, w_ref[1280:3328, :]).astype(BF16)

    return pl.pallas_call(
        body, name="fwd_in", grid=(s_len // tm,),
        in_specs=[ANY, _rows(tm, D_MODEL), _whole((1, D_MODEL)),
                  pl.BlockSpec((IN_WIDTH, D_MODEL), lambda i: (0, 0), pipeline_mode=pl.Buffered(1))],
        out_specs=[_rows(tm, D_MODEL), _rows(tm, 512), _rows(tm, 512), _rows(tm, 256), _rows(tm, 2048)],
        out_shape=[jax.ShapeDtypeStruct((s_len, n), BF16) for n in (D_MODEL, 512, 512, 256, 2048)],
        compiler_params=_params(),
    )(after, x, g_mix, w_in_t)


def _attn_constants(sinks):
    r = np.arange(BLOCK)[:, None]
    qi = np.arange(BLOCK)[None, :]
    dist = np.where(r <= qi, qi - r, BLOCK + qi - r).astype(np.float32)
    slopes = np.array([2.0 ** (-8.0 * (h + 1) / N_HEADS) for h in range(N_HEADS)], dtype=np.float32)
    bias = (-slopes[:, None, None] * dist[None]).reshape(N_KV_HEADS, GROUP, BLOCK, BLOCK)
    bias = np.ascontiguousarray(bias.transpose(0, 2, 1, 3)).reshape(N_KV_HEADS, BLOCK, GROUP * BLOCK)
    sink_row = jnp.repeat(sinks.astype(F32).reshape(N_KV_HEADS, GROUP), BLOCK, axis=1)[:, None, :]
    return jnp.asarray(bias.astype(np.float32)), sink_row


def _own_block_mask():
    shape = (BLOCK, GROUP * BLOCK)
    r = lax.broadcasted_iota(jnp.int32, shape, 0)
    qi = lax.broadcasted_iota(jnp.int32, shape, 1) & (BLOCK - 1)
    return r <= qi


def _pack_keys(t, own):
    return jnp.where(own, t[BLOCK:], t[:BLOCK])


def _unpack_keys(t, own):
    zero = jnp.zeros_like(t)
    return jnp.concatenate([jnp.where(own, zero, t), jnp.where(own, t, zero)], axis=0)


def _left_half(shape):
    return lax.broadcasted_iota(jnp.int32, shape, 1) < HEAD_DIM


def _dup_halves(slab):
    swapped = pltpu.roll(slab, HEAD_DIM, 1)
    left = _left_half(slab.shape)
    return jnp.where(left, slab, swapped), jnp.where(left, swapped, slab)


def _fill_kv_slabs(kvh_ref, kv_ref, ka_ref, vd_ref):
    for rows, src in ((slice(0, BLOCK), kvh_ref), (slice(BLOCK, None), kv_ref)):
        kvf = src[...].astype(F32)
        for ref, lanes in ((ka_ref, slice(0, KV_WIDTH)), (vd_ref, slice(KV_WIDTH, 2 * KV_WIDTH))):
            d0, d1 = _dup_halves(kvf[:, lanes])
            ref[0, rows, :] = d0.astype(BF16)
            ref[1, rows, :] = d1.astype(BF16)


def _stack_pairs(a, h):
    pieces = []
    for j in range(2):
        pair = a[:, h * 256 + j * LANES:h * 256 + (j + 1) * LANES]
        left = _left_half(pair.shape)
        zero = jnp.zeros_like(pair)
        pieces += [jnp.where(left, pair, zero), jnp.where(left, zero, pair)]
    return jnp.concatenate(pieces, axis=0)


def _attn_probs(kk, q_st, bias_p, sink_row, own, first):
    s = _pack_keys(_dot_nt(kk, q_st), own) * ATTN_SCALE + bias_p
    if first is not None:
        s = jnp.where(jnp.logical_and(first, jnp.logical_not(own)), NEG_INF, s)
    m = jnp.maximum(jnp.max(s, axis=0, keepdims=True), sink_row)
    p = jnp.exp(s - m)
    es = jnp.exp(sink_row - m)
    inv = 1.0 / (jnp.sum(p, axis=0, keepdims=True) + es)
    return p * inv, es * inv


def _pool_d(ext, cur, g, row0):
    w = POOL_WINDOWS[g]
    acc = ext
    k = 1
    while k < w:
        acc = acc + pltpu.roll(acc, k, 0)
        k *= 2
    return _window_mean(acc[POOL_HALO:, :], w, row0) - cur


def _window_mean(total, w, row0):
    t = row0 + lax.broadcasted_iota(jnp.int32, (POOL_HALO, total.shape[1]), 0)
    head = total[:POOL_HALO] / jnp.minimum(t + 1, w).astype(F32)
    return jnp.concatenate([head, total[POOL_HALO:] * (1.0 / w)], axis=0)


def _mixers_fwd(zp, q, kv, pool_w, pool_b, pool_scale, bias_t, sink_row, first_tile=0, n_tiles=None, earlier=None):
    s_len = zp.shape[0]
    tq = min(512, s_len)
    nb = tq // BLOCK
    n_tiles = s_len // tq - first_tile if n_tiles is None else n_tiles
    extra = [] if earlier is None else list(earlier)

    def body(zp_ref, zph_ref, q_ref, kv_ref, kvh_ref, pw_ref, pb_ref, ps_ref, bias_ref, sink_ref, *rest):
        pm_ref, o_ref, ka_ref, vd_ref = rest[len(extra):]
        i = pl.program_id(0) + first_tile
        cur = zp_ref[...].astype(F32)
        halo = zph_ref[...].astype(F32) * (i > 0).astype(F32)
        ext = jnp.concatenate([halo, cur], axis=0)
        for g in range(4):
            sl = slice(g * POOL_GROUP_DIM, (g + 1) * POOL_GROUP_DIM)
            d = _pool_d(ext[:, sl], cur[:, sl], g, i * tq)
            y = _dot(d.astype(BF16), pw_ref[g]) + pb_ref[:, sl]
            pm_ref[:, sl] = (y * ps_ref[:, sl]).astype(BF16)
        _fill_kv_slabs(kvh_ref, kv_ref, ka_ref, vd_ref)
        own = _own_block_mask()
        for b in range(nb):
            rq = slice(b * BLOCK, (b + 1) * BLOCK)
            rk = slice(b * BLOCK, (b + 2) * BLOCK)
            qb = q_ref[rq, :]
            for h in range(N_KV_HEADS):
                pn, _ = _attn_probs(ka_ref[h, rk, :], _stack_pairs(qb, h), bias_ref[h], sink_ref[h], own,
                                    (i == 0) if b == 0 else None)
                pn = _unpack_keys(pn, own).astype(BF16)
                vd = vd_ref[h, rk, :]
                left = _left_half(vd.shape)
                zero = jnp.zeros_like(vd)
                va, vb = jnp.where(left, vd, zero), jnp.where(left, zero, vd)
                for j in range(2):
                    o_pair = (_dot_tn(pn[:, (2 * j) * BLOCK:(2 * j + 1) * BLOCK], va)
                              + _dot_tn(pn[:, (2 * j + 1) * BLOCK:(2 * j + 2) * BLOCK], vb))
                    o_ref[rq, h * 256 + j * LANES:h * 256 + (j + 1) * LANES] = o_pair.astype(BF16)

    def tile(n):
        return pl.BlockSpec((tq, n), lambda i: (i + first_tile, 0))

    halo_pool = pl.BlockSpec(
        (POOL_HALO, 512), lambda i: (jnp.maximum((i + first_tile) * (tq // POOL_HALO) - 1, 0), 0))
    halo_kv = pl.BlockSpec((BLOCK, 256), lambda i: (jnp.maximum((i + first_tile) * nb - 1, 0), 0))
    return pl.pallas_call(
        body, name="mixers_fwd", grid=(n_tiles,),
        in_specs=[tile(512), halo_pool, tile(512), tile(256), halo_kv,
                  _whole((4, 128, 128)), _whole((1, 512)), _whole((1, 512)),
                  _whole((N_KV_HEADS, BLOCK, GROUP * BLOCK)), _whole((N_KV_HEADS, 1, GROUP * BLOCK))]
        + [ANY] * len(extra),
        out_specs=[tile(512), tile(512)],
        out_shape=[jax.ShapeDtypeStruct((s_len, 512), BF16)] * 2,
        input_output_aliases={11: 0, 12: 1} if extra else {},
        scratch_shapes=[pltpu.VMEM((N_KV_HEADS, tq + BLOCK, LANES), BF16)] * 2,
        compiler_params=_params(),
    )(zp, zp, q, kv, kv, pool_w, pool_b, pool_scale, bias_t, sink_row, *extra)


def _gated_mix(pm, o, zg, pp_ref, pa_ref):
    yp = _dot(pm, pp_ref[...])
    ya = _dot(o, pa_ref[...])
    gp = jax.nn.sigmoid(zg[:, :D_MODEL].astype(F32))
    ga = jax.nn.sigmoid(zg[:, D_MODEL:].astype(F32))
    return yp, ya, gp, ga


def _core(x, pm, o, zg, tgt, g_mlp, g_fin, p_pool, p_attn, w_out, w_up_blocks, w_down):
    s_len = x.shape[0]
    tm = min(256, s_len)
    n_chunks = D_FF // FF_CHUNK
    up_block = D_FF // N_DEV
    per_chunk = FF_CHUNK // up_block

    def body(x_ref, pm_ref, o_ref, zg_ref, tgt_ref, gm_ref, gf_ref, pp_ref, pa_ref, wo_ref, wu_ref, wd_ref,
             mixed_ref, dh1_ref, a_ref, dap_ref, u2_ref, dh2_ref, small_ref,
             dyp_ref, dya_ref, dzg_ref, dpm_ref, do_ref, dh1b_ref):
        i = pl.program_id(0)

        @pl.when(i == 0)
        def _():
            small_ref[...] = jnp.zeros_like(small_ref)

        yp, ya, gp, ga = _gated_mix(pm_ref[...], o_ref[...], zg_ref[...], pp_ref, pa_ref)
        mixed = (gp * yp + ga * ya).astype(BF16)
        mixed_ref[...] = mixed
        h1 = x_ref[...] + _dot(mixed, wo_ref[...])
        r2, xh2, u2 = _rms_fwd(h1, gm_ref[...])
        u2 = u2.astype(BF16)
        u2_ref[...] = u2
        acc = jnp.zeros((tm, D_MODEL), F32)
        for c in range(n_chunks):
            cs = slice(c * FF_CHUNK, (c + 1) * FF_CHUNK)
            a = jnp.concatenate([_dot(u2, wu_ref[per_chunk * c + j]) for j in range(per_chunk)], axis=1)
            a = jnp.maximum(a, 0.0)
            a_ref[:, cs] = a.astype(BF16)
            acc = acc + _dot((a * a).astype(BF16), wd_ref[cs, :])
        h2 = h1 + acc
        r3, xh3, y = _rms_fwd(h2, gf_ref[...])
        diff = y - tgt_ref[...]
        small_ref[2:3, :] += 0.5 * jnp.sum(jnp.mean(diff * diff, axis=-1, keepdims=True))
        dy = diff * (1.0 / D_MODEL)
        dh2, dgf = _rms_bwd(dy, xh3, r3, gf_ref[...])
        small_ref[1:2, :] += dgf
        dh2_bf = dh2.astype(BF16)
        dh2_ref[...] = dh2_bf
        du2 = jnp.zeros((tm, D_MODEL), F32)
        for c in range(n_chunks):
            cs = slice(c * FF_CHUNK, (c + 1) * FF_CHUNK)
            ds = _dot_nt(dh2_bf, wd_ref[cs, :])
            dap = (ds * (2.0 * a_ref[:, cs].astype(F32))).astype(BF16)
            dap_ref[:, cs] = dap
            for j in range(per_chunk):
                du2 = du2 + _dot_nt(dap[:, j * up_block:(j + 1) * up_block], wu_ref[per_chunk * c + j])
        dh1n, dgm = _rms_bwd(du2, xh2, r2, gm_ref[...])
        small_ref[0:1, :] += dgm
        dh1 = dh2 + dh1n
        dh1_ref[...] = dh1
        dh1_bf = dh1.astype(BF16)
        dh1b_ref[...] = dh1_bf
        dm = _dot_nt(dh1_bf, wo_ref[...])
        dyp = (dm * gp).astype(BF16)
        dya = (dm * ga).astype(BF16)
        dyp_ref[...] = dyp
        dya_ref[...] = dya
        dzg_ref[:, :D_MODEL] = (dm * yp * (gp * (1.0 - gp))).astype(BF16)
        dzg_ref[:, D_MODEL:] = (dm * ya * (ga * (1.0 - ga))).astype(BF16)
        dpm_ref[...] = _dot_nt(dyp, pp_ref[...]).astype(BF16)
        do_ref[...] = _dot_nt(dya, pa_ref[...]).astype(BF16)

    def fixed(shape):
        return pl.BlockSpec(shape, lambda i: (0,) * len(shape), pipeline_mode=pl.Buffered(1))

    widths_dtypes = ((D_MODEL, BF16), (D_MODEL, F32), (D_FF, BF16), (D_FF, BF16), (D_MODEL, BF16), (D_MODEL, BF16))
    back = ((D_MODEL, BF16), (D_MODEL, BF16), (2048, BF16), (512, BF16), (512, BF16), (D_MODEL, BF16))
    return pl.pallas_call(
        body, name="core", grid=(s_len // tm,),
        in_specs=[_rows(tm, D_MODEL), _rows(tm, 512), _rows(tm, 512), _rows(tm, 2048), _rows(tm, D_MODEL),
                  _whole((1, D_MODEL)), _whole((1, D_MODEL)),
                  fixed((512, D_MODEL)), fixed((512, D_MODEL)), fixed((D_MODEL, D_MODEL)),
                  fixed((N_DEV, D_MODEL, up_block)), fixed((D_FF, D_MODEL))],
        out_specs=[_rows(tm, n) for n, _ in widths_dtypes] + [_whole((8, D_MODEL))] + [_rows(tm, n) for n, _ in back],
        out_shape=[jax.ShapeDtypeStruct((s_len, n), d) for n, d in widths_dtypes]
        + [jax.ShapeDtypeStruct((8, D_MODEL), F32)] + [jax.ShapeDtypeStruct((s_len, n), d) for n, d in back],
        compiler_params=_params(),
    )(x, pm, o, zg, tgt, g_mlp, g_fin, p_pool, p_attn, w_out, w_up_blocks, w_down)


def _tn_matmul(a, b, square_a=False, col_blocks=None, after=None):
    s_len, ka = a.shape
    nb = b.shape[1]
    tt = min(2048, s_len)
    tk = min(1024, ka)
    tn = min(1024, nb)
    n_t = s_len // tt
    if col_blocks is None:
        out_spec = pl.BlockSpec((tk, tn), lambda k, j, t: (k, j))
        out_shape = jax.ShapeDtypeStruct((ka, nb), BF16)
    else:
        width = nb // col_blocks
        per_tile = tn // width
        out_spec = pl.BlockSpec((per_tile, tk, width), lambda k, j, t: (j, k, 0))
        out_shape = jax.ShapeDtypeStruct((col_blocks, ka, width), BF16)

    extra = [] if after is None else [after]

    def body(a_ref, b_ref, *rest):
        o_ref, acc_ref = rest[len(extra):]
        t = pl.program_id(2)

        @pl.when(t == 0)
        def _():
            acc_ref[...] = jnp.zeros_like(acc_ref)

        av = a_ref[...]
        if square_a:
            av = av * av
        acc_ref[...] += _dot_tn(av.astype(BF16), b_ref[...].astype(BF16))

        @pl.when(t == n_t - 1)
        def _():
            if col_blocks is None:
                o_ref[...] = acc_ref[...].astype(o_ref.dtype)
            else:
                for blk in range(per_tile):
                    o_ref[blk] = acc_ref[:, blk * width:(blk + 1) * width].astype(o_ref.dtype)

    return pl.pallas_call(
        body, name="tn_matmul", grid=(ka // tk, nb // tn, n_t),
        in_specs=[pl.BlockSpec((tt, tk), lambda k, j, t: (t, k)), pl.BlockSpec((tt, tn), lambda k, j, t: (t, j))]
        + [ANY] * len(extra),
        out_specs=out_spec, out_shape=out_shape,
        scratch_shapes=[pltpu.VMEM((tk, tn), F32)],
        compiler_params=_params(3),
    )(a, b, *extra)


def _tn_w_in(u, dzp, dq, dkv, dzg):
    s_len = u.shape[0]
    tt = min(1024, s_len)
    n_t = s_len // tt
    width = IN_WIDTH // N_DEV
    pieces = ((0, 512), (512, 1024), (1024, 1280), (1280, IN_WIDTH))

    def body(u_ref, dzp_ref, dq_ref, dkv_ref, dzg_ref, o_ref, acc_ref):
        t = pl.program_id(0)

        @pl.when(t == 0)
        def _():
            acc_ref[...] = jnp.zeros_like(acc_ref)

        uv = u_ref[...]
        for (c0, c1), ref in zip(pieces, (dzp_ref, dq_ref, dkv_ref, dzg_ref)):
            acc_ref[c0:c1, :] += _dot_tn(ref[...], uv)

        @pl.when(t == n_t - 1)
        def _():
            for j in range(N_DEV):
                o_ref[j] = acc_ref[j * width:(j + 1) * width, :].astype(BF16)

    return pl.pallas_call(
        body, name="tn_w_in", grid=(n_t,),
        in_specs=[_rows(tt, D_MODEL)] + [_rows(tt, c1 - c0) for c0, c1 in pieces],
        out_specs=_whole((N_DEV, width, D_MODEL)),
        out_shape=jax.ShapeDtypeStruct((N_DEV, width, D_MODEL), BF16),
        scratch_shapes=[pltpu.VMEM((IN_WIDTH, D_MODEL), F32)],
        compiler_params=_params(),
    )(u, dzp, dq, dkv, dzg)


MIX_POOL_B = 4 * POOL_GROUP_DIM
MIX_SINKS = MIX_POOL_B + 8
MIX_ROWS = MIX_SINKS + 8


def _mixers_bwd(after, zp, q, kv, dpm, do, pool_w, pool_b, pool_scale, bias_t, sink_row):
    s_len = zp.shape[0]
    tq = min(512, s_len)
    nb = tq // BLOCK
    n_steps = s_len // tq

    def body(after_ref, zp_ref, zph_ref, q_ref, kv_ref, kvh_ref, dpm_ref, dpmh_ref, do_ref, pw_ref, pb_ref, ps_ref,
             bias_ref, sink_ref, dzp_ref, dq_ref, dkv_ref, small_ref, dps_ref,
             ka_ref, vd_ref, dsk_acc, dkv_acc):
        i = pl.program_id(0)

        @pl.when(i == 0)
        def _():
            dkv_acc[...] = jnp.zeros_like(dkv_acc)
            small_ref[...] = jnp.zeros_like(small_ref)
            dps_ref[...] = jnp.zeros_like(dps_ref)
            dsk_acc[...] = jnp.zeros_like(dsk_acc)

        cur = zp_ref[...].astype(F32)
        halo = zph_ref[...].astype(F32) * (i > 0).astype(F32)
        ext = jnp.concatenate([halo, cur], axis=0)
        dpm_next = dpmh_ref[...].astype(F32) * (i < n_steps - 1).astype(F32)
        dpm_ext = jnp.concatenate([dpm_ref[...].astype(F32), dpm_next], axis=0)
        n_ext = tq + POOL_HALO
        for g in range(4):
            sl = slice(g * POOL_GROUP_DIM, (g + 1) * POOL_GROUP_DIM)
            w = POOL_WINDOWS[g]
            d = _pool_d(ext[:, sl], cur[:, sl], g, i * tq).astype(BF16)
            y_lin = _dot(d, pw_ref[g]) + pb_ref[:, sl]
            dps_ref[:, sl] += jnp.sum(dpm_ext[:tq, sl] * y_lin, axis=0, keepdims=True)
            dyl_ext = dpm_ext[:, sl] * ps_ref[:, sl]
            small_ref[MIX_POOL_B + g:MIX_POOL_B + g + 1, :] += jnp.sum(dyl_ext[:tq], axis=0, keepdims=True)
            dyl_bf = dyl_ext.astype(BF16)
            small_ref[g * POOL_GROUP_DIM:(g + 1) * POOL_GROUP_DIM, :] += _dot_tn(d, dyl_bf[:tq])
            dd = _dot_nt(dyl_bf, pw_ref[g])
            e = _window_mean(dd, w, i * tq)
            acc = e
            k = 1
            while k < w:
                acc = acc + pltpu.roll(acc, n_ext - k, 0)
                k *= 2
            dzp_ref[:, sl] = (acc[:tq] - dd[:tq]).astype(BF16)

        _fill_kv_slabs(kvh_ref, kv_ref, ka_ref, vd_ref)

        def fold(dup):
            return dup + pltpu.roll(dup, HEAD_DIM, 1)

        own = _own_block_mask()
        for b in range(nb):
            rq = slice(b * BLOCK, (b + 1) * BLOCK)
            rk = slice(b * BLOCK, (b + 2) * BLOCK)
            qb = q_ref[rq, :]
            dob = do_ref[rq, :]
            dk_dup, dv_dup = [], []
            for h in range(N_KV_HEADS):
                kk = ka_ref[h, rk, :]
                q_st = _stack_pairs(qb, h)
                do_st = _stack_pairs(dob, h)
                pn, psink = _attn_probs(kk, q_st, bias_ref[h], sink_ref[h], own, (i == 0) if b == 0 else None)
                dp = _pack_keys(_dot_nt(vd_ref[h, rk, :], do_st), own)
                delta = jnp.sum(pn * dp, axis=0, keepdims=True)
                dsk_acc[h] += -psink * delta
                ds = _unpack_keys((pn * (dp - delta)) * ATTN_SCALE, own).astype(BF16)
                pn = _unpack_keys(pn, own)
                dq_st = _dot_tn(ds, kk)
                for j in range(2):
                    left = _left_half((BLOCK, LANES))
                    dq_pair = jnp.where(left, dq_st[(2 * j) * BLOCK:(2 * j + 1) * BLOCK],
                                        dq_st[(2 * j + 1) * BLOCK:(2 * j + 2) * BLOCK])
                    dq_ref[rq, h * 256 + j * LANES:h * 256 + (j + 1) * LANES] = dq_pair.astype(BF16)
                dk_dup.append(fold(_dot(ds, q_st)))
                dv_dup.append(fold(_dot(pn.astype(BF16), do_st)))
            left = _left_half((2 * BLOCK, LANES))
            dkv_blk = jnp.concatenate([jnp.where(left, dk_dup[0], dk_dup[1]),
                                       jnp.where(left, dv_dup[0], dv_dup[1])], axis=1)
            g0 = pl.multiple_of(i * tq + b * BLOCK, BLOCK)
            dkv_acc[pl.ds(g0, 2 * BLOCK), :] += dkv_blk

        @pl.when(i == n_steps - 1)
        def _():
            dkv_ref[...] = dkv_acc[BLOCK:, :].astype(BF16)
            lane = lax.broadcasted_iota(jnp.int32, (1, LANES), 1)
            row = jnp.zeros((1, LANES), F32)
            for h in range(N_KV_HEADS):
                for g in range(GROUP):
                    tot = jnp.sum(dsk_acc[h, :, g * BLOCK:(g + 1) * BLOCK], axis=1, keepdims=True)
                    row = jnp.where(lane == GROUP * h + g, tot, row)
            small_ref[MIX_SINKS:MIX_SINKS + 1, :] = row

    blocks_per_tile = tq // POOL_HALO
    last_halo = s_len // POOL_HALO - 1
    halo_prev = pl.BlockSpec((POOL_HALO, 512), lambda i: (jnp.maximum(i * blocks_per_tile - 1, 0), 0))
    halo_next = pl.BlockSpec((POOL_HALO, 512), lambda i: (jnp.minimum((i + 1) * blocks_per_tile, last_halo), 0))
    halo_kv = pl.BlockSpec((BLOCK, 256), lambda i: (jnp.maximum(i * nb - 1, 0), 0))
    return pl.pallas_call(
        body, name="mixers_bwd", grid=(n_steps,),
        in_specs=[ANY, _rows(tq, 512), halo_prev, _rows(tq, 512), _rows(tq, 256), halo_kv,
                  _rows(tq, 512), halo_next, _rows(tq, 512),
                  _whole((4, 128, 128)), _whole((1, 512)), _whole((1, 512)),
                  _whole((N_KV_HEADS, BLOCK, GROUP * BLOCK)), _whole((N_KV_HEADS, 1, GROUP * BLOCK))],
        out_specs=[_rows(tq, 512), _rows(tq, 512), _whole((s_len, 256)),
                   _whole((MIX_ROWS, LANES)), _whole((1, 512))],
        out_shape=[jax.ShapeDtypeStruct((s_len, 512), BF16), jax.ShapeDtypeStruct((s_len, 512), BF16),
                   jax.ShapeDtypeStruct((s_len, 256), BF16), jax.ShapeDtypeStruct((MIX_ROWS, LANES), F32),
                   jax.ShapeDtypeStruct((1, 512), F32)],
        scratch_shapes=[pltpu.VMEM((N_KV_HEADS, tq + BLOCK, LANES), BF16)] * 2
        + [pltpu.VMEM((N_KV_HEADS, 1, GROUP * BLOCK), F32), pltpu.VMEM((s_len + BLOCK, 256), F32)],
        compiler_params=_params(),
    )(after, zp, zp, q, kv, kv, dpm, dpm, do, pool_w, pool_b, pool_scale, bias_t, sink_row)


def _in_bwd(after, dzp, dq, dkv, dzg, w_in_t, x, dh1, g_mix):
    s_len = x.shape[0]
    tm = min(512, s_len)

    def body(after_ref, dzp_ref, dq_ref, dkv_ref, dzg_ref, w_ref, x_ref, dh1_ref, g_ref, dx_ref, dg_ref):
        i = pl.program_id(0)

        @pl.when(i == 0)
        def _():
            dg_ref[...] = jnp.zeros_like(dg_ref)

        du = _dot(dzp_ref[...], w_ref[0:512, :])
        du = du + _dot(dq_ref[...], w_ref[512:1024, :])
        du = du + _dot(dkv_ref[...], w_ref[1024:1280, :])
        du = du + _dot(dzg_ref[...], w_ref[1280:3328, :])
        r, xh, _ = _rms_fwd(x_ref[...], g_ref[...])
        dxn, dg = _rms_bwd(du, xh, r, g_ref[...])
        dg_ref[...] += dg
        dx_ref[...] = dh1_ref[...] + dxn

    return pl.pallas_call(
        body, name="in_bwd", grid=(s_len // tm,),
        in_specs=[ANY, _rows(tm, 512), _rows(tm, 512), _rows(tm, 256), _rows(tm, 2048), _whole((IN_WIDTH, D_MODEL)),
                  _rows(tm, D_MODEL), _rows(tm, D_MODEL), _whole((1, D_MODEL))],
        out_specs=[_rows(tm, D_MODEL), _whole((1, D_MODEL))],
        out_shape=[jax.ShapeDtypeStruct((s_len, D_MODEL), F32), jax.ShapeDtypeStruct((1, D_MODEL), F32)],
        compiler_params=_params(),
    )(after, dzp, dq, dkv, dzg, w_in_t, x, dh1, g_mix)


def _all_gather_weights(name, shards, after=None):
    n = len(shards)
    extra = [] if after is None else [after]
    n_extra = len(extra)

    def body(*refs):
        ins, outs = refs[:n], refs[n + n_extra:2 * n + n_extra]
        send_sems, recv_sems, local_sems = refs[2 * n + n_extra:]
        x, y, c = lax.axis_index("x"), lax.axis_index("y"), lax.axis_index("c")
        me, sibling = (x, y, c), (x, y, 1 - c)
        chips = [(1 - x, y), (x, 1 - y), (1 - x, 1 - y)]

        def slot(a, px, py, pc):
            return outs[a].at[4 * px + 2 * py + pc]

        def copy(a, k, block, to, src=None):
            return pltpu.make_async_remote_copy(
                src_ref=slot(a, *block) if src is None else src, dst_ref=slot(a, *block),
                send_sem=send_sems.at[a, k], recv_sem=recv_sems.at[a, k], device_id=to, device_id_type=MESH)

        mine = [pltpu.make_async_copy(ins[a], slot(a, *me), local_sems.at[a]) for a in range(n)]
        for cp in mine:
            cp.start()
        first = []
        for a in range(n):
            first.append(copy(a, 0, me, sibling, src=ins[a]))
            first += [copy(a, 1 + j, me, (*chip, c), src=ins[a]) for j, chip in enumerate(chips)]
        for cp in first:
            cp.start()
        passed = []
        for a in range(n):
            for j, chip in enumerate(chips):
                copy(a, 1 + j, (*chip, c), me).wait_recv()
                cp = copy(a, 4 + j, (*chip, c), sibling)
                cp.start()
                passed.append(cp)
        for a in range(n):
            copy(a, 0, sibling, me).wait_recv()
            for j, chip in enumerate(chips):
                copy(a, 4 + j, (*chip, 1 - c), me).wait_recv()
        for cp in first + passed:
            cp.wait_send()
        for cp in mine:
            cp.wait()

    return pl.pallas_call(
        body, name=name,
        in_specs=[ANY] * (n + n_extra), out_specs=[ANY] * n,
        out_shape=[jax.ShapeDtypeStruct((N_DEV,) + s.shape, s.dtype) for s in shards],
        scratch_shapes=[pltpu.SemaphoreType.DMA((n, 7)), pltpu.SemaphoreType.DMA((n, 7)), pltpu.SemaphoreType.DMA((n,))],
    )(*shards, *extra)


HBM_SPEC = pl.BlockSpec(memory_space=pltpu.HBM)
SEM_SPEC = pl.BlockSpec(memory_space=pltpu.SEMAPHORE)
DATAFLOW = pltpu.SideEffectType.DATAFLOW_SIDE_EFFECTING
N_PEERS = N_DEV - 1


CHIP_PEERS = (1, 2, 4, 6)
RELAYED = (2, 4, 6)


def _peer_copies(srcs, lands, scatter, send_sems, recv_sems):
    x, y, c = lax.axis_index("x"), lax.axis_index("y"), lax.axis_index("c")
    me_idx = 4 * x + 2 * y + c
    copies = []
    for k in range(1, N_DEV):
        px = 1 - x if (k >> 2) & 1 else x
        py = 1 - y if (k >> 1) & 1 else y
        pc = 1 - c if k & 1 else c
        p_idx = 4 * px + 2 * py + pc
        for a in range(len(srcs)):
            if scatter[a] == "chip" and k not in CHIP_PEERS:
                continue
            src = srcs[a].at[p_idx] if scatter[a] is True else srcs[a]
            dst = lands[a].at[k] if scatter[a] is True else lands[a].at[me_idx]
            copies.append(pltpu.make_async_remote_copy(
                src_ref=src, dst_ref=dst, send_sem=send_sems.at[a * N_PEERS + k - 1],
                recv_sem=recv_sems.at[a * N_PEERS + k - 1],
                device_id=(px, py, pc), device_id_type=MESH))
    return copies


def _exchange_start(name, srcs, scatter, after):
    n = len(srcs)
    lands = [lax.empty(s.shape if sc is True else (N_DEV,) + s.shape, s.dtype) for s, sc in zip(srcs, scatter)]

    def body(*refs):
        src_refs, land_refs = refs[:n], refs[n:2 * n]
        send_sems, recv_sems = refs[2 * n + 1], refs[2 * n + 2]
        token = refs[4 * n + 3]
        for cp in _peer_copies(src_refs, land_refs, scatter, send_sems, recv_sems):
            cp.start()
        token[...] = jnp.zeros_like(token)

    hbm = lambda t: pltpu.HBM(t.shape, t.dtype)
    outs = pl.pallas_call(
        body, name=name,
        out_shape=[pltpu.SemaphoreType.DMA((n * N_PEERS,)), pltpu.SemaphoreType.DMA((n * N_PEERS,))]
        + [hbm(t) for t in srcs] + [hbm(t) for t in lands] + [jax.ShapeDtypeStruct((8, LANES), F32)],
        in_specs=[HBM_SPEC] * (2 * n) + [ANY],
        out_specs=[SEM_SPEC, SEM_SPEC] + [HBM_SPEC] * (2 * n) + [pl.BlockSpec(memory_space=pltpu.VMEM)],
        input_output_aliases={i: 2 + i for i in range(2 * n)},
        compiler_params=pltpu.CompilerParams(has_side_effects=DATAFLOW),
    )(*[pltpu.with_memory_space_constraint(t, pltpu.HBM) for t in list(srcs) + lands], after)
    return dict(n=n, scatter=scatter, send_sems=outs[0], recv_sems=outs[1], srcs=outs[2:2 + n],
                lands=outs[2 + n:2 + 2 * n], token=outs[2 + 2 * n])


def _exchange_wait(name, handle, after):
    n, scatter = handle["n"], handle["scatter"]

    def body(*refs):
        src_refs, land_refs = refs[:n], refs[n:2 * n]
        send_sems, recv_sems = refs[2 * n], refs[2 * n + 1]
        for cp in _peer_copies(src_refs, land_refs, scatter, send_sems, recv_sems):
            cp.wait_send()
            cp.wait_recv()

    both = list(handle["srcs"]) + list(handle["lands"])
    outs = pl.pallas_call(
        body, name=name,
        out_shape=[pltpu.HBM(t.shape, t.dtype) for t in both],
        in_specs=[HBM_SPEC] * (2 * n) + [SEM_SPEC, SEM_SPEC, ANY],
        out_specs=[HBM_SPEC] * (2 * n),
        input_output_aliases={i: i for i in range(2 * n)},
        compiler_params=pltpu.CompilerParams(has_side_effects=DATAFLOW),
    )(*both, handle["send_sems"], handle["recv_sems"], after)
    me_idx = _my_index()
    lands = [lax.dynamic_update_index_in_dim(land, src, me_idx, 0) if sc is False else land
             for land, src, sc in zip(outs[n:], outs[:n], scatter)]
    return lands, outs[:n]


def _my_index():
    return 4 * lax.axis_index("x") + 2 * lax.axis_index("y") + lax.axis_index("c")


N_RELAYED = len(RELAYED) + 1


def _relay_copies(bufs, send_sems, recv_sems):
    x, y, c = lax.axis_index("x"), lax.axis_index("y"), lax.axis_index("c")
    slots = []
    for k in RELAYED:
        px = 1 - x if (k >> 2) & 1 else x
        py = 1 - y if (k >> 1) & 1 else y
        slots.append(4 * px + 2 * py + c)
    slots.append(4 * x + 2 * y + (1 - c))
    copies = []
    for j, slot in enumerate(slots):
        for a, buf in enumerate(bufs):
            copies.append(pltpu.make_async_remote_copy(
                src_ref=buf.at[slot], dst_ref=buf.at[slot], send_sem=send_sems.at[a * N_RELAYED + j],
                recv_sem=recv_sems.at[a * N_RELAYED + j], device_id=(x, y, 1 - c), device_id_type=MESH))
    return copies


def _relay_start(name, bufs, after):
    n = len(bufs)

    def body(*refs):
        send_sems, recv_sems = refs[n + 1], refs[n + 2]
        for cp in _relay_copies(refs[:n], send_sems, recv_sems):
            cp.start()
        token = refs[2 * n + 3]
        token[...] = jnp.zeros_like(token)

    n_sems = n * N_RELAYED
    outs = pl.pallas_call(
        body, name=name,
        out_shape=[pltpu.SemaphoreType.DMA((n_sems,)), pltpu.SemaphoreType.DMA((n_sems,))]
        + [pltpu.HBM(t.shape, t.dtype) for t in bufs] + [jax.ShapeDtypeStruct((8, LANES), F32)],
        in_specs=[HBM_SPEC] * n + [ANY],
        out_specs=[SEM_SPEC, SEM_SPEC] + [HBM_SPEC] * n + [pl.BlockSpec(memory_space=pltpu.VMEM)],
        input_output_aliases={i: 2 + i for i in range(n)},
        compiler_params=pltpu.CompilerParams(has_side_effects=DATAFLOW),
    )(*[pltpu.with_memory_space_constraint(t, pltpu.HBM) for t in bufs], after)
    return dict(n=n, send_sems=outs[0], recv_sems=outs[1], bufs=outs[2:2 + n], token=outs[2 + n])


def _relay_wait(name, handle, after):
    n = handle["n"]

    def body(*refs):
        for cp in _relay_copies(refs[:n], refs[n], refs[n + 1]):
            cp.wait_send()
            cp.wait_recv()

    return pl.pallas_call(
        body, name=name,
        out_shape=[pltpu.HBM(t.shape, t.dtype) for t in handle["bufs"]],
        in_specs=[HBM_SPEC] * n + [SEM_SPEC, SEM_SPEC, ANY],
        out_specs=[HBM_SPEC] * n,
        input_output_aliases={i: i for i in range(n)},
        compiler_params=pltpu.CompilerParams(has_side_effects=DATAFLOW),
    )(*handle["bufs"], handle["send_sems"], handle["recv_sems"], after)


def _adamw(parts, w, m, v, sent=None):
    r, c = w.shape
    tr = 256 if r % 256 == 0 else r
    own = sent is not None

    def body(*refs):
        if own:
            _, p_ref, own_ref, w_ref, m_ref, v_ref, g_ref, d_ref, nm_ref, nv_ref = refs
            g = own_ref[...].astype(F32)
        else:
            p_ref, w_ref, m_ref, v_ref, g_ref, d_ref, nm_ref, nv_ref = refs
            g = p_ref[0].astype(F32)
        for k in range(1, N_DEV):
            g = g + p_ref[k].astype(F32)
        m_new = ADAM_B1 * m_ref[...] + (1.0 - ADAM_B1) * g
        v_new = ADAM_B2 * v_ref[...] + (1.0 - ADAM_B2) * (g * g)
        m_hat = m_new / (1.0 - ADAM_B1 ** ADAM_STEP)
        v_hat = v_new / (1.0 - ADAM_B2 ** ADAM_STEP)
        g_ref[...] = g
        d_ref[...] = -ADAM_LR * (m_hat / (jnp.sqrt(v_hat) + ADAM_EPS) + ADAM_WD * w_ref[...])
        nm_ref[...] = m_new
        nv_ref[...] = v_new

    out_shape = [jax.ShapeDtypeStruct((r, c), F32)] * 4
    if not own:
        return pl.pallas_call(
            body, name="adamw", grid=(r // tr,),
            in_specs=[pl.BlockSpec((N_DEV, tr, c), lambda i: (0, i, 0))] + [_rows(tr, c)] * 3,
            out_specs=[_rows(tr, c)] * 4, out_shape=out_shape, compiler_params=_params(),
        )(parts, w, m, v)
    rows = pl.BlockSpec((tr, c), lambda i, me: (i, 0))
    return pl.pallas_call(
        body, name="adamw_own", out_shape=out_shape, compiler_params=_params(),
        grid_spec=pltpu.PrefetchScalarGridSpec(
            num_scalar_prefetch=1, grid=(r // tr,),
            in_specs=[pl.BlockSpec((N_DEV, tr, c), lambda i, me: (0, i, 0)),
                      pl.BlockSpec((None, tr, c), lambda i, me: (me[0], i, 0))] + [rows] * 3,
            out_specs=[rows] * 4),
    )(_my_index().reshape(1).astype(jnp.int32), parts, sent, w, m, v)


def _adam_step(g, w, m, v):
    m_new = ADAM_B1 * m + (1.0 - ADAM_B1) * g
    v_new = ADAM_B2 * v + (1.0 - ADAM_B2) * (g * g)
    m_hat = m_new / (1.0 - ADAM_B1 ** ADAM_STEP)
    v_hat = v_new / (1.0 - ADAM_B2 ** ADAM_STEP)
    return -ADAM_LR * (m_hat / (jnp.sqrt(v_hat) + ADAM_EPS) + ADAM_WD * w), m_new, v_new


SMALL_NAMES = ("norm_mix", "pool_w", "pool_b", "pool_scale", "attn_sinks", "norm_mlp", "norm_final")


def _adamw_small(mlp_all, mix_all, scale_all, nmix_all, w, m, v):
    def body(mlp_ref, mix_ref, scale_ref, nmix_ref, *refs):
        ins, outs = refs[:21], refs[21:]

        def total(ref, rows, lanes=slice(None)):
            g = ref[0, rows, lanes]
            for k in range(1, N_DEV):
                g = g + ref[k, rows, lanes]
            return g

        grads = dict(
            norm_mix=total(nmix_ref, slice(0, 1)), pool_w=total(mix_ref, slice(0, MIX_POOL_B)),
            pool_b=total(mix_ref, slice(MIX_POOL_B, MIX_POOL_B + 4)), pool_scale=total(scale_ref, slice(0, 1)),
            attn_sinks=total(mix_ref, slice(MIX_SINKS, MIX_SINKS + 1)),
            norm_mlp=total(mlp_ref, slice(0, 1)), norm_final=total(mlp_ref, slice(1, 2)))
        for i, name in enumerate(SMALL_NAMES):
            g = grads[name]
            d, m_new, v_new = _adam_step(g, ins[3 * i][...], ins[3 * i + 1][...], ins[3 * i + 2][...])
            for ref, val in zip(outs[4 * i:4 * i + 4], (g, d, m_new, v_new)):
                ref[...] = val
        outs[28][...] = jnp.broadcast_to(total(mlp_ref, slice(2, 3), slice(0, LANES)), (8, LANES))

    operands, out_shape = [], []
    for name in SMALL_NAMES:
        operands += [w[name], m[name], v[name]]
        out_shape += [jax.ShapeDtypeStruct(w[name].shape, F32)] * 4
    out_shape.append(jax.ShapeDtypeStruct((8, LANES), F32))
    outs = pl.pallas_call(body, name="adamw_small", out_shape=out_shape)(
        mlp_all, mix_all, scale_all, nmix_all, *operands)
    return {name: outs[4 * i:4 * i + 4] for i, name in enumerate(SMALL_NAMES)}, outs[28]


def kernel(x, norm_mix, w_in, pool_w, pool_b, pool_scale, attn_sinks, p_pool, p_attn, w_out, norm_mlp, w_up, w_down, norm_final, loss_target, m_norm_mix, m_w_in, m_pool_w, m_pool_b, m_pool_scale, m_attn_sinks, m_p_pool, m_p_attn, m_w_out, m_norm_mlp, m_w_up, m_w_down, m_norm_final, v_norm_mix, v_w_in, v_pool_w, v_pool_b, v_pool_scale, v_attn_sinks, v_p_pool, v_p_attn, v_w_out, v_norm_mlp, v_w_up, v_w_down, v_norm_final):
    xs = x[0]
    tgt = loss_target[0]
    s_len = xs.shape[0]

    p_pool_bf, p_attn_bf, w_out_bf, w_up_bf, w_down_bf = [
        t[0].astype(BF16) for t in (p_pool, p_attn, w_out, w_up, w_down)]
    w_in_bf = w_in[0].T.astype(BF16)
    (w_in_g,) = _all_gather_weights("all_gather_w_in", [w_in_bf])
    ag_rest = _exchange_start(
        "ag_rest_start", [p_pool_bf, p_attn_bf, w_out_bf, w_up_bf, w_down_bf], ("chip",) * 5, w_in_g)

    pool_w_bf = pool_w[0].astype(BF16)
    pool_b_row = pool_b[0].reshape(1, POOL_WIDTH)
    bias_t, sink_row = _attn_constants(attn_sinks[0])

    w_in_t = w_in_g.reshape(IN_WIDTH, D_MODEL)
    u, zp, q, kv, zg = _fwd_in(ag_rest["token"], xs, norm_mix, w_in_t)
    n_mixer_tiles = s_len // min(512, s_len)
    half = min(n_mixer_tiles - 1, n_mixer_tiles // 2 + 2)
    mixer_args = (zp, q, kv, pool_w_bf, pool_b_row, pool_scale, bias_t, sink_row)
    pm, o = _mixers_fwd(*mixer_args, n_tiles=half) if half else (None, None)
    first_level, _ = _exchange_wait("ag_rest_wait", ag_rest, zg if o is None else o)
    relay = _relay_start("ag_relay_start", first_level, zg)
    pm, o = _mixers_fwd(*mixer_args, first_tile=half, earlier=None if o is None else (relay["token"], pm, o))
    p_pool_g, p_attn_g, w_out_g, w_up_g, w_down_g = _relay_wait("ag_relay_wait", relay, o)
    p_pool_f = p_pool_g.transpose(1, 0, 2).reshape(POOL_WIDTH, D_MODEL)
    p_attn_f = p_attn_g.transpose(1, 0, 2).reshape(ATTN_WIDTH, D_MODEL)
    w_out_f = w_out_g.reshape(D_MODEL, D_MODEL)
    w_down_f = w_down_g.reshape(D_FF, D_MODEL)
    mixed, dh1, a, dapre, u2, dh2, small_mlp, dyp, dya, dzg, dpm, do, dh1_bf = _core(
        xs, pm, o, zg, tgt, norm_mlp, norm_final.reshape(1, D_MODEL), p_pool_f, p_attn_f, w_out_f, w_up_g, w_down_f)
    gw_down = _tn_matmul(a, dh2, square_a=True)
    gw_up = _tn_matmul(u2, dapre, col_blocks=N_DEV)
    ex_mlp = _exchange_start(
        "ex_mlp_start", [gw_up, gw_down.reshape(N_DEV, D_FF // N_DEV, D_MODEL)], (True, True), small_mlp)
    dzp, dq, dkv, small_mix, g_pool_scale = _mixers_bwd(
        ex_mlp["token"], zp, q, kv, dpm, do, pool_w_bf, pool_b_row, pool_scale, bias_t, sink_row)
    gw_in = _tn_w_in(u, dzp, dq, dkv, dzg)
    ex_in = _exchange_start(
        "ex_in_start", [gw_in, small_mlp, small_mix, g_pool_scale], (True, False, False, False), dq)
    gw_out = _tn_matmul(mixed, dh1_bf, after=ex_in["token"])
    gp_pool = _tn_matmul(pm, dyp, col_blocks=N_DEV, after=ex_in["token"])
    gp_attn = _tn_matmul(o, dya, col_blocks=N_DEV, after=ex_in["token"])
    ex_proj = _exchange_start(
        "ex_proj_start", [gp_pool, gp_attn, gw_out.reshape(N_DEV, D_MODEL // N_DEV, D_MODEL)], (True,) * 3,
        ex_in["token"])
    dx, g_norm_mix = _in_bwd(ex_proj["token"], dzp, dq, dkv, dzg, w_in_t, xs, dh1, norm_mix)

    big_w = dict(w_in=w_in, p_pool=p_pool, p_attn=p_attn, w_out=w_out, w_up=w_up, w_down=w_down)
    big_m = dict(w_in=m_w_in, p_pool=m_p_pool, p_attn=m_p_attn, w_out=m_w_out, w_up=m_w_up, w_down=m_w_down)
    big_v = dict(w_in=v_w_in, p_pool=v_p_pool, p_attn=v_p_attn, w_out=v_w_out, w_up=v_w_up, w_down=v_w_down)
    res = {}

    def update(names, recvs, sents):
        for name, parts, sent in zip(names, recvs, sents):
            flip = (lambda t: t.T) if name == "w_in" else (lambda t: t)
            outs = _adamw(parts, flip(big_w[name][0]), flip(big_m[name][0]), flip(big_v[name][0]), sent)
            res[name] = [flip(t)[None] for t in outs]

    update(["w_up", "w_down"], *_exchange_wait("ex_mlp_wait", ex_mlp, dx))
    (norm_mix_all,) = _all_gather_weights("all_gather_norm_mix", [g_norm_mix], res["w_down"][0])
    (r_in, mlp_all, mix_all, scale_all), (s_in, _, _, _) = _exchange_wait("ex_in_wait", ex_in, norm_mix_all)
    update(["w_in"], [r_in], [s_in])

    natural = dict(norm_mix=(1, D_MODEL), pool_w=(MIX_POOL_B, LANES), pool_b=(4, LANES), pool_scale=(1, POOL_WIDTH),
                   attn_sinks=(1, LANES), norm_mlp=(1, D_MODEL), norm_final=(1, D_MODEL))

    def as_2d(t, name):
        if name == "attn_sinks":
            return jnp.pad(t, ((0, 0), (0, LANES - N_HEADS)))
        return t.reshape(natural[name])

    small_w = dict(norm_mix=norm_mix, pool_w=pool_w, pool_b=pool_b, pool_scale=pool_scale, attn_sinks=attn_sinks,
                   norm_mlp=norm_mlp, norm_final=norm_final)
    small_m = dict(norm_mix=m_norm_mix, pool_w=m_pool_w, pool_b=m_pool_b, pool_scale=m_pool_scale,
                   attn_sinks=m_attn_sinks, norm_mlp=m_norm_mlp, norm_final=m_norm_final)
    small_v = dict(norm_mix=v_norm_mix, pool_w=v_pool_w, pool_b=v_pool_b, pool_scale=v_pool_scale,
                   attn_sinks=v_attn_sinks, norm_mlp=v_norm_mlp, norm_final=v_norm_final)
    small_res, loss_all = _adamw_small(
        mlp_all, mix_all, scale_all, norm_mix_all,
        *[{k: as_2d(t, k) for k, t in d.items()} for d in (small_w, small_m, small_v)])
    loss = loss_all[0, 0]
    for name in SMALL_NAMES:
        shape = small_w[name].shape
        res[name] = [(t[:, :N_HEADS] if name == "attn_sinks" else t).reshape(shape) for t in small_res[name]]
    update(["p_pool", "p_attn", "w_out"], *_exchange_wait("ex_proj_wait", ex_proj, loss_all))

    order = ["norm_mix", "w_in", "pool_w", "pool_b", "pool_scale", "attn_sinks", "p_pool", "p_attn", "w_out",
             "norm_mlp", "w_up", "w_down", "norm_final"]
    out = [loss, dx[None]]
    for kind in range(4):
        out += [res[name][kind] for name in order]
    return tuple(out)
```

```python
import functools
import math

import numpy as np
import jax
import jax.numpy as jnp
from jax import lax
from jax.experimental import pallas as pl
from jax.experimental.pallas import tpu as pltpu

F32 = jnp.float32
BF16 = jnp.bfloat16

D_MODEL = 1024
POOL_WIDTH = 512
ATTN_WIDTH = 512
KV_WIDTH = 128
HEAD_DIM = 64
N_HEADS = 8
N_KV_HEADS = 2
GROUP = 4
BLOCK = 128
POOL_WINDOWS = (2, 4, 8, 16)
POOL_GROUP_DIM = 128
POOL_HALO = 16
D_FF = 4096
FF_CHUNK = 1024
IN_WIDTH = 3328
RMS_EPS = 1e-5
NEG_INF = -1e30
ATTN_SCALE = 1.0 / math.sqrt(HEAD_DIM)
N_DEV = 8

ADAM_LR = 0.001
ADAM_B1 = 0.9
ADAM_B2 = 0.999
ADAM_EPS = 1e-08
ADAM_WD = 0.01
ADAM_STEP = 10

LANES = 128
VMEM_LIMIT_BYTES = 56 * 1024 * 1024
MESH = pl.DeviceIdType.MESH


def _params(n_grid_axes=1):
    return pltpu.CompilerParams(
        dimension_semantics=("arbitrary",) * n_grid_axes, vmem_limit_bytes=VMEM_LIMIT_BYTES)


def _dot(a, b):
    return jnp.dot(a, b, preferred_element_type=F32)


def _dot_nt(a, b):
    return lax.dot_general(a, b, (((1,), (1,)), ((), ())), preferred_element_type=F32)


def _dot_tn(a, b):
    return lax.dot_general(a, b, (((0,), (0,)), ((), ())), preferred_element_type=F32)


ANY = pl.BlockSpec(memory_space=pl.ANY)


def _rows(tm, n):
    return pl.BlockSpec((tm, n), lambda i: (i, 0))


def _whole(shape):
    zeros = (0,) * len(shape)
    return pl.BlockSpec(shape, lambda i: zeros)


def _rms_fwd(h, g):
    r = lax.rsqrt(jnp.mean(h * h, axis=-1, keepdims=True) + RMS_EPS)
    xh = h * r
    return r, xh, xh * g


def _rms_bwd(dy, xh, r, g):
    dxh = dy * g
    dh = r * (dxh - xh * jnp.mean(dxh * xh, axis=-1, keepdims=True))
    return dh, jnp.sum(dy * xh, axis=0, keepdims=True)


def _fwd_in(after, x, g_mix, w_in_t):
    s_len = x.shape[0]
    tm = min(512, s_len)

    def body(after_ref, x_ref, g_ref, w_ref, u_ref, zp_ref, q_ref, kv_ref, zg_ref):
        _, _, u = _rms_fwd(x_ref[...], g_ref[...])
        u = u.astype(BF16)
        u_ref[...] = u
        zp_ref[...] = _dot_nt(u, w_ref[0:512, :]).astype(BF16)
        q_ref[...] = _dot_nt(u, w_ref[512:1024, :]).astype(BF16)
        kv_ref[...] = _dot_nt(u, w_ref[1024:1280, :]).astype(BF16)
        zg_ref[...] = _dot_nt(u, w_ref[1280:3328, :]).astype(BF16)

    return pl.pallas_call(
        body, name="fwd_in", grid=(s_len // tm,),
        in_specs=[ANY, _rows(tm, D_MODEL), _whole((1, D_MODEL)),
                  pl.BlockSpec((IN_WIDTH, D_MODEL), lambda i: (0, 0), pipeline_mode=pl.Buffered(1))],
        out_specs=[_rows(tm, D_MODEL), _rows(tm, 512), _rows(tm, 512), _rows(tm, 256), _rows(tm, 2048)],
        out_shape=[jax.ShapeDtypeStruct((s_len, n), BF16) for n in (D_MODEL, 512, 512, 256, 2048)],
        compiler_params=_params(),
    )(after, x, g_mix, w_in_t)


def _attn_constants(sinks):
    r = np.arange(BLOCK)[:, None]
    qi = np.arange(BLOCK)[None, :]
    dist = np.where(r <= qi, qi - r, BLOCK + qi - r).astype(np.float32)
    slopes = np.array([2.0 ** (-8.0 * (h + 1) / N_HEADS) for h in range(N_HEADS)], dtype=np.float32)
    bias = (-slopes[:, None, None] * dist[None]).reshape(N_KV_HEADS, GROUP, BLOCK, BLOCK)
    bias = np.ascontiguousarray(bias.transpose(0, 2, 1, 3)).reshape(N_KV_HEADS, BLOCK, GROUP * BLOCK)
    sink_row = jnp.repeat(sinks.astype(F32).reshape(N_KV_HEADS, GROUP), BLOCK, axis=1)[:, None, :]
    return jnp.asarray(bias.astype(np.float32)), sink_row


def _own_block_mask():
    shape = (BLOCK, GROUP * BLOCK)
    r = lax.broadcasted_iota(jnp.int32, shape, 0)
    qi = lax.broadcasted_iota(jnp.int32, shape, 1) & (BLOCK - 1)
    return r <= qi


def _pack_keys(t, own):
    return jnp.where(own, t[BLOCK:], t[:BLOCK])


def _unpack_keys(t, own):
    zero = jnp.zeros_like(t)
    return jnp.concatenate([jnp.where(own, zero, t), jnp.where(own, t, zero)], axis=0)


def _left_half(shape):
    return lax.broadcasted_iota(jnp.int32, shape, 1) < HEAD_DIM


def _dup_halves(slab):
    swapped = pltpu.roll(slab, HEAD_DIM, 1)
    left = _left_half(slab.shape)
    return jnp.where(left, slab, swapped), jnp.where(left, swapped, slab)


def _fill_kv_slabs(kvh_ref, kv_ref, ka_ref, vd_ref):
    for rows, src in ((slice(0, BLOCK), kvh_ref), (slice(BLOCK, None), kv_ref)):
        kvf = src[...].astype(F32)
        for ref, lanes in ((ka_ref, slice(0, KV_WIDTH)), (vd_ref, slice(KV_WIDTH, 2 * KV_WIDTH))):
            d0, d1 = _dup_halves(kvf[:, lanes])
            ref[0, rows, :] = d0.astype(BF16)
            ref[1, rows, :] = d1.astype(BF16)


def _stack_pairs(a, h):
    pieces = []
    for j in range(2):
        pair = a[:, h * 256 + j * LANES:h * 256 + (j + 1) * LANES]
        left = _left_half(pair.shape)
        zero = jnp.zeros_like(pair)
        pieces += [jnp.where(left, pair, zero), jnp.where(left, zero, pair)]
    return jnp.concatenate(pieces, axis=0)


def _attn_probs(kk, q_st, bias_p, sink_row, own, first):
    s = _pack_keys(_dot_nt(kk, q_st), own) * ATTN_SCALE + bias_p
    if first is not None:
        s = jnp.where(jnp.logical_and(first, jnp.logical_not(own)), NEG_INF, s)
    m = jnp.maximum(jnp.max(s, axis=0, keepdims=True), sink_row)
    p = jnp.exp(s - m)
    es = jnp.exp(sink_row - m)
    inv = 1.0 / (jnp.sum(p, axis=0, keepdims=True) + es)
    return p * inv, es * inv


def _pool_d(ext, cur, g, row0):
    w = POOL_WINDOWS[g]
    acc = ext
    k = 1
    while k < w:
        acc = acc + pltpu.roll(acc, k, 0)
        k *= 2
    return _window_mean(acc[POOL_HALO:, :], w, row0) - cur


def _window_mean(total, w, row0):
    t = row0 + lax.broadcasted_iota(jnp.int32, (POOL_HALO, total.shape[1]), 0)
    head = total[:POOL_HALO] / jnp.minimum(t + 1, w).astype(F32)
    return jnp.concatenate([head, total[POOL_HALO:] * (1.0 / w)], axis=0)


def _mixers_fwd(zp, q, kv, pool_w, pool_b, pool_scale, bias_t, sink_row, first_tile=0, n_tiles=None, earlier=None):
    s_len = zp.shape[0]
    tq = min(512, s_len)
    nb = tq // BLOCK
    n_tiles = s_len // tq - first_tile if n_tiles is None else n_tiles
    extra = [] if earlier is None else list(earlier)

    def body(zp_ref, zph_ref, q_ref, kv_ref, kvh_ref, pw_ref, pb_ref, ps_ref, bias_ref, sink_ref, *rest):
        pm_ref, o_ref, ka_ref, vd_ref = rest[len(extra):]
        i = pl.program_id(0) + first_tile
        cur = zp_ref[...].astype(F32)
        halo = zph_ref[...].astype(F32) * (i > 0).astype(F32)
        ext = jnp.concatenate([halo, cur], axis=0)
        for g in range(4):
            sl = slice(g * POOL_GROUP_DIM, (g + 1) * POOL_GROUP_DIM)
            d = _pool_d(ext[:, sl], cur[:, sl], g, i * tq)
            y = _dot(d.astype(BF16), pw_ref[g]) + pb_ref[:, sl]
            pm_ref[:, sl] = (y * ps_ref[:, sl]).astype(BF16)
        _fill_kv_slabs(kvh_ref, kv_ref, ka_ref, vd_ref)
        own = _own_block_mask()
        for b in range(nb):
            rq = slice(b * BLOCK, (b + 1) * BLOCK)
            rk = slice(b * BLOCK, (b + 2) * BLOCK)
            qb = q_ref[rq, :]
            for h in range(N_KV_HEADS):
                pn, _ = _attn_probs(ka_ref[h, rk, :], _stack_pairs(qb, h), bias_ref[h], sink_ref[h], own,
                                    (i == 0) if b == 0 else None)
                pn = _unpack_keys(pn, own).astype(BF16)
                vd = vd_ref[h, rk, :]
                left = _left_half(vd.shape)
                zero = jnp.zeros_like(vd)
                va, vb = jnp.where(left, vd, zero), jnp.where(left, zero, vd)
                for j in range(2):
                    o_pair = (_dot_tn(pn[:, (2 * j) * BLOCK:(2 * j + 1) * BLOCK], va)
                              + _dot_tn(pn[:, (2 * j + 1) * BLOCK:(2 * j + 2) * BLOCK], vb))
                    o_ref[rq, h * 256 + j * LANES:h * 256 + (j + 1) * LANES] = o_pair.astype(BF16)

    def tile(n):
        return pl.BlockSpec((tq, n), lambda i: (i + first_tile, 0))

    halo_pool = pl.BlockSpec(
        (POOL_HALO, 512), lambda i: (jnp.maximum((i + first_tile) * (tq // POOL_HALO) - 1, 0), 0))
    halo_kv = pl.BlockSpec((BLOCK, 256), lambda i: (jnp.maximum((i + first_tile) * nb - 1, 0), 0))
    return pl.pallas_call(
        body, name="mixers_fwd", grid=(n_tiles,),
        in_specs=[tile(512), halo_pool, tile(512), tile(256), halo_kv,
                  _whole((4, 128, 128)), _whole((1, 512)), _whole((1, 512)),
                  _whole((N_KV_HEADS, BLOCK, GROUP * BLOCK)), _whole((N_KV_HEADS, 1, GROUP * BLOCK))]
        + [ANY] * len(extra),
        out_specs=[tile(512), tile(512)],
        out_shape=[jax.ShapeDtypeStruct((s_len, 512), BF16)] * 2,
        input_output_aliases={11: 0, 12: 1} if extra else {},
        scratch_shapes=[pltpu.VMEM((N_KV_HEADS, tq + BLOCK, LANES), BF16)] * 2,
        compiler_params=_params(),
    )(zp, zp, q, kv, kv, pool_w, pool_b, pool_scale, bias_t, sink_row, *extra)


def _gated_mix(pm, o, zg, pp_ref, pa_ref):
    yp = _dot(pm, pp_ref[...])
    ya = _dot(o, pa_ref[...])
    gp = jax.nn.sigmoid(zg[:, :D_MODEL].astype(F32))
    ga = jax.nn.sigmoid(zg[:, D_MODEL:].astype(F32))
    return yp, ya, gp, ga


def _core(x, pm, o, zg, tgt, g_mlp, g_fin, p_pool, p_attn, w_out, w_up_blocks, w_down):
    s_len = x.shape[0]
    tm = min(256, s_len)
    n_chunks = D_FF // FF_CHUNK
    up_block = D_FF // N_DEV
    per_chunk = FF_CHUNK // up_block

    def body(x_ref, pm_ref, o_ref, zg_ref, tgt_ref, gm_ref, gf_ref, pp_ref, pa_ref, wo_ref, wu_ref, wd_ref,
             dh1_ref, a_ref, dap_ref, pack_ref, small_ref,
             dyp_ref, dya_ref, dzg_ref, dpm_ref, do_ref):
        i = pl.program_id(0)

        @pl.when(i == 0)
        def _():
            small_ref[...] = jnp.zeros_like(small_ref)

        yp, ya, gp, ga = _gated_mix(pm_ref[...], o_ref[...], zg_ref[...], pp_ref, pa_ref)
        mixed = (gp * yp + ga * ya).astype(BF16)
        pack_ref[:, 0:D_MODEL] = mixed
        h1 = x_ref[...] + _dot(mixed, wo_ref[...])
        r2, xh2, u2 = _rms_fwd(h1, gm_ref[...])
        u2 = u2.astype(BF16)
        pack_ref[:, D_MODEL:2 * D_MODEL] = u2
        acc = jnp.zeros((tm, D_MODEL), F32)
        for c in range(n_chunks):
            cs = slice(c * FF_CHUNK, (c + 1) * FF_CHUNK)
            a = jnp.concatenate([_dot(u2, wu_ref[per_chunk * c + j]) for j in range(per_chunk)], axis=1)
            a = jnp.maximum(a, 0.0)
            a_ref[:, cs] = a.astype(BF16)
            acc = acc + _dot((a * a).astype(BF16), wd_ref[cs, :])
        h2 = h1 + acc
        r3, xh3, y = _rms_fwd(h2, gf_ref[...])
        diff = y - tgt_ref[...]
        small_ref[2:3, :] += 0.5 * jnp.sum(jnp.mean(diff * diff, axis=-1, keepdims=True))
        dy = diff * (1.0 / D_MODEL)
        dh2, dgf = _rms_bwd(dy, xh3, r3, gf_ref[...])
        small_ref[1:2, :] += dgf
        dh2_bf = dh2.astype(BF16)
        pack_ref[:, 2 * D_MODEL:3 * D_MODEL] = dh2_bf
        du2 = jnp.zeros((tm, D_MODEL), F32)
        for c in range(n_chunks):
            cs = slice(c * FF_CHUNK, (c + 1) * FF_CHUNK)
            ds = _dot_nt(dh2_bf, wd_ref[cs, :])
            dap = (ds * (2.0 * a_ref[:, cs].astype(F32))).astype(BF16)
            dap_ref[:, cs] = dap
            for j in range(per_chunk):
                du2 = du2 + _dot_nt(dap[:, j * up_block:(j + 1) * up_block], wu_ref[per_chunk * c + j])
        dh1n, dgm = _rms_bwd(du2, xh2, r2, gm_ref[...])
        small_ref[0:1, :] += dgm
        dh1 = dh2 + dh1n
        dh1_ref[...] = dh1
        dh1_bf = dh1.astype(BF16)
        pack_ref[:, 3 * D_MODEL:4 * D_MODEL] = dh1_bf
        dm = _dot_nt(dh1_bf, wo_ref[...])
        dyp = (dm * gp).astype(BF16)
        dya = (dm * ga).astype(BF16)
        dyp_ref[...] = dyp
        dya_ref[...] = dya
        dzg_ref[:, :D_MODEL] = (dm * yp * (gp * (1.0 - gp))).astype(BF16)
        dzg_ref[:, D_MODEL:] = (dm * ya * (ga * (1.0 - ga))).astype(BF16)
        dpm_ref[...] = _dot_nt(dyp, pp_ref[...]).astype(BF16)
        do_ref[...] = _dot_nt(dya, pa_ref[...]).astype(BF16)

    def fixed(shape):
        return pl.BlockSpec(shape, lambda i: (0,) * len(shape), pipeline_mode=pl.Buffered(1))

    widths_dtypes = ((D_MODEL, F32), (D_FF, BF16), (D_FF, BF16), (4 * D_MODEL, BF16))
    back = ((D_MODEL, BF16), (D_MODEL, BF16), (2048, BF16), (512, BF16), (512, BF16))
    return pl.pallas_call(
        body, name="core", grid=(s_len // tm,),
        in_specs=[_rows(tm, D_MODEL), _rows(tm, 512), _rows(tm, 512), _rows(tm, 2048), _rows(tm, D_MODEL),
                  _whole((1, D_MODEL)), _whole((1, D_MODEL)),
                  fixed((512, D_MODEL)), fixed((512, D_MODEL)), fixed((D_MODEL, D_MODEL)),
                  fixed((N_DEV, D_MODEL, up_block)), fixed((D_FF, D_MODEL))],
        out_specs=[_rows(tm, n) for n, _ in widths_dtypes] + [_whole((8, D_MODEL))] + [_rows(tm, n) for n, _ in back],
        out_shape=[jax.ShapeDtypeStruct((s_len, n), d) for n, d in widths_dtypes]
        + [jax.ShapeDtypeStruct((8, D_MODEL), F32)] + [jax.ShapeDtypeStruct((s_len, n), d) for n, d in back],
        compiler_params=_params(),
    )(x, pm, o, zg, tgt, g_mlp, g_fin, p_pool, p_attn, w_out, w_up_blocks, w_down)


def _tn_matmul(a, b, square_a=False, col_blocks=None, after=None, a_cols=None, b_cols=None):
    s_len = a.shape[0]
    a0, ka = (0, a.shape[1]) if a_cols is None else a_cols
    b0, nb = (0, b.shape[1]) if b_cols is None else b_cols
    tt = min(2048, s_len)
    tk = min(1024, ka)
    tn = min(1024, nb)
    n_t = s_len // tt
    if col_blocks is None:
        out_spec = pl.BlockSpec((tk, tn), lambda k, j, t: (k, j))
        out_shape = jax.ShapeDtypeStruct((ka, nb), BF16)
    else:
        width = nb // col_blocks
        per_tile = tn // width
        out_spec = pl.BlockSpec((per_tile, tk, width), lambda k, j, t: (j, k, 0))
        out_shape = jax.ShapeDtypeStruct((col_blocks, ka, width), BF16)

    extra = [] if after is None else [after]

    def body(a_ref, b_ref, *rest):
        o_ref, acc_ref = rest[len(extra):]
        t = pl.program_id(2)

        @pl.when(t == 0)
        def _():
            acc_ref[...] = jnp.zeros_like(acc_ref)

        av = a_ref[...]
        if square_a:
            av = av * av
        acc_ref[...] += _dot_tn(av.astype(BF16), b_ref[...].astype(BF16))

        @pl.when(t == n_t - 1)
        def _():
            if col_blocks is None:
                o_ref[...] = acc_ref[...].astype(o_ref.dtype)
            else:
                for blk in range(per_tile):
                    o_ref[blk] = acc_ref[:, blk * width:(blk + 1) * width].astype(o_ref.dtype)

    return pl.pallas_call(
        body, name="tn_matmul", grid=(ka // tk, nb // tn, n_t),
        in_specs=[pl.BlockSpec((tt, tk), lambda k, j, t: (t, k + a0 // tk)),
                  pl.BlockSpec((tt, tn), lambda k, j, t: (t, j + b0 // tn))]
        + [ANY] * len(extra),
        out_specs=out_spec, out_shape=out_shape,
        scratch_shapes=[pltpu.VMEM((tk, tn), F32)],
        compiler_params=_params(3),
    )(a, b, *extra)


def _tn_w_in(u, dzp, dq, dkv, dzg):
    s_len = u.shape[0]
    tt = min(1024, s_len)
    n_t = s_len // tt
    width = IN_WIDTH // N_DEV
    pieces = ((0, 512), (512, 1024), (1024, 1280), (1280, IN_WIDTH))

    def body(u_ref, dzp_ref, dq_ref, dkv_ref, dzg_ref, o_ref, acc_ref):
        t = pl.program_id(0)

        @pl.when(t == 0)
        def _():
            acc_ref[...] = jnp.zeros_like(acc_ref)

        uv = u_ref[...]
        for (c0, c1), ref in zip(pieces, (dzp_ref, dq_ref, dkv_ref, dzg_ref)):
            acc_ref[c0:c1, :] += _dot_tn(ref[...], uv)

        @pl.when(t == n_t - 1)
        def _():
            for j in range(N_DEV):
                o_ref[j] = acc_ref[j * width:(j + 1) * width, :].astype(BF16)

    return pl.pallas_call(
        body, name="tn_w_in", grid=(n_t,),
        in_specs=[_rows(tt, D_MODEL)] + [_rows(tt, c1 - c0) for c0, c1 in pieces],
        out_specs=_whole((N_DEV, width, D_MODEL)),
        out_shape=jax.ShapeDtypeStruct((N_DEV, width, D_MODEL), BF16),
        scratch_shapes=[pltpu.VMEM((IN_WIDTH, D_MODEL), F32)],
        compiler_params=_params(),
    )(u, dzp, dq, dkv, dzg)


MIX_POOL_B = 4 * POOL_GROUP_DIM
MIX_SINKS = MIX_POOL_B + 8
MIX_ROWS = MIX_SINKS + 8


def _mixers_bwd(after, zp, q, kv, dpm, do, pool_w, pool_b, pool_scale, bias_t, sink_row):
    s_len = zp.shape[0]
    tq = min(512, s_len)
    nb = tq // BLOCK
    n_steps = s_len // tq

    def body(after_ref, zp_ref, zph_ref, q_ref, kv_ref, kvh_ref, dpm_ref, dpmh_ref, do_ref, pw_ref, pb_ref, ps_ref,
             bias_ref, sink_ref, dzp_ref, dq_ref, dkv_ref, small_ref, dps_ref,
             ka_ref, vd_ref, dsk_acc, dkv_acc):
        i = pl.program_id(0)

        @pl.when(i == 0)
        def _():
            dkv_acc[...] = jnp.zeros_like(dkv_acc)
            small_ref[...] = jnp.zeros_like(small_ref)
            dps_ref[...] = jnp.zeros_like(dps_ref)
            dsk_acc[...] = jnp.zeros_like(dsk_acc)

        cur = zp_ref[...].astype(F32)
        halo = zph_ref[...].astype(F32) * (i > 0).astype(F32)
        ext = jnp.concatenate([halo, cur], axis=0)
        dpm_next = dpmh_ref[...].astype(F32) * (i < n_steps - 1).astype(F32)
        dpm_ext = jnp.concatenate([dpm_ref[...].astype(F32), dpm_next], axis=0)
        n_ext = tq + POOL_HALO
        for g in range(4):
            sl = slice(g * POOL_GROUP_DIM, (g + 1) * POOL_GROUP_DIM)
            w = POOL_WINDOWS[g]
            d = _pool_d(ext[:, sl], cur[:, sl], g, i * tq).astype(BF16)
            y_lin = _dot(d, pw_ref[g]) + pb_ref[:, sl]
            dps_ref[:, sl] += jnp.sum(dpm_ext[:tq, sl] * y_lin, axis=0, keepdims=True)
            dyl_ext = dpm_ext[:, sl] * ps_ref[:, sl]
            small_ref[MIX_POOL_B + g:MIX_POOL_B + g + 1, :] += jnp.sum(dyl_ext[:tq], axis=0, keepdims=True)
            dyl_bf = dyl_ext.astype(BF16)
            small_ref[g * POOL_GROUP_DIM:(g + 1) * POOL_GROUP_DIM, :] += _dot_tn(d, dyl_bf[:tq])
            dd = _dot_nt(dyl_bf, pw_ref[g])
            e = _window_mean(dd, w, i * tq)
            acc = e
            k = 1
            while k < w:
                acc = acc + pltpu.roll(acc, n_ext - k, 0)
                k *= 2
            dzp_ref[:, sl] = (acc[:tq] - dd[:tq]).astype(BF16)

        _fill_kv_slabs(kvh_ref, kv_ref, ka_ref, vd_ref)

        def fold(dup):
            return dup + pltpu.roll(dup, HEAD_DIM, 1)

        own = _own_block_mask()
        for b in range(nb):
            rq = slice(b * BLOCK, (b + 1) * BLOCK)
            rk = slice(b * BLOCK, (b + 2) * BLOCK)
            qb = q_ref[rq, :]
            dob = do_ref[rq, :]
            dk_dup, dv_dup = [], []
            for h in range(N_KV_HEADS):
                kk = ka_ref[h, rk, :]
                q_st = _stack_pairs(qb, h)
                do_st = _stack_pairs(dob, h)
                pn, psink = _attn_probs(kk, q_st, bias_ref[h], sink_ref[h], own, (i == 0) if b == 0 else None)
                dp = _pack_keys(_dot_nt(vd_ref[h, rk, :], do_st), own)
                delta = jnp.sum(pn * dp, axis=0, keepdims=True)
                dsk_acc[h] += -psink * delta
                ds = _unpack_keys((pn * (dp - delta)) * ATTN_SCALE, own).astype(BF16)
                pn = _unpack_keys(pn, own)
                dq_st = _dot_tn(ds, kk)
                for j in range(2):
                    left = _left_half((BLOCK, LANES))
                    dq_pair = jnp.where(left, dq_st[(2 * j) * BLOCK:(2 * j + 1) * BLOCK],
                                        dq_st[(2 * j + 1) * BLOCK:(2 * j + 2) * BLOCK])
                    dq_ref[rq, h * 256 + j * LANES:h * 256 + (j + 1) * LANES] = dq_pair.astype(BF16)
                dk_dup.append(fold(_dot(ds, q_st)))
                dv_dup.append(fold(_dot(pn.astype(BF16), do_st)))
            left = _left_half((2 * BLOCK, LANES))
            dkv_blk = jnp.concatenate([jnp.where(left, dk_dup[0], dk_dup[1]),
                                       jnp.where(left, dv_dup[0], dv_dup[1])], axis=1)
            g0 = pl.multiple_of(i * tq + b * BLOCK, BLOCK)
            dkv_acc[pl.ds(g0, 2 * BLOCK), :] += dkv_blk

        @pl.when(i == n_steps - 1)
        def _():
            dkv_ref[...] = dkv_acc[BLOCK:, :].astype(BF16)
            lane = lax.broadcasted_iota(jnp.int32, (1, LANES), 1)
            row = jnp.zeros((1, LANES), F32)
            for h in range(N_KV_HEADS):
                for g in range(GROUP):
                    tot = jnp.sum(dsk_acc[h, :, g * BLOCK:(g + 1) * BLOCK], axis=1, keepdims=True)
                    row = jnp.where(lane == GROUP * h + g, tot, row)
            small_ref[MIX_SINKS:MIX_SINKS + 1, :] = row

    blocks_per_tile = tq // POOL_HALO
    last_halo = s_len // POOL_HALO - 1
    halo_prev = pl.BlockSpec((POOL_HALO, 512), lambda i: (jnp.maximum(i * blocks_per_tile - 1, 0), 0))
    halo_next = pl.BlockSpec((POOL_HALO, 512), lambda i: (jnp.minimum((i + 1) * blocks_per_tile, last_halo), 0))
    halo_kv = pl.BlockSpec((BLOCK, 256), lambda i: (jnp.maximum(i * nb - 1, 0), 0))
    return pl.pallas_call(
        body, name="mixers_bwd", grid=(n_steps,),
        in_specs=[ANY, _rows(tq, 512), halo_prev, _rows(tq, 512), _rows(tq, 256), halo_kv,
                  _rows(tq, 512), halo_next, _rows(tq, 512),
                  _whole((4, 128, 128)), _whole((1, 512)), _whole((1, 512)),
                  _whole((N_KV_HEADS, BLOCK, GROUP * BLOCK)), _whole((N_KV_HEADS, 1, GROUP * BLOCK))],
        out_specs=[_rows(tq, 512), _rows(tq, 512), _whole((s_len, 256)),
                   _whole((MIX_ROWS, LANES)), _whole((1, 512))],
        out_shape=[jax.ShapeDtypeStruct((s_len, 512), BF16), jax.ShapeDtypeStruct((s_len, 512), BF16),
                   jax.ShapeDtypeStruct((s_len, 256), BF16), jax.ShapeDtypeStruct((MIX_ROWS, LANES), F32),
                   jax.ShapeDtypeStruct((1, 512), F32)],
        scratch_shapes=[pltpu.VMEM((N_KV_HEADS, tq + BLOCK, LANES), BF16)] * 2
        + [pltpu.VMEM((N_KV_HEADS, 1, GROUP * BLOCK), F32), pltpu.VMEM((s_len + BLOCK, 256), F32)],
        compiler_params=_params(),
    )(after, zp, zp, q, kv, kv, dpm, dpm, do, pool_w, pool_b, pool_scale, bias_t, sink_row)


def _in_bwd(after, dzp, dq, dkv, dzg, w_in_t, x, dh1, g_mix):
    s_len = x.shape[0]
    tm = min(512, s_len)

    def body(after_ref, dzp_ref, dq_ref, dkv_ref, dzg_ref, w_ref, x_ref, dh1_ref, g_ref, dx_ref, dg_ref):
        i = pl.program_id(0)

        @pl.when(i == 0)
        def _():
            dg_ref[...] = jnp.zeros_like(dg_ref)

        du = _dot(dzp_ref[...], w_ref[0:512, :])
        du = du + _dot(dq_ref[...], w_ref[512:1024, :])
        du = du + _dot(dkv_ref[...], w_ref[1024:1280, :])
        du = du + _dot(dzg_ref[...], w_ref[1280:3328, :])
        r, xh, _ = _rms_fwd(x_ref[...], g_ref[...])
        dxn, dg = _rms_bwd(du, xh, r, g_ref[...])
        dg_ref[...] += dg
        dx_ref[...] = dh1_ref[...] + dxn

    return pl.pallas_call(
        body, name="in_bwd", grid=(s_len // tm,),
        in_specs=[ANY, _rows(tm, 512), _rows(tm, 512), _rows(tm, 256), _rows(tm, 2048), _whole((IN_WIDTH, D_MODEL)),
                  _rows(tm, D_MODEL), _rows(tm, D_MODEL), _whole((1, D_MODEL))],
        out_specs=[_rows(tm, D_MODEL), _whole((1, D_MODEL))],
        out_shape=[jax.ShapeDtypeStruct((s_len, D_MODEL), F32), jax.ShapeDtypeStruct((1, D_MODEL), F32)],
        compiler_params=_params(),
    )(after, dzp, dq, dkv, dzg, w_in_t, x, dh1, g_mix)


def _all_gather_weights(name, shards, after=None):
    n = len(shards)
    extra = [] if after is None else [after]
    n_extra = len(extra)

    def body(*refs):
        ins, outs = refs[:n], refs[n + n_extra:2 * n + n_extra]
        send_sems, recv_sems, local_sems = refs[2 * n + n_extra:]
        x, y, c = lax.axis_index("x"), lax.axis_index("y"), lax.axis_index("c")
        me, sibling = (x, y, c), (x, y, 1 - c)
        chips = [(1 - x, y), (x, 1 - y), (1 - x, 1 - y)]

        def slot(a, px, py, pc):
            return outs[a].at[4 * px + 2 * py + pc]

        def copy(a, k, block, to, src=None):
            return pltpu.make_async_remote_copy(
                src_ref=slot(a, *block) if src is None else src, dst_ref=slot(a, *block),
                send_sem=send_sems.at[a, k], recv_sem=recv_sems.at[a, k], device_id=to, device_id_type=MESH)

        mine = [pltpu.make_async_copy(ins[a], slot(a, *me), local_sems.at[a]) for a in range(n)]
        for cp in mine:
            cp.start()
        first = []
        for a in range(n):
            first.append(copy(a, 0, me, sibling, src=ins[a]))
            first += [copy(a, 1 + j, me, (*chip, c), src=ins[a]) for j, chip in enumerate(chips)]
        for cp in first:
            cp.start()
        passed = []
        for a in range(n):
            for j, chip in enumerate(chips):
                copy(a, 1 + j, (*chip, c), me).wait_recv()
                cp = copy(a, 4 + j, (*chip, c), sibling)
                cp.start()
                passed.append(cp)
        for a in range(n):
            copy(a, 0, sibling, me).wait_recv()
            for j, chip in enumerate(chips):
                copy(a, 4 + j, (*chip, 1 - c), me).wait_recv()
        for cp in first + passed:
            cp.wait_send()
        for cp in mine:
            cp.wait()

    return pl.pallas_call(
        body, name=name,
        in_specs=[ANY] * (n + n_extra), out_specs=[ANY] * n,
        out_shape=[jax.ShapeDtypeStruct((N_DEV,) + s.shape, s.dtype) for s in shards],
        scratch_shapes=[pltpu.SemaphoreType.DMA((n, 7)), pltpu.SemaphoreType.DMA((n, 7)), pltpu.SemaphoreType.DMA((n,))],
    )(*shards, *extra)


HBM_SPEC = pl.BlockSpec(memory_space=pltpu.HBM)
SEM_SPEC = pl.BlockSpec(memory_space=pltpu.SEMAPHORE)
DATAFLOW = pltpu.SideEffectType.DATAFLOW_SIDE_EFFECTING
N_PEERS = N_DEV - 1


CHIP_PEERS = (1, 2, 4, 6)
RELAYED = (2, 4, 6)


def _peer_copies(srcs, lands, scatter, send_sems, recv_sems):
    x, y, c = lax.axis_index("x"), lax.axis_index("y"), lax.axis_index("c")
    me_idx = 4 * x + 2 * y + c
    copies = []
    for k in range(1, N_DEV):
        px = 1 - x if (k >> 2) & 1 else x
        py = 1 - y if (k >> 1) & 1 else y
        pc = 1 - c if k & 1 else c
        p_idx = 4 * px + 2 * py + pc
        for a in range(len(srcs)):
            if scatter[a] == "chip" and k not in CHIP_PEERS:
                continue
            src = srcs[a].at[p_idx] if scatter[a] is True else srcs[a]
            dst = lands[a].at[k] if scatter[a] is True else lands[a].at[me_idx]
            copies.append(pltpu.make_async_remote_copy(
                src_ref=src, dst_ref=dst, send_sem=send_sems.at[a * N_PEERS + k - 1],
                recv_sem=recv_sems.at[a * N_PEERS + k - 1],
                device_id=(px, py, pc), device_id_type=MESH))
    return copies


def _exchange_start(name, srcs, scatter, after):
    n = len(srcs)
    lands = [lax.empty(s.shape if sc is True else (N_DEV,) + s.shape, s.dtype) for s, sc in zip(srcs, scatter)]

    def body(*refs):
        src_refs, land_refs = refs[:n], refs[n:2 * n]
        send_sems, recv_sems = refs[2 * n + 1], refs[2 * n + 2]
        token = refs[4 * n + 3]
        for cp in _peer_copies(src_refs, land_refs, scatter, send_sems, recv_sems):
            cp.start()
        token[...] = jnp.zeros_like(token)

    hbm = lambda t: pltpu.HBM(t.shape, t.dtype)
    outs = pl.pallas_call(
        body, name=name,
        out_shape=[pltpu.SemaphoreType.DMA((n * N_PEERS,)), pltpu.SemaphoreType.DMA((n * N_PEERS,))]
        + [hbm(t) for t in srcs] + [hbm(t) for t in lands] + [jax.ShapeDtypeStruct((8, LANES), F32)],
        in_specs=[HBM_SPEC] * (2 * n) + [ANY],
        out_specs=[SEM_SPEC, SEM_SPEC] + [HBM_SPEC] * (2 * n) + [pl.BlockSpec(memory_space=pltpu.VMEM)],
        input_output_aliases={i: 2 + i for i in range(2 * n)},
        compiler_params=pltpu.CompilerParams(has_side_effects=DATAFLOW),
    )(*[pltpu.with_memory_space_constraint(t, pltpu.HBM) for t in list(srcs) + lands], after)
    return dict(n=n, scatter=scatter, send_sems=outs[0], recv_sems=outs[1], srcs=outs[2:2 + n],
                lands=outs[2 + n:2 + 2 * n], token=outs[2 + 2 * n])


def _exchange_wait(name, handle, after):
    n, scatter = handle["n"], handle["scatter"]

    def body(*refs):
        src_refs, land_refs = refs[:n], refs[n:2 * n]
        send_sems, recv_sems = refs[2 * n], refs[2 * n + 1]
        for cp in _peer_copies(src_refs, land_refs, scatter, send_sems, recv_sems):
            cp.wait_send()
            cp.wait_recv()

    both = list(handle["srcs"]) + list(handle["lands"])
    outs = pl.pallas_call(
        body, name=name,
        out_shape=[pltpu.HBM(t.shape, t.dtype) for t in both],
        in_specs=[HBM_SPEC] * (2 * n) + [SEM_SPEC, SEM_SPEC, ANY],
        out_specs=[HBM_SPEC] * (2 * n),
        input_output_aliases={i: i for i in range(2 * n)},
        compiler_params=pltpu.CompilerParams(has_side_effects=DATAFLOW),
    )(*both, handle["send_sems"], handle["recv_sems"], after)
    me_idx = _my_index()
    lands = [lax.dynamic_update_index_in_dim(land, src, me_idx, 0) if sc is False else land
             for land, src, sc in zip(outs[n:], outs[:n], scatter)]
    return lands, outs[:n]


def _my_index():
    return 4 * lax.axis_index("x") + 2 * lax.axis_index("y") + lax.axis_index("c")


N_RELAYED = len(RELAYED) + 1


def _relay_copies(bufs, send_sems, recv_sems):
    x, y, c = lax.axis_index("x"), lax.axis_index("y"), lax.axis_index("c")
    slots = []
    for k in RELAYED:
        px = 1 - x if (k >> 2) & 1 else x
        py = 1 - y if (k >> 1) & 1 else y
        slots.append(4 * px + 2 * py + c)
    slots.append(4 * x + 2 * y + (1 - c))
    copies = []
    for j, slot in enumerate(slots):
        for a, buf in enumerate(bufs):
            copies.append(pltpu.make_async_remote_copy(
                src_ref=buf.at[slot], dst_ref=buf.at[slot], send_sem=send_sems.at[a * N_RELAYED + j],
                recv_sem=recv_sems.at[a * N_RELAYED + j], device_id=(x, y, 1 - c), device_id_type=MESH))
    return copies


def _relay_start(name, bufs, after):
    n = len(bufs)

    def body(*refs):
        send_sems, recv_sems = refs[n + 1], refs[n + 2]
        for cp in _relay_copies(refs[:n], send_sems, recv_sems):
            cp.start()
        token = refs[2 * n + 3]
        token[...] = jnp.zeros_like(token)

    n_sems = n * N_RELAYED
    outs = pl.pallas_call(
        body, name=name,
        out_shape=[pltpu.SemaphoreType.DMA((n_sems,)), pltpu.SemaphoreType.DMA((n_sems,))]
        + [pltpu.HBM(t.shape, t.dtype) for t in bufs] + [jax.ShapeDtypeStruct((8, LANES), F32)],
        in_specs=[HBM_SPEC] * n + [ANY],
        out_specs=[SEM_SPEC, SEM_SPEC] + [HBM_SPEC] * n + [pl.BlockSpec(memory_space=pltpu.VMEM)],
        input_output_aliases={i: 2 + i for i in range(n)},
        compiler_params=pltpu.CompilerParams(has_side_effects=DATAFLOW),
    )(*[pltpu.with_memory_space_constraint(t, pltpu.HBM) for t in bufs], after)
    return dict(n=n, send_sems=outs[0], recv_sems=outs[1], bufs=outs[2:2 + n], token=outs[2 + n])


def _relay_wait(name, handle, after):
    n = handle["n"]

    def body(*refs):
        for cp in _relay_copies(refs[:n], refs[n], refs[n + 1]):
            cp.wait_send()
            cp.wait_recv()

    return pl.pallas_call(
        body, name=name,
        out_shape=[pltpu.HBM(t.shape, t.dtype) for t in handle["bufs"]],
        in_specs=[HBM_SPEC] * n + [SEM_SPEC, SEM_SPEC, ANY],
        out_specs=[HBM_SPEC] * n,
        input_output_aliases={i: i for i in range(n)},
        compiler_params=pltpu.CompilerParams(has_side_effects=DATAFLOW),
    )(*handle["bufs"], handle["send_sems"], handle["recv_sems"], after)


def _adamw(parts, w, m, v, sent=None):
    r, c = w.shape
    tr = 256 if r % 256 == 0 else r
    own = sent is not None

    def body(*refs):
        if own:
            _, p_ref, own_ref, w_ref, m_ref, v_ref, g_ref, d_ref, nm_ref, nv_ref = refs
            g = own_ref[...].astype(F32)
        else:
            p_ref, w_ref, m_ref, v_ref, g_ref, d_ref, nm_ref, nv_ref = refs
            g = p_ref[0].astype(F32)
        for k in range(1, N_DEV):
            g = g + p_ref[k].astype(F32)
        m_new = ADAM_B1 * m_ref[...] + (1.0 - ADAM_B1) * g
        v_new = ADAM_B2 * v_ref[...] + (1.0 - ADAM_B2) * (g * g)
        m_hat = m_new / (1.0 - ADAM_B1 ** ADAM_STEP)
        v_hat = v_new / (1.0 - ADAM_B2 ** ADAM_STEP)
        g_ref[...] = g
        d_ref[...] = -ADAM_LR * (m_hat / (jnp.sqrt(v_hat) + ADAM_EPS) + ADAM_WD * w_ref[...])
        nm_ref[...] = m_new
        nv_ref[...] = v_new

    out_shape = [jax.ShapeDtypeStruct((r, c), F32)] * 4
    if not own:
        return pl.pallas_call(
            body, name="adamw", grid=(r // tr,),
            in_specs=[pl.BlockSpec((N_DEV, tr, c), lambda i: (0, i, 0))] + [_rows(tr, c)] * 3,
            out_specs=[_rows(tr, c)] * 4, out_shape=out_shape, compiler_params=_params(),
        )(parts, w, m, v)
    rows = pl.BlockSpec((tr, c), lambda i, me: (i, 0))
    return pl.pallas_call(
        body, name="adamw_own", out_shape=out_shape, compiler_params=_params(),
        grid_spec=pltpu.PrefetchScalarGridSpec(
            num_scalar_prefetch=1, grid=(r // tr,),
            in_specs=[pl.BlockSpec((N_DEV, tr, c), lambda i, me: (0, i, 0)),
                      pl.BlockSpec((None, tr, c), lambda i, me: (me[0], i, 0))] + [rows] * 3,
            out_specs=[rows] * 4),
    )(_my_index().reshape(1).astype(jnp.int32), parts, sent, w, m, v)


def _adam_step(g, w, m, v):
    m_new = ADAM_B1 * m + (1.0 - ADAM_B1) * g
    v_new = ADAM_B2 * v + (1.0 - ADAM_B2) * (g * g)
    m_hat = m_new / (1.0 - ADAM_B1 ** ADAM_STEP)
    v_hat = v_new / (1.0 - ADAM_B2 ** ADAM_STEP)
    return -ADAM_LR * (m_hat / (jnp.sqrt(v_hat) + ADAM_EPS) + ADAM_WD * w), m_new, v_new


SMALL_NAMES = ("norm_mix", "pool_w", "pool_b", "pool_scale", "attn_sinks", "norm_mlp", "norm_final")


def _adamw_small(mlp_all, mix_all, scale_all, nmix_all, w, m, v):
    def body(mlp_ref, mix_ref, scale_ref, nmix_ref, *refs):
        ins, outs = refs[:21], refs[21:]

        def total(ref, rows, lanes=slice(None)):
            g = ref[0, rows, lanes]
            for k in range(1, N_DEV):
                g = g + ref[k, rows, lanes]
            return g

        grads = dict(
            norm_mix=total(nmix_ref, slice(0, 1)), pool_w=total(mix_ref, slice(0, MIX_POOL_B)),
            pool_b=total(mix_ref, slice(MIX_POOL_B, MIX_POOL_B + 4)), pool_scale=total(scale_ref, slice(0, 1)),
            attn_sinks=total(mix_ref, slice(MIX_SINKS, MIX_SINKS + 1)),
            norm_mlp=total(mlp_ref, slice(0, 1)), norm_final=total(mlp_ref, slice(1, 2)))
        for i, name in enumerate(SMALL_NAMES):
            g = grads[name]
            d, m_new, v_new = _adam_step(g, ins[3 * i][...], ins[3 * i + 1][...], ins[3 * i + 2][...])
            for ref, val in zip(outs[4 * i:4 * i + 4], (g, d, m_new, v_new)):
                ref[...] = val
        outs[28][...] = jnp.broadcast_to(total(mlp_ref, slice(2, 3), slice(0, LANES)), (8, LANES))

    operands, out_shape = [], []
    for name in SMALL_NAMES:
        operands += [w[name], m[name], v[name]]
        out_shape += [jax.ShapeDtypeStruct(w[name].shape, F32)] * 4
    out_shape.append(jax.ShapeDtypeStruct((8, LANES), F32))
    outs = pl.pallas_call(body, name="adamw_small", out_shape=out_shape)(
        mlp_all, mix_all, scale_all, nmix_all, *operands)
    return {name: outs[4 * i:4 * i + 4] for i, name in enumerate(SMALL_NAMES)}, outs[28]


def kernel(x, norm_mix, w_in, pool_w, pool_b, pool_scale, attn_sinks, p_pool, p_attn, w_out, norm_mlp, w_up, w_down, norm_final, loss_target, m_norm_mix, m_w_in, m_pool_w, m_pool_b, m_pool_scale, m_attn_sinks, m_p_pool, m_p_attn, m_w_out, m_norm_mlp, m_w_up, m_w_down, m_norm_final, v_norm_mix, v_w_in, v_pool_w, v_pool_b, v_pool_scale, v_attn_sinks, v_p_pool, v_p_attn, v_w_out, v_norm_mlp, v_w_up, v_w_down, v_norm_final):
    xs = x[0]
    tgt = loss_target[0]
    s_len = xs.shape[0]

    p_pool_bf, p_attn_bf, w_out_bf, w_up_bf, w_down_bf = [
        t[0].astype(BF16) for t in (p_pool, p_attn, w_out, w_up, w_down)]
    w_in_bf = w_in[0].T.astype(BF16)
    (w_in_g,) = _all_gather_weights("all_gather_w_in", [w_in_bf])
    ag_rest = _exchange_start(
        "ag_rest_start", [p_pool_bf, p_attn_bf, w_out_bf, w_up_bf, w_down_bf], ("chip",) * 5, w_in_g)

    pool_w_bf = pool_w[0].astype(BF16)
    pool_b_row = pool_b[0].reshape(1, POOL_WIDTH)
    bias_t, sink_row = _attn_constants(attn_sinks[0])

    w_in_t = w_in_g.reshape(IN_WIDTH, D_MODEL)
    u, zp, q, kv, zg = _fwd_in(ag_rest["token"], xs, norm_mix, w_in_t)
    n_mixer_tiles = s_len // min(512, s_len)
    half = min(n_mixer_tiles - 1, n_mixer_tiles // 2 + 1)
    mixer_args = (zp, q, kv, pool_w_bf, pool_b_row, pool_scale, bias_t, sink_row)
    pm, o = _mixers_fwd(*mixer_args, n_tiles=half) if half else (None, None)
    first_level, _ = _exchange_wait("ag_rest_wait", ag_rest, zg if o is None else o)
    relay = _relay_start("ag_relay_start", first_level, zg)
    pm, o = _mixers_fwd(*mixer_args, first_tile=half, earlier=None if o is None else (relay["token"], pm, o))
    p_pool_g, p_attn_g, w_out_g, w_up_g, w_down_g = _relay_wait("ag_relay_wait", relay, o)
    p_pool_f = p_pool_g.transpose(1, 0, 2).reshape(POOL_WIDTH, D_MODEL)
    p_attn_f = p_attn_g.transpose(1, 0, 2).reshape(ATTN_WIDTH, D_MODEL)
    w_out_f = w_out_g.reshape(D_MODEL, D_MODEL)
    w_down_f = w_down_g.reshape(D_FF, D_MODEL)
    dh1, a, dapre, pack, small_mlp, dyp, dya, dzg, dpm, do = _core(
        xs, pm, o, zg, tgt, norm_mlp, norm_final.reshape(1, D_MODEL), p_pool_f, p_attn_f, w_out_f, w_up_g, w_down_f)
    gw_down = _tn_matmul(a, pack, square_a=True, b_cols=(2 * D_MODEL, D_MODEL))
    gw_up = _tn_matmul(pack, dapre, col_blocks=N_DEV, a_cols=(D_MODEL, D_MODEL))
    ex_mlp = _exchange_start(
        "ex_mlp_start", [gw_up, gw_down.reshape(N_DEV, D_FF // N_DEV, D_MODEL)], (True, True), small_mlp)
    dzp, dq, dkv, small_mix, g_pool_scale = _mixers_bwd(
        ex_mlp["token"], zp, q, kv, dpm, do, pool_w_bf, pool_b_row, pool_scale, bias_t, sink_row)
    gw_in = _tn_w_in(u, dzp, dq, dkv, dzg)
    ex_in = _exchange_start(
        "ex_in_start", [gw_in, small_mlp, small_mix, g_pool_scale], (True, False, False, False), dq)
    gw_out = _tn_matmul(pack, pack, after=ex_in["token"], a_cols=(0, D_MODEL), b_cols=(3 * D_MODEL, D_MODEL))
    gp_pool = _tn_matmul(pm, dyp, col_blocks=N_DEV, after=ex_in["token"])
    gp_attn = _tn_matmul(o, dya, col_blocks=N_DEV, after=ex_in["token"])
    ex_proj = _exchange_start(
        "ex_proj_start", [gp_pool, gp_attn, gw_out.reshape(N_DEV, D_MODEL // N_DEV, D_MODEL)], (True,) * 3,
        ex_in["token"])
    dx, g_norm_mix = _in_bwd(ex_proj["token"], dzp, dq, dkv, dzg, w_in_t, xs, dh1, norm_mix)

    big_w = dict(w_in=w_in, p_pool=p_pool, p_attn=p_attn, w_out=w_out, w_up=w_up, w_down=w_down)
    big_m = dict(w_in=m_w_in, p_pool=m_p_pool, p_attn=m_p_attn, w_out=m_w_out, w_up=m_w_up, w_down=m_w_down)
    big_v = dict(w_in=v_w_in, p_pool=v_p_pool, p_attn=v_p_attn, w_out=v_w_out, w_up=v_w_up, w_down=v_w_down)
    res = {}

    def update(names, recvs, sents):
        for name, parts, sent in zip(names, recvs, sents):
            flip = (lambda t: t.T) if name == "w_in" else (lambda t: t)
            outs = _adamw(parts, flip(big_w[name][0]), flip(big_m[name][0]), flip(big_v[name][0]), sent)
            res[name] = [flip(t)[None] for t in outs]

    update(["w_up", "w_down"], *_exchange_wait("ex_mlp_wait", ex_mlp, dx))
    (norm_mix_all,) = _all_gather_weights("all_gather_norm_mix", [g_norm_mix], res["w_down"][0])
    (r_in, mlp_all, mix_all, scale_all), (s_in, _, _, _) = _exchange_wait("ex_in_wait", ex_in, norm_mix_all)
    update(["w_in"], [r_in], [s_in])

    natural = dict(norm_mix=(1, D_MODEL), pool_w=(MIX_POOL_B, LANES), pool_b=(4, LANES), pool_scale=(1, POOL_WIDTH),
                   attn_sinks=(1, LANES), norm_mlp=(1, D_MODEL), norm_final=(1, D_MODEL))

    def as_2d(t, name):
        if name == "attn_sinks":
            return jnp.pad(t, ((0, 0), (0, LANES - N_HEADS)))
        return t.reshape(natural[name])

    small_w = dict(norm_mix=norm_mix, pool_w=pool_w, pool_b=pool_b, pool_scale=pool_scale, attn_sinks=attn_sinks,
                   norm_mlp=norm_mlp, norm_final=norm_final)
    small_m = dict(norm_mix=m_norm_mix, pool_w=m_pool_w, pool_b=m_pool_b, pool_scale=m_pool_scale,
                   attn_sinks=m_attn_sinks, norm_mlp=m_norm_mlp, norm_final=m_norm_final)
    small_v = dict(norm_mix=v_norm_mix, pool_w=v_pool_w, pool_b=v_pool_b, pool_scale=v_pool_scale,
                   attn_sinks=v_attn_sinks, norm_mlp=v_norm_mlp, norm_final=v_norm_final)
    small_res, loss_all = _adamw_small(
        mlp_all, mix_all, scale_all, norm_mix_all,
        *[{k: as_2d(t, k) for k, t in d.items()} for d in (small_w, small_m, small_v)])
    loss = loss_all[0, 0]
    for name in SMALL_NAMES:
        shape = small_w[name].shape
        res[name] = [(t[:, :N_HEADS] if name == "attn_sinks" else t).reshape(shape) for t in small_res[name]]
    update(["p_pool", "p_attn", "w_out"], *_exchange_wait("ex_proj_wait", ex_proj, loss_all))

    order = ["norm_mix", "w_in", "pool_w", "pool_b", "pool_scale", "attn_sinks", "p_pool", "p_attn", "w_out",
             "norm_mlp", "w_up", "w_down", "norm_final"]
    out = [loss, dx[None]]
    for kind in range(4):
        out += [res[name][kind] for name in order]
    return tuple(out)
```

```python
import functools
import math

import numpy as np
import jax
import jax.numpy as jnp
from jax import lax
from jax.experimental import pallas as pl
from jax.experimental.pallas import tpu as pltpu

F32 = jnp.float32
BF16 = jnp.bfloat16

D_MODEL = 1024
POOL_WIDTH = 512
ATTN_WIDTH = 512
KV_WIDTH = 128
HEAD_DIM = 64
N_HEADS = 8
N_KV_HEADS = 2
GROUP = 4
BLOCK = 128
POOL_WINDOWS = (2, 4, 8, 16)
POOL_GROUP_DIM = 128
POOL_HALO = 16
D_FF = 4096
FF_CHUNK = 1024
IN_WIDTH = 3328
RMS_EPS = 1e-5
NEG_INF = -1e30
ATTN_SCALE = 1.0 / math.sqrt(HEAD_DIM)
N_DEV = 8

ADAM_LR = 0.001
ADAM_B1 = 0.9
ADAM_B2 = 0.999
ADAM_EPS = 1e-08
ADAM_WD = 0.01
ADAM_STEP = 10

LANES = 128
VMEM_LIMIT_BYTES = 56 * 1024 * 1024
MESH = pl.DeviceIdType.MESH


def _params(n_grid_axes=1):
    return pltpu.CompilerParams(
        dimension_semantics=("arbitrary",) * n_grid_axes, vmem_limit_bytes=VMEM_LIMIT_BYTES)


def _dot(a, b):
    return jnp.dot(a, b, preferred_element_type=F32)


def _dot_nt(a, b):
    return lax.dot_general(a, b, (((1,), (1,)), ((), ())), preferred_element_type=F32)


def _dot_tn(a, b):
    return lax.dot_general(a, b, (((0,), (0,)), ((), ())), preferred_element_type=F32)


ANY = pl.BlockSpec(memory_space=pl.ANY)


def _rows(tm, n):
    return pl.BlockSpec((tm, n), lambda i: (i, 0))


def _whole(shape):
    zeros = (0,) * len(shape)
    return pl.BlockSpec(shape, lambda i: zeros)


def _rms_fwd(h, g):
    r = lax.rsqrt(jnp.mean(h * h, axis=-1, keepdims=True) + RMS_EPS)
    xh = h * r
    return r, xh, xh * g


def _rms_bwd(dy, xh, r, g):
    dxh = dy * g
    dh = r * (dxh - xh * jnp.mean(dxh * xh, axis=-1, keepdims=True))
    return dh, jnp.sum(dy * xh, axis=0, keepdims=True)


def _fwd_in(after, x, g_mix, w_in_t):
    s_len = x.shape[0]
    tm = min(512, s_len)

    def body(after_ref, x_ref, g_ref, w_ref, u_ref, zp_ref, q_ref, kv_ref, zg_ref):
        _, _, u = _rms_fwd(x_ref[...], g_ref[...])
        u = u.astype(BF16)
        u_ref[...] = u
        zp_ref[...] = _dot_nt(u, w_ref[0:512, :]).astype(BF16)
        q_ref[...] = _dot_nt(u, w_ref[512:1024, :]).astype(BF16)
        kv_ref[...] = _dot_nt(u, w_ref[1024:1280, :]).astype(BF16)
        zg_ref[...] = _dot_nt(u, w_ref[1280:3328, :]).astype(BF16)

    return pl.pallas_call(
        body, name="fwd_in", grid=(s_len // tm,),
        in_specs=[ANY, _rows(tm, D_MODEL), _whole((1, D_MODEL)),
                  pl.BlockSpec((IN_WIDTH, D_MODEL), lambda i: (0, 0), pipeline_mode=pl.Buffered(1))],
        out_specs=[_rows(tm, D_MODEL), _rows(tm, 512), _rows(tm, 512), _rows(tm, 256), _rows(tm, 2048)],
        out_shape=[jax.ShapeDtypeStruct((s_len, n), BF16) for n in (D_MODEL, 512, 512, 256, 2048)],
        compiler_params=_params(),
    )(after, x, g_mix, w_in_t)


def _attn_constants(sinks):
    r = np.arange(BLOCK)[:, None]
    qi = np.arange(BLOCK)[None, :]
    dist = np.where(r <= qi, qi - r, BLOCK + qi - r).astype(np.float32)
    slopes = np.array([2.0 ** (-8.0 * (h + 1) / N_HEADS) for h in range(N_HEADS)], dtype=np.float32)
    bias = (-slopes[:, None, None] * dist[None]).reshape(N_KV_HEADS, GROUP, BLOCK, BLOCK)
    bias = np.ascontiguousarray(bias.transpose(0, 2, 1, 3)).reshape(N_KV_HEADS, BLOCK, GROUP * BLOCK)
    sink_row = jnp.repeat(sinks.astype(F32).reshape(N_KV_HEADS, GROUP), BLOCK, axis=1)[:, None, :]
    return jnp.asarray(bias.astype(np.float32)), sink_row


def _own_block_mask():
    shape = (BLOCK, GROUP * BLOCK)
    r = lax.broadcasted_iota(jnp.int32, shape, 0)
    qi = lax.broadcasted_iota(jnp.int32, shape, 1) & (BLOCK - 1)
    return r <= qi


def _pack_keys(t, own):
    return jnp.where(own, t[BLOCK:], t[:BLOCK])


def _unpack_keys(t, own):
    zero = jnp.zeros_like(t)
    return jnp.concatenate([jnp.where(own, zero, t), jnp.where(own, t, zero)], axis=0)


def _left_half(shape):
    return lax.broadcasted_iota(jnp.int32, shape, 1) < HEAD_DIM


def _dup_halves(slab):
    swapped = pltpu.roll(slab, HEAD_DIM, 1)
    left = _left_half(slab.shape)
    return jnp.where(left, slab, swapped), jnp.where(left, swapped, slab)


def _fill_kv_slabs(kvh_ref, kv_ref, ka_ref, vd_ref):
    for rows, src in ((slice(0, BLOCK), kvh_ref), (slice(BLOCK, None), kv_ref)):
        kvf = src[...].astype(F32)
        for ref, lanes in ((ka_ref, slice(0, KV_WIDTH)), (vd_ref, slice(KV_WIDTH, 2 * KV_WIDTH))):
            d0, d1 = _dup_halves(kvf[:, lanes])
            ref[0, rows, :] = d0.astype(BF16)
            ref[1, rows, :] = d1.astype(BF16)


def _stack_pairs(a, h):
    pieces = []
    for j in range(2):
        pair = a[:, h * 256 + j * LANES:h * 256 + (j + 1) * LANES]
        left = _left_half(pair.shape)
        zero = jnp.zeros_like(pair)
        pieces += [jnp.where(left, pair, zero), jnp.where(left, zero, pair)]
    return jnp.concatenate(pieces, axis=0)


def _attn_probs(kk, q_st, bias_p, sink_row, own, first):
    s = _pack_keys(_dot_nt(kk, q_st), own) * ATTN_SCALE + bias_p
    if first is not None:
        s = jnp.where(jnp.logical_and(first, jnp.logical_not(own)), NEG_INF, s)
    m = jnp.maximum(jnp.max(s, axis=0, keepdims=True), sink_row)
    p = jnp.exp(s - m)
    es = jnp.exp(sink_row - m)
    inv = 1.0 / (jnp.sum(p, axis=0, keepdims=True) + es)
    return p * inv, es * inv


def _pool_d(ext, cur, g, row0):
    w = POOL_WINDOWS[g]
    acc = ext
    k = 1
    while k < w:
        acc = acc + pltpu.roll(acc, k, 0)
        k *= 2
    return _window_mean(acc[POOL_HALO:, :], w, row0) - cur


def _window_mean(total, w, row0):
    t = row0 + lax.broadcasted_iota(jnp.int32, (POOL_HALO, total.shape[1]), 0)
    head = total[:POOL_HALO] / jnp.minimum(t + 1, w).astype(F32)
    return jnp.concatenate([head, total[POOL_HALO:] * (1.0 / w)], axis=0)


def _mixers_fwd(zp, q, kv, pool_w, pool_b, pool_scale, bias_t, sink_row, first_tile=0, n_tiles=None, earlier=None):
    s_len = zp.shape[0]
    tq = min(512, s_len)
    nb = tq // BLOCK
    n_tiles = s_len // tq - first_tile if n_tiles is None else n_tiles
    extra = [] if earlier is None else list(earlier)

    def body(zp_ref, zph_ref, q_ref, kv_ref, kvh_ref, pw_ref, pb_ref, ps_ref, bias_ref, sink_ref, *rest):
        pm_ref, o_ref, ka_ref, vd_ref = rest[len(extra):]
        i = pl.program_id(0) + first_tile
        cur = zp_ref[...].astype(F32)
        halo = zph_ref[...].astype(F32) * (i > 0).astype(F32)
        ext = jnp.concatenate([halo, cur], axis=0)
        for g in range(4):
            sl = slice(g * POOL_GROUP_DIM, (g + 1) * POOL_GROUP_DIM)
            d = _pool_d(ext[:, sl], cur[:, sl], g, i * tq)
            y = _dot(d.astype(BF16), pw_ref[g]) + pb_ref[:, sl]
            pm_ref[:, sl] = (y * ps_ref[:, sl]).astype(BF16)
        _fill_kv_slabs(kvh_ref, kv_ref, ka_ref, vd_ref)
        own = _own_block_mask()
        for b in range(nb):
            rq = slice(b * BLOCK, (b + 1) * BLOCK)
            rk = slice(b * BLOCK, (b + 2) * BLOCK)
            qb = q_ref[rq, :]
            for h in range(N_KV_HEADS):
                pn, _ = _attn_probs(ka_ref[h, rk, :], _stack_pairs(qb, h), bias_ref[h], sink_ref[h], own,
                                    (i == 0) if b == 0 else None)
                pn = _unpack_keys(pn, own).astype(BF16)
                vd = vd_ref[h, rk, :]
                left = _left_half(vd.shape)
                zero = jnp.zeros_like(vd)
                va, vb = jnp.where(left, vd, zero), jnp.where(left, zero, vd)
                for j in range(2):
                    o_pair = (_dot_tn(pn[:, (2 * j) * BLOCK:(2 * j + 1) * BLOCK], va)
                              + _dot_tn(pn[:, (2 * j + 1) * BLOCK:(2 * j + 2) * BLOCK], vb))
                    o_ref[rq, h * 256 + j * LANES:h * 256 + (j + 1) * LANES] = o_pair.astype(BF16)

    def tile(n):
        return pl.BlockSpec((tq, n), lambda i: (i + first_tile, 0))

    halo_pool = pl.BlockSpec(
        (POOL_HALO, 512), lambda i: (jnp.maximum((i + first_tile) * (tq // POOL_HALO) - 1, 0), 0))
    halo_kv = pl.BlockSpec((BLOCK, 256), lambda i: (jnp.maximum((i + first_tile) * nb - 1, 0), 0))
    return pl.pallas_call(
        body, name="mixers_fwd", grid=(n_tiles,),
        in_specs=[tile(512), halo_pool, tile(512), tile(256), halo_kv,
                  _whole((4, 128, 128)), _whole((1, 512)), _whole((1, 512)),
                  _whole((N_KV_HEADS, BLOCK, GROUP * BLOCK)), _whole((N_KV_HEADS, 1, GROUP * BLOCK))]
        + [ANY] * len(extra),
        out_specs=[tile(512), tile(512)],
        out_shape=[jax.ShapeDtypeStruct((s_len, 512), BF16)] * 2,
        input_output_aliases={11: 0, 12: 1} if extra else {},
        scratch_shapes=[pltpu.VMEM((N_KV_HEADS, tq + BLOCK, LANES), BF16)] * 2,
        compiler_params=_params(),
    )(zp, zp, q, kv, kv, pool_w, pool_b, pool_scale, bias_t, sink_row, *extra)


def _gated_mix(pm, o, zg, pp_ref, pa_ref):
    yp = _dot_nt(pm, pp_ref[...])
    ya = _dot_nt(o, pa_ref[...])
    gp = jax.nn.sigmoid(zg[:, :D_MODEL].astype(F32))
    ga = jax.nn.sigmoid(zg[:, D_MODEL:].astype(F32))
    return yp, ya, gp, ga


def _core(x, pm, o, zg, tgt, g_mlp, g_fin, p_pool, p_attn, w_out, w_up_blocks, w_down):
    s_len = x.shape[0]
    tm = min(256, s_len)
    n_chunks = D_FF // FF_CHUNK
    up_block = D_FF // N_DEV
    per_chunk = FF_CHUNK // up_block

    def body(x_ref, pm_ref, o_ref, zg_ref, tgt_ref, gm_ref, gf_ref, pp_ref, pa_ref, wo_ref, wu_ref, wd_ref,
             dh1_ref, a_ref, dap_ref, pack_ref, small_ref,
             dyp_ref, dya_ref, dzg_ref, dpm_ref, do_ref):
        i = pl.program_id(0)

        @pl.when(i == 0)
        def _():
            small_ref[...] = jnp.zeros_like(small_ref)

        yp, ya, gp, ga = _gated_mix(pm_ref[...], o_ref[...], zg_ref[...], pp_ref, pa_ref)
        mixed = (gp * yp + ga * ya).astype(BF16)
        pack_ref[:, 0:D_MODEL] = mixed
        h1 = x_ref[...] + _dot(mixed, wo_ref[...])
        r2, xh2, u2 = _rms_fwd(h1, gm_ref[...])
        u2 = u2.astype(BF16)
        pack_ref[:, D_MODEL:2 * D_MODEL] = u2
        acc = jnp.zeros((tm, D_MODEL), F32)
        for c in range(n_chunks):
            cs = slice(c * FF_CHUNK, (c + 1) * FF_CHUNK)
            a = jnp.concatenate([_dot(u2, wu_ref[per_chunk * c + j]) for j in range(per_chunk)], axis=1)
            a = jnp.maximum(a, 0.0)
            a_ref[:, cs] = a.astype(BF16)
            acc = acc + _dot((a * a).astype(BF16), wd_ref[cs, :])
        h2 = h1 + acc
        r3, xh3, y = _rms_fwd(h2, gf_ref[...])
        diff = y - tgt_ref[...]
        small_ref[2:3, :] += 0.5 * jnp.sum(jnp.mean(diff * diff, axis=-1, keepdims=True))
        dy = diff * (1.0 / D_MODEL)
        dh2, dgf = _rms_bwd(dy, xh3, r3, gf_ref[...])
        small_ref[1:2, :] += dgf
        dh2_bf = dh2.astype(BF16)
        pack_ref[:, 2 * D_MODEL:3 * D_MODEL] = dh2_bf
        du2 = jnp.zeros((tm, D_MODEL), F32)
        for c in range(n_chunks):
            cs = slice(c * FF_CHUNK, (c + 1) * FF_CHUNK)
            ds = _dot_nt(dh2_bf, wd_ref[cs, :])
            dap = (ds * (2.0 * a_ref[:, cs].astype(F32))).astype(BF16)
            dap_ref[:, cs] = dap
            for j in range(per_chunk):
                du2 = du2 + _dot_nt(dap[:, j * up_block:(j + 1) * up_block], wu_ref[per_chunk * c + j])
        dh1n, dgm = _rms_bwd(du2, xh2, r2, gm_ref[...])
        small_ref[0:1, :] += dgm
        dh1 = dh2 + dh1n
        dh1_ref[...] = dh1
        dh1_bf = dh1.astype(BF16)
        pack_ref[:, 3 * D_MODEL:4 * D_MODEL] = dh1_bf
        dm = _dot_nt(dh1_bf, wo_ref[...])
        dyp = (dm * gp).astype(BF16)
        dya = (dm * ga).astype(BF16)
        dyp_ref[...] = dyp
        dya_ref[...] = dya
        dzg_ref[:, :D_MODEL] = (dm * yp * (gp * (1.0 - gp))).astype(BF16)
        dzg_ref[:, D_MODEL:] = (dm * ya * (ga * (1.0 - ga))).astype(BF16)
        dpm_ref[...] = _dot(dyp, pp_ref[...]).astype(BF16)
        do_ref[...] = _dot(dya, pa_ref[...]).astype(BF16)

    def fixed(shape):
        return pl.BlockSpec(shape, lambda i: (0,) * len(shape), pipeline_mode=pl.Buffered(1))

    widths_dtypes = ((D_MODEL, F32), (D_FF, BF16), (D_FF, BF16), (4 * D_MODEL, BF16))
    back = ((D_MODEL, BF16), (D_MODEL, BF16), (2048, BF16), (512, BF16), (512, BF16))
    return pl.pallas_call(
        body, name="core", grid=(s_len // tm,),
        in_specs=[_rows(tm, D_MODEL), _rows(tm, 512), _rows(tm, 512), _rows(tm, 2048), _rows(tm, D_MODEL),
                  _whole((1, D_MODEL)), _whole((1, D_MODEL)),
                  fixed((D_MODEL, 512)), fixed((D_MODEL, 512)), fixed((D_MODEL, D_MODEL)),
                  fixed((N_DEV, D_MODEL, up_block)), fixed((D_FF, D_MODEL))],
        out_specs=[_rows(tm, n) for n, _ in widths_dtypes] + [_whole((8, D_MODEL))] + [_rows(tm, n) for n, _ in back],
        out_shape=[jax.ShapeDtypeStruct((s_len, n), d) for n, d in widths_dtypes]
        + [jax.ShapeDtypeStruct((8, D_MODEL), F32)] + [jax.ShapeDtypeStruct((s_len, n), d) for n, d in back],
        compiler_params=_params(),
    )(x, pm, o, zg, tgt, g_mlp, g_fin, p_pool, p_attn, w_out, w_up_blocks, w_down)


def _tn_matmul(a, b, square_a=False, col_blocks=None, after=None, a_cols=None, b_cols=None):
    s_len = a.shape[0]
    a0, ka = (0, a.shape[1]) if a_cols is None else a_cols
    b0, nb = (0, b.shape[1]) if b_cols is None else b_cols
    tt = min(2048, s_len)
    tk = min(1024, ka)
    tn = min(1024, nb)
    n_t = s_len // tt
    if col_blocks is None:
        out_spec = pl.BlockSpec((tk, tn), lambda k, j, t: (k, j))
        out_shape = jax.ShapeDtypeStruct((ka, nb), BF16)
    else:
        width = nb // col_blocks
        per_tile = tn // width
        out_spec = pl.BlockSpec((per_tile, tk, width), lambda k, j, t: (j, k, 0))
        out_shape = jax.ShapeDtypeStruct((col_blocks, ka, width), BF16)

    extra = [] if after is None else [after]

    def body(a_ref, b_ref, *rest):
        o_ref, acc_ref = rest[len(extra):]
        t = pl.program_id(2)

        @pl.when(t == 0)
        def _():
            acc_ref[...] = jnp.zeros_like(acc_ref)

        av = a_ref[...]
        if square_a:
            av = av * av
        acc_ref[...] += _dot_tn(av.astype(BF16), b_ref[...].astype(BF16))

        @pl.when(t == n_t - 1)
        def _():
            if col_blocks is None:
                o_ref[...] = acc_ref[...].astype(o_ref.dtype)
            else:
                for blk in range(per_tile):
                    o_ref[blk] = acc_ref[:, blk * width:(blk + 1) * width].astype(o_ref.dtype)

    return pl.pallas_call(
        body, name="tn_matmul", grid=(ka // tk, nb // tn, n_t),
        in_specs=[pl.BlockSpec((tt, tk), lambda k, j, t: (t, k + a0 // tk)),
                  pl.BlockSpec((tt, tn), lambda k, j, t: (t, j + b0 // tn))]
        + [ANY] * len(extra),
        out_specs=out_spec, out_shape=out_shape,
        scratch_shapes=[pltpu.VMEM((tk, tn), F32)],
        compiler_params=_params(3),
    )(a, b, *extra)


def _tn_w_in(u, dzp, dq, dkv, dzg):
    s_len = u.shape[0]
    tt = min(1024, s_len)
    n_t = s_len // tt
    width = IN_WIDTH // N_DEV
    pieces = ((0, 512), (512, 1024), (1024, 1280), (1280, IN_WIDTH))

    def body(u_ref, dzp_ref, dq_ref, dkv_ref, dzg_ref, o_ref, acc_ref):
        t = pl.program_id(0)

        @pl.when(t == 0)
        def _():
            acc_ref[...] = jnp.zeros_like(acc_ref)

        uv = u_ref[...]
        for (c0, c1), ref in zip(pieces, (dzp_ref, dq_ref, dkv_ref, dzg_ref)):
            acc_ref[c0:c1, :] += _dot_tn(ref[...], uv)

        @pl.when(t == n_t - 1)
        def _():
            for j in range(N_DEV):
                o_ref[j] = acc_ref[j * width:(j + 1) * width, :].astype(BF16)

    return pl.pallas_call(
        body, name="tn_w_in", grid=(n_t,),
        in_specs=[_rows(tt, D_MODEL)] + [_rows(tt, c1 - c0) for c0, c1 in pieces],
        out_specs=_whole((N_DEV, width, D_MODEL)),
        out_shape=jax.ShapeDtypeStruct((N_DEV, width, D_MODEL), BF16),
        scratch_shapes=[pltpu.VMEM((IN_WIDTH, D_MODEL), F32)],
        compiler_params=_params(),
    )(u, dzp, dq, dkv, dzg)


MIX_POOL_B = 4 * POOL_GROUP_DIM
MIX_SINKS = MIX_POOL_B + 8
MIX_ROWS = MIX_SINKS + 8


def _mixers_bwd(after, zp, q, kv, dpm, do, pool_w, pool_b, pool_scale, bias_t, sink_row):
    s_len = zp.shape[0]
    tq = min(512, s_len)
    nb = tq // BLOCK
    n_steps = s_len // tq

    def body(after_ref, zp_ref, zph_ref, q_ref, kv_ref, kvh_ref, dpm_ref, dpmh_ref, do_ref, pw_ref, pb_ref, ps_ref,
             bias_ref, sink_ref, dzp_ref, dq_ref, dkv_ref, small_ref, dps_ref,
             ka_ref, vd_ref, dsk_acc, dkv_acc):
        i = pl.program_id(0)

        @pl.when(i == 0)
        def _():
            dkv_acc[...] = jnp.zeros_like(dkv_acc)
            small_ref[...] = jnp.zeros_like(small_ref)
            dps_ref[...] = jnp.zeros_like(dps_ref)
            dsk_acc[...] = jnp.zeros_like(dsk_acc)

        cur = zp_ref[...].astype(F32)
        halo = zph_ref[...].astype(F32) * (i > 0).astype(F32)
        ext = jnp.concatenate([halo, cur], axis=0)
        dpm_next = dpmh_ref[...].astype(F32) * (i < n_steps - 1).astype(F32)
        dpm_ext = jnp.concatenate([dpm_ref[...].astype(F32), dpm_next], axis=0)
        n_ext = tq + POOL_HALO
        for g in range(4):
            sl = slice(g * POOL_GROUP_DIM, (g + 1) * POOL_GROUP_DIM)
            w = POOL_WINDOWS[g]
            d = _pool_d(ext[:, sl], cur[:, sl], g, i * tq).astype(BF16)
            y_lin = _dot(d, pw_ref[g]) + pb_ref[:, sl]
            dps_ref[:, sl] += jnp.sum(dpm_ext[:tq, sl] * y_lin, axis=0, keepdims=True)
            dyl_ext = dpm_ext[:, sl] * ps_ref[:, sl]
            small_ref[MIX_POOL_B + g:MIX_POOL_B + g + 1, :] += jnp.sum(dyl_ext[:tq], axis=0, keepdims=True)
            dyl_bf = dyl_ext.astype(BF16)
            small_ref[g * POOL_GROUP_DIM:(g + 1) * POOL_GROUP_DIM, :] += _dot_tn(d, dyl_bf[:tq])
            dd = _dot_nt(dyl_bf, pw_ref[g])
            e = _window_mean(dd, w, i * tq)
            acc = e
            k = 1
            while k < w:
                acc = acc + pltpu.roll(acc, n_ext - k, 0)
                k *= 2
            dzp_ref[:, sl] = (acc[:tq] - dd[:tq]).astype(BF16)

        _fill_kv_slabs(kvh_ref, kv_ref, ka_ref, vd_ref)

        def fold(dup):
            return dup + pltpu.roll(dup, HEAD_DIM, 1)

        own = _own_block_mask()
        for b in range(nb):
            rq = slice(b * BLOCK, (b + 1) * BLOCK)
            rk = slice(b * BLOCK, (b + 2) * BLOCK)
            qb = q_ref[rq, :]
            dob = do_ref[rq, :]
            dk_dup, dv_dup = [], []
            for h in range(N_KV_HEADS):
                kk = ka_ref[h, rk, :]
                q_st = _stack_pairs(qb, h)
                do_st = _stack_pairs(dob, h)
                pn, psink = _attn_probs(kk, q_st, bias_ref[h], sink_ref[h], own, (i == 0) if b == 0 else None)
                dp = _pack_keys(_dot_nt(vd_ref[h, rk, :], do_st), own)
                delta = jnp.sum(pn * dp, axis=0, keepdims=True)
                dsk_acc[h] += -psink * delta
                ds = _unpack_keys((pn * (dp - delta)) * ATTN_SCALE, own).astype(BF16)
                pn = _unpack_keys(pn, own)
                dq_st = _dot_tn(ds, kk)
                for j in range(2):
                    left = _left_half((BLOCK, LANES))
                    dq_pair = jnp.where(left, dq_st[(2 * j) * BLOCK:(2 * j + 1) * BLOCK],
                                        dq_st[(2 * j + 1) * BLOCK:(2 * j + 2) * BLOCK])
                    dq_ref[rq, h * 256 + j * LANES:h * 256 + (j + 1) * LANES] = dq_pair.astype(BF16)
                dk_dup.append(fold(_dot(ds, q_st)))
                dv_dup.append(fold(_dot(pn.astype(BF16), do_st)))
            left = _left_half((2 * BLOCK, LANES))
            dkv_blk = jnp.concatenate([jnp.where(left, dk_dup[0], dk_dup[1]),
                                       jnp.where(left, dv_dup[0], dv_dup[1])], axis=1)
            g0 = pl.multiple_of(i * tq + b * BLOCK, BLOCK)
            dkv_acc[pl.ds(g0, 2 * BLOCK), :] += dkv_blk

        @pl.when(i == n_steps - 1)
        def _():
            dkv_ref[...] = dkv_acc[BLOCK:, :].astype(BF16)
            lane = lax.broadcasted_iota(jnp.int32, (1, LANES), 1)
            row = jnp.zeros((1, LANES), F32)
            for h in range(N_KV_HEADS):
                for g in range(GROUP):
                    tot = jnp.sum(dsk_acc[h, :, g * BLOCK:(g + 1) * BLOCK], axis=1, keepdims=True)
                    row = jnp.where(lane == GROUP * h + g, tot, row)
            small_ref[MIX_SINKS:MIX_SINKS + 1, :] = row

    blocks_per_tile = tq // POOL_HALO
    last_halo = s_len // POOL_HALO - 1
    halo_prev = pl.BlockSpec((POOL_HALO, 512), lambda i: (jnp.maximum(i * blocks_per_tile - 1, 0), 0))
    halo_next = pl.BlockSpec((POOL_HALO, 512), lambda i: (jnp.minimum((i + 1) * blocks_per_tile, last_halo), 0))
    halo_kv = pl.BlockSpec((BLOCK, 256), lambda i: (jnp.maximum(i * nb - 1, 0), 0))
    return pl.pallas_call(
        body, name="mixers_bwd", grid=(n_steps,),
        in_specs=[ANY, _rows(tq, 512), halo_prev, _rows(tq, 512), _rows(tq, 256), halo_kv,
                  _rows(tq, 512), halo_next, _rows(tq, 512),
                  _whole((4, 128, 128)), _whole((1, 512)), _whole((1, 512)),
                  _whole((N_KV_HEADS, BLOCK, GROUP * BLOCK)), _whole((N_KV_HEADS, 1, GROUP * BLOCK))],
        out_specs=[_rows(tq, 512), _rows(tq, 512), _whole((s_len, 256)),
                   _whole((MIX_ROWS, LANES)), _whole((1, 512))],
        out_shape=[jax.ShapeDtypeStruct((s_len, 512), BF16), jax.ShapeDtypeStruct((s_len, 512), BF16),
                   jax.ShapeDtypeStruct((s_len, 256), BF16), jax.ShapeDtypeStruct((MIX_ROWS, LANES), F32),
                   jax.ShapeDtypeStruct((1, 512), F32)],
        scratch_shapes=[pltpu.VMEM((N_KV_HEADS, tq + BLOCK, LANES), BF16)] * 2
        + [pltpu.VMEM((N_KV_HEADS, 1, GROUP * BLOCK), F32), pltpu.VMEM((s_len + BLOCK, 256), F32)],
        compiler_params=_params(),
    )(after, zp, zp, q, kv, kv, dpm, dpm, do, pool_w, pool_b, pool_scale, bias_t, sink_row)


def _in_bwd(after, dzp, dq, dkv, dzg, w_in_t, x, dh1, g_mix):
    s_len = x.shape[0]
    tm = min(512, s_len)

    def body(after_ref, dzp_ref, dq_ref, dkv_ref, dzg_ref, w_ref, x_ref, dh1_ref, g_ref, dx_ref, dg_ref):
        i = pl.program_id(0)

        @pl.when(i == 0)
        def _():
            dg_ref[...] = jnp.zeros_like(dg_ref)

        du = _dot(dzp_ref[...], w_ref[0:512, :])
        du = du + _dot(dq_ref[...], w_ref[512:1024, :])
        du = du + _dot(dkv_ref[...], w_ref[1024:1280, :])
        du = du + _dot(dzg_ref[...], w_ref[1280:3328, :])
        r, xh, _ = _rms_fwd(x_ref[...], g_ref[...])
        dxn, dg = _rms_bwd(du, xh, r, g_ref[...])
        dg_ref[...] += dg
        dx_ref[...] = dh1_ref[...] + dxn

    return pl.pallas_call(
        body, name="in_bwd", grid=(s_len // tm,),
        in_specs=[ANY, _rows(tm, 512), _rows(tm, 512), _rows(tm, 256), _rows(tm, 2048), _whole((IN_WIDTH, D_MODEL)),
                  _rows(tm, D_MODEL), _rows(tm, D_MODEL), _whole((1, D_MODEL))],
        out_specs=[_rows(tm, D_MODEL), _whole((1, D_MODEL))],
        out_shape=[jax.ShapeDtypeStruct((s_len, D_MODEL), F32), jax.ShapeDtypeStruct((1, D_MODEL), F32)],
        compiler_params=_params(),
    )(after, dzp, dq, dkv, dzg, w_in_t, x, dh1, g_mix)


def _all_gather_weights(name, shards, after=None):
    n = len(shards)
    extra = [] if after is None else [after]
    n_extra = len(extra)

    def body(*refs):
        ins, outs = refs[:n], refs[n + n_extra:2 * n + n_extra]
        send_sems, recv_sems, local_sems = refs[2 * n + n_extra:]
        x, y, c = lax.axis_index("x"), lax.axis_index("y"), lax.axis_index("c")
        me, sibling = (x, y, c), (x, y, 1 - c)
        chips = [(1 - x, y), (x, 1 - y), (1 - x, 1 - y)]

        def slot(a, px, py, pc):
            return outs[a].at[4 * px + 2 * py + pc]

        def copy(a, k, block, to, src=None):
            return pltpu.make_async_remote_copy(
                src_ref=slot(a, *block) if src is None else src, dst_ref=slot(a, *block),
                send_sem=send_sems.at[a, k], recv_sem=recv_sems.at[a, k], device_id=to, device_id_type=MESH)

        mine = [pltpu.make_async_copy(ins[a], slot(a, *me), local_sems.at[a]) for a in range(n)]
        for cp in mine:
            cp.start()
        first = []
        for a in range(n):
            first.append(copy(a, 0, me, sibling, src=ins[a]))
            first += [copy(a, 1 + j, me, (*chip, c), src=ins[a]) for j, chip in enumerate(chips)]
        for cp in first:
            cp.start()
        passed = []
        for a in range(n):
            for j, chip in enumerate(chips):
                copy(a, 1 + j, (*chip, c), me).wait_recv()
                cp = copy(a, 4 + j, (*chip, c), sibling)
                cp.start()
                passed.append(cp)
        for a in range(n):
            copy(a, 0, sibling, me).wait_recv()
            for j, chip in enumerate(chips):
                copy(a, 4 + j, (*chip, 1 - c), me).wait_recv()
        for cp in first + passed:
            cp.wait_send()
        for cp in mine:
            cp.wait()

    return pl.pallas_call(
        body, name=name,
        in_specs=[ANY] * (n + n_extra), out_specs=[ANY] * n,
        out_shape=[jax.ShapeDtypeStruct((N_DEV,) + s.shape, s.dtype) for s in shards],
        scratch_shapes=[pltpu.SemaphoreType.DMA((n, 7)), pltpu.SemaphoreType.DMA((n, 7)), pltpu.SemaphoreType.DMA((n,))],
    )(*shards, *extra)


HBM_SPEC = pl.BlockSpec(memory_space=pltpu.HBM)
SEM_SPEC = pl.BlockSpec(memory_space=pltpu.SEMAPHORE)
DATAFLOW = pltpu.SideEffectType.DATAFLOW_SIDE_EFFECTING
N_PEERS = N_DEV - 1


CHIP_PEERS = (1, 2, 4, 6)
RELAYED = (2, 4, 6)


def _peer_copies(srcs, lands, scatter, send_sems, recv_sems):
    x, y, c = lax.axis_index("x"), lax.axis_index("y"), lax.axis_index("c")
    me_idx = 4 * x + 2 * y + c
    copies = []
    for k in range(1, N_DEV):
        px = 1 - x if (k >> 2) & 1 else x
        py = 1 - y if (k >> 1) & 1 else y
        pc = 1 - c if k & 1 else c
        p_idx = 4 * px + 2 * py + pc
        for a in range(len(srcs)):
            if scatter[a] == "chip" and k not in CHIP_PEERS:
                continue
            src = srcs[a].at[p_idx] if scatter[a] is True else srcs[a]
            dst = lands[a].at[k] if scatter[a] is True else lands[a].at[me_idx]
            copies.append(pltpu.make_async_remote_copy(
                src_ref=src, dst_ref=dst, send_sem=send_sems.at[a * N_PEERS + k - 1],
                recv_sem=recv_sems.at[a * N_PEERS + k - 1],
                device_id=(px, py, pc), device_id_type=MESH))
    return copies


def _exchange_start(name, srcs, scatter, after):
    n = len(srcs)
    lands = [lax.empty(s.shape if sc is True else (N_DEV,) + s.shape, s.dtype) for s, sc in zip(srcs, scatter)]

    def body(*refs):
        src_refs, land_refs = refs[:n], refs[n:2 * n]
        send_sems, recv_sems = refs[2 * n + 1], refs[2 * n + 2]
        token = refs[4 * n + 3]
        for cp in _peer_copies(src_refs, land_refs, scatter, send_sems, recv_sems):
            cp.start()
        token[...] = jnp.zeros_like(token)

    hbm = lambda t: pltpu.HBM(t.shape, t.dtype)
    outs = pl.pallas_call(
        body, name=name,
        out_shape=[pltpu.SemaphoreType.DMA((n * N_PEERS,)), pltpu.SemaphoreType.DMA((n * N_PEERS,))]
        + [hbm(t) for t in srcs] + [hbm(t) for t in lands] + [jax.ShapeDtypeStruct((8, LANES), F32)],
        in_specs=[HBM_SPEC] * (2 * n) + [ANY],
        out_specs=[SEM_SPEC, SEM_SPEC] + [HBM_SPEC] * (2 * n) + [pl.BlockSpec(memory_space=pltpu.VMEM)],
        input_output_aliases={i: 2 + i for i in range(2 * n)},
        compiler_params=pltpu.CompilerParams(has_side_effects=DATAFLOW),
    )(*[pltpu.with_memory_space_constraint(t, pltpu.HBM) for t in list(srcs) + lands], after)
    return dict(n=n, scatter=scatter, send_sems=outs[0], recv_sems=outs[1], srcs=outs[2:2 + n],
                lands=outs[2 + n:2 + 2 * n], token=outs[2 + 2 * n])


def _exchange_wait(name, handle, after):
    n, scatter = handle["n"], handle["scatter"]

    def body(*refs):
        src_refs, land_refs = refs[:n], refs[n:2 * n]
        send_sems, recv_sems = refs[2 * n], refs[2 * n + 1]
        for cp in _peer_copies(src_refs, land_refs, scatter, send_sems, recv_sems):
            cp.wait_send()
            cp.wait_recv()

    both = list(handle["srcs"]) + list(handle["lands"])
    outs = pl.pallas_call(
        body, name=name,
        out_shape=[pltpu.HBM(t.shape, t.dtype) for t in both],
        in_specs=[HBM_SPEC] * (2 * n) + [SEM_SPEC, SEM_SPEC, ANY],
        out_specs=[HBM_SPEC] * (2 * n),
        input_output_aliases={i: i for i in range(2 * n)},
        compiler_params=pltpu.CompilerParams(has_side_effects=DATAFLOW),
    )(*both, handle["send_sems"], handle["recv_sems"], after)
    me_idx = _my_index()
    lands = [lax.dynamic_update_index_in_dim(land, src, me_idx, 0) if sc is False else land
             for land, src, sc in zip(outs[n:], outs[:n], scatter)]
    return lands, outs[:n]


def _my_index():
    return 4 * lax.axis_index("x") + 2 * lax.axis_index("y") + lax.axis_index("c")


N_RELAYED = len(RELAYED) + 1


def _relay_copies(bufs, send_sems, recv_sems):
    x, y, c = lax.axis_index("x"), lax.axis_index("y"), lax.axis_index("c")
    slots = []
    for k in RELAYED:
        px = 1 - x if (k >> 2) & 1 else x
        py = 1 - y if (k >> 1) & 1 else y
        slots.append(4 * px + 2 * py + c)
    slots.append(4 * x + 2 * y + (1 - c))
    copies = []
    for j, slot in enumerate(slots):
        for a, buf in enumerate(bufs):
            copies.append(pltpu.make_async_remote_copy(
                src_ref=buf.at[slot], dst_ref=buf.at[slot], send_sem=send_sems.at[a * N_RELAYED + j],
                recv_sem=recv_sems.at[a * N_RELAYED + j], device_id=(x, y, 1 - c), device_id_type=MESH))
    return copies


def _relay_start(name, bufs, after):
    n = len(bufs)

    def body(*refs):
        send_sems, recv_sems = refs[n + 1], refs[n + 2]
        for cp in _relay_copies(refs[:n], send_sems, recv_sems):
            cp.start()
        token = refs[2 * n + 3]
        token[...] = jnp.zeros_like(token)

    n_sems = n * N_RELAYED
    outs = pl.pallas_call(
        body, name=name,
        out_shape=[pltpu.SemaphoreType.DMA((n_sems,)), pltpu.SemaphoreType.DMA((n_sems,))]
        + [pltpu.HBM(t.shape, t.dtype) for t in bufs] + [jax.ShapeDtypeStruct((8, LANES), F32)],
        in_specs=[HBM_SPEC] * n + [ANY],
        out_specs=[SEM_SPEC, SEM_SPEC] + [HBM_SPEC] * n + [pl.BlockSpec(memory_space=pltpu.VMEM)],
        input_output_aliases={i: 2 + i for i in range(n)},
        compiler_params=pltpu.CompilerParams(has_side_effects=DATAFLOW),
    )(*[pltpu.with_memory_space_constraint(t, pltpu.HBM) for t in bufs], after)
    return dict(n=n, send_sems=outs[0], recv_sems=outs[1], bufs=outs[2:2 + n], token=outs[2 + n])


def _relay_wait(name, handle, after):
    n = handle["n"]

    def body(*refs):
        for cp in _relay_copies(refs[:n], refs[n], refs[n + 1]):
            cp.wait_send()
            cp.wait_recv()

    return pl.pallas_call(
        body, name=name,
        out_shape=[pltpu.HBM(t.shape, t.dtype) for t in handle["bufs"]],
        in_specs=[HBM_SPEC] * n + [SEM_SPEC, SEM_SPEC, ANY],
        out_specs=[HBM_SPEC] * n,
        input_output_aliases={i: i for i in range(n)},
        compiler_params=pltpu.CompilerParams(has_side_effects=DATAFLOW),
    )(*handle["bufs"], handle["send_sems"], handle["recv_sems"], after)


def _adamw(parts, w, m, v, sent=None):
    r, c = w.shape
    tr = 256 if r % 256 == 0 else r
    own = sent is not None

    def body(*refs):
        if own:
            _, p_ref, own_ref, w_ref, m_ref, v_ref, g_ref, d_ref, nm_ref, nv_ref = refs
            g = own_ref[...].astype(F32)
        else:
            p_ref, w_ref, m_ref, v_ref, g_ref, d_ref, nm_ref, nv_ref = refs
            g = p_ref[0].astype(F32)
        for k in range(1, N_DEV):
            g = g + p_ref[k].astype(F32)
        m_new = ADAM_B1 * m_ref[...] + (1.0 - ADAM_B1) * g
        v_new = ADAM_B2 * v_ref[...] + (1.0 - ADAM_B2) * (g * g)
        m_hat = m_new / (1.0 - ADAM_B1 ** ADAM_STEP)
        v_hat = v_new / (1.0 - ADAM_B2 ** ADAM_STEP)
        g_ref[...] = g
        d_ref[...] = -ADAM_LR * (m_hat / (jnp.sqrt(v_hat) + ADAM_EPS) + ADAM_WD * w_ref[...])
        nm_ref[...] = m_new
        nv_ref[...] = v_new

    out_shape = [jax.ShapeDtypeStruct((r, c), F32)] * 4
    if not own:
        return pl.pallas_call(
            body, name="adamw", grid=(r // tr,),
            in_specs=[pl.BlockSpec((N_DEV, tr, c), lambda i: (0, i, 0))] + [_rows(tr, c)] * 3,
            out_specs=[_rows(tr, c)] * 4, out_shape=out_shape, compiler_params=_params(),
        )(parts, w, m, v)
    rows = pl.BlockSpec((tr, c), lambda i, me: (i, 0))
    return pl.pallas_call(
        body, name="adamw_own", out_shape=out_shape, compiler_params=_params(),
        grid_spec=pltpu.PrefetchScalarGridSpec(
            num_scalar_prefetch=1, grid=(r // tr,),
            in_specs=[pl.BlockSpec((N_DEV, tr, c), lambda i, me: (0, i, 0)),
                      pl.BlockSpec((None, tr, c), lambda i, me: (me[0], i, 0))] + [rows] * 3,
            out_specs=[rows] * 4),
    )(_my_index().reshape(1).astype(jnp.int32), parts, sent, w, m, v)


def _adam_step(g, w, m, v):
    m_new = ADAM_B1 * m + (1.0 - ADAM_B1) * g
    v_new = ADAM_B2 * v + (1.0 - ADAM_B2) * (g * g)
    m_hat = m_new / (1.0 - ADAM_B1 ** ADAM_STEP)
    v_hat = v_new / (1.0 - ADAM_B2 ** ADAM_STEP)
    return -ADAM_LR * (m_hat / (jnp.sqrt(v_hat) + ADAM_EPS) + ADAM_WD * w), m_new, v_new


SMALL_NAMES = ("norm_mix", "pool_w", "pool_b", "pool_scale", "attn_sinks", "norm_mlp", "norm_final")


def _adamw_small(mlp_all, mix_all, scale_all, nmix_all, w, m, v):
    def body(mlp_ref, mix_ref, scale_ref, nmix_ref, *refs):
        ins, outs = refs[:21], refs[21:]

        def total(ref, rows, lanes=slice(None)):
            g = ref[0, rows, lanes]
            for k in range(1, N_DEV):
                g = g + ref[k, rows, lanes]
            return g

        grads = dict(
            norm_mix=total(nmix_ref, slice(0, 1)), pool_w=total(mix_ref, slice(0, MIX_POOL_B)),
            pool_b=total(mix_ref, slice(MIX_POOL_B, MIX_POOL_B + 4)), pool_scale=total(scale_ref, slice(0, 1)),
            attn_sinks=total(mix_ref, slice(MIX_SINKS, MIX_SINKS + 1)),
            norm_mlp=total(mlp_ref, slice(0, 1)), norm_final=total(mlp_ref, slice(1, 2)))
        for i, name in enumerate(SMALL_NAMES):
            g = grads[name]
            d, m_new, v_new = _adam_step(g, ins[3 * i][...], ins[3 * i + 1][...], ins[3 * i + 2][...])
            for ref, val in zip(outs[4 * i:4 * i + 4], (g, d, m_new, v_new)):
                ref[...] = val
        outs[28][...] = jnp.broadcast_to(total(mlp_ref, slice(2, 3), slice(0, LANES)), (8, LANES))

    operands, out_shape = [], []
    for name in SMALL_NAMES:
        operands += [w[name], m[name], v[name]]
        out_shape += [jax.ShapeDtypeStruct(w[name].shape, F32)] * 4
    out_shape.append(jax.ShapeDtypeStruct((8, LANES), F32))
    outs = pl.pallas_call(body, name="adamw_small", out_shape=out_shape)(
        mlp_all, mix_all, scale_all, nmix_all, *operands)
    return {name: outs[4 * i:4 * i + 4] for i, name in enumerate(SMALL_NAMES)}, outs[28]


def kernel(x, norm_mix, w_in, pool_w, pool_b, pool_scale, attn_sinks, p_pool, p_attn, w_out, norm_mlp, w_up, w_down, norm_final, loss_target, m_norm_mix, m_w_in, m_pool_w, m_pool_b, m_pool_scale, m_attn_sinks, m_p_pool, m_p_attn, m_w_out, m_norm_mlp, m_w_up, m_w_down, m_norm_final, v_norm_mix, v_w_in, v_pool_w, v_pool_b, v_pool_scale, v_attn_sinks, v_p_pool, v_p_attn, v_w_out, v_norm_mlp, v_w_up, v_w_down, v_norm_final):
    xs = x[0]
    tgt = loss_target[0]
    s_len = xs.shape[0]

    w_out_bf, w_up_bf, w_down_bf = [t[0].astype(BF16) for t in (w_out, w_up, w_down)]
    p_pool_bf, p_attn_bf = [t[0].T.astype(BF16) for t in (p_pool, p_attn)]
    w_in_bf = w_in[0].T.astype(BF16)
    (w_in_g,) = _all_gather_weights("all_gather_w_in", [w_in_bf])
    ag_rest = _exchange_start(
        "ag_rest_start", [p_pool_bf, p_attn_bf, w_out_bf, w_up_bf, w_down_bf], ("chip",) * 5, w_in_g)

    pool_w_bf = pool_w[0].astype(BF16)
    pool_b_row = pool_b[0].reshape(1, POOL_WIDTH)
    bias_t, sink_row = _attn_constants(attn_sinks[0])

    w_in_t = w_in_g.reshape(IN_WIDTH, D_MODEL)
    u, zp, q, kv, zg = _fwd_in(ag_rest["token"], xs, norm_mix, w_in_t)
    n_mixer_tiles = s_len // min(512, s_len)
    half = min(n_mixer_tiles - 1, n_mixer_tiles // 2 + 1)
    mixer_args = (zp, q, kv, pool_w_bf, pool_b_row, pool_scale, bias_t, sink_row)
    pm, o = _mixers_fwd(*mixer_args, n_tiles=half) if half else (None, None)
    first_level, _ = _exchange_wait("ag_rest_wait", ag_rest, zg if o is None else o)
    relay = _relay_start("ag_relay_start", first_level, zg)
    pm, o = _mixers_fwd(*mixer_args, first_tile=half, earlier=None if o is None else (relay["token"], pm, o))
    p_pool_g, p_attn_g, w_out_g, w_up_g, w_down_g = _relay_wait("ag_relay_wait", relay, o)
    p_pool_f = p_pool_g.reshape(D_MODEL, POOL_WIDTH)
    p_attn_f = p_attn_g.reshape(D_MODEL, ATTN_WIDTH)
    w_out_f = w_out_g.reshape(D_MODEL, D_MODEL)
    w_down_f = w_down_g.reshape(D_FF, D_MODEL)
    dh1, a, dapre, pack, small_mlp, dyp, dya, dzg, dpm, do = _core(
        xs, pm, o, zg, tgt, norm_mlp, norm_final.reshape(1, D_MODEL), p_pool_f, p_attn_f, w_out_f, w_up_g, w_down_f)
    gw_down = _tn_matmul(a, pack, square_a=True, b_cols=(2 * D_MODEL, D_MODEL))
    gw_up = _tn_matmul(pack, dapre, col_blocks=N_DEV, a_cols=(D_MODEL, D_MODEL))
    ex_mlp = _exchange_start(
        "ex_mlp_start", [gw_up, gw_down.reshape(N_DEV, D_FF // N_DEV, D_MODEL)], (True, True), small_mlp)
    dzp, dq, dkv, small_mix, g_pool_scale = _mixers_bwd(
        ex_mlp["token"], zp, q, kv, dpm, do, pool_w_bf, pool_b_row, pool_scale, bias_t, sink_row)
    gw_in = _tn_w_in(u, dzp, dq, dkv, dzg)
    ex_in = _exchange_start(
        "ex_in_start", [gw_in, small_mlp, small_mix, g_pool_scale], (True, False, False, False), dq)
    gw_out = _tn_matmul(pack, pack, after=ex_in["token"], a_cols=(0, D_MODEL), b_cols=(3 * D_MODEL, D_MODEL))
    gp_pool = _tn_matmul(pm, dyp, col_blocks=N_DEV, after=ex_in["token"])
    gp_attn = _tn_matmul(o, dya, col_blocks=N_DEV, after=ex_in["token"])
    ex_proj = _exchange_start(
        "ex_proj_start", [gp_pool, gp_attn, gw_out.reshape(N_DEV, D_MODEL // N_DEV, D_MODEL)], (True,) * 3,
        ex_in["token"])
    dx, g_norm_mix = _in_bwd(ex_proj["token"], dzp, dq, dkv, dzg, w_in_t, xs, dh1, norm_mix)

    big_w = dict(w_in=w_in, p_pool=p_pool, p_attn=p_attn, w_out=w_out, w_up=w_up, w_down=w_down)
    big_m = dict(w_in=m_w_in, p_pool=m_p_pool, p_attn=m_p_attn, w_out=m_w_out, w_up=m_w_up, w_down=m_w_down)
    big_v = dict(w_in=v_w_in, p_pool=v_p_pool, p_attn=v_p_attn, w_out=v_w_out, w_up=v_w_up, w_down=v_w_down)
    res = {}

    def update(names, recvs, sents):
        for name, parts, sent in zip(names, recvs, sents):
            flip = (lambda t: t.T) if name == "w_in" else (lambda t: t)
            outs = _adamw(parts, flip(big_w[name][0]), flip(big_m[name][0]), flip(big_v[name][0]), sent)
            res[name] = [flip(t)[None] for t in outs]

    update(["w_up", "w_down"], *_exchange_wait("ex_mlp_wait", ex_mlp, dx))
    (norm_mix_all,) = _all_gather_weights("all_gather_norm_mix", [g_norm_mix], res["w_down"][0])
    (r_in, mlp_all, mix_all, scale_all), (s_in, _, _, _) = _exchange_wait("ex_in_wait", ex_in, norm_mix_all)
    update(["w_in"], [r_in], [s_in])

    natural = dict(norm_mix=(1, D_MODEL), pool_w=(MIX_POOL_B, LANES), pool_b=(4, LANES), pool_scale=(1, POOL_WIDTH),
                   attn_sinks=(1, LANES), norm_mlp=(1, D_MODEL), norm_final=(1, D_MODEL))

    def as_2d(t, name):
        if name == "attn_sinks":
            return jnp.pad(t, ((0, 0), (0, LANES - N_HEADS)))
        return t.reshape(natural[name])

    small_w = dict(norm_mix=norm_mix, pool_w=pool_w, pool_b=pool_b, pool_scale=pool_scale, attn_sinks=attn_sinks,
                   norm_mlp=norm_mlp, norm_final=norm_final)
    small_m = dict(norm_mix=m_norm_mix, pool_w=m_pool_w, pool_b=m_pool_b, pool_scale=m_pool_scale,
                   attn_sinks=m_attn_sinks, norm_mlp=m_norm_mlp, norm_final=m_norm_final)
    small_v = dict(norm_mix=v_norm_mix, pool_w=v_pool_w, pool_b=v_pool_b, pool_scale=v_pool_scale,
                   attn_sinks=v_attn_sinks, norm_mlp=v_norm_mlp, norm_final=v_norm_final)
    small_res, loss_all = _adamw_small(
        mlp_all, mix_all, scale_all, norm_mix_all,
        *[{k: as_2d(t, k) for k, t in d.items()} for d in (small_w, small_m, small_v)])
    loss = loss_all[0, 0]
    for name in SMALL_NAMES:
        shape = small_w[name].shape
        res[name] = [(t[:, :N_HEADS] if name == "attn_sinks" else t).reshape(shape) for t in small_res[name]]
    update(["p_pool", "p_attn", "w_out"], *_exchange_wait("ex_proj_wait", ex_proj, loss_all))

    order = ["norm_mix", "w_in", "pool_w", "pool_b", "pool_scale", "attn_sinks", "p_pool", "p_attn", "w_out",
             "norm_mlp", "w_up", "w_down", "norm_final"]
    out = [loss, dx[None]]
    for kind in range(4):
        out += [res[name][kind] for name in order]
    return tuple(out)
```
